```python
import jax, jax.numpy as jnp
from jax import lax
import numpy as np

D_MODEL = 1024
BATCH = 8
SEQ = 8192
DEPTH = 2

N_MIXERS = 2
POOL_WINDOWS = (2, 4, 8, 16)
N_POOL_GROUPS = len(POOL_WINDOWS)
POOL_GROUP_DIM = D_MODEL // N_POOL_GROUPS
CONV_WIDTH = 31
FFN_DIM = ((8 * D_MODEL // 3 + 255) // 256) * 256
FFN_CONV_WIDTH = 3
N_MOD = 6
EPS = 1e-6
N_POOL_LAYERS = (DEPTH + 1) // 2
N_CONV_LAYERS = DEPTH // 2

kernel_name = "hybrid_pool_conformer_convffn_trunk"


def rms_norm(x, g):
    xf = x.astype(jnp.float32)
    y = xf * lax.rsqrt(jnp.mean(xf * xf, axis=-1, keepdims=True) + EPS)
    return (y * g.astype(jnp.float32)).astype(x.dtype)


def layer_norm(x, g, b):
    xf = x.astype(jnp.float32)
    mu = jnp.mean(xf, axis=-1, keepdims=True)
    var = jnp.mean(jnp.square(xf - mu), axis=-1, keepdims=True)
    y = (xf - mu) * lax.rsqrt(var + EPS)
    return (y * g.astype(jnp.float32) + b.astype(jnp.float32)).astype(x.dtype)


def causal_depthwise_conv(x, w):
    k = w.shape[0]
    return lax.conv_general_dilated(
        x, w[:, None, :].astype(x.dtype), window_strides=(1,),
        padding=((k - 1, 0),), dimension_numbers=("NWC", "WIO", "NWC"),
        feature_group_count=x.shape[-1])


def causal_mean_pool(u, window):
    s = u.shape[1]
    cs = jnp.cumsum(u.astype(jnp.float32), axis=1)
    lag = jnp.pad(cs, ((0, 0), (window, 0), (0, 0)))[:, :s]
    cnt = jnp.minimum(jnp.arange(1, s + 1), window).astype(jnp.float32)
    return ((cs - lag) / cnt[None, :, None]).astype(u.dtype)


def pool_mixer(h, w_groups, scale):
    b, s, d = h.shape
    hg = h.reshape(b, s, N_POOL_GROUPS, POOL_GROUP_DIM)
    pooled = jnp.stack(
        [causal_mean_pool(hg[:, :, g], POOL_WINDOWS[g]) - hg[:, :, g]
         for g in range(N_POOL_GROUPS)], axis=2)
    y = jnp.einsum("bsgc,gcd->bsgd", pooled, w_groups).reshape(b, s, d)
    return y * scale


def conformer_conv_module(h, w_pw1, b_pw1, w_dw, b_dw, ln_g, ln_b, w_pw2, b_pw2):
    a = h @ w_pw1 + b_pw1
    val, gt = jnp.split(a, 2, axis=-1)
    u = val * jax.nn.sigmoid(gt)
    u = causal_depthwise_conv(u, w_dw) + b_dw
    u = jax.nn.silu(layer_norm(u, ln_g, ln_b))
    return u @ w_pw2 + b_pw2


def conv_ffn(h, w_up, w_dw, w_down):
    a = causal_depthwise_conv(h @ w_up, w_dw)
    g, v = jnp.split(a, 2, axis=-1)
    return (jax.nn.silu(g) * v) @ w_down


def _fwd_setup_inputs(seed: int = 0) -> dict:
    key = jax.random.key(seed)
    ks = jax.random.split(key, 24)
    d, f = D_MODEL, FFN_DIM
    nrm = lambda k, shape, s: jax.random.normal(k, shape, jnp.float32) * s
    return {
        "x": nrm(ks[0], (BATCH, SEQ, d), 1.0),
        "c": nrm(ks[1], (BATCH, d), 1.0),
        "ada_w": nrm(ks[2], (DEPTH, d, N_MOD * d), 0.5 * d ** -0.5),
        "ada_b": nrm(ks[3], (DEPTH, N_MOD * d), 0.02),
        "pre_g": 1.0 + nrm(ks[4], (DEPTH, 2, d), 0.05),
        "post_g": 1.0 + nrm(ks[5], (DEPTH, 2, d), 0.05),
        "pool_w": nrm(ks[6], (N_POOL_LAYERS, N_POOL_GROUPS, POOL_GROUP_DIM, POOL_GROUP_DIM), POOL_GROUP_DIM ** -0.5),
        "pool_scale": 1.0 + nrm(ks[7], (N_POOL_LAYERS, d), 0.1),
        "cv_w_pw1": nrm(ks[8], (N_CONV_LAYERS, d, 2 * d), d ** -0.5),
        "cv_b_pw1": nrm(ks[9], (N_CONV_LAYERS, 2 * d), 0.02),
        "cv_w_dw": nrm(ks[10], (N_CONV_LAYERS, CONV_WIDTH, d), CONV_WIDTH ** -0.5),
        "cv_b_dw": nrm(ks[11], (N_CONV_LAYERS, d), 0.02),
        "cv_ln_g": 1.0 + nrm(ks[12], (N_CONV_LAYERS, d), 0.05),
        "cv_ln_b": nrm(ks[13], (N_CONV_LAYERS, d), 0.02),
        "cv_w_pw2": nrm(ks[14], (N_CONV_LAYERS, d, d), d ** -0.5),
        "cv_b_pw2": nrm(ks[15], (N_CONV_LAYERS, d), 0.02),
        "ffn_w_up": nrm(ks[16], (DEPTH, d, 2 * f), d ** -0.5),
        "ffn_w_dw": nrm(ks[17], (DEPTH, FFN_CONV_WIDTH, 2 * f), FFN_CONV_WIDTH ** -0.5),
        "ffn_w_down": nrm(ks[18], (DEPTH, f, d), f ** -0.5),
    }


def _fwd_reference(x, c, ada_w, ada_b, pre_g, post_g, pool_w, pool_scale,
              cv_w_pw1, cv_b_pw1, cv_w_dw, cv_b_dw, cv_ln_g, cv_ln_b, cv_w_pw2, cv_b_pw2,
              ffn_w_up, ffn_w_dw, ffn_w_down):
    c_act = jax.nn.silu(c)
    for i in range(DEPTH):
        mod = c_act @ ada_w[i] + ada_b[i]
        sh_m, sc_m, gt_m, sh_f, sc_f, gt_f = [m[:, None, :] for m in jnp.split(mod, N_MOD, axis=-1)]

        h = rms_norm(x, pre_g[i, 0]) * (1.0 + sc_m) + sh_m
        j = i // N_MIXERS
        if i % N_MIXERS == 0:
            y = pool_mixer(h, pool_w[j], pool_scale[j])
        else:
            y = conformer_conv_module(h, cv_w_pw1[j], cv_b_pw1[j], cv_w_dw[j], cv_b_dw[j],
                                      cv_ln_g[j], cv_ln_b[j], cv_w_pw2[j], cv_b_pw2[j])
        x = x + gt_m * rms_norm(y, post_g[i, 0])

        h = rms_norm(x, pre_g[i, 1]) * (1.0 + sc_f) + sh_f
        y = conv_ffn(h, ffn_w_up[i], ffn_w_dw[i], ffn_w_down[i])
        x = x + gt_f * rms_norm(y, post_g[i, 1])
    return x


import jax as _jax
import jax.numpy as _jnp

TWIN_FORMAT = 'train_step'
FWD_PARAMS = ['x', 'c', 'ada_w', 'ada_b', 'pre_g', 'post_g', 'pool_w', 'pool_scale', 'cv_w_pw1', 'cv_b_pw1', 'cv_w_dw', 'cv_b_dw', 'cv_ln_g', 'cv_ln_b', 'cv_w_pw2', 'cv_b_pw2', 'ffn_w_up', 'ffn_w_dw', 'ffn_w_down']
TWIN_WEIGHTS = ['ada_w', 'ada_b', 'pre_g', 'post_g', 'pool_w', 'pool_scale', 'cv_w_pw1', 'cv_b_pw1', 'cv_w_dw', 'cv_b_dw', 'cv_ln_g', 'cv_ln_b', 'cv_w_pw2', 'cv_b_pw2', 'ffn_w_up', 'ffn_w_dw', 'ffn_w_down']
TWIN_DIFF_INPUT = 'x'
TWIN_INPUTS = ['x', 'c', 'ada_w', 'ada_b', 'pre_g', 'post_g', 'pool_w', 'pool_scale', 'cv_w_pw1', 'cv_b_pw1', 'cv_w_dw', 'cv_b_dw', 'cv_ln_g', 'cv_ln_b', 'cv_w_pw2', 'cv_b_pw2', 'ffn_w_up', 'ffn_w_dw', 'ffn_w_down', 'loss_target', 'm_ada_w', 'm_ada_b', 'm_pre_g', 'm_post_g', 'm_pool_w', 'm_pool_scale', 'm_cv_w_pw1', 'm_cv_b_pw1', 'm_cv_w_dw', 'm_cv_b_dw', 'm_cv_ln_g', 'm_cv_ln_b', 'm_cv_w_pw2', 'm_cv_b_pw2', 'm_ffn_w_up', 'm_ffn_w_dw', 'm_ffn_w_down', 'v_ada_w', 'v_ada_b', 'v_pre_g', 'v_post_g', 'v_pool_w', 'v_pool_scale', 'v_cv_w_pw1', 'v_cv_b_pw1', 'v_cv_w_dw', 'v_cv_b_dw', 'v_cv_ln_g', 'v_cv_ln_b', 'v_cv_w_pw2', 'v_cv_b_pw2', 'v_ffn_w_up', 'v_ffn_w_dw', 'v_ffn_w_down']
TWIN_OUTPUTS = ['loss', 'grad_x', 'grad_ada_w', 'grad_ada_b', 'grad_pre_g', 'grad_post_g', 'grad_pool_w', 'grad_pool_scale', 'grad_cv_w_pw1', 'grad_cv_b_pw1', 'grad_cv_w_dw', 'grad_cv_b_dw', 'grad_cv_ln_g', 'grad_cv_ln_b', 'grad_cv_w_pw2', 'grad_cv_b_pw2', 'grad_ffn_w_up', 'grad_ffn_w_dw', 'grad_ffn_w_down', 'delta_ada_w', 'delta_ada_b', 'delta_pre_g', 'delta_post_g', 'delta_pool_w', 'delta_pool_scale', 'delta_cv_w_pw1', 'delta_cv_b_pw1', 'delta_cv_w_dw', 'delta_cv_b_dw', 'delta_cv_ln_g', 'delta_cv_ln_b', 'delta_cv_w_pw2', 'delta_cv_b_pw2', 'delta_ffn_w_up', 'delta_ffn_w_dw', 'delta_ffn_w_down', 'new_m_ada_w', 'new_m_ada_b', 'new_m_pre_g', 'new_m_post_g', 'new_m_pool_w', 'new_m_pool_scale', 'new_m_cv_w_pw1', 'new_m_cv_b_pw1', 'new_m_cv_w_dw', 'new_m_cv_b_dw', 'new_m_cv_ln_g', 'new_m_cv_ln_b', 'new_m_cv_w_pw2', 'new_m_cv_b_pw2', 'new_m_ffn_w_up', 'new_m_ffn_w_dw', 'new_m_ffn_w_down', 'new_v_ada_w', 'new_v_ada_b', 'new_v_pre_g', 'new_v_post_g', 'new_v_pool_w', 'new_v_pool_scale', 'new_v_cv_w_pw1', 'new_v_cv_b_pw1', 'new_v_cv_w_dw', 'new_v_cv_b_dw', 'new_v_cv_ln_g', 'new_v_cv_ln_b', 'new_v_cv_w_pw2', 'new_v_cv_b_pw2', 'new_v_ffn_w_up', 'new_v_ffn_w_dw', 'new_v_ffn_w_down']
TWIN_LEAF_KINDS = {'loss': 'loss', 'grad_x': 'grad_x', 'grad_ada_w': 'grad_w', 'grad_ada_b': 'grad_w', 'grad_pre_g': 'grad_w', 'grad_post_g': 'grad_w', 'grad_pool_w': 'grad_w', 'grad_pool_scale': 'grad_w', 'grad_cv_w_pw1': 'grad_w', 'grad_cv_b_pw1': 'grad_w', 'grad_cv_w_dw': 'grad_w', 'grad_cv_b_dw': 'grad_w', 'grad_cv_ln_g': 'grad_w', 'grad_cv_ln_b': 'grad_w', 'grad_cv_w_pw2': 'grad_w', 'grad_cv_b_pw2': 'grad_w', 'grad_ffn_w_up': 'grad_w', 'grad_ffn_w_dw': 'grad_w', 'grad_ffn_w_down': 'grad_w', 'delta_ada_w': 'delta_w', 'delta_ada_b': 'delta_w', 'delta_pre_g': 'delta_w', 'delta_post_g': 'delta_w', 'delta_pool_w': 'delta_w', 'delta_pool_scale': 'delta_w', 'delta_cv_w_pw1': 'delta_w', 'delta_cv_b_pw1': 'delta_w', 'delta_cv_w_dw': 'delta_w', 'delta_cv_b_dw': 'delta_w', 'delta_cv_ln_g': 'delta_w', 'delta_cv_ln_b': 'delta_w', 'delta_cv_w_pw2': 'delta_w', 'delta_cv_b_pw2': 'delta_w', 'delta_ffn_w_up': 'delta_w', 'delta_ffn_w_dw': 'delta_w', 'delta_ffn_w_down': 'delta_w', 'new_m_ada_w': 'new_m', 'new_m_ada_b': 'new_m', 'new_m_pre_g': 'new_m', 'new_m_post_g': 'new_m', 'new_m_pool_w': 'new_m', 'new_m_pool_scale': 'new_m', 'new_m_cv_w_pw1': 'new_m', 'new_m_cv_b_pw1': 'new_m', 'new_m_cv_w_dw': 'new_m', 'new_m_cv_b_dw': 'new_m', 'new_m_cv_ln_g': 'new_m', 'new_m_cv_ln_b': 'new_m', 'new_m_cv_w_pw2': 'new_m', 'new_m_cv_b_pw2': 'new_m', 'new_m_ffn_w_up': 'new_m', 'new_m_ffn_w_dw': 'new_m', 'new_m_ffn_w_down': 'new_m', 'new_v_ada_w': 'new_v', 'new_v_ada_b': 'new_v', 'new_v_pre_g': 'new_v', 'new_v_post_g': 'new_v', 'new_v_pool_w': 'new_v', 'new_v_pool_scale': 'new_v', 'new_v_cv_w_pw1': 'new_v', 'new_v_cv_b_pw1': 'new_v', 'new_v_cv_w_dw': 'new_v', 'new_v_cv_b_dw': 'new_v', 'new_v_cv_ln_g': 'new_v', 'new_v_cv_ln_b': 'new_v', 'new_v_cv_w_pw2': 'new_v', 'new_v_cv_b_pw2': 'new_v', 'new_v_ffn_w_up': 'new_v', 'new_v_ffn_w_dw': 'new_v', 'new_v_ffn_w_down': 'new_v'}


def _forward(args):
    return _fwd_reference(*[args[k] for k in FWD_PARAMS])


def _output_shape():
    out = _jax.eval_shape(lambda: _forward(_fwd_setup_inputs(0)))
    return out.shape, out.dtype

N_MICROBATCH = 1
ADAM_LR = 0.001
ADAM_B1 = 0.9
ADAM_B2 = 0.999
ADAM_EPS = 1e-08
ADAM_WD = 0.01
ADAM_STEP = 10
PER_EXAMPLE_BATCH_AXIS = {'x': 0, 'c': 0, 'loss_target': 0}
SHARED_INPUTS = []
_WEIGHT_DTYPES = {'ada_w': _jnp.float32, 'ada_b': _jnp.float32, 'pre_g': _jnp.float32, 'post_g': _jnp.float32, 'pool_w': _jnp.float32, 'pool_scale': _jnp.float32, 'cv_w_pw1': _jnp.float32, 'cv_b_pw1': _jnp.float32, 'cv_w_dw': _jnp.float32, 'cv_b_dw': _jnp.float32, 'cv_ln_g': _jnp.float32, 'cv_ln_b': _jnp.float32, 'cv_w_pw2': _jnp.float32, 'cv_b_pw2': _jnp.float32, 'ffn_w_up': _jnp.float32, 'ffn_w_dw': _jnp.float32, 'ffn_w_down': _jnp.float32}
MOMENT_SCALE = {'ada_w': 3.467663e+00, 'ada_b': 6.658983e+00, 'pre_g': 2.605679e-01, 'post_g': 7.591533e+00, 'pool_w': 4.479049e-01, 'pool_scale': 4.894070e+00, 'cv_w_pw1': 1.936587e-01, 'cv_b_pw1': 7.838011e-01, 'cv_w_dw': 2.769421e-01, 'cv_b_dw': 1.850683e+00, 'cv_ln_g': 8.076851e-01, 'cv_ln_b': 1.214720e+00, 'cv_w_pw2': 5.186133e-01, 'cv_b_pw2': 2.675889e+00, 'ffn_w_up': 9.366315e-02, 'ffn_w_dw': 9.812444e-02, 'ffn_w_down': 1.766184e-01}


def _to_microbatches(a, axis):
    t = _jnp.moveaxis(a, axis, 0)
    t = t.reshape((N_MICROBATCH, t.shape[0] // N_MICROBATCH) + t.shape[1:])
    return _jnp.moveaxis(t, 1, axis + 1)


def setup_inputs(seed: int = 0) -> dict:
    inp = _fwd_setup_inputs(seed)
    key = _jax.random.fold_in(_jax.random.key(seed), 7919)
    shape, _ = _output_shape()
    out = dict(inp)
    out["loss_target"] = _jax.random.normal(_jax.random.fold_in(key, 0), shape, _jnp.float32)
    for i, name in enumerate(TWIN_WEIGHTS):
        w = inp[name].astype(_jnp.float32)
        if MOMENT_SCALE is None:
            s = _jnp.sqrt(_jnp.mean(_jnp.square(w)) + 1e-30)
        else:
            s = MOMENT_SCALE[name]
        km, kv = _jax.random.split(_jax.random.fold_in(key, i + 1))
        out[name] = w
        out["m_" + name] = s * _jax.random.normal(km, w.shape, _jnp.float32)
        out["v_" + name] = (s * s) * _jax.random.uniform(kv, w.shape, _jnp.float32, 0.5, 1.5)
    if N_MICROBATCH > 1:
        for name, axis in PER_EXAMPLE_BATCH_AXIS.items():
            out[name] = _to_microbatches(out[name], axis)
    return {'x': out['x'], 'c': out['c'], 'ada_w': out['ada_w'], 'ada_b': out['ada_b'], 'pre_g': out['pre_g'], 'post_g': out['post_g'], 'pool_w': out['pool_w'], 'pool_scale': out['pool_scale'], 'cv_w_pw1': out['cv_w_pw1'], 'cv_b_pw1': out['cv_b_pw1'], 'cv_w_dw': out['cv_w_dw'], 'cv_b_dw': out['cv_b_dw'], 'cv_ln_g': out['cv_ln_g'], 'cv_ln_b': out['cv_ln_b'], 'cv_w_pw2': out['cv_w_pw2'], 'cv_b_pw2': out['cv_b_pw2'], 'ffn_w_up': out['ffn_w_up'], 'ffn_w_dw': out['ffn_w_dw'], 'ffn_w_down': out['ffn_w_down'], 'loss_target': out['loss_target'], 'm_ada_w': out['m_ada_w'], 'm_ada_b': out['m_ada_b'], 'm_pre_g': out['m_pre_g'], 'm_post_g': out['m_post_g'], 'm_pool_w': out['m_pool_w'], 'm_pool_scale': out['m_pool_scale'], 'm_cv_w_pw1': out['m_cv_w_pw1'], 'm_cv_b_pw1': out['m_cv_b_pw1'], 'm_cv_w_dw': out['m_cv_w_dw'], 'm_cv_b_dw': out['m_cv_b_dw'], 'm_cv_ln_g': out['m_cv_ln_g'], 'm_cv_ln_b': out['m_cv_ln_b'], 'm_cv_w_pw2': out['m_cv_w_pw2'], 'm_cv_b_pw2': out['m_cv_b_pw2'], 'm_ffn_w_up': out['m_ffn_w_up'], 'm_ffn_w_dw': out['m_ffn_w_dw'], 'm_ffn_w_down': out['m_ffn_w_down'], 'v_ada_w': out['v_ada_w'], 'v_ada_b': out['v_ada_b'], 'v_pre_g': out['v_pre_g'], 'v_post_g': out['v_post_g'], 'v_pool_w': out['v_pool_w'], 'v_pool_scale': out['v_pool_scale'], 'v_cv_w_pw1': out['v_cv_w_pw1'], 'v_cv_b_pw1': out['v_cv_b_pw1'], 'v_cv_w_dw': out['v_cv_w_dw'], 'v_cv_b_dw': out['v_cv_b_dw'], 'v_cv_ln_g': out['v_cv_ln_g'], 'v_cv_ln_b': out['v_cv_ln_b'], 'v_cv_w_pw2': out['v_cv_w_pw2'], 'v_cv_b_pw2': out['v_cv_b_pw2'], 'v_ffn_w_up': out['v_ffn_w_up'], 'v_ffn_w_dw': out['v_ffn_w_dw'], 'v_ffn_w_down': out['v_ffn_w_down']}


def _loss(weights, diff, rest, loss_target):
    with _jax.named_scope("forward"):
        args = {**rest, TWIN_DIFF_INPUT: diff, **{k: w.astype(_WEIGHT_DTYPES[k]) for k, w in weights.items()}}
        y = _forward(args)
    with _jax.named_scope("loss_head"):
        err = _jnp.square(y.astype(_jnp.float32) - loss_target)
        return 0.5 * _jnp.sum(_jnp.mean(err, axis=-1)) if err.ndim else 0.5 * err


def _adamw(w, g, m, v):
    m = ADAM_B1 * m + (1.0 - ADAM_B1) * g
    v = ADAM_B2 * v + (1.0 - ADAM_B2) * _jnp.square(g)
    m_hat = m / (1.0 - ADAM_B1 ** ADAM_STEP)
    v_hat = v / (1.0 - ADAM_B2 ** ADAM_STEP)
    delta = -ADAM_LR * (m_hat / (_jnp.sqrt(v_hat) + ADAM_EPS) + ADAM_WD * w)
    return delta, m, v


def reference(x, c, ada_w, ada_b, pre_g, post_g, pool_w, pool_scale, cv_w_pw1, cv_b_pw1, cv_w_dw, cv_b_dw, cv_ln_g, cv_ln_b, cv_w_pw2, cv_b_pw2, ffn_w_up, ffn_w_dw, ffn_w_down, loss_target, m_ada_w, m_ada_b, m_pre_g, m_post_g, m_pool_w, m_pool_scale, m_cv_w_pw1, m_cv_b_pw1, m_cv_w_dw, m_cv_b_dw, m_cv_ln_g, m_cv_ln_b, m_cv_w_pw2, m_cv_b_pw2, m_ffn_w_up, m_ffn_w_dw, m_ffn_w_down, v_ada_w, v_ada_b, v_pre_g, v_post_g, v_pool_w, v_pool_scale, v_cv_w_pw1, v_cv_b_pw1, v_cv_w_dw, v_cv_b_dw, v_cv_ln_g, v_cv_ln_b, v_cv_w_pw2, v_cv_b_pw2, v_ffn_w_up, v_ffn_w_dw, v_ffn_w_down):
    given = dict(x=x, c=c, ada_w=ada_w, ada_b=ada_b, pre_g=pre_g, post_g=post_g, pool_w=pool_w, pool_scale=pool_scale, cv_w_pw1=cv_w_pw1, cv_b_pw1=cv_b_pw1, cv_w_dw=cv_w_dw, cv_b_dw=cv_b_dw, cv_ln_g=cv_ln_g, cv_ln_b=cv_ln_b, cv_w_pw2=cv_w_pw2, cv_b_pw2=cv_b_pw2, ffn_w_up=ffn_w_up, ffn_w_dw=ffn_w_dw, ffn_w_down=ffn_w_down, loss_target=loss_target, m_ada_w=m_ada_w, m_ada_b=m_ada_b, m_pre_g=m_pre_g, m_post_g=m_post_g, m_pool_w=m_pool_w, m_pool_scale=m_pool_scale, m_cv_w_pw1=m_cv_w_pw1, m_cv_b_pw1=m_cv_b_pw1, m_cv_w_dw=m_cv_w_dw, m_cv_b_dw=m_cv_b_dw, m_cv_ln_g=m_cv_ln_g, m_cv_ln_b=m_cv_ln_b, m_cv_w_pw2=m_cv_w_pw2, m_cv_b_pw2=m_cv_b_pw2, m_ffn_w_up=m_ffn_w_up, m_ffn_w_dw=m_ffn_w_dw, m_ffn_w_down=m_ffn_w_down, v_ada_w=v_ada_w, v_ada_b=v_ada_b, v_pre_g=v_pre_g, v_post_g=v_post_g, v_pool_w=v_pool_w, v_pool_scale=v_pool_scale, v_cv_w_pw1=v_cv_w_pw1, v_cv_b_pw1=v_cv_b_pw1, v_cv_w_dw=v_cv_w_dw, v_cv_b_dw=v_cv_b_dw, v_cv_ln_g=v_cv_ln_g, v_cv_ln_b=v_cv_ln_b, v_cv_w_pw2=v_cv_w_pw2, v_cv_b_pw2=v_cv_b_pw2, v_ffn_w_up=v_ffn_w_up, v_ffn_w_dw=v_ffn_w_dw, v_ffn_w_down=v_ffn_w_down)
    weights = {n: given[n] for n in TWIN_WEIGHTS}
    shared = {n: given[n] for n in SHARED_INPUTS}
    per_example = {n: given[n] for n in ['x', 'c']}
    grad_fn = _jax.value_and_grad(_loss, argnums=(0, 1))

    def one_microbatch(ex, loss_target):
        ex = dict(ex)
        diff = ex.pop(TWIN_DIFF_INPUT)
        return grad_fn(weights, diff, {**shared, **ex}, loss_target)

    if N_MICROBATCH == 1:
        loss, (grad_w, grad_x) = one_microbatch(per_example, given["loss_target"])
    else:
        def body(carry, xs):
            loss_sum, grad_sum = carry
            l_k, (gw_k, gx_k) = one_microbatch(xs[0], xs[1])
            with _jax.named_scope("update"):
                return (loss_sum + l_k, _jax.tree.map(_jnp.add, grad_sum, gw_k)), gx_k

        init = (_jnp.zeros((), _jnp.float32), _jax.tree.map(_jnp.zeros_like, weights))
        (loss, grad_w), grad_x = _jax.lax.scan(body, init, (per_example, given["loss_target"]))
    with _jax.named_scope("update"):
        delta_w, new_m, new_v = {}, {}, {}
        for n in TWIN_WEIGHTS:
            delta_w[n], new_m[n], new_v[n] = _adamw(weights[n], grad_w[n], given["m_" + n], given["v_" + n])
    return (loss, grad_x, *[grad_w[n] for n in TWIN_WEIGHTS], *[delta_w[n] for n in TWIN_WEIGHTS],
            *[new_m[n] for n in TWIN_WEIGHTS], *[new_v[n] for n in TWIN_WEIGHTS])
```

```python
import functools

import jax
import jax.numpy as jnp
from jax import lax
from jax.experimental import pallas as pl
from jax.experimental.pallas import tpu as pltpu

F32 = jnp.float32
BF16 = jnp.bfloat16
EPS = 1e-6
N_CHIPS = 4
N_DEV = 8
POOL_WINDOWS = (2, 4, 8, 16)
POOL_HALO = 16
FFN_HALO = 16
ADAM_LR = 0.001
ADAM_B1 = 0.9
ADAM_B2 = 0.999
ADAM_EPS = 1e-08
ADAM_WD = 0.01
ADAM_STEP = 10
V7X_VMEM_LIMIT = 58 * 1024 * 1024
MESH = pl.DeviceIdType.MESH


def _cparams(sem=None, vmem=V7X_VMEM_LIMIT):
    return pltpu.CompilerParams(dimension_semantics=sem, vmem_limit_bytes=vmem)


def _row_tile(n, want):
    if n <= want:
        return n
    t = want - want % 8
    while n % t:
        t -= 8
    return t


def _lane_chunks(width):
    out, c = [], 0
    while c < width:
        w = min(512, width - c)
        out.append((c, w))
        c += w
    return out


def _dot(a, b):
    return jnp.dot(a, b, preferred_element_type=F32)


def _dot_nt(a, b):
    return lax.dot_general(a, b, (((1,), (1,)), ((), ())), preferred_element_type=F32)


def _dot_tn(a, b):
    return lax.dot_general(a, b, (((0,), (0,)), ((), ())), preferred_element_type=F32)


def _rms(x):
    r = lax.rsqrt(jnp.mean(x * x, axis=-1, keepdims=True) + EPS)
    return x * r, r


def _rms_bwd(dyn, yn, r):
    return r * (dyn - yn * jnp.mean(dyn * yn, axis=-1, keepdims=True))


def _sigmoid(x):
    return 1.0 / (1.0 + jnp.exp(-x))


def _colsum(x):
    return jnp.sum(x, axis=0, keepdims=True)


def _shift_down(x, k):
    return x if k == 0 else pltpu.roll(x, k, 0)


def _shift_up(x, k):
    return x if k == 0 else pltpu.roll(x, x.shape[0] - k, 0)


def _vec_rows(vec):
    return vec[0:1] * vec[1:2], vec[2:3], vec[3:4], vec[4:5]


def _add_rows(sum_ref, rows):
    for k, r in enumerate(rows):
        sum_ref[k:k + 1, :] += r


def _ada_forward(c, ada_w):
    n_layers, d, ncol = ada_w.shape

    def body(c_ref, w_ref, call_ref, mod_ref, part_ref, sendbuf, send_sems, recv_sems, send2, recv2):
        x, y, cc = lax.axis_index("x"), lax.axis_index("y"), lax.axis_index("c")
        me = 4 * x + 2 * y + cc
        rel = [(x, y, 1 - cc), (1 - x, y, cc), (x, 1 - y, cc), (1 - x, 1 - y, cc),
               (1 - x, y, 1 - cc), (x, 1 - y, 1 - cc), (1 - x, 1 - y, 1 - cc)]
        cv = c_ref[...]
        call_ref[me] = jnp.broadcast_to(cv * _sigmoid(cv), (8, d))

        def gather(k, block, to):
            blk = call_ref.at[block]
            return pltpu.make_async_remote_copy(src_ref=blk, dst_ref=blk, send_sem=send_sems.at[k],
                                                recv_sem=recv_sems.at[k], device_id=to, device_id_type=MESH)

        for k, to in enumerate(rel):
            gather(k, me, to).start()
        for k, (px, py, pc) in enumerate(rel):
            gather(k, 4 * px + 2 * py + pc, rel[k]).wait_recv()
        for k, to in enumerate(rel):
            gather(k, me, to).wait_send()

        ca = call_ref[...].reshape(8 * N_DEV, d)
        for l in range(n_layers):
            part_ref[l] = jnp.dot(ca, w_ref[l], preferred_element_type=F32, precision=lax.Precision.HIGHEST)

        j = 2 * x + y
        chips = [(1 - x, y), (x, 1 - y), (1 - x, 1 - y)]

        def rows_of(b):
            return part_ref[:, pl.ds(pl.multiple_of(8 * b, 8), 8), :]

        def scatter(k, src_j, to):
            return pltpu.make_async_remote_copy(
                src_ref=sendbuf.at[k], dst_ref=mod_ref.at[src_j], send_sem=send2.at[k], recv_sem=recv2.at[k],
                device_id=to, device_id_type=MESH)

        mod_ref[j] = rows_of(me)
        for k, (px, py) in enumerate(chips):
            sendbuf[k] = rows_of(4 * px + 2 * py + cc)
            scatter(k, j, (px, py, cc)).start()
        for k, (px, py) in enumerate(chips):
            scatter(k, 2 * px + py, (px, py, cc)).wait_recv()
        for k, (px, py) in enumerate(chips):
            scatter(k, j, (px, py, cc)).wait_send()

    vm = pl.BlockSpec(memory_space=pltpu.VMEM)
    return pl.pallas_call(
        body, name="ada_forward",
        out_shape=(jax.ShapeDtypeStruct((N_DEV, 8, d), F32), jax.ShapeDtypeStruct((N_CHIPS, n_layers, 8, ncol), F32)),
        in_specs=[vm, vm], out_specs=(vm, vm),
        scratch_shapes=[pltpu.VMEM((n_layers, 8 * N_DEV, ncol), F32), pltpu.VMEM((3, n_layers, 8, ncol), F32),
                        pltpu.SemaphoreType.DMA((7,)), pltpu.SemaphoreType.DMA((7,)),
                        pltpu.SemaphoreType.DMA((3,)), pltpu.SemaphoreType.DMA((3,))],
        compiler_params=_cparams(),
    )(c, ada_w)


def _cast_bf16(w2d):
    r, c = w2d.shape
    tr = _row_tile(r, 256)

    def body(w_ref, o_ref):
        o_ref[...] = w_ref[...].astype(BF16)

    return pl.pallas_call(
        body, name="cast_bf16", grid=(r // tr,),
        in_specs=[pl.BlockSpec((tr, c), lambda i: (i, 0))], out_specs=pl.BlockSpec((tr, c), lambda i: (i, 0)),
        out_shape=jax.ShapeDtypeStruct((r, c), BF16), compiler_params=_cparams(("parallel",)),
    )(w2d)


def _all_gather_shards(shards):
    n = len(shards)

    def body(*refs):
        ins, outs = refs[:n], refs[n:2 * n]
        ici_send, ici_recv, d2d_send, d2d_recv, local_sems = refs[2 * n:]
        x, y, c = lax.axis_index("x"), lax.axis_index("y"), lax.axis_index("c")
        j = 2 * x + y
        chips = [(1 - x, y), (x, 1 - y), (1 - x, 1 - y)]
        half = [s.shape[0] // 2 for s in shards]

        def rows(k, h):
            return pl.ds(h * half[k], half[k])

        local = [pltpu.make_async_copy(ins[k], outs[k].at[j], local_sems.at[k]) for k in range(n)]
        for cp in local:
            cp.start()

        def ici(k, r, src_j, to):
            return pltpu.make_async_remote_copy(
                src_ref=ins[k].at[rows(k, c), :], dst_ref=outs[k].at[src_j, rows(k, c), :],
                send_sem=ici_send.at[k * 3 + r], recv_sem=ici_recv.at[k * 3 + r], device_id=to, device_id_type=MESH)

        def d2d(k, r, src_j, hc):
            return pltpu.make_async_remote_copy(
                src_ref=outs[k].at[src_j, rows(k, hc), :], dst_ref=outs[k].at[src_j, rows(k, hc), :],
                send_sem=d2d_send.at[k * 3 + r], recv_sem=d2d_recv.at[k * 3 + r],
                device_id=(x, y, 1 - c), device_id_type=MESH)

        for k in range(n):
            for r, (px, py) in enumerate(chips):
                ici(k, r, j, (px, py, c)).start()
        for k in range(n):
            for r, (px, py) in enumerate(chips):
                ici(k, r, 2 * px + py, (px, py, c)).wait_recv()
                d2d(k, r, 2 * px + py, c).start()
        for k in range(n):
            for r, (px, py) in enumerate(chips):
                d2d(k, r, 2 * px + py, 1 - c).wait_recv()
        for k in range(n):
            for r, (px, py) in enumerate(chips):
                ici(k, r, j, (px, py, c)).wait_send()
                d2d(k, r, 2 * px + py, c).wait_send()
        for cp in local:
            cp.wait()

    anyspec = pl.BlockSpec(memory_space=pl.ANY)
    return pl.pallas_call(
        body, name="all_gather_shards",
        out_shape=tuple(jax.ShapeDtypeStruct((N_CHIPS,) + s.shape, s.dtype) for s in shards),
        in_specs=[anyspec] * n, out_specs=tuple([anyspec] * n),
        scratch_shapes=[pltpu.SemaphoreType.DMA((3 * n,)), pltpu.SemaphoreType.DMA((3 * n,)),
                        pltpu.SemaphoreType.DMA((3 * n,)), pltpu.SemaphoreType.DMA((3 * n,)),
                        pltpu.SemaphoreType.DMA((n,))],
        compiler_params=pltpu.CompilerParams(has_side_effects=True),
    )(*shards)


def _pool_core(he, w_ref, scale, first_row, halo, n_rows):
    d = he.shape[1]
    gd = d // len(POOL_WINDOWS)
    t = first_row + lax.broadcasted_iota(jnp.int32, (n_rows, 1), 0)
    pooled, ypre, cnts = [], [], []
    for g, w in enumerate(POOL_WINDOWS):
        hg = he[:, g * gd:(g + 1) * gd]
        s, k = hg, 1
        while k < w:
            s = s + _shift_down(s, k)
            k *= 2
        cnt = jnp.minimum(t + 1, w).astype(F32)
        p = s[halo:] / cnt - hg[halo:]
        pooled.append(p.astype(BF16))
        cnts.append(cnt)
        ypre.append(_dot(pooled[-1], w_ref[g]))
    return pooled, jnp.concatenate(ypre, axis=1), cnts


def _pool_forward(x, vec, pool_w):
    s, d = x.shape
    ts = _row_tile(s, 512)
    n_g, gd, _ = pool_w.shape

    def body(x_ref, vec_ref, w_ref, o_ref, carry):
        i = pl.program_id(0)

        @pl.when(i == 0)
        def _():
            carry[...] = jnp.zeros_like(carry)

        vec = vec_ref[...]
        a, sh, gt, gpost = _vec_rows(vec)
        xb = x_ref[...]
        xn, _ = _rms(xb)
        h = xn * a + sh
        he = jnp.concatenate([carry[...], h], axis=0)
        carry[...] = h[ts - POOL_HALO:]
        _, ypre, _ = _pool_core(he, w_ref, vec[5:6], i * ts, POOL_HALO, ts)
        yn, _ = _rms(ypre * vec[5:6])
        o_ref[...] = xb + gt * (yn * gpost)

    return pl.pallas_call(
        body, name="pool_forward", grid=(s // ts,),
        in_specs=[pl.BlockSpec((ts, d), lambda i: (i, 0)), pl.BlockSpec((8, d), lambda i: (0, 0)),
                  pl.BlockSpec((n_g, gd, gd), lambda i: (0, 0, 0))],
        out_specs=pl.BlockSpec((ts, d), lambda i: (i, 0)),
        out_shape=jax.ShapeDtypeStruct((s, d), F32),
        scratch_shapes=[pltpu.VMEM((POOL_HALO, d), F32)],
        compiler_params=_cparams(("arbitrary",)),
    )(x, vec, pool_w)


def _pool_backward(dout, x, vec, pool_w):
    s, d = x.shape
    ts = _row_tile(s, 512)
    nb = s // ts
    hb = ts // POOL_HALO
    n_g, gd, _ = pool_w.shape

    def body(do_ref, x_ref, xh_ref, vec_ref, w_ref, dx_ref, sum_ref, dw_ref, carry):
        step = pl.program_id(0)
        i = nb - 1 - step

        @pl.when(step == 0)
        def _():
            carry[...] = jnp.zeros_like(carry)
            sum_ref[...] = jnp.zeros_like(sum_ref)
            dw_ref[...] = jnp.zeros_like(dw_ref)

        vec = vec_ref[...]
        a, sh, gt, gpost = _vec_rows(vec)
        scale = vec[5:6]
        do = do_ref[...]
        xe = jnp.concatenate([xh_ref[...], x_ref[...]], axis=0)
        xne, re = _rms(xe)
        he = xne * a + sh
        rowid = lax.broadcasted_iota(jnp.int32, (POOL_HALO + ts, 1), 0)
        he = jnp.where((rowid >= POOL_HALO) | (i > 0), he, 0.0)
        xn, r = xne[POOL_HALO:], re[POOL_HALO:]
        pooled, ypre, cnts = _pool_core(he, w_ref, scale, i * ts, POOL_HALO, ts)
        yn, ry = _rms(ypre * scale)
        dyn = do * (gt * gpost)
        dy = _rms_bwd(dyn, yn, ry)
        dypre = (dy * scale).astype(BF16)
        dh_parts, q_parts = [], []
        for g, w in enumerate(POOL_WINDOWS):
            dyg = dypre[:, g * gd:(g + 1) * gd]
            dpool = _dot_nt(dyg, w_ref[g])
            dw_ref[g] += _dot_tn(pooled[g], dyg)
            q = dpool / cnts[g]
            qe = jnp.concatenate([q, carry[:, g * gd:(g + 1) * gd]], axis=0)
            acc, k = qe, 1
            while k < w:
                acc = acc + _shift_up(acc, k)
                k *= 2
            dh_parts.append(acc[:ts] - dpool)
            q_parts.append(q[:POOL_HALO])
        carry[...] = jnp.concatenate(q_parts, axis=1)
        dh = jnp.concatenate(dh_parts, axis=1)
        dxn = dh * a
        dx_ref[...] = do + _rms_bwd(dxn, xn, r)
        _add_rows(sum_ref, [_colsum(do * gt * yn), _colsum(do * yn * gpost), _colsum(dh * xn * vec[1:2]),
                            _colsum(dh * xn * vec[0:1]), _colsum(dh), _colsum(dy * ypre)])

    blk = lambda st: (nb - 1 - st, 0)
    return pl.pallas_call(
        body, name="pool_backward", grid=(nb,),
        in_specs=[pl.BlockSpec((ts, d), blk), pl.BlockSpec((ts, d), blk),
                  pl.BlockSpec((POOL_HALO, d), lambda st: (jnp.maximum((nb - 1 - st) * hb - 1, 0), 0)),
                  pl.BlockSpec((8, d), lambda st: (0, 0)), pl.BlockSpec((n_g, gd, gd), lambda st: (0, 0, 0))],
        out_specs=(pl.BlockSpec((ts, d), blk), pl.BlockSpec((8, d), lambda st: (0, 0)),
                   pl.BlockSpec((n_g, gd, gd), lambda st: (0, 0, 0))),
        out_shape=(jax.ShapeDtypeStruct((s, d), F32), jax.ShapeDtypeStruct((8, d), F32),
                   jax.ShapeDtypeStruct((n_g, gd, gd), F32)),
        scratch_shapes=[pltpu.VMEM((POOL_HALO, d), F32)],
        compiler_params=_cparams(("arbitrary",)),
    )(dout, x, x, vec, pool_w)


def _ffn_forward(x, vec, w_up, w_dw, w_down):
    s, d = x.shape
    _, _, cs = w_up.shape
    ts = _row_tile(s, 256)
    chunks = _lane_chunks(cs)

    def body(x_ref, vec_ref, wup_ref, wdw_ref, wdn_ref, o_ref, h_ref, a0_ref, y_ref, carry, yacc):
        i = pl.program_id(0)

        @pl.when(i == 0)
        def _():
            carry[...] = jnp.zeros_like(carry)

        vec = vec_ref[...]
        a, sh, gt, gpost = _vec_rows(vec)
        xb = x_ref[...]
        xn, _ = _rms(xb)
        hb = (xn * a + sh).astype(BF16)
        h_ref[...] = hb
        yacc[...] = jnp.zeros_like(yacc)
        for q in range(2):
            for c0, cw in chunks:
                conv = []
                for j in (q, q + 2):
                    a0 = _dot(hb, wup_ref[j, :, c0:c0 + cw])
                    a0_ref[j, :, c0:c0 + cw] = a0.astype(BF16)
                    ae = jnp.concatenate([carry[j, :, c0:c0 + cw], a0], axis=0)
                    carry[j, :, c0:c0 + cw] = a0[ts - FFN_HALO:]
                    w = wdw_ref[:, j * cs + c0:j * cs + c0 + cw]
                    conv.append((w[2:3] * ae + w[1:2] * _shift_down(ae, 1) + w[0:1] * _shift_down(ae, 2))[FFN_HALO:])
                u = (conv[0] * _sigmoid(conv[0]) * conv[1]).astype(BF16)
                yacc[...] += _dot(u, wdn_ref[q, c0:c0 + cw, :])
        y = yacc[...]
        y_ref[...] = y
        yn, _ = _rms(y)
        o_ref[...] = xb + gt * (yn * gpost)

    const3 = lambda i: (0, 0, 0)
    return pl.pallas_call(
        body, name="ffn_forward", grid=(s // ts,),
        in_specs=[pl.BlockSpec((ts, d), lambda i: (i, 0)), pl.BlockSpec((8, d), lambda i: (0, 0)),
                  pl.BlockSpec(w_up.shape, const3, pipeline_mode=pl.Buffered(1)),
                  pl.BlockSpec(w_dw.shape, lambda i: (0, 0)),
                  pl.BlockSpec(w_down.shape, const3, pipeline_mode=pl.Buffered(1))],
        out_specs=(pl.BlockSpec((ts, d), lambda i: (i, 0)), pl.BlockSpec((ts, d), lambda i: (i, 0)),
                   pl.BlockSpec((4, ts, cs), lambda i: (0, i, 0)), pl.BlockSpec((ts, d), lambda i: (i, 0))),
        out_shape=(jax.ShapeDtypeStruct((s, d), F32), jax.ShapeDtypeStruct((s, d), BF16),
                   jax.ShapeDtypeStruct((4, s, cs), BF16), jax.ShapeDtypeStruct((s, d), F32)),
        scratch_shapes=[pltpu.VMEM((4, FFN_HALO, cs), F32), pltpu.VMEM((ts, d), F32)],
        compiler_params=_cparams(("arbitrary",)),
    )(x, vec, w_up, w_dw, w_down)


def _ffn_backward(dout, x, y, a0, vec, w_up, w_dw, w_down):
    s, d = x.shape
    _, _, cs = w_up.shape
    ts = _row_tile(s, 256)
    nb = s // ts
    hb = ts // FFN_HALO
    chunks = _lane_chunks(cs)

    def body(do_ref, x_ref, y_ref, a0_ref, a0h_ref, vec_ref, wup_ref, wdw_ref, wdn_ref,
             dx_ref, da0_ref, u_ref, dy_ref, sum_ref, dwdw_ref, carry, dhacc):
        step = pl.program_id(0)
        i = nb - 1 - step

        @pl.when(step == 0)
        def _():
            carry[...] = jnp.zeros_like(carry)
            sum_ref[...] = jnp.zeros_like(sum_ref)
            dwdw_ref[...] = jnp.zeros_like(dwdw_ref)

        vec = vec_ref[...]
        a, sh, gt, gpost = _vec_rows(vec)
        do = do_ref[...]
        yn, ry = _rms(y_ref[...])
        dy = _rms_bwd(do * (gt * gpost), yn, ry)
        dyb = dy.astype(BF16)
        dy_ref[...] = dyb
        not_first = (i > 0).astype(F32)
        dhacc[...] = jnp.zeros_like(dhacc)
        for q in range(2):
            for c0, cw in chunks:
                ae, r1, r2, conv, wts = {}, {}, {}, {}, {}
                for j in (q, q + 2):
                    halo = a0h_ref[j, :, c0:c0 + cw].astype(F32) * not_first
                    ae[j] = jnp.concatenate([halo, a0_ref[j, :, c0:c0 + cw].astype(F32)], axis=0)
                    r1[j], r2[j] = _shift_down(ae[j], 1), _shift_down(ae[j], 2)
                    wts[j] = wdw_ref[:, j * cs + c0:j * cs + c0 + cw]
                    conv[j] = (wts[j][2:3] * ae[j] + wts[j][1:2] * r1[j] + wts[j][0:1] * r2[j])[FFN_HALO:]
                cg, cv = conv[q], conv[q + 2]
                sg = _sigmoid(cg)
                sl = cg * sg
                u_ref[q, :, c0:c0 + cw] = (sl * cv).astype(BF16)
                du = _dot_nt(dyb, wdn_ref[q, c0:c0 + cw, :])
                dconv = {q: du * cv * (sg * (1.0 + cg * (1.0 - sg))), q + 2: du * sl}
                for j in (q, q + 2):
                    dae = jnp.concatenate([dconv[j], carry[j, :, c0:c0 + cw]], axis=0)
                    carry[j, :, c0:c0 + cw] = dconv[j][:FFN_HALO]
                    w = wts[j]
                    da0 = (w[2:3] * dae + w[1:2] * _shift_up(dae, 1) + w[0:1] * _shift_up(dae, 2))[:ts]
                    da0b = da0.astype(BF16)
                    da0_ref[j, :, c0:c0 + cw] = da0b
                    dhacc[...] += _dot_nt(da0b, wup_ref[j, :, c0:c0 + cw])
                    lanes = slice(j * cs + c0, j * cs + c0 + cw)
                    dwdw_ref[0:1, lanes] += _colsum(dconv[j] * r2[j][FFN_HALO:])
                    dwdw_ref[1:2, lanes] += _colsum(dconv[j] * r1[j][FFN_HALO:])
                    dwdw_ref[2:3, lanes] += _colsum(dconv[j] * ae[j][FFN_HALO:])
        dh = dhacc[...]
        xn, r = _rms(x_ref[...])
        dx_ref[...] = do + _rms_bwd(dh * a, xn, r)
        _add_rows(sum_ref, [_colsum(do * gt * yn), _colsum(do * yn * gpost), _colsum(dh * xn * vec[1:2]),
                            _colsum(dh * xn * vec[0:1]), _colsum(dh)])

    blk = lambda st: (nb - 1 - st, 0)
    blk3 = lambda st: (0, nb - 1 - st, 0)
    const3 = lambda st: (0, 0, 0)
    return pl.pallas_call(
        body, name="ffn_backward", grid=(nb,),
        in_specs=[pl.BlockSpec((ts, d), blk), pl.BlockSpec((ts, d), blk), pl.BlockSpec((ts, d), blk),
                  pl.BlockSpec((4, ts, cs), blk3),
                  pl.BlockSpec((4, FFN_HALO, cs), lambda st: (0, jnp.maximum((nb - 1 - st) * hb - 1, 0), 0)),
                  pl.BlockSpec((8, d), lambda st: (0, 0)),
                  pl.BlockSpec(w_up.shape, const3, pipeline_mode=pl.Buffered(1)),
                  pl.BlockSpec(w_dw.shape, lambda st: (0, 0)),
                  pl.BlockSpec(w_down.shape, const3, pipeline_mode=pl.Buffered(1))],
        out_specs=(pl.BlockSpec((ts, d), blk), pl.BlockSpec((4, ts, cs), blk3), pl.BlockSpec((2, ts, cs), blk3),
                   pl.BlockSpec((ts, d), blk), pl.BlockSpec((8, d), lambda st: (0, 0)),
                   pl.BlockSpec((8, 4 * cs), lambda st: (0, 0))),
        out_shape=(jax.ShapeDtypeStruct((s, d), F32), jax.ShapeDtypeStruct((4, s, cs), BF16),
                   jax.ShapeDtypeStruct((2, s, cs), BF16), jax.ShapeDtypeStruct((s, d), BF16),
                   jax.ShapeDtypeStruct((8, d), F32), jax.ShapeDtypeStruct((8, 4 * cs), F32)),
        scratch_shapes=[pltpu.VMEM((4, FFN_HALO, cs), F32), pltpu.VMEM((ts, d), F32)],
        compiler_params=_cparams(("arbitrary",)),
    )(dout, x, y, a0, a0, vec, w_up, w_dw, w_down)


def _conv_halo(width):
    return -(-(width - 1) // 8) * 8


def _conv_forward(x, vec, cvec, w_pw1, b_pw1, w_dw, w_pw2):
    s, d = x.shape
    kw = w_dw.shape[0]
    halo = _conv_halo(kw)
    ts = _row_tile(s, 256)
    hd = d // 2

    def body(x_ref, vec_ref, cvec_ref, w1_ref, b1_ref, wdw_ref, w2_ref,
             o_ref, h_ref, a_ref, uc_ref, z_ref, y_ref, carry):
        i = pl.program_id(0)

        @pl.when(i == 0)
        def _():
            carry[...] = jnp.zeros_like(carry)

        vec, cvec = vec_ref[...], cvec_ref[...]
        a, sh, gt, gpost = _vec_rows(vec)
        xb = x_ref[...]
        xn, _ = _rms(xb)
        hb = (xn * a + sh).astype(BF16)
        h_ref[...] = hb
        for j in range(4):
            a_ref[:, j * hd:(j + 1) * hd] = _dot(hb, w1_ref[j]) + b1_ref[:, j * hd:(j + 1) * hd]
        u = a_ref[:, :d] * _sigmoid(a_ref[:, d:])
        ue = jnp.concatenate([carry[...], u], axis=0)
        carry[...] = u[ts - halo:]
        uc = jnp.zeros((ts, d), F32) + cvec[0:1]
        for k in range(kw):
            uc = uc + wdw_ref[k:k + 1, :] * _shift_down(ue, kw - 1 - k)[halo:]
        uc_ref[...] = uc
        mu = jnp.mean(uc, axis=-1, keepdims=True)
        cen = uc - mu
        rstd = lax.rsqrt(jnp.mean(cen * cen, axis=-1, keepdims=True) + EPS)
        l = cen * rstd * cvec[1:2] + cvec[2:3]
        zb = (l * _sigmoid(l)).astype(BF16)
        z_ref[...] = zb
        y = _dot(zb, w2_ref[...]) + cvec[3:4]
        y_ref[...] = y
        yn, _ = _rms(y)
        o_ref[...] = xb + gt * (yn * gpost)

    row = lambda i: (i, 0)
    const2 = lambda i: (0, 0)
    return pl.pallas_call(
        body, name="conv_forward", grid=(s // ts,),
        in_specs=[pl.BlockSpec((ts, d), row), pl.BlockSpec((8, d), const2), pl.BlockSpec((8, d), const2),
                  pl.BlockSpec(w_pw1.shape, lambda i: (0, 0, 0)), pl.BlockSpec(b_pw1.shape, const2),
                  pl.BlockSpec(w_dw.shape, const2), pl.BlockSpec(w_pw2.shape, const2)],
        out_specs=(pl.BlockSpec((ts, d), row), pl.BlockSpec((ts, d), row), pl.BlockSpec((ts, 2 * d), row),
                   pl.BlockSpec((ts, d), row), pl.BlockSpec((ts, d), row), pl.BlockSpec((ts, d), row)),
        out_shape=(jax.ShapeDtypeStruct((s, d), F32), jax.ShapeDtypeStruct((s, d), BF16),
                   jax.ShapeDtypeStruct((s, 2 * d), F32), jax.ShapeDtypeStruct((s, d), F32),
                   jax.ShapeDtypeStruct((s, d), BF16), jax.ShapeDtypeStruct((s, d), F32)),
        scratch_shapes=[pltpu.VMEM((halo, d), F32)],
        compiler_params=_cparams(("arbitrary",)),
    )(x, vec, cvec, w_pw1, b_pw1, w_dw, w_pw2)


def _conv_backward(dout, x, y, a_pre, uc, vec, cvec, w_pw1, w_dw, w_pw2):
    s, d = x.shape
    kw = w_dw.shape[0]
    kpad = -(-kw // 8) * 8
    halo = _conv_halo(kw)
    ts = _row_tile(s, 256)
    nb = s // ts
    hb = ts // halo
    hd = d // 2

    def body(do_ref, x_ref, y_ref, a_ref, ah_ref, uc_ref, vec_ref, cvec_ref, w1_ref, wdw_ref, w2_ref,
             dx_ref, da_ref, dy_ref, sum_ref, dwdw_ref, carry):
        step = pl.program_id(0)
        i = nb - 1 - step

        @pl.when(step == 0)
        def _():
            carry[...] = jnp.zeros_like(carry)
            sum_ref[...] = jnp.zeros_like(sum_ref)
            dwdw_ref[...] = jnp.zeros_like(dwdw_ref)

        vec, cvec = vec_ref[...], cvec_ref[...]
        a, sh, gt, gpost = _vec_rows(vec)
        do = do_ref[...]
        yn, ry = _rms(y_ref[...])
        dy = _rms_bwd(do * (gt * gpost), yn, ry)
        dyb = dy.astype(BF16)
        dy_ref[...] = dyb
        dz = _dot_nt(dyb, w2_ref[...])
        uc = uc_ref[...]
        mu = jnp.mean(uc, axis=-1, keepdims=True)
        cen = uc - mu
        rstd = lax.rsqrt(jnp.mean(cen * cen, axis=-1, keepdims=True) + EPS)
        lhat = cen * rstd
        l = lhat * cvec[1:2] + cvec[2:3]
        sgl = _sigmoid(l)
        dl = dz * (sgl * (1.0 + l * (1.0 - sgl)))
        dlhat = dl * cvec[1:2]
        duc = rstd * (dlhat - jnp.mean(dlhat, axis=-1, keepdims=True)
                      - lhat * jnp.mean(dlhat * lhat, axis=-1, keepdims=True))
        ae = jnp.concatenate([ah_ref[...] * (i > 0).astype(F32), a_ref[...]], axis=0)
        sgate = _sigmoid(ae[:, d:])
        val = ae[:, :d]
        ue = val * sgate
        rowid = lax.broadcasted_iota(jnp.int32, (halo + ts, 1), 0)
        ue = jnp.where((rowid >= halo) | (i > 0), ue, 0.0)
        duce = jnp.concatenate([duc, carry[...]], axis=0)
        carry[...] = duc[:halo]
        du = jnp.zeros((ts, d), F32)
        for k in range(kw):
            du = du + wdw_ref[k:k + 1, :] * _shift_up(duce, kw - 1 - k)[:ts]
            dwdw_ref[k:k + 1, :] += _colsum(duc * _shift_down(ue, kw - 1 - k)[halo:])
        sg, vl = sgate[halo:], val[halo:]
        dval = du * sg
        dgate = du * vl * (sg * (1.0 - sg))
        dvb, dgb = dval.astype(BF16), dgate.astype(BF16)
        dh = jnp.zeros((ts, d), F32)
        for j in range(2):
            da_ref[j] = dvb[:, j * hd:(j + 1) * hd]
            da_ref[j + 2] = dgb[:, j * hd:(j + 1) * hd]
            dh = dh + _dot_nt(dvb[:, j * hd:(j + 1) * hd], w1_ref[j]) + _dot_nt(dgb[:, j * hd:(j + 1) * hd], w1_ref[j + 2])
        xn, r = _rms(x_ref[...])
        dx_ref[...] = do + _rms_bwd(dh * a, xn, r)
        _add_rows(sum_ref, [_colsum(do * gt * yn), _colsum(do * yn * gpost), _colsum(dh * xn * vec[1:2]),
                            _colsum(dh * xn * vec[0:1]), _colsum(dh), _colsum(dy), _colsum(dl * lhat), _colsum(dl),
                            _colsum(duc), _colsum(dval), _colsum(dgate)])

    blk = lambda st: (nb - 1 - st, 0)
    const2 = lambda st: (0, 0)
    return pl.pallas_call(
        body, name="conv_backward", grid=(nb,),
        in_specs=[pl.BlockSpec((ts, d), blk), pl.BlockSpec((ts, d), blk), pl.BlockSpec((ts, d), blk),
                  pl.BlockSpec((ts, 2 * d), blk),
                  pl.BlockSpec((halo, 2 * d), lambda st: (jnp.maximum((nb - 1 - st) * hb - 1, 0), 0)),
                  pl.BlockSpec((ts, d), blk), pl.BlockSpec((8, d), const2), pl.BlockSpec((8, d), const2),
                  pl.BlockSpec(w_pw1.shape, lambda st: (0, 0, 0)), pl.BlockSpec(w_dw.shape, const2),
                  pl.BlockSpec(w_pw2.shape, const2)],
        out_specs=(pl.BlockSpec((ts, d), blk), pl.BlockSpec((4, ts, hd), lambda st: (0, nb - 1 - st, 0)),
                   pl.BlockSpec((ts, d), blk), pl.BlockSpec((16, d), const2), pl.BlockSpec((kpad, d), const2)),
        out_shape=(jax.ShapeDtypeStruct((s, d), F32), jax.ShapeDtypeStruct((4, s, hd), BF16),
                   jax.ShapeDtypeStruct((s, d), BF16), jax.ShapeDtypeStruct((16, d), F32),
                   jax.ShapeDtypeStruct((kpad, d), F32)),
        scratch_shapes=[pltpu.VMEM((halo, d), F32)],
        compiler_params=_cparams(("arbitrary",)),
    )(dout, x, y, a_pre, a_pre, uc, vec, cvec, w_pw1, w_dw, w_pw2)


def _loss_head(y, target):
    s, d = y.shape
    ts = _row_tile(s, 512)

    def body(y_ref, t_ref, dy_ref, l_ref):
        @pl.when(pl.program_id(0) == 0)
        def _():
            l_ref[...] = jnp.zeros_like(l_ref)

        e = y_ref[...] - t_ref[...]
        dy_ref[...] = e * (1.0 / d)
        l_ref[0:1, :] += _colsum(e * e) * (0.5 / d)

    row = lambda i: (i, 0)
    return pl.pallas_call(
        body, name="loss_head", grid=(s // ts,),
        in_specs=[pl.BlockSpec((ts, d), row), pl.BlockSpec((ts, d), row)],
        out_specs=(pl.BlockSpec((ts, d), row), pl.BlockSpec((8, d), lambda i: (0, 0))),
        out_shape=(jax.ShapeDtypeStruct((s, d), F32), jax.ShapeDtypeStruct((8, d), F32)),
        compiler_params=_cparams(("arbitrary",)),
    )(y, target)


def _weight_grad(a, b):
    na, s, k = a.shape
    nb_, _, n = b.shape
    nj = max(na, nb_)
    ts = _row_tile(s, 512)

    def body(a_ref, b_ref, o_ref):
        @pl.when(pl.program_id(1) == 0)
        def _():
            o_ref[...] = jnp.zeros_like(o_ref)

        o_ref[0] += _dot_tn(a_ref[0], b_ref[0])

    return pl.pallas_call(
        body, name="weight_grad", grid=(nj, s // ts),
        in_specs=[pl.BlockSpec((1, ts, k), (lambda j, t: (j, t, 0)) if na > 1 else (lambda j, t: (0, t, 0))),
                  pl.BlockSpec((1, ts, n), (lambda j, t: (j, t, 0)) if nb_ > 1 else (lambda j, t: (0, t, 0)))],
        out_specs=pl.BlockSpec((1, k, n), lambda j, t: (j, 0, 0)),
        out_shape=jax.ShapeDtypeStruct((nj, k, n), F32),
        compiler_params=_cparams(("parallel", "arbitrary")),
    )(a, b)


def _adamw(w, g, m, v):
    nl, r, c = w.shape
    tr = _row_tile(r, 256)
    c1 = 1.0 / (1.0 - ADAM_B1 ** ADAM_STEP)
    c2 = 1.0 / (1.0 - ADAM_B2 ** ADAM_STEP)

    def body(w_ref, g_ref, m_ref, v_ref, d_ref, nm_ref, nv_ref):
        g_ = g_ref[...]
        nm = ADAM_B1 * m_ref[...] + (1.0 - ADAM_B1) * g_
        nv = ADAM_B2 * v_ref[...] + (1.0 - ADAM_B2) * (g_ * g_)
        nm_ref[...] = nm
        nv_ref[...] = nv
        d_ref[...] = -ADAM_LR * ((nm * c1) / (jnp.sqrt(nv * c2) + ADAM_EPS) + ADAM_WD * w_ref[...])

    spec = pl.BlockSpec((1, tr, c), lambda l, i: (l, i, 0))
    shp = jax.ShapeDtypeStruct((nl, r, c), F32)
    return pl.pallas_call(
        body, name="adamw", grid=(nl, r // tr), in_specs=[spec] * 4, out_specs=(spec,) * 3, out_shape=(shp,) * 3,
        compiler_params=_cparams(("parallel", "parallel")),
    )(w, g, m, v)


def _swap_halves(grads):
    n = len(grads)

    def body(*refs):
        ins, outs = refs[:n], refs[n:2 * n]
        send_sems, recv_sems = refs[2 * n:]
        x, y, c = lax.axis_index("x"), lax.axis_index("y"), lax.axis_index("c")
        copies = [pltpu.make_async_remote_copy(
            src_ref=ins[k].at[:, 1 - c], dst_ref=outs[k], send_sem=send_sems.at[k], recv_sem=recv_sems.at[k],
            device_id=(x, y, 1 - c), device_id_type=MESH) for k in range(n)]
        for cp in copies:
            cp.start()
        for cp in copies:
            cp.wait()

    anyspec = pl.BlockSpec(memory_space=pl.ANY)
    return pl.pallas_call(
        body, name="swap_halves",
        out_shape=tuple(jax.ShapeDtypeStruct((g.shape[0],) + g.shape[2:], g.dtype) for g in grads),
        in_specs=[anyspec] * n, out_specs=tuple([anyspec] * n),
        scratch_shapes=[pltpu.SemaphoreType.DMA((n,)), pltpu.SemaphoreType.DMA((n,))],
        compiler_params=pltpu.CompilerParams(has_side_effects=True),
    )(*grads)


def _add_my_half(g, other, core):
    _, _, h, c = g.shape
    th = _row_tile(h, 256)

    def body(core_ref, g_ref, o_ref, out_ref):
        out_ref[...] = g_ref[:, 0] + o_ref[...]

    return pl.pallas_call(
        body, name="add_my_half",
        grid_spec=pltpu.PrefetchScalarGridSpec(
            num_scalar_prefetch=1, grid=(4, h // th),
            in_specs=[pl.BlockSpec((1, 1, th, c), lambda j, i, core_ref: (j, core_ref[0], i, 0)),
                      pl.BlockSpec((1, th, c), lambda j, i, core_ref: (j, i, 0))],
            out_specs=pl.BlockSpec((1, th, c), lambda j, i, core_ref: (j, i, 0))),
        out_shape=jax.ShapeDtypeStruct(other.shape, F32),
        compiler_params=_cparams(("parallel", "parallel")),
    )(core, g, other)


def _exchange_chips(parts):
    n = len(parts)

    def body(*refs):
        ins, outs = refs[:n], refs[n:2 * n]
        send_sems, recv_sems = refs[2 * n:]
        x, y, c = lax.axis_index("x"), lax.axis_index("y"), lax.axis_index("c")
        chips = [(1 - x, y), (x, 1 - y), (1 - x, 1 - y)]
        copies = []
        for k in range(n):
            for r, (px, py) in enumerate(chips):
                copies.append(pltpu.make_async_remote_copy(
                    src_ref=ins[k].at[2 * px + py], dst_ref=outs[k].at[r],
                    send_sem=send_sems.at[3 * k + r], recv_sem=recv_sems.at[3 * k + r],
                    device_id=(px, py, c), device_id_type=MESH))
        for cp in copies:
            cp.start()
        for cp in copies:
            cp.wait()

    anyspec = pl.BlockSpec(memory_space=pl.ANY)
    return pl.pallas_call(
        body, name="exchange_chips",
        out_shape=tuple(jax.ShapeDtypeStruct((3,) + p.shape[1:], p.dtype) for p in parts),
        in_specs=[anyspec] * n, out_specs=tuple([anyspec] * n),
        scratch_shapes=[pltpu.SemaphoreType.DMA((3 * n,)), pltpu.SemaphoreType.DMA((3 * n,))],
        compiler_params=pltpu.CompilerParams(has_side_effects=True),
    )(*parts)


def _sum_for_my_chip(part, got, chip):
    _, h, c = part.shape
    th = _row_tile(h, 256)

    def body(chip_ref, p_ref, q_ref, out_ref):
        out_ref[...] = ((p_ref[0] + q_ref[0]) + q_ref[1]) + q_ref[2]

    return pl.pallas_call(
        body, name="sum_for_my_chip",
        grid_spec=pltpu.PrefetchScalarGridSpec(
            num_scalar_prefetch=1, grid=(h // th,),
            in_specs=[pl.BlockSpec((1, th, c), lambda i, chip_ref: (chip_ref[0], i, 0)),
                      pl.BlockSpec((3, th, c), lambda i, chip_ref: (0, i, 0))],
            out_specs=pl.BlockSpec((th, c), lambda i, chip_ref: (i, 0))),
        out_shape=jax.ShapeDtypeStruct((h, c), F32),
        compiler_params=_cparams(("parallel",)),
    )(chip, part, got)


def _join_halves(halves):
    n = len(halves)

    def body(*refs):
        ins, outs = refs[:n], refs[n:2 * n]
        send_sems, recv_sems, local_sems = refs[2 * n:]
        x, y, c = lax.axis_index("x"), lax.axis_index("y"), lax.axis_index("c")
        local = [pltpu.make_async_copy(ins[k], outs[k].at[c], local_sems.at[k]) for k in range(n)]
        copies = [pltpu.make_async_remote_copy(
            src_ref=ins[k], dst_ref=outs[k].at[c], send_sem=send_sems.at[k], recv_sem=recv_sems.at[k],
            device_id=(x, y, 1 - c), device_id_type=MESH) for k in range(n)]
        for cp in local + copies:
            cp.start()
        for cp in copies:
            cp.wait_send()
        for k in range(n):
            pltpu.make_async_remote_copy(
                src_ref=ins[k], dst_ref=outs[k].at[1 - c], send_sem=send_sems.at[k], recv_sem=recv_sems.at[k],
                device_id=(x, y, 1 - c), device_id_type=MESH).wait_recv()
        for cp in local:
            cp.wait()

    anyspec = pl.BlockSpec(memory_space=pl.ANY)
    return pl.pallas_call(
        body, name="join_halves",
        out_shape=tuple(jax.ShapeDtypeStruct((2,) + h.shape, h.dtype) for h in halves),
        in_specs=[anyspec] * n, out_specs=tuple([anyspec] * n),
        scratch_shapes=[pltpu.SemaphoreType.DMA((n,)), pltpu.SemaphoreType.DMA((n,)), pltpu.SemaphoreType.DMA((n,))],
        compiler_params=pltpu.CompilerParams(has_side_effects=True),
    )(*halves)


def _reduce_scatter(grads, core, chip):
    split = [g.reshape(4, 2, g.shape[1] // 2, g.shape[2]) for g in grads]
    other = _swap_halves(split)
    parts = [_add_my_half(g, o, core) for g, o in zip(split, other)]
    got = _exchange_chips(parts)
    halves = [_sum_for_my_chip(p, q, chip) for p, q in zip(parts, got)]
    full = _join_halves(halves)
    return [f.reshape(g.shape[1], g.shape[2]) for f, g in zip(full, grads)]


def _all_gather_rows(block):
    m, n = block.shape

    def body(x_ref, out_ref, send_sems, recv_sems):
        x, y, c = lax.axis_index("x"), lax.axis_index("y"), lax.axis_index("c")
        me, sibling = (x, y, c), (x, y, 1 - c)
        chips = [(1 - x, y), (x, 1 - y), (1 - x, 1 - y)]

        def slot(px, py, pc):
            return out_ref.at[4 * px + 2 * py + pc]

        def copy(k, block_of, to, src=None):
            return pltpu.make_async_remote_copy(
                src_ref=slot(*block_of) if src is None else src, dst_ref=slot(*block_of),
                send_sem=send_sems.at[k], recv_sem=recv_sems.at[k], device_id=to, device_id_type=MESH)

        out_ref[4 * x + 2 * y + c] = x_ref[...]
        first = [copy(0, me, sibling, src=x_ref)]
        first += [copy(1 + r, me, (*chip, c), src=x_ref) for r, chip in enumerate(chips)]
        for cp in first:
            cp.start()
        passed = [copy(4 + r, (*chip, c), sibling) for r, chip in enumerate(chips)]
        for r, chip in enumerate(chips):
            copy(1 + r, (*chip, c), me).wait_recv()
            passed[r].start()
        copy(0, sibling, me).wait_recv()
        for r, chip in enumerate(chips):
            copy(4 + r, (*chip, 1 - c), me).wait_recv()
        for cp in first + passed:
            cp.wait_send()

    vm = pl.BlockSpec(memory_space=pltpu.VMEM)
    return pl.pallas_call(
        body, name="all_gather_rows",
        out_shape=jax.ShapeDtypeStruct((N_DEV, m, n), block.dtype),
        in_specs=[vm], out_specs=vm,
        scratch_shapes=[pltpu.SemaphoreType.DMA((7,)), pltpu.SemaphoreType.DMA((7,))],
        compiler_params=_cparams(),
    )(block)


def _sum_devices(gathered):
    nd, m, n = gathered.shape

    def body(g_ref, o_ref):
        acc = g_ref[0]
        for b in range(1, nd):
            acc = acc + g_ref[b]
        o_ref[...] = acc

    return pl.pallas_call(
        body, name="sum_devices", out_shape=jax.ShapeDtypeStruct((m, n), F32),
        in_specs=[pl.BlockSpec(memory_space=pltpu.VMEM)], out_specs=pl.BlockSpec(memory_space=pltpu.VMEM),
        compiler_params=_cparams(),
    )(gathered)


def _ada_weight_grad(c_all, dmod_cols):
    nl, nd, ncol = dmod_cols.shape
    d = c_all.shape[1]

    def body(c_ref, dm_ref, o_ref):
        o_ref[0] = lax.dot_general(c_ref[...], dm_ref[0], (((0,), (0,)), ((), ())),
                                   preferred_element_type=F32, precision=lax.Precision.HIGHEST)

    return pl.pallas_call(
        body, name="ada_weight_grad", grid=(nl,),
        in_specs=[pl.BlockSpec((nd, d), lambda l: (0, 0)), pl.BlockSpec((1, nd, ncol), lambda l: (l, 0, 0))],
        out_specs=pl.BlockSpec((1, d, ncol), lambda l: (l, 0, 0)),
        out_shape=jax.ShapeDtypeStruct((nl, d, ncol), F32), compiler_params=_cparams(("parallel",)),
    )(c_all, dmod_cols)


def _pad_rows(a, rows):
    return jnp.pad(a, ((0, rows - a.shape[0]), (0, 0)))


def _shard_cols(full, chip, width):
    return lax.dynamic_slice_in_dim(full, chip * width, width, axis=full.ndim - 1)


def kernel(x, c, ada_w, ada_b, pre_g, post_g, pool_w, pool_scale, cv_w_pw1, cv_b_pw1, cv_w_dw, cv_b_dw, cv_ln_g, cv_ln_b, cv_w_pw2, cv_b_pw2, ffn_w_up, ffn_w_dw, ffn_w_down, loss_target, m_ada_w, m_ada_b, m_pre_g, m_post_g, m_pool_w, m_pool_scale, m_cv_w_pw1, m_cv_b_pw1, m_cv_w_dw, m_cv_b_dw, m_cv_ln_g, m_cv_ln_b, m_cv_w_pw2, m_cv_b_pw2, m_ffn_w_up, m_ffn_w_dw, m_ffn_w_down, v_ada_w, v_ada_b, v_pre_g, v_post_g, v_pool_w, v_pool_scale, v_cv_w_pw1, v_cv_b_pw1, v_cv_w_dw, v_cv_b_dw, v_cv_ln_g, v_cv_ln_b, v_cv_w_pw2, v_cv_b_pw2, v_ffn_w_up, v_ffn_w_dw, v_ffn_w_down):
    s, d = x.shape[1], x.shape[2]
    dq = d // N_CHIPS
    n_g = pool_w.shape[1]
    gq = pool_w.shape[2]
    gd = pool_w.shape[3]
    kw = cv_w_dw.shape[1]
    cs = ffn_w_up.shape[2]
    fq = ffn_w_down.shape[1]
    chip = 2 * lax.axis_index("x") + lax.axis_index("y")
    core = lax.axis_index("c")
    chip1 = jnp.reshape(chip, (1,)).astype(jnp.int32)
    core1 = jnp.reshape(core, (1,)).astype(jnp.int32)
    xs, tgt = x[0], loss_target[0]

    c_rep, mod_rep = _ada_forward(c, ada_w)
    c_all = c_rep[:, 0, :]
    mod = mod_rep[:, :, 0, :].transpose(1, 0, 2).reshape(ada_b.shape) + ada_b

    small_rows = [pre_g.reshape(4, dq), post_g.reshape(4, dq), cv_w_dw[0], cv_b_dw, cv_ln_g, cv_ln_b, cv_b_pw2,
                  cv_b_pw1.reshape(2, dq)]
    small = jnp.concatenate(small_rows, axis=0)
    n_small = small.shape[0]
    small = _pad_rows(small, -(-n_small // 16) * 16)
    dwf = _pad_rows(ffn_w_dw.reshape(6, cs), 16)
    shards = [_cast_bf16(pool_w.reshape(n_g * gq, gd)), _cast_bf16(cv_w_pw1[0]), _cast_bf16(cv_w_pw2[0]),
              _cast_bf16(ffn_w_up[0]), _cast_bf16(ffn_w_up[1]), _cast_bf16(ffn_w_down[0]), _cast_bf16(ffn_w_down[1]),
              small, dwf]
    g_pool, g_pw1, g_pw2, g_up0, g_up1, g_dn0, g_dn1, g_small, g_dwf = _all_gather_shards(shards)
    poolw_full = g_pool.reshape(N_CHIPS, n_g, gq, gd).transpose(1, 0, 2, 3).reshape(n_g, gd, gd)
    pw2_full = g_pw2.reshape(d, d)
    w_up = [g_up0, g_up1]
    w_down = [g_dn0.reshape(2, 2 * fq, d), g_dn1.reshape(2, 2 * fq, d)]
    smallf = g_small.transpose(1, 0, 2).reshape(g_small.shape[1], d)
    pre_full, post_full = smallf[0:4].reshape(2, 2, d), smallf[4:8].reshape(2, 2, d)
    wdw31 = smallf[8:8 + kw]
    o = 8 + kw
    b_dw, ln_g, ln_b, b_pw2 = smallf[o:o + 1], smallf[o + 1:o + 2], smallf[o + 2:o + 3], smallf[o + 3:o + 4]
    b_pw1 = g_small[:, o + 4:o + 6, :].reshape(1, 2 * d)
    ffn_dw = g_dwf[:, :6, :].transpose(1, 0, 2).reshape(2, 3, N_CHIPS * cs)

    def sub_vec(layer, sub, extra=None):
        m6 = mod[layer].reshape(6, d)
        rows = [pre_full[layer, sub][None], 1.0 + m6[3 * sub + 1][None], m6[3 * sub][None], m6[3 * sub + 2][None],
                post_full[layer, sub][None]]
        if extra is not None:
            rows.append(extra)
        return _pad_rows(jnp.concatenate(rows, axis=0), 8)

    vec_pool = sub_vec(0, 0, pool_scale)
    vec_f0, vec_conv, vec_f1 = sub_vec(0, 1), sub_vec(1, 0), sub_vec(1, 1)
    cvec = _pad_rows(jnp.concatenate([b_dw, ln_g, ln_b, b_pw2], axis=0), 8)

    x1 = _pool_forward(xs, vec_pool, poolw_full)
    x2, h_f0, a0_f0, y_f0 = _ffn_forward(x1, vec_f0, w_up[0], ffn_dw[0], w_down[0])
    x3, h_cv, a_cv, uc_cv, z_cv, y_cv = _conv_forward(x2, vec_conv, cvec, g_pw1, b_pw1, wdw31, pw2_full)
    x4, h_f1, a0_f1, y_f1 = _ffn_forward(x3, vec_f1, w_up[1], ffn_dw[1], w_down[1])
    dx4, loss_rows = _loss_head(x4, tgt)

    dx3, da0_f1, u_f1, dy_f1, sum_f1, dwdw_f1 = _ffn_backward(dx4, x3, y_f1, a0_f1, vec_f1, w_up[1], ffn_dw[1], w_down[1])
    gw_up1 = _weight_grad(h_f1[None], da0_f1)
    gw_dn1 = _weight_grad(u_f1, dy_f1[None])
    dx2, da_cv, dy_cv, sum_cv, dwdw_cv = _conv_backward(dx3, x2, y_cv, a_cv, uc_cv, vec_conv, cvec, g_pw1, wdw31, pw2_full)
    gw_pw1 = _weight_grad(h_cv[None], da_cv)
    gw_pw2 = _weight_grad(z_cv[None], dy_cv[None])
    dx1, da0_f0, u_f0, dy_f0, sum_f0, dwdw_f0 = _ffn_backward(dx2, x1, y_f0, a0_f0, vec_f0, w_up[0], ffn_dw[0], w_down[0])
    gw_up0 = _weight_grad(h_f0[None], da0_f0)
    gw_dn0 = _weight_grad(u_f0, dy_f0[None])
    dx0, sum_pool, gw_pool = _pool_backward(dx1, xs, vec_pool, poolw_full)

    gw_pool4 = gw_pool.reshape(n_g, N_CHIPS, gq, gd).transpose(1, 0, 2, 3).reshape(N_CHIPS, n_g * gq, gd)
    big = [gw_pool4, gw_pw1, gw_pw2.reshape(N_CHIPS, dq, d), gw_up0, gw_up1,
           gw_dn0.reshape(N_CHIPS, fq, d), gw_dn1.reshape(N_CHIPS, fq, d)]
    r_pool, r_pw1, r_pw2, r_up0, r_up1, r_dn0, r_dn1 = _reduce_scatter(big, core1, chip1)

    slab = jnp.concatenate([sum_f1, sum_cv, dwdw_cv, sum_f0, sum_pool, loss_rows], axis=0)
    wide = jnp.concatenate([dwdw_f1, dwdw_f0], axis=0)
    slab_all = _all_gather_rows(slab)
    wide_all = _all_gather_rows(wide)
    tot = _sum_devices(slab_all)
    tot_wide = _sum_devices(wide_all)
    kpad = dwdw_cv.shape[0]
    o_cv, o_dw, o_f0 = 8, 24, 24 + kpad
    o_pool, o_loss = o_f0 + 8, o_f0 + 16
    loss = jnp.sum(tot[o_loss])
    dmod_l0 = jnp.concatenate([slab_all[:, o_pool + 4], slab_all[:, o_pool + 3], slab_all[:, o_pool + 1],
                               slab_all[:, o_f0 + 4], slab_all[:, o_f0 + 3], slab_all[:, o_f0 + 1]], axis=-1)
    dmod_l1 = jnp.concatenate([slab_all[:, o_cv + 4], slab_all[:, o_cv + 3], slab_all[:, o_cv + 1],
                               slab_all[:, 4], slab_all[:, 3], slab_all[:, 1]], axis=-1)
    dmod = jnp.stack([dmod_l0, dmod_l1], axis=0)
    g_ada_b = _sum_devices(dmod.transpose(1, 0, 2))
    ncol = ada_w.shape[2]
    g_ada_w = _ada_weight_grad(c_all, _shard_cols(dmod, chip, ncol))

    g_pre = jnp.stack([jnp.stack([tot[o_pool + 2], tot[o_f0 + 2]]), jnp.stack([tot[o_cv + 2], tot[2]])])
    g_post = jnp.stack([jnp.stack([tot[o_pool + 0], tot[o_f0 + 0]]), jnp.stack([tot[o_cv + 0], tot[0]])])
    g_pool_scale = tot[o_pool + 5][None]
    g_b_pw2, g_ln_g, g_ln_b, g_b_dw = tot[o_cv + 5], tot[o_cv + 6], tot[o_cv + 7], tot[o_cv + 8]
    g_b_pw1 = jnp.concatenate([tot[o_cv + 9], tot[o_cv + 10]])
    g_w_dw31 = tot[o_dw:o_dw + kw]
    g_ffn_dw = jnp.stack([tot_wide[8:11], tot_wide[0:3]])

    grads_small = {
        "pre_g": _shard_cols(g_pre, chip, dq), "post_g": _shard_cols(g_post, chip, dq),
        "pool_scale": g_pool_scale, "cv_b_pw1": _shard_cols(g_b_pw1[None], chip, 2 * dq),
        "cv_w_dw": _shard_cols(g_w_dw31[None], chip, dq), "cv_b_dw": _shard_cols(g_b_dw[None], chip, dq),
        "cv_ln_g": _shard_cols(g_ln_g[None], chip, dq), "cv_ln_b": _shard_cols(g_ln_b[None], chip, dq),
        "cv_b_pw2": _shard_cols(g_b_pw2[None], chip, dq), "ffn_w_dw": _shard_cols(g_ffn_dw, chip, cs),
        "ada_b": g_ada_b,
    }
    params_small = {
        "pre_g": (pre_g, m_pre_g, v_pre_g), "post_g": (post_g, m_post_g, v_post_g),
        "pool_scale": (pool_scale, m_pool_scale, v_pool_scale), "cv_b_pw1": (cv_b_pw1, m_cv_b_pw1, v_cv_b_pw1),
        "cv_w_dw": (cv_w_dw, m_cv_w_dw, v_cv_w_dw), "cv_b_dw": (cv_b_dw, m_cv_b_dw, v_cv_b_dw),
        "cv_ln_g": (cv_ln_g, m_cv_ln_g, v_cv_ln_g), "cv_ln_b": (cv_ln_b, m_cv_ln_b, v_cv_ln_b),
        "cv_b_pw2": (cv_b_pw2, m_cv_b_pw2, v_cv_b_pw2), "ffn_w_dw": (ffn_w_dw, m_ffn_w_dw, v_ffn_w_dw),
        "ada_b": (ada_b, m_ada_b, v_ada_b),
    }
    names = list(params_small)
    sizes = [params_small[nm][0].size for nm in names]
    padded = [-(-sz // 1024) * 1024 for sz in sizes]

    def pack(arrs):
        flat = [jnp.pad(a.reshape(-1), (0, p - a.size)) for a, p in zip(arrs, padded)]
        return jnp.concatenate(flat).reshape(1, -1, 128)

    pk_w = pack([params_small[nm][0] for nm in names])
    pk_m = pack([params_small[nm][1] for nm in names])
    pk_v = pack([params_small[nm][2] for nm in names])
    pk_g = pack([grads_small[nm].reshape(params_small[nm][0].shape) for nm in names])
    pk_d, pk_nm, pk_nv = _adamw(pk_w, pk_g, pk_m, pk_v)

    def unpack(pk):
        flat, out, off = pk.reshape(-1), {}, 0
        for nm, sz, p in zip(names, sizes, padded):
            out[nm] = flat[off:off + sz].reshape(params_small[nm][0].shape)
            off += p
        return out

    small_d, small_m, small_v = unpack(pk_d), unpack(pk_nm), unpack(pk_nv)
    small_g = {nm: grads_small[nm].reshape(params_small[nm][0].shape) for nm in names}

    big_g = {
        "ada_w": g_ada_w,
        "pool_w": r_pool.reshape(pool_w.shape),
        "cv_w_pw1": r_pw1[None], "cv_w_pw2": r_pw2[None],
        "ffn_w_up": jnp.stack([r_up0, r_up1]), "ffn_w_down": jnp.stack([r_dn0, r_dn1]),
    }
    big_p = {
        "ada_w": (ada_w, m_ada_w, v_ada_w), "pool_w": (pool_w, m_pool_w, v_pool_w),
        "cv_w_pw1": (cv_w_pw1, m_cv_w_pw1, v_cv_w_pw1), "cv_w_pw2": (cv_w_pw2, m_cv_w_pw2, v_cv_w_pw2),
        "ffn_w_up": (ffn_w_up, m_ffn_w_up, v_ffn_w_up), "ffn_w_down": (ffn_w_down, m_ffn_w_down, v_ffn_w_down),
    }
    big_d, big_m, big_v = {}, {}, {}
    for nm, (w, m, v) in big_p.items():
        shp = w.shape
        as3 = lambda t: t.reshape((-1,) + shp[-2:])
        dl, nm_, nv_ = _adamw(as3(w), as3(big_g[nm]), as3(m), as3(v))
        big_d[nm], big_m[nm], big_v[nm] = dl.reshape(shp), nm_.reshape(shp), nv_.reshape(shp)

    order = ["ada_w", "ada_b", "pre_g", "post_g", "pool_w", "pool_scale", "cv_w_pw1", "cv_b_pw1", "cv_w_dw", "cv_b_dw",
             "cv_ln_g", "cv_ln_b", "cv_w_pw2", "cv_b_pw2", "ffn_w_up", "ffn_w_dw", "ffn_w_down"]
    pick = lambda bigs, smalls: [bigs[nm] if nm in bigs else smalls[nm] for nm in order]
    return (loss, dx0[None], *pick(big_g, small_g), *pick(big_d, small_d), *pick(big_m, small_m),
            *pick(big_v, small_v))
```

```python
import functools

import jax
import jax.numpy as jnp
from jax import lax
from jax.experimental import pallas as pl
from jax.experimental.pallas import tpu as pltpu

F32 = jnp.float32
BF16 = jnp.bfloat16
EPS = 1e-6
N_CHIPS = 4
N_DEV = 8
POOL_WINDOWS = (2, 4, 8, 16)
POOL_HALO = 16
FFN_HALO = 16
ADAM_LR = 0.001
ADAM_B1 = 0.9
ADAM_B2 = 0.999
ADAM_EPS = 1e-08
ADAM_WD = 0.01
ADAM_STEP = 10
V7X_VMEM_LIMIT = 58 * 1024 * 1024
MESH = pl.DeviceIdType.MESH


def _cparams(sem=None, vmem=V7X_VMEM_LIMIT):
    return pltpu.CompilerParams(dimension_semantics=sem, vmem_limit_bytes=vmem)


def _row_tile(n, want):
    if n <= want:
        return n
    t = want - want % 8
    while n % t:
        t -= 8
    return t


def _lane_chunks(width):
    out, c = [], 0
    while c < width:
        w = min(512, width - c)
        out.append((c, w))
        c += w
    return out


def _dot(a, b):
    return jnp.dot(a, b, preferred_element_type=F32)


def _dot_nt(a, b):
    return lax.dot_general(a, b, (((1,), (1,)), ((), ())), preferred_element_type=F32)


def _dot_tn(a, b):
    return lax.dot_general(a, b, (((0,), (0,)), ((), ())), preferred_element_type=F32)


def _rms(x):
    r = lax.rsqrt(jnp.mean(x * x, axis=-1, keepdims=True) + EPS)
    return x * r, r


def _rms_bwd(dyn, yn, r):
    return r * (dyn - yn * jnp.mean(dyn * yn, axis=-1, keepdims=True))


def _sigmoid(x):
    return 1.0 / (1.0 + jnp.exp(-x))


def _colsum(x):
    return jnp.sum(x, axis=0, keepdims=True)


def _shift_down(x, k):
    return x if k == 0 else pltpu.roll(x, k, 0)


def _shift_up(x, k):
    return x if k == 0 else pltpu.roll(x, x.shape[0] - k, 0)


def _vec_rows(vec):
    return vec[0:1] * vec[1:2], vec[2:3], vec[3:4], vec[4:5]


def _add_rows(sum_ref, rows):
    for k, r in enumerate(rows):
        sum_ref[k:k + 1, :] += r


def _ada_forward(c, ada_w):
    n_layers, d, ncol = ada_w.shape

    def body(c_ref, w_ref, call_ref, mod_ref, part_ref, sendbuf, send_sems, recv_sems, send2, recv2):
        x, y, cc = lax.axis_index("x"), lax.axis_index("y"), lax.axis_index("c")
        me = 4 * x + 2 * y + cc
        rel = [(x, y, 1 - cc), (1 - x, y, cc), (x, 1 - y, cc), (1 - x, 1 - y, cc),
               (1 - x, y, 1 - cc), (x, 1 - y, 1 - cc), (1 - x, 1 - y, 1 - cc)]
        cv = c_ref[...]
        call_ref[me] = jnp.broadcast_to(cv * _sigmoid(cv), (8, d))

        def gather(k, block, to):
            blk = call_ref.at[block]
            return pltpu.make_async_remote_copy(src_ref=blk, dst_ref=blk, send_sem=send_sems.at[k],
                                                recv_sem=recv_sems.at[k], device_id=to, device_id_type=MESH)

        for k, to in enumerate(rel):
            gather(k, me, to).start()
        for k, (px, py, pc) in enumerate(rel):
            gather(k, 4 * px + 2 * py + pc, rel[k]).wait_recv()
        for k, to in enumerate(rel):
            gather(k, me, to).wait_send()

        ca = call_ref[...].reshape(8 * N_DEV, d)
        for l in range(n_layers):
            part_ref[l] = jnp.dot(ca, w_ref[l], preferred_element_type=F32, precision=lax.Precision.HIGHEST)

        j = 2 * x + y
        chips = [(1 - x, y), (x, 1 - y), (1 - x, 1 - y)]

        def rows_of(b):
            return part_ref[:, pl.ds(pl.multiple_of(8 * b, 8), 8), :]

        def scatter(k, src_j, to):
            return pltpu.make_async_remote_copy(
                src_ref=sendbuf.at[k], dst_ref=mod_ref.at[src_j], send_sem=send2.at[k], recv_sem=recv2.at[k],
                device_id=to, device_id_type=MESH)

        mod_ref[j] = rows_of(me)
        for k, (px, py) in enumerate(chips):
            sendbuf[k] = rows_of(4 * px + 2 * py + cc)
            scatter(k, j, (px, py, cc)).start()
        for k, (px, py) in enumerate(chips):
            scatter(k, 2 * px + py, (px, py, cc)).wait_recv()
        for k, (px, py) in enumerate(chips):
            scatter(k, j, (px, py, cc)).wait_send()

    vm = pl.BlockSpec(memory_space=pltpu.VMEM)
    return pl.pallas_call(
        body, name="ada_forward",
        out_shape=(jax.ShapeDtypeStruct((N_DEV, 8, d), F32), jax.ShapeDtypeStruct((N_CHIPS, n_layers, 8, ncol), F32)),
        in_specs=[vm, vm], out_specs=(vm, vm),
        scratch_shapes=[pltpu.VMEM((n_layers, 8 * N_DEV, ncol), F32), pltpu.VMEM((3, n_layers, 8, ncol), F32),
                        pltpu.SemaphoreType.DMA((7,)), pltpu.SemaphoreType.DMA((7,)),
                        pltpu.SemaphoreType.DMA((3,)), pltpu.SemaphoreType.DMA((3,))],
        compiler_params=_cparams(),
    )(c, ada_w)


def _cast_into_slot(w2d, chip):
    r, c = w2d.shape
    tr = _row_tile(r, 256)

    def body(chip_ref, w_ref, o_ref):
        o_ref[0] = w_ref[...].astype(BF16)

    return pl.pallas_call(
        body, name="cast_into_slot",
        grid_spec=pltpu.PrefetchScalarGridSpec(
            num_scalar_prefetch=1, grid=(r // tr,),
            in_specs=[pl.BlockSpec((tr, c), lambda i, chip_ref: (i, 0))],
            out_specs=pl.BlockSpec((1, tr, c), lambda i, chip_ref: (chip_ref[0], i, 0))),
        out_shape=jax.ShapeDtypeStruct((N_CHIPS, r, c), BF16), compiler_params=_cparams(("parallel",)),
    )(chip, w2d)


def _place():
    x, y, c = lax.axis_index("x"), lax.axis_index("y"), lax.axis_index("c")
    return x, y, c, [(1 - x, y), (x, 1 - y), (1 - x, 1 - y)]


def _remote(src, dst, send_sem, recv_sem, to):
    return pltpu.make_async_remote_copy(src_ref=src, dst_ref=dst, send_sem=send_sem, recv_sem=recv_sem,
                                        device_id=to, device_id_type=MESH)


class _AllGather:
    def __init__(self, bufs):
        self.arrays = list(bufs)
        self.outs = [jax.ShapeDtypeStruct(b.shape, b.dtype) for b in bufs]
        self.aliased = True
        self.n_sems = 6 * len(bufs)

    def run(self, phase, ins, outs, send_sems, recv_sems, base):
        x, y, c, chips = _place()
        j = 2 * x + y
        for k, buf in enumerate(outs):
            half = buf.shape[1] // 2

            def part(src_j, h):
                return buf.at[src_j, pl.ds(h * half, half), :]

            def ici(r, src_j, to):
                s = base + 6 * k + r
                return _remote(part(src_j, c), part(src_j, c), send_sems.at[s], recv_sems.at[s], to)

            def d2d(r, src_j, h):
                s = base + 6 * k + 3 + r
                return _remote(part(src_j, h), part(src_j, h), send_sems.at[s], recv_sems.at[s], (x, y, 1 - c))

            for r, (px, py) in enumerate(chips):
                if phase == 0:
                    ici(r, j, (px, py, c)).start()
                elif phase == 1:
                    ici(r, 2 * px + py, (px, py, c)).wait_recv()
                    d2d(r, 2 * px + py, c).start()
                else:
                    d2d(r, 2 * px + py, 1 - c).wait_recv()
                    ici(r, j, (px, py, c)).wait_send()
                    d2d(r, 2 * px + py, c).wait_send()


class _Swap:
    def __init__(self, grads):
        self.arrays = list(grads)
        self.outs = [jax.ShapeDtypeStruct((g.shape[0],) + g.shape[2:], g.dtype) for g in grads]
        self.aliased = False
        self.n_sems = len(grads)

    def run(self, phase, ins, outs, send_sems, recv_sems, base):
        x, y, c, _ = _place()
        for k in range(len(ins)):
            cp = _remote(ins[k].at[:, 1 - c], outs[k], send_sems.at[base + k], recv_sems.at[base + k], (x, y, 1 - c))
            if phase == 0:
                cp.start()
            elif phase == 2:
                cp.wait()


class _Exchange:
    def __init__(self, parts):
        self.arrays = list(parts)
        self.outs = [jax.ShapeDtypeStruct((3,) + p.shape[1:], p.dtype) for p in parts]
        self.aliased = False
        self.n_sems = 3 * len(parts)

    def run(self, phase, ins, outs, send_sems, recv_sems, base):
        x, y, c, chips = _place()
        for k in range(len(ins)):
            for r, (px, py) in enumerate(chips):
                s = base + 3 * k + r
                cp = _remote(ins[k].at[2 * px + py], outs[k].at[r], send_sems.at[s], recv_sems.at[s], (px, py, c))
                if phase == 0:
                    cp.start()
                elif phase == 2:
                    cp.wait()


class _Join:
    def __init__(self, bufs):
        self.arrays = list(bufs)
        self.outs = [jax.ShapeDtypeStruct(b.shape, b.dtype) for b in bufs]
        self.aliased = True
        self.n_sems = len(bufs)

    def run(self, phase, ins, outs, send_sems, recv_sems, base):
        x, y, c, _ = _place()
        for k, buf in enumerate(outs):
            mine = _remote(buf.at[c], buf.at[c], send_sems.at[base + k], recv_sems.at[base + k], (x, y, 1 - c))
            if phase == 0:
                mine.start()
            elif phase == 2:
                mine.wait_send()
                _remote(buf.at[1 - c], buf.at[1 - c], send_sems.at[base + k], recv_sems.at[base + k],
                        (x, y, 1 - c)).wait_recv()


class _Comm:
    def __init__(self, ops):
        self.ops = list(ops)
        self.arrays = [a for op in self.ops for a in op.arrays]
        self.outs = [o for op in self.ops for o in op.outs]
        self.n_sems = sum(op.n_sems for op in self.ops)

    def specs(self):
        return [pl.BlockSpec(memory_space=pl.ANY)] * len(self.arrays)

    def aliases(self, first_in, first_out):
        out, k = {}, 0
        for op in self.ops:
            for i in range(len(op.arrays)):
                if op.aliased:
                    out[first_in + k + i] = first_out + k + i
            k += len(op.arrays)
        return out

    def scratch(self):
        return [pltpu.SemaphoreType.DMA((self.n_sems,)), pltpu.SemaphoreType.DMA((self.n_sems,))]

    def run(self, phase, ins, outs, send_sems, recv_sems):
        k = base = 0
        for op in self.ops:
            n = len(op.arrays)
            op.run(phase, ins[k:k + n], outs[k:k + n], send_sems, recv_sems, base)
            k += n
            base += op.n_sems

    def split(self, results):
        out, k = [], 0
        for op in self.ops:
            out.append(list(results[k:k + len(op.arrays)]))
            k += len(op.arrays)
        return out


def _communicate(ops):
    comm = _Comm(ops)
    n = len(comm.arrays)

    def body(*refs):
        ins, outs, (send_sems, recv_sems) = refs[:n], refs[n:2 * n], refs[2 * n:]
        for phase in range(3):
            comm.run(phase, ins, outs, send_sems, recv_sems)

    res = pl.pallas_call(
        body, name="communicate", out_shape=tuple(comm.outs), in_specs=comm.specs(), out_specs=tuple(comm.specs()),
        input_output_aliases=comm.aliases(0, 0), scratch_shapes=comm.scratch(),
    )(*comm.arrays)
    return comm.split(res)


def _pool_core(he, w_ref, scale, first_row, halo, n_rows):
    d = he.shape[1]
    gd = d // len(POOL_WINDOWS)
    t = first_row + lax.broadcasted_iota(jnp.int32, (n_rows, 1), 0)
    pooled, ypre, cnts = [], [], []
    for g, w in enumerate(POOL_WINDOWS):
        hg = he[:, g * gd:(g + 1) * gd]
        s, k = hg, 1
        while k < w:
            s = s + _shift_down(s, k)
            k *= 2
        cnt = jnp.minimum(t + 1, w).astype(F32)
        p = s[halo:] / cnt - hg[halo:]
        pooled.append(p.astype(BF16))
        cnts.append(cnt)
        ypre.append(_dot(pooled[-1], w_ref[g]))
    return pooled, jnp.concatenate(ypre, axis=1), cnts


def _pool_forward(x, vec, pool_w):
    s, d = x.shape
    ts = _row_tile(s, 512)
    n_g, gd, _ = pool_w.shape

    def body(x_ref, vec_ref, w_ref, o_ref, carry):
        i = pl.program_id(0)

        @pl.when(i == 0)
        def _():
            carry[...] = jnp.zeros_like(carry)

        vec = vec_ref[...]
        a, sh, gt, gpost = _vec_rows(vec)
        xb = x_ref[...]
        xn, _ = _rms(xb)
        h = xn * a + sh
        he = jnp.concatenate([carry[...], h], axis=0)
        carry[...] = h[ts - POOL_HALO:]
        _, ypre, _ = _pool_core(he, w_ref, vec[5:6], i * ts, POOL_HALO, ts)
        yn, _ = _rms(ypre * vec[5:6])
        o_ref[...] = xb + gt * (yn * gpost)

    return pl.pallas_call(
        body, name="pool_forward", grid=(s // ts,),
        in_specs=[pl.BlockSpec((ts, d), lambda i: (i, 0)), pl.BlockSpec((8, d), lambda i: (0, 0)),
                  pl.BlockSpec((n_g, gd, gd), lambda i: (0, 0, 0))],
        out_specs=pl.BlockSpec((ts, d), lambda i: (i, 0)),
        out_shape=jax.ShapeDtypeStruct((s, d), F32),
        scratch_shapes=[pltpu.VMEM((POOL_HALO, d), F32)],
        compiler_params=_cparams(("arbitrary",)),
    )(x, vec, pool_w)


def _pool_backward(dout, x, vec, pool_w):
    s, d = x.shape
    ts = _row_tile(s, 512)
    nb = s // ts
    hb = ts // POOL_HALO
    n_g, gd, _ = pool_w.shape

    def body(do_ref, x_ref, xh_ref, vec_ref, w_ref, dx_ref, sum_ref, dw_ref, carry):
        step = pl.program_id(0)
        i = nb - 1 - step

        @pl.when(step == 0)
        def _():
            carry[...] = jnp.zeros_like(carry)
            sum_ref[...] = jnp.zeros_like(sum_ref)
            dw_ref[...] = jnp.zeros_like(dw_ref)

        vec = vec_ref[...]
        a, sh, gt, gpost = _vec_rows(vec)
        scale = vec[5:6]
        do = do_ref[...]
        xe = jnp.concatenate([xh_ref[...], x_ref[...]], axis=0)
        xne, re = _rms(xe)
        he = xne * a + sh
        rowid = lax.broadcasted_iota(jnp.int32, (POOL_HALO + ts, 1), 0)
        he = jnp.where((rowid >= POOL_HALO) | (i > 0), he, 0.0)
        xn, r = xne[POOL_HALO:], re[POOL_HALO:]
        pooled, ypre, cnts = _pool_core(he, w_ref, scale, i * ts, POOL_HALO, ts)
        yn, ry = _rms(ypre * scale)
        dyn = do * (gt * gpost)
        dy = _rms_bwd(dyn, yn, ry)
        dypre = (dy * scale).astype(BF16)
        dh_parts, q_parts = [], []
        for g, w in enumerate(POOL_WINDOWS):
            dyg = dypre[:, g * gd:(g + 1) * gd]
            dpool = _dot_nt(dyg, w_ref[g])
            dw_ref[g] += _dot_tn(pooled[g], dyg)
            q = dpool / cnts[g]
            qe = jnp.concatenate([q, carry[:, g * gd:(g + 1) * gd]], axis=0)
            acc, k = qe, 1
            while k < w:
                acc = acc + _shift_up(acc, k)
                k *= 2
            dh_parts.append(acc[:ts] - dpool)
            q_parts.append(q[:POOL_HALO])
        carry[...] = jnp.concatenate(q_parts, axis=1)
        dh = jnp.concatenate(dh_parts, axis=1)
        dxn = dh * a
        dx_ref[...] = do + _rms_bwd(dxn, xn, r)
        _add_rows(sum_ref, [_colsum(do * gt * yn), _colsum(do * yn * gpost), _colsum(dh * xn * vec[1:2]),
                            _colsum(dh * xn * vec[0:1]), _colsum(dh), _colsum(dy * ypre)])

    blk = lambda st: (nb - 1 - st, 0)
    return pl.pallas_call(
        body, name="pool_backward", grid=(nb,),
        in_specs=[pl.BlockSpec((ts, d), blk), pl.BlockSpec((ts, d), blk),
                  pl.BlockSpec((POOL_HALO, d), lambda st: (jnp.maximum((nb - 1 - st) * hb - 1, 0), 0)),
                  pl.BlockSpec((8, d), lambda st: (0, 0)), pl.BlockSpec((n_g, gd, gd), lambda st: (0, 0, 0))],
        out_specs=(pl.BlockSpec((ts, d), blk), pl.BlockSpec((8, d), lambda st: (0, 0)),
                   pl.BlockSpec((n_g, gd, gd), lambda st: (0, 0, 0))),
        out_shape=(jax.ShapeDtypeStruct((s, d), F32), jax.ShapeDtypeStruct((8, d), F32),
                   jax.ShapeDtypeStruct((n_g, gd, gd), F32)),
        scratch_shapes=[pltpu.VMEM((POOL_HALO, d), F32)],
        compiler_params=_cparams(("arbitrary",)),
    )(dout, x, x, vec, pool_w)


def _ffn_forward(x, vec, w_up, w_dw, w_down, comm=None):
    s, d = x.shape
    _, _, cs = w_up.shape
    ts = _row_tile(s, 256)
    nb = s // ts
    chunks = _lane_chunks(cs)
    comm = comm or _Comm([])
    nc = len(comm.arrays)

    def body(*refs):
        x_ref, vec_ref, wup_ref, wdw_ref, wdn_ref = refs[:5]
        cin = refs[5:5 + nc]
        o_ref, h_ref, a0_ref, y_ref = refs[5 + nc:9 + nc]
        cout = refs[9 + nc:9 + 2 * nc]
        carry, yacc = refs[9 + 2 * nc:11 + 2 * nc]
        sems = refs[11 + 2 * nc:]
        i = pl.program_id(0)

        @pl.when(i == 0)
        def _():
            carry[...] = jnp.zeros_like(carry)
            if nc:
                comm.run(0, cin, cout, *sems)

        if nc:
            @pl.when(i == (3 * nb) // 4)
            def _():
                comm.run(1, cin, cout, *sems)

        vec = vec_ref[...]
        a, sh, gt, gpost = _vec_rows(vec)
        xb = x_ref[...]
        xn, _ = _rms(xb)
        hb = (xn * a + sh).astype(BF16)
        h_ref[...] = hb
        yacc[...] = jnp.zeros_like(yacc)
        for q in range(2):
            for c0, cw in chunks:
                conv = []
                for j in (q, q + 2):
                    a0 = _dot(hb, wup_ref[j, :, c0:c0 + cw])
                    a0_ref[j, :, c0:c0 + cw] = a0.astype(BF16)
                    ae = jnp.concatenate([carry[j, :, c0:c0 + cw], a0], axis=0)
                    carry[j, :, c0:c0 + cw] = a0[ts - FFN_HALO:]
                    w = wdw_ref[:, j * cs + c0:j * cs + c0 + cw]
                    conv.append((w[2:3] * ae + w[1:2] * _shift_down(ae, 1) + w[0:1] * _shift_down(ae, 2))[FFN_HALO:])
                u = (conv[0] * _sigmoid(conv[0]) * conv[1]).astype(BF16)
                yacc[...] += _dot(u, wdn_ref[q, c0:c0 + cw, :])
        y = yacc[...]
        y_ref[...] = y
        yn, _ = _rms(y)
        o_ref[...] = xb + gt * (yn * gpost)
        if nc:
            @pl.when(i == nb - 1)
            def _():
                comm.run(2, cin, cout, *sems)

    const3 = lambda i: (0, 0, 0)
    res = pl.pallas_call(
        body, name="ffn_forward", grid=(nb,),
        in_specs=[pl.BlockSpec((ts, d), lambda i: (i, 0)), pl.BlockSpec((8, d), lambda i: (0, 0)),
                  pl.BlockSpec(w_up.shape, const3, pipeline_mode=pl.Buffered(1)),
                  pl.BlockSpec(w_dw.shape, lambda i: (0, 0)),
                  pl.BlockSpec(w_down.shape, const3, pipeline_mode=pl.Buffered(1))] + comm.specs(),
        out_specs=(pl.BlockSpec((ts, d), lambda i: (i, 0)), pl.BlockSpec((ts, d), lambda i: (i, 0)),
                   pl.BlockSpec((4, ts, cs), lambda i: (0, i, 0)), pl.BlockSpec((ts, d), lambda i: (i, 0)),
                   *comm.specs()),
        out_shape=(jax.ShapeDtypeStruct((s, d), F32), jax.ShapeDtypeStruct((s, d), BF16),
                   jax.ShapeDtypeStruct((4, s, cs), BF16), jax.ShapeDtypeStruct((s, d), F32), *comm.outs),
        input_output_aliases=comm.aliases(5, 4),
        scratch_shapes=[pltpu.VMEM((4, FFN_HALO, cs), F32), pltpu.VMEM((ts, d), F32)] + (comm.scratch() if nc else []),
        compiler_params=_cparams(("arbitrary",)),
    )(x, vec, w_up, w_dw, w_down, *comm.arrays)
    return res[:4], comm.split(res[4:])


def _ffn_backward(dout, x, y, a0, vec, w_up, w_dw, w_down):
    s, d = x.shape
    _, _, cs = w_up.shape
    ts = _row_tile(s, 256)
    nb = s // ts
    hb = ts // FFN_HALO
    chunks = _lane_chunks(cs)

    def body(do_ref, x_ref, y_ref, a0_ref, a0h_ref, vec_ref, wup_ref, wdw_ref, wdn_ref,
             dx_ref, da0_ref, u_ref, dy_ref, sum_ref, dwdw_ref, carry, dhacc):
        step = pl.program_id(0)
        i = nb - 1 - step

        @pl.when(step == 0)
        def _():
            carry[...] = jnp.zeros_like(carry)
            sum_ref[...] = jnp.zeros_like(sum_ref)
            dwdw_ref[...] = jnp.zeros_like(dwdw_ref)

        vec = vec_ref[...]
        a, sh, gt, gpost = _vec_rows(vec)
        do = do_ref[...]
        yn, ry = _rms(y_ref[...])
        dy = _rms_bwd(do * (gt * gpost), yn, ry)
        dyb = dy.astype(BF16)
        dy_ref[...] = dyb
        not_first = (i > 0).astype(F32)
        dhacc[...] = jnp.zeros_like(dhacc)
        for q in range(2):
            for c0, cw in chunks:
                ae, r1, r2, conv, wts = {}, {}, {}, {}, {}
                for j in (q, q + 2):
                    halo = a0h_ref[j, :, c0:c0 + cw].astype(F32) * not_first
                    ae[j] = jnp.concatenate([halo, a0_ref[j, :, c0:c0 + cw].astype(F32)], axis=0)
                    r1[j], r2[j] = _shift_down(ae[j], 1), _shift_down(ae[j], 2)
                    wts[j] = wdw_ref[:, j * cs + c0:j * cs + c0 + cw]
                    conv[j] = (wts[j][2:3] * ae[j] + wts[j][1:2] * r1[j] + wts[j][0:1] * r2[j])[FFN_HALO:]
                cg, cv = conv[q], conv[q + 2]
                sg = _sigmoid(cg)
                sl = cg * sg
                u_ref[q, :, c0:c0 + cw] = (sl * cv).astype(BF16)
                du = _dot_nt(dyb, wdn_ref[q, c0:c0 + cw, :])
                dconv = {q: du * cv * (sg * (1.0 + cg * (1.0 - sg))), q + 2: du * sl}
                for j in (q, q + 2):
                    dae = jnp.concatenate([dconv[j], carry[j, :, c0:c0 + cw]], axis=0)
                    carry[j, :, c0:c0 + cw] = dconv[j][:FFN_HALO]
                    w = wts[j]
                    da0 = (w[2:3] * dae + w[1:2] * _shift_up(dae, 1) + w[0:1] * _shift_up(dae, 2))[:ts]
                    da0b = da0.astype(BF16)
                    da0_ref[j, :, c0:c0 + cw] = da0b
                    dhacc[...] += _dot_nt(da0b, wup_ref[j, :, c0:c0 + cw])
                    lanes = slice(j * cs + c0, j * cs + c0 + cw)
                    dwdw_ref[0:1, lanes] += _colsum(dconv[j] * r2[j][FFN_HALO:])
                    dwdw_ref[1:2, lanes] += _colsum(dconv[j] * r1[j][FFN_HALO:])
                    dwdw_ref[2:3, lanes] += _colsum(dconv[j] * ae[j][FFN_HALO:])
        dh = dhacc[...]
        xn, r = _rms(x_ref[...])
        dx_ref[...] = do + _rms_bwd(dh * a, xn, r)
        _add_rows(sum_ref, [_colsum(do * gt * yn), _colsum(do * yn * gpost), _colsum(dh * xn * vec[1:2]),
                            _colsum(dh * xn * vec[0:1]), _colsum(dh)])

    blk = lambda st: (nb - 1 - st, 0)
    blk3 = lambda st: (0, nb - 1 - st, 0)
    const3 = lambda st: (0, 0, 0)
    return pl.pallas_call(
        body, name="ffn_backward", grid=(nb,),
        in_specs=[pl.BlockSpec((ts, d), blk), pl.BlockSpec((ts, d), blk), pl.BlockSpec((ts, d), blk),
                  pl.BlockSpec((4, ts, cs), blk3),
                  pl.BlockSpec((4, FFN_HALO, cs), lambda st: (0, jnp.maximum((nb - 1 - st) * hb - 1, 0), 0)),
                  pl.BlockSpec((8, d), lambda st: (0, 0)),
                  pl.BlockSpec(w_up.shape, const3, pipeline_mode=pl.Buffered(1)),
                  pl.BlockSpec(w_dw.shape, lambda st: (0, 0)),
                  pl.BlockSpec(w_down.shape, const3, pipeline_mode=pl.Buffered(1))],
        out_specs=(pl.BlockSpec((ts, d), blk), pl.BlockSpec((4, ts, cs), blk3), pl.BlockSpec((2, ts, cs), blk3),
                   pl.BlockSpec((ts, d), blk), pl.BlockSpec((8, d), lambda st: (0, 0)),
                   pl.BlockSpec((8, 4 * cs), lambda st: (0, 0))),
        out_shape=(jax.ShapeDtypeStruct((s, d), F32), jax.ShapeDtypeStruct((4, s, cs), BF16),
                   jax.ShapeDtypeStruct((2, s, cs), BF16), jax.ShapeDtypeStruct((s, d), BF16),
                   jax.ShapeDtypeStruct((8, d), F32), jax.ShapeDtypeStruct((8, 4 * cs), F32)),
        scratch_shapes=[pltpu.VMEM((4, FFN_HALO, cs), F32), pltpu.VMEM((ts, d), F32)],
        compiler_params=_cparams(("arbitrary",)),
    )(dout, x, y, a0, a0, vec, w_up, w_dw, w_down)


def _conv_halo(width):
    return -(-(width - 1) // 8) * 8


def _conv_forward(x, vec, cvec, w_pw1, b_pw1, w_dw, w_pw2):
    s, d = x.shape
    kw = w_dw.shape[0]
    halo = _conv_halo(kw)
    ts = _row_tile(s, 256)
    hd = d // 2

    def body(x_ref, vec_ref, cvec_ref, w1_ref, b1_ref, wdw_ref, w2_ref,
             o_ref, h_ref, a_ref, uc_ref, z_ref, y_ref, carry):
        i = pl.program_id(0)

        @pl.when(i == 0)
        def _():
            carry[...] = jnp.zeros_like(carry)

        vec, cvec = vec_ref[...], cvec_ref[...]
        a, sh, gt, gpost = _vec_rows(vec)
        xb = x_ref[...]
        xn, _ = _rms(xb)
        hb = (xn * a + sh).astype(BF16)
        h_ref[...] = hb
        for j in range(4):
            a_ref[:, j * hd:(j + 1) * hd] = _dot(hb, w1_ref[j]) + b1_ref[:, j * hd:(j + 1) * hd]
        u = a_ref[:, :d] * _sigmoid(a_ref[:, d:])
        ue = jnp.concatenate([carry[...], u], axis=0)
        carry[...] = u[ts - halo:]
        uc = jnp.zeros((ts, d), F32) + cvec[0:1]
        for k in range(kw):
            uc = uc + wdw_ref[k:k + 1, :] * _shift_down(ue, kw - 1 - k)[halo:]
        uc_ref[...] = uc
        mu = jnp.mean(uc, axis=-1, keepdims=True)
        cen = uc - mu
        rstd = lax.rsqrt(jnp.mean(cen * cen, axis=-1, keepdims=True) + EPS)
        l = cen * rstd * cvec[1:2] + cvec[2:3]
        zb = (l * _sigmoid(l)).astype(BF16)
        z_ref[...] = zb
        y = _dot(zb, w2_ref[...]) + cvec[3:4]
        y_ref[...] = y
        yn, _ = _rms(y)
        o_ref[...] = xb + gt * (yn * gpost)

    row = lambda i: (i, 0)
    const2 = lambda i: (0, 0)
    return pl.pallas_call(
        body, name="conv_forward", grid=(s // ts,),
        in_specs=[pl.BlockSpec((ts, d), row), pl.BlockSpec((8, d), const2), pl.BlockSpec((8, d), const2),
                  pl.BlockSpec(w_pw1.shape, lambda i: (0, 0, 0)), pl.BlockSpec(b_pw1.shape, const2),
                  pl.BlockSpec(w_dw.shape, const2), pl.BlockSpec(w_pw2.shape, const2)],
        out_specs=(pl.BlockSpec((ts, d), row), pl.BlockSpec((ts, d), row), pl.BlockSpec((ts, 2 * d), row),
                   pl.BlockSpec((ts, d), row), pl.BlockSpec((ts, d), row), pl.BlockSpec((ts, d), row)),
        out_shape=(jax.ShapeDtypeStruct((s, d), F32), jax.ShapeDtypeStruct((s, d), BF16),
                   jax.ShapeDtypeStruct((s, 2 * d), F32), jax.ShapeDtypeStruct((s, d), F32),
                   jax.ShapeDtypeStruct((s, d), BF16), jax.ShapeDtypeStruct((s, d), F32)),
        scratch_shapes=[pltpu.VMEM((halo, d), F32)],
        compiler_params=_cparams(("arbitrary",)),
    )(x, vec, cvec, w_pw1, b_pw1, w_dw, w_pw2)


def _conv_backward(dout, x, y, a_pre, uc, vec, cvec, w_pw1, w_dw, w_pw2):
    s, d = x.shape
    kw = w_dw.shape[0]
    kpad = -(-kw // 8) * 8
    halo = _conv_halo(kw)
    ts = _row_tile(s, 256)
    nb = s // ts
    hb = ts // halo
    hd = d // 2

    def body(do_ref, x_ref, y_ref, a_ref, ah_ref, uc_ref, vec_ref, cvec_ref, w1_ref, wdw_ref, w2_ref,
             dx_ref, da_ref, dy_ref, sum_ref, dwdw_ref, carry):
        step = pl.program_id(0)
        i = nb - 1 - step

        @pl.when(step == 0)
        def _():
            carry[...] = jnp.zeros_like(carry)
            sum_ref[...] = jnp.zeros_like(sum_ref)
            dwdw_ref[...] = jnp.zeros_like(dwdw_ref)

        vec, cvec = vec_ref[...], cvec_ref[...]
        a, sh, gt, gpost = _vec_rows(vec)
        do = do_ref[...]
        yn, ry = _rms(y_ref[...])
        dy = _rms_bwd(do * (gt * gpost), yn, ry)
        dyb = dy.astype(BF16)
        dy_ref[...] = dyb
        dz = _dot_nt(dyb, w2_ref[...])
        uc = uc_ref[...]
        mu = jnp.mean(uc, axis=-1, keepdims=True)
        cen = uc - mu
        rstd = lax.rsqrt(jnp.mean(cen * cen, axis=-1, keepdims=True) + EPS)
        lhat = cen * rstd
        l = lhat * cvec[1:2] + cvec[2:3]
        sgl = _sigmoid(l)
        dl = dz * (sgl * (1.0 + l * (1.0 - sgl)))
        dlhat = dl * cvec[1:2]
        duc = rstd * (dlhat - jnp.mean(dlhat, axis=-1, keepdims=True)
                      - lhat * jnp.mean(dlhat * lhat, axis=-1, keepdims=True))
        ae = jnp.concatenate([ah_ref[...] * (i > 0).astype(F32), a_ref[...]], axis=0)
        sgate = _sigmoid(ae[:, d:])
        val = ae[:, :d]
        ue = val * sgate
        rowid = lax.broadcasted_iota(jnp.int32, (halo + ts, 1), 0)
        ue = jnp.where((rowid >= halo) | (i > 0), ue, 0.0)
        duce = jnp.concatenate([duc, carry[...]], axis=0)
        carry[...] = duc[:halo]
        du = jnp.zeros((ts, d), F32)
        for k in range(kw):
            du = du + wdw_ref[k:k + 1, :] * _shift_up(duce, kw - 1 - k)[:ts]
            dwdw_ref[k:k + 1, :] += _colsum(duc * _shift_down(ue, kw - 1 - k)[halo:])
        sg, vl = sgate[halo:], val[halo:]
        dval = du * sg
        dgate = du * vl * (sg * (1.0 - sg))
        dvb, dgb = dval.astype(BF16), dgate.astype(BF16)
        dh = jnp.zeros((ts, d), F32)
        for j in range(2):
            da_ref[j] = dvb[:, j * hd:(j + 1) * hd]
            da_ref[j + 2] = dgb[:, j * hd:(j + 1) * hd]
            dh = dh + _dot_nt(dvb[:, j * hd:(j + 1) * hd], w1_ref[j]) + _dot_nt(dgb[:, j * hd:(j + 1) * hd], w1_ref[j + 2])
        xn, r = _rms(x_ref[...])
        dx_ref[...] = do + _rms_bwd(dh * a, xn, r)
        _add_rows(sum_ref, [_colsum(do * gt * yn), _colsum(do * yn * gpost), _colsum(dh * xn * vec[1:2]),
                            _colsum(dh * xn * vec[0:1]), _colsum(dh), _colsum(dy), _colsum(dl * lhat), _colsum(dl),
                            _colsum(duc), _colsum(dval), _colsum(dgate)])

    blk = lambda st: (nb - 1 - st, 0)
    const2 = lambda st: (0, 0)
    return pl.pallas_call(
        body, name="conv_backward", grid=(nb,),
        in_specs=[pl.BlockSpec((ts, d), blk), pl.BlockSpec((ts, d), blk), pl.BlockSpec((ts, d), blk),
                  pl.BlockSpec((ts, 2 * d), blk),
                  pl.BlockSpec((halo, 2 * d), lambda st: (jnp.maximum((nb - 1 - st) * hb - 1, 0), 0)),
                  pl.BlockSpec((ts, d), blk), pl.BlockSpec((8, d), const2), pl.BlockSpec((8, d), const2),
                  pl.BlockSpec(w_pw1.shape, lambda st: (0, 0, 0)), pl.BlockSpec(w_dw.shape, const2),
                  pl.BlockSpec(w_pw2.shape, const2)],
        out_specs=(pl.BlockSpec((ts, d), blk), pl.BlockSpec((4, ts, hd), lambda st: (0, nb - 1 - st, 0)),
                   pl.BlockSpec((ts, d), blk), pl.BlockSpec((16, d), const2), pl.BlockSpec((kpad, d), const2)),
        out_shape=(jax.ShapeDtypeStruct((s, d), F32), jax.ShapeDtypeStruct((4, s, hd), BF16),
                   jax.ShapeDtypeStruct((s, d), BF16), jax.ShapeDtypeStruct((16, d), F32),
                   jax.ShapeDtypeStruct((kpad, d), F32)),
        scratch_shapes=[pltpu.VMEM((halo, d), F32)],
        compiler_params=_cparams(("arbitrary",)),
    )(dout, x, y, a_pre, a_pre, uc, vec, cvec, w_pw1, w_dw, w_pw2)


def _loss_head(y, target):
    s, d = y.shape
    ts = _row_tile(s, 512)

    def body(y_ref, t_ref, dy_ref, l_ref):
        @pl.when(pl.program_id(0) == 0)
        def _():
            l_ref[...] = jnp.zeros_like(l_ref)

        e = y_ref[...] - t_ref[...]
        dy_ref[...] = e * (1.0 / d)
        l_ref[0:1, :] += _colsum(e * e) * (0.5 / d)

    row = lambda i: (i, 0)
    return pl.pallas_call(
        body, name="loss_head", grid=(s // ts,),
        in_specs=[pl.BlockSpec((ts, d), row), pl.BlockSpec((ts, d), row)],
        out_specs=(pl.BlockSpec((ts, d), row), pl.BlockSpec((8, d), lambda i: (0, 0))),
        out_shape=(jax.ShapeDtypeStruct((s, d), F32), jax.ShapeDtypeStruct((8, d), F32)),
        compiler_params=_cparams(("arbitrary",)),
    )(y, target)


def _weight_grad(a, b, comm=None):
    na, s, k = a.shape
    nb_, _, n = b.shape
    nj = max(na, nb_)
    ts = _row_tile(s, 512)
    nt = s // ts
    comm = comm or _Comm([])
    nc = len(comm.arrays)

    def body(*refs):
        a_ref, b_ref = refs[:2]
        cin = refs[2:2 + nc]
        o_ref = refs[2 + nc]
        cout = refs[3 + nc:3 + 2 * nc]
        sems = refs[3 + 2 * nc:]
        j, t = pl.program_id(0), pl.program_id(1)

        if nc:
            @pl.when((j == 0) & (t == 0))
            def _():
                comm.run(0, cin, cout, *sems)

            @pl.when((j == nj // 2) & (t == nt // 2))
            def _():
                comm.run(1, cin, cout, *sems)

        @pl.when(t == 0)
        def _():
            o_ref[...] = jnp.zeros_like(o_ref)

        o_ref[0] += _dot_tn(a_ref[0], b_ref[0])

        if nc:
            @pl.when((j == nj - 1) & (t == nt - 1))
            def _():
                comm.run(2, cin, cout, *sems)

    res = pl.pallas_call(
        body, name="weight_grad", grid=(nj, nt),
        in_specs=[pl.BlockSpec((1, ts, k), (lambda j, t: (j, t, 0)) if na > 1 else (lambda j, t: (0, t, 0))),
                  pl.BlockSpec((1, ts, n), (lambda j, t: (j, t, 0)) if nb_ > 1 else (lambda j, t: (0, t, 0)))]
        + comm.specs(),
        out_specs=(pl.BlockSpec((1, k, n), lambda j, t: (j, 0, 0)), *comm.specs()),
        out_shape=(jax.ShapeDtypeStruct((nj, k, n), F32), *comm.outs),
        input_output_aliases=comm.aliases(2, 1),
        scratch_shapes=comm.scratch() if nc else [],
        compiler_params=_cparams(("arbitrary", "arbitrary") if nc else ("parallel", "arbitrary")),
    )(a, b, *comm.arrays)
    return res[0], comm.split(res[1:])


def _adamw(w, g, m, v):
    nl, r, c = w.shape
    tr = _row_tile(r, 256)
    c1 = 1.0 / (1.0 - ADAM_B1 ** ADAM_STEP)
    c2 = 1.0 / (1.0 - ADAM_B2 ** ADAM_STEP)

    def body(w_ref, g_ref, m_ref, v_ref, d_ref, nm_ref, nv_ref):
        g_ = g_ref[...]
        nm = ADAM_B1 * m_ref[...] + (1.0 - ADAM_B1) * g_
        nv = ADAM_B2 * v_ref[...] + (1.0 - ADAM_B2) * (g_ * g_)
        nm_ref[...] = nm
        nv_ref[...] = nv
        d_ref[...] = -ADAM_LR * ((nm * c1) / (jnp.sqrt(nv * c2) + ADAM_EPS) + ADAM_WD * w_ref[...])

    spec = pl.BlockSpec((1, tr, c), lambda l, i: (l, i, 0))
    shp = jax.ShapeDtypeStruct((nl, r, c), F32)
    return pl.pallas_call(
        body, name="adamw", grid=(nl, r // tr), in_specs=[spec] * 4, out_specs=(spec,) * 3, out_shape=(shp,) * 3,
        compiler_params=_cparams(("parallel", "parallel")),
    )(w, g, m, v)


def _add_my_half(g, other, idx):
    _, _, h, c = g.shape
    th = _row_tile(h, 256)

    def body(idx_ref, g_ref, o_ref, out_ref):
        out_ref[...] = (g_ref[:, 0] + o_ref[...]).astype(BF16)

    return pl.pallas_call(
        body, name="add_my_half",
        grid_spec=pltpu.PrefetchScalarGridSpec(
            num_scalar_prefetch=1, grid=(4, h // th),
            in_specs=[pl.BlockSpec((1, 1, th, c), lambda j, i, idx_ref: (j, idx_ref[1], i, 0)),
                      pl.BlockSpec((1, th, c), lambda j, i, idx_ref: (j, i, 0))],
            out_specs=pl.BlockSpec((1, th, c), lambda j, i, idx_ref: (j, i, 0))),
        out_shape=jax.ShapeDtypeStruct(other.shape, BF16),
        compiler_params=_cparams(("parallel", "parallel")),
    )(idx, g, other)


def _sum_for_my_chip(g, other, got, idx):
    _, _, h, c = g.shape
    th = _row_tile(h, 256)

    def body(idx_ref, g_ref, o_ref, q_ref, out_ref):
        out_ref[0] = (((g_ref[0, 0] + o_ref[0]) + q_ref[0].astype(F32)) + q_ref[1].astype(F32)) + q_ref[2].astype(F32)

    return pl.pallas_call(
        body, name="sum_for_my_chip",
        grid_spec=pltpu.PrefetchScalarGridSpec(
            num_scalar_prefetch=1, grid=(h // th,),
            in_specs=[pl.BlockSpec((1, 1, th, c), lambda i, idx_ref: (idx_ref[0], idx_ref[1], i, 0)),
                      pl.BlockSpec((1, th, c), lambda i, idx_ref: (idx_ref[0], i, 0)),
                      pl.BlockSpec((3, th, c), lambda i, idx_ref: (0, i, 0))],
            out_specs=pl.BlockSpec((1, th, c), lambda i, idx_ref: (idx_ref[1], i, 0))),
        out_shape=jax.ShapeDtypeStruct((2, h, c), F32),
        compiler_params=_cparams(("parallel",)),
    )(idx, g, other, got)


class _Reducer:
    def __init__(self, idx):
        self.idx = idx
        self.groups = []

    def add(self, grads):
        group = {"state": 0, "g": [g.reshape(4, 2, g.shape[1] // 2, g.shape[2]) for g in grads]}
        self.groups.append(group)
        return group

    def steps(self):
        ops, owners = [], []
        for gr in self.groups:
            if gr["state"] == 0:
                ops.append(_Swap(gr["g"]))
            elif gr["state"] == 1:
                ops.append(_Exchange(gr["parts"]))
            elif gr["state"] == 2:
                ops.append(_Join(gr["bufs"]))
            else:
                continue
            owners.append(gr)
        return ops, owners

    def absorb(self, owners, results):
        for gr, res in zip(owners, results):
            if gr["state"] == 0:
                gr["other"] = res
                gr["parts"] = [_add_my_half(g, o, self.idx) for g, o in zip(gr["g"], res)]
            elif gr["state"] == 1:
                gr["bufs"] = [_sum_for_my_chip(g, o, q, self.idx) for g, o, q in zip(gr["g"], gr["other"], res)]
            else:
                gr["full"] = [b.reshape(2 * b.shape[1], b.shape[2]) for b in res]
            gr["state"] += 1

    def drain(self):
        while any(gr["state"] < 3 for gr in self.groups):
            ops, owners = self.steps()
            self.absorb(owners, _communicate(ops))


def _all_gather_rows(block):
    m, n = block.shape

    def body(x_ref, out_ref, send_sems, recv_sems):
        x, y, c = lax.axis_index("x"), lax.axis_index("y"), lax.axis_index("c")
        me, sibling = (x, y, c), (x, y, 1 - c)
        chips = [(1 - x, y), (x, 1 - y), (1 - x, 1 - y)]

        def slot(px, py, pc):
            return out_ref.at[4 * px + 2 * py + pc]

        def copy(k, block_of, to, src=None):
            return pltpu.make_async_remote_copy(
                src_ref=slot(*block_of) if src is None else src, dst_ref=slot(*block_of),
                send_sem=send_sems.at[k], recv_sem=recv_sems.at[k], device_id=to, device_id_type=MESH)

        out_ref[4 * x + 2 * y + c] = x_ref[...]
        first = [copy(0, me, sibling, src=x_ref)]
        first += [copy(1 + r, me, (*chip, c), src=x_ref) for r, chip in enumerate(chips)]
        for cp in first:
            cp.start()
        passed = [copy(4 + r, (*chip, c), sibling) for r, chip in enumerate(chips)]
        for r, chip in enumerate(chips):
            copy(1 + r, (*chip, c), me).wait_recv()
            passed[r].start()
        copy(0, sibling, me).wait_recv()
        for r, chip in enumerate(chips):
            copy(4 + r, (*chip, 1 - c), me).wait_recv()
        for cp in first + passed:
            cp.wait_send()

    vm = pl.BlockSpec(memory_space=pltpu.VMEM)
    return pl.pallas_call(
        body, name="all_gather_rows",
        out_shape=jax.ShapeDtypeStruct((N_DEV, m, n), block.dtype),
        in_specs=[vm], out_specs=vm,
        scratch_shapes=[pltpu.SemaphoreType.DMA((7,)), pltpu.SemaphoreType.DMA((7,))],
        compiler_params=_cparams(),
    )(block)


def _sum_devices(gathered):
    nd, m, n = gathered.shape

    def body(g_ref, o_ref):
        acc = g_ref[0]
        for b in range(1, nd):
            acc = acc + g_ref[b]
        o_ref[...] = acc

    return pl.pallas_call(
        body, name="sum_devices", out_shape=jax.ShapeDtypeStruct((m, n), F32),
        in_specs=[pl.BlockSpec(memory_space=pltpu.VMEM)], out_specs=pl.BlockSpec(memory_space=pltpu.VMEM),
        compiler_params=_cparams(),
    )(gathered)


def _ada_weight_grad(c_all, dmod_cols):
    nl, nd, ncol = dmod_cols.shape
    d = c_all.shape[1]

    def body(c_ref, dm_ref, o_ref):
        o_ref[0] = lax.dot_general(c_ref[...], dm_ref[0], (((0,), (0,)), ((), ())),
                                   preferred_element_type=F32, precision=lax.Precision.HIGHEST)

    return pl.pallas_call(
        body, name="ada_weight_grad", grid=(nl,),
        in_specs=[pl.BlockSpec((nd, d), lambda l: (0, 0)), pl.BlockSpec((1, nd, ncol), lambda l: (l, 0, 0))],
        out_specs=pl.BlockSpec((1, d, ncol), lambda l: (l, 0, 0)),
        out_shape=jax.ShapeDtypeStruct((nl, d, ncol), F32), compiler_params=_cparams(("parallel",)),
    )(c_all, dmod_cols)


def _pad_rows(a, rows):
    return jnp.pad(a, ((0, rows - a.shape[0]), (0, 0)))


def _shard_cols(full, chip, width):
    return lax.dynamic_slice_in_dim(full, chip * width, width, axis=full.ndim - 1)


def kernel(x, c, ada_w, ada_b, pre_g, post_g, pool_w, pool_scale, cv_w_pw1, cv_b_pw1, cv_w_dw, cv_b_dw, cv_ln_g, cv_ln_b, cv_w_pw2, cv_b_pw2, ffn_w_up, ffn_w_dw, ffn_w_down, loss_target, m_ada_w, m_ada_b, m_pre_g, m_post_g, m_pool_w, m_pool_scale, m_cv_w_pw1, m_cv_b_pw1, m_cv_w_dw, m_cv_b_dw, m_cv_ln_g, m_cv_ln_b, m_cv_w_pw2, m_cv_b_pw2, m_ffn_w_up, m_ffn_w_dw, m_ffn_w_down, v_ada_w, v_ada_b, v_pre_g, v_post_g, v_pool_w, v_pool_scale, v_cv_w_pw1, v_cv_b_pw1, v_cv_w_dw, v_cv_b_dw, v_cv_ln_g, v_cv_ln_b, v_cv_w_pw2, v_cv_b_pw2, v_ffn_w_up, v_ffn_w_dw, v_ffn_w_down):
    s, d = x.shape[1], x.shape[2]
    dq = d // N_CHIPS
    n_g = pool_w.shape[1]
    gq = pool_w.shape[2]
    gd = pool_w.shape[3]
    kw = cv_w_dw.shape[1]
    cs = ffn_w_up.shape[2]
    fq = ffn_w_down.shape[1]
    chip = 2 * lax.axis_index("x") + lax.axis_index("y")
    core = lax.axis_index("c")
    chip1 = jnp.reshape(chip, (1,)).astype(jnp.int32)
    core1 = jnp.reshape(core, (1,)).astype(jnp.int32)
    xs, tgt = x[0], loss_target[0]

    c_rep, mod_rep = _ada_forward(c, ada_w)
    c_all = c_rep[:, 0, :]
    mod = mod_rep[:, :, 0, :].transpose(1, 0, 2).reshape(ada_b.shape) + ada_b

    small_rows = [pre_g.reshape(4, dq), post_g.reshape(4, dq), cv_w_dw[0], cv_b_dw, cv_ln_g, cv_ln_b, cv_b_pw2,
                  cv_b_pw1.reshape(2, dq)]
    small = jnp.concatenate(small_rows, axis=0)
    n_small = small.shape[0]
    small = _pad_rows(small, -(-n_small // 16) * 16)
    dwf = _pad_rows(ffn_w_dw.reshape(6, cs), 16)
    def slot(a):
        return lax.dynamic_update_slice_in_dim(jnp.zeros((N_CHIPS,) + a.shape, a.dtype), a[None], chip, axis=0)

    first = [_cast_into_slot(pool_w.reshape(n_g * gq, gd), chip1), _cast_into_slot(ffn_w_up[0], chip1),
             _cast_into_slot(ffn_w_down[0], chip1), slot(small), slot(dwf)]
    (g_pool, g_up0, g_dn0, g_small, g_dwf), = _communicate([_AllGather(first)])
    later = _AllGather([_cast_into_slot(cv_w_pw1[0], chip1), _cast_into_slot(cv_w_pw2[0], chip1),
                        _cast_into_slot(ffn_w_up[1], chip1), _cast_into_slot(ffn_w_down[1], chip1)])
    poolw_full = g_pool.reshape(N_CHIPS, n_g, gq, gd).transpose(1, 0, 2, 3).reshape(n_g, gd, gd)
    smallf = g_small.transpose(1, 0, 2).reshape(g_small.shape[1], d)
    pre_full, post_full = smallf[0:4].reshape(2, 2, d), smallf[4:8].reshape(2, 2, d)
    wdw31 = smallf[8:8 + kw]
    o = 8 + kw
    b_dw, ln_g, ln_b, b_pw2 = smallf[o:o + 1], smallf[o + 1:o + 2], smallf[o + 2:o + 3], smallf[o + 3:o + 4]
    b_pw1 = g_small[:, o + 4:o + 6, :].reshape(1, 2 * d)
    ffn_dw = g_dwf[:, :6, :].transpose(1, 0, 2).reshape(2, 3, N_CHIPS * cs)

    def sub_vec(layer, sub, extra=None):
        m6 = mod[layer].reshape(6, d)
        rows = [pre_full[layer, sub][None], 1.0 + m6[3 * sub + 1][None], m6[3 * sub][None], m6[3 * sub + 2][None],
                post_full[layer, sub][None]]
        if extra is not None:
            rows.append(extra)
        return _pad_rows(jnp.concatenate(rows, axis=0), 8)

    vec_pool = sub_vec(0, 0, pool_scale)
    vec_f0, vec_conv, vec_f1 = sub_vec(0, 1), sub_vec(1, 0), sub_vec(1, 1)
    cvec = _pad_rows(jnp.concatenate([b_dw, ln_g, ln_b, b_pw2], axis=0), 8)

    x1 = _pool_forward(xs, vec_pool, poolw_full)
    w_up0, w_dn0 = g_up0, g_dn0.reshape(2, 2 * fq, d)
    (x2, h_f0, a0_f0, y_f0), ((g_pw1, g_pw2, g_up1, g_dn1),) = _ffn_forward(
        x1, vec_f0, w_up0, ffn_dw[0], w_dn0, _Comm([later]))
    pw2_full = g_pw2.reshape(d, d)
    w_up1, w_dn1 = g_up1, g_dn1.reshape(2, 2 * fq, d)
    x3, h_cv, a_cv, uc_cv, z_cv, y_cv = _conv_forward(x2, vec_conv, cvec, g_pw1, b_pw1, wdw31, pw2_full)
    (x4, h_f1, a0_f1, y_f1), _ = _ffn_forward(x3, vec_f1, w_up1, ffn_dw[1], w_dn1)
    dx4, loss_rows = _loss_head(x4, tgt)

    dx3, da0_f1, u_f1, dy_f1, sum_f1, dwdw_f1 = _ffn_backward(dx4, x3, y_f1, a0_f1, vec_f1, w_up1, ffn_dw[1], w_dn1)
    dx2, da_cv, dy_cv, sum_cv, dwdw_cv = _conv_backward(dx3, x2, y_cv, a_cv, uc_cv, vec_conv, cvec, g_pw1, wdw31, pw2_full)
    dx1, da0_f0, u_f0, dy_f0, sum_f0, dwdw_f0 = _ffn_backward(dx2, x1, y_f0, a0_f0, vec_f0, w_up0, ffn_dw[0], w_dn0)
    dx0, sum_pool, gw_pool = _pool_backward(dx1, xs, vec_pool, poolw_full)
    gw_pool4 = gw_pool.reshape(n_g, N_CHIPS, gq, gd).transpose(1, 0, 2, 3).reshape(N_CHIPS, n_g * gq, gd)

    red = _Reducer(jnp.concatenate([chip1, core1]))

    def grad_stage(a, b, shape=None):
        ops, owners = red.steps()
        gw, results = _weight_grad(a, b, _Comm(ops))
        red.absorb(owners, results)
        return gw if shape is None else gw.reshape(shape)

    r_up1 = red.add([grad_stage(h_f1[None], da0_f1)])
    r_up0 = red.add([grad_stage(h_f0[None], da0_f0)])
    r_dn1 = red.add([grad_stage(u_f1, dy_f1[None], (N_CHIPS, fq, d))])
    r_dn0 = red.add([grad_stage(u_f0, dy_f0[None], (N_CHIPS, fq, d))])
    r_pw1 = red.add([grad_stage(h_cv[None], da_cv)])
    r_last = red.add([grad_stage(z_cv[None], dy_cv[None], (N_CHIPS, dq, d)), gw_pool4])
    red.drain()
    r_up1, r_up0, r_dn1, r_dn0, r_pw1 = [gr["full"][0] for gr in (r_up1, r_up0, r_dn1, r_dn0, r_pw1)]
    r_pw2, r_pool = r_last["full"]

    slab = jnp.concatenate([sum_f1, sum_cv, dwdw_cv, sum_f0, sum_pool, loss_rows], axis=0)
    wide = jnp.concatenate([dwdw_f1, dwdw_f0], axis=0)
    slab_all = _all_gather_rows(slab)
    wide_all = _all_gather_rows(wide)
    tot = _sum_devices(slab_all)
    tot_wide = _sum_devices(wide_all)
    kpad = dwdw_cv.shape[0]
    o_cv, o_dw, o_f0 = 8, 24, 24 + kpad
    o_pool, o_loss = o_f0 + 8, o_f0 + 16
    loss = jnp.sum(tot[o_loss])
    dmod_l0 = jnp.concatenate([slab_all[:, o_pool + 4], slab_all[:, o_pool + 3], slab_all[:, o_pool + 1],
                               slab_all[:, o_f0 + 4], slab_all[:, o_f0 + 3], slab_all[:, o_f0 + 1]], axis=-1)
    dmod_l1 = jnp.concatenate([slab_all[:, o_cv + 4], slab_all[:, o_cv + 3], slab_all[:, o_cv + 1],
                               slab_all[:, 4], slab_all[:, 3], slab_all[:, 1]], axis=-1)
    dmod = jnp.stack([dmod_l0, dmod_l1], axis=0)
    g_ada_b = _sum_devices(dmod.transpose(1, 0, 2))
    ncol = ada_w.shape[2]
    g_ada_w = _ada_weight_grad(c_all, _shard_cols(dmod, chip, ncol))

    g_pre = jnp.stack([jnp.stack([tot[o_pool + 2], tot[o_f0 + 2]]), jnp.stack([tot[o_cv + 2], tot[2]])])
    g_post = jnp.stack([jnp.stack([tot[o_pool + 0], tot[o_f0 + 0]]), jnp.stack([tot[o_cv + 0], tot[0]])])
    g_pool_scale = tot[o_pool + 5][None]
    g_b_pw2, g_ln_g, g_ln_b, g_b_dw = tot[o_cv + 5], tot[o_cv + 6], tot[o_cv + 7], tot[o_cv + 8]
    g_b_pw1 = jnp.concatenate([tot[o_cv + 9], tot[o_cv + 10]])
    g_w_dw31 = tot[o_dw:o_dw + kw]
    g_ffn_dw = jnp.stack([tot_wide[8:11], tot_wide[0:3]])

    grads_small = {
        "pre_g": _shard_cols(g_pre, chip, dq), "post_g": _shard_cols(g_post, chip, dq),
        "pool_scale": g_pool_scale, "cv_b_pw1": _shard_cols(g_b_pw1[None], chip, 2 * dq),
        "cv_w_dw": _shard_cols(g_w_dw31[None], chip, dq), "cv_b_dw": _shard_cols(g_b_dw[None], chip, dq),
        "cv_ln_g": _shard_cols(g_ln_g[None], chip, dq), "cv_ln_b": _shard_cols(g_ln_b[None], chip, dq),
        "cv_b_pw2": _shard_cols(g_b_pw2[None], chip, dq), "ffn_w_dw": _shard_cols(g_ffn_dw, chip, cs),
        "ada_b": g_ada_b,
    }
    params_small = {
        "pre_g": (pre_g, m_pre_g, v_pre_g), "post_g": (post_g, m_post_g, v_post_g),
        "pool_scale": (pool_scale, m_pool_scale, v_pool_scale), "cv_b_pw1": (cv_b_pw1, m_cv_b_pw1, v_cv_b_pw1),
        "cv_w_dw": (cv_w_dw, m_cv_w_dw, v_cv_w_dw), "cv_b_dw": (cv_b_dw, m_cv_b_dw, v_cv_b_dw),
        "cv_ln_g": (cv_ln_g, m_cv_ln_g, v_cv_ln_g), "cv_ln_b": (cv_ln_b, m_cv_ln_b, v_cv_ln_b),
        "cv_b_pw2": (cv_b_pw2, m_cv_b_pw2, v_cv_b_pw2), "ffn_w_dw": (ffn_w_dw, m_ffn_w_dw, v_ffn_w_dw),
        "ada_b": (ada_b, m_ada_b, v_ada_b),
    }
    names = list(params_small)
    sizes = [params_small[nm][0].size for nm in names]
    padded = [-(-sz // 1024) * 1024 for sz in sizes]

    def pack(arrs):
        flat = [jnp.pad(a.reshape(-1), (0, p - a.size)) for a, p in zip(arrs, padded)]
        return jnp.concatenate(flat).reshape(1, -1, 128)

    pk_w = pack([params_small[nm][0] for nm in names])
    pk_m = pack([params_small[nm][1] for nm in names])
    pk_v = pack([params_small[nm][2] for nm in names])
    pk_g = pack([grads_small[nm].reshape(params_small[nm][0].shape) for nm in names])
    pk_d, pk_nm, pk_nv = _adamw(pk_w, pk_g, pk_m, pk_v)

    def unpack(pk):
        flat, out, off = pk.reshape(-1), {}, 0
        for nm, sz, p in zip(names, sizes, padded):
            out[nm] = flat[off:off + sz].reshape(params_small[nm][0].shape)
            off += p
        return out

    small_d, small_m, small_v = unpack(pk_d), unpack(pk_nm), unpack(pk_nv)
    small_g = {nm: grads_small[nm].reshape(params_small[nm][0].shape) for nm in names}

    big_g = {
        "ada_w": g_ada_w,
        "pool_w": r_pool.reshape(pool_w.shape),
        "cv_w_pw1": r_pw1[None], "cv_w_pw2": r_pw2[None],
        "ffn_w_up": jnp.stack([r_up0, r_up1]), "ffn_w_down": jnp.stack([r_dn0, r_dn1]),
    }
    big_p = {
        "ada_w": (ada_w, m_ada_w, v_ada_w), "pool_w": (pool_w, m_pool_w, v_pool_w),
        "cv_w_pw1": (cv_w_pw1, m_cv_w_pw1, v_cv_w_pw1), "cv_w_pw2": (cv_w_pw2, m_cv_w_pw2, v_cv_w_pw2),
        "ffn_w_up": (ffn_w_up, m_ffn_w_up, v_ffn_w_up), "ffn_w_down": (ffn_w_down, m_ffn_w_down, v_ffn_w_down),
    }
    big_d, big_m, big_v = {}, {}, {}
    for nm, (w, m, v) in big_p.items():
        shp = w.shape
        as3 = lambda t: t.reshape((-1,) + shp[-2:])
        dl, nm_, nv_ = _adamw(as3(w), as3(big_g[nm]), as3(m), as3(v))
        big_d[nm], big_m[nm], big_v[nm] = dl.reshape(shp), nm_.reshape(shp), nv_.reshape(shp)

    order = ["ada_w", "ada_b", "pre_g", "post_g", "pool_w", "pool_scale", "cv_w_pw1", "cv_b_pw1", "cv_w_dw", "cv_b_dw",
             "cv_ln_g", "cv_ln_b", "cv_w_pw2", "cv_b_pw2", "ffn_w_up", "ffn_w_dw", "ffn_w_down"]
    pick = lambda bigs, smalls: [bigs[nm] if nm in bigs else smalls[nm] for nm in order]
    return (loss, dx0[None], *pick(big_g, small_g), *pick(big_d, small_d), *pick(big_m, small_m),
            *pick(big_v, small_v))
```

```python
import functools

import jax
import jax.numpy as jnp
from jax import lax
from jax.experimental import pallas as pl
from jax.experimental.pallas import tpu as pltpu

F32 = jnp.float32
BF16 = jnp.bfloat16
EPS = 1e-6
N_CHIPS = 4
N_DEV = 8
POOL_WINDOWS = (2, 4, 8, 16)
POOL_HALO = 16
FFN_HALO = 16
ADAM_LR = 0.001
ADAM_B1 = 0.9
ADAM_B2 = 0.999
ADAM_EPS = 1e-08
ADAM_WD = 0.01
ADAM_STEP = 10
V7X_VMEM_LIMIT = 58 * 1024 * 1024
MESH = pl.DeviceIdType.MESH


def _cparams(sem=None, vmem=V7X_VMEM_LIMIT):
    return pltpu.CompilerParams(dimension_semantics=sem, vmem_limit_bytes=vmem)


def _row_tile(n, want):
    if n <= want:
        return n
    t = want - want % 8
    while n % t:
        t -= 8
    return t


def _lane_chunks(width):
    out, c = [], 0
    while c < width:
        w = min(512, width - c)
        out.append((c, w))
        c += w
    return out


def _dot(a, b):
    return jnp.dot(a, b, preferred_element_type=F32)


def _dot_nt(a, b):
    return lax.dot_general(a, b, (((1,), (1,)), ((), ())), preferred_element_type=F32)


def _dot_tn(a, b):
    return lax.dot_general(a, b, (((0,), (0,)), ((), ())), preferred_element_type=F32)


def _rms(x):
    r = lax.rsqrt(jnp.mean(x * x, axis=-1, keepdims=True) + EPS)
    return x * r, r


def _rms_bwd(dyn, yn, r):
    return r * (dyn - yn * jnp.mean(dyn * yn, axis=-1, keepdims=True))


def _sigmoid(x):
    return 0.5 * jnp.tanh(0.5 * x) + 0.5


def _colsum(x):
    return jnp.sum(x, axis=0, keepdims=True)


def _shift_down(x, k):
    return x if k == 0 else pltpu.roll(x, k, 0)


def _shift_up(x, k):
    return x if k == 0 else pltpu.roll(x, x.shape[0] - k, 0)


def _vec_rows(vec):
    return vec[0:1] * vec[1:2], vec[2:3], vec[3:4], vec[4:5]


def _add_rows(sum_ref, rows):
    for k, r in enumerate(rows):
        sum_ref[k:k + 1, :] += r


def _ada_forward(c, ada_w):
    n_layers, d, ncol = ada_w.shape

    def body(c_ref, w_ref, call_ref, mod_ref, part_ref, sendbuf, send_sems, recv_sems, send2, recv2):
        x, y, cc = lax.axis_index("x"), lax.axis_index("y"), lax.axis_index("c")
        me = 4 * x + 2 * y + cc
        rel = [(x, y, 1 - cc), (1 - x, y, cc), (x, 1 - y, cc), (1 - x, 1 - y, cc),
               (1 - x, y, 1 - cc), (x, 1 - y, 1 - cc), (1 - x, 1 - y, 1 - cc)]
        cv = c_ref[...]
        call_ref[me] = jnp.broadcast_to(cv * _sigmoid(cv), (8, d))

        def gather(k, block, to):
            blk = call_ref.at[block]
            return pltpu.make_async_remote_copy(src_ref=blk, dst_ref=blk, send_sem=send_sems.at[k],
                                                recv_sem=recv_sems.at[k], device_id=to, device_id_type=MESH)

        for k, to in enumerate(rel):
            gather(k, me, to).start()
        for k, (px, py, pc) in enumerate(rel):
            gather(k, 4 * px + 2 * py + pc, rel[k]).wait_recv()
        for k, to in enumerate(rel):
            gather(k, me, to).wait_send()

        ca = call_ref[...].reshape(8 * N_DEV, d)
        for l in range(n_layers):
            part_ref[l] = jnp.dot(ca, w_ref[l], preferred_element_type=F32, precision=lax.Precision.HIGHEST)

        j = 2 * x + y
        chips = [(1 - x, y), (x, 1 - y), (1 - x, 1 - y)]

        def rows_of(b):
            return part_ref[:, pl.ds(pl.multiple_of(8 * b, 8), 8), :]

        def scatter(k, src_j, to):
            return pltpu.make_async_remote_copy(
                src_ref=sendbuf.at[k], dst_ref=mod_ref.at[src_j], send_sem=send2.at[k], recv_sem=recv2.at[k],
                device_id=to, device_id_type=MESH)

        mod_ref[j] = rows_of(me)
        for k, (px, py) in enumerate(chips):
            sendbuf[k] = rows_of(4 * px + 2 * py + cc)
            scatter(k, j, (px, py, cc)).start()
        for k, (px, py) in enumerate(chips):
            scatter(k, 2 * px + py, (px, py, cc)).wait_recv()
        for k, (px, py) in enumerate(chips):
            scatter(k, j, (px, py, cc)).wait_send()

    vm = pl.BlockSpec(memory_space=pltpu.VMEM)
    return pl.pallas_call(
        body, name="ada_forward",
        out_shape=(jax.ShapeDtypeStruct((N_DEV, 8, d), F32), jax.ShapeDtypeStruct((N_CHIPS, n_layers, 8, ncol), F32)),
        in_specs=[vm, vm], out_specs=(vm, vm),
        scratch_shapes=[pltpu.VMEM((n_layers, 8 * N_DEV, ncol), F32), pltpu.VMEM((3, n_layers, 8, ncol), F32),
                        pltpu.SemaphoreType.DMA((7,)), pltpu.SemaphoreType.DMA((7,)),
                        pltpu.SemaphoreType.DMA((3,)), pltpu.SemaphoreType.DMA((3,))],
        compiler_params=_cparams(),
    )(c, ada_w)


def _cast_into_slot(w2d, chip):
    r, c = w2d.shape
    tr = _row_tile(r, 256)

    def body(chip_ref, w_ref, o_ref):
        o_ref[0] = w_ref[...].astype(BF16)

    return pl.pallas_call(
        body, name="cast_into_slot",
        grid_spec=pltpu.PrefetchScalarGridSpec(
            num_scalar_prefetch=1, grid=(r // tr,),
            in_specs=[pl.BlockSpec((tr, c), lambda i, chip_ref: (i, 0))],
            out_specs=pl.BlockSpec((1, tr, c), lambda i, chip_ref: (chip_ref[0], i, 0))),
        out_shape=jax.ShapeDtypeStruct((N_CHIPS, r, c), BF16), compiler_params=_cparams(("parallel",)),
    )(chip, w2d)


def _place():
    x, y, c = lax.axis_index("x"), lax.axis_index("y"), lax.axis_index("c")
    return x, y, c, [(1 - x, y), (x, 1 - y), (1 - x, 1 - y)]


def _remote(src, dst, send_sem, recv_sem, to):
    return pltpu.make_async_remote_copy(src_ref=src, dst_ref=dst, send_sem=send_sem, recv_sem=recv_sem,
                                        device_id=to, device_id_type=MESH)


class _AllGather:
    def __init__(self, bufs):
        self.arrays = list(bufs)
        self.outs = [jax.ShapeDtypeStruct(b.shape, b.dtype) for b in bufs]
        self.aliased = True
        self.n_sems = 6 * len(bufs)

    def run(self, phase, ins, outs, send_sems, recv_sems, base):
        x, y, c, chips = _place()
        j = 2 * x + y
        for k, buf in enumerate(outs):
            half = buf.shape[1] // 2

            def part(src_j, h):
                return buf.at[src_j, pl.ds(h * half, half), :]

            def ici(r, src_j, to):
                s = base + 6 * k + r
                return _remote(part(src_j, c), part(src_j, c), send_sems.at[s], recv_sems.at[s], to)

            def d2d(r, src_j, h):
                s = base + 6 * k + 3 + r
                return _remote(part(src_j, h), part(src_j, h), send_sems.at[s], recv_sems.at[s], (x, y, 1 - c))

            for r, (px, py) in enumerate(chips):
                if phase == 0:
                    ici(r, j, (px, py, c)).start()
                elif phase == 1:
                    ici(r, 2 * px + py, (px, py, c)).wait_recv()
                    d2d(r, 2 * px + py, c).start()
                else:
                    d2d(r, 2 * px + py, 1 - c).wait_recv()
                    ici(r, j, (px, py, c)).wait_send()
                    d2d(r, 2 * px + py, c).wait_send()


class _Swap:
    def __init__(self, grads):
        self.arrays = list(grads)
        self.outs = [jax.ShapeDtypeStruct((g.shape[0],) + g.shape[2:], g.dtype) for g in grads]
        self.aliased = False
        self.n_sems = len(grads)

    def run(self, phase, ins, outs, send_sems, recv_sems, base):
        x, y, c, _ = _place()
        for k in range(len(ins)):
            cp = _remote(ins[k].at[:, 1 - c], outs[k], send_sems.at[base + k], recv_sems.at[base + k], (x, y, 1 - c))
            if phase == 0:
                cp.start()
            elif phase == 2:
                cp.wait()


class _Exchange:
    def __init__(self, parts):
        self.arrays = list(parts)
        self.outs = [jax.ShapeDtypeStruct((3,) + p.shape[1:], p.dtype) for p in parts]
        self.aliased = False
        self.n_sems = 3 * len(parts)

    def run(self, phase, ins, outs, send_sems, recv_sems, base):
        x, y, c, chips = _place()
        for k in range(len(ins)):
            for r, (px, py) in enumerate(chips):
                s = base + 3 * k + r
                cp = _remote(ins[k].at[2 * px + py], outs[k].at[r], send_sems.at[s], recv_sems.at[s], (px, py, c))
                if phase == 0:
                    cp.start()
                elif phase == 2:
                    cp.wait()


class _Join:
    def __init__(self, bufs):
        self.arrays = list(bufs)
        self.outs = [jax.ShapeDtypeStruct(b.shape, b.dtype) for b in bufs]
        self.aliased = True
        self.n_sems = len(bufs)

    def run(self, phase, ins, outs, send_sems, recv_sems, base):
        x, y, c, _ = _place()
        for k, buf in enumerate(outs):
            mine = _remote(buf.at[c], buf.at[c], send_sems.at[base + k], recv_sems.at[base + k], (x, y, 1 - c))
            if phase == 0:
                mine.start()
            elif phase == 2:
                mine.wait_send()
                _remote(buf.at[1 - c], buf.at[1 - c], send_sems.at[base + k], recv_sems.at[base + k],
                        (x, y, 1 - c)).wait_recv()


class _Comm:
    def __init__(self, ops):
        self.ops = list(ops)
        self.arrays = [a for op in self.ops for a in op.arrays]
        self.outs = [o for op in self.ops for o in op.outs]
        self.n_sems = sum(op.n_sems for op in self.ops)

    def specs(self):
        return [pl.BlockSpec(memory_space=pl.ANY)] * len(self.arrays)

    def aliases(self, first_in, first_out):
        out, k = {}, 0
        for op in self.ops:
            for i in range(len(op.arrays)):
                if op.aliased:
                    out[first_in + k + i] = first_out + k + i
            k += len(op.arrays)
        return out

    def scratch(self):
        return [pltpu.SemaphoreType.DMA((self.n_sems,)), pltpu.SemaphoreType.DMA((self.n_sems,))]

    def run(self, phase, ins, outs, send_sems, recv_sems):
        k = base = 0
        for op in self.ops:
            n = len(op.arrays)
            op.run(phase, ins[k:k + n], outs[k:k + n], send_sems, recv_sems, base)
            k += n
            base += op.n_sems

    def split(self, results):
        out, k = [], 0
        for op in self.ops:
            out.append(list(results[k:k + len(op.arrays)]))
            k += len(op.arrays)
        return out


def _communicate(ops):
    comm = _Comm(ops)
    n = len(comm.arrays)

    def body(*refs):
        ins, outs, (send_sems, recv_sems) = refs[:n], refs[n:2 * n], refs[2 * n:]
        for phase in range(3):
            comm.run(phase, ins, outs, send_sems, recv_sems)

    res = pl.pallas_call(
        body, name="communicate", out_shape=tuple(comm.outs), in_specs=comm.specs(), out_specs=tuple(comm.specs()),
        input_output_aliases=comm.aliases(0, 0), scratch_shapes=comm.scratch(),
    )(*comm.arrays)
    return comm.split(res)


def _pool_core(he, w_ref, scale, first_row, halo, n_rows):
    d = he.shape[1]
    gd = d // len(POOL_WINDOWS)
    t = first_row + lax.broadcasted_iota(jnp.int32, (n_rows, 1), 0)
    pooled, ypre, cnts = [], [], []
    for g, w in enumerate(POOL_WINDOWS):
        hg = he[:, g * gd:(g + 1) * gd]
        s, k = hg, 1
        while k < w:
            s = s + _shift_down(s, k)
            k *= 2
        cnt = jnp.minimum(t + 1, w).astype(F32)
        p = s[halo:] / cnt - hg[halo:]
        pooled.append(p.astype(BF16))
        cnts.append(cnt)
        ypre.append(_dot(pooled[-1], w_ref[g]))
    return pooled, jnp.concatenate(ypre, axis=1), cnts


def _pool_forward(x, vec, pool_w):
    s, d = x.shape
    ts = _row_tile(s, 512)
    n_g, gd, _ = pool_w.shape

    def body(x_ref, vec_ref, w_ref, o_ref, carry):
        i = pl.program_id(0)

        @pl.when(i == 0)
        def _():
            carry[...] = jnp.zeros_like(carry)

        vec = vec_ref[...]
        a, sh, gt, gpost = _vec_rows(vec)
        xb = x_ref[...]
        xn, _ = _rms(xb)
        h = xn * a + sh
        he = jnp.concatenate([carry[...], h], axis=0)
        carry[...] = h[ts - POOL_HALO:]
        _, ypre, _ = _pool_core(he, w_ref, vec[5:6], i * ts, POOL_HALO, ts)
        yn, _ = _rms(ypre * vec[5:6])
        o_ref[...] = xb + gt * (yn * gpost)

    return pl.pallas_call(
        body, name="pool_forward", grid=(s // ts,),
        in_specs=[pl.BlockSpec((ts, d), lambda i: (i, 0)), pl.BlockSpec((8, d), lambda i: (0, 0)),
                  pl.BlockSpec((n_g, gd, gd), lambda i: (0, 0, 0))],
        out_specs=pl.BlockSpec((ts, d), lambda i: (i, 0)),
        out_shape=jax.ShapeDtypeStruct((s, d), F32),
        scratch_shapes=[pltpu.VMEM((POOL_HALO, d), F32)],
        compiler_params=_cparams(("arbitrary",)),
    )(x, vec, pool_w)


def _pool_backward(dout, x, vec, pool_w):
    s, d = x.shape
    ts = _row_tile(s, 512)
    nb = s // ts
    hb = ts // POOL_HALO
    n_g, gd, _ = pool_w.shape

    def body(do_ref, x_ref, xh_ref, vec_ref, w_ref, dx_ref, sum_ref, dw_ref, carry):
        step = pl.program_id(0)
        i = nb - 1 - step

        @pl.when(step == 0)
        def _():
            carry[...] = jnp.zeros_like(carry)
            sum_ref[...] = jnp.zeros_like(sum_ref)
            dw_ref[...] = jnp.zeros_like(dw_ref)

        vec = vec_ref[...]
        a, sh, gt, gpost = _vec_rows(vec)
        scale = vec[5:6]
        do = do_ref[...]
        xe = jnp.concatenate([xh_ref[...], x_ref[...]], axis=0)
        xne, re = _rms(xe)
        he = xne * a + sh
        rowid = lax.broadcasted_iota(jnp.int32, (POOL_HALO + ts, 1), 0)
        he = jnp.where((rowid >= POOL_HALO) | (i > 0), he, 0.0)
        xn, r = xne[POOL_HALO:], re[POOL_HALO:]
        pooled, ypre, cnts = _pool_core(he, w_ref, scale, i * ts, POOL_HALO, ts)
        yn, ry = _rms(ypre * scale)
        dyn = do * (gt * gpost)
        dy = _rms_bwd(dyn, yn, ry)
        dypre = (dy * scale).astype(BF16)
        dh_parts, q_parts = [], []
        for g, w in enumerate(POOL_WINDOWS):
            dyg = dypre[:, g * gd:(g + 1) * gd]
            dpool = _dot_nt(dyg, w_ref[g])
            dw_ref[g] += _dot_tn(pooled[g], dyg)
            q = dpool / cnts[g]
            qe = jnp.concatenate([q, carry[:, g * gd:(g + 1) * gd]], axis=0)
            acc, k = qe, 1
            while k < w:
                acc = acc + _shift_up(acc, k)
                k *= 2
            dh_parts.append(acc[:ts] - dpool)
            q_parts.append(q[:POOL_HALO])
        carry[...] = jnp.concatenate(q_parts, axis=1)
        dh = jnp.concatenate(dh_parts, axis=1)
        dxn = dh * a
        dx_ref[...] = do + _rms_bwd(dxn, xn, r)
        _add_rows(sum_ref, [_colsum(do * gt * yn), _colsum(do * yn * gpost), _colsum(dh * xn * vec[1:2]),
                            _colsum(dh * xn * vec[0:1]), _colsum(dh), _colsum(dy * ypre)])

    blk = lambda st: (nb - 1 - st, 0)
    return pl.pallas_call(
        body, name="pool_backward", grid=(nb,),
        in_specs=[pl.BlockSpec((ts, d), blk), pl.BlockSpec((ts, d), blk),
                  pl.BlockSpec((POOL_HALO, d), lambda st: (jnp.maximum((nb - 1 - st) * hb - 1, 0), 0)),
                  pl.BlockSpec((8, d), lambda st: (0, 0)), pl.BlockSpec((n_g, gd, gd), lambda st: (0, 0, 0))],
        out_specs=(pl.BlockSpec((ts, d), blk), pl.BlockSpec((8, d), lambda st: (0, 0)),
                   pl.BlockSpec((n_g, gd, gd), lambda st: (0, 0, 0))),
        out_shape=(jax.ShapeDtypeStruct((s, d), F32), jax.ShapeDtypeStruct((8, d), F32),
                   jax.ShapeDtypeStruct((n_g, gd, gd), F32)),
        scratch_shapes=[pltpu.VMEM((POOL_HALO, d), F32)],
        compiler_params=_cparams(("arbitrary",)),
    )(dout, x, x, vec, pool_w)


def _ffn_forward(x, vec, w_up, w_dw, w_down, comm=None):
    s, d = x.shape
    _, _, cs = w_up.shape
    ts = _row_tile(s, 256)
    nb = s // ts
    chunks = _lane_chunks(cs)
    comm = comm or _Comm([])
    nc = len(comm.arrays)

    def body(*refs):
        x_ref, vec_ref, wup_ref, wdw_ref, wdn_ref = refs[:5]
        cin = refs[5:5 + nc]
        o_ref, h_ref, a0_ref, cc_ref, u_ref, y_ref = refs[5 + nc:11 + nc]
        cout = refs[11 + nc:11 + 2 * nc]
        carry = refs[11 + 2 * nc]
        sems = refs[12 + 2 * nc:]
        i = pl.program_id(0)

        @pl.when(i == 0)
        def _():
            carry[...] = jnp.zeros_like(carry)
            if nc:
                comm.run(0, cin, cout, *sems)

        if nc:
            @pl.when(i == (3 * nb) // 4)
            def _():
                comm.run(1, cin, cout, *sems)

        vec = vec_ref[...]
        a, sh, gt, gpost = _vec_rows(vec)
        xb = x_ref[...]
        xn, _ = _rms(xb)
        hb = (xn * a + sh).astype(BF16)
        h_ref[...] = hb
        for q in range(2):
            for c0, cw in chunks:
                conv = []
                for j in (q, q + 2):
                    a0 = _dot(hb, wup_ref[j, :, c0:c0 + cw])
                    a0_ref[j, :, c0:c0 + cw] = a0.astype(BF16)
                    ae = jnp.concatenate([carry[j, :, c0:c0 + cw], a0], axis=0)
                    carry[j, :, c0:c0 + cw] = a0[ts - FFN_HALO:]
                    w = wdw_ref[:, j * cs + c0:j * cs + c0 + cw]
                    conv.append((w[2:3] * ae + w[1:2] * _shift_down(ae, 1) + w[0:1] * _shift_down(ae, 2))[FFN_HALO:])
                    cc_ref[j, :, c0:c0 + cw] = conv[-1].astype(BF16)
                u_ref[q, :, c0:c0 + cw] = (conv[0] * _sigmoid(conv[0]) * conv[1]).astype(BF16)
        y = _dot(u_ref[0], wdn_ref[0]) + _dot(u_ref[1], wdn_ref[1])
        y_ref[...] = y
        yn, _ = _rms(y)
        o_ref[...] = xb + gt * (yn * gpost)
        if nc:
            @pl.when(i == nb - 1)
            def _():
                comm.run(2, cin, cout, *sems)

    const3 = lambda i: (0, 0, 0)
    res = pl.pallas_call(
        body, name="ffn_forward", grid=(nb,),
        in_specs=[pl.BlockSpec((ts, d), lambda i: (i, 0)), pl.BlockSpec((8, d), lambda i: (0, 0)),
                  pl.BlockSpec(w_up.shape, const3, pipeline_mode=pl.Buffered(1)),
                  pl.BlockSpec(w_dw.shape, lambda i: (0, 0)),
                  pl.BlockSpec(w_down.shape, const3, pipeline_mode=pl.Buffered(1))] + comm.specs(),
        out_specs=(pl.BlockSpec((ts, d), lambda i: (i, 0)), pl.BlockSpec((ts, d), lambda i: (i, 0)),
                   pl.BlockSpec((4, ts, cs), lambda i: (0, i, 0)), pl.BlockSpec((4, ts, cs), lambda i: (0, i, 0)),
                   pl.BlockSpec((2, ts, cs), lambda i: (0, i, 0)), pl.BlockSpec((ts, d), lambda i: (i, 0)),
                   *comm.specs()),
        out_shape=(jax.ShapeDtypeStruct((s, d), F32), jax.ShapeDtypeStruct((s, d), BF16),
                   jax.ShapeDtypeStruct((4, s, cs), BF16), jax.ShapeDtypeStruct((4, s, cs), BF16),
                   jax.ShapeDtypeStruct((2, s, cs), BF16), jax.ShapeDtypeStruct((s, d), F32), *comm.outs),
        input_output_aliases=comm.aliases(5, 6),
        scratch_shapes=[pltpu.VMEM((4, FFN_HALO, cs), F32)] + (comm.scratch() if nc else []),
        compiler_params=_cparams(("arbitrary",)),
    )(x, vec, w_up, w_dw, w_down, *comm.arrays)
    return res[:6], comm.split(res[6:])


def _ffn_backward(dout, x, y, a0, cc, vec, w_up, w_dw, w_down):
    s, d = x.shape
    _, _, cs = w_up.shape
    ts = _row_tile(s, 256)
    nb = s // ts
    chunks = _lane_chunks(cs)

    def body(do_ref, x_ref, y_ref, a0_ref, cc_ref, vec_ref, wup_ref, wdw_ref, wdn_ref,
             dx_ref, da0_ref, dy_ref, sum_ref, dwdw_ref, carry):
        step = pl.program_id(0)

        @pl.when(step == 0)
        def _():
            carry[...] = jnp.zeros_like(carry)
            sum_ref[...] = jnp.zeros_like(sum_ref)
            dwdw_ref[...] = jnp.zeros_like(dwdw_ref)

        vec = vec_ref[...]
        a, sh, gt, gpost = _vec_rows(vec)
        do = do_ref[...]
        yn, ry = _rms(y_ref[...])
        dy = _rms_bwd(do * (gt * gpost), yn, ry)
        dyb = dy.astype(BF16)
        dy_ref[...] = dyb
        for q in range(2):
            for c0, cw in chunks:
                cg = cc_ref[q, :, c0:c0 + cw].astype(F32)
                cv = cc_ref[q + 2, :, c0:c0 + cw].astype(F32)
                sg = _sigmoid(cg)
                sl = cg * sg
                du = _dot_nt(dyb, wdn_ref[q, c0:c0 + cw, :])
                dconv = {q: du * cv * (sg * (1.0 + cg * (1.0 - sg))), q + 2: du * sl}
                for j in (q, q + 2):
                    dae = jnp.concatenate([dconv[j], carry[j, :, c0:c0 + cw]], axis=0)
                    carry[j, :, c0:c0 + cw] = dconv[j][:FFN_HALO]
                    up1, up2 = _shift_up(dae, 1)[:ts], _shift_up(dae, 2)[:ts]
                    lanes = slice(j * cs + c0, j * cs + c0 + cw)
                    w = wdw_ref[:, lanes]
                    da0_ref[j, :, c0:c0 + cw] = (w[2:3] * dconv[j] + w[1:2] * up1 + w[0:1] * up2).astype(BF16)
                    a0 = a0_ref[j, :, c0:c0 + cw].astype(F32)
                    dwdw_ref[0:1, lanes] += _colsum(up2 * a0)
                    dwdw_ref[1:2, lanes] += _colsum(up1 * a0)
                    dwdw_ref[2:3, lanes] += _colsum(dconv[j] * a0)
        dh = _dot_nt(da0_ref[0], wup_ref[0])
        for j in range(1, 4):
            dh = dh + _dot_nt(da0_ref[j], wup_ref[j])
        xn, r = _rms(x_ref[...])
        dx_ref[...] = do + _rms_bwd(dh * a, xn, r)
        _add_rows(sum_ref, [_colsum(do * gt * yn), _colsum(do * yn * gpost), _colsum(dh * xn * vec[1:2]),
                            _colsum(dh * xn * vec[0:1]), _colsum(dh)])

    blk = lambda st: (nb - 1 - st, 0)
    blk3 = lambda st: (0, nb - 1 - st, 0)
    const3 = lambda st: (0, 0, 0)
    return pl.pallas_call(
        body, name="ffn_backward", grid=(nb,),
        in_specs=[pl.BlockSpec((ts, d), blk), pl.BlockSpec((ts, d), blk), pl.BlockSpec((ts, d), blk),
                  pl.BlockSpec((4, ts, cs), blk3), pl.BlockSpec((4, ts, cs), blk3),
                  pl.BlockSpec((8, d), lambda st: (0, 0)),
                  pl.BlockSpec(w_up.shape, const3, pipeline_mode=pl.Buffered(1)),
                  pl.BlockSpec(w_dw.shape, lambda st: (0, 0)),
                  pl.BlockSpec(w_down.shape, const3, pipeline_mode=pl.Buffered(1))],
        out_specs=(pl.BlockSpec((ts, d), blk), pl.BlockSpec((4, ts, cs), blk3),
                   pl.BlockSpec((ts, d), blk), pl.BlockSpec((8, d), lambda st: (0, 0)),
                   pl.BlockSpec((8, 4 * cs), lambda st: (0, 0))),
        out_shape=(jax.ShapeDtypeStruct((s, d), F32), jax.ShapeDtypeStruct((4, s, cs), BF16),
                   jax.ShapeDtypeStruct((s, d), BF16),
                   jax.ShapeDtypeStruct((8, d), F32), jax.ShapeDtypeStruct((8, 4 * cs), F32)),
        scratch_shapes=[pltpu.VMEM((4, FFN_HALO, cs), F32)],
        compiler_params=_cparams(("arbitrary",)),
    )(dout, x, y, a0, cc, vec, w_up, w_dw, w_down)


def _conv_halo(width):
    return -(-(width - 1) // 8) * 8


def _conv_forward(x, vec, cvec, w_pw1, b_pw1, w_dw, w_pw2):
    s, d = x.shape
    kw = w_dw.shape[0]
    halo = _conv_halo(kw)
    ts = _row_tile(s, 256)
    hd = d // 2

    def body(x_ref, vec_ref, cvec_ref, w1_ref, b1_ref, wdw_ref, w2_ref,
             o_ref, h_ref, a_ref, uc_ref, z_ref, y_ref, carry):
        i = pl.program_id(0)

        @pl.when(i == 0)
        def _():
            carry[...] = jnp.zeros_like(carry)

        vec, cvec = vec_ref[...], cvec_ref[...]
        a, sh, gt, gpost = _vec_rows(vec)
        xb = x_ref[...]
        xn, _ = _rms(xb)
        hb = (xn * a + sh).astype(BF16)
        h_ref[...] = hb
        for j in range(4):
            a_ref[:, j * hd:(j + 1) * hd] = _dot(hb, w1_ref[j]) + b1_ref[:, j * hd:(j + 1) * hd]
        u = a_ref[:, :d] * _sigmoid(a_ref[:, d:])
        ue = jnp.concatenate([carry[...], u], axis=0)
        carry[...] = u[ts - halo:]
        uc = jnp.zeros((ts, d), F32) + cvec[0:1]
        for k in range(kw):
            uc = uc + wdw_ref[k:k + 1, :] * _shift_down(ue, kw - 1 - k)[halo:]
        uc_ref[...] = uc
        mu = jnp.mean(uc, axis=-1, keepdims=True)
        cen = uc - mu
        rstd = lax.rsqrt(jnp.mean(cen * cen, axis=-1, keepdims=True) + EPS)
        l = cen * rstd * cvec[1:2] + cvec[2:3]
        zb = (l * _sigmoid(l)).astype(BF16)
        z_ref[...] = zb
        y = _dot(zb, w2_ref[...]) + cvec[3:4]
        y_ref[...] = y
        yn, _ = _rms(y)
        o_ref[...] = xb + gt * (yn * gpost)

    row = lambda i: (i, 0)
    const2 = lambda i: (0, 0)
    return pl.pallas_call(
        body, name="conv_forward", grid=(s // ts,),
        in_specs=[pl.BlockSpec((ts, d), row), pl.BlockSpec((8, d), const2), pl.BlockSpec((8, d), const2),
                  pl.BlockSpec(w_pw1.shape, lambda i: (0, 0, 0)), pl.BlockSpec(b_pw1.shape, const2),
                  pl.BlockSpec(w_dw.shape, const2), pl.BlockSpec(w_pw2.shape, const2)],
        out_specs=(pl.BlockSpec((ts, d), row), pl.BlockSpec((ts, d), row), pl.BlockSpec((ts, 2 * d), row),
                   pl.BlockSpec((ts, d), row), pl.BlockSpec((ts, d), row), pl.BlockSpec((ts, d), row)),
        out_shape=(jax.ShapeDtypeStruct((s, d), F32), jax.ShapeDtypeStruct((s, d), BF16),
                   jax.ShapeDtypeStruct((s, 2 * d), F32), jax.ShapeDtypeStruct((s, d), F32),
                   jax.ShapeDtypeStruct((s, d), BF16), jax.ShapeDtypeStruct((s, d), F32)),
        scratch_shapes=[pltpu.VMEM((halo, d), F32)],
        compiler_params=_cparams(("arbitrary",)),
    )(x, vec, cvec, w_pw1, b_pw1, w_dw, w_pw2)


def _conv_backward(dout, x, y, a_pre, uc, vec, cvec, w_pw1, w_dw, w_pw2):
    s, d = x.shape
    kw = w_dw.shape[0]
    kpad = -(-kw // 8) * 8
    halo = _conv_halo(kw)
    ts = _row_tile(s, 256)
    nb = s // ts
    hb = ts // halo
    hd = d // 2

    def body(do_ref, x_ref, y_ref, a_ref, ah_ref, uc_ref, vec_ref, cvec_ref, w1_ref, wdw_ref, w2_ref,
             dx_ref, da_ref, dy_ref, sum_ref, dwdw_ref, carry):
        step = pl.program_id(0)
        i = nb - 1 - step

        @pl.when(step == 0)
        def _():
            carry[...] = jnp.zeros_like(carry)
            sum_ref[...] = jnp.zeros_like(sum_ref)
            dwdw_ref[...] = jnp.zeros_like(dwdw_ref)

        vec, cvec = vec_ref[...], cvec_ref[...]
        a, sh, gt, gpost = _vec_rows(vec)
        do = do_ref[...]
        yn, ry = _rms(y_ref[...])
        dy = _rms_bwd(do * (gt * gpost), yn, ry)
        dyb = dy.astype(BF16)
        dy_ref[...] = dyb
        dz = _dot_nt(dyb, w2_ref[...])
        uc = uc_ref[...]
        mu = jnp.mean(uc, axis=-1, keepdims=True)
        cen = uc - mu
        rstd = lax.rsqrt(jnp.mean(cen * cen, axis=-1, keepdims=True) + EPS)
        lhat = cen * rstd
        l = lhat * cvec[1:2] + cvec[2:3]
        sgl = _sigmoid(l)
        dl = dz * (sgl * (1.0 + l * (1.0 - sgl)))
        dlhat = dl * cvec[1:2]
        duc = rstd * (dlhat - jnp.mean(dlhat, axis=-1, keepdims=True)
                      - lhat * jnp.mean(dlhat * lhat, axis=-1, keepdims=True))
        ae = jnp.concatenate([ah_ref[...] * (i > 0).astype(F32), a_ref[...]], axis=0)
        sgate = _sigmoid(ae[:, d:])
        val = ae[:, :d]
        ue = val * sgate
        rowid = lax.broadcasted_iota(jnp.int32, (halo + ts, 1), 0)
        ue = jnp.where((rowid >= halo) | (i > 0), ue, 0.0)
        duce = jnp.concatenate([duc, carry[...]], axis=0)
        carry[...] = duc[:halo]
        du = jnp.zeros((ts, d), F32)
        for k in range(kw):
            du = du + wdw_ref[k:k + 1, :] * _shift_up(duce, kw - 1 - k)[:ts]
            dwdw_ref[k:k + 1, :] += _colsum(duc * _shift_down(ue, kw - 1 - k)[halo:])
        sg, vl = sgate[halo:], val[halo:]
        dval = du * sg
        dgate = du * vl * (sg * (1.0 - sg))
        dvb, dgb = dval.astype(BF16), dgate.astype(BF16)
        dh = jnp.zeros((ts, d), F32)
        for j in range(2):
            da_ref[j] = dvb[:, j * hd:(j + 1) * hd]
            da_ref[j + 2] = dgb[:, j * hd:(j + 1) * hd]
            dh = dh + _dot_nt(dvb[:, j * hd:(j + 1) * hd], w1_ref[j]) + _dot_nt(dgb[:, j * hd:(j + 1) * hd], w1_ref[j + 2])
        xn, r = _rms(x_ref[...])
        dx_ref[...] = do + _rms_bwd(dh * a, xn, r)
        _add_rows(sum_ref, [_colsum(do * gt * yn), _colsum(do * yn * gpost), _colsum(dh * xn * vec[1:2]),
                            _colsum(dh * xn * vec[0:1]), _colsum(dh), _colsum(dy), _colsum(dl * lhat), _colsum(dl),
                            _colsum(duc), _colsum(dval), _colsum(dgate)])

    blk = lambda st: (nb - 1 - st, 0)
    const2 = lambda st: (0, 0)
    return pl.pallas_call(
        body, name="conv_backward", grid=(nb,),
        in_specs=[pl.BlockSpec((ts, d), blk), pl.BlockSpec((ts, d), blk), pl.BlockSpec((ts, d), blk),
                  pl.BlockSpec((ts, 2 * d), blk),
                  pl.BlockSpec((halo, 2 * d), lambda st: (jnp.maximum((nb - 1 - st) * hb - 1, 0), 0)),
                  pl.BlockSpec((ts, d), blk), pl.BlockSpec((8, d), const2), pl.BlockSpec((8, d), const2),
                  pl.BlockSpec(w_pw1.shape, lambda st: (0, 0, 0)), pl.BlockSpec(w_dw.shape, const2),
                  pl.BlockSpec(w_pw2.shape, const2)],
        out_specs=(pl.BlockSpec((ts, d), blk), pl.BlockSpec((4, ts, hd), lambda st: (0, nb - 1 - st, 0)),
                   pl.BlockSpec((ts, d), blk), pl.BlockSpec((16, d), const2), pl.BlockSpec((kpad, d), const2)),
        out_shape=(jax.ShapeDtypeStruct((s, d), F32), jax.ShapeDtypeStruct((4, s, hd), BF16),
                   jax.ShapeDtypeStruct((s, d), BF16), jax.ShapeDtypeStruct((16, d), F32),
                   jax.ShapeDtypeStruct((kpad, d), F32)),
        scratch_shapes=[pltpu.VMEM((halo, d), F32)],
        compiler_params=_cparams(("arbitrary",)),
    )(dout, x, y, a_pre, a_pre, uc, vec, cvec, w_pw1, w_dw, w_pw2)


def _loss_head(y, target):
    s, d = y.shape
    ts = _row_tile(s, 512)

    def body(y_ref, t_ref, dy_ref, l_ref):
        @pl.when(pl.program_id(0) == 0)
        def _():
            l_ref[...] = jnp.zeros_like(l_ref)

        e = y_ref[...] - t_ref[...]
        dy_ref[...] = e * (1.0 / d)
        l_ref[0:1, :] += _colsum(e * e) * (0.5 / d)

    row = lambda i: (i, 0)
    return pl.pallas_call(
        body, name="loss_head", grid=(s // ts,),
        in_specs=[pl.BlockSpec((ts, d), row), pl.BlockSpec((ts, d), row)],
        out_specs=(pl.BlockSpec((ts, d), row), pl.BlockSpec((8, d), lambda i: (0, 0))),
        out_shape=(jax.ShapeDtypeStruct((s, d), F32), jax.ShapeDtypeStruct((8, d), F32)),
        compiler_params=_cparams(("arbitrary",)),
    )(y, target)


def _weight_grad(a, b, comm=None):
    na, s, k = a.shape
    nb_, _, n = b.shape
    nj = max(na, nb_)
    ts = _row_tile(s, 2048)
    nt = s // ts
    comm = comm or _Comm([])
    nc = len(comm.arrays)

    def body(*refs):
        a_ref, b_ref = refs[:2]
        cin = refs[2:2 + nc]
        o_ref = refs[2 + nc]
        cout = refs[3 + nc:3 + 2 * nc]
        sems = refs[3 + 2 * nc:]
        j, t = pl.program_id(0), pl.program_id(1)

        if nc:
            @pl.when((j == 0) & (t == 0))
            def _():
                comm.run(0, cin, cout, *sems)

            @pl.when((j == nj // 2) & (t == nt // 2))
            def _():
                comm.run(1, cin, cout, *sems)

        @pl.when(t == 0)
        def _():
            o_ref[...] = jnp.zeros_like(o_ref)

        o_ref[0] += _dot_tn(a_ref[0], b_ref[0])

        if nc:
            @pl.when((j == nj - 1) & (t == nt - 1))
            def _():
                comm.run(2, cin, cout, *sems)

    res = pl.pallas_call(
        body, name="weight_grad", grid=(nj, nt),
        in_specs=[pl.BlockSpec((1, ts, k), (lambda j, t: (j, t, 0)) if na > 1 else (lambda j, t: (0, t, 0))),
                  pl.BlockSpec((1, ts, n), (lambda j, t: (j, t, 0)) if nb_ > 1 else (lambda j, t: (0, t, 0)))]
        + comm.specs(),
        out_specs=(pl.BlockSpec((1, k, n), lambda j, t: (j, 0, 0)), *comm.specs()),
        out_shape=(jax.ShapeDtypeStruct((nj, k, n), F32), *comm.outs),
        input_output_aliases=comm.aliases(2, 1),
        scratch_shapes=comm.scratch() if nc else [],
        compiler_params=_cparams(("arbitrary", "arbitrary") if nc else ("parallel", "arbitrary")),
    )(a, b, *comm.arrays)
    return res[0], comm.split(res[1:])


def _adamw(w, g, m, v):
    nl, r, c = w.shape
    tr = _row_tile(r, 256)
    c1 = 1.0 / (1.0 - ADAM_B1 ** ADAM_STEP)
    c2 = 1.0 / (1.0 - ADAM_B2 ** ADAM_STEP)

    def body(w_ref, g_ref, m_ref, v_ref, d_ref, nm_ref, nv_ref):
        g_ = g_ref[...]
        nm = ADAM_B1 * m_ref[...] + (1.0 - ADAM_B1) * g_
        nv = ADAM_B2 * v_ref[...] + (1.0 - ADAM_B2) * (g_ * g_)
        nm_ref[...] = nm
        nv_ref[...] = nv
        d_ref[...] = -ADAM_LR * ((nm * c1) / (jnp.sqrt(nv * c2) + ADAM_EPS) + ADAM_WD * w_ref[...])

    spec = pl.BlockSpec((1, tr, c), lambda l, i: (l, i, 0))
    shp = jax.ShapeDtypeStruct((nl, r, c), F32)
    return pl.pallas_call(
        body, name="adamw", grid=(nl, r // tr), in_specs=[spec] * 4, out_specs=(spec,) * 3, out_shape=(shp,) * 3,
        compiler_params=_cparams(("parallel", "parallel")),
    )(w, g, m, v)


def _add_my_half(g, other, idx):
    _, _, h, c = g.shape
    th = _row_tile(h, 256)

    def body(idx_ref, g_ref, o_ref, out_ref):
        out_ref[...] = (g_ref[:, 0] + o_ref[...]).astype(BF16)

    return pl.pallas_call(
        body, name="add_my_half",
        grid_spec=pltpu.PrefetchScalarGridSpec(
            num_scalar_prefetch=1, grid=(4, h // th),
            in_specs=[pl.BlockSpec((1, 1, th, c), lambda j, i, idx_ref: (j, idx_ref[1], i, 0)),
                      pl.BlockSpec((1, th, c), lambda j, i, idx_ref: (j, i, 0))],
            out_specs=pl.BlockSpec((1, th, c), lambda j, i, idx_ref: (j, i, 0))),
        out_shape=jax.ShapeDtypeStruct(other.shape, BF16),
        compiler_params=_cparams(("parallel", "parallel")),
    )(idx, g, other)


def _sum_for_my_chip(g, other, got, idx):
    _, _, h, c = g.shape
    th = _row_tile(h, 256)

    def body(idx_ref, g_ref, o_ref, q_ref, out_ref):
        out_ref[0] = (((g_ref[0, 0] + o_ref[0]) + q_ref[0].astype(F32)) + q_ref[1].astype(F32)) + q_ref[2].astype(F32)

    return pl.pallas_call(
        body, name="sum_for_my_chip",
        grid_spec=pltpu.PrefetchScalarGridSpec(
            num_scalar_prefetch=1, grid=(h // th,),
            in_specs=[pl.BlockSpec((1, 1, th, c), lambda i, idx_ref: (idx_ref[0], idx_ref[1], i, 0)),
                      pl.BlockSpec((1, th, c), lambda i, idx_ref: (idx_ref[0], i, 0)),
                      pl.BlockSpec((3, th, c), lambda i, idx_ref: (0, i, 0))],
            out_specs=pl.BlockSpec((1, th, c), lambda i, idx_ref: (idx_ref[1], i, 0))),
        out_shape=jax.ShapeDtypeStruct((2, h, c), F32),
        compiler_params=_cparams(("parallel",)),
    )(idx, g, other, got)


class _Reducer:
    def __init__(self, idx):
        self.idx = idx
        self.groups = []

    def add(self, grads):
        group = {"state": 0, "g": [g.reshape(4, 2, g.shape[1] // 2, g.shape[2]) for g in grads]}
        self.groups.append(group)
        return group

    def steps(self):
        ops, owners = [], []
        for gr in self.groups:
            if gr["state"] == 0:
                ops.append(_Swap(gr["g"]))
            elif gr["state"] == 1:
                ops.append(_Exchange(gr["parts"]))
            elif gr["state"] == 2:
                ops.append(_Join(gr["bufs"]))
            else:
                continue
            owners.append(gr)
        return ops, owners

    def absorb(self, owners, results):
        for gr, res in zip(owners, results):
            if gr["state"] == 0:
                gr["other"] = res
                gr["parts"] = [_add_my_half(g, o, self.idx) for g, o in zip(gr["g"], res)]
            elif gr["state"] == 1:
                gr["bufs"] = [_sum_for_my_chip(g, o, q, self.idx) for g, o, q in zip(gr["g"], gr["other"], res)]
            else:
                gr["full"] = [b.reshape(2 * b.shape[1], b.shape[2]) for b in res]
            gr["state"] += 1

    def drain(self):
        while any(gr["state"] < 3 for gr in self.groups):
            ops, owners = self.steps()
            self.absorb(owners, _communicate(ops))


def _all_gather_rows(block):
    m, n = block.shape

    def body(x_ref, out_ref, send_sems, recv_sems):
        x, y, c = lax.axis_index("x"), lax.axis_index("y"), lax.axis_index("c")
        me, sibling = (x, y, c), (x, y, 1 - c)
        chips = [(1 - x, y), (x, 1 - y), (1 - x, 1 - y)]

        def slot(px, py, pc):
            return out_ref.at[4 * px + 2 * py + pc]

        def copy(k, block_of, to, src=None):
            return pltpu.make_async_remote_copy(
                src_ref=slot(*block_of) if src is None else src, dst_ref=slot(*block_of),
                send_sem=send_sems.at[k], recv_sem=recv_sems.at[k], device_id=to, device_id_type=MESH)

        out_ref[4 * x + 2 * y + c] = x_ref[...]
        first = [copy(0, me, sibling, src=x_ref)]
        first += [copy(1 + r, me, (*chip, c), src=x_ref) for r, chip in enumerate(chips)]
        for cp in first:
            cp.start()
        passed = [copy(4 + r, (*chip, c), sibling) for r, chip in enumerate(chips)]
        for r, chip in enumerate(chips):
            copy(1 + r, (*chip, c), me).wait_recv()
            passed[r].start()
        copy(0, sibling, me).wait_recv()
        for r, chip in enumerate(chips):
            copy(4 + r, (*chip, 1 - c), me).wait_recv()
        for cp in first + passed:
            cp.wait_send()

    vm = pl.BlockSpec(memory_space=pltpu.VMEM)
    return pl.pallas_call(
        body, name="all_gather_rows",
        out_shape=jax.ShapeDtypeStruct((N_DEV, m, n), block.dtype),
        in_specs=[vm], out_specs=vm,
        scratch_shapes=[pltpu.SemaphoreType.DMA((7,)), pltpu.SemaphoreType.DMA((7,))],
        compiler_params=_cparams(),
    )(block)


def _sum_devices(gathered):
    nd, m, n = gathered.shape

    def body(g_ref, o_ref):
        acc = g_ref[0]
        for b in range(1, nd):
            acc = acc + g_ref[b]
        o_ref[...] = acc

    return pl.pallas_call(
        body, name="sum_devices", out_shape=jax.ShapeDtypeStruct((m, n), F32),
        in_specs=[pl.BlockSpec(memory_space=pltpu.VMEM)], out_specs=pl.BlockSpec(memory_space=pltpu.VMEM),
        compiler_params=_cparams(),
    )(gathered)


def _ada_weight_grad(c_all, dmod_cols):
    nl, nd, ncol = dmod_cols.shape
    d = c_all.shape[1]

    def body(c_ref, dm_ref, o_ref):
        o_ref[0] = lax.dot_general(c_ref[...], dm_ref[0], (((0,), (0,)), ((), ())),
                                   preferred_element_type=F32, precision=lax.Precision.HIGHEST)

    return pl.pallas_call(
        body, name="ada_weight_grad", grid=(nl,),
        in_specs=[pl.BlockSpec((nd, d), lambda l: (0, 0)), pl.BlockSpec((1, nd, ncol), lambda l: (l, 0, 0))],
        out_specs=pl.BlockSpec((1, d, ncol), lambda l: (l, 0, 0)),
        out_shape=jax.ShapeDtypeStruct((nl, d, ncol), F32), compiler_params=_cparams(("parallel",)),
    )(c_all, dmod_cols)


def _pad_rows(a, rows):
    return jnp.pad(a, ((0, rows - a.shape[0]), (0, 0)))


def _shard_cols(full, chip, width):
    return lax.dynamic_slice_in_dim(full, chip * width, width, axis=full.ndim - 1)


def kernel(x, c, ada_w, ada_b, pre_g, post_g, pool_w, pool_scale, cv_w_pw1, cv_b_pw1, cv_w_dw, cv_b_dw, cv_ln_g, cv_ln_b, cv_w_pw2, cv_b_pw2, ffn_w_up, ffn_w_dw, ffn_w_down, loss_target, m_ada_w, m_ada_b, m_pre_g, m_post_g, m_pool_w, m_pool_scale, m_cv_w_pw1, m_cv_b_pw1, m_cv_w_dw, m_cv_b_dw, m_cv_ln_g, m_cv_ln_b, m_cv_w_pw2, m_cv_b_pw2, m_ffn_w_up, m_ffn_w_dw, m_ffn_w_down, v_ada_w, v_ada_b, v_pre_g, v_post_g, v_pool_w, v_pool_scale, v_cv_w_pw1, v_cv_b_pw1, v_cv_w_dw, v_cv_b_dw, v_cv_ln_g, v_cv_ln_b, v_cv_w_pw2, v_cv_b_pw2, v_ffn_w_up, v_ffn_w_dw, v_ffn_w_down):
    s, d = x.shape[1], x.shape[2]
    dq = d // N_CHIPS
    n_g = pool_w.shape[1]
    gq = pool_w.shape[2]
    gd = pool_w.shape[3]
    kw = cv_w_dw.shape[1]
    cs = ffn_w_up.shape[2]
    fq = ffn_w_down.shape[1]
    chip = 2 * lax.axis_index("x") + lax.axis_index("y")
    core = lax.axis_index("c")
    chip1 = jnp.reshape(chip, (1,)).astype(jnp.int32)
    core1 = jnp.reshape(core, (1,)).astype(jnp.int32)
    xs, tgt = x[0], loss_target[0]

    c_rep, mod_rep = _ada_forward(c, ada_w)
    c_all = c_rep[:, 0, :]
    mod = mod_rep[:, :, 0, :].transpose(1, 0, 2).reshape(ada_b.shape) + ada_b

    small_rows = [pre_g.reshape(4, dq), post_g.reshape(4, dq), cv_w_dw[0], cv_b_dw, cv_ln_g, cv_ln_b, cv_b_pw2,
                  cv_b_pw1.reshape(2, dq)]
    small = jnp.concatenate(small_rows, axis=0)
    n_small = small.shape[0]
    small = _pad_rows(small, -(-n_small // 16) * 16)
    dwf = _pad_rows(ffn_w_dw.reshape(6, cs), 16)
    def slot(a):
        return lax.dynamic_update_slice_in_dim(jnp.zeros((N_CHIPS,) + a.shape, a.dtype), a[None], chip, axis=0)

    first = [_cast_into_slot(pool_w.reshape(n_g * gq, gd), chip1), _cast_into_slot(ffn_w_up[0], chip1),
             _cast_into_slot(ffn_w_down[0], chip1), slot(small), slot(dwf)]
    (g_pool, g_up0, g_dn0, g_small, g_dwf), = _communicate([_AllGather(first)])
    later = _AllGather([_cast_into_slot(cv_w_pw1[0], chip1), _cast_into_slot(cv_w_pw2[0], chip1),
                        _cast_into_slot(ffn_w_up[1], chip1), _cast_into_slot(ffn_w_down[1], chip1)])
    poolw_full = g_pool.reshape(N_CHIPS, n_g, gq, gd).transpose(1, 0, 2, 3).reshape(n_g, gd, gd)
    smallf = g_small.transpose(1, 0, 2).reshape(g_small.shape[1], d)
    pre_full, post_full = smallf[0:4].reshape(2, 2, d), smallf[4:8].reshape(2, 2, d)
    wdw31 = smallf[8:8 + kw]
    o = 8 + kw
    b_dw, ln_g, ln_b, b_pw2 = smallf[o:o + 1], smallf[o + 1:o + 2], smallf[o + 2:o + 3], smallf[o + 3:o + 4]
    b_pw1 = g_small[:, o + 4:o + 6, :].reshape(1, 2 * d)
    ffn_dw = g_dwf[:, :6, :].transpose(1, 0, 2).reshape(2, 3, N_CHIPS * cs)

    def sub_vec(layer, sub, extra=None):
        m6 = mod[layer].reshape(6, d)
        rows = [pre_full[layer, sub][None], 1.0 + m6[3 * sub + 1][None], m6[3 * sub][None], m6[3 * sub + 2][None],
                post_full[layer, sub][None]]
        if extra is not None:
            rows.append(extra)
        return _pad_rows(jnp.concatenate(rows, axis=0), 8)

    vec_pool = sub_vec(0, 0, pool_scale)
    vec_f0, vec_conv, vec_f1 = sub_vec(0, 1), sub_vec(1, 0), sub_vec(1, 1)
    cvec = _pad_rows(jnp.concatenate([b_dw, ln_g, ln_b, b_pw2], axis=0), 8)

    x1 = _pool_forward(xs, vec_pool, poolw_full)
    w_up0, w_dn0 = g_up0, g_dn0.reshape(2, 2 * fq, d)
    (x2, h_f0, a0_f0, cc_f0, u_f0, y_f0), ((g_pw1, g_pw2, g_up1, g_dn1),) = _ffn_forward(
        x1, vec_f0, w_up0, ffn_dw[0], w_dn0, _Comm([later]))
    pw2_full = g_pw2.reshape(d, d)
    w_up1, w_dn1 = g_up1, g_dn1.reshape(2, 2 * fq, d)
    x3, h_cv, a_cv, uc_cv, z_cv, y_cv = _conv_forward(x2, vec_conv, cvec, g_pw1, b_pw1, wdw31, pw2_full)
    (x4, h_f1, a0_f1, cc_f1, u_f1, y_f1), _ = _ffn_forward(x3, vec_f1, w_up1, ffn_dw[1], w_dn1)
    dx4, loss_rows = _loss_head(x4, tgt)

    dx3, da0_f1, dy_f1, sum_f1, dwdw_f1 = _ffn_backward(dx4, x3, y_f1, a0_f1, cc_f1, vec_f1, w_up1, ffn_dw[1], w_dn1)
    dx2, da_cv, dy_cv, sum_cv, dwdw_cv = _conv_backward(dx3, x2, y_cv, a_cv, uc_cv, vec_conv, cvec, g_pw1, wdw31, pw2_full)
    dx1, da0_f0, dy_f0, sum_f0, dwdw_f0 = _ffn_backward(dx2, x1, y_f0, a0_f0, cc_f0, vec_f0, w_up0, ffn_dw[0], w_dn0)
    dx0, sum_pool, gw_pool = _pool_backward(dx1, xs, vec_pool, poolw_full)
    gw_pool4 = gw_pool.reshape(n_g, N_CHIPS, gq, gd).transpose(1, 0, 2, 3).reshape(N_CHIPS, n_g * gq, gd)

    red = _Reducer(jnp.concatenate([chip1, core1]))

    def grad_stage(a, b, shape=None):
        ops, owners = red.steps()
        gw, results = _weight_grad(a, b, _Comm(ops))
        red.absorb(owners, results)
        return gw if shape is None else gw.reshape(shape)

    r_up1 = red.add([grad_stage(h_f1[None], da0_f1)])
    r_up0 = red.add([grad_stage(h_f0[None], da0_f0)])
    r_dn1 = red.add([grad_stage(u_f1, dy_f1[None], (N_CHIPS, fq, d))])
    r_dn0 = red.add([grad_stage(u_f0, dy_f0[None], (N_CHIPS, fq, d))])
    r_pw1 = red.add([grad_stage(h_cv[None], da_cv)])
    r_last = red.add([grad_stage(z_cv[None], dy_cv[None], (N_CHIPS, dq, d)), gw_pool4])
    red.drain()
    r_up1, r_up0, r_dn1, r_dn0, r_pw1 = [gr["full"][0] for gr in (r_up1, r_up0, r_dn1, r_dn0, r_pw1)]
    r_pw2, r_pool = r_last["full"]

    slab = jnp.concatenate([sum_f1, sum_cv, dwdw_cv, sum_f0, sum_pool, loss_rows], axis=0)
    wide = jnp.concatenate([dwdw_f1, dwdw_f0], axis=0)
    slab_all = _all_gather_rows(slab)
    wide_all = _all_gather_rows(wide)
    tot = _sum_devices(slab_all)
    tot_wide = _sum_devices(wide_all)
    kpad = dwdw_cv.shape[0]
    o_cv, o_dw, o_f0 = 8, 24, 24 + kpad
    o_pool, o_loss = o_f0 + 8, o_f0 + 16
    loss = jnp.sum(tot[o_loss])
    dmod_l0 = jnp.concatenate([slab_all[:, o_pool + 4], slab_all[:, o_pool + 3], slab_all[:, o_pool + 1],
                               slab_all[:, o_f0 + 4], slab_all[:, o_f0 + 3], slab_all[:, o_f0 + 1]], axis=-1)
    dmod_l1 = jnp.concatenate([slab_all[:, o_cv + 4], slab_all[:, o_cv + 3], slab_all[:, o_cv + 1],
                               slab_all[:, 4], slab_all[:, 3], slab_all[:, 1]], axis=-1)
    dmod = jnp.stack([dmod_l0, dmod_l1], axis=0)
    g_ada_b = _sum_devices(dmod.transpose(1, 0, 2))
    ncol = ada_w.shape[2]
    g_ada_w = _ada_weight_grad(c_all, _shard_cols(dmod, chip, ncol))

    g_pre = jnp.stack([jnp.stack([tot[o_pool + 2], tot[o_f0 + 2]]), jnp.stack([tot[o_cv + 2], tot[2]])])
    g_post = jnp.stack([jnp.stack([tot[o_pool + 0], tot[o_f0 + 0]]), jnp.stack([tot[o_cv + 0], tot[0]])])
    g_pool_scale = tot[o_pool + 5][None]
    g_b_pw2, g_ln_g, g_ln_b, g_b_dw = tot[o_cv + 5], tot[o_cv + 6], tot[o_cv + 7], tot[o_cv + 8]
    g_b_pw1 = jnp.concatenate([tot[o_cv + 9], tot[o_cv + 10]])
    g_w_dw31 = tot[o_dw:o_dw + kw]
    g_ffn_dw = jnp.stack([tot_wide[8:11], tot_wide[0:3]])

    grads_small = {
        "pre_g": _shard_cols(g_pre, chip, dq), "post_g": _shard_cols(g_post, chip, dq),
        "pool_scale": g_pool_scale, "cv_b_pw1": _shard_cols(g_b_pw1[None], chip, 2 * dq),
        "cv_w_dw": _shard_cols(g_w_dw31[None], chip, dq), "cv_b_dw": _shard_cols(g_b_dw[None], chip, dq),
        "cv_ln_g": _shard_cols(g_ln_g[None], chip, dq), "cv_ln_b": _shard_cols(g_ln_b[None], chip, dq),
        "cv_b_pw2": _shard_cols(g_b_pw2[None], chip, dq), "ffn_w_dw": _shard_cols(g_ffn_dw, chip, cs),
        "ada_b": g_ada_b,
    }
    params_small = {
        "pre_g": (pre_g, m_pre_g, v_pre_g), "post_g": (post_g, m_post_g, v_post_g),
        "pool_scale": (pool_scale, m_pool_scale, v_pool_scale), "cv_b_pw1": (cv_b_pw1, m_cv_b_pw1, v_cv_b_pw1),
        "cv_w_dw": (cv_w_dw, m_cv_w_dw, v_cv_w_dw), "cv_b_dw": (cv_b_dw, m_cv_b_dw, v_cv_b_dw),
        "cv_ln_g": (cv_ln_g, m_cv_ln_g, v_cv_ln_g), "cv_ln_b": (cv_ln_b, m_cv_ln_b, v_cv_ln_b),
        "cv_b_pw2": (cv_b_pw2, m_cv_b_pw2, v_cv_b_pw2), "ffn_w_dw": (ffn_w_dw, m_ffn_w_dw, v_ffn_w_dw),
        "ada_b": (ada_b, m_ada_b, v_ada_b),
    }
    names = list(params_small)
    sizes = [params_small[nm][0].size for nm in names]
    padded = [-(-sz // 1024) * 1024 for sz in sizes]

    def pack(arrs):
        flat = [jnp.pad(a.reshape(-1), (0, p - a.size)) for a, p in zip(arrs, padded)]
        return jnp.concatenate(flat).reshape(1, -1, 128)

    pk_w = pack([params_small[nm][0] for nm in names])
    pk_m = pack([params_small[nm][1] for nm in names])
    pk_v = pack([params_small[nm][2] for nm in names])
    pk_g = pack([grads_small[nm].reshape(params_small[nm][0].shape) for nm in names])
    pk_d, pk_nm, pk_nv = _adamw(pk_w, pk_g, pk_m, pk_v)

    def unpack(pk):
        flat, out, off = pk.reshape(-1), {}, 0
        for nm, sz, p in zip(names, sizes, padded):
            out[nm] = flat[off:off + sz].reshape(params_small[nm][0].shape)
            off += p
        return out

    small_d, small_m, small_v = unpack(pk_d), unpack(pk_nm), unpack(pk_nv)
    small_g = {nm: grads_small[nm].reshape(params_small[nm][0].shape) for nm in names}

    big_g = {
        "ada_w": g_ada_w,
        "pool_w": r_pool.reshape(pool_w.shape),
        "cv_w_pw1": r_pw1[None], "cv_w_pw2": r_pw2[None],
        "ffn_w_up": jnp.stack([r_up0, r_up1]), "ffn_w_down": jnp.stack([r_dn0, r_dn1]),
    }
    big_p = {
        "ada_w": (ada_w, m_ada_w, v_ada_w), "pool_w": (pool_w, m_pool_w, v_pool_w),
        "cv_w_pw1": (cv_w_pw1, m_cv_w_pw1, v_cv_w_pw1), "cv_w_pw2": (cv_w_pw2, m_cv_w_pw2, v_cv_w_pw2),
        "ffn_w_up": (ffn_w_up, m_ffn_w_up, v_ffn_w_up), "ffn_w_down": (ffn_w_down, m_ffn_w_down, v_ffn_w_down),
    }
    big_d, big_m, big_v = {}, {}, {}
    for nm, (w, m, v) in big_p.items():
        shp = w.shape
        as3 = lambda t: t.reshape((-1,) + shp[-2:])
        dl, nm_, nv_ = _adamw(as3(w), as3(big_g[nm]), as3(m), as3(v))
        big_d[nm], big_m[nm], big_v[nm] = dl.reshape(shp), nm_.reshape(shp), nv_.reshape(shp)

    order = ["ada_w", "ada_b", "pre_g", "post_g", "pool_w", "pool_scale", "cv_w_pw1", "cv_b_pw1", "cv_w_dw", "cv_b_dw",
             "cv_ln_g", "cv_ln_b", "cv_w_pw2", "cv_b_pw2", "ffn_w_up", "ffn_w_dw", "ffn_w_down"]
    pick = lambda bigs, smalls: [bigs[nm] if nm in bigs else smalls[nm] for nm in order]
    return (loss, dx0[None], *pick(big_g, small_g), *pick(big_d, small_d), *pick(big_m, small_m),
            *pick(big_v, small_v))
```

```python
import functools

import jax
import jax.numpy as jnp
from jax import lax
from jax.experimental import pallas as pl
from jax.experimental.pallas import tpu as pltpu

F32 = jnp.float32
BF16 = jnp.bfloat16
EPS = 1e-6
N_CHIPS = 4
N_DEV = 8
POOL_WINDOWS = (2, 4, 8, 16)
POOL_HALO = 16
FFN_HALO = 16
CONV_ROWS, CONV_LANES = 128, 128
ADAM_LR = 0.001
ADAM_B1 = 0.9
ADAM_B2 = 0.999
ADAM_EPS = 1e-08
ADAM_WD = 0.01
ADAM_STEP = 10
V7X_VMEM_LIMIT = 58 * 1024 * 1024
MESH = pl.DeviceIdType.MESH


def _cparams(sem=None, vmem=V7X_VMEM_LIMIT):
    return pltpu.CompilerParams(dimension_semantics=sem, vmem_limit_bytes=vmem)


def _row_tile(n, want):
    if n <= want:
        return n
    t = want - want % 8
    while n % t:
        t -= 8
    return t


def _lane_chunks(width):
    out, c = [], 0
    while c < width:
        w = min(512, width - c)
        out.append((c, w))
        c += w
    return out


def _dot(a, b):
    return jnp.dot(a, b, preferred_element_type=F32)


def _dot_nt(a, b):
    return lax.dot_general(a, b, (((1,), (1,)), ((), ())), preferred_element_type=F32)


def _dot_tn(a, b):
    return lax.dot_general(a, b, (((0,), (0,)), ((), ())), preferred_element_type=F32)


def _rms(x):
    r = lax.rsqrt(jnp.mean(x * x, axis=-1, keepdims=True) + EPS)
    return x * r, r


def _rms_bwd(dyn, yn, r):
    return r * (dyn - yn * jnp.mean(dyn * yn, axis=-1, keepdims=True))


def _sigmoid(x):
    return 0.5 * jnp.tanh(0.5 * x) + 0.5


def _colsum(x):
    return jnp.sum(x, axis=0, keepdims=True)


def _shift_down(x, k):
    return x if k == 0 else pltpu.roll(x, k, 0)


def _shift_up(x, k):
    return x if k == 0 else pltpu.roll(x, x.shape[0] - k, 0)


def _vec_rows(vec):
    return vec[0:1] * vec[1:2], vec[2:3], vec[3:4], vec[4:5]


def _norm_sums(do, yn, dh, xn, vec):
    p, q = _colsum(do * yn), _colsum(dh * xn)
    return [p * vec[3:4], p * vec[4:5], q * vec[1:2], q * vec[0:1]]


def _add_rows(sum_ref, rows):
    for k, r in enumerate(rows):
        sum_ref[k:k + 1, :] += r


def _ada_forward(c, ada_w):
    n_layers, d, ncol = ada_w.shape

    def body(c_ref, w_ref, call_ref, mod_ref, part_ref, sendbuf, send_sems, recv_sems, send2, recv2):
        x, y, cc = lax.axis_index("x"), lax.axis_index("y"), lax.axis_index("c")
        me = 4 * x + 2 * y + cc
        rel = [(x, y, 1 - cc), (1 - x, y, cc), (x, 1 - y, cc), (1 - x, 1 - y, cc),
               (1 - x, y, 1 - cc), (x, 1 - y, 1 - cc), (1 - x, 1 - y, 1 - cc)]
        cv = c_ref[...]
        call_ref[me] = jnp.broadcast_to(cv * _sigmoid(cv), (8, d))

        def gather(k, block, to):
            blk = call_ref.at[block]
            return pltpu.make_async_remote_copy(src_ref=blk, dst_ref=blk, send_sem=send_sems.at[k],
                                                recv_sem=recv_sems.at[k], device_id=to, device_id_type=MESH)

        for k, to in enumerate(rel):
            gather(k, me, to).start()
        for k, (px, py, pc) in enumerate(rel):
            gather(k, 4 * px + 2 * py + pc, rel[k]).wait_recv()
        for k, to in enumerate(rel):
            gather(k, me, to).wait_send()

        ca = call_ref[...].reshape(8 * N_DEV, d)
        for l in range(n_layers):
            part_ref[l] = jnp.dot(ca, w_ref[l], preferred_element_type=F32, precision=lax.Precision.HIGHEST)

        j = 2 * x + y
        chips = [(1 - x, y), (x, 1 - y), (1 - x, 1 - y)]

        def rows_of(b):
            return part_ref[:, pl.ds(pl.multiple_of(8 * b, 8), 8), :]

        def scatter(k, src_j, to):
            return pltpu.make_async_remote_copy(
                src_ref=sendbuf.at[k], dst_ref=mod_ref.at[src_j], send_sem=send2.at[k], recv_sem=recv2.at[k],
                device_id=to, device_id_type=MESH)

        mod_ref[j] = rows_of(me)
        for k, (px, py) in enumerate(chips):
            sendbuf[k] = rows_of(4 * px + 2 * py + cc)
            scatter(k, j, (px, py, cc)).start()
        for k, (px, py) in enumerate(chips):
            scatter(k, 2 * px + py, (px, py, cc)).wait_recv()
        for k, (px, py) in enumerate(chips):
            scatter(k, j, (px, py, cc)).wait_send()

    vm = pl.BlockSpec(memory_space=pltpu.VMEM)
    return pl.pallas_call(
        body, name="ada_forward",
        out_shape=(jax.ShapeDtypeStruct((N_DEV, 8, d), F32), jax.ShapeDtypeStruct((N_CHIPS, n_layers, 8, ncol), F32)),
        in_specs=[vm, vm], out_specs=(vm, vm),
        scratch_shapes=[pltpu.VMEM((n_layers, 8 * N_DEV, ncol), F32), pltpu.VMEM((3, n_layers, 8, ncol), F32),
                        pltpu.SemaphoreType.DMA((7,)), pltpu.SemaphoreType.DMA((7,)),
                        pltpu.SemaphoreType.DMA((3,)), pltpu.SemaphoreType.DMA((3,))],
        compiler_params=_cparams(),
    )(c, ada_w)


def _cast_into_slot(w2d, chip):
    r, c = w2d.shape
    tr = _row_tile(r, 256)

    def body(chip_ref, w_ref, o_ref):
        o_ref[0] = w_ref[...].astype(BF16)

    return pl.pallas_call(
        body, name="cast_into_slot",
        grid_spec=pltpu.PrefetchScalarGridSpec(
            num_scalar_prefetch=1, grid=(r // tr,),
            in_specs=[pl.BlockSpec((tr, c), lambda i, chip_ref: (i, 0))],
            out_specs=pl.BlockSpec((1, tr, c), lambda i, chip_ref: (chip_ref[0], i, 0))),
        out_shape=jax.ShapeDtypeStruct((N_CHIPS, r, c), BF16), compiler_params=_cparams(("parallel",)),
    )(chip, w2d)


def _place():
    x, y, c = lax.axis_index("x"), lax.axis_index("y"), lax.axis_index("c")
    return x, y, c, [(1 - x, y), (x, 1 - y), (1 - x, 1 - y)]


def _remote(src, dst, send_sem, recv_sem, to):
    return pltpu.make_async_remote_copy(src_ref=src, dst_ref=dst, send_sem=send_sem, recv_sem=recv_sem,
                                        device_id=to, device_id_type=MESH)


class _AllGather:
    def __init__(self, bufs):
        self.arrays = list(bufs)
        self.outs = [jax.ShapeDtypeStruct(b.shape, b.dtype) for b in bufs]
        self.aliased = True
        self.n_sems = 6 * len(bufs)

    def run(self, phase, ins, outs, send_sems, recv_sems, base):
        x, y, c, chips = _place()
        j = 2 * x + y
        for k, buf in enumerate(outs):
            half = buf.shape[1] // 2

            def part(src_j, h):
                return buf.at[src_j, pl.ds(h * half, half), :]

            def ici(r, src_j, to):
                s = base + 6 * k + r
                return _remote(part(src_j, c), part(src_j, c), send_sems.at[s], recv_sems.at[s], to)

            def d2d(r, src_j, h):
                s = base + 6 * k + 3 + r
                return _remote(part(src_j, h), part(src_j, h), send_sems.at[s], recv_sems.at[s], (x, y, 1 - c))

            for r, (px, py) in enumerate(chips):
                if phase == 0:
                    ici(r, j, (px, py, c)).start()
                elif phase == 1:
                    ici(r, 2 * px + py, (px, py, c)).wait_recv()
                    d2d(r, 2 * px + py, c).start()
                else:
                    d2d(r, 2 * px + py, 1 - c).wait_recv()
                    ici(r, j, (px, py, c)).wait_send()
                    d2d(r, 2 * px + py, c).wait_send()


class _Swap:
    def __init__(self, grads):
        self.arrays = list(grads)
        self.outs = [jax.ShapeDtypeStruct((g.shape[0],) + g.shape[2:], g.dtype) for g in grads]
        self.aliased = False
        self.n_sems = len(grads)

    def run(self, phase, ins, outs, send_sems, recv_sems, base):
        x, y, c, _ = _place()
        for k in range(len(ins)):
            cp = _remote(ins[k].at[:, 1 - c], outs[k], send_sems.at[base + k], recv_sems.at[base + k], (x, y, 1 - c))
            if phase == 0:
                cp.start()
            elif phase == 2:
                cp.wait()


class _Exchange:
    def __init__(self, parts):
        self.arrays = list(parts)
        self.outs = [jax.ShapeDtypeStruct((3,) + p.shape[1:], p.dtype) for p in parts]
        self.aliased = False
        self.n_sems = 3 * len(parts)

    def run(self, phase, ins, outs, send_sems, recv_sems, base):
        x, y, c, chips = _place()
        for k in range(len(ins)):
            for r, (px, py) in enumerate(chips):
                s = base + 3 * k + r
                cp = _remote(ins[k].at[2 * px + py], outs[k].at[r], send_sems.at[s], recv_sems.at[s], (px, py, c))
                if phase == 0:
                    cp.start()
                elif phase == 2:
                    cp.wait()


class _Join:
    def __init__(self, bufs):
        self.arrays = list(bufs)
        self.outs = [jax.ShapeDtypeStruct(b.shape, b.dtype) for b in bufs]
        self.aliased = True
        self.n_sems = len(bufs)

    def run(self, phase, ins, outs, send_sems, recv_sems, base):
        x, y, c, _ = _place()
        for k, buf in enumerate(outs):
            mine = _remote(buf.at[c], buf.at[c], send_sems.at[base + k], recv_sems.at[base + k], (x, y, 1 - c))
            if phase == 0:
                mine.start()
            elif phase == 2:
                mine.wait_send()
                _remote(buf.at[1 - c], buf.at[1 - c], send_sems.at[base + k], recv_sems.at[base + k],
                        (x, y, 1 - c)).wait_recv()


class _Comm:
    def __init__(self, ops):
        self.ops = list(ops)
        self.arrays = [a for op in self.ops for a in op.arrays]
        self.outs = [o for op in self.ops for o in op.outs]
        self.n_sems = sum(op.n_sems for op in self.ops)

    def specs(self):
        return [pl.BlockSpec(memory_space=pl.ANY)] * len(self.arrays)

    def aliases(self, first_in, first_out):
        out, k = {}, 0
        for op in self.ops:
            for i in range(len(op.arrays)):
                if op.aliased:
                    out[first_in + k + i] = first_out + k + i
            k += len(op.arrays)
        return out

    def scratch(self):
        return [pltpu.SemaphoreType.DMA((self.n_sems,)), pltpu.SemaphoreType.DMA((self.n_sems,))]

    def run(self, phase, ins, outs, send_sems, recv_sems):
        k = base = 0
        for op in self.ops:
            n = len(op.arrays)
            op.run(phase, ins[k:k + n], outs[k:k + n], send_sems, recv_sems, base)
            k += n
            base += op.n_sems

    def split(self, results):
        out, k = [], 0
        for op in self.ops:
            out.append(list(results[k:k + len(op.arrays)]))
            k += len(op.arrays)
        return out


def _communicate(ops):
    comm = _Comm(ops)
    n = len(comm.arrays)

    def body(*refs):
        ins, outs, (send_sems, recv_sems) = refs[:n], refs[n:2 * n], refs[2 * n:]
        for phase in range(3):
            comm.run(phase, ins, outs, send_sems, recv_sems)

    res = pl.pallas_call(
        body, name="communicate", out_shape=tuple(comm.outs), in_specs=comm.specs(), out_specs=tuple(comm.specs()),
        input_output_aliases=comm.aliases(0, 0), scratch_shapes=comm.scratch(),
    )(*comm.arrays)
    return comm.split(res)


def _pool_core(he, w_ref, scale, first_row, halo, n_rows):
    d = he.shape[1]
    gd = d // len(POOL_WINDOWS)
    t = first_row + lax.broadcasted_iota(jnp.int32, (n_rows, 1), 0)
    pooled, ypre, cnts = [], [], []
    for g, w in enumerate(POOL_WINDOWS):
        hg = he[:, g * gd:(g + 1) * gd]
        s, k = hg, 1
        while k < w:
            s = s + _shift_down(s, k)
            k *= 2
        cnt = jnp.minimum(t + 1, w).astype(F32)
        p = s[halo:] / cnt - hg[halo:]
        pooled.append(p.astype(BF16))
        cnts.append(cnt)
        ypre.append(_dot(pooled[-1], w_ref[g]))
    return pooled, jnp.concatenate(ypre, axis=1), cnts


def _pool_forward(x, vec, pool_w, comm=None):
    s, d = x.shape
    ts = _row_tile(s, 512)
    nb = s // ts
    n_g, gd, _ = pool_w.shape
    comm = comm or _Comm([])
    nc = len(comm.arrays)

    def body(*refs):
        x_ref, vec_ref, w_ref = refs[:3]
        cin = refs[3:3 + nc]
        o_ref = refs[3 + nc]
        cout = refs[4 + nc:4 + 2 * nc]
        carry = refs[4 + 2 * nc]
        sems = refs[5 + 2 * nc:]
        i = pl.program_id(0)

        @pl.when(i == 0)
        def _():
            carry[...] = jnp.zeros_like(carry)
            if nc:
                comm.run(0, cin, cout, *sems)

        if nc:
            @pl.when(i == nb - 1)
            def _():
                comm.run(1, cin, cout, *sems)

        vec = vec_ref[...]
        a, sh, gt, gpost = _vec_rows(vec)
        xb = x_ref[...]
        xn, _ = _rms(xb)
        h = xn * a + sh
        he = jnp.concatenate([carry[...], h], axis=0)
        carry[...] = h[ts - POOL_HALO:]
        _, ypre, _ = _pool_core(he, w_ref, vec[5:6], i * ts, POOL_HALO, ts)
        yn, _ = _rms(ypre * vec[5:6])
        o_ref[...] = xb + gt * (yn * gpost)
        if nc:
            @pl.when(i == nb - 1)
            def _():
                comm.run(2, cin, cout, *sems)

    res = pl.pallas_call(
        body, name="pool_forward", grid=(nb,),
        in_specs=[pl.BlockSpec((ts, d), lambda i: (i, 0)), pl.BlockSpec((8, d), lambda i: (0, 0)),
                  pl.BlockSpec((n_g, gd, gd), lambda i: (0, 0, 0))] + comm.specs(),
        out_specs=(pl.BlockSpec((ts, d), lambda i: (i, 0)), *comm.specs()),
        out_shape=(jax.ShapeDtypeStruct((s, d), F32), *comm.outs),
        input_output_aliases=comm.aliases(3, 1),
        scratch_shapes=[pltpu.VMEM((POOL_HALO, d), F32)] + (comm.scratch() if nc else []),
        compiler_params=_cparams(("arbitrary",)),
    )(x, vec, pool_w, *comm.arrays)
    return res[0], comm.split(res[1:])


def _pool_backward(dout, x, vec, pool_w):
    s, d = x.shape
    ts = _row_tile(s, 512)
    nb = s // ts
    hb = ts // POOL_HALO
    n_g, gd, _ = pool_w.shape

    def body(do_ref, x_ref, xh_ref, vec_ref, w_ref, dx_ref, sum_ref, dw_ref, carry):
        step = pl.program_id(0)
        i = nb - 1 - step

        @pl.when(step == 0)
        def _():
            carry[...] = jnp.zeros_like(carry)
            sum_ref[...] = jnp.zeros_like(sum_ref)
            dw_ref[...] = jnp.zeros_like(dw_ref)

        vec = vec_ref[...]
        a, sh, gt, gpost = _vec_rows(vec)
        scale = vec[5:6]
        do = do_ref[...]
        xe = jnp.concatenate([xh_ref[...], x_ref[...]], axis=0)
        xne, re = _rms(xe)
        he = xne * a + sh
        rowid = lax.broadcasted_iota(jnp.int32, (POOL_HALO + ts, 1), 0)
        he = jnp.where((rowid >= POOL_HALO) | (i > 0), he, 0.0)
        xn, r = xne[POOL_HALO:], re[POOL_HALO:]
        pooled, ypre, cnts = _pool_core(he, w_ref, scale, i * ts, POOL_HALO, ts)
        yn, ry = _rms(ypre * scale)
        dyn = do * (gt * gpost)
        dy = _rms_bwd(dyn, yn, ry)
        dypre = (dy * scale).astype(BF16)
        dh_parts, q_parts = [], []
        for g, w in enumerate(POOL_WINDOWS):
            dyg = dypre[:, g * gd:(g + 1) * gd]
            dpool = _dot_nt(dyg, w_ref[g])
            dw_ref[g] += _dot_tn(pooled[g], dyg)
            q = dpool / cnts[g]
            qe = jnp.concatenate([q, carry[:, g * gd:(g + 1) * gd]], axis=0)
            acc, k = qe, 1
            while k < w:
                acc = acc + _shift_up(acc, k)
                k *= 2
            dh_parts.append(acc[:ts] - dpool)
            q_parts.append(q[:POOL_HALO])
        carry[...] = jnp.concatenate(q_parts, axis=1)
        dh = jnp.concatenate(dh_parts, axis=1)
        dxn = dh * a
        dx_ref[...] = do + _rms_bwd(dxn, xn, r)
        _add_rows(sum_ref, _norm_sums(do, yn, dh, xn, vec) + [_colsum(dh), _colsum(dy * ypre)])

    blk = lambda st: (nb - 1 - st, 0)
    return pl.pallas_call(
        body, name="pool_backward", grid=(nb,),
        in_specs=[pl.BlockSpec((ts, d), blk), pl.BlockSpec((ts, d), blk),
                  pl.BlockSpec((POOL_HALO, d), lambda st: (jnp.maximum((nb - 1 - st) * hb - 1, 0), 0)),
                  pl.BlockSpec((8, d), lambda st: (0, 0)), pl.BlockSpec((n_g, gd, gd), lambda st: (0, 0, 0))],
        out_specs=(pl.BlockSpec((ts, d), blk), pl.BlockSpec((8, d), lambda st: (0, 0)),
                   pl.BlockSpec((n_g, gd, gd), lambda st: (0, 0, 0))),
        out_shape=(jax.ShapeDtypeStruct((s, d), F32), jax.ShapeDtypeStruct((8, d), F32),
                   jax.ShapeDtypeStruct((n_g, gd, gd), F32)),
        scratch_shapes=[pltpu.VMEM((POOL_HALO, d), F32)],
        compiler_params=_cparams(("arbitrary",)),
    )(dout, x, x, vec, pool_w)


def _ffn_forward(x, vec, w_up, w_dw, w_down, comm=None, target=None):
    s, d = x.shape
    _, _, cs = w_up.shape
    ts = _row_tile(s, 256)
    nb = s // ts
    chunks = _lane_chunks(cs)
    comm = comm or _Comm([])
    nc = len(comm.arrays)
    nl = 0 if target is None else 1
    n_in, n_out = 5 + nl, 6 + nl

    def body(*refs):
        x_ref, vec_ref, wup_ref, wdw_ref, wdn_ref = refs[:5]
        cin = refs[n_in:n_in + nc]
        o_ref, h_ref, a0_ref, cc_ref, u_ref, y_ref = refs[n_in + nc:n_in + nc + 6]
        loss_ref = refs[n_in + nc + 6] if nl else None
        cout = refs[n_in + nc + n_out:n_in + 2 * nc + n_out]
        carry = refs[n_in + 2 * nc + n_out]
        sems = refs[n_in + 2 * nc + n_out + 1:]
        i = pl.program_id(0)

        @pl.when(i == 0)
        def _():
            carry[...] = jnp.zeros_like(carry)
            if nl:
                loss_ref[...] = jnp.zeros_like(loss_ref)
            if nc:
                comm.run(0, cin, cout, *sems)

        if nc:
            @pl.when(i == (3 * nb) // 4)
            def _():
                comm.run(1, cin, cout, *sems)

        vec = vec_ref[...]
        a, sh, gt, gpost = _vec_rows(vec)
        xb = x_ref[...]
        xn, _ = _rms(xb)
        hb = (xn * a + sh).astype(BF16)
        h_ref[...] = hb
        for q in range(2):
            for c0, cw in chunks:
                conv = []
                for j in (q, q + 2):
                    a0 = _dot(hb, wup_ref[j, :, c0:c0 + cw])
                    a0_ref[j, :, c0:c0 + cw] = a0.astype(BF16)
                    ae = jnp.concatenate([carry[j, :, c0:c0 + cw], a0], axis=0)
                    carry[j, :, c0:c0 + cw] = a0[ts - FFN_HALO:]
                    w = wdw_ref[:, j * cs + c0:j * cs + c0 + cw]
                    conv.append((w[2:3] * ae + w[1:2] * _shift_down(ae, 1) + w[0:1] * _shift_down(ae, 2))[FFN_HALO:])
                    cc_ref[j, :, c0:c0 + cw] = conv[-1].astype(BF16)
                u_ref[q, :, c0:c0 + cw] = (conv[0] * _sigmoid(conv[0]) * conv[1]).astype(BF16)
        y = _dot(u_ref[0], wdn_ref[0]) + _dot(u_ref[1], wdn_ref[1])
        y_ref[...] = y
        yn, _ = _rms(y)
        x_out = xb + gt * (yn * gpost)
        if nl:
            err = x_out - refs[5][...]
            o_ref[...] = err * (1.0 / d)
            loss_ref[0:1, :] += _colsum(err * err) * (0.5 / d)
        else:
            o_ref[...] = x_out
        if nc:
            @pl.when(i == nb - 1)
            def _():
                comm.run(2, cin, cout, *sems)

    const3 = lambda i: (0, 0, 0)
    res = pl.pallas_call(
        body, name="ffn_forward", grid=(nb,),
        in_specs=[pl.BlockSpec((ts, d), lambda i: (i, 0)), pl.BlockSpec((8, d), lambda i: (0, 0)),
                  pl.BlockSpec(w_up.shape, const3, pipeline_mode=pl.Buffered(1)),
                  pl.BlockSpec(w_dw.shape, lambda i: (0, 0)),
                  pl.BlockSpec(w_down.shape, const3, pipeline_mode=pl.Buffered(1))]
        + [pl.BlockSpec((ts, d), lambda i: (i, 0))] * nl + comm.specs(),
        out_specs=(pl.BlockSpec((ts, d), lambda i: (i, 0)), pl.BlockSpec((ts, d), lambda i: (i, 0)),
                   pl.BlockSpec((4, ts, cs), lambda i: (0, i, 0)), pl.BlockSpec((4, ts, cs), lambda i: (0, i, 0)),
                   pl.BlockSpec((2, ts, cs), lambda i: (0, i, 0)), pl.BlockSpec((ts, d), lambda i: (i, 0)),
                   *[pl.BlockSpec((8, d), lambda i: (0, 0))] * nl, *comm.specs()),
        out_shape=(jax.ShapeDtypeStruct((s, d), F32), jax.ShapeDtypeStruct((s, d), BF16),
                   jax.ShapeDtypeStruct((4, s, cs), BF16), jax.ShapeDtypeStruct((4, s, cs), BF16),
                   jax.ShapeDtypeStruct((2, s, cs), BF16), jax.ShapeDtypeStruct((s, d), F32),
                   *[jax.ShapeDtypeStruct((8, d), F32)] * nl, *comm.outs),
        input_output_aliases=comm.aliases(n_in, n_out),
        scratch_shapes=[pltpu.VMEM((4, FFN_HALO, cs), F32)] + (comm.scratch() if nc else []),
        compiler_params=_cparams(("arbitrary",)),
    )(x, vec, w_up, w_dw, w_down, *([target] * nl), *comm.arrays)
    return res[:n_out], comm.split(res[n_out:])


def _ffn_backward(dout, x, y, a0, cc, vec, w_up, w_dw, w_down):
    s, d = x.shape
    _, _, cs = w_up.shape
    ts = _row_tile(s, 256)
    nb = s // ts
    chunks = _lane_chunks(cs)

    def body(do_ref, x_ref, y_ref, a0_ref, cc_ref, vec_ref, wup_ref, wdw_ref, wdn_ref,
             dx_ref, da0_ref, dy_ref, sum_ref, dwdw_ref, carry):
        step = pl.program_id(0)

        @pl.when(step == 0)
        def _():
            carry[...] = jnp.zeros_like(carry)
            sum_ref[...] = jnp.zeros_like(sum_ref)
            dwdw_ref[...] = jnp.zeros_like(dwdw_ref)

        vec = vec_ref[...]
        a, sh, gt, gpost = _vec_rows(vec)
        do = do_ref[...]
        yn, ry = _rms(y_ref[...])
        dy = _rms_bwd(do * (gt * gpost), yn, ry)
        dyb = dy.astype(BF16)
        dy_ref[...] = dyb
        for q in range(2):
            for c0, cw in chunks:
                cg = cc_ref[q, :, c0:c0 + cw].astype(F32)
                cv = cc_ref[q + 2, :, c0:c0 + cw].astype(F32)
                sg = _sigmoid(cg)
                sl = cg * sg
                du = _dot_nt(dyb, wdn_ref[q, c0:c0 + cw, :])
                dconv = {q: du * cv * (sg * (1.0 + cg * (1.0 - sg))), q + 2: du * sl}
                for j in (q, q + 2):
                    dae = jnp.concatenate([dconv[j], carry[j, :, c0:c0 + cw]], axis=0)
                    carry[j, :, c0:c0 + cw] = dconv[j][:FFN_HALO]
                    up1 = _shift_down(dae, FFN_HALO - 1)[FFN_HALO:]
                    up2 = _shift_down(dae, FFN_HALO - 2)[FFN_HALO:]
                    lanes = slice(j * cs + c0, j * cs + c0 + cw)
                    w = wdw_ref[:, lanes]
                    da0_ref[j, :, c0:c0 + cw] = (w[2:3] * dconv[j] + w[1:2] * up1 + w[0:1] * up2).astype(BF16)
                    a0 = a0_ref[j, :, c0:c0 + cw].astype(F32)
                    dwdw_ref[0:1, lanes] += _colsum(up2 * a0)
                    dwdw_ref[1:2, lanes] += _colsum(up1 * a0)
                    dwdw_ref[2:3, lanes] += _colsum(dconv[j] * a0)
        dh = _dot_nt(da0_ref[0], wup_ref[0])
        for j in range(1, 4):
            dh = dh + _dot_nt(da0_ref[j], wup_ref[j])
        xn, r = _rms(x_ref[...])
        dx_ref[...] = do + _rms_bwd(dh * a, xn, r)
        _add_rows(sum_ref, _norm_sums(do, yn, dh, xn, vec) + [_colsum(dh)])

    blk = lambda st: (nb - 1 - st, 0)
    blk3 = lambda st: (0, nb - 1 - st, 0)
    const3 = lambda st: (0, 0, 0)
    return pl.pallas_call(
        body, name="ffn_backward", grid=(nb,),
        in_specs=[pl.BlockSpec((ts, d), blk), pl.BlockSpec((ts, d), blk), pl.BlockSpec((ts, d), blk),
                  pl.BlockSpec((4, ts, cs), blk3), pl.BlockSpec((4, ts, cs), blk3),
                  pl.BlockSpec((8, d), lambda st: (0, 0)),
                  pl.BlockSpec(w_up.shape, const3, pipeline_mode=pl.Buffered(1)),
                  pl.BlockSpec(w_dw.shape, lambda st: (0, 0)),
                  pl.BlockSpec(w_down.shape, const3, pipeline_mode=pl.Buffered(1))],
        out_specs=(pl.BlockSpec((ts, d), blk), pl.BlockSpec((4, ts, cs), blk3),
                   pl.BlockSpec((ts, d), blk), pl.BlockSpec((8, d), lambda st: (0, 0)),
                   pl.BlockSpec((8, 4 * cs), lambda st: (0, 0))),
        out_shape=(jax.ShapeDtypeStruct((s, d), F32), jax.ShapeDtypeStruct((4, s, cs), BF16),
                   jax.ShapeDtypeStruct((s, d), BF16),
                   jax.ShapeDtypeStruct((8, d), F32), jax.ShapeDtypeStruct((8, 4 * cs), F32)),
        scratch_shapes=[pltpu.VMEM((4, FFN_HALO, cs), F32)],
        compiler_params=_cparams(("arbitrary",)),
    )(dout, x, y, a0, cc, vec, w_up, w_dw, w_down)


def _conv_halo(width):
    return -(-(width - 1) // 8) * 8


def _conv_forward(x, vec, cvec, w_pw1, b_pw1, w_dw, w_pw2):
    s, d = x.shape
    kw = w_dw.shape[0]
    halo = _conv_halo(kw)
    ts = _row_tile(s, 256)
    hd = d // 2

    def body(x_ref, vec_ref, cvec_ref, w1_ref, b1_ref, wdw_ref, w2_ref,
             o_ref, h_ref, a_ref, uc_ref, z_ref, y_ref, carry):
        i = pl.program_id(0)

        @pl.when(i == 0)
        def _():
            carry[...] = jnp.zeros_like(carry)

        vec, cvec = vec_ref[...], cvec_ref[...]
        a, sh, gt, gpost = _vec_rows(vec)
        xb = x_ref[...]
        xn, _ = _rms(xb)
        hb = (xn * a + sh).astype(BF16)
        h_ref[...] = hb
        for j in range(4):
            a_ref[:, j * hd:(j + 1) * hd] = _dot(hb, w1_ref[j]) + b1_ref[:, j * hd:(j + 1) * hd]
        u = a_ref[:, :d] * _sigmoid(a_ref[:, d:])
        carry[halo:, :] = u
        for r0 in range(0, ts, CONV_ROWS):
            for l0 in range(0, d, CONV_LANES):
                lanes = slice(l0, l0 + CONV_LANES)
                src = carry[r0:r0 + CONV_ROWS + halo, lanes]
                acc = jnp.zeros((CONV_ROWS, CONV_LANES), F32) + cvec[0:1, lanes]
                for k in range(kw):
                    acc = acc + wdw_ref[k:k + 1, lanes] * _shift_down(src, kw - 1 - k)[halo:]
                uc_ref[r0:r0 + CONV_ROWS, lanes] = acc
        carry[:halo, :] = u[ts - halo:]
        uc = uc_ref[...]
        mu = jnp.mean(uc, axis=-1, keepdims=True)
        cen = uc - mu
        rstd = lax.rsqrt(jnp.mean(cen * cen, axis=-1, keepdims=True) + EPS)
        l = cen * rstd * cvec[1:2] + cvec[2:3]
        zb = (l * _sigmoid(l)).astype(BF16)
        z_ref[...] = zb
        y = _dot(zb, w2_ref[...]) + cvec[3:4]
        y_ref[...] = y
        yn, _ = _rms(y)
        o_ref[...] = xb + gt * (yn * gpost)

    row = lambda i: (i, 0)
    const2 = lambda i: (0, 0)
    return pl.pallas_call(
        body, name="conv_forward", grid=(s // ts,),
        in_specs=[pl.BlockSpec((ts, d), row), pl.BlockSpec((8, d), const2), pl.BlockSpec((8, d), const2),
                  pl.BlockSpec(w_pw1.shape, lambda i: (0, 0, 0)), pl.BlockSpec(b_pw1.shape, const2),
                  pl.BlockSpec(w_dw.shape, const2), pl.BlockSpec(w_pw2.shape, const2)],
        out_specs=(pl.BlockSpec((ts, d), row), pl.BlockSpec((ts, d), row), pl.BlockSpec((ts, 2 * d), row),
                   pl.BlockSpec((ts, d), row), pl.BlockSpec((ts, d), row), pl.BlockSpec((ts, d), row)),
        out_shape=(jax.ShapeDtypeStruct((s, d), F32), jax.ShapeDtypeStruct((s, d), BF16),
                   jax.ShapeDtypeStruct((s, 2 * d), F32), jax.ShapeDtypeStruct((s, d), F32),
                   jax.ShapeDtypeStruct((s, d), BF16), jax.ShapeDtypeStruct((s, d), F32)),
        scratch_shapes=[pltpu.VMEM((halo + ts, d), F32)],
        compiler_params=_cparams(("arbitrary",)),
    )(x, vec, cvec, w_pw1, b_pw1, w_dw, w_pw2)


def _conv_backward(dout, x, y, a_pre, uc, vec, cvec, w_pw1, w_dw, w_pw2):
    s, d = x.shape
    kw = w_dw.shape[0]
    kpad = -(-kw // 8) * 8
    halo = _conv_halo(kw)
    ts = _row_tile(s, 256)
    nb = s // ts
    hb = ts // halo
    hd = d // 2

    def body(do_ref, x_ref, y_ref, a_ref, ah_ref, uc_ref, vec_ref, cvec_ref, w1_ref, wdw_ref, w2_ref,
             dx_ref, da_ref, dy_ref, sum_ref, dwdw_ref, carry):
        step = pl.program_id(0)
        i = nb - 1 - step

        @pl.when(step == 0)
        def _():
            carry[...] = jnp.zeros_like(carry)
            sum_ref[...] = jnp.zeros_like(sum_ref)
            dwdw_ref[...] = jnp.zeros_like(dwdw_ref)

        vec, cvec = vec_ref[...], cvec_ref[...]
        a, sh, gt, gpost = _vec_rows(vec)
        do = do_ref[...]
        yn, ry = _rms(y_ref[...])
        dy = _rms_bwd(do * (gt * gpost), yn, ry)
        dyb = dy.astype(BF16)
        dy_ref[...] = dyb
        dz = _dot_nt(dyb, w2_ref[...])
        uc = uc_ref[...]
        mu = jnp.mean(uc, axis=-1, keepdims=True)
        cen = uc - mu
        rstd = lax.rsqrt(jnp.mean(cen * cen, axis=-1, keepdims=True) + EPS)
        lhat = cen * rstd
        l = lhat * cvec[1:2] + cvec[2:3]
        sgl = _sigmoid(l)
        dl = dz * (sgl * (1.0 + l * (1.0 - sgl)))
        dlhat = dl * cvec[1:2]
        duc = rstd * (dlhat - jnp.mean(dlhat, axis=-1, keepdims=True)
                      - lhat * jnp.mean(dlhat * lhat, axis=-1, keepdims=True))
        ae = jnp.concatenate([ah_ref[...] * (i > 0).astype(F32), a_ref[...]], axis=0)
        sgate = _sigmoid(ae[:, d:])
        val = ae[:, :d]
        ue = val * sgate
        rowid = lax.broadcasted_iota(jnp.int32, (halo + ts, 1), 0)
        ue = jnp.where((rowid >= halo) | (i > 0), ue, 0.0)
        duce = jnp.concatenate([duc, carry[...]], axis=0)
        carry[...] = duc[:halo]
        du = jnp.zeros((ts, d), F32)
        for k in range(kw):
            du = du + wdw_ref[k:k + 1, :] * _shift_down(duce, halo - (kw - 1 - k))[halo:]
            dwdw_ref[k:k + 1, :] += _colsum(duc * _shift_down(ue, kw - 1 - k)[halo:])
        sg, vl = sgate[halo:], val[halo:]
        dval = du * sg
        dgate = du * vl * (sg * (1.0 - sg))
        dvb, dgb = dval.astype(BF16), dgate.astype(BF16)
        dh = jnp.zeros((ts, d), F32)
        for j in range(2):
            da_ref[j] = dvb[:, j * hd:(j + 1) * hd]
            da_ref[j + 2] = dgb[:, j * hd:(j + 1) * hd]
            dh = dh + _dot_nt(dvb[:, j * hd:(j + 1) * hd], w1_ref[j]) + _dot_nt(dgb[:, j * hd:(j + 1) * hd], w1_ref[j + 2])
        xn, r = _rms(x_ref[...])
        dx_ref[...] = do + _rms_bwd(dh * a, xn, r)
        _add_rows(sum_ref, _norm_sums(do, yn, dh, xn, vec) + [_colsum(dh), _colsum(dy), _colsum(dl * lhat), _colsum(dl),
                            _colsum(duc), _colsum(dval), _colsum(dgate)])

    blk = lambda st: (nb - 1 - st, 0)
    const2 = lambda st: (0, 0)
    return pl.pallas_call(
        body, name="conv_backward", grid=(nb,),
        in_specs=[pl.BlockSpec((ts, d), blk), pl.BlockSpec((ts, d), blk), pl.BlockSpec((ts, d), blk),
                  pl.BlockSpec((ts, 2 * d), blk),
                  pl.BlockSpec((halo, 2 * d), lambda st: (jnp.maximum((nb - 1 - st) * hb - 1, 0), 0)),
                  pl.BlockSpec((ts, d), blk), pl.BlockSpec((8, d), const2), pl.BlockSpec((8, d), const2),
                  pl.BlockSpec(w_pw1.shape, lambda st: (0, 0, 0)), pl.BlockSpec(w_dw.shape, const2),
                  pl.BlockSpec(w_pw2.shape, const2)],
        out_specs=(pl.BlockSpec((ts, d), blk), pl.BlockSpec((4, ts, hd), lambda st: (0, nb - 1 - st, 0)),
                   pl.BlockSpec((ts, d), blk), pl.BlockSpec((16, d), const2), pl.BlockSpec((kpad, d), const2)),
        out_shape=(jax.ShapeDtypeStruct((s, d), F32), jax.ShapeDtypeStruct((4, s, hd), BF16),
                   jax.ShapeDtypeStruct((s, d), BF16), jax.ShapeDtypeStruct((16, d), F32),
                   jax.ShapeDtypeStruct((kpad, d), F32)),
        scratch_shapes=[pltpu.VMEM((halo, d), F32)],
        compiler_params=_cparams(("arbitrary",)),
    )(dout, x, y, a_pre, a_pre, uc, vec, cvec, w_pw1, w_dw, w_pw2)


def _weight_grad(a, b, comm=None):
    na, s, k = a.shape
    nb_, _, n = b.shape
    nj = max(na, nb_)
    ts = _row_tile(s, 2048)
    nt = s // ts
    comm = comm or _Comm([])
    nc = len(comm.arrays)

    def body(*refs):
        a_ref, b_ref = refs[:2]
        cin = refs[2:2 + nc]
        o_ref = refs[2 + nc]
        cout = refs[3 + nc:3 + 2 * nc]
        sems = refs[3 + 2 * nc:]
        j, t = pl.program_id(0), pl.program_id(1)

        if nc:
            @pl.when((j == 0) & (t == 0))
            def _():
                comm.run(0, cin, cout, *sems)

            @pl.when((j == nj // 2) & (t == nt // 2))
            def _():
                comm.run(1, cin, cout, *sems)

        @pl.when(t == 0)
        def _():
            o_ref[...] = jnp.zeros_like(o_ref)

        o_ref[0] += _dot_tn(a_ref[0], b_ref[0])

        if nc:
            @pl.when((j == nj - 1) & (t == nt - 1))
            def _():
                comm.run(2, cin, cout, *sems)

    res = pl.pallas_call(
        body, name="weight_grad", grid=(nj, nt),
        in_specs=[pl.BlockSpec((1, ts, k), (lambda j, t: (j, t, 0)) if na > 1 else (lambda j, t: (0, t, 0))),
                  pl.BlockSpec((1, ts, n), (lambda j, t: (j, t, 0)) if nb_ > 1 else (lambda j, t: (0, t, 0)))]
        + comm.specs(),
        out_specs=(pl.BlockSpec((1, k, n), lambda j, t: (j, 0, 0)), *comm.specs()),
        out_shape=(jax.ShapeDtypeStruct((nj, k, n), F32), *comm.outs),
        input_output_aliases=comm.aliases(2, 1),
        scratch_shapes=comm.scratch() if nc else [],
        compiler_params=_cparams(("arbitrary", "arbitrary") if nc else ("parallel", "arbitrary")),
    )(a, b, *comm.arrays)
    return res[0], comm.split(res[1:])


def _adamw(w, g, m, v):
    nl, r, c = w.shape
    tr = _row_tile(r, 256)
    c1 = 1.0 / (1.0 - ADAM_B1 ** ADAM_STEP)
    c2 = 1.0 / (1.0 - ADAM_B2 ** ADAM_STEP)

    def body(w_ref, g_ref, m_ref, v_ref, d_ref, nm_ref, nv_ref):
        g_ = g_ref[...]
        nm = ADAM_B1 * m_ref[...] + (1.0 - ADAM_B1) * g_
        nv = ADAM_B2 * v_ref[...] + (1.0 - ADAM_B2) * (g_ * g_)
        nm_ref[...] = nm
        nv_ref[...] = nv
        d_ref[...] = -ADAM_LR * ((nm * c1) / (jnp.sqrt(nv * c2) + ADAM_EPS) + ADAM_WD * w_ref[...])

    spec = pl.BlockSpec((1, tr, c), lambda l, i: (l, i, 0))
    shp = jax.ShapeDtypeStruct((nl, r, c), F32)
    return pl.pallas_call(
        body, name="adamw", grid=(nl, r // tr), in_specs=[spec] * 4, out_specs=(spec,) * 3, out_shape=(shp,) * 3,
        compiler_params=_cparams(("parallel", "parallel")),
    )(w, g, m, v)


def _add_my_half(g, other, idx):
    _, _, h, c = g.shape
    th = _row_tile(h, 256)

    def body(idx_ref, g_ref, o_ref, out_ref):
        out_ref[...] = (g_ref[:, 0] + o_ref[...]).astype(BF16)

    return pl.pallas_call(
        body, name="add_my_half",
        grid_spec=pltpu.PrefetchScalarGridSpec(
            num_scalar_prefetch=1, grid=(4, h // th),
            in_specs=[pl.BlockSpec((1, 1, th, c), lambda j, i, idx_ref: (j, idx_ref[1], i, 0)),
                      pl.BlockSpec((1, th, c), lambda j, i, idx_ref: (j, i, 0))],
            out_specs=pl.BlockSpec((1, th, c), lambda j, i, idx_ref: (j, i, 0))),
        out_shape=jax.ShapeDtypeStruct(other.shape, BF16),
        compiler_params=_cparams(("parallel", "parallel")),
    )(idx, g, other)


def _sum_for_my_chip(g, other, got, idx):
    _, _, h, c = g.shape
    th = _row_tile(h, 256)

    def body(idx_ref, g_ref, o_ref, q_ref, out_ref):
        out_ref[0] = (((g_ref[0, 0] + o_ref[0]) + q_ref[0].astype(F32)) + q_ref[1].astype(F32)) + q_ref[2].astype(F32)

    return pl.pallas_call(
        body, name="sum_for_my_chip",
        grid_spec=pltpu.PrefetchScalarGridSpec(
            num_scalar_prefetch=1, grid=(h // th,),
            in_specs=[pl.BlockSpec((1, 1, th, c), lambda i, idx_ref: (idx_ref[0], idx_ref[1], i, 0)),
                      pl.BlockSpec((1, th, c), lambda i, idx_ref: (idx_ref[0], i, 0)),
                      pl.BlockSpec((3, th, c), lambda i, idx_ref: (0, i, 0))],
            out_specs=pl.BlockSpec((1, th, c), lambda i, idx_ref: (idx_ref[1], i, 0))),
        out_shape=jax.ShapeDtypeStruct((2, h, c), F32),
        compiler_params=_cparams(("parallel",)),
    )(idx, g, other, got)


class _Reducer:
    def __init__(self, idx):
        self.idx = idx
        self.groups = []

    def add(self, grads):
        group = {"state": 0, "g": [g.reshape(4, 2, g.shape[1] // 2, g.shape[2]) for g in grads]}
        self.groups.append(group)
        return group

    def steps(self):
        ops, owners = [], []
        for gr in self.groups:
            if gr["state"] == 0:
                ops.append(_Swap(gr["g"]))
            elif gr["state"] == 1:
                ops.append(_Exchange(gr["parts"]))
            elif gr["state"] == 2:
                ops.append(_Join(gr["bufs"]))
            else:
                continue
            owners.append(gr)
        return ops, owners

    def absorb(self, owners, results):
        for gr, res in zip(owners, results):
            if gr["state"] == 0:
                gr["other"] = res
                gr["parts"] = [_add_my_half(g, o, self.idx) for g, o in zip(gr["g"], res)]
            elif gr["state"] == 1:
                gr["bufs"] = [_sum_for_my_chip(g, o, q, self.idx) for g, o, q in zip(gr["g"], gr["other"], res)]
            else:
                gr["full"] = [b.reshape(2 * b.shape[1], b.shape[2]) for b in res]
            gr["state"] += 1

    def drain(self):
        while any(gr["state"] < 3 for gr in self.groups):
            ops, owners = self.steps()
            self.absorb(owners, _communicate(ops))


def _all_gather_rows(block):
    m, n = block.shape

    def body(x_ref, out_ref, send_sems, recv_sems):
        x, y, c = lax.axis_index("x"), lax.axis_index("y"), lax.axis_index("c")
        me, sibling = (x, y, c), (x, y, 1 - c)
        chips = [(1 - x, y), (x, 1 - y), (1 - x, 1 - y)]

        def slot(px, py, pc):
            return out_ref.at[4 * px + 2 * py + pc]

        def copy(k, block_of, to, src=None):
            return pltpu.make_async_remote_copy(
                src_ref=slot(*block_of) if src is None else src, dst_ref=slot(*block_of),
                send_sem=send_sems.at[k], recv_sem=recv_sems.at[k], device_id=to, device_id_type=MESH)

        out_ref[4 * x + 2 * y + c] = x_ref[...]
        first = [copy(0, me, sibling, src=x_ref)]
        first += [copy(1 + r, me, (*chip, c), src=x_ref) for r, chip in enumerate(chips)]
        for cp in first:
            cp.start()
        passed = [copy(4 + r, (*chip, c), sibling) for r, chip in enumerate(chips)]
        for r, chip in enumerate(chips):
            copy(1 + r, (*chip, c), me).wait_recv()
            passed[r].start()
        copy(0, sibling, me).wait_recv()
        for r, chip in enumerate(chips):
            copy(4 + r, (*chip, 1 - c), me).wait_recv()
        for cp in first + passed:
            cp.wait_send()

    vm = pl.BlockSpec(memory_space=pltpu.VMEM)
    return pl.pallas_call(
        body, name="all_gather_rows",
        out_shape=jax.ShapeDtypeStruct((N_DEV, m, n), block.dtype),
        in_specs=[vm], out_specs=vm,
        scratch_shapes=[pltpu.SemaphoreType.DMA((7,)), pltpu.SemaphoreType.DMA((7,))],
        compiler_params=_cparams(),
    )(block)


def _sum_devices(gathered):
    nd, m, n = gathered.shape

    def body(g_ref, o_ref):
        acc = g_ref[0]
        for b in range(1, nd):
            acc = acc + g_ref[b]
        o_ref[...] = acc

    return pl.pallas_call(
        body, name="sum_devices", out_shape=jax.ShapeDtypeStruct((m, n), F32),
        in_specs=[pl.BlockSpec(memory_space=pltpu.VMEM)], out_specs=pl.BlockSpec(memory_space=pltpu.VMEM),
        compiler_params=_cparams(),
    )(gathered)


def _ada_weight_grad(c_all, dmod_cols):
    nl, nd, ncol = dmod_cols.shape
    d = c_all.shape[1]

    def body(c_ref, dm_ref, o_ref):
        o_ref[0] = lax.dot_general(c_ref[...], dm_ref[0], (((0,), (0,)), ((), ())),
                                   preferred_element_type=F32, precision=lax.Precision.HIGHEST)

    return pl.pallas_call(
        body, name="ada_weight_grad", grid=(nl,),
        in_specs=[pl.BlockSpec((nd, d), lambda l: (0, 0)), pl.BlockSpec((1, nd, ncol), lambda l: (l, 0, 0))],
        out_specs=pl.BlockSpec((1, d, ncol), lambda l: (l, 0, 0)),
        out_shape=jax.ShapeDtypeStruct((nl, d, ncol), F32), compiler_params=_cparams(("parallel",)),
    )(c_all, dmod_cols)


def _pad_rows(a, rows):
    return jnp.pad(a, ((0, rows - a.shape[0]), (0, 0)))


def _shard_cols(full, chip, width):
    return lax.dynamic_slice_in_dim(full, chip * width, width, axis=full.ndim - 1)


def kernel(x, c, ada_w, ada_b, pre_g, post_g, pool_w, pool_scale, cv_w_pw1, cv_b_pw1, cv_w_dw, cv_b_dw, cv_ln_g, cv_ln_b, cv_w_pw2, cv_b_pw2, ffn_w_up, ffn_w_dw, ffn_w_down, loss_target, m_ada_w, m_ada_b, m_pre_g, m_post_g, m_pool_w, m_pool_scale, m_cv_w_pw1, m_cv_b_pw1, m_cv_w_dw, m_cv_b_dw, m_cv_ln_g, m_cv_ln_b, m_cv_w_pw2, m_cv_b_pw2, m_ffn_w_up, m_ffn_w_dw, m_ffn_w_down, v_ada_w, v_ada_b, v_pre_g, v_post_g, v_pool_w, v_pool_scale, v_cv_w_pw1, v_cv_b_pw1, v_cv_w_dw, v_cv_b_dw, v_cv_ln_g, v_cv_ln_b, v_cv_w_pw2, v_cv_b_pw2, v_ffn_w_up, v_ffn_w_dw, v_ffn_w_down):
    s, d = x.shape[1], x.shape[2]
    dq = d // N_CHIPS
    n_g = pool_w.shape[1]
    gq = pool_w.shape[2]
    gd = pool_w.shape[3]
    kw = cv_w_dw.shape[1]
    cs = ffn_w_up.shape[2]
    fq = ffn_w_down.shape[1]
    chip = 2 * lax.axis_index("x") + lax.axis_index("y")
    core = lax.axis_index("c")
    chip1 = jnp.reshape(chip, (1,)).astype(jnp.int32)
    core1 = jnp.reshape(core, (1,)).astype(jnp.int32)
    xs, tgt = x[0], loss_target[0]

    c_rep, mod_rep = _ada_forward(c, ada_w)
    c_all = c_rep[:, 0, :]
    mod = mod_rep[:, :, 0, :].transpose(1, 0, 2).reshape(ada_b.shape) + ada_b

    small_rows = [pre_g.reshape(4, dq), post_g.reshape(4, dq), cv_w_dw[0], cv_b_dw, cv_ln_g, cv_ln_b, cv_b_pw2,
                  cv_b_pw1.reshape(2, dq)]
    small = jnp.concatenate(small_rows, axis=0)
    n_small = small.shape[0]
    small = _pad_rows(small, -(-n_small // 16) * 16)
    dwf = _pad_rows(ffn_w_dw.reshape(6, cs), 16)
    def slot(a):
        return lax.dynamic_update_slice_in_dim(jnp.zeros((N_CHIPS,) + a.shape, a.dtype), a[None], chip, axis=0)

    first = [_cast_into_slot(pool_w.reshape(n_g * gq, gd), chip1), slot(small), slot(dwf)]
    (g_pool, g_small, g_dwf), = _communicate([_AllGather(first)])
    second = _AllGather([_cast_into_slot(ffn_w_up[0], chip1), _cast_into_slot(ffn_w_down[0], chip1)])
    later = _AllGather([_cast_into_slot(cv_w_pw1[0], chip1), _cast_into_slot(cv_w_pw2[0], chip1),
                        _cast_into_slot(ffn_w_up[1], chip1), _cast_into_slot(ffn_w_down[1], chip1)])
    poolw_full = g_pool.reshape(N_CHIPS, n_g, gq, gd).transpose(1, 0, 2, 3).reshape(n_g, gd, gd)
    smallf = g_small.transpose(1, 0, 2).reshape(g_small.shape[1], d)
    pre_full, post_full = smallf[0:4].reshape(2, 2, d), smallf[4:8].reshape(2, 2, d)
    wdw31 = smallf[8:8 + kw]
    o = 8 + kw
    b_dw, ln_g, ln_b, b_pw2 = smallf[o:o + 1], smallf[o + 1:o + 2], smallf[o + 2:o + 3], smallf[o + 3:o + 4]
    b_pw1 = g_small[:, o + 4:o + 6, :].reshape(1, 2 * d)
    ffn_dw = g_dwf[:, :6, :].transpose(1, 0, 2).reshape(2, 3, N_CHIPS * cs)

    def sub_vec(layer, sub, extra=None):
        m6 = mod[layer].reshape(6, d)
        rows = [pre_full[layer, sub][None], 1.0 + m6[3 * sub + 1][None], m6[3 * sub][None], m6[3 * sub + 2][None],
                post_full[layer, sub][None]]
        if extra is not None:
            rows.append(extra)
        return _pad_rows(jnp.concatenate(rows, axis=0), 8)

    vec_pool = sub_vec(0, 0, pool_scale)
    vec_f0, vec_conv, vec_f1 = sub_vec(0, 1), sub_vec(1, 0), sub_vec(1, 1)
    cvec = _pad_rows(jnp.concatenate([b_dw, ln_g, ln_b, b_pw2], axis=0), 8)

    x1, ((g_up0, g_dn0),) = _pool_forward(xs, vec_pool, poolw_full, _Comm([second]))
    w_up0, w_dn0 = g_up0, g_dn0.reshape(2, 2 * fq, d)
    (x2, h_f0, a0_f0, cc_f0, u_f0, y_f0), ((g_pw1, g_pw2, g_up1, g_dn1),) = _ffn_forward(
        x1, vec_f0, w_up0, ffn_dw[0], w_dn0, _Comm([later]))
    pw2_full = g_pw2.reshape(d, d)
    w_up1, w_dn1 = g_up1, g_dn1.reshape(2, 2 * fq, d)
    x3, h_cv, a_cv, uc_cv, z_cv, y_cv = _conv_forward(x2, vec_conv, cvec, g_pw1, b_pw1, wdw31, pw2_full)
    (dx4, h_f1, a0_f1, cc_f1, u_f1, y_f1, loss_rows), _ = _ffn_forward(x3, vec_f1, w_up1, ffn_dw[1], w_dn1, target=tgt)

    dx3, da0_f1, dy_f1, sum_f1, dwdw_f1 = _ffn_backward(dx4, x3, y_f1, a0_f1, cc_f1, vec_f1, w_up1, ffn_dw[1], w_dn1)
    dx2, da_cv, dy_cv, sum_cv, dwdw_cv = _conv_backward(dx3, x2, y_cv, a_cv, uc_cv, vec_conv, cvec, g_pw1, wdw31, pw2_full)
    dx1, da0_f0, dy_f0, sum_f0, dwdw_f0 = _ffn_backward(dx2, x1, y_f0, a0_f0, cc_f0, vec_f0, w_up0, ffn_dw[0], w_dn0)
    dx0, sum_pool, gw_pool = _pool_backward(dx1, xs, vec_pool, poolw_full)
    gw_pool4 = gw_pool.reshape(n_g, N_CHIPS, gq, gd).transpose(1, 0, 2, 3).reshape(N_CHIPS, n_g * gq, gd)

    red = _Reducer(jnp.concatenate([chip1, core1]))

    def grad_stage(a, b, shape=None):
        ops, owners = red.steps()
        gw, results = _weight_grad(a, b, _Comm(ops))
        red.absorb(owners, results)
        return gw if shape is None else gw.reshape(shape)

    r_up1 = red.add([grad_stage(h_f1[None], da0_f1)])
    r_up0 = red.add([grad_stage(h_f0[None], da0_f0)])
    r_dn1 = red.add([grad_stage(u_f1, dy_f1[None], (N_CHIPS, fq, d))])
    r_dn0 = red.add([grad_stage(u_f0, dy_f0[None], (N_CHIPS, fq, d))])
    r_pw1 = red.add([grad_stage(h_cv[None], da_cv)])
    r_last = red.add([grad_stage(z_cv[None], dy_cv[None], (N_CHIPS, dq, d)), gw_pool4])
    red.drain()
    r_up1, r_up0, r_dn1, r_dn0, r_pw1 = [gr["full"][0] for gr in (r_up1, r_up0, r_dn1, r_dn0, r_pw1)]
    r_pw2, r_pool = r_last["full"]

    slab = jnp.concatenate([sum_f1, sum_cv, dwdw_cv, sum_f0, sum_pool, loss_rows], axis=0)
    wide = jnp.concatenate([dwdw_f1, dwdw_f0], axis=0)
    n_slab = slab.shape[0]
    both_all = _all_gather_rows(jnp.concatenate([slab, wide.reshape(-1, d)], axis=0))
    tot_both = _sum_devices(both_all)
    slab_all, tot = both_all[:, :n_slab], tot_both[:n_slab]
    tot_wide = tot_both[n_slab:].reshape(wide.shape)
    kpad = dwdw_cv.shape[0]
    o_cv, o_dw, o_f0 = 8, 24, 24 + kpad
    o_pool, o_loss = o_f0 + 8, o_f0 + 16
    loss = jnp.sum(tot[o_loss])
    dmod_l0 = jnp.concatenate([slab_all[:, o_pool + 4], slab_all[:, o_pool + 3], slab_all[:, o_pool + 1],
                               slab_all[:, o_f0 + 4], slab_all[:, o_f0 + 3], slab_all[:, o_f0 + 1]], axis=-1)
    dmod_l1 = jnp.concatenate([slab_all[:, o_cv + 4], slab_all[:, o_cv + 3], slab_all[:, o_cv + 1],
                               slab_all[:, 4], slab_all[:, 3], slab_all[:, 1]], axis=-1)
    dmod = jnp.stack([dmod_l0, dmod_l1], axis=0)
    g_ada_b = _sum_devices(dmod.transpose(1, 0, 2))
    ncol = ada_w.shape[2]
    g_ada_w = _ada_weight_grad(c_all, _shard_cols(dmod, chip, ncol))

    g_pre = jnp.stack([jnp.stack([tot[o_pool + 2], tot[o_f0 + 2]]), jnp.stack([tot[o_cv + 2], tot[2]])])
    g_post = jnp.stack([jnp.stack([tot[o_pool + 0], tot[o_f0 + 0]]), jnp.stack([tot[o_cv + 0], tot[0]])])
    g_pool_scale = tot[o_pool + 5][None]
    g_b_pw2, g_ln_g, g_ln_b, g_b_dw = tot[o_cv + 5], tot[o_cv + 6], tot[o_cv + 7], tot[o_cv + 8]
    g_b_pw1 = jnp.concatenate([tot[o_cv + 9], tot[o_cv + 10]])
    g_w_dw31 = tot[o_dw:o_dw + kw]
    g_ffn_dw = jnp.stack([tot_wide[8:11], tot_wide[0:3]])

    grads_small = {
        "pre_g": _shard_cols(g_pre, chip, dq), "post_g": _shard_cols(g_post, chip, dq),
        "pool_scale": g_pool_scale, "cv_b_pw1": _shard_cols(g_b_pw1[None], chip, 2 * dq),
        "cv_w_dw": _shard_cols(g_w_dw31[None], chip, dq), "cv_b_dw": _shard_cols(g_b_dw[None], chip, dq),
        "cv_ln_g": _shard_cols(g_ln_g[None], chip, dq), "cv_ln_b": _shard_cols(g_ln_b[None], chip, dq),
        "cv_b_pw2": _shard_cols(g_b_pw2[None], chip, dq), "ffn_w_dw": _shard_cols(g_ffn_dw, chip, cs),
        "ada_b": g_ada_b,
    }
    params_small = {
        "pre_g": (pre_g, m_pre_g, v_pre_g), "post_g": (post_g, m_post_g, v_post_g),
        "pool_scale": (pool_scale, m_pool_scale, v_pool_scale), "cv_b_pw1": (cv_b_pw1, m_cv_b_pw1, v_cv_b_pw1),
        "cv_w_dw": (cv_w_dw, m_cv_w_dw, v_cv_w_dw), "cv_b_dw": (cv_b_dw, m_cv_b_dw, v_cv_b_dw),
        "cv_ln_g": (cv_ln_g, m_cv_ln_g, v_cv_ln_g), "cv_ln_b": (cv_ln_b, m_cv_ln_b, v_cv_ln_b),
        "cv_b_pw2": (cv_b_pw2, m_cv_b_pw2, v_cv_b_pw2), "ffn_w_dw": (ffn_w_dw, m_ffn_w_dw, v_ffn_w_dw),
        "ada_b": (ada_b, m_ada_b, v_ada_b),
    }
    names = list(params_small)
    sizes = [params_small[nm][0].size for nm in names]
    padded = [-(-sz // 1024) * 1024 for sz in sizes]

    def pack(arrs):
        flat = [jnp.pad(a.reshape(-1), (0, p - a.size)) for a, p in zip(arrs, padded)]
        return jnp.concatenate(flat).reshape(1, -1, 128)

    pk_w = pack([params_small[nm][0] for nm in names])
    pk_m = pack([params_small[nm][1] for nm in names])
    pk_v = pack([params_small[nm][2] for nm in names])
    pk_g = pack([grads_small[nm].reshape(params_small[nm][0].shape) for nm in names])
    pk_d, pk_nm, pk_nv = _adamw(pk_w, pk_g, pk_m, pk_v)

    def unpack(pk):
        flat, out, off = pk.reshape(-1), {}, 0
        for nm, sz, p in zip(names, sizes, padded):
            out[nm] = flat[off:off + sz].reshape(params_small[nm][0].shape)
            off += p
        return out

    small_d, small_m, small_v = unpack(pk_d), unpack(pk_nm), unpack(pk_nv)
    small_g = {nm: grads_small[nm].reshape(params_small[nm][0].shape) for nm in names}

    big_g = {
        "ada_w": g_ada_w,
        "pool_w": r_pool.reshape(pool_w.shape),
        "cv_w_pw1": r_pw1[None], "cv_w_pw2": r_pw2[None],
        "ffn_w_up": jnp.stack([r_up0, r_up1]), "ffn_w_down": jnp.stack([r_dn0, r_dn1]),
    }
    big_p = {
        "ada_w": (ada_w, m_ada_w, v_ada_w), "pool_w": (pool_w, m_pool_w, v_pool_w),
        "cv_w_pw1": (cv_w_pw1, m_cv_w_pw1, v_cv_w_pw1), "cv_w_pw2": (cv_w_pw2, m_cv_w_pw2, v_cv_w_pw2),
        "ffn_w_up": (ffn_w_up, m_ffn_w_up, v_ffn_w_up), "ffn_w_down": (ffn_w_down, m_ffn_w_down, v_ffn_w_down),
    }
    big_d, big_m, big_v = {}, {}, {}
    for nm, (w, m, v) in big_p.items():
        shp = w.shape
        as3 = lambda t: t.reshape((-1,) + shp[-2:])
        dl, nm_, nv_ = _adamw(as3(w), as3(big_g[nm]), as3(m), as3(v))
        big_d[nm], big_m[nm], big_v[nm] = dl.reshape(shp), nm_.reshape(shp), nv_.reshape(shp)

    order = ["ada_w", "ada_b", "pre_g", "post_g", "pool_w", "pool_scale", "cv_w_pw1", "cv_b_pw1", "cv_w_dw", "cv_b_dw",
             "cv_ln_g", "cv_ln_b", "cv_w_pw2", "cv_b_pw2", "ffn_w_up", "ffn_w_dw", "ffn_w_down"]
    pick = lambda bigs, smalls: [bigs[nm] if nm in bigs else smalls[nm] for nm in order]
    return (loss, dx0[None], *pick(big_g, small_g), *pick(big_d, small_d), *pick(big_m, small_m),
            *pick(big_v, small_v))
```

```python
import functools

import jax
import jax.numpy as jnp
from jax import lax
from jax.experimental import pallas as pl
from jax.experimental.pallas import tpu as pltpu

F32 = jnp.float32
BF16 = jnp.bfloat16
EPS = 1e-6
N_CHIPS = 4
N_DEV = 8
POOL_WINDOWS = (2, 4, 8, 16)
POOL_HALO = 16
FFN_HALO = 16
MXU_LANES = 256
CONV_ROWS, CONV_LANES = 128, 128
ADAM_LR = 0.001
ADAM_B1 = 0.9
ADAM_B2 = 0.999
ADAM_EPS = 1e-08
ADAM_WD = 0.01
ADAM_STEP = 10
V7X_VMEM_LIMIT = 58 * 1024 * 1024
MESH = pl.DeviceIdType.MESH


def _cparams(sem=None, vmem=V7X_VMEM_LIMIT):
    return pltpu.CompilerParams(dimension_semantics=sem, vmem_limit_bytes=vmem)


def _row_tile(n, want):
    if n <= want:
        return n
    t = want - want % 8
    while n % t:
        t -= 8
    return t


def _lane_chunks(width):
    out, c = [], 0
    while c < width:
        w = min(512, width - c)
        out.append((c, w))
        c += w
    return out


def _dot(a, b):
    return jnp.dot(a, b, preferred_element_type=F32)


def _dot_nt(a, b):
    return lax.dot_general(a, b, (((1,), (1,)), ((), ())), preferred_element_type=F32)


def _store_dot_nt(dst, a_ref, b_ref):
    dst[...] = _dot_nt(a_ref[...], b_ref[...])


def _store_dot_nt2(dst, a1_ref, a2_ref, b1_ref, b2_ref):
    dst[...] = _dot_nt(a1_ref[...], b1_ref[...]) + _dot_nt(a2_ref[...], b2_ref[...])


def _dot_tn(a, b):
    return lax.dot_general(a, b, (((0,), (0,)), ((), ())), preferred_element_type=F32)


def _rms(x):
    r = lax.rsqrt(jnp.mean(x * x, axis=-1, keepdims=True) + EPS)
    return x * r, r


def _rms_bwd(dyn, yn, r):
    return r * (dyn - yn * jnp.mean(dyn * yn, axis=-1, keepdims=True))


def _sigmoid(x):
    return 0.5 * jnp.tanh(0.5 * x) + 0.5


def _colsum(x):
    return jnp.sum(x, axis=0, keepdims=True)


def _shift_down(x, k):
    return x if k == 0 else pltpu.roll(x, k, 0)


def _shift_up(x, k):
    return x if k == 0 else pltpu.roll(x, x.shape[0] - k, 0)


def _vec_rows(vec):
    return vec[0:1] * vec[1:2], vec[2:3], vec[3:4], vec[4:5]


def _norm_sums(do, yn, dh, xn, vec):
    p, q = _colsum(do * yn), _colsum(dh * xn)
    return [p * vec[3:4], p * vec[4:5], q * vec[1:2], q * vec[0:1]]


def _add_rows(sum_ref, rows):
    for k, r in enumerate(rows):
        sum_ref[k:k + 1, :] += r


def _ada_forward(c, ada_w):
    n_layers, d, ncol = ada_w.shape

    def body(c_ref, w_ref, call_ref, mod_ref, part_ref, sendbuf, send_sems, recv_sems, send2, recv2):
        x, y, cc = lax.axis_index("x"), lax.axis_index("y"), lax.axis_index("c")
        me = 4 * x + 2 * y + cc
        rel = [(x, y, 1 - cc), (1 - x, y, cc), (x, 1 - y, cc), (1 - x, 1 - y, cc),
               (1 - x, y, 1 - cc), (x, 1 - y, 1 - cc), (1 - x, 1 - y, 1 - cc)]
        cv = c_ref[...]
        call_ref[me] = jnp.broadcast_to(cv * _sigmoid(cv), (8, d))

        def gather(k, block, to):
            blk = call_ref.at[block]
            return pltpu.make_async_remote_copy(src_ref=blk, dst_ref=blk, send_sem=send_sems.at[k],
                                                recv_sem=recv_sems.at[k], device_id=to, device_id_type=MESH)

        for k, to in enumerate(rel):
            gather(k, me, to).start()
        for k, (px, py, pc) in enumerate(rel):
            gather(k, 4 * px + 2 * py + pc, rel[k]).wait_recv()
        for k, to in enumerate(rel):
            gather(k, me, to).wait_send()

        ca = call_ref[...].reshape(8 * N_DEV, d)
        for l in range(n_layers):
            part_ref[l] = jnp.dot(ca, w_ref[l], preferred_element_type=F32, precision=lax.Precision.HIGHEST)

        j = 2 * x + y
        chips = [(1 - x, y), (x, 1 - y), (1 - x, 1 - y)]

        def rows_of(b):
            return part_ref[:, pl.ds(pl.multiple_of(8 * b, 8), 8), :]

        def scatter(k, src_j, to):
            return pltpu.make_async_remote_copy(
                src_ref=sendbuf.at[k], dst_ref=mod_ref.at[src_j], send_sem=send2.at[k], recv_sem=recv2.at[k],
                device_id=to, device_id_type=MESH)

        mod_ref[j] = rows_of(me)
        for k, (px, py) in enumerate(chips):
            sendbuf[k] = rows_of(4 * px + 2 * py + cc)
            scatter(k, j, (px, py, cc)).start()
        for k, (px, py) in enumerate(chips):
            scatter(k, 2 * px + py, (px, py, cc)).wait_recv()
        for k, (px, py) in enumerate(chips):
            scatter(k, j, (px, py, cc)).wait_send()

    vm = pl.BlockSpec(memory_space=pltpu.VMEM)
    return pl.pallas_call(
        body, name="ada_forward",
        out_shape=(jax.ShapeDtypeStruct((N_DEV, 8, d), F32), jax.ShapeDtypeStruct((N_CHIPS, n_layers, 8, ncol), F32)),
        in_specs=[vm, vm], out_specs=(vm, vm),
        scratch_shapes=[pltpu.VMEM((n_layers, 8 * N_DEV, ncol), F32), pltpu.VMEM((3, n_layers, 8, ncol), F32),
                        pltpu.SemaphoreType.DMA((7,)), pltpu.SemaphoreType.DMA((7,)),
                        pltpu.SemaphoreType.DMA((3,)), pltpu.SemaphoreType.DMA((3,))],
        compiler_params=_cparams(),
    )(c, ada_w)


def _cast_into_slot(w2d, chip):
    r, c = w2d.shape
    tr = _row_tile(r, 256)

    def body(chip_ref, w_ref, o_ref):
        o_ref[0] = w_ref[...].astype(BF16)

    return pl.pallas_call(
        body, name="cast_into_slot",
        grid_spec=pltpu.PrefetchScalarGridSpec(
            num_scalar_prefetch=1, grid=(r // tr,),
            in_specs=[pl.BlockSpec((tr, c), lambda i, chip_ref: (i, 0))],
            out_specs=pl.BlockSpec((1, tr, c), lambda i, chip_ref: (chip_ref[0], i, 0))),
        out_shape=jax.ShapeDtypeStruct((N_CHIPS, r, c), BF16), compiler_params=_cparams(("parallel",)),
    )(chip, w2d)


def _place():
    x, y, c = lax.axis_index("x"), lax.axis_index("y"), lax.axis_index("c")
    return x, y, c, [(1 - x, y), (x, 1 - y), (1 - x, 1 - y)]


def _remote(src, dst, send_sem, recv_sem, to):
    return pltpu.make_async_remote_copy(src_ref=src, dst_ref=dst, send_sem=send_sem, recv_sem=recv_sem,
                                        device_id=to, device_id_type=MESH)


class _AllGather:
    def __init__(self, bufs):
        self.arrays = list(bufs)
        self.outs = [jax.ShapeDtypeStruct(b.shape, b.dtype) for b in bufs]
        self.aliased = True
        self.n_sems = 6 * len(bufs)

    def run(self, phase, ins, outs, send_sems, recv_sems, base):
        x, y, c, chips = _place()
        j = 2 * x + y
        for k, buf in enumerate(outs):
            half = buf.shape[1] // 2

            def part(src_j, h):
                return buf.at[src_j, pl.ds(h * half, half), :]

            def ici(r, src_j, to):
                s = base + 6 * k + r
                return _remote(part(src_j, c), part(src_j, c), send_sems.at[s], recv_sems.at[s], to)

            def d2d(r, src_j, h):
                s = base + 6 * k + 3 + r
                return _remote(part(src_j, h), part(src_j, h), send_sems.at[s], recv_sems.at[s], (x, y, 1 - c))

            for r, (px, py) in enumerate(chips):
                if phase == 0:
                    ici(r, j, (px, py, c)).start()
                elif phase == 1:
                    ici(r, 2 * px + py, (px, py, c)).wait_recv()
                    d2d(r, 2 * px + py, c).start()
                else:
                    d2d(r, 2 * px + py, 1 - c).wait_recv()
                    ici(r, j, (px, py, c)).wait_send()
                    d2d(r, 2 * px + py, c).wait_send()


class _Swap:
    def __init__(self, grads):
        self.arrays = list(grads)
        self.outs = [jax.ShapeDtypeStruct((g.shape[0],) + g.shape[2:], g.dtype) for g in grads]
        self.aliased = False
        self.n_sems = len(grads)

    def run(self, phase, ins, outs, send_sems, recv_sems, base):
        x, y, c, _ = _place()
        for k in range(len(ins)):
            cp = _remote(ins[k].at[:, 1 - c], outs[k], send_sems.at[base + k], recv_sems.at[base + k], (x, y, 1 - c))
            if phase == 0:
                cp.start()
            elif phase == 2:
                cp.wait()


class _Exchange:
    def __init__(self, parts):
        self.arrays = list(parts)
        self.outs = [jax.ShapeDtypeStruct((3,) + p.shape[1:], p.dtype) for p in parts]
        self.aliased = False
        self.n_sems = 3 * len(parts)

    def run(self, phase, ins, outs, send_sems, recv_sems, base):
        x, y, c, chips = _place()
        for k in range(len(ins)):
            for r, (px, py) in enumerate(chips):
                s = base + 3 * k + r
                cp = _remote(ins[k].at[2 * px + py], outs[k].at[r], send_sems.at[s], recv_sems.at[s], (px, py, c))
                if phase == 0:
                    cp.start()
                elif phase == 2:
                    cp.wait()


class _Join:
    def __init__(self, bufs):
        self.arrays = list(bufs)
        self.outs = [jax.ShapeDtypeStruct(b.shape, b.dtype) for b in bufs]
        self.aliased = True
        self.n_sems = len(bufs)

    def run(self, phase, ins, outs, send_sems, recv_sems, base):
        x, y, c, _ = _place()
        for k, buf in enumerate(outs):
            mine = _remote(buf.at[c], buf.at[c], send_sems.at[base + k], recv_sems.at[base + k], (x, y, 1 - c))
            if phase == 0:
                mine.start()
            elif phase == 2:
                mine.wait_send()
                _remote(buf.at[1 - c], buf.at[1 - c], send_sems.at[base + k], recv_sems.at[base + k],
                        (x, y, 1 - c)).wait_recv()


class _Comm:
    def __init__(self, ops):
        self.ops = list(ops)
        self.arrays = [a for op in self.ops for a in op.arrays]
        self.outs = [o for op in self.ops for o in op.outs]
        self.n_sems = sum(op.n_sems for op in self.ops)

    def specs(self):
        return [pl.BlockSpec(memory_space=pl.ANY)] * len(self.arrays)

    def aliases(self, first_in, first_out):
        out, k = {}, 0
        for op in self.ops:
            for i in range(len(op.arrays)):
                if op.aliased:
                    out[first_in + k + i] = first_out + k + i
            k += len(op.arrays)
        return out

    def scratch(self):
        return [pltpu.SemaphoreType.DMA((self.n_sems,)), pltpu.SemaphoreType.DMA((self.n_sems,))]

    def run(self, phase, ins, outs, send_sems, recv_sems):
        k = base = 0
        for op in self.ops:
            n = len(op.arrays)
            op.run(phase, ins[k:k + n], outs[k:k + n], send_sems, recv_sems, base)
            k += n
            base += op.n_sems

    def split(self, results):
        out, k = [], 0
        for op in self.ops:
            out.append(list(results[k:k + len(op.arrays)]))
            k += len(op.arrays)
        return out


def _communicate(ops):
    comm = _Comm(ops)
    n = len(comm.arrays)

    def body(*refs):
        ins, outs, (send_sems, recv_sems) = refs[:n], refs[n:2 * n], refs[2 * n:]
        for phase in range(3):
            comm.run(phase, ins, outs, send_sems, recv_sems)

    res = pl.pallas_call(
        body, name="communicate", out_shape=tuple(comm.outs), in_specs=comm.specs(), out_specs=tuple(comm.specs()),
        input_output_aliases=comm.aliases(0, 0), scratch_shapes=comm.scratch(),
    )(*comm.arrays)
    return comm.split(res)


def _pool_core(he, w_ref, scale, first_row, halo, n_rows):
    d = he.shape[1]
    gd = d // len(POOL_WINDOWS)
    t = first_row + lax.broadcasted_iota(jnp.int32, (n_rows, 1), 0)
    pooled, ypre, cnts = [], [], []
    for g, w in enumerate(POOL_WINDOWS):
        hg = he[:, g * gd:(g + 1) * gd]
        s, k = hg, 1
        while k < w:
            s = s + _shift_down(s, k)
            k *= 2
        cnt = jnp.minimum(t + 1, w).astype(F32)
        p = s[halo:] / cnt - hg[halo:]
        pooled.append(p.astype(BF16))
        cnts.append(cnt)
        ypre.append(_dot(pooled[-1], w_ref[g]))
    return pooled, jnp.concatenate(ypre, axis=1), cnts


def _pool_forward(x, vec, pool_w, comm=None):
    s, d = x.shape
    ts = _row_tile(s, 512)
    nb = s // ts
    n_g, gd, _ = pool_w.shape
    comm = comm or _Comm([])
    nc = len(comm.arrays)

    def body(*refs):
        x_ref, vec_ref, w_ref = refs[:3]
        cin = refs[3:3 + nc]
        o_ref = refs[3 + nc]
        cout = refs[4 + nc:4 + 2 * nc]
        carry = refs[4 + 2 * nc]
        sems = refs[5 + 2 * nc:]
        i = pl.program_id(0)

        @pl.when(i == 0)
        def _():
            carry[...] = jnp.zeros_like(carry)
            if nc:
                comm.run(0, cin, cout, *sems)

        if nc:
            @pl.when(i == nb - 1)
            def _():
                comm.run(1, cin, cout, *sems)

        vec = vec_ref[...]
        a, sh, gt, gpost = _vec_rows(vec)
        xb = x_ref[...]
        xn, _ = _rms(xb)
        h = xn * a + sh
        he = jnp.concatenate([carry[...], h], axis=0)
        carry[...] = h[ts - POOL_HALO:]
        _, ypre, _ = _pool_core(he, w_ref, vec[5:6], i * ts, POOL_HALO, ts)
        yn, _ = _rms(ypre * vec[5:6])
        o_ref[...] = xb + gt * (yn * gpost)
        if nc:
            @pl.when(i == nb - 1)
            def _():
                comm.run(2, cin, cout, *sems)

    res = pl.pallas_call(
        body, name="pool_forward", grid=(nb,),
        in_specs=[pl.BlockSpec((ts, d), lambda i: (i, 0)), pl.BlockSpec((8, d), lambda i: (0, 0)),
                  pl.BlockSpec((n_g, gd, gd), lambda i: (0, 0, 0))] + comm.specs(),
        out_specs=(pl.BlockSpec((ts, d), lambda i: (i, 0)), *comm.specs()),
        out_shape=(jax.ShapeDtypeStruct((s, d), F32), *comm.outs),
        input_output_aliases=comm.aliases(3, 1),
        scratch_shapes=[pltpu.VMEM((POOL_HALO, d), F32)] + (comm.scratch() if nc else []),
        compiler_params=_cparams(("arbitrary",)),
    )(x, vec, pool_w, *comm.arrays)
    return res[0], comm.split(res[1:])


def _pool_backward(dout, x, vec, pool_w):
    s, d = x.shape
    ts = _row_tile(s, 512)
    nb = s // ts
    hb = ts // POOL_HALO
    n_g, gd, _ = pool_w.shape

    def body(do_ref, x_ref, xh_ref, vec_ref, w_ref, dx_ref, sum_ref, dw_ref, carry):
        step = pl.program_id(0)
        i = nb - 1 - step

        @pl.when(step == 0)
        def _():
            carry[...] = jnp.zeros_like(carry)
            sum_ref[...] = jnp.zeros_like(sum_ref)
            dw_ref[...] = jnp.zeros_like(dw_ref)

        vec = vec_ref[...]
        a, sh, gt, gpost = _vec_rows(vec)
        scale = vec[5:6]
        do = do_ref[...]
        xe = jnp.concatenate([xh_ref[...], x_ref[...]], axis=0)
        xne, re = _rms(xe)
        he = xne * a + sh
        rowid = lax.broadcasted_iota(jnp.int32, (POOL_HALO + ts, 1), 0)
        he = jnp.where((rowid >= POOL_HALO) | (i > 0), he, 0.0)
        xn, r = xne[POOL_HALO:], re[POOL_HALO:]
        pooled, ypre, cnts = _pool_core(he, w_ref, scale, i * ts, POOL_HALO, ts)
        yn, ry = _rms(ypre * scale)
        dyn = do * (gt * gpost)
        dy = _rms_bwd(dyn, yn, ry)
        dypre = (dy * scale).astype(BF16)
        dh_parts, q_parts = [], []
        for g, w in enumerate(POOL_WINDOWS):
            dyg = dypre[:, g * gd:(g + 1) * gd]
            dpool = _dot_nt(dyg, w_ref[g])
            dw_ref[g] += _dot_tn(pooled[g], dyg)
            q = dpool / cnts[g]
            qe = jnp.concatenate([q, carry[:, g * gd:(g + 1) * gd]], axis=0)
            acc, k = qe, 1
            while k < w:
                acc = acc + _shift_up(acc, k)
                k *= 2
            dh_parts.append(acc[:ts] - dpool)
            q_parts.append(q[:POOL_HALO])
        carry[...] = jnp.concatenate(q_parts, axis=1)
        dh = jnp.concatenate(dh_parts, axis=1)
        dxn = dh * a
        dx_ref[...] = do + _rms_bwd(dxn, xn, r)
        _add_rows(sum_ref, _norm_sums(do, yn, dh, xn, vec) + [_colsum(dh), _colsum(dy * ypre)])

    blk = lambda st: (nb - 1 - st, 0)
    return pl.pallas_call(
        body, name="pool_backward", grid=(nb,),
        in_specs=[pl.BlockSpec((ts, d), blk), pl.BlockSpec((ts, d), blk),
                  pl.BlockSpec((POOL_HALO, d), lambda st: (jnp.maximum((nb - 1 - st) * hb - 1, 0), 0)),
                  pl.BlockSpec((8, d), lambda st: (0, 0)), pl.BlockSpec((n_g, gd, gd), lambda st: (0, 0, 0))],
        out_specs=(pl.BlockSpec((ts, d), blk), pl.BlockSpec((8, d), lambda st: (0, 0)),
                   pl.BlockSpec((n_g, gd, gd), lambda st: (0, 0, 0))),
        out_shape=(jax.ShapeDtypeStruct((s, d), F32), jax.ShapeDtypeStruct((8, d), F32),
                   jax.ShapeDtypeStruct((n_g, gd, gd), F32)),
        scratch_shapes=[pltpu.VMEM((POOL_HALO, d), F32)],
        compiler_params=_cparams(("arbitrary",)),
    )(dout, x, x, vec, pool_w)


def _ffn_forward(x, vec, w_up, w_dw, w_down, comm=None, target=None):
    s, d = x.shape
    _, _, cs = w_up.shape
    ts = _row_tile(s, 256)
    nb = s // ts
    chunks = _lane_chunks(cs)
    comm = comm or _Comm([])
    nc = len(comm.arrays)
    nl = 0 if target is None else 1
    n_in, n_out = 5 + nl, 6 + nl

    def body(*refs):
        x_ref, vec_ref, wup_ref, wdw_ref, wdn_ref = refs[:5]
        cin = refs[n_in:n_in + nc]
        o_ref, h_ref, a0_ref, cc_ref, u_ref, y_ref = refs[n_in + nc:n_in + nc + 6]
        loss_ref = refs[n_in + nc + 6] if nl else None
        cout = refs[n_in + nc + n_out:n_in + 2 * nc + n_out]
        carry = refs[n_in + 2 * nc + n_out]
        sems = refs[n_in + 2 * nc + n_out + 1:]
        i = pl.program_id(0)

        @pl.when(i == 0)
        def _():
            carry[...] = jnp.zeros_like(carry)
            if nl:
                loss_ref[...] = jnp.zeros_like(loss_ref)
            if nc:
                comm.run(0, cin, cout, *sems)

        if nc:
            @pl.when(i == (3 * nb) // 4)
            def _():
                comm.run(1, cin, cout, *sems)

        vec = vec_ref[...]
        a, sh, gt, gpost = _vec_rows(vec)
        xb = x_ref[...]
        xn, _ = _rms(xb)
        hb = (xn * a + sh).astype(BF16)
        h_ref[...] = hb
        for q in range(2):
            for c0, cw in chunks:
                conv = []
                for j in (q, q + 2):
                    a0 = _dot(hb, wup_ref[j, :, c0:c0 + cw])
                    a0_ref[j, :, c0:c0 + cw] = a0.astype(BF16)
                    ae = jnp.concatenate([carry[j, :, c0:c0 + cw], a0], axis=0)
                    carry[j, :, c0:c0 + cw] = a0[ts - FFN_HALO:]
                    w = wdw_ref[:, j * cs + c0:j * cs + c0 + cw]
                    conv.append((w[2:3] * ae + w[1:2] * _shift_down(ae, 1) + w[0:1] * _shift_down(ae, 2))[FFN_HALO:])
                    cc_ref[j, :, c0:c0 + cw] = conv[-1].astype(BF16)
                u_ref[q, :, c0:c0 + cw] = (conv[0] * _sigmoid(conv[0]) * conv[1]).astype(BF16)
        y = _dot(u_ref[0], wdn_ref[0]) + _dot(u_ref[1], wdn_ref[1])
        y_ref[...] = y
        yn, _ = _rms(y)
        x_out = xb + gt * (yn * gpost)
        if nl:
            err = x_out - refs[5][...]
            o_ref[...] = err * (1.0 / d)
            loss_ref[0:1, :] += _colsum(err * err) * (0.5 / d)
        else:
            o_ref[...] = x_out
        if nc:
            @pl.when(i == nb - 1)
            def _():
                comm.run(2, cin, cout, *sems)

    const3 = lambda i: (0, 0, 0)
    res = pl.pallas_call(
        body, name="ffn_forward", grid=(nb,),
        in_specs=[pl.BlockSpec((ts, d), lambda i: (i, 0)), pl.BlockSpec((8, d), lambda i: (0, 0)),
                  pl.BlockSpec(w_up.shape, const3, pipeline_mode=pl.Buffered(1)),
                  pl.BlockSpec(w_dw.shape, lambda i: (0, 0)),
                  pl.BlockSpec(w_down.shape, const3, pipeline_mode=pl.Buffered(1))]
        + [pl.BlockSpec((ts, d), lambda i: (i, 0))] * nl + comm.specs(),
        out_specs=(pl.BlockSpec((ts, d), lambda i: (i, 0)), pl.BlockSpec((ts, d), lambda i: (i, 0)),
                   pl.BlockSpec((4, ts, cs), lambda i: (0, i, 0)), pl.BlockSpec((4, ts, cs), lambda i: (0, i, 0)),
                   pl.BlockSpec((2, ts, cs), lambda i: (0, i, 0)), pl.BlockSpec((ts, d), lambda i: (i, 0)),
                   *[pl.BlockSpec((8, d), lambda i: (0, 0))] * nl, *comm.specs()),
        out_shape=(jax.ShapeDtypeStruct((s, d), F32), jax.ShapeDtypeStruct((s, d), BF16),
                   jax.ShapeDtypeStruct((4, s, cs), BF16), jax.ShapeDtypeStruct((4, s, cs), BF16),
                   jax.ShapeDtypeStruct((2, s, cs), BF16), jax.ShapeDtypeStruct((s, d), F32),
                   *[jax.ShapeDtypeStruct((8, d), F32)] * nl, *comm.outs),
        input_output_aliases=comm.aliases(n_in, n_out),
        scratch_shapes=[pltpu.VMEM((4, FFN_HALO, cs), F32)] + (comm.scratch() if nc else []),
        compiler_params=_cparams(("arbitrary",)),
    )(x, vec, w_up, w_dw, w_down, *([target] * nl), *comm.arrays)
    return res[:n_out], comm.split(res[n_out:])


def _ffn_backward(dout, x, y, a0, cc, vec, w_up, w_dw, w_down):
    s, d = x.shape
    _, _, cs = w_up.shape
    ts = _row_tile(s, 256)
    nb = s // ts
    chunks = _lane_chunks(cs)

    def body(do_ref, x_ref, y_ref, a0_ref, cc_ref, vec_ref, wup_ref, wdw_ref, wdn_ref,
             dx_ref, da0_ref, dy_ref, sum_ref, dwdw_ref, carry, du_s, dh_s):
        step = pl.program_id(0)

        @pl.when(step == 0)
        def _():
            carry[...] = jnp.zeros_like(carry)
            sum_ref[...] = jnp.zeros_like(sum_ref)
            dwdw_ref[...] = jnp.zeros_like(dwdw_ref)

        vec = vec_ref[...]
        a, sh, gt, gpost = _vec_rows(vec)
        do = do_ref[...]
        yn, ry = _rms(y_ref[...])
        dy = _rms_bwd(do * (gt * gpost), yn, ry)
        dyb = dy.astype(BF16)
        dy_ref[...] = dyb
        order = [(q, c0, cw) for q in range(2) for c0, cw in chunks]

        def du_pieces(idx):
            q, c0, cw = order[idx]
            return [functools.partial(_store_dot_nt, du_s.at[idx % 2, :, n0:min(n0 + MXU_LANES, cw)], dy_ref,
                                      wdn_ref.at[q, c0 + n0:c0 + min(n0 + MXU_LANES, cw), :])
                    for n0 in range(0, cw, MXU_LANES)]

        def dh_pieces():
            return [functools.partial(_store_dot_nt2, dh_s.at[:, n0:n0 + MXU_LANES], da0_ref.at[0], da0_ref.at[2],
                                      wup_ref.at[0, n0:n0 + MXU_LANES, :], wup_ref.at[2, n0:n0 + MXU_LANES, :])
                    for n0 in range(0, d, MXU_LANES)]

        for piece in du_pieces(0):
            piece()
        later = dh_pieces()
        for idx, (q, c0, cw) in enumerate(order):
            work = du_pieces(idx + 1) if idx + 1 < len(order) else []
            if q == 1:
                share = -(-len(later) // (len(order) - idx))
                work, later = work + later[:share], later[share:]

            def pump(part, of=3):
                for piece in work[part::of]:
                    piece()

            cg = cc_ref[q, :, c0:c0 + cw].astype(F32)
            cv = cc_ref[q + 2, :, c0:c0 + cw].astype(F32)
            sg = _sigmoid(cg)
            sl = cg * sg
            du = du_s[idx % 2, :, :cw]
            dconv = {q: du * cv * (sg * (1.0 + cg * (1.0 - sg))), q + 2: du * sl}
            pump(0)
            for part, j in enumerate((q, q + 2)):
                dae = jnp.concatenate([dconv[j], carry[j, :, c0:c0 + cw]], axis=0)
                carry[j, :, c0:c0 + cw] = dconv[j][:FFN_HALO]
                up1 = _shift_down(dae, FFN_HALO - 1)[FFN_HALO:]
                up2 = _shift_down(dae, FFN_HALO - 2)[FFN_HALO:]
                lanes = slice(j * cs + c0, j * cs + c0 + cw)
                w = wdw_ref[:, lanes]
                da0_ref[j, :, c0:c0 + cw] = (w[2:3] * dconv[j] + w[1:2] * up1 + w[0:1] * up2).astype(BF16)
                a0 = a0_ref[j, :, c0:c0 + cw].astype(F32)
                dwdw_ref[0:1, lanes] += _colsum(up2 * a0)
                dwdw_ref[1:2, lanes] += _colsum(up1 * a0)
                dwdw_ref[2:3, lanes] += _colsum(dconv[j] * a0)
                pump(part + 1)
        dh = dh_s[...] + _dot_nt(da0_ref[1], wup_ref[1]) + _dot_nt(da0_ref[3], wup_ref[3])
        xn, r = _rms(x_ref[...])
        dx_ref[...] = do + _rms_bwd(dh * a, xn, r)
        _add_rows(sum_ref, _norm_sums(do, yn, dh, xn, vec) + [_colsum(dh)])

    blk = lambda st: (nb - 1 - st, 0)
    blk3 = lambda st: (0, nb - 1 - st, 0)
    const3 = lambda st: (0, 0, 0)
    return pl.pallas_call(
        body, name="ffn_backward", grid=(nb,),
        in_specs=[pl.BlockSpec((ts, d), blk), pl.BlockSpec((ts, d), blk), pl.BlockSpec((ts, d), blk),
                  pl.BlockSpec((4, ts, cs), blk3), pl.BlockSpec((4, ts, cs), blk3),
                  pl.BlockSpec((8, d), lambda st: (0, 0)),
                  pl.BlockSpec(w_up.shape, const3, pipeline_mode=pl.Buffered(1)),
                  pl.BlockSpec(w_dw.shape, lambda st: (0, 0)),
                  pl.BlockSpec(w_down.shape, const3, pipeline_mode=pl.Buffered(1))],
        out_specs=(pl.BlockSpec((ts, d), blk), pl.BlockSpec((4, ts, cs), blk3),
                   pl.BlockSpec((ts, d), blk), pl.BlockSpec((8, d), lambda st: (0, 0)),
                   pl.BlockSpec((8, 4 * cs), lambda st: (0, 0))),
        out_shape=(jax.ShapeDtypeStruct((s, d), F32), jax.ShapeDtypeStruct((4, s, cs), BF16),
                   jax.ShapeDtypeStruct((s, d), BF16),
                   jax.ShapeDtypeStruct((8, d), F32), jax.ShapeDtypeStruct((8, 4 * cs), F32)),
        scratch_shapes=[pltpu.VMEM((4, FFN_HALO, cs), F32), pltpu.VMEM((2, ts, max(cw for _, cw in chunks)), F32),
                        pltpu.VMEM((ts, d), F32)],
        compiler_params=_cparams(("arbitrary",)),
    )(dout, x, y, a0, cc, vec, w_up, w_dw, w_down)


def _conv_halo(width):
    return -(-(width - 1) // 8) * 8


def _conv_forward(x, vec, cvec, w_pw1, b_pw1, w_dw, w_pw2):
    s, d = x.shape
    kw = w_dw.shape[0]
    halo = _conv_halo(kw)
    ts = _row_tile(s, 256)
    hd = d // 2

    def body(x_ref, vec_ref, cvec_ref, w1_ref, b1_ref, wdw_ref, w2_ref,
             o_ref, h_ref, a_ref, uc_ref, z_ref, y_ref, carry):
        i = pl.program_id(0)

        @pl.when(i == 0)
        def _():
            carry[...] = jnp.zeros_like(carry)

        vec, cvec = vec_ref[...], cvec_ref[...]
        a, sh, gt, gpost = _vec_rows(vec)
        xb = x_ref[...]
        xn, _ = _rms(xb)
        hb = (xn * a + sh).astype(BF16)
        h_ref[...] = hb
        for j in range(4):
            a_ref[:, j * hd:(j + 1) * hd] = _dot(hb, w1_ref[j]) + b1_ref[:, j * hd:(j + 1) * hd]
        u = a_ref[:, :d] * _sigmoid(a_ref[:, d:])
        carry[halo:, :] = u
        for r0 in range(0, ts, CONV_ROWS):
            for l0 in range(0, d, CONV_LANES):
                lanes = slice(l0, l0 + CONV_LANES)
                src = carry[r0:r0 + CONV_ROWS + halo, lanes]
                acc = jnp.zeros((CONV_ROWS, CONV_LANES), F32) + cvec[0:1, lanes]
                for k in range(kw):
                    acc = acc + wdw_ref[k:k + 1, lanes] * _shift_down(src, kw - 1 - k)[halo:]
                uc_ref[r0:r0 + CONV_ROWS, lanes] = acc
        carry[:halo, :] = u[ts - halo:]
        uc = uc_ref[...]
        mu = jnp.mean(uc, axis=-1, keepdims=True)
        cen = uc - mu
        rstd = lax.rsqrt(jnp.mean(cen * cen, axis=-1, keepdims=True) + EPS)
        l = cen * rstd * cvec[1:2] + cvec[2:3]
        zb = (l * _sigmoid(l)).astype(BF16)
        z_ref[...] = zb
        y = _dot(zb, w2_ref[...]) + cvec[3:4]
        y_ref[...] = y
        yn, _ = _rms(y)
        o_ref[...] = xb + gt * (yn * gpost)

    row = lambda i: (i, 0)
    const2 = lambda i: (0, 0)
    return pl.pallas_call(
        body, name="conv_forward", grid=(s // ts,),
        in_specs=[pl.BlockSpec((ts, d), row), pl.BlockSpec((8, d), const2), pl.BlockSpec((8, d), const2),
                  pl.BlockSpec(w_pw1.shape, lambda i: (0, 0, 0)), pl.BlockSpec(b_pw1.shape, const2),
                  pl.BlockSpec(w_dw.shape, const2), pl.BlockSpec(w_pw2.shape, const2)],
        out_specs=(pl.BlockSpec((ts, d), row), pl.BlockSpec((ts, d), row), pl.BlockSpec((ts, 2 * d), row),
                   pl.BlockSpec((ts, d), row), pl.BlockSpec((ts, d), row), pl.BlockSpec((ts, d), row)),
        out_shape=(jax.ShapeDtypeStruct((s, d), F32), jax.ShapeDtypeStruct((s, d), BF16),
                   jax.ShapeDtypeStruct((s, 2 * d), F32), jax.ShapeDtypeStruct((s, d), F32),
                   jax.ShapeDtypeStruct((s, d), BF16), jax.ShapeDtypeStruct((s, d), F32)),
        scratch_shapes=[pltpu.VMEM((halo + ts, d), F32)],
        compiler_params=_cparams(("arbitrary",)),
    )(x, vec, cvec, w_pw1, b_pw1, w_dw, w_pw2)


def _conv_backward(dout, x, y, a_pre, uc, vec, cvec, w_pw1, w_dw, w_pw2):
    s, d = x.shape
    kw = w_dw.shape[0]
    kpad = -(-kw // 8) * 8
    halo = _conv_halo(kw)
    ts = _row_tile(s, 256)
    nb = s // ts
    hb = ts // halo
    hd = d // 2

    def body(do_ref, x_ref, y_ref, a_ref, ah_ref, uc_ref, vec_ref, cvec_ref, w1_ref, wdw_ref, w2_ref,
             dx_ref, da_ref, dy_ref, sum_ref, dwdw_ref, carry):
        step = pl.program_id(0)
        i = nb - 1 - step

        @pl.when(step == 0)
        def _():
            carry[...] = jnp.zeros_like(carry)
            sum_ref[...] = jnp.zeros_like(sum_ref)
            dwdw_ref[...] = jnp.zeros_like(dwdw_ref)

        vec, cvec = vec_ref[...], cvec_ref[...]
        a, sh, gt, gpost = _vec_rows(vec)
        do = do_ref[...]
        yn, ry = _rms(y_ref[...])
        dy = _rms_bwd(do * (gt * gpost), yn, ry)
        dyb = dy.astype(BF16)
        dy_ref[...] = dyb
        dz = _dot_nt(dyb, w2_ref[...])
        uc = uc_ref[...]
        mu = jnp.mean(uc, axis=-1, keepdims=True)
        cen = uc - mu
        rstd = lax.rsqrt(jnp.mean(cen * cen, axis=-1, keepdims=True) + EPS)
        lhat = cen * rstd
        l = lhat * cvec[1:2] + cvec[2:3]
        sgl = _sigmoid(l)
        dl = dz * (sgl * (1.0 + l * (1.0 - sgl)))
        dlhat = dl * cvec[1:2]
        duc = rstd * (dlhat - jnp.mean(dlhat, axis=-1, keepdims=True)
                      - lhat * jnp.mean(dlhat * lhat, axis=-1, keepdims=True))
        ae = jnp.concatenate([ah_ref[...] * (i > 0).astype(F32), a_ref[...]], axis=0)
        sgate = _sigmoid(ae[:, d:])
        val = ae[:, :d]
        ue = val * sgate
        rowid = lax.broadcasted_iota(jnp.int32, (halo + ts, 1), 0)
        ue = jnp.where((rowid >= halo) | (i > 0), ue, 0.0)
        duce = jnp.concatenate([duc, carry[...]], axis=0)
        carry[...] = duc[:halo]
        du = jnp.zeros((ts, d), F32)
        for k in range(kw):
            du = du + wdw_ref[k:k + 1, :] * _shift_down(duce, halo - (kw - 1 - k))[halo:]
            dwdw_ref[k:k + 1, :] += _colsum(duc * _shift_down(ue, kw - 1 - k)[halo:])
        sg, vl = sgate[halo:], val[halo:]
        dval = du * sg
        dgate = du * vl * (sg * (1.0 - sg))
        dvb, dgb = dval.astype(BF16), dgate.astype(BF16)
        dh = jnp.zeros((ts, d), F32)
        for j in range(2):
            da_ref[j] = dvb[:, j * hd:(j + 1) * hd]
            da_ref[j + 2] = dgb[:, j * hd:(j + 1) * hd]
            dh = dh + _dot_nt(dvb[:, j * hd:(j + 1) * hd], w1_ref[j]) + _dot_nt(dgb[:, j * hd:(j + 1) * hd], w1_ref[j + 2])
        xn, r = _rms(x_ref[...])
        dx_ref[...] = do + _rms_bwd(dh * a, xn, r)
        _add_rows(sum_ref, _norm_sums(do, yn, dh, xn, vec) + [_colsum(dh), _colsum(dy), _colsum(dl * lhat), _colsum(dl),
                            _colsum(duc), _colsum(dval), _colsum(dgate)])

    blk = lambda st: (nb - 1 - st, 0)
    const2 = lambda st: (0, 0)
    return pl.pallas_call(
        body, name="conv_backward", grid=(nb,),
        in_specs=[pl.BlockSpec((ts, d), blk), pl.BlockSpec((ts, d), blk), pl.BlockSpec((ts, d), blk),
                  pl.BlockSpec((ts, 2 * d), blk),
                  pl.BlockSpec((halo, 2 * d), lambda st: (jnp.maximum((nb - 1 - st) * hb - 1, 0), 0)),
                  pl.BlockSpec((ts, d), blk), pl.BlockSpec((8, d), const2), pl.BlockSpec((8, d), const2),
                  pl.BlockSpec(w_pw1.shape, lambda st: (0, 0, 0)), pl.BlockSpec(w_dw.shape, const2),
                  pl.BlockSpec(w_pw2.shape, const2)],
        out_specs=(pl.BlockSpec((ts, d), blk), pl.BlockSpec((4, ts, hd), lambda st: (0, nb - 1 - st, 0)),
                   pl.BlockSpec((ts, d), blk), pl.BlockSpec((16, d), const2), pl.BlockSpec((kpad, d), const2)),
        out_shape=(jax.ShapeDtypeStruct((s, d), F32), jax.ShapeDtypeStruct((4, s, hd), BF16),
                   jax.ShapeDtypeStruct((s, d), BF16), jax.ShapeDtypeStruct((16, d), F32),
                   jax.ShapeDtypeStruct((kpad, d), F32)),
        scratch_shapes=[pltpu.VMEM((halo, d), F32)],
        compiler_params=_cparams(("arbitrary",)),
    )(dout, x, y, a_pre, a_pre, uc, vec, cvec, w_pw1, w_dw, w_pw2)


def _weight_grad(a, b, comm=None):
    na, s, k = a.shape
    nb_, _, n = b.shape
    nj = max(na, nb_)
    ts = _row_tile(s, 2048)
    nt = s // ts
    comm = comm or _Comm([])
    nc = len(comm.arrays)

    def body(*refs):
        a_ref, b_ref = refs[:2]
        cin = refs[2:2 + nc]
        o_ref = refs[2 + nc]
        cout = refs[3 + nc:3 + 2 * nc]
        sems = refs[3 + 2 * nc:]
        j, t = pl.program_id(0), pl.program_id(1)

        if nc:
            @pl.when((j == 0) & (t == 0))
            def _():
                comm.run(0, cin, cout, *sems)

            @pl.when((j == nj // 2) & (t == nt // 2))
            def _():
                comm.run(1, cin, cout, *sems)

        @pl.when(t == 0)
        def _():
            o_ref[...] = jnp.zeros_like(o_ref)

        o_ref[0] += _dot_tn(a_ref[0], b_ref[0])

        if nc:
            @pl.when((j == nj - 1) & (t == nt - 1))
            def _():
                comm.run(2, cin, cout, *sems)

    res = pl.pallas_call(
        body, name="weight_grad", grid=(nj, nt),
        in_specs=[pl.BlockSpec((1, ts, k), (lambda j, t: (j, t, 0)) if na > 1 else (lambda j, t: (0, t, 0))),
                  pl.BlockSpec((1, ts, n), (lambda j, t: (j, t, 0)) if nb_ > 1 else (lambda j, t: (0, t, 0)))]
        + comm.specs(),
        out_specs=(pl.BlockSpec((1, k, n), lambda j, t: (j, 0, 0)), *comm.specs()),
        out_shape=(jax.ShapeDtypeStruct((nj, k, n), F32), *comm.outs),
        input_output_aliases=comm.aliases(2, 1),
        scratch_shapes=comm.scratch() if nc else [],
        compiler_params=_cparams(("arbitrary", "arbitrary") if nc else ("parallel", "arbitrary")),
    )(a, b, *comm.arrays)
    return res[0], comm.split(res[1:])


def _adamw(w, g, m, v):
    nl, r, c = w.shape
    tr = _row_tile(r, 256)
    c1 = 1.0 / (1.0 - ADAM_B1 ** ADAM_STEP)
    c2 = 1.0 / (1.0 - ADAM_B2 ** ADAM_STEP)

    def body(w_ref, g_ref, m_ref, v_ref, d_ref, nm_ref, nv_ref):
        g_ = g_ref[...]
        nm = ADAM_B1 * m_ref[...] + (1.0 - ADAM_B1) * g_
        nv = ADAM_B2 * v_ref[...] + (1.0 - ADAM_B2) * (g_ * g_)
        nm_ref[...] = nm
        nv_ref[...] = nv
        d_ref[...] = -ADAM_LR * ((nm * c1) / (jnp.sqrt(nv * c2) + ADAM_EPS) + ADAM_WD * w_ref[...])

    spec = pl.BlockSpec((1, tr, c), lambda l, i: (l, i, 0))
    shp = jax.ShapeDtypeStruct((nl, r, c), F32)
    return pl.pallas_call(
        body, name="adamw", grid=(nl, r // tr), in_specs=[spec] * 4, out_specs=(spec,) * 3, out_shape=(shp,) * 3,
        compiler_params=_cparams(("parallel", "parallel")),
    )(w, g, m, v)


def _add_my_half(g, other, idx):
    _, _, h, c = g.shape
    th = _row_tile(h, 256)

    def body(idx_ref, g_ref, o_ref, out_ref):
        out_ref[...] = (g_ref[:, 0] + o_ref[...]).astype(BF16)

    return pl.pallas_call(
        body, name="add_my_half",
        grid_spec=pltpu.PrefetchScalarGridSpec(
            num_scalar_prefetch=1, grid=(4, h // th),
            in_specs=[pl.BlockSpec((1, 1, th, c), lambda j, i, idx_ref: (j, idx_ref[1], i, 0)),
                      pl.BlockSpec((1, th, c), lambda j, i, idx_ref: (j, i, 0))],
            out_specs=pl.BlockSpec((1, th, c), lambda j, i, idx_ref: (j, i, 0))),
        out_shape=jax.ShapeDtypeStruct(other.shape, BF16),
        compiler_params=_cparams(("parallel", "parallel")),
    )(idx, g, other)


def _sum_for_my_chip(g, other, got, idx):
    _, _, h, c = g.shape
    th = _row_tile(h, 256)

    def body(idx_ref, g_ref, o_ref, q_ref, out_ref):
        out_ref[0] = (((g_ref[0, 0] + o_ref[0]) + q_ref[0].astype(F32)) + q_ref[1].astype(F32)) + q_ref[2].astype(F32)

    return pl.pallas_call(
        body, name="sum_for_my_chip",
        grid_spec=pltpu.PrefetchScalarGridSpec(
            num_scalar_prefetch=1, grid=(h // th,),
            in_specs=[pl.BlockSpec((1, 1, th, c), lambda i, idx_ref: (idx_ref[0], idx_ref[1], i, 0)),
                      pl.BlockSpec((1, th, c), lambda i, idx_ref: (idx_ref[0], i, 0)),
                      pl.BlockSpec((3, th, c), lambda i, idx_ref: (0, i, 0))],
            out_specs=pl.BlockSpec((1, th, c), lambda i, idx_ref: (idx_ref[1], i, 0))),
        out_shape=jax.ShapeDtypeStruct((2, h, c), F32),
        compiler_params=_cparams(("parallel",)),
    )(idx, g, other, got)


class _Reducer:
    def __init__(self, idx):
        self.idx = idx
        self.groups = []

    def add(self, grads):
        group = {"state": 0, "g": [g.reshape(4, 2, g.shape[1] // 2, g.shape[2]) for g in grads]}
        self.groups.append(group)
        return group

    def steps(self):
        ops, owners = [], []
        for gr in self.groups:
            if gr["state"] == 0:
                ops.append(_Swap(gr["g"]))
            elif gr["state"] == 1:
                ops.append(_Exchange(gr["parts"]))
            elif gr["state"] == 2:
                ops.append(_Join(gr["bufs"]))
            else:
                continue
            owners.append(gr)
        return ops, owners

    def absorb(self, owners, results):
        for gr, res in zip(owners, results):
            if gr["state"] == 0:
                gr["other"] = res
                gr["parts"] = [_add_my_half(g, o, self.idx) for g, o in zip(gr["g"], res)]
            elif gr["state"] == 1:
                gr["bufs"] = [_sum_for_my_chip(g, o, q, self.idx) for g, o, q in zip(gr["g"], gr["other"], res)]
            else:
                gr["full"] = [b.reshape(2 * b.shape[1], b.shape[2]) for b in res]
            gr["state"] += 1

    def drain(self):
        while any(gr["state"] < 3 for gr in self.groups):
            ops, owners = self.steps()
            self.absorb(owners, _communicate(ops))


def _all_gather_rows(block):
    m, n = block.shape

    def body(x_ref, out_ref, send_sems, recv_sems):
        x, y, c = lax.axis_index("x"), lax.axis_index("y"), lax.axis_index("c")
        me, sibling = (x, y, c), (x, y, 1 - c)
        chips = [(1 - x, y), (x, 1 - y), (1 - x, 1 - y)]

        def slot(px, py, pc):
            return out_ref.at[4 * px + 2 * py + pc]

        def copy(k, block_of, to, src=None):
            return pltpu.make_async_remote_copy(
                src_ref=slot(*block_of) if src is None else src, dst_ref=slot(*block_of),
                send_sem=send_sems.at[k], recv_sem=recv_sems.at[k], device_id=to, device_id_type=MESH)

        out_ref[4 * x + 2 * y + c] = x_ref[...]
        first = [copy(0, me, sibling, src=x_ref)]
        first += [copy(1 + r, me, (*chip, c), src=x_ref) for r, chip in enumerate(chips)]
        for cp in first:
            cp.start()
        passed = [copy(4 + r, (*chip, c), sibling) for r, chip in enumerate(chips)]
        for r, chip in enumerate(chips):
            copy(1 + r, (*chip, c), me).wait_recv()
            passed[r].start()
        copy(0, sibling, me).wait_recv()
        for r, chip in enumerate(chips):
            copy(4 + r, (*chip, 1 - c), me).wait_recv()
        for cp in first + passed:
            cp.wait_send()

    vm = pl.BlockSpec(memory_space=pltpu.VMEM)
    return pl.pallas_call(
        body, name="all_gather_rows",
        out_shape=jax.ShapeDtypeStruct((N_DEV, m, n), block.dtype),
        in_specs=[vm], out_specs=vm,
        scratch_shapes=[pltpu.SemaphoreType.DMA((7,)), pltpu.SemaphoreType.DMA((7,))],
        compiler_params=_cparams(),
    )(block)


def _sum_devices(gathered):
    nd, m, n = gathered.shape

    def body(g_ref, o_ref):
        acc = g_ref[0]
        for b in range(1, nd):
            acc = acc + g_ref[b]
        o_ref[...] = acc

    return pl.pallas_call(
        body, name="sum_devices", out_shape=jax.ShapeDtypeStruct((m, n), F32),
        in_specs=[pl.BlockSpec(memory_space=pltpu.VMEM)], out_specs=pl.BlockSpec(memory_space=pltpu.VMEM),
        compiler_params=_cparams(),
    )(gathered)


def _ada_weight_grad(c_all, dmod_cols):
    nl, nd, ncol = dmod_cols.shape
    d = c_all.shape[1]

    def body(c_ref, dm_ref, o_ref):
        o_ref[0] = lax.dot_general(c_ref[...], dm_ref[0], (((0,), (0,)), ((), ())),
                                   preferred_element_type=F32, precision=lax.Precision.HIGHEST)

    return pl.pallas_call(
        body, name="ada_weight_grad", grid=(nl,),
        in_specs=[pl.BlockSpec((nd, d), lambda l: (0, 0)), pl.BlockSpec((1, nd, ncol), lambda l: (l, 0, 0))],
        out_specs=pl.BlockSpec((1, d, ncol), lambda l: (l, 0, 0)),
        out_shape=jax.ShapeDtypeStruct((nl, d, ncol), F32), compiler_params=_cparams(("parallel",)),
    )(c_all, dmod_cols)


def _pad_rows(a, rows):
    return jnp.pad(a, ((0, rows - a.shape[0]), (0, 0)))


def _shard_cols(full, chip, width):
    return lax.dynamic_slice_in_dim(full, chip * width, width, axis=full.ndim - 1)


def kernel(x, c, ada_w, ada_b, pre_g, post_g, pool_w, pool_scale, cv_w_pw1, cv_b_pw1, cv_w_dw, cv_b_dw, cv_ln_g, cv_ln_b, cv_w_pw2, cv_b_pw2, ffn_w_up, ffn_w_dw, ffn_w_down, loss_target, m_ada_w, m_ada_b, m_pre_g, m_post_g, m_pool_w, m_pool_scale, m_cv_w_pw1, m_cv_b_pw1, m_cv_w_dw, m_cv_b_dw, m_cv_ln_g, m_cv_ln_b, m_cv_w_pw2, m_cv_b_pw2, m_ffn_w_up, m_ffn_w_dw, m_ffn_w_down, v_ada_w, v_ada_b, v_pre_g, v_post_g, v_pool_w, v_pool_scale, v_cv_w_pw1, v_cv_b_pw1, v_cv_w_dw, v_cv_b_dw, v_cv_ln_g, v_cv_ln_b, v_cv_w_pw2, v_cv_b_pw2, v_ffn_w_up, v_ffn_w_dw, v_ffn_w_down):
    s, d = x.shape[1], x.shape[2]
    dq = d // N_CHIPS
    n_g = pool_w.shape[1]
    gq = pool_w.shape[2]
    gd = pool_w.shape[3]
    kw = cv_w_dw.shape[1]
    cs = ffn_w_up.shape[2]
    fq = ffn_w_down.shape[1]
    chip = 2 * lax.axis_index("x") + lax.axis_index("y")
    core = lax.axis_index("c")
    chip1 = jnp.reshape(chip, (1,)).astype(jnp.int32)
    core1 = jnp.reshape(core, (1,)).astype(jnp.int32)
    xs, tgt = x[0], loss_target[0]

    c_rep, mod_rep = _ada_forward(c, ada_w)
    c_all = c_rep[:, 0, :]
    mod = mod_rep[:, :, 0, :].transpose(1, 0, 2).reshape(ada_b.shape) + ada_b

    small_rows = [pre_g.reshape(4, dq), post_g.reshape(4, dq), cv_w_dw[0], cv_b_dw, cv_ln_g, cv_ln_b, cv_b_pw2,
                  cv_b_pw1.reshape(2, dq)]
    small = jnp.concatenate(small_rows, axis=0)
    n_small = small.shape[0]
    small = _pad_rows(small, -(-n_small // 16) * 16)
    dwf = _pad_rows(ffn_w_dw.reshape(6, cs), 16)
    def slot(a):
        return lax.dynamic_update_slice_in_dim(jnp.zeros((N_CHIPS,) + a.shape, a.dtype), a[None], chip, axis=0)

    first = [_cast_into_slot(pool_w.reshape(n_g * gq, gd), chip1), slot(small), slot(dwf)]
    (g_pool, g_small, g_dwf), = _communicate([_AllGather(first)])
    second = _AllGather([_cast_into_slot(ffn_w_up[0], chip1), _cast_into_slot(ffn_w_down[0], chip1)])
    later = _AllGather([_cast_into_slot(cv_w_pw1[0], chip1), _cast_into_slot(cv_w_pw2[0], chip1),
                        _cast_into_slot(ffn_w_up[1], chip1), _cast_into_slot(ffn_w_down[1], chip1)])
    poolw_full = g_pool.reshape(N_CHIPS, n_g, gq, gd).transpose(1, 0, 2, 3).reshape(n_g, gd, gd)
    smallf = g_small.transpose(1, 0, 2).reshape(g_small.shape[1], d)
    pre_full, post_full = smallf[0:4].reshape(2, 2, d), smallf[4:8].reshape(2, 2, d)
    wdw31 = smallf[8:8 + kw]
    o = 8 + kw
    b_dw, ln_g, ln_b, b_pw2 = smallf[o:o + 1], smallf[o + 1:o + 2], smallf[o + 2:o + 3], smallf[o + 3:o + 4]
    b_pw1 = g_small[:, o + 4:o + 6, :].reshape(1, 2 * d)
    ffn_dw = g_dwf[:, :6, :].transpose(1, 0, 2).reshape(2, 3, N_CHIPS * cs)

    def sub_vec(layer, sub, extra=None):
        m6 = mod[layer].reshape(6, d)
        rows = [pre_full[layer, sub][None], 1.0 + m6[3 * sub + 1][None], m6[3 * sub][None], m6[3 * sub + 2][None],
                post_full[layer, sub][None]]
        if extra is not None:
            rows.append(extra)
        return _pad_rows(jnp.concatenate(rows, axis=0), 8)

    vec_pool = sub_vec(0, 0, pool_scale)
    vec_f0, vec_conv, vec_f1 = sub_vec(0, 1), sub_vec(1, 0), sub_vec(1, 1)
    cvec = _pad_rows(jnp.concatenate([b_dw, ln_g, ln_b, b_pw2], axis=0), 8)

    x1, ((g_up0, g_dn0),) = _pool_forward(xs, vec_pool, poolw_full, _Comm([second]))
    w_up0, w_dn0 = g_up0, g_dn0.reshape(2, 2 * fq, d)
    (x2, h_f0, a0_f0, cc_f0, u_f0, y_f0), ((g_pw1, g_pw2, g_up1, g_dn1),) = _ffn_forward(
        x1, vec_f0, w_up0, ffn_dw[0], w_dn0, _Comm([later]))
    pw2_full = g_pw2.reshape(d, d)
    w_up1, w_dn1 = g_up1, g_dn1.reshape(2, 2 * fq, d)
    x3, h_cv, a_cv, uc_cv, z_cv, y_cv = _conv_forward(x2, vec_conv, cvec, g_pw1, b_pw1, wdw31, pw2_full)
    (dx4, h_f1, a0_f1, cc_f1, u_f1, y_f1, loss_rows), _ = _ffn_forward(x3, vec_f1, w_up1, ffn_dw[1], w_dn1, target=tgt)

    dx3, da0_f1, dy_f1, sum_f1, dwdw_f1 = _ffn_backward(dx4, x3, y_f1, a0_f1, cc_f1, vec_f1, w_up1, ffn_dw[1], w_dn1)
    dx2, da_cv, dy_cv, sum_cv, dwdw_cv = _conv_backward(dx3, x2, y_cv, a_cv, uc_cv, vec_conv, cvec, g_pw1, wdw31, pw2_full)
    dx1, da0_f0, dy_f0, sum_f0, dwdw_f0 = _ffn_backward(dx2, x1, y_f0, a0_f0, cc_f0, vec_f0, w_up0, ffn_dw[0], w_dn0)
    dx0, sum_pool, gw_pool = _pool_backward(dx1, xs, vec_pool, poolw_full)
    gw_pool4 = gw_pool.reshape(n_g, N_CHIPS, gq, gd).transpose(1, 0, 2, 3).reshape(N_CHIPS, n_g * gq, gd)

    red = _Reducer(jnp.concatenate([chip1, core1]))

    def grad_stage(a, b, shape=None):
        ops, owners = red.steps()
        gw, results = _weight_grad(a, b, _Comm(ops))
        red.absorb(owners, results)
        return gw if shape is None else gw.reshape(shape)

    r_up1 = red.add([grad_stage(h_f1[None], da0_f1)])
    r_up0 = red.add([grad_stage(h_f0[None], da0_f0)])
    r_dn1 = red.add([grad_stage(u_f1, dy_f1[None], (N_CHIPS, fq, d))])
    r_dn0 = red.add([grad_stage(u_f0, dy_f0[None], (N_CHIPS, fq, d))])
    r_pw1 = red.add([grad_stage(h_cv[None], da_cv)])
    r_last = red.add([grad_stage(z_cv[None], dy_cv[None], (N_CHIPS, dq, d)), gw_pool4])
    red.drain()
    r_up1, r_up0, r_dn1, r_dn0, r_pw1 = [gr["full"][0] for gr in (r_up1, r_up0, r_dn1, r_dn0, r_pw1)]
    r_pw2, r_pool = r_last["full"]

    slab = jnp.concatenate([sum_f1, sum_cv, dwdw_cv, sum_f0, sum_pool, loss_rows], axis=0)
    wide = jnp.concatenate([dwdw_f1, dwdw_f0], axis=0)
    n_slab = slab.shape[0]
    both_all = _all_gather_rows(jnp.concatenate([slab, wide.reshape(-1, d)], axis=0))
    tot_both = _sum_devices(both_all)
    slab_all, tot = both_all[:, :n_slab], tot_both[:n_slab]
    tot_wide = tot_both[n_slab:].reshape(wide.shape)
    kpad = dwdw_cv.shape[0]
    o_cv, o_dw, o_f0 = 8, 24, 24 + kpad
    o_pool, o_loss = o_f0 + 8, o_f0 + 16
    loss = jnp.sum(tot[o_loss])
    dmod_l0 = jnp.concatenate([slab_all[:, o_pool + 4], slab_all[:, o_pool + 3], slab_all[:, o_pool + 1],
                               slab_all[:, o_f0 + 4], slab_all[:, o_f0 + 3], slab_all[:, o_f0 + 1]], axis=-1)
    dmod_l1 = jnp.concatenate([slab_all[:, o_cv + 4], slab_all[:, o_cv + 3], slab_all[:, o_cv + 1],
                               slab_all[:, 4], slab_all[:, 3], slab_all[:, 1]], axis=-1)
    dmod = jnp.stack([dmod_l0, dmod_l1], axis=0)
    g_ada_b = _sum_devices(dmod.transpose(1, 0, 2))
    ncol = ada_w.shape[2]
    g_ada_w = _ada_weight_grad(c_all, _shard_cols(dmod, chip, ncol))

    g_pre = jnp.stack([jnp.stack([tot[o_pool + 2], tot[o_f0 + 2]]), jnp.stack([tot[o_cv + 2], tot[2]])])
    g_post = jnp.stack([jnp.stack([tot[o_pool + 0], tot[o_f0 + 0]]), jnp.stack([tot[o_cv + 0], tot[0]])])
    g_pool_scale = tot[o_pool + 5][None]
    g_b_pw2, g_ln_g, g_ln_b, g_b_dw = tot[o_cv + 5], tot[o_cv + 6], tot[o_cv + 7], tot[o_cv + 8]
    g_b_pw1 = jnp.concatenate([tot[o_cv + 9], tot[o_cv + 10]])
    g_w_dw31 = tot[o_dw:o_dw + kw]
    g_ffn_dw = jnp.stack([tot_wide[8:11], tot_wide[0:3]])

    grads_small = {
        "pre_g": _shard_cols(g_pre, chip, dq), "post_g": _shard_cols(g_post, chip, dq),
        "pool_scale": g_pool_scale, "cv_b_pw1": _shard_cols(g_b_pw1[None], chip, 2 * dq),
        "cv_w_dw": _shard_cols(g_w_dw31[None], chip, dq), "cv_b_dw": _shard_cols(g_b_dw[None], chip, dq),
        "cv_ln_g": _shard_cols(g_ln_g[None], chip, dq), "cv_ln_b": _shard_cols(g_ln_b[None], chip, dq),
        "cv_b_pw2": _shard_cols(g_b_pw2[None], chip, dq), "ffn_w_dw": _shard_cols(g_ffn_dw, chip, cs),
        "ada_b": g_ada_b,
    }
    params_small = {
        "pre_g": (pre_g, m_pre_g, v_pre_g), "post_g": (post_g, m_post_g, v_post_g),
        "pool_scale": (pool_scale, m_pool_scale, v_pool_scale), "cv_b_pw1": (cv_b_pw1, m_cv_b_pw1, v_cv_b_pw1),
        "cv_w_dw": (cv_w_dw, m_cv_w_dw, v_cv_w_dw), "cv_b_dw": (cv_b_dw, m_cv_b_dw, v_cv_b_dw),
        "cv_ln_g": (cv_ln_g, m_cv_ln_g, v_cv_ln_g), "cv_ln_b": (cv_ln_b, m_cv_ln_b, v_cv_ln_b),
        "cv_b_pw2": (cv_b_pw2, m_cv_b_pw2, v_cv_b_pw2), "ffn_w_dw": (ffn_w_dw, m_ffn_w_dw, v_ffn_w_dw),
        "ada_b": (ada_b, m_ada_b, v_ada_b),
    }
    names = list(params_small)
    sizes = [params_small[nm][0].size for nm in names]
    padded = [-(-sz // 1024) * 1024 for sz in sizes]

    def pack(arrs):
        flat = [jnp.pad(a.reshape(-1), (0, p - a.size)) for a, p in zip(arrs, padded)]
        return jnp.concatenate(flat).reshape(1, -1, 128)

    pk_w = pack([params_small[nm][0] for nm in names])
    pk_m = pack([params_small[nm][1] for nm in names])
    pk_v = pack([params_small[nm][2] for nm in names])
    pk_g = pack([grads_small[nm].reshape(params_small[nm][0].shape) for nm in names])
    pk_d, pk_nm, pk_nv = _adamw(pk_w, pk_g, pk_m, pk_v)

    def unpack(pk):
        flat, out, off = pk.reshape(-1), {}, 0
        for nm, sz, p in zip(names, sizes, padded):
            out[nm] = flat[off:off + sz].reshape(params_small[nm][0].shape)
            off += p
        return out

    small_d, small_m, small_v = unpack(pk_d), unpack(pk_nm), unpack(pk_nv)
    small_g = {nm: grads_small[nm].reshape(params_small[nm][0].shape) for nm in names}

    big_g = {
        "ada_w": g_ada_w,
        "pool_w": r_pool.reshape(pool_w.shape),
        "cv_w_pw1": r_pw1[None], "cv_w_pw2": r_pw2[None],
        "ffn_w_up": jnp.stack([r_up0, r_up1]), "ffn_w_down": jnp.stack([r_dn0, r_dn1]),
    }
    big_p = {
        "ada_w": (ada_w, m_ada_w, v_ada_w), "pool_w": (pool_w, m_pool_w, v_pool_w),
        "cv_w_pw1": (cv_w_pw1, m_cv_w_pw1, v_cv_w_pw1), "cv_w_pw2": (cv_w_pw2, m_cv_w_pw2, v_cv_w_pw2),
        "ffn_w_up": (ffn_w_up, m_ffn_w_up, v_ffn_w_up), "ffn_w_down": (ffn_w_down, m_ffn_w_down, v_ffn_w_down),
    }
    big_d, big_m, big_v = {}, {}, {}
    for nm, (w, m, v) in big_p.items():
        shp = w.shape
        as3 = lambda t: t.reshape((-1,) + shp[-2:])
        dl, nm_, nv_ = _adamw(as3(w), as3(big_g[nm]), as3(m), as3(v))
        big_d[nm], big_m[nm], big_v[nm] = dl.reshape(shp), nm_.reshape(shp), nv_.reshape(shp)

    order = ["ada_w", "ada_b", "pre_g", "post_g", "pool_w", "pool_scale", "cv_w_pw1", "cv_b_pw1", "cv_w_dw", "cv_b_dw",
             "cv_ln_g", "cv_ln_b", "cv_w_pw2", "cv_b_pw2", "ffn_w_up", "ffn_w_dw", "ffn_w_down"]
    pick = lambda bigs, smalls: [bigs[nm] if nm in bigs else smalls[nm] for nm in order]
    return (loss, dx0[None], *pick(big_g, small_g), *pick(big_d, small_d), *pick(big_m, small_m),
            *pick(big_v, small_v))
```

```python
import functools

import jax
import jax.numpy as jnp
from jax import lax
from jax.experimental import pallas as pl
from jax.experimental.pallas import tpu as pltpu

F32 = jnp.float32
BF16 = jnp.bfloat16
EPS = 1e-6
N_CHIPS = 4
N_DEV = 8
POOL_WINDOWS = (2, 4, 8, 16)
POOL_HALO = 16
FFN_HALO = 16
MXU_LANES = 256
CONV_ROWS, CONV_LANES = 128, 128
ADAM_LR = 0.001
ADAM_B1 = 0.9
ADAM_B2 = 0.999
ADAM_EPS = 1e-08
ADAM_WD = 0.01
ADAM_STEP = 10
V7X_VMEM_LIMIT = 58 * 1024 * 1024
MESH = pl.DeviceIdType.MESH


def _cparams(sem=None, vmem=V7X_VMEM_LIMIT):
    return pltpu.CompilerParams(dimension_semantics=sem, vmem_limit_bytes=vmem)


def _row_tile(n, want):
    if n <= want:
        return n
    t = want - want % 8
    while n % t:
        t -= 8
    return t


def _lane_chunks(width):
    out, c = [], 0
    while c < width:
        w = min(512, width - c)
        out.append((c, w))
        c += w
    return out


def _dot(a, b):
    return jnp.dot(a, b, preferred_element_type=F32)


def _dot_nt(a, b):
    return lax.dot_general(a, b, (((1,), (1,)), ((), ())), preferred_element_type=F32)


def _store_dot_nt(dst, a_ref, b_ref):
    dst[...] = _dot_nt(a_ref[...], b_ref[...])


def _store_dot_nt2(dst, a1_ref, a2_ref, b1_ref, b2_ref):
    dst[...] = _dot_nt(a1_ref[...], b1_ref[...]) + _dot_nt(a2_ref[...], b2_ref[...])


def _dot_tn(a, b):
    return lax.dot_general(a, b, (((0,), (0,)), ((), ())), preferred_element_type=F32)


def _rms(x):
    r = lax.rsqrt(jnp.mean(x * x, axis=-1, keepdims=True) + EPS)
    return x * r, r


def _rms_bwd(dyn, yn, r):
    return r * (dyn - yn * jnp.mean(dyn * yn, axis=-1, keepdims=True))


def _sigmoid(x):
    return 0.5 * jnp.tanh(0.5 * x) + 0.5


def _colsum(x):
    return jnp.sum(x, axis=0, keepdims=True)


def _shift_down(x, k):
    return x if k == 0 else pltpu.roll(x, k, 0)


def _shift_up(x, k):
    return x if k == 0 else pltpu.roll(x, x.shape[0] - k, 0)


def _vec_rows(vec):
    return vec[0:1] * vec[1:2], vec[2:3], vec[3:4], vec[4:5]


def _norm_sums(do, yn, dh, xn, vec):
    p, q = _colsum(do * yn), _colsum(dh * xn)
    return [p * vec[3:4], p * vec[4:5], q * vec[1:2], q * vec[0:1]]


def _add_rows(sum_ref, rows):
    for k, r in enumerate(rows):
        sum_ref[k:k + 1, :] += r


def _ada_forward(c, ada_w):
    n_layers, d, ncol = ada_w.shape

    def body(c_ref, w_ref, call_ref, mod_ref, part_ref, sendbuf, send_sems, recv_sems, send2, recv2):
        x, y, cc = lax.axis_index("x"), lax.axis_index("y"), lax.axis_index("c")
        me = 4 * x + 2 * y + cc
        rel = [(x, y, 1 - cc), (1 - x, y, cc), (x, 1 - y, cc), (1 - x, 1 - y, cc),
               (1 - x, y, 1 - cc), (x, 1 - y, 1 - cc), (1 - x, 1 - y, 1 - cc)]
        cv = c_ref[...]
        call_ref[me] = jnp.broadcast_to(cv * _sigmoid(cv), (8, d))

        def gather(k, block, to):
            blk = call_ref.at[block]
            return pltpu.make_async_remote_copy(src_ref=blk, dst_ref=blk, send_sem=send_sems.at[k],
                                                recv_sem=recv_sems.at[k], device_id=to, device_id_type=MESH)

        for k, to in enumerate(rel):
            gather(k, me, to).start()
        for k, (px, py, pc) in enumerate(rel):
            gather(k, 4 * px + 2 * py + pc, rel[k]).wait_recv()
        for k, to in enumerate(rel):
            gather(k, me, to).wait_send()

        ca = call_ref[...].reshape(8 * N_DEV, d)
        for l in range(n_layers):
            part_ref[l] = jnp.dot(ca, w_ref[l], preferred_element_type=F32, precision=lax.Precision.HIGHEST)

        j = 2 * x + y
        chips = [(1 - x, y), (x, 1 - y), (1 - x, 1 - y)]

        def rows_of(b):
            return part_ref[:, pl.ds(pl.multiple_of(8 * b, 8), 8), :]

        def scatter(k, src_j, to):
            return pltpu.make_async_remote_copy(
                src_ref=sendbuf.at[k], dst_ref=mod_ref.at[src_j], send_sem=send2.at[k], recv_sem=recv2.at[k],
                device_id=to, device_id_type=MESH)

        mod_ref[j] = rows_of(me)
        for k, (px, py) in enumerate(chips):
            sendbuf[k] = rows_of(4 * px + 2 * py + cc)
            scatter(k, j, (px, py, cc)).start()
        for k, (px, py) in enumerate(chips):
            scatter(k, 2 * px + py, (px, py, cc)).wait_recv()
        for k, (px, py) in enumerate(chips):
            scatter(k, j, (px, py, cc)).wait_send()

    vm = pl.BlockSpec(memory_space=pltpu.VMEM)
    return pl.pallas_call(
        body, name="ada_forward",
        out_shape=(jax.ShapeDtypeStruct((N_DEV, 8, d), F32), jax.ShapeDtypeStruct((N_CHIPS, n_layers, 8, ncol), F32)),
        in_specs=[vm, vm], out_specs=(vm, vm),
        scratch_shapes=[pltpu.VMEM((n_layers, 8 * N_DEV, ncol), F32), pltpu.VMEM((3, n_layers, 8, ncol), F32),
                        pltpu.SemaphoreType.DMA((7,)), pltpu.SemaphoreType.DMA((7,)),
                        pltpu.SemaphoreType.DMA((3,)), pltpu.SemaphoreType.DMA((3,))],
        compiler_params=_cparams(),
    )(c, ada_w)


def _cast_into_slot(w2d, chip):
    r, c = w2d.shape
    tr = _row_tile(r, 256)

    def body(chip_ref, w_ref, o_ref):
        o_ref[0] = w_ref[...].astype(BF16)

    return pl.pallas_call(
        body, name="cast_into_slot",
        grid_spec=pltpu.PrefetchScalarGridSpec(
            num_scalar_prefetch=1, grid=(r // tr,),
            in_specs=[pl.BlockSpec((tr, c), lambda i, chip_ref: (i, 0))],
            out_specs=pl.BlockSpec((1, tr, c), lambda i, chip_ref: (chip_ref[0], i, 0))),
        out_shape=jax.ShapeDtypeStruct((N_CHIPS, r, c), BF16), compiler_params=_cparams(("parallel",)),
    )(chip, w2d)


def _place():
    x, y, c = lax.axis_index("x"), lax.axis_index("y"), lax.axis_index("c")
    return x, y, c, [(1 - x, y), (x, 1 - y), (1 - x, 1 - y)]


def _remote(src, dst, send_sem, recv_sem, to):
    return pltpu.make_async_remote_copy(src_ref=src, dst_ref=dst, send_sem=send_sem, recv_sem=recv_sem,
                                        device_id=to, device_id_type=MESH)


class _AllGather:
    def __init__(self, bufs):
        self.arrays = list(bufs)
        self.outs = [jax.ShapeDtypeStruct(b.shape, b.dtype) for b in bufs]
        self.aliased = True
        self.n_sems = 6 * len(bufs)

    def run(self, phase, ins, outs, send_sems, recv_sems, base):
        x, y, c, chips = _place()
        j = 2 * x + y
        for k, buf in enumerate(outs):
            half = buf.shape[1] // 2

            def part(src_j, h):
                return buf.at[src_j, pl.ds(h * half, half), :]

            def ici(r, src_j, to):
                s = base + 6 * k + r
                return _remote(part(src_j, c), part(src_j, c), send_sems.at[s], recv_sems.at[s], to)

            def d2d(r, src_j, h):
                s = base + 6 * k + 3 + r
                return _remote(part(src_j, h), part(src_j, h), send_sems.at[s], recv_sems.at[s], (x, y, 1 - c))

            for r, (px, py) in enumerate(chips):
                if phase == 0:
                    ici(r, j, (px, py, c)).start()
                elif phase == 1:
                    ici(r, 2 * px + py, (px, py, c)).wait_recv()
                    d2d(r, 2 * px + py, c).start()
                else:
                    d2d(r, 2 * px + py, 1 - c).wait_recv()
                    ici(r, j, (px, py, c)).wait_send()
                    d2d(r, 2 * px + py, c).wait_send()


class _Swap:
    def __init__(self, grads):
        self.arrays = list(grads)
        self.outs = [jax.ShapeDtypeStruct((g.shape[0],) + g.shape[2:], g.dtype) for g in grads]
        self.aliased = False
        self.n_sems = len(grads)

    def run(self, phase, ins, outs, send_sems, recv_sems, base):
        x, y, c, _ = _place()
        for k in range(len(ins)):
            cp = _remote(ins[k].at[:, 1 - c], outs[k], send_sems.at[base + k], recv_sems.at[base + k], (x, y, 1 - c))
            if phase == 0:
                cp.start()
            elif phase == 2:
                cp.wait()


class _Exchange:
    def __init__(self, parts):
        self.arrays = list(parts)
        self.outs = [jax.ShapeDtypeStruct((3,) + p.shape[1:], p.dtype) for p in parts]
        self.aliased = False
        self.n_sems = 3 * len(parts)

    def run(self, phase, ins, outs, send_sems, recv_sems, base):
        x, y, c, chips = _place()
        for k in range(len(ins)):
            for r, (px, py) in enumerate(chips):
                s = base + 3 * k + r
                cp = _remote(ins[k].at[2 * px + py], outs[k].at[r], send_sems.at[s], recv_sems.at[s], (px, py, c))
                if phase == 0:
                    cp.start()
                elif phase == 2:
                    cp.wait()


class _Join:
    def __init__(self, bufs):
        self.arrays = list(bufs)
        self.outs = [jax.ShapeDtypeStruct(b.shape, b.dtype) for b in bufs]
        self.aliased = True
        self.n_sems = len(bufs)

    def run(self, phase, ins, outs, send_sems, recv_sems, base):
        x, y, c, _ = _place()
        for k, buf in enumerate(outs):
            mine = _remote(buf.at[c], buf.at[c], send_sems.at[base + k], recv_sems.at[base + k], (x, y, 1 - c))
            if phase == 0:
                mine.start()
            elif phase == 2:
                mine.wait_send()
                _remote(buf.at[1 - c], buf.at[1 - c], send_sems.at[base + k], recv_sems.at[base + k],
                        (x, y, 1 - c)).wait_recv()


class _Comm:
    def __init__(self, ops):
        self.ops = list(ops)
        self.arrays = [a for op in self.ops for a in op.arrays]
        self.outs = [o for op in self.ops for o in op.outs]
        self.n_sems = sum(op.n_sems for op in self.ops)

    def specs(self):
        return [pl.BlockSpec(memory_space=pl.ANY)] * len(self.arrays)

    def aliases(self, first_in, first_out):
        out, k = {}, 0
        for op in self.ops:
            for i in range(len(op.arrays)):
                if op.aliased:
                    out[first_in + k + i] = first_out + k + i
            k += len(op.arrays)
        return out

    def scratch(self):
        return [pltpu.SemaphoreType.DMA((self.n_sems,)), pltpu.SemaphoreType.DMA((self.n_sems,))]

    def run(self, phase, ins, outs, send_sems, recv_sems):
        k = base = 0
        for op in self.ops:
            n = len(op.arrays)
            op.run(phase, ins[k:k + n], outs[k:k + n], send_sems, recv_sems, base)
            k += n
            base += op.n_sems

    def split(self, results):
        out, k = [], 0
        for op in self.ops:
            out.append(list(results[k:k + len(op.arrays)]))
            k += len(op.arrays)
        return out


def _communicate(ops):
    comm = _Comm(ops)
    n = len(comm.arrays)

    def body(*refs):
        ins, outs, (send_sems, recv_sems) = refs[:n], refs[n:2 * n], refs[2 * n:]
        for phase in range(3):
            comm.run(phase, ins, outs, send_sems, recv_sems)

    res = pl.pallas_call(
        body, name="communicate", out_shape=tuple(comm.outs), in_specs=comm.specs(), out_specs=tuple(comm.specs()),
        input_output_aliases=comm.aliases(0, 0), scratch_shapes=comm.scratch(),
    )(*comm.arrays)
    return comm.split(res)


def _pool_core(he, w_ref, scale, first_row, halo, n_rows):
    d = he.shape[1]
    gd = d // len(POOL_WINDOWS)
    t = first_row + lax.broadcasted_iota(jnp.int32, (n_rows, 1), 0)
    pooled, ypre, cnts = [], [], []
    for g, w in enumerate(POOL_WINDOWS):
        hg = he[:, g * gd:(g + 1) * gd]
        s, k = hg, 1
        while k < w:
            s = s + _shift_down(s, k)
            k *= 2
        cnt = jnp.minimum(t + 1, w).astype(F32)
        p = s[halo:] / cnt - hg[halo:]
        pooled.append(p.astype(BF16))
        cnts.append(cnt)
        ypre.append(_dot(pooled[-1], w_ref[g]))
    return pooled, jnp.concatenate(ypre, axis=1), cnts


def _pool_forward(x, vec, pool_w, comm=None):
    s, d = x.shape
    ts = _row_tile(s, 512)
    nb = s // ts
    n_g, gd, _ = pool_w.shape
    comm = comm or _Comm([])
    nc = len(comm.arrays)

    def body(*refs):
        x_ref, vec_ref, w_ref = refs[:3]
        cin = refs[3:3 + nc]
        o_ref = refs[3 + nc]
        cout = refs[4 + nc:4 + 2 * nc]
        carry = refs[4 + 2 * nc]
        sems = refs[5 + 2 * nc:]
        i = pl.program_id(0)

        @pl.when(i == 0)
        def _():
            carry[...] = jnp.zeros_like(carry)
            if nc:
                comm.run(0, cin, cout, *sems)

        if nc:
            @pl.when(i == nb - 1)
            def _():
                comm.run(1, cin, cout, *sems)

        vec = vec_ref[...]
        a, sh, gt, gpost = _vec_rows(vec)
        xb = x_ref[...]
        xn, _ = _rms(xb)
        h = xn * a + sh
        he = jnp.concatenate([carry[...], h], axis=0)
        carry[...] = h[ts - POOL_HALO:]
        _, ypre, _ = _pool_core(he, w_ref, vec[5:6], i * ts, POOL_HALO, ts)
        yn, _ = _rms(ypre * vec[5:6])
        o_ref[...] = xb + gt * (yn * gpost)
        if nc:
            @pl.when(i == nb - 1)
            def _():
                comm.run(2, cin, cout, *sems)

    res = pl.pallas_call(
        body, name="pool_forward", grid=(nb,),
        in_specs=[pl.BlockSpec((ts, d), lambda i: (i, 0)), pl.BlockSpec((8, d), lambda i: (0, 0)),
                  pl.BlockSpec((n_g, gd, gd), lambda i: (0, 0, 0))] + comm.specs(),
        out_specs=(pl.BlockSpec((ts, d), lambda i: (i, 0)), *comm.specs()),
        out_shape=(jax.ShapeDtypeStruct((s, d), F32), *comm.outs),
        input_output_aliases=comm.aliases(3, 1),
        scratch_shapes=[pltpu.VMEM((POOL_HALO, d), F32)] + (comm.scratch() if nc else []),
        compiler_params=_cparams(("arbitrary",)),
    )(x, vec, pool_w, *comm.arrays)
    return res[0], comm.split(res[1:])


def _pool_backward(dout, x, vec, pool_w):
    s, d = x.shape
    ts = _row_tile(s, 512)
    nb = s // ts
    hb = ts // POOL_HALO
    n_g, gd, _ = pool_w.shape

    def body(do_ref, x_ref, xh_ref, vec_ref, w_ref, dx_ref, sum_ref, dw_ref, carry):
        step = pl.program_id(0)
        i = nb - 1 - step

        @pl.when(step == 0)
        def _():
            carry[...] = jnp.zeros_like(carry)
            sum_ref[...] = jnp.zeros_like(sum_ref)
            dw_ref[...] = jnp.zeros_like(dw_ref)

        vec = vec_ref[...]
        a, sh, gt, gpost = _vec_rows(vec)
        scale = vec[5:6]
        do = do_ref[...]
        xe = jnp.concatenate([xh_ref[...], x_ref[...]], axis=0)
        xne, re = _rms(xe)
        he = xne * a + sh
        rowid = lax.broadcasted_iota(jnp.int32, (POOL_HALO + ts, 1), 0)
        he = jnp.where((rowid >= POOL_HALO) | (i > 0), he, 0.0)
        xn, r = xne[POOL_HALO:], re[POOL_HALO:]
        pooled, ypre, cnts = _pool_core(he, w_ref, scale, i * ts, POOL_HALO, ts)
        yn, ry = _rms(ypre * scale)
        dyn = do * (gt * gpost)
        dy = _rms_bwd(dyn, yn, ry)
        dypre = (dy * scale).astype(BF16)
        dh_parts, q_parts = [], []
        for g, w in enumerate(POOL_WINDOWS):
            dyg = dypre[:, g * gd:(g + 1) * gd]
            dpool = _dot_nt(dyg, w_ref[g])
            dw_ref[g] += _dot_tn(pooled[g], dyg)
            q = dpool / cnts[g]
            qe = jnp.concatenate([q, carry[:, g * gd:(g + 1) * gd]], axis=0)
            acc, k = qe, 1
            while k < w:
                acc = acc + _shift_up(acc, k)
                k *= 2
            dh_parts.append(acc[:ts] - dpool)
            q_parts.append(q[:POOL_HALO])
        carry[...] = jnp.concatenate(q_parts, axis=1)
        dh = jnp.concatenate(dh_parts, axis=1)
        dxn = dh * a
        dx_ref[...] = do + _rms_bwd(dxn, xn, r)
        _add_rows(sum_ref, _norm_sums(do, yn, dh, xn, vec) + [_colsum(dh), _colsum(dy * ypre)])

    blk = lambda st: (nb - 1 - st, 0)
    return pl.pallas_call(
        body, name="pool_backward", grid=(nb,),
        in_specs=[pl.BlockSpec((ts, d), blk), pl.BlockSpec((ts, d), blk),
                  pl.BlockSpec((POOL_HALO, d), lambda st: (jnp.maximum((nb - 1 - st) * hb - 1, 0), 0)),
                  pl.BlockSpec((8, d), lambda st: (0, 0)), pl.BlockSpec((n_g, gd, gd), lambda st: (0, 0, 0))],
        out_specs=(pl.BlockSpec((ts, d), blk), pl.BlockSpec((8, d), lambda st: (0, 0)),
                   pl.BlockSpec((n_g, gd, gd), lambda st: (0, 0, 0))),
        out_shape=(jax.ShapeDtypeStruct((s, d), F32), jax.ShapeDtypeStruct((8, d), F32),
                   jax.ShapeDtypeStruct((n_g, gd, gd), F32)),
        scratch_shapes=[pltpu.VMEM((POOL_HALO, d), F32)],
        compiler_params=_cparams(("arbitrary",)),
    )(dout, x, x, vec, pool_w)


def _ffn_forward(x, vec, w_up, w_dw, w_down, comm=None, target=None):
    s, d = x.shape
    _, _, cs = w_up.shape
    ts = _row_tile(s, 256)
    nb = s // ts
    chunks = _lane_chunks(cs)
    comm = comm or _Comm([])
    nc = len(comm.arrays)
    nl = 0 if target is None else 1
    n_in, n_out = 5 + nl, 6 + nl

    def body(*refs):
        x_ref, vec_ref, wup_ref, wdw_ref, wdn_ref = refs[:5]
        cin = refs[n_in:n_in + nc]
        o_ref, h_ref, a0_ref, cc_ref, u_ref, y_ref = refs[n_in + nc:n_in + nc + 6]
        loss_ref = refs[n_in + nc + 6] if nl else None
        cout = refs[n_in + nc + n_out:n_in + 2 * nc + n_out]
        carry = refs[n_in + 2 * nc + n_out]
        sems = refs[n_in + 2 * nc + n_out + 1:]
        i = pl.program_id(0)

        @pl.when(i == 0)
        def _():
            carry[...] = jnp.zeros_like(carry)
            if nl:
                loss_ref[...] = jnp.zeros_like(loss_ref)
            if nc:
                comm.run(0, cin, cout, *sems)

        if nc:
            @pl.when(i == (3 * nb) // 4)
            def _():
                comm.run(1, cin, cout, *sems)

        vec = vec_ref[...]
        a, sh, gt, gpost = _vec_rows(vec)
        xb = x_ref[...]
        xn, _ = _rms(xb)
        hb = (xn * a + sh).astype(BF16)
        h_ref[...] = hb
        for q in range(2):
            for c0, cw in chunks:
                conv = []
                for j in (q, q + 2):
                    a0 = _dot(hb, wup_ref[j, :, c0:c0 + cw])
                    a0_ref[j, :, c0:c0 + cw] = a0.astype(BF16)
                    ae = jnp.concatenate([carry[j, :, c0:c0 + cw], a0], axis=0)
                    carry[j, :, c0:c0 + cw] = a0[ts - FFN_HALO:]
                    w = wdw_ref[:, j * cs + c0:j * cs + c0 + cw]
                    conv.append((w[2:3] * ae + w[1:2] * _shift_down(ae, 1) + w[0:1] * _shift_down(ae, 2))[FFN_HALO:])
                    cc_ref[j, :, c0:c0 + cw] = conv[-1].astype(BF16)
                u_ref[q, :, c0:c0 + cw] = (conv[0] * _sigmoid(conv[0]) * conv[1]).astype(BF16)
        y = _dot(u_ref[0], wdn_ref[0]) + _dot(u_ref[1], wdn_ref[1])
        y_ref[...] = y
        yn, _ = _rms(y)
        x_out = xb + gt * (yn * gpost)
        if nl:
            err = x_out - refs[5][...]
            o_ref[...] = err * (1.0 / d)
            loss_ref[0:1, :] += _colsum(err * err) * (0.5 / d)
        else:
            o_ref[...] = x_out
        if nc:
            @pl.when(i == nb - 1)
            def _():
                comm.run(2, cin, cout, *sems)

    const3 = lambda i: (0, 0, 0)
    res = pl.pallas_call(
        body, name="ffn_forward", grid=(nb,),
        in_specs=[pl.BlockSpec((ts, d), lambda i: (i, 0)), pl.BlockSpec((8, d), lambda i: (0, 0)),
                  pl.BlockSpec(w_up.shape, const3, pipeline_mode=pl.Buffered(1)),
                  pl.BlockSpec(w_dw.shape, lambda i: (0, 0)),
                  pl.BlockSpec(w_down.shape, const3, pipeline_mode=pl.Buffered(1))]
        + [pl.BlockSpec((ts, d), lambda i: (i, 0))] * nl + comm.specs(),
        out_specs=(pl.BlockSpec((ts, d), lambda i: (i, 0)), pl.BlockSpec((ts, d), lambda i: (i, 0)),
                   pl.BlockSpec((4, ts, cs), lambda i: (0, i, 0)), pl.BlockSpec((4, ts, cs), lambda i: (0, i, 0)),
                   pl.BlockSpec((2, ts, cs), lambda i: (0, i, 0)), pl.BlockSpec((ts, d), lambda i: (i, 0)),
                   *[pl.BlockSpec((8, d), lambda i: (0, 0))] * nl, *comm.specs()),
        out_shape=(jax.ShapeDtypeStruct((s, d), F32), jax.ShapeDtypeStruct((s, d), BF16),
                   jax.ShapeDtypeStruct((4, s, cs), BF16), jax.ShapeDtypeStruct((4, s, cs), BF16),
                   jax.ShapeDtypeStruct((2, s, cs), BF16), jax.ShapeDtypeStruct((s, d), F32),
                   *[jax.ShapeDtypeStruct((8, d), F32)] * nl, *comm.outs),
        input_output_aliases=comm.aliases(n_in, n_out),
        scratch_shapes=[pltpu.VMEM((4, FFN_HALO, cs), F32)] + (comm.scratch() if nc else []),
        compiler_params=_cparams(("arbitrary",)),
    )(x, vec, w_up, w_dw, w_down, *([target] * nl), *comm.arrays)
    return res[:n_out], comm.split(res[n_out:])


def _ffn_backward(dout, x, y, a0, cc, vec, w_up, w_dw, w_down):
    s, d = x.shape
    _, _, cs = w_up.shape
    ts = _row_tile(s, 256)
    nb = s // ts
    chunks = _lane_chunks(cs)

    def body(do_ref, x_ref, y_ref, a0_ref, cc_ref, vec_ref, wup_ref, wdw_ref, wdn_ref,
             dx_ref, da0_ref, dy_ref, sum_ref, dwdw_ref, carry, du_s, dh_s):
        step = pl.program_id(0)

        @pl.when(step == 0)
        def _():
            carry[...] = jnp.zeros_like(carry)
            sum_ref[...] = jnp.zeros_like(sum_ref)
            dwdw_ref[...] = jnp.zeros_like(dwdw_ref)

        vec = vec_ref[...]
        a, sh, gt, gpost = _vec_rows(vec)
        do = do_ref[...]
        yn, ry = _rms(y_ref[...])
        dy = _rms_bwd(do * (gt * gpost), yn, ry)
        dyb = dy.astype(BF16)
        dy_ref[...] = dyb
        order = [(q, c0, cw) for q in range(2) for c0, cw in chunks]

        def du_pieces(idx):
            q, c0, cw = order[idx]
            return [functools.partial(_store_dot_nt, du_s.at[idx % 2, :, n0:min(n0 + MXU_LANES, cw)], dy_ref,
                                      wdn_ref.at[q, c0 + n0:c0 + min(n0 + MXU_LANES, cw), :])
                    for n0 in range(0, cw, MXU_LANES)]

        def dh_pieces():
            return [functools.partial(_store_dot_nt2, dh_s.at[:, n0:n0 + MXU_LANES], da0_ref.at[0], da0_ref.at[2],
                                      wup_ref.at[0, n0:n0 + MXU_LANES, :], wup_ref.at[2, n0:n0 + MXU_LANES, :])
                    for n0 in range(0, d, MXU_LANES)]

        for piece in du_pieces(0):
            piece()
        later = dh_pieces()
        for idx, (q, c0, cw) in enumerate(order):
            work = du_pieces(idx + 1) if idx + 1 < len(order) else []
            if q == 1:
                share = -(-len(later) // (len(order) - idx))
                work, later = work + later[:share], later[share:]

            def pump(part, of=3):
                for piece in work[part::of]:
                    piece()

            cg = cc_ref[q, :, c0:c0 + cw].astype(F32)
            cv = cc_ref[q + 2, :, c0:c0 + cw].astype(F32)
            sg = _sigmoid(cg)
            sl = cg * sg
            du = du_s[idx % 2, :, :cw]
            dconv = {q: du * cv * (sg * (1.0 + cg * (1.0 - sg))), q + 2: du * sl}
            pump(0)
            for part, j in enumerate((q, q + 2)):
                dae = jnp.concatenate([dconv[j], carry[j, :, c0:c0 + cw]], axis=0)
                carry[j, :, c0:c0 + cw] = dconv[j][:FFN_HALO]
                up1 = _shift_down(dae, FFN_HALO - 1)[FFN_HALO:]
                up2 = _shift_down(dae, FFN_HALO - 2)[FFN_HALO:]
                lanes = slice(j * cs + c0, j * cs + c0 + cw)
                w = wdw_ref[:, lanes]
                da0_ref[j, :, c0:c0 + cw] = (w[2:3] * dconv[j] + w[1:2] * up1 + w[0:1] * up2).astype(BF16)
                a0 = a0_ref[j, :, c0:c0 + cw].astype(F32)
                dwdw_ref[0:1, lanes] += _colsum(up2 * a0)
                dwdw_ref[1:2, lanes] += _colsum(up1 * a0)
                dwdw_ref[2:3, lanes] += _colsum(dconv[j] * a0)
                pump(part + 1)
        dh = dh_s[...] + _dot_nt(da0_ref[1], wup_ref[1]) + _dot_nt(da0_ref[3], wup_ref[3])
        xn, r = _rms(x_ref[...])
        dx_ref[...] = do + _rms_bwd(dh * a, xn, r)
        _add_rows(sum_ref, _norm_sums(do, yn, dh, xn, vec) + [_colsum(dh)])

    blk = lambda st: (nb - 1 - st, 0)
    blk3 = lambda st: (0, nb - 1 - st, 0)
    const3 = lambda st: (0, 0, 0)
    return pl.pallas_call(
        body, name="ffn_backward", grid=(nb,),
        in_specs=[pl.BlockSpec((ts, d), blk), pl.BlockSpec((ts, d), blk), pl.BlockSpec((ts, d), blk),
                  pl.BlockSpec((4, ts, cs), blk3), pl.BlockSpec((4, ts, cs), blk3),
                  pl.BlockSpec((8, d), lambda st: (0, 0)),
                  pl.BlockSpec(w_up.shape, const3, pipeline_mode=pl.Buffered(1)),
                  pl.BlockSpec(w_dw.shape, lambda st: (0, 0)),
                  pl.BlockSpec(w_down.shape, const3, pipeline_mode=pl.Buffered(1))],
        out_specs=(pl.BlockSpec((ts, d), blk), pl.BlockSpec((4, ts, cs), blk3),
                   pl.BlockSpec((ts, d), blk), pl.BlockSpec((8, d), lambda st: (0, 0)),
                   pl.BlockSpec((8, 4 * cs), lambda st: (0, 0))),
        out_shape=(jax.ShapeDtypeStruct((s, d), F32), jax.ShapeDtypeStruct((4, s, cs), BF16),
                   jax.ShapeDtypeStruct((s, d), BF16),
                   jax.ShapeDtypeStruct((8, d), F32), jax.ShapeDtypeStruct((8, 4 * cs), F32)),
        scratch_shapes=[pltpu.VMEM((4, FFN_HALO, cs), F32), pltpu.VMEM((2, ts, max(cw for _, cw in chunks)), F32),
                        pltpu.VMEM((ts, d), F32)],
        compiler_params=_cparams(("arbitrary",)),
    )(dout, x, y, a0, cc, vec, w_up, w_dw, w_down)


def _conv_halo(width):
    return -(-(width - 1) // 8) * 8


def _conv_forward(x, vec, cvec, w_pw1, b_pw1, w_dw, w_pw2):
    s, d = x.shape
    kw = w_dw.shape[0]
    halo = _conv_halo(kw)
    ts = _row_tile(s, 256)
    hd = d // 2

    def body(x_ref, vec_ref, cvec_ref, w1_ref, b1_ref, wdw_ref, w2_ref,
             o_ref, h_ref, a_ref, uc_ref, z_ref, y_ref, carry):
        i = pl.program_id(0)

        @pl.when(i == 0)
        def _():
            carry[...] = jnp.zeros_like(carry)

        vec, cvec = vec_ref[...], cvec_ref[...]
        a, sh, gt, gpost = _vec_rows(vec)
        xb = x_ref[...]
        xn, _ = _rms(xb)
        hb = (xn * a + sh).astype(BF16)
        h_ref[...] = hb
        for j in range(4):
            a_ref[:, j * hd:(j + 1) * hd] = _dot(hb, w1_ref[j]) + b1_ref[:, j * hd:(j + 1) * hd]
        u = a_ref[:, :d] * _sigmoid(a_ref[:, d:])
        carry[halo:, :] = u
        for r0 in range(0, ts, CONV_ROWS):
            for l0 in range(0, d, CONV_LANES):
                lanes = slice(l0, l0 + CONV_LANES)
                src = carry[r0:r0 + CONV_ROWS + halo, lanes]
                acc = jnp.zeros((CONV_ROWS, CONV_LANES), F32) + cvec[0:1, lanes]
                for k in range(kw):
                    acc = acc + wdw_ref[k:k + 1, lanes] * _shift_down(src, kw - 1 - k)[halo:]
                uc_ref[r0:r0 + CONV_ROWS, lanes] = acc
        carry[:halo, :] = u[ts - halo:]
        uc = uc_ref[...]
        mu = jnp.mean(uc, axis=-1, keepdims=True)
        cen = uc - mu
        rstd = lax.rsqrt(jnp.mean(cen * cen, axis=-1, keepdims=True) + EPS)
        l = cen * rstd * cvec[1:2] + cvec[2:3]
        zb = (l * _sigmoid(l)).astype(BF16)
        z_ref[...] = zb
        y = _dot(zb, w2_ref[...]) + cvec[3:4]
        y_ref[...] = y
        yn, _ = _rms(y)
        o_ref[...] = xb + gt * (yn * gpost)

    row = lambda i: (i, 0)
    const2 = lambda i: (0, 0)
    return pl.pallas_call(
        body, name="conv_forward", grid=(s // ts,),
        in_specs=[pl.BlockSpec((ts, d), row), pl.BlockSpec((8, d), const2), pl.BlockSpec((8, d), const2),
                  pl.BlockSpec(w_pw1.shape, lambda i: (0, 0, 0)), pl.BlockSpec(b_pw1.shape, const2),
                  pl.BlockSpec(w_dw.shape, const2), pl.BlockSpec(w_pw2.shape, const2)],
        out_specs=(pl.BlockSpec((ts, d), row), pl.BlockSpec((ts, d), row), pl.BlockSpec((ts, 2 * d), row),
                   pl.BlockSpec((ts, d), row), pl.BlockSpec((ts, d), row), pl.BlockSpec((ts, d), row)),
        out_shape=(jax.ShapeDtypeStruct((s, d), F32), jax.ShapeDtypeStruct((s, d), BF16),
                   jax.ShapeDtypeStruct((s, 2 * d), F32), jax.ShapeDtypeStruct((s, d), F32),
                   jax.ShapeDtypeStruct((s, d), BF16), jax.ShapeDtypeStruct((s, d), F32)),
        scratch_shapes=[pltpu.VMEM((halo + ts, d), F32)],
        compiler_params=_cparams(("arbitrary",)),
    )(x, vec, cvec, w_pw1, b_pw1, w_dw, w_pw2)


def _conv_backward(dout, x, y, a_pre, uc, vec, cvec, w_pw1, w_dw, w_pw2):
    s, d = x.shape
    kw = w_dw.shape[0]
    kpad = -(-kw // 8) * 8
    halo = _conv_halo(kw)
    ts = _row_tile(s, 256)
    nb = s // ts
    hb = ts // halo
    hd = d // 2

    def body(do_ref, x_ref, y_ref, a_ref, ah_ref, uc_ref, vec_ref, cvec_ref, w1_ref, wdw_ref, w2_ref,
             dx_ref, da_ref, dy_ref, sum_ref, dwdw_ref, carry):
        step = pl.program_id(0)
        i = nb - 1 - step

        @pl.when(step == 0)
        def _():
            carry[...] = jnp.zeros_like(carry)
            sum_ref[...] = jnp.zeros_like(sum_ref)
            dwdw_ref[...] = jnp.zeros_like(dwdw_ref)

        vec, cvec = vec_ref[...], cvec_ref[...]
        a, sh, gt, gpost = _vec_rows(vec)
        do = do_ref[...]
        yn, ry = _rms(y_ref[...])
        dy = _rms_bwd(do * (gt * gpost), yn, ry)
        dyb = dy.astype(BF16)
        dy_ref[...] = dyb
        dz = _dot_nt(dyb, w2_ref[...])
        uc = uc_ref[...]
        mu = jnp.mean(uc, axis=-1, keepdims=True)
        cen = uc - mu
        rstd = lax.rsqrt(jnp.mean(cen * cen, axis=-1, keepdims=True) + EPS)
        lhat = cen * rstd
        l = lhat * cvec[1:2] + cvec[2:3]
        sgl = _sigmoid(l)
        dl = dz * (sgl * (1.0 + l * (1.0 - sgl)))
        dlhat = dl * cvec[1:2]
        duc = rstd * (dlhat - jnp.mean(dlhat, axis=-1, keepdims=True)
                      - lhat * jnp.mean(dlhat * lhat, axis=-1, keepdims=True))
        ae = jnp.concatenate([ah_ref[...] * (i > 0).astype(F32), a_ref[...]], axis=0)
        sgate = _sigmoid(ae[:, d:])
        val = ae[:, :d]
        ue = val * sgate
        rowid = lax.broadcasted_iota(jnp.int32, (halo + ts, 1), 0)
        ue = jnp.where((rowid >= halo) | (i > 0), ue, 0.0)
        duce = jnp.concatenate([duc, carry[...]], axis=0)
        carry[...] = duc[:halo]
        du = jnp.zeros((ts, d), F32)
        for k in range(kw):
            du = du + wdw_ref[k:k + 1, :] * _shift_down(duce, halo - (kw - 1 - k))[halo:]
            dwdw_ref[k:k + 1, :] += _colsum(duc * _shift_down(ue, kw - 1 - k)[halo:])
        sg, vl = sgate[halo:], val[halo:]
        dval = du * sg
        dgate = du * vl * (sg * (1.0 - sg))
        dvb, dgb = dval.astype(BF16), dgate.astype(BF16)
        dh = jnp.zeros((ts, d), F32)
        for j in range(2):
            da_ref[j] = dvb[:, j * hd:(j + 1) * hd]
            da_ref[j + 2] = dgb[:, j * hd:(j + 1) * hd]
            dh = dh + _dot_nt(dvb[:, j * hd:(j + 1) * hd], w1_ref[j]) + _dot_nt(dgb[:, j * hd:(j + 1) * hd], w1_ref[j + 2])
        xn, r = _rms(x_ref[...])
        dx_ref[...] = do + _rms_bwd(dh * a, xn, r)
        _add_rows(sum_ref, _norm_sums(do, yn, dh, xn, vec) + [_colsum(dh), _colsum(dy), _colsum(dl * lhat), _colsum(dl),
                            _colsum(duc), _colsum(dval), _colsum(dgate)])

    blk = lambda st: (nb - 1 - st, 0)
    const2 = lambda st: (0, 0)
    return pl.pallas_call(
        body, name="conv_backward", grid=(nb,),
        in_specs=[pl.BlockSpec((ts, d), blk), pl.BlockSpec((ts, d), blk), pl.BlockSpec((ts, d), blk),
                  pl.BlockSpec((ts, 2 * d), blk),
                  pl.BlockSpec((halo, 2 * d), lambda st: (jnp.maximum((nb - 1 - st) * hb - 1, 0), 0)),
                  pl.BlockSpec((ts, d), blk), pl.BlockSpec((8, d), const2), pl.BlockSpec((8, d), const2),
                  pl.BlockSpec(w_pw1.shape, lambda st: (0, 0, 0)), pl.BlockSpec(w_dw.shape, const2),
                  pl.BlockSpec(w_pw2.shape, const2)],
        out_specs=(pl.BlockSpec((ts, d), blk), pl.BlockSpec((4, ts, hd), lambda st: (0, nb - 1 - st, 0)),
                   pl.BlockSpec((ts, d), blk), pl.BlockSpec((16, d), const2), pl.BlockSpec((kpad, d), const2)),
        out_shape=(jax.ShapeDtypeStruct((s, d), F32), jax.ShapeDtypeStruct((4, s, hd), BF16),
                   jax.ShapeDtypeStruct((s, d), BF16), jax.ShapeDtypeStruct((16, d), F32),
                   jax.ShapeDtypeStruct((kpad, d), F32)),
        scratch_shapes=[pltpu.VMEM((halo, d), F32)],
        compiler_params=_cparams(("arbitrary",)),
    )(dout, x, y, a_pre, a_pre, uc, vec, cvec, w_pw1, w_dw, w_pw2)


def _weight_grad(a, b, comm=None):
    na, s, k = a.shape
    nb_, _, n = b.shape
    nj = max(na, nb_)
    ts = _row_tile(s, 2048)
    nt = s // ts
    comm = comm or _Comm([])
    nc = len(comm.arrays)

    def body(*refs):
        a_ref, b_ref = refs[:2]
        cin = refs[2:2 + nc]
        o_ref = refs[2 + nc]
        cout = refs[3 + nc:3 + 2 * nc]
        sems = refs[3 + 2 * nc:]
        j, t = pl.program_id(0), pl.program_id(1)

        if nc:
            @pl.when((j == 0) & (t == 0))
            def _():
                comm.run(0, cin, cout, *sems)

            @pl.when((j == nj // 2) & (t == nt // 2))
            def _():
                comm.run(1, cin, cout, *sems)

        @pl.when(t == 0)
        def _():
            o_ref[...] = jnp.zeros_like(o_ref)

        o_ref[0] += _dot_tn(a_ref[0], b_ref[0])

        if nc:
            @pl.when((j == nj - 1) & (t == nt - 1))
            def _():
                comm.run(2, cin, cout, *sems)

    res = pl.pallas_call(
        body, name="weight_grad", grid=(nj, nt),
        in_specs=[pl.BlockSpec((1, ts, k), (lambda j, t: (j, t, 0)) if na > 1 else (lambda j, t: (0, t, 0))),
                  pl.BlockSpec((1, ts, n), (lambda j, t: (j, t, 0)) if nb_ > 1 else (lambda j, t: (0, t, 0)))]
        + comm.specs(),
        out_specs=(pl.BlockSpec((1, k, n), lambda j, t: (j, 0, 0)), *comm.specs()),
        out_shape=(jax.ShapeDtypeStruct((nj, k, n), F32), *comm.outs),
        input_output_aliases=comm.aliases(2, 1),
        scratch_shapes=comm.scratch() if nc else [],
        compiler_params=_cparams(("arbitrary", "arbitrary") if nc else ("parallel", "arbitrary")),
    )(a, b, *comm.arrays)
    return res[0], comm.split(res[1:])


def _adamw(w, g, m, v, comm=None):
    nl, r, c = w.shape
    tr = _row_tile(r, 256)
    nr = r // tr
    c1 = 1.0 / (1.0 - ADAM_B1 ** ADAM_STEP)
    c2 = 1.0 / (1.0 - ADAM_B2 ** ADAM_STEP)
    comm = comm or _Comm([])
    nc = len(comm.arrays)

    def body(*refs):
        w_ref, g_ref, m_ref, v_ref = refs[:4]
        cin = refs[4:4 + nc]
        d_ref, nm_ref, nv_ref = refs[4 + nc:7 + nc]
        cout = refs[7 + nc:7 + 2 * nc]
        sems = refs[7 + 2 * nc:]
        l, i = pl.program_id(0), pl.program_id(1)
        if nc:
            @pl.when((l == 0) & (i == 0))
            def _():
                comm.run(0, cin, cout, *sems)

            @pl.when((l == nl // 2) & (i == nr // 2))
            def _():
                comm.run(1, cin, cout, *sems)

        g_ = g_ref[...]
        nm = ADAM_B1 * m_ref[...] + (1.0 - ADAM_B1) * g_
        nv = ADAM_B2 * v_ref[...] + (1.0 - ADAM_B2) * (g_ * g_)
        nm_ref[...] = nm
        nv_ref[...] = nv
        d_ref[...] = -ADAM_LR * ((nm * c1) / (jnp.sqrt(nv * c2) + ADAM_EPS) + ADAM_WD * w_ref[...])
        if nc:
            @pl.when((l == nl - 1) & (i == nr - 1))
            def _():
                comm.run(2, cin, cout, *sems)

    spec = pl.BlockSpec((1, tr, c), lambda l, i: (l, i, 0))
    shp = jax.ShapeDtypeStruct((nl, r, c), F32)
    res = pl.pallas_call(
        body, name="adamw", grid=(nl, nr), in_specs=[spec] * 4 + comm.specs(),
        out_specs=(spec,) * 3 + tuple(comm.specs()), out_shape=(shp,) * 3 + tuple(comm.outs),
        input_output_aliases=comm.aliases(4, 3), scratch_shapes=comm.scratch() if nc else [],
        compiler_params=_cparams(("arbitrary", "arbitrary") if nc else ("parallel", "parallel")),
    )(w, g, m, v, *comm.arrays)
    return res[:3], comm.split(res[3:])


def _add_my_half(g, other, idx):
    _, _, h, c = g.shape
    th = _row_tile(h, 256)

    def body(idx_ref, g_ref, o_ref, out_ref):
        out_ref[...] = (g_ref[:, 0] + o_ref[...]).astype(BF16)

    return pl.pallas_call(
        body, name="add_my_half",
        grid_spec=pltpu.PrefetchScalarGridSpec(
            num_scalar_prefetch=1, grid=(4, h // th),
            in_specs=[pl.BlockSpec((1, 1, th, c), lambda j, i, idx_ref: (j, idx_ref[1], i, 0)),
                      pl.BlockSpec((1, th, c), lambda j, i, idx_ref: (j, i, 0))],
            out_specs=pl.BlockSpec((1, th, c), lambda j, i, idx_ref: (j, i, 0))),
        out_shape=jax.ShapeDtypeStruct(other.shape, BF16),
        compiler_params=_cparams(("parallel", "parallel")),
    )(idx, g, other)


def _sum_for_my_chip(g, other, got, idx):
    _, _, h, c = g.shape
    th = _row_tile(h, 256)

    def body(idx_ref, g_ref, o_ref, q_ref, out_ref):
        out_ref[0] = (((g_ref[0, 0] + o_ref[0]) + q_ref[0].astype(F32)) + q_ref[1].astype(F32)) + q_ref[2].astype(F32)

    return pl.pallas_call(
        body, name="sum_for_my_chip",
        grid_spec=pltpu.PrefetchScalarGridSpec(
            num_scalar_prefetch=1, grid=(h // th,),
            in_specs=[pl.BlockSpec((1, 1, th, c), lambda i, idx_ref: (idx_ref[0], idx_ref[1], i, 0)),
                      pl.BlockSpec((1, th, c), lambda i, idx_ref: (idx_ref[0], i, 0)),
                      pl.BlockSpec((3, th, c), lambda i, idx_ref: (0, i, 0))],
            out_specs=pl.BlockSpec((1, th, c), lambda i, idx_ref: (idx_ref[1], i, 0))),
        out_shape=jax.ShapeDtypeStruct((2, h, c), F32),
        compiler_params=_cparams(("parallel",)),
    )(idx, g, other, got)


class _Reducer:
    def __init__(self, idx):
        self.idx = idx
        self.groups = []

    def add(self, grads):
        group = {"state": 0, "g": [g.reshape(4, 2, g.shape[1] // 2, g.shape[2]) for g in grads]}
        self.groups.append(group)
        return group

    def steps(self):
        ops, owners = [], []
        for gr in self.groups:
            if gr["state"] == 0:
                ops.append(_Swap(gr["g"]))
            elif gr["state"] == 1:
                ops.append(_Exchange(gr["parts"]))
            elif gr["state"] == 2:
                ops.append(_Join(gr["bufs"]))
            else:
                continue
            owners.append(gr)
        return ops, owners

    def absorb(self, owners, results):
        for gr, res in zip(owners, results):
            if gr["state"] == 0:
                gr["other"] = res
                gr["parts"] = [_add_my_half(g, o, self.idx) for g, o in zip(gr["g"], res)]
            elif gr["state"] == 1:
                gr["bufs"] = [_sum_for_my_chip(g, o, q, self.idx) for g, o, q in zip(gr["g"], gr["other"], res)]
            else:
                gr["full"] = [b.reshape(2 * b.shape[1], b.shape[2]) for b in res]
            gr["state"] += 1

    def drain(self):
        while any(gr["state"] < 3 for gr in self.groups):
            ops, owners = self.steps()
            self.absorb(owners, _communicate(ops))


class _GatherRows:
    def __init__(self, bufs):
        self.arrays = list(bufs)
        self.outs = [jax.ShapeDtypeStruct(b.shape, b.dtype) for b in bufs]
        self.aliased = True
        self.n_sems = 7 * len(bufs)

    def run(self, phase, ins, outs, send_sems, recv_sems, base):
        x, y, c, chips = _place()
        me, sibling = (x, y, c), (x, y, 1 - c)
        for k, buf in enumerate(outs):
            def copy(i, block_of, to):
                blk = buf.at[4 * block_of[0] + 2 * block_of[1] + block_of[2]]
                return _remote(blk, blk, send_sems.at[base + 7 * k + i], recv_sems.at[base + 7 * k + i], to)

            if phase == 0:
                copy(0, me, sibling).start()
            for r, (px, py) in enumerate(chips):
                if phase == 0:
                    copy(1 + r, me, (px, py, c)).start()
                elif phase == 1:
                    copy(1 + r, (px, py, c), me).wait_recv()
                    copy(4 + r, (px, py, c), sibling).start()
                else:
                    copy(4 + r, (px, py, 1 - c), me).wait_recv()
                    copy(1 + r, me, (px, py, c)).wait_send()
                    copy(4 + r, (px, py, c), sibling).wait_send()
            if phase == 2:
                copy(0, sibling, me).wait_recv()
                copy(0, me, sibling).wait_send()


def _sum_devices(gathered):
    nd, m, n = gathered.shape

    def body(g_ref, o_ref):
        acc = g_ref[0]
        for b in range(1, nd):
            acc = acc + g_ref[b]
        o_ref[...] = acc

    return pl.pallas_call(
        body, name="sum_devices", out_shape=jax.ShapeDtypeStruct((m, n), F32),
        in_specs=[pl.BlockSpec(memory_space=pltpu.VMEM)], out_specs=pl.BlockSpec(memory_space=pltpu.VMEM),
        compiler_params=_cparams(),
    )(gathered)


def _ada_weight_grad(c_all, dmod_cols):
    nl, nd, ncol = dmod_cols.shape
    d = c_all.shape[1]

    def body(c_ref, dm_ref, o_ref):
        o_ref[0] = lax.dot_general(c_ref[...], dm_ref[0], (((0,), (0,)), ((), ())),
                                   preferred_element_type=F32, precision=lax.Precision.HIGHEST)

    return pl.pallas_call(
        body, name="ada_weight_grad", grid=(nl,),
        in_specs=[pl.BlockSpec((nd, d), lambda l: (0, 0)), pl.BlockSpec((1, nd, ncol), lambda l: (l, 0, 0))],
        out_specs=pl.BlockSpec((1, d, ncol), lambda l: (l, 0, 0)),
        out_shape=jax.ShapeDtypeStruct((nl, d, ncol), F32), compiler_params=_cparams(("parallel",)),
    )(c_all, dmod_cols)


def _pad_rows(a, rows):
    return jnp.pad(a, ((0, rows - a.shape[0]), (0, 0)))


def _shard_cols(full, chip, width):
    return lax.dynamic_slice_in_dim(full, chip * width, width, axis=full.ndim - 1)


def kernel(x, c, ada_w, ada_b, pre_g, post_g, pool_w, pool_scale, cv_w_pw1, cv_b_pw1, cv_w_dw, cv_b_dw, cv_ln_g, cv_ln_b, cv_w_pw2, cv_b_pw2, ffn_w_up, ffn_w_dw, ffn_w_down, loss_target, m_ada_w, m_ada_b, m_pre_g, m_post_g, m_pool_w, m_pool_scale, m_cv_w_pw1, m_cv_b_pw1, m_cv_w_dw, m_cv_b_dw, m_cv_ln_g, m_cv_ln_b, m_cv_w_pw2, m_cv_b_pw2, m_ffn_w_up, m_ffn_w_dw, m_ffn_w_down, v_ada_w, v_ada_b, v_pre_g, v_post_g, v_pool_w, v_pool_scale, v_cv_w_pw1, v_cv_b_pw1, v_cv_w_dw, v_cv_b_dw, v_cv_ln_g, v_cv_ln_b, v_cv_w_pw2, v_cv_b_pw2, v_ffn_w_up, v_ffn_w_dw, v_ffn_w_down):
    s, d = x.shape[1], x.shape[2]
    dq = d // N_CHIPS
    n_g = pool_w.shape[1]
    gq = pool_w.shape[2]
    gd = pool_w.shape[3]
    kw = cv_w_dw.shape[1]
    cs = ffn_w_up.shape[2]
    fq = ffn_w_down.shape[1]
    chip = 2 * lax.axis_index("x") + lax.axis_index("y")
    core = lax.axis_index("c")
    chip1 = jnp.reshape(chip, (1,)).astype(jnp.int32)
    core1 = jnp.reshape(core, (1,)).astype(jnp.int32)
    xs, tgt = x[0], loss_target[0]

    c_rep, mod_rep = _ada_forward(c, ada_w)
    c_all = c_rep[:, 0, :]
    mod = mod_rep[:, :, 0, :].transpose(1, 0, 2).reshape(ada_b.shape) + ada_b

    small_rows = [pre_g.reshape(4, dq), post_g.reshape(4, dq), cv_w_dw[0], cv_b_dw, cv_ln_g, cv_ln_b, cv_b_pw2,
                  cv_b_pw1.reshape(2, dq)]
    small = jnp.concatenate(small_rows, axis=0)
    n_small = small.shape[0]
    small = _pad_rows(small, -(-n_small // 16) * 16)
    dwf = _pad_rows(ffn_w_dw.reshape(6, cs), 16)
    def slot(a):
        return lax.dynamic_update_slice_in_dim(jnp.zeros((N_CHIPS,) + a.shape, a.dtype), a[None], chip, axis=0)

    first = [_cast_into_slot(pool_w.reshape(n_g * gq, gd), chip1), slot(small), slot(dwf)]
    (g_pool, g_small, g_dwf), = _communicate([_AllGather(first)])
    second = _AllGather([_cast_into_slot(ffn_w_up[0], chip1), _cast_into_slot(ffn_w_down[0], chip1)])
    later = _AllGather([_cast_into_slot(cv_w_pw1[0], chip1), _cast_into_slot(cv_w_pw2[0], chip1),
                        _cast_into_slot(ffn_w_up[1], chip1), _cast_into_slot(ffn_w_down[1], chip1)])
    poolw_full = g_pool.reshape(N_CHIPS, n_g, gq, gd).transpose(1, 0, 2, 3).reshape(n_g, gd, gd)
    smallf = g_small.transpose(1, 0, 2).reshape(g_small.shape[1], d)
    pre_full, post_full = smallf[0:4].reshape(2, 2, d), smallf[4:8].reshape(2, 2, d)
    wdw31 = smallf[8:8 + kw]
    o = 8 + kw
    b_dw, ln_g, ln_b, b_pw2 = smallf[o:o + 1], smallf[o + 1:o + 2], smallf[o + 2:o + 3], smallf[o + 3:o + 4]
    b_pw1 = g_small[:, o + 4:o + 6, :].reshape(1, 2 * d)
    ffn_dw = g_dwf[:, :6, :].transpose(1, 0, 2).reshape(2, 3, N_CHIPS * cs)

    def sub_vec(layer, sub, extra=None):
        m6 = mod[layer].reshape(6, d)
        rows = [pre_full[layer, sub][None], 1.0 + m6[3 * sub + 1][None], m6[3 * sub][None], m6[3 * sub + 2][None],
                post_full[layer, sub][None]]
        if extra is not None:
            rows.append(extra)
        return _pad_rows(jnp.concatenate(rows, axis=0), 8)

    vec_pool = sub_vec(0, 0, pool_scale)
    vec_f0, vec_conv, vec_f1 = sub_vec(0, 1), sub_vec(1, 0), sub_vec(1, 1)
    cvec = _pad_rows(jnp.concatenate([b_dw, ln_g, ln_b, b_pw2], axis=0), 8)

    x1, ((g_up0, g_dn0),) = _pool_forward(xs, vec_pool, poolw_full, _Comm([second]))
    w_up0, w_dn0 = g_up0, g_dn0.reshape(2, 2 * fq, d)
    (x2, h_f0, a0_f0, cc_f0, u_f0, y_f0), ((g_pw1, g_pw2, g_up1, g_dn1),) = _ffn_forward(
        x1, vec_f0, w_up0, ffn_dw[0], w_dn0, _Comm([later]))
    pw2_full = g_pw2.reshape(d, d)
    w_up1, w_dn1 = g_up1, g_dn1.reshape(2, 2 * fq, d)
    x3, h_cv, a_cv, uc_cv, z_cv, y_cv = _conv_forward(x2, vec_conv, cvec, g_pw1, b_pw1, wdw31, pw2_full)
    (dx4, h_f1, a0_f1, cc_f1, u_f1, y_f1, loss_rows), _ = _ffn_forward(x3, vec_f1, w_up1, ffn_dw[1], w_dn1, target=tgt)

    dx3, da0_f1, dy_f1, sum_f1, dwdw_f1 = _ffn_backward(dx4, x3, y_f1, a0_f1, cc_f1, vec_f1, w_up1, ffn_dw[1], w_dn1)
    dx2, da_cv, dy_cv, sum_cv, dwdw_cv = _conv_backward(dx3, x2, y_cv, a_cv, uc_cv, vec_conv, cvec, g_pw1, wdw31, pw2_full)
    dx1, da0_f0, dy_f0, sum_f0, dwdw_f0 = _ffn_backward(dx2, x1, y_f0, a0_f0, cc_f0, vec_f0, w_up0, ffn_dw[0], w_dn0)
    dx0, sum_pool, gw_pool = _pool_backward(dx1, xs, vec_pool, poolw_full)
    gw_pool4 = gw_pool.reshape(n_g, N_CHIPS, gq, gd).transpose(1, 0, 2, 3).reshape(N_CHIPS, n_g * gq, gd)

    slab = jnp.concatenate([sum_f1, sum_cv, dwdw_cv, sum_f0, sum_pool, loss_rows], axis=0)
    wide = jnp.concatenate([dwdw_f1, dwdw_f0], axis=0)
    n_slab = slab.shape[0]
    mine = jnp.concatenate([slab, wide.reshape(-1, d)], axis=0)
    rows_of_all = lax.dynamic_update_slice_in_dim(jnp.zeros((N_DEV,) + mine.shape, F32), mine[None], 2 * chip + core, 0)
    red = _Reducer(jnp.concatenate([chip1, core1]))

    def carried(call, *args, extra=()):
        ops, owners = red.steps()
        out, results = call(*args, _Comm(ops + list(extra)))
        red.absorb(owners, results[:len(ops)])
        return out, results[len(ops):]

    gw_up1, ((both_all,),) = carried(_weight_grad, h_f1[None], da0_f1, extra=[_GatherRows([rows_of_all])])
    r_up1 = red.add([gw_up1])
    r_up0 = red.add([carried(_weight_grad, h_f0[None], da0_f0)[0]])
    r_dn1 = red.add([carried(_weight_grad, u_f1, dy_f1[None])[0].reshape(N_CHIPS, fq, d)])
    r_dn0 = red.add([carried(_weight_grad, u_f0, dy_f0[None])[0].reshape(N_CHIPS, fq, d)])
    r_pw1 = red.add([carried(_weight_grad, h_cv[None], da_cv)[0]])
    r_last = red.add([carried(_weight_grad, z_cv[None], dy_cv[None])[0].reshape(N_CHIPS, dq, d), gw_pool4])

    tot_both = _sum_devices(both_all)
    slab_all, tot = both_all[:, :n_slab], tot_both[:n_slab]
    tot_wide = tot_both[n_slab:].reshape(wide.shape)
    kpad = dwdw_cv.shape[0]
    o_cv, o_dw, o_f0 = 8, 24, 24 + kpad
    o_pool, o_loss = o_f0 + 8, o_f0 + 16
    loss = jnp.sum(tot[o_loss])
    dmod_l0 = jnp.concatenate([slab_all[:, o_pool + 4], slab_all[:, o_pool + 3], slab_all[:, o_pool + 1],
                               slab_all[:, o_f0 + 4], slab_all[:, o_f0 + 3], slab_all[:, o_f0 + 1]], axis=-1)
    dmod_l1 = jnp.concatenate([slab_all[:, o_cv + 4], slab_all[:, o_cv + 3], slab_all[:, o_cv + 1],
                               slab_all[:, 4], slab_all[:, 3], slab_all[:, 1]], axis=-1)
    dmod = jnp.stack([dmod_l0, dmod_l1], axis=0)
    g_ada_b = _sum_devices(dmod.transpose(1, 0, 2))
    ncol = ada_w.shape[2]
    g_ada_w = _ada_weight_grad(c_all, _shard_cols(dmod, chip, ncol))

    g_pre = jnp.stack([jnp.stack([tot[o_pool + 2], tot[o_f0 + 2]]), jnp.stack([tot[o_cv + 2], tot[2]])])
    g_post = jnp.stack([jnp.stack([tot[o_pool + 0], tot[o_f0 + 0]]), jnp.stack([tot[o_cv + 0], tot[0]])])
    g_pool_scale = tot[o_pool + 5][None]
    g_b_pw2, g_ln_g, g_ln_b, g_b_dw = tot[o_cv + 5], tot[o_cv + 6], tot[o_cv + 7], tot[o_cv + 8]
    g_b_pw1 = jnp.concatenate([tot[o_cv + 9], tot[o_cv + 10]])
    g_w_dw31 = tot[o_dw:o_dw + kw]
    g_ffn_dw = jnp.stack([tot_wide[8:11], tot_wide[0:3]])

    grads_small = {
        "pre_g": _shard_cols(g_pre, chip, dq), "post_g": _shard_cols(g_post, chip, dq),
        "pool_scale": g_pool_scale, "cv_b_pw1": _shard_cols(g_b_pw1[None], chip, 2 * dq),
        "cv_w_dw": _shard_cols(g_w_dw31[None], chip, dq), "cv_b_dw": _shard_cols(g_b_dw[None], chip, dq),
        "cv_ln_g": _shard_cols(g_ln_g[None], chip, dq), "cv_ln_b": _shard_cols(g_ln_b[None], chip, dq),
        "cv_b_pw2": _shard_cols(g_b_pw2[None], chip, dq), "ffn_w_dw": _shard_cols(g_ffn_dw, chip, cs),
        "ada_b": g_ada_b,
    }
    params_small = {
        "pre_g": (pre_g, m_pre_g, v_pre_g), "post_g": (post_g, m_post_g, v_post_g),
        "pool_scale": (pool_scale, m_pool_scale, v_pool_scale), "cv_b_pw1": (cv_b_pw1, m_cv_b_pw1, v_cv_b_pw1),
        "cv_w_dw": (cv_w_dw, m_cv_w_dw, v_cv_w_dw), "cv_b_dw": (cv_b_dw, m_cv_b_dw, v_cv_b_dw),
        "cv_ln_g": (cv_ln_g, m_cv_ln_g, v_cv_ln_g), "cv_ln_b": (cv_ln_b, m_cv_ln_b, v_cv_ln_b),
        "cv_b_pw2": (cv_b_pw2, m_cv_b_pw2, v_cv_b_pw2), "ffn_w_dw": (ffn_w_dw, m_ffn_w_dw, v_ffn_w_dw),
        "ada_b": (ada_b, m_ada_b, v_ada_b),
    }
    names = list(params_small)
    sizes = [params_small[nm][0].size for nm in names]
    padded = [-(-sz // 1024) * 1024 for sz in sizes]

    def pack(arrs):
        flat = [jnp.pad(a.reshape(-1), (0, p - a.size)) for a, p in zip(arrs, padded)]
        return jnp.concatenate(flat).reshape(1, -1, 128)

    pk_w = pack([params_small[nm][0] for nm in names])
    pk_m = pack([params_small[nm][1] for nm in names])
    pk_v = pack([params_small[nm][2] for nm in names])
    pk_g = pack([grads_small[nm].reshape(params_small[nm][0].shape) for nm in names])
    (pk_d, pk_nm, pk_nv), _ = _adamw(pk_w, pk_g, pk_m, pk_v)

    def unpack(pk):
        flat, out, off = pk.reshape(-1), {}, 0
        for nm, sz, p in zip(names, sizes, padded):
            out[nm] = flat[off:off + sz].reshape(params_small[nm][0].shape)
            off += p
        return out

    small_d, small_m, small_v = unpack(pk_d), unpack(pk_nm), unpack(pk_nv)
    small_g = {nm: grads_small[nm].reshape(params_small[nm][0].shape) for nm in names}

    big_p = {
        "ada_w": (ada_w, m_ada_w, v_ada_w), "pool_w": (pool_w, m_pool_w, v_pool_w),
        "cv_w_pw1": (cv_w_pw1, m_cv_w_pw1, v_cv_w_pw1), "cv_w_pw2": (cv_w_pw2, m_cv_w_pw2, v_cv_w_pw2),
        "ffn_w_up": (ffn_w_up, m_ffn_w_up, v_ffn_w_up), "ffn_w_down": (ffn_w_down, m_ffn_w_down, v_ffn_w_down),
    }
    big_g, big_d, big_m, big_v = {}, {}, {}, {}

    def update(nm, grad):
        w, m, v = big_p[nm]
        as3 = lambda t: t.reshape((-1,) + w.shape[-2:])
        (dl, nm_, nv_), _ = carried(_adamw, as3(w), as3(grad), as3(m), as3(v))
        big_g[nm] = grad.reshape(w.shape)
        big_d[nm], big_m[nm], big_v[nm] = dl.reshape(w.shape), nm_.reshape(w.shape), nv_.reshape(w.shape)

    full = lambda group, k=0: group["full"][k]
    update("ffn_w_up", jnp.stack([full(r_up0), full(r_up1)]))
    update("ada_w", g_ada_w)
    update("ffn_w_down", jnp.stack([full(r_dn0), full(r_dn1)]))
    red.drain()
    update("cv_w_pw1", full(r_pw1))
    update("cv_w_pw2", full(r_last, 0))
    update("pool_w", full(r_last, 1))

    order = ["ada_w", "ada_b", "pre_g", "post_g", "pool_w", "pool_scale", "cv_w_pw1", "cv_b_pw1", "cv_w_dw", "cv_b_dw",
             "cv_ln_g", "cv_ln_b", "cv_w_pw2", "cv_b_pw2", "ffn_w_up", "ffn_w_dw", "ffn_w_down"]
    pick = lambda bigs, smalls: [bigs[nm] if nm in bigs else smalls[nm] for nm in order]
    return (loss, dx0[None], *pick(big_g, small_g), *pick(big_d, small_d), *pick(big_m, small_m),
            *pick(big_v, small_v))
```

```python
import functools

import jax
import jax.numpy as jnp
from jax import lax
from jax.experimental import pallas as pl
from jax.experimental.pallas import tpu as pltpu

F32 = jnp.float32
BF16 = jnp.bfloat16
EPS = 1e-6
N_CHIPS = 4
N_DEV = 8
POOL_WINDOWS = (2, 4, 8, 16)
POOL_HALO = 16
FFN_HALO = 16
MXU_LANES = 256
CONV_ROWS, CONV_LANES = 128, 128
ADAM_LR = 0.001
ADAM_B1 = 0.9
ADAM_B2 = 0.999
ADAM_EPS = 1e-08
ADAM_WD = 0.01
ADAM_STEP = 10
V7X_VMEM_LIMIT = 58 * 1024 * 1024
MESH = pl.DeviceIdType.MESH


def _cparams(sem=None, vmem=V7X_VMEM_LIMIT):
    return pltpu.CompilerParams(dimension_semantics=sem, vmem_limit_bytes=vmem)


def _row_tile(n, want):
    if n <= want:
        return n
    t = want - want % 8
    while n % t:
        t -= 8
    return t


def _lane_chunks(width):
    out, c = [], 0
    while c < width:
        w = min(512, width - c)
        out.append((c, w))
        c += w
    return out


def _dot(a, b):
    return jnp.dot(a, b, preferred_element_type=F32)


def _dot_nt(a, b):
    return lax.dot_general(a, b, (((1,), (1,)), ((), ())), preferred_element_type=F32)


def _store_dot_nt(dst, a_ref, b_ref):
    dst[...] = _dot_nt(a_ref[...], b_ref[...])


def _store_dot_nt2(dst, a1_ref, a2_ref, b1_ref, b2_ref):
    dst[...] = _dot_nt(a1_ref[...], b1_ref[...]) + _dot_nt(a2_ref[...], b2_ref[...])


def _dot_tn(a, b):
    return lax.dot_general(a, b, (((0,), (0,)), ((), ())), preferred_element_type=F32)


def _rms(x):
    r = lax.rsqrt(jnp.mean(x * x, axis=-1, keepdims=True) + EPS)
    return x * r, r


def _rms_bwd(dyn, yn, r):
    return r * (dyn - yn * jnp.mean(dyn * yn, axis=-1, keepdims=True))


def _sigmoid(x):
    return 0.5 * jnp.tanh(0.5 * x) + 0.5


def _colsum(x):
    return jnp.sum(x, axis=0, keepdims=True)


def _shift_down(x, k):
    return x if k == 0 else pltpu.roll(x, k, 0)


def _shift_up(x, k):
    return x if k == 0 else pltpu.roll(x, x.shape[0] - k, 0)


def _vec_rows(vec):
    return vec[0:1] * vec[1:2], vec[2:3], vec[3:4], vec[4:5]


def _norm_sums(do, yn, dh, xn, vec):
    p, q = _colsum(do * yn), _colsum(dh * xn)
    return [p * vec[3:4], p * vec[4:5], q * vec[1:2], q * vec[0:1]]


def _add_rows(sum_ref, rows):
    for k, r in enumerate(rows):
        sum_ref[k:k + 1, :] += r


def _ada_forward(c, ada_w):
    n_layers, d, ncol = ada_w.shape

    def body(c_ref, w_ref, call_ref, mod_ref, part_ref, sendbuf, send_sems, recv_sems, send2, recv2):
        x, y, cc = lax.axis_index("x"), lax.axis_index("y"), lax.axis_index("c")
        me = 4 * x + 2 * y + cc
        rel = [(x, y, 1 - cc), (1 - x, y, cc), (x, 1 - y, cc), (1 - x, 1 - y, cc),
               (1 - x, y, 1 - cc), (x, 1 - y, 1 - cc), (1 - x, 1 - y, 1 - cc)]
        cv = c_ref[...]
        call_ref[me] = jnp.broadcast_to(cv * _sigmoid(cv), (8, d))

        def gather(k, block, to):
            blk = call_ref.at[block]
            return pltpu.make_async_remote_copy(src_ref=blk, dst_ref=blk, send_sem=send_sems.at[k],
                                                recv_sem=recv_sems.at[k], device_id=to, device_id_type=MESH)

        for k, to in enumerate(rel):
            gather(k, me, to).start()
        for k, (px, py, pc) in enumerate(rel):
            gather(k, 4 * px + 2 * py + pc, rel[k]).wait_recv()
        for k, to in enumerate(rel):
            gather(k, me, to).wait_send()

        ca = call_ref[...].reshape(8 * N_DEV, d)
        for l in range(n_layers):
            part_ref[l] = jnp.dot(ca, w_ref[l], preferred_element_type=F32, precision=lax.Precision.HIGHEST)

        j = 2 * x + y
        chips = [(1 - x, y), (x, 1 - y), (1 - x, 1 - y)]

        def rows_of(b):
            return part_ref[:, pl.ds(pl.multiple_of(8 * b, 8), 8), :]

        def scatter(k, src_j, to):
            return pltpu.make_async_remote_copy(
                src_ref=sendbuf.at[k], dst_ref=mod_ref.at[src_j], send_sem=send2.at[k], recv_sem=recv2.at[k],
                device_id=to, device_id_type=MESH)

        mod_ref[j] = rows_of(me)
        for k, (px, py) in enumerate(chips):
            sendbuf[k] = rows_of(4 * px + 2 * py + cc)
            scatter(k, j, (px, py, cc)).start()
        for k, (px, py) in enumerate(chips):
            scatter(k, 2 * px + py, (px, py, cc)).wait_recv()
        for k, (px, py) in enumerate(chips):
            scatter(k, j, (px, py, cc)).wait_send()

    vm = pl.BlockSpec(memory_space=pltpu.VMEM)
    return pl.pallas_call(
        body, name="ada_forward",
        out_shape=(jax.ShapeDtypeStruct((N_DEV, 8, d), F32), jax.ShapeDtypeStruct((N_CHIPS, n_layers, 8, ncol), F32)),
        in_specs=[vm, vm], out_specs=(vm, vm),
        scratch_shapes=[pltpu.VMEM((n_layers, 8 * N_DEV, ncol), F32), pltpu.VMEM((3, n_layers, 8, ncol), F32),
                        pltpu.SemaphoreType.DMA((7,)), pltpu.SemaphoreType.DMA((7,)),
                        pltpu.SemaphoreType.DMA((3,)), pltpu.SemaphoreType.DMA((3,))],
        compiler_params=_cparams(),
    )(c, ada_w)


def _cast_into_slot(w2d, slot, n_slots=N_CHIPS, dtype=None):
    r, c = w2d.shape
    tr = _row_tile(r, 256)
    dtype = dtype or BF16

    def body(slot_ref, w_ref, o_ref):
        o_ref[0] = w_ref[...].astype(dtype)

    return pl.pallas_call(
        body, name="cast_into_slot",
        grid_spec=pltpu.PrefetchScalarGridSpec(
            num_scalar_prefetch=1, grid=(r // tr,),
            in_specs=[pl.BlockSpec((tr, c), lambda i, slot_ref: (i, 0))],
            out_specs=pl.BlockSpec((1, tr, c), lambda i, slot_ref: (slot_ref[0], i, 0))),
        out_shape=jax.ShapeDtypeStruct((n_slots, r, c), dtype), compiler_params=_cparams(("parallel",)),
    )(slot, w2d)


def _place():
    x, y, c = lax.axis_index("x"), lax.axis_index("y"), lax.axis_index("c")
    return x, y, c, [(1 - x, y), (x, 1 - y), (1 - x, 1 - y)]


def _remote(src, dst, send_sem, recv_sem, to):
    return pltpu.make_async_remote_copy(src_ref=src, dst_ref=dst, send_sem=send_sem, recv_sem=recv_sem,
                                        device_id=to, device_id_type=MESH)


class _AllGather:
    def __init__(self, bufs):
        self.arrays = list(bufs)
        self.outs = [jax.ShapeDtypeStruct(b.shape, b.dtype) for b in bufs]
        self.aliased = True
        self.n_sems = 6 * len(bufs)

    def run(self, phase, ins, outs, send_sems, recv_sems, base):
        x, y, c, chips = _place()
        j = 2 * x + y
        for k, buf in enumerate(outs):
            half = buf.shape[1] // 2

            def part(src_j, h):
                return buf.at[src_j, pl.ds(h * half, half), :]

            def ici(r, src_j, to):
                s = base + 6 * k + r
                return _remote(part(src_j, c), part(src_j, c), send_sems.at[s], recv_sems.at[s], to)

            def d2d(r, src_j, h):
                s = base + 6 * k + 3 + r
                return _remote(part(src_j, h), part(src_j, h), send_sems.at[s], recv_sems.at[s], (x, y, 1 - c))

            for r, (px, py) in enumerate(chips):
                if phase == 0:
                    ici(r, j, (px, py, c)).start()
                elif phase == 1:
                    ici(r, 2 * px + py, (px, py, c)).wait_recv()
                    d2d(r, 2 * px + py, c).start()
                else:
                    d2d(r, 2 * px + py, 1 - c).wait_recv()
                    ici(r, j, (px, py, c)).wait_send()
                    d2d(r, 2 * px + py, c).wait_send()


class _Swap:
    def __init__(self, grads):
        self.arrays = list(grads)
        self.outs = [jax.ShapeDtypeStruct((g.shape[0],) + g.shape[2:], g.dtype) for g in grads]
        self.aliased = False
        self.n_sems = len(grads)

    def run(self, phase, ins, outs, send_sems, recv_sems, base):
        x, y, c, _ = _place()
        for k in range(len(ins)):
            cp = _remote(ins[k].at[:, 1 - c], outs[k], send_sems.at[base + k], recv_sems.at[base + k], (x, y, 1 - c))
            if phase == 0:
                cp.start()
            elif phase == 2:
                cp.wait()


class _Exchange:
    def __init__(self, parts):
        self.arrays = list(parts)
        self.outs = [jax.ShapeDtypeStruct((3,) + p.shape[1:], p.dtype) for p in parts]
        self.aliased = False
        self.n_sems = 3 * len(parts)

    def run(self, phase, ins, outs, send_sems, recv_sems, base):
        x, y, c, chips = _place()
        for k in range(len(ins)):
            for r, (px, py) in enumerate(chips):
                s = base + 3 * k + r
                cp = _remote(ins[k].at[2 * px + py], outs[k].at[r], send_sems.at[s], recv_sems.at[s], (px, py, c))
                if phase == 0:
                    cp.start()
                elif phase == 2:
                    cp.wait()


class _Join:
    def __init__(self, bufs):
        self.arrays = list(bufs)
        self.outs = [jax.ShapeDtypeStruct(b.shape, b.dtype) for b in bufs]
        self.aliased = True
        self.n_sems = len(bufs)

    def run(self, phase, ins, outs, send_sems, recv_sems, base):
        x, y, c, _ = _place()
        for k, buf in enumerate(outs):
            mine = _remote(buf.at[c], buf.at[c], send_sems.at[base + k], recv_sems.at[base + k], (x, y, 1 - c))
            if phase == 0:
                mine.start()
            elif phase == 2:
                mine.wait_send()
                _remote(buf.at[1 - c], buf.at[1 - c], send_sems.at[base + k], recv_sems.at[base + k],
                        (x, y, 1 - c)).wait_recv()


class _Comm:
    def __init__(self, ops):
        self.ops = list(ops)
        self.arrays = [a for op in self.ops for a in op.arrays]
        self.outs = [o for op in self.ops for o in op.outs]
        self.n_sems = sum(op.n_sems for op in self.ops)

    def specs(self):
        return [pl.BlockSpec(memory_space=pl.ANY)] * len(self.arrays)

    def aliases(self, first_in, first_out):
        out, k = {}, 0
        for op in self.ops:
            for i in range(len(op.arrays)):
                if op.aliased:
                    out[first_in + k + i] = first_out + k + i
            k += len(op.arrays)
        return out

    def scratch(self):
        return [pltpu.SemaphoreType.DMA((self.n_sems,)), pltpu.SemaphoreType.DMA((self.n_sems,))]

    def run(self, phase, ins, outs, send_sems, recv_sems):
        k = base = 0
        for op in self.ops:
            n = len(op.arrays)
            op.run(phase, ins[k:k + n], outs[k:k + n], send_sems, recv_sems, base)
            k += n
            base += op.n_sems

    def split(self, results):
        out, k = [], 0
        for op in self.ops:
            out.append(list(results[k:k + len(op.arrays)]))
            k += len(op.arrays)
        return out


def _communicate(ops):
    comm = _Comm(ops)
    n = len(comm.arrays)

    def body(*refs):
        ins, outs, (send_sems, recv_sems) = refs[:n], refs[n:2 * n], refs[2 * n:]
        for phase in range(3):
            comm.run(phase, ins, outs, send_sems, recv_sems)

    res = pl.pallas_call(
        body, name="communicate", out_shape=tuple(comm.outs), in_specs=comm.specs(), out_specs=tuple(comm.specs()),
        input_output_aliases=comm.aliases(0, 0), scratch_shapes=comm.scratch(),
    )(*comm.arrays)
    return comm.split(res)


def _pool_core(he, w_ref, scale, first_row, halo, n_rows):
    d = he.shape[1]
    gd = d // len(POOL_WINDOWS)
    t = first_row + lax.broadcasted_iota(jnp.int32, (n_rows, 1), 0)
    pooled, ypre, cnts = [], [], []
    for g, w in enumerate(POOL_WINDOWS):
        hg = he[:, g * gd:(g + 1) * gd]
        s, k = hg, 1
        while k < w:
            s = s + _shift_down(s, k)
            k *= 2
        cnt = jnp.minimum(t + 1, w).astype(F32)
        p = s[halo:] / cnt - hg[halo:]
        pooled.append(p.astype(BF16))
        cnts.append(cnt)
        ypre.append(_dot(pooled[-1], w_ref[g]))
    return pooled, jnp.concatenate(ypre, axis=1), cnts


def _pool_forward(x, vec, pool_w, comm=None):
    s, d = x.shape
    ts = _row_tile(s, 512)
    nb = s // ts
    n_g, gd, _ = pool_w.shape
    comm = comm or _Comm([])
    nc = len(comm.arrays)

    def body(*refs):
        x_ref, vec_ref, w_ref = refs[:3]
        cin = refs[3:3 + nc]
        o_ref = refs[3 + nc]
        cout = refs[4 + nc:4 + 2 * nc]
        carry = refs[4 + 2 * nc]
        sems = refs[5 + 2 * nc:]
        i = pl.program_id(0)

        @pl.when(i == 0)
        def _():
            carry[...] = jnp.zeros_like(carry)
            if nc:
                comm.run(0, cin, cout, *sems)

        if nc:
            @pl.when(i == nb - 1)
            def _():
                comm.run(1, cin, cout, *sems)

        vec = vec_ref[...]
        a, sh, gt, gpost = _vec_rows(vec)
        xb = x_ref[...]
        xn, _ = _rms(xb)
        h = xn * a + sh
        he = jnp.concatenate([carry[...], h], axis=0)
        carry[...] = h[ts - POOL_HALO:]
        _, ypre, _ = _pool_core(he, w_ref, vec[5:6], i * ts, POOL_HALO, ts)
        yn, _ = _rms(ypre * vec[5:6])
        o_ref[...] = xb + gt * (yn * gpost)
        if nc:
            @pl.when(i == nb - 1)
            def _():
                comm.run(2, cin, cout, *sems)

    res = pl.pallas_call(
        body, name="pool_forward", grid=(nb,),
        in_specs=[pl.BlockSpec((ts, d), lambda i: (i, 0)), pl.BlockSpec((8, d), lambda i: (0, 0)),
                  pl.BlockSpec((n_g, gd, gd), lambda i: (0, 0, 0))] + comm.specs(),
        out_specs=(pl.BlockSpec((ts, d), lambda i: (i, 0)), *comm.specs()),
        out_shape=(jax.ShapeDtypeStruct((s, d), F32), *comm.outs),
        input_output_aliases=comm.aliases(3, 1),
        scratch_shapes=[pltpu.VMEM((POOL_HALO, d), F32)] + (comm.scratch() if nc else []),
        compiler_params=_cparams(("arbitrary",)),
    )(x, vec, pool_w, *comm.arrays)
    return res[0], comm.split(res[1:])


def _pool_backward(dout, x, vec, pool_w):
    s, d = x.shape
    ts = _row_tile(s, 512)
    nb = s // ts
    hb = ts // POOL_HALO
    n_g, gd, _ = pool_w.shape

    def body(do_ref, x_ref, xh_ref, vec_ref, w_ref, dx_ref, sum_ref, dw_ref, carry):
        step = pl.program_id(0)
        i = nb - 1 - step

        @pl.when(step == 0)
        def _():
            carry[...] = jnp.zeros_like(carry)
            sum_ref[...] = jnp.zeros_like(sum_ref)
            dw_ref[...] = jnp.zeros_like(dw_ref)

        vec = vec_ref[...]
        a, sh, gt, gpost = _vec_rows(vec)
        scale = vec[5:6]
        do = do_ref[...]
        xe = jnp.concatenate([xh_ref[...], x_ref[...]], axis=0)
        xne, re = _rms(xe)
        he = xne * a + sh
        rowid = lax.broadcasted_iota(jnp.int32, (POOL_HALO + ts, 1), 0)
        he = jnp.where((rowid >= POOL_HALO) | (i > 0), he, 0.0)
        xn, r = xne[POOL_HALO:], re[POOL_HALO:]
        pooled, ypre, cnts = _pool_core(he, w_ref, scale, i * ts, POOL_HALO, ts)
        yn, ry = _rms(ypre * scale)
        dyn = do * (gt * gpost)
        dy = _rms_bwd(dyn, yn, ry)
        dypre = (dy * scale).astype(BF16)
        dh_parts, q_parts = [], []
        for g, w in enumerate(POOL_WINDOWS):
            dyg = dypre[:, g * gd:(g + 1) * gd]
            dpool = _dot_nt(dyg, w_ref[g])
            dw_ref[g] += _dot_tn(pooled[g], dyg)
            q = dpool / cnts[g]
            qe = jnp.concatenate([q, carry[:, g * gd:(g + 1) * gd]], axis=0)
            acc, k = qe, 1
            while k < w:
                acc = acc + _shift_up(acc, k)
                k *= 2
            dh_parts.append(acc[:ts] - dpool)
            q_parts.append(q[:POOL_HALO])
        carry[...] = jnp.concatenate(q_parts, axis=1)
        dh = jnp.concatenate(dh_parts, axis=1)
        dxn = dh * a
        dx_ref[...] = do + _rms_bwd(dxn, xn, r)
        _add_rows(sum_ref, _norm_sums(do, yn, dh, xn, vec) + [_colsum(dh), _colsum(dy * ypre)])

    blk = lambda st: (nb - 1 - st, 0)
    return pl.pallas_call(
        body, name="pool_backward", grid=(nb,),
        in_specs=[pl.BlockSpec((ts, d), blk), pl.BlockSpec((ts, d), blk),
                  pl.BlockSpec((POOL_HALO, d), lambda st: (jnp.maximum((nb - 1 - st) * hb - 1, 0), 0)),
                  pl.BlockSpec((8, d), lambda st: (0, 0)), pl.BlockSpec((n_g, gd, gd), lambda st: (0, 0, 0))],
        out_specs=(pl.BlockSpec((ts, d), blk), pl.BlockSpec((8, d), lambda st: (0, 0)),
                   pl.BlockSpec((n_g, gd, gd), lambda st: (0, 0, 0))),
        out_shape=(jax.ShapeDtypeStruct((s, d), F32), jax.ShapeDtypeStruct((8, d), F32),
                   jax.ShapeDtypeStruct((n_g, gd, gd), F32)),
        scratch_shapes=[pltpu.VMEM((POOL_HALO, d), F32)],
        compiler_params=_cparams(("arbitrary",)),
    )(dout, x, x, vec, pool_w)


def _ffn_forward(x, vec, w_up, w_dw, w_down, comm=None, target=None):
    s, d = x.shape
    _, _, cs = w_up.shape
    ts = _row_tile(s, 256)
    nb = s // ts
    chunks = _lane_chunks(cs)
    comm = comm or _Comm([])
    nc = len(comm.arrays)
    nl = 0 if target is None else 1
    n_in, n_out = 5 + nl, 6 + nl

    def body(*refs):
        x_ref, vec_ref, wup_ref, wdw_ref, wdn_ref = refs[:5]
        cin = refs[n_in:n_in + nc]
        o_ref, h_ref, a0_ref, cc_ref, u_ref, y_ref = refs[n_in + nc:n_in + nc + 6]
        loss_ref = refs[n_in + nc + 6] if nl else None
        cout = refs[n_in + nc + n_out:n_in + 2 * nc + n_out]
        carry = refs[n_in + 2 * nc + n_out]
        sems = refs[n_in + 2 * nc + n_out + 1:]
        i = pl.program_id(0)

        @pl.when(i == 0)
        def _():
            carry[...] = jnp.zeros_like(carry)
            if nl:
                loss_ref[...] = jnp.zeros_like(loss_ref)
            if nc:
                comm.run(0, cin, cout, *sems)

        if nc:
            @pl.when(i == (3 * nb) // 4)
            def _():
                comm.run(1, cin, cout, *sems)

        vec = vec_ref[...]
        a, sh, gt, gpost = _vec_rows(vec)
        xb = x_ref[...]
        xn, _ = _rms(xb)
        hb = (xn * a + sh).astype(BF16)
        h_ref[...] = hb
        for q in range(2):
            for c0, cw in chunks:
                conv = []
                for j in (q, q + 2):
                    a0 = _dot(hb, wup_ref[j, :, c0:c0 + cw])
                    a0_ref[j, :, c0:c0 + cw] = a0.astype(BF16)
                    ae = jnp.concatenate([carry[j, :, c0:c0 + cw], a0], axis=0)
                    carry[j, :, c0:c0 + cw] = a0[ts - FFN_HALO:]
                    w = wdw_ref[:, j * cs + c0:j * cs + c0 + cw]
                    conv.append((w[2:3] * ae + w[1:2] * _shift_down(ae, 1) + w[0:1] * _shift_down(ae, 2))[FFN_HALO:])
                    cc_ref[j, :, c0:c0 + cw] = conv[-1].astype(BF16)
                u_ref[q, :, c0:c0 + cw] = (conv[0] * _sigmoid(conv[0]) * conv[1]).astype(BF16)
        y = _dot(u_ref[0], wdn_ref[0]) + _dot(u_ref[1], wdn_ref[1])
        y_ref[...] = y
        yn, _ = _rms(y)
        x_out = xb + gt * (yn * gpost)
        if nl:
            err = x_out - refs[5][...]
            o_ref[...] = err * (1.0 / d)
            loss_ref[0:1, :] += _colsum(err * err) * (0.5 / d)
        else:
            o_ref[...] = x_out
        if nc:
            @pl.when(i == nb - 1)
            def _():
                comm.run(2, cin, cout, *sems)

    const3 = lambda i: (0, 0, 0)
    res = pl.pallas_call(
        body, name="ffn_forward", grid=(nb,),
        in_specs=[pl.BlockSpec((ts, d), lambda i: (i, 0)), pl.BlockSpec((8, d), lambda i: (0, 0)),
                  pl.BlockSpec(w_up.shape, const3, pipeline_mode=pl.Buffered(1)),
                  pl.BlockSpec(w_dw.shape, lambda i: (0, 0)),
                  pl.BlockSpec(w_down.shape, const3, pipeline_mode=pl.Buffered(1))]
        + [pl.BlockSpec((ts, d), lambda i: (i, 0))] * nl + comm.specs(),
        out_specs=(pl.BlockSpec((ts, d), lambda i: (i, 0)), pl.BlockSpec((ts, d), lambda i: (i, 0)),
                   pl.BlockSpec((4, ts, cs), lambda i: (0, i, 0)), pl.BlockSpec((4, ts, cs), lambda i: (0, i, 0)),
                   pl.BlockSpec((2, ts, cs), lambda i: (0, i, 0)), pl.BlockSpec((ts, d), lambda i: (i, 0)),
                   *[pl.BlockSpec((8, d), lambda i: (0, 0))] * nl, *comm.specs()),
        out_shape=(jax.ShapeDtypeStruct((s, d), F32), jax.ShapeDtypeStruct((s, d), BF16),
                   jax.ShapeDtypeStruct((4, s, cs), BF16), jax.ShapeDtypeStruct((4, s, cs), BF16),
                   jax.ShapeDtypeStruct((2, s, cs), BF16), jax.ShapeDtypeStruct((s, d), F32),
                   *[jax.ShapeDtypeStruct((8, d), F32)] * nl, *comm.outs),
        input_output_aliases=comm.aliases(n_in, n_out),
        scratch_shapes=[pltpu.VMEM((4, FFN_HALO, cs), F32)] + (comm.scratch() if nc else []),
        compiler_params=_cparams(("arbitrary",)),
    )(x, vec, w_up, w_dw, w_down, *([target] * nl), *comm.arrays)
    return res[:n_out], comm.split(res[n_out:])


def _ffn_backward(dout, x, y, a0, cc, vec, w_up, w_dw, w_down):
    s, d = x.shape
    _, _, cs = w_up.shape
    ts = _row_tile(s, 256)
    nb = s // ts
    chunks = _lane_chunks(cs)

    def body(do_ref, x_ref, y_ref, a0_ref, cc_ref, vec_ref, wup_ref, wdw_ref, wdn_ref,
             dx_ref, da0_ref, dy_ref, sum_ref, dwdw_ref, carry, du_s, dh_s):
        step = pl.program_id(0)

        @pl.when(step == 0)
        def _():
            carry[...] = jnp.zeros_like(carry)
            sum_ref[...] = jnp.zeros_like(sum_ref)
            dwdw_ref[...] = jnp.zeros_like(dwdw_ref)

        vec = vec_ref[...]
        a, sh, gt, gpost = _vec_rows(vec)
        do = do_ref[...]
        yn, ry = _rms(y_ref[...])
        dy = _rms_bwd(do * (gt * gpost), yn, ry)
        dyb = dy.astype(BF16)
        dy_ref[...] = dyb
        order = [(q, c0, cw) for q in range(2) for c0, cw in chunks]

        def du_pieces(idx):
            q, c0, cw = order[idx]
            return [functools.partial(_store_dot_nt, du_s.at[idx % 2, :, n0:min(n0 + MXU_LANES, cw)], dy_ref,
                                      wdn_ref.at[q, c0 + n0:c0 + min(n0 + MXU_LANES, cw), :])
                    for n0 in range(0, cw, MXU_LANES)]

        def dh_pieces():
            return [functools.partial(_store_dot_nt2, dh_s.at[:, n0:n0 + MXU_LANES], da0_ref.at[0], da0_ref.at[2],
                                      wup_ref.at[0, n0:n0 + MXU_LANES, :], wup_ref.at[2, n0:n0 + MXU_LANES, :])
                    for n0 in range(0, d, MXU_LANES)]

        for piece in du_pieces(0):
            piece()
        later = dh_pieces()
        for idx, (q, c0, cw) in enumerate(order):
            work = du_pieces(idx + 1) if idx + 1 < len(order) else []
            if q == 1:
                share = -(-len(later) // (len(order) - idx))
                work, later = work + later[:share], later[share:]

            def pump(part, of=3):
                for piece in work[part::of]:
                    piece()

            cg = cc_ref[q, :, c0:c0 + cw].astype(F32)
            cv = cc_ref[q + 2, :, c0:c0 + cw].astype(F32)
            sg = _sigmoid(cg)
            sl = cg * sg
            du = du_s[idx % 2, :, :cw]
            dconv = {q: du * cv * (sg * (1.0 + cg * (1.0 - sg))), q + 2: du * sl}
            pump(0)
            for part, j in enumerate((q, q + 2)):
                dae = jnp.concatenate([dconv[j], carry[j, :, c0:c0 + cw]], axis=0)
                carry[j, :, c0:c0 + cw] = dconv[j][:FFN_HALO]
                up1 = _shift_down(dae, FFN_HALO - 1)[FFN_HALO:]
                up2 = _shift_down(dae, FFN_HALO - 2)[FFN_HALO:]
                lanes = slice(j * cs + c0, j * cs + c0 + cw)
                w = wdw_ref[:, lanes]
                da0_ref[j, :, c0:c0 + cw] = (w[2:3] * dconv[j] + w[1:2] * up1 + w[0:1] * up2).astype(BF16)
                a0 = a0_ref[j, :, c0:c0 + cw].astype(F32)
                dwdw_ref[0:1, lanes] += _colsum(up2 * a0)
                dwdw_ref[1:2, lanes] += _colsum(up1 * a0)
                dwdw_ref[2:3, lanes] += _colsum(dconv[j] * a0)
                pump(part + 1)
        dh = dh_s[...] + _dot_nt(da0_ref[1], wup_ref[1]) + _dot_nt(da0_ref[3], wup_ref[3])
        xn, r = _rms(x_ref[...])
        dx_ref[...] = do + _rms_bwd(dh * a, xn, r)
        _add_rows(sum_ref, _norm_sums(do, yn, dh, xn, vec) + [_colsum(dh)])

    blk = lambda st: (nb - 1 - st, 0)
    blk3 = lambda st: (0, nb - 1 - st, 0)
    const3 = lambda st: (0, 0, 0)
    return pl.pallas_call(
        body, name="ffn_backward", grid=(nb,),
        in_specs=[pl.BlockSpec((ts, d), blk), pl.BlockSpec((ts, d), blk), pl.BlockSpec((ts, d), blk),
                  pl.BlockSpec((4, ts, cs), blk3), pl.BlockSpec((4, ts, cs), blk3),
                  pl.BlockSpec((8, d), lambda st: (0, 0)),
                  pl.BlockSpec(w_up.shape, const3, pipeline_mode=pl.Buffered(1)),
                  pl.BlockSpec(w_dw.shape, lambda st: (0, 0)),
                  pl.BlockSpec(w_down.shape, const3, pipeline_mode=pl.Buffered(1))],
        out_specs=(pl.BlockSpec((ts, d), blk), pl.BlockSpec((4, ts, cs), blk3),
                   pl.BlockSpec((ts, d), blk), pl.BlockSpec((8, d), lambda st: (0, 0)),
                   pl.BlockSpec((8, 4 * cs), lambda st: (0, 0))),
        out_shape=(jax.ShapeDtypeStruct((s, d), F32), jax.ShapeDtypeStruct((4, s, cs), BF16),
                   jax.ShapeDtypeStruct((s, d), BF16),
                   jax.ShapeDtypeStruct((8, d), F32), jax.ShapeDtypeStruct((8, 4 * cs), F32)),
        scratch_shapes=[pltpu.VMEM((4, FFN_HALO, cs), F32), pltpu.VMEM((2, ts, max(cw for _, cw in chunks)), F32),
                        pltpu.VMEM((ts, d), F32)],
        compiler_params=_cparams(("arbitrary",)),
    )(dout, x, y, a0, cc, vec, w_up, w_dw, w_down)


def _conv_halo(width):
    return -(-(width - 1) // 8) * 8


def _conv_forward(x, vec, cvec, w_pw1, b_pw1, w_dw, w_pw2):
    s, d = x.shape
    kw = w_dw.shape[0]
    halo = _conv_halo(kw)
    ts = _row_tile(s, 256)
    hd = d // 2

    def body(x_ref, vec_ref, cvec_ref, w1_ref, b1_ref, wdw_ref, w2_ref,
             o_ref, h_ref, a_ref, uc_ref, z_ref, y_ref, carry):
        i = pl.program_id(0)

        @pl.when(i == 0)
        def _():
            carry[...] = jnp.zeros_like(carry)

        vec, cvec = vec_ref[...], cvec_ref[...]
        a, sh, gt, gpost = _vec_rows(vec)
        xb = x_ref[...]
        xn, _ = _rms(xb)
        hb = (xn * a + sh).astype(BF16)
        h_ref[...] = hb
        for j in range(4):
            a_ref[:, j * hd:(j + 1) * hd] = _dot(hb, w1_ref[j]) + b1_ref[:, j * hd:(j + 1) * hd]
        u = a_ref[:, :d] * _sigmoid(a_ref[:, d:])
        carry[halo:, :] = u
        for r0 in range(0, ts, CONV_ROWS):
            for l0 in range(0, d, CONV_LANES):
                lanes = slice(l0, l0 + CONV_LANES)
                src = carry[r0:r0 + CONV_ROWS + halo, lanes]
                acc = jnp.zeros((CONV_ROWS, CONV_LANES), F32) + cvec[0:1, lanes]
                for k in range(kw):
                    acc = acc + wdw_ref[k:k + 1, lanes] * _shift_down(src, kw - 1 - k)[halo:]
                uc_ref[r0:r0 + CONV_ROWS, lanes] = acc
        carry[:halo, :] = u[ts - halo:]
        uc = uc_ref[...]
        mu = jnp.mean(uc, axis=-1, keepdims=True)
        cen = uc - mu
        rstd = lax.rsqrt(jnp.mean(cen * cen, axis=-1, keepdims=True) + EPS)
        l = cen * rstd * cvec[1:2] + cvec[2:3]
        zb = (l * _sigmoid(l)).astype(BF16)
        z_ref[...] = zb
        y = _dot(zb, w2_ref[...]) + cvec[3:4]
        y_ref[...] = y
        yn, _ = _rms(y)
        o_ref[...] = xb + gt * (yn * gpost)

    row = lambda i: (i, 0)
    const2 = lambda i: (0, 0)
    return pl.pallas_call(
        body, name="conv_forward", grid=(s // ts,),
        in_specs=[pl.BlockSpec((ts, d), row), pl.BlockSpec((8, d), const2), pl.BlockSpec((8, d), const2),
                  pl.BlockSpec(w_pw1.shape, lambda i: (0, 0, 0)), pl.BlockSpec(b_pw1.shape, const2),
                  pl.BlockSpec(w_dw.shape, const2), pl.BlockSpec(w_pw2.shape, const2)],
        out_specs=(pl.BlockSpec((ts, d), row), pl.BlockSpec((ts, d), row), pl.BlockSpec((ts, 2 * d), row),
                   pl.BlockSpec((ts, d), row), pl.BlockSpec((ts, d), row), pl.BlockSpec((ts, d), row)),
        out_shape=(jax.ShapeDtypeStruct((s, d), F32), jax.ShapeDtypeStruct((s, d), BF16),
                   jax.ShapeDtypeStruct((s, 2 * d), F32), jax.ShapeDtypeStruct((s, d), F32),
                   jax.ShapeDtypeStruct((s, d), BF16), jax.ShapeDtypeStruct((s, d), F32)),
        scratch_shapes=[pltpu.VMEM((halo + ts, d), F32)],
        compiler_params=_cparams(("arbitrary",)),
    )(x, vec, cvec, w_pw1, b_pw1, w_dw, w_pw2)


def _conv_backward(dout, x, y, a_pre, uc, vec, cvec, w_pw1, w_dw, w_pw2):
    s, d = x.shape
    kw = w_dw.shape[0]
    kpad = -(-kw // 8) * 8
    halo = _conv_halo(kw)
    ts = _row_tile(s, 256)
    nb = s // ts
    hb = ts // halo
    hd = d // 2

    def body(do_ref, x_ref, y_ref, a_ref, ah_ref, uc_ref, vec_ref, cvec_ref, w1_ref, wdw_ref, w2_ref,
             dx_ref, da_ref, dy_ref, sum_ref, dwdw_ref, carry):
        step = pl.program_id(0)
        i = nb - 1 - step

        @pl.when(step == 0)
        def _():
            carry[...] = jnp.zeros_like(carry)
            sum_ref[...] = jnp.zeros_like(sum_ref)
            dwdw_ref[...] = jnp.zeros_like(dwdw_ref)

        vec, cvec = vec_ref[...], cvec_ref[...]
        a, sh, gt, gpost = _vec_rows(vec)
        do = do_ref[...]
        yn, ry = _rms(y_ref[...])
        dy = _rms_bwd(do * (gt * gpost), yn, ry)
        dyb = dy.astype(BF16)
        dy_ref[...] = dyb
        dz = _dot_nt(dyb, w2_ref[...])
        uc = uc_ref[...]
        mu = jnp.mean(uc, axis=-1, keepdims=True)
        cen = uc - mu
        rstd = lax.rsqrt(jnp.mean(cen * cen, axis=-1, keepdims=True) + EPS)
        lhat = cen * rstd
        l = lhat * cvec[1:2] + cvec[2:3]
        sgl = _sigmoid(l)
        dl = dz * (sgl * (1.0 + l * (1.0 - sgl)))
        dlhat = dl * cvec[1:2]
        duc = rstd * (dlhat - jnp.mean(dlhat, axis=-1, keepdims=True)
                      - lhat * jnp.mean(dlhat * lhat, axis=-1, keepdims=True))
        ae = jnp.concatenate([ah_ref[...] * (i > 0).astype(F32), a_ref[...]], axis=0)
        sgate = _sigmoid(ae[:, d:])
        val = ae[:, :d]
        ue = val * sgate
        rowid = lax.broadcasted_iota(jnp.int32, (halo + ts, 1), 0)
        ue = jnp.where((rowid >= halo) | (i > 0), ue, 0.0)
        duce = jnp.concatenate([duc, carry[...]], axis=0)
        carry[...] = duc[:halo]
        du = jnp.zeros((ts, d), F32)
        for k in range(kw):
            du = du + wdw_ref[k:k + 1, :] * _shift_down(duce, halo - (kw - 1 - k))[halo:]
            dwdw_ref[k:k + 1, :] += _colsum(duc * _shift_down(ue, kw - 1 - k)[halo:])
        sg, vl = sgate[halo:], val[halo:]
        dval = du * sg
        dgate = du * vl * (sg * (1.0 - sg))
        dvb, dgb = dval.astype(BF16), dgate.astype(BF16)
        dh = jnp.zeros((ts, d), F32)
        for j in range(2):
            da_ref[j] = dvb[:, j * hd:(j + 1) * hd]
            da_ref[j + 2] = dgb[:, j * hd:(j + 1) * hd]
            dh = dh + _dot_nt(dvb[:, j * hd:(j + 1) * hd], w1_ref[j]) + _dot_nt(dgb[:, j * hd:(j + 1) * hd], w1_ref[j + 2])
        xn, r = _rms(x_ref[...])
        dx_ref[...] = do + _rms_bwd(dh * a, xn, r)
        _add_rows(sum_ref, _norm_sums(do, yn, dh, xn, vec) + [_colsum(dh), _colsum(dy), _colsum(dl * lhat), _colsum(dl),
                            _colsum(duc), _colsum(dval), _colsum(dgate)])

    blk = lambda st: (nb - 1 - st, 0)
    const2 = lambda st: (0, 0)
    return pl.pallas_call(
        body, name="conv_backward", grid=(nb,),
        in_specs=[pl.BlockSpec((ts, d), blk), pl.BlockSpec((ts, d), blk), pl.BlockSpec((ts, d), blk),
                  pl.BlockSpec((ts, 2 * d), blk),
                  pl.BlockSpec((halo, 2 * d), lambda st: (jnp.maximum((nb - 1 - st) * hb - 1, 0), 0)),
                  pl.BlockSpec((ts, d), blk), pl.BlockSpec((8, d), const2), pl.BlockSpec((8, d), const2),
                  pl.BlockSpec(w_pw1.shape, lambda st: (0, 0, 0)), pl.BlockSpec(w_dw.shape, const2),
                  pl.BlockSpec(w_pw2.shape, const2)],
        out_specs=(pl.BlockSpec((ts, d), blk), pl.BlockSpec((4, ts, hd), lambda st: (0, nb - 1 - st, 0)),
                   pl.BlockSpec((ts, d), blk), pl.BlockSpec((16, d), const2), pl.BlockSpec((kpad, d), const2)),
        out_shape=(jax.ShapeDtypeStruct((s, d), F32), jax.ShapeDtypeStruct((4, s, hd), BF16),
                   jax.ShapeDtypeStruct((s, d), BF16), jax.ShapeDtypeStruct((16, d), F32),
                   jax.ShapeDtypeStruct((kpad, d), F32)),
        scratch_shapes=[pltpu.VMEM((halo, d), F32)],
        compiler_params=_cparams(("arbitrary",)),
    )(dout, x, y, a_pre, a_pre, uc, vec, cvec, w_pw1, w_dw, w_pw2)


def _weight_grad(a, b, comm=None):
    na, s, k = a.shape
    nb_, _, n = b.shape
    nj = max(na, nb_)
    ts = _row_tile(s, 2048)
    nt = s // ts
    comm = comm or _Comm([])
    nc = len(comm.arrays)

    def body(*refs):
        a_ref, b_ref = refs[:2]
        cin = refs[2:2 + nc]
        o_ref = refs[2 + nc]
        cout = refs[3 + nc:3 + 2 * nc]
        sems = refs[3 + 2 * nc:]
        j, t = pl.program_id(0), pl.program_id(1)

        if nc:
            @pl.when((j == 0) & (t == 0))
            def _():
                comm.run(0, cin, cout, *sems)

            @pl.when((j == nj // 2) & (t == nt // 2))
            def _():
                comm.run(1, cin, cout, *sems)

        @pl.when(t == 0)
        def _():
            o_ref[...] = jnp.zeros_like(o_ref)

        o_ref[0] += _dot_tn(a_ref[0], b_ref[0])

        if nc:
            @pl.when((j == nj - 1) & (t == nt - 1))
            def _():
                comm.run(2, cin, cout, *sems)

    res = pl.pallas_call(
        body, name="weight_grad", grid=(nj, nt),
        in_specs=[pl.BlockSpec((1, ts, k), (lambda j, t: (j, t, 0)) if na > 1 else (lambda j, t: (0, t, 0))),
                  pl.BlockSpec((1, ts, n), (lambda j, t: (j, t, 0)) if nb_ > 1 else (lambda j, t: (0, t, 0)))]
        + comm.specs(),
        out_specs=(pl.BlockSpec((1, k, n), lambda j, t: (j, 0, 0)), *comm.specs()),
        out_shape=(jax.ShapeDtypeStruct((nj, k, n), F32), *comm.outs),
        input_output_aliases=comm.aliases(2, 1),
        scratch_shapes=comm.scratch() if nc else [],
        compiler_params=_cparams(("arbitrary", "arbitrary") if nc else ("parallel", "arbitrary")),
    )(a, b, *comm.arrays)
    return res[0], comm.split(res[1:])


def _adamw(w, g, m, v, comm=None):
    nl, r, c = w.shape
    tr = _row_tile(r, 256)
    nr = r // tr
    c1 = 1.0 / (1.0 - ADAM_B1 ** ADAM_STEP)
    c2 = 1.0 / (1.0 - ADAM_B2 ** ADAM_STEP)
    comm = comm or _Comm([])
    nc = len(comm.arrays)

    def body(*refs):
        w_ref, g_ref, m_ref, v_ref = refs[:4]
        cin = refs[4:4 + nc]
        d_ref, nm_ref, nv_ref = refs[4 + nc:7 + nc]
        cout = refs[7 + nc:7 + 2 * nc]
        sems = refs[7 + 2 * nc:]
        l, i = pl.program_id(0), pl.program_id(1)
        if nc:
            @pl.when((l == 0) & (i == 0))
            def _():
                comm.run(0, cin, cout, *sems)

            @pl.when((l == nl // 2) & (i == nr // 2))
            def _():
                comm.run(1, cin, cout, *sems)

        g_ = g_ref[...]
        nm = ADAM_B1 * m_ref[...] + (1.0 - ADAM_B1) * g_
        nv = ADAM_B2 * v_ref[...] + (1.0 - ADAM_B2) * (g_ * g_)
        nm_ref[...] = nm
        nv_ref[...] = nv
        d_ref[...] = -ADAM_LR * ((nm * c1) / (jnp.sqrt(nv * c2) + ADAM_EPS) + ADAM_WD * w_ref[...])
        if nc:
            @pl.when((l == nl - 1) & (i == nr - 1))
            def _():
                comm.run(2, cin, cout, *sems)

    spec = pl.BlockSpec((1, tr, c), lambda l, i: (l, i, 0))
    shp = jax.ShapeDtypeStruct((nl, r, c), F32)
    res = pl.pallas_call(
        body, name="adamw", grid=(nl, nr), in_specs=[spec] * 4 + comm.specs(),
        out_specs=(spec,) * 3 + tuple(comm.specs()), out_shape=(shp,) * 3 + tuple(comm.outs),
        input_output_aliases=comm.aliases(4, 3), scratch_shapes=comm.scratch() if nc else [],
        compiler_params=_cparams(("arbitrary", "arbitrary") if nc else ("parallel", "parallel")),
    )(w, g, m, v, *comm.arrays)
    return res[:3], comm.split(res[3:])


def _add_my_half(g, other, idx):
    _, _, h, c = g.shape
    th = _row_tile(h, 256)

    def body(idx_ref, g_ref, o_ref, out_ref):
        out_ref[...] = (g_ref[:, 0] + o_ref[...]).astype(BF16)

    return pl.pallas_call(
        body, name="add_my_half",
        grid_spec=pltpu.PrefetchScalarGridSpec(
            num_scalar_prefetch=1, grid=(4, h // th),
            in_specs=[pl.BlockSpec((1, 1, th, c), lambda j, i, idx_ref: (j, idx_ref[1], i, 0)),
                      pl.BlockSpec((1, th, c), lambda j, i, idx_ref: (j, i, 0))],
            out_specs=pl.BlockSpec((1, th, c), lambda j, i, idx_ref: (j, i, 0))),
        out_shape=jax.ShapeDtypeStruct(other.shape, BF16),
        compiler_params=_cparams(("parallel", "parallel")),
    )(idx, g, other)


def _sum_for_my_chip(g, other, got, idx):
    _, _, h, c = g.shape
    th = _row_tile(h, 256)

    def body(idx_ref, g_ref, o_ref, q_ref, out_ref):
        out_ref[0] = (((g_ref[0, 0] + o_ref[0]) + q_ref[0].astype(F32)) + q_ref[1].astype(F32)) + q_ref[2].astype(F32)

    return pl.pallas_call(
        body, name="sum_for_my_chip",
        grid_spec=pltpu.PrefetchScalarGridSpec(
            num_scalar_prefetch=1, grid=(h // th,),
            in_specs=[pl.BlockSpec((1, 1, th, c), lambda i, idx_ref: (idx_ref[0], idx_ref[1], i, 0)),
                      pl.BlockSpec((1, th, c), lambda i, idx_ref: (idx_ref[0], i, 0)),
                      pl.BlockSpec((3, th, c), lambda i, idx_ref: (0, i, 0))],
            out_specs=pl.BlockSpec((1, th, c), lambda i, idx_ref: (idx_ref[1], i, 0))),
        out_shape=jax.ShapeDtypeStruct((2, h, c), F32),
        compiler_params=_cparams(("parallel",)),
    )(idx, g, other, got)


class _Reducer:
    def __init__(self, idx):
        self.idx = idx
        self.groups = []

    def add(self, grads):
        group = {"state": 0, "g": [g.reshape(4, 2, g.shape[1] // 2, g.shape[2]) for g in grads]}
        self.groups.append(group)
        return group

    def steps(self):
        ops, owners = [], []
        for gr in self.groups:
            if gr["state"] == 0:
                ops.append(_Swap(gr["g"]))
            elif gr["state"] == 1:
                ops.append(_Exchange(gr["parts"]))
            elif gr["state"] == 2:
                ops.append(_Join(gr["bufs"]))
            else:
                continue
            owners.append(gr)
        return ops, owners

    def absorb(self, owners, results):
        for gr, res in zip(owners, results):
            if gr["state"] == 0:
                gr["other"] = res
                gr["parts"] = [_add_my_half(g, o, self.idx) for g, o in zip(gr["g"], res)]
            elif gr["state"] == 1:
                gr["bufs"] = [_sum_for_my_chip(g, o, q, self.idx) for g, o, q in zip(gr["g"], gr["other"], res)]
            else:
                gr["full"] = [b.reshape(2 * b.shape[1], b.shape[2]) for b in res]
            gr["state"] += 1

    def drain(self):
        while any(gr["state"] < 3 for gr in self.groups):
            ops, owners = self.steps()
            self.absorb(owners, _communicate(ops))


class _GatherRows:
    def __init__(self, bufs):
        self.arrays = list(bufs)
        self.outs = [jax.ShapeDtypeStruct(b.shape, b.dtype) for b in bufs]
        self.aliased = True
        self.n_sems = 7 * len(bufs)

    def run(self, phase, ins, outs, send_sems, recv_sems, base):
        x, y, c, chips = _place()
        me, sibling = (x, y, c), (x, y, 1 - c)
        for k, buf in enumerate(outs):
            def copy(i, block_of, to):
                blk = buf.at[4 * block_of[0] + 2 * block_of[1] + block_of[2]]
                return _remote(blk, blk, send_sems.at[base + 7 * k + i], recv_sems.at[base + 7 * k + i], to)

            if phase == 0:
                copy(0, me, sibling).start()
            for r, (px, py) in enumerate(chips):
                if phase == 0:
                    copy(1 + r, me, (px, py, c)).start()
                elif phase == 1:
                    copy(1 + r, (px, py, c), me).wait_recv()
                    copy(4 + r, (px, py, c), sibling).start()
                else:
                    copy(4 + r, (px, py, 1 - c), me).wait_recv()
                    copy(1 + r, me, (px, py, c)).wait_send()
                    copy(4 + r, (px, py, c), sibling).wait_send()
            if phase == 2:
                copy(0, sibling, me).wait_recv()
                copy(0, me, sibling).wait_send()


def _sum_devices(gathered):
    nd, m, n = gathered.shape

    def body(g_ref, o_ref):
        acc = g_ref[0]
        for b in range(1, nd):
            acc = acc + g_ref[b]
        o_ref[...] = acc

    return pl.pallas_call(
        body, name="sum_devices", out_shape=jax.ShapeDtypeStruct((m, n), F32),
        in_specs=[pl.BlockSpec(memory_space=pltpu.VMEM)], out_specs=pl.BlockSpec(memory_space=pltpu.VMEM),
        compiler_params=_cparams(),
    )(gathered)


def _ada_weight_grad(c_all, dmod_cols):
    nl, nd, ncol = dmod_cols.shape
    d = c_all.shape[1]

    def body(c_ref, dm_ref, o_ref):
        o_ref[0] = lax.dot_general(c_ref[...], dm_ref[0], (((0,), (0,)), ((), ())),
                                   preferred_element_type=F32, precision=lax.Precision.HIGHEST)

    return pl.pallas_call(
        body, name="ada_weight_grad", grid=(nl,),
        in_specs=[pl.BlockSpec((nd, d), lambda l: (0, 0)), pl.BlockSpec((1, nd, ncol), lambda l: (l, 0, 0))],
        out_specs=pl.BlockSpec((1, d, ncol), lambda l: (l, 0, 0)),
        out_shape=jax.ShapeDtypeStruct((nl, d, ncol), F32), compiler_params=_cparams(("parallel",)),
    )(c_all, dmod_cols)


def _pad_rows(a, rows):
    return jnp.pad(a, ((0, rows - a.shape[0]), (0, 0)))


def _shard_cols(full, chip, width):
    return lax.dynamic_slice_in_dim(full, chip * width, width, axis=full.ndim - 1)


def kernel(x, c, ada_w, ada_b, pre_g, post_g, pool_w, pool_scale, cv_w_pw1, cv_b_pw1, cv_w_dw, cv_b_dw, cv_ln_g, cv_ln_b, cv_w_pw2, cv_b_pw2, ffn_w_up, ffn_w_dw, ffn_w_down, loss_target, m_ada_w, m_ada_b, m_pre_g, m_post_g, m_pool_w, m_pool_scale, m_cv_w_pw1, m_cv_b_pw1, m_cv_w_dw, m_cv_b_dw, m_cv_ln_g, m_cv_ln_b, m_cv_w_pw2, m_cv_b_pw2, m_ffn_w_up, m_ffn_w_dw, m_ffn_w_down, v_ada_w, v_ada_b, v_pre_g, v_post_g, v_pool_w, v_pool_scale, v_cv_w_pw1, v_cv_b_pw1, v_cv_w_dw, v_cv_b_dw, v_cv_ln_g, v_cv_ln_b, v_cv_w_pw2, v_cv_b_pw2, v_ffn_w_up, v_ffn_w_dw, v_ffn_w_down):
    s, d = x.shape[1], x.shape[2]
    dq = d // N_CHIPS
    n_g = pool_w.shape[1]
    gq = pool_w.shape[2]
    gd = pool_w.shape[3]
    kw = cv_w_dw.shape[1]
    cs = ffn_w_up.shape[2]
    fq = ffn_w_down.shape[1]
    chip = 2 * lax.axis_index("x") + lax.axis_index("y")
    core = lax.axis_index("c")
    chip1 = jnp.reshape(chip, (1,)).astype(jnp.int32)
    core1 = jnp.reshape(core, (1,)).astype(jnp.int32)
    xs, tgt = x[0], loss_target[0]

    c_rep, mod_rep = _ada_forward(c, ada_w)
    c_all = c_rep[:, 0, :]
    mod = mod_rep[:, :, 0, :].transpose(1, 0, 2).reshape(ada_b.shape) + ada_b

    small_rows = [pre_g.reshape(4, dq), post_g.reshape(4, dq), cv_w_dw[0], cv_b_dw, cv_ln_g, cv_ln_b, cv_b_pw2,
                  cv_b_pw1.reshape(2, dq)]
    small = jnp.concatenate(small_rows, axis=0)
    n_small = small.shape[0]
    small = _pad_rows(small, -(-n_small // 16) * 16)
    dwf = _pad_rows(ffn_w_dw.reshape(6, cs), 16)
    def slot(a):
        return lax.dynamic_update_slice_in_dim(jnp.zeros((N_CHIPS,) + a.shape, a.dtype), a[None], chip, axis=0)

    first = [_cast_into_slot(pool_w.reshape(n_g * gq, gd), chip1), slot(small), slot(dwf)]
    (g_pool, g_small, g_dwf), = _communicate([_AllGather(first)])
    second = _AllGather([_cast_into_slot(ffn_w_up[0], chip1), _cast_into_slot(ffn_w_down[0], chip1)])
    later = _AllGather([_cast_into_slot(cv_w_pw1[0], chip1), _cast_into_slot(cv_w_pw2[0], chip1),
                        _cast_into_slot(ffn_w_up[1], chip1), _cast_into_slot(ffn_w_down[1], chip1)])
    poolw_full = g_pool.reshape(N_CHIPS, n_g, gq, gd).transpose(1, 0, 2, 3).reshape(n_g, gd, gd)
    smallf = g_small.transpose(1, 0, 2).reshape(g_small.shape[1], d)
    pre_full, post_full = smallf[0:4].reshape(2, 2, d), smallf[4:8].reshape(2, 2, d)
    wdw31 = smallf[8:8 + kw]
    o = 8 + kw
    b_dw, ln_g, ln_b, b_pw2 = smallf[o:o + 1], smallf[o + 1:o + 2], smallf[o + 2:o + 3], smallf[o + 3:o + 4]
    b_pw1 = g_small[:, o + 4:o + 6, :].reshape(1, 2 * d)
    ffn_dw = g_dwf[:, :6, :].transpose(1, 0, 2).reshape(2, 3, N_CHIPS * cs)

    def sub_vec(layer, sub, extra=None):
        m6 = mod[layer].reshape(6, d)
        rows = [pre_full[layer, sub][None], 1.0 + m6[3 * sub + 1][None], m6[3 * sub][None], m6[3 * sub + 2][None],
                post_full[layer, sub][None]]
        if extra is not None:
            rows.append(extra)
        return _pad_rows(jnp.concatenate(rows, axis=0), 8)

    vec_pool = sub_vec(0, 0, pool_scale)
    vec_f0, vec_conv, vec_f1 = sub_vec(0, 1), sub_vec(1, 0), sub_vec(1, 1)
    cvec = _pad_rows(jnp.concatenate([b_dw, ln_g, ln_b, b_pw2], axis=0), 8)

    x1, ((g_up0, g_dn0),) = _pool_forward(xs, vec_pool, poolw_full, _Comm([second]))
    w_up0, w_dn0 = g_up0, g_dn0.reshape(2, 2 * fq, d)
    (x2, h_f0, a0_f0, cc_f0, u_f0, y_f0), ((g_pw1, g_pw2, g_up1, g_dn1),) = _ffn_forward(
        x1, vec_f0, w_up0, ffn_dw[0], w_dn0, _Comm([later]))
    pw2_full = g_pw2.reshape(d, d)
    w_up1, w_dn1 = g_up1, g_dn1.reshape(2, 2 * fq, d)
    x3, h_cv, a_cv, uc_cv, z_cv, y_cv = _conv_forward(x2, vec_conv, cvec, g_pw1, b_pw1, wdw31, pw2_full)
    (dx4, h_f1, a0_f1, cc_f1, u_f1, y_f1, loss_rows), _ = _ffn_forward(x3, vec_f1, w_up1, ffn_dw[1], w_dn1, target=tgt)

    dx3, da0_f1, dy_f1, sum_f1, dwdw_f1 = _ffn_backward(dx4, x3, y_f1, a0_f1, cc_f1, vec_f1, w_up1, ffn_dw[1], w_dn1)
    dx2, da_cv, dy_cv, sum_cv, dwdw_cv = _conv_backward(dx3, x2, y_cv, a_cv, uc_cv, vec_conv, cvec, g_pw1, wdw31, pw2_full)
    dx1, da0_f0, dy_f0, sum_f0, dwdw_f0 = _ffn_backward(dx2, x1, y_f0, a0_f0, cc_f0, vec_f0, w_up0, ffn_dw[0], w_dn0)
    dx0, sum_pool, gw_pool = _pool_backward(dx1, xs, vec_pool, poolw_full)
    gw_pool4 = gw_pool.reshape(n_g, N_CHIPS, gq, gd).transpose(1, 0, 2, 3).reshape(N_CHIPS, n_g * gq, gd)

    slab = jnp.concatenate([sum_f1, sum_cv, dwdw_cv, sum_f0, sum_pool, loss_rows], axis=0)
    wide = jnp.concatenate([dwdw_f1, dwdw_f0], axis=0)
    n_slab = slab.shape[0]
    mine = jnp.concatenate([slab, wide.reshape(-1, d)], axis=0)
    rows_of_all = _cast_into_slot(mine, 2 * chip1 + core1, N_DEV, F32)
    red = _Reducer(jnp.concatenate([chip1, core1]))

    def carried(call, *args, extra=()):
        ops, owners = red.steps()
        out, results = call(*args, _Comm(ops + list(extra)))
        red.absorb(owners, results[:len(ops)])
        return out, results[len(ops):]

    gw_up1, ((both_all,),) = carried(_weight_grad, h_f1[None], da0_f1, extra=[_GatherRows([rows_of_all])])
    r_up1 = red.add([gw_up1])
    r_up0 = red.add([carried(_weight_grad, h_f0[None], da0_f0)[0]])
    r_dn1 = red.add([carried(_weight_grad, u_f1, dy_f1[None])[0].reshape(N_CHIPS, fq, d)])
    r_dn0 = red.add([carried(_weight_grad, u_f0, dy_f0[None])[0].reshape(N_CHIPS, fq, d)])
    r_pw1 = red.add([carried(_weight_grad, h_cv[None], da_cv)[0]])
    r_last = red.add([carried(_weight_grad, z_cv[None], dy_cv[None])[0].reshape(N_CHIPS, dq, d), gw_pool4])

    tot_both = _sum_devices(both_all)
    slab_all, tot = both_all[:, :n_slab], tot_both[:n_slab]
    tot_wide = tot_both[n_slab:].reshape(wide.shape)
    kpad = dwdw_cv.shape[0]
    o_cv, o_dw, o_f0 = 8, 24, 24 + kpad
    o_pool, o_loss = o_f0 + 8, o_f0 + 16
    loss = jnp.sum(tot[o_loss])
    dmod_l0 = jnp.concatenate([slab_all[:, o_pool + 4], slab_all[:, o_pool + 3], slab_all[:, o_pool + 1],
                               slab_all[:, o_f0 + 4], slab_all[:, o_f0 + 3], slab_all[:, o_f0 + 1]], axis=-1)
    dmod_l1 = jnp.concatenate([slab_all[:, o_cv + 4], slab_all[:, o_cv + 3], slab_all[:, o_cv + 1],
                               slab_all[:, 4], slab_all[:, 3], slab_all[:, 1]], axis=-1)
    dmod = jnp.stack([dmod_l0, dmod_l1], axis=0)
    g_ada_b = _sum_devices(dmod.transpose(1, 0, 2))
    ncol = ada_w.shape[2]
    g_ada_w = _ada_weight_grad(c_all, _shard_cols(dmod, chip, ncol))

    g_pre = jnp.stack([jnp.stack([tot[o_pool + 2], tot[o_f0 + 2]]), jnp.stack([tot[o_cv + 2], tot[2]])])
    g_post = jnp.stack([jnp.stack([tot[o_pool + 0], tot[o_f0 + 0]]), jnp.stack([tot[o_cv + 0], tot[0]])])
    g_pool_scale = tot[o_pool + 5][None]
    g_b_pw2, g_ln_g, g_ln_b, g_b_dw = tot[o_cv + 5], tot[o_cv + 6], tot[o_cv + 7], tot[o_cv + 8]
    g_b_pw1 = jnp.concatenate([tot[o_cv + 9], tot[o_cv + 10]])
    g_w_dw31 = tot[o_dw:o_dw + kw]
    g_ffn_dw = jnp.stack([tot_wide[8:11], tot_wide[0:3]])

    grads_small = {
        "pre_g": _shard_cols(g_pre, chip, dq), "post_g": _shard_cols(g_post, chip, dq),
        "pool_scale": g_pool_scale, "cv_b_pw1": _shard_cols(g_b_pw1[None], chip, 2 * dq),
        "cv_w_dw": _shard_cols(g_w_dw31[None], chip, dq), "cv_b_dw": _shard_cols(g_b_dw[None], chip, dq),
        "cv_ln_g": _shard_cols(g_ln_g[None], chip, dq), "cv_ln_b": _shard_cols(g_ln_b[None], chip, dq),
        "cv_b_pw2": _shard_cols(g_b_pw2[None], chip, dq), "ffn_w_dw": _shard_cols(g_ffn_dw, chip, cs),
        "ada_b": g_ada_b,
    }
    params_small = {
        "pre_g": (pre_g, m_pre_g, v_pre_g), "post_g": (post_g, m_post_g, v_post_g),
        "pool_scale": (pool_scale, m_pool_scale, v_pool_scale), "cv_b_pw1": (cv_b_pw1, m_cv_b_pw1, v_cv_b_pw1),
        "cv_w_dw": (cv_w_dw, m_cv_w_dw, v_cv_w_dw), "cv_b_dw": (cv_b_dw, m_cv_b_dw, v_cv_b_dw),
        "cv_ln_g": (cv_ln_g, m_cv_ln_g, v_cv_ln_g), "cv_ln_b": (cv_ln_b, m_cv_ln_b, v_cv_ln_b),
        "cv_b_pw2": (cv_b_pw2, m_cv_b_pw2, v_cv_b_pw2), "ffn_w_dw": (ffn_w_dw, m_ffn_w_dw, v_ffn_w_dw),
        "ada_b": (ada_b, m_ada_b, v_ada_b),
    }
    names = list(params_small)
    sizes = [params_small[nm][0].size for nm in names]
    padded = [-(-sz // 1024) * 1024 for sz in sizes]

    def pack(arrs):
        flat = [jnp.pad(a.reshape(-1), (0, p - a.size)) for a, p in zip(arrs, padded)]
        return jnp.concatenate(flat).reshape(1, -1, 128)

    pk_w = pack([params_small[nm][0] for nm in names])
    pk_m = pack([params_small[nm][1] for nm in names])
    pk_v = pack([params_small[nm][2] for nm in names])
    pk_g = pack([grads_small[nm].reshape(params_small[nm][0].shape) for nm in names])
    (pk_d, pk_nm, pk_nv), _ = _adamw(pk_w, pk_g, pk_m, pk_v)

    def unpack(pk):
        flat, out, off = pk.reshape(-1), {}, 0
        for nm, sz, p in zip(names, sizes, padded):
            out[nm] = flat[off:off + sz].reshape(params_small[nm][0].shape)
            off += p
        return out

    small_d, small_m, small_v = unpack(pk_d), unpack(pk_nm), unpack(pk_nv)
    small_g = {nm: grads_small[nm].reshape(params_small[nm][0].shape) for nm in names}

    red.drain()
    big_p = {
        "ada_w": (ada_w, m_ada_w, v_ada_w), "pool_w": (pool_w, m_pool_w, v_pool_w),
        "cv_w_pw1": (cv_w_pw1, m_cv_w_pw1, v_cv_w_pw1), "cv_w_pw2": (cv_w_pw2, m_cv_w_pw2, v_cv_w_pw2),
        "ffn_w_up": (ffn_w_up, m_ffn_w_up, v_ffn_w_up), "ffn_w_down": (ffn_w_down, m_ffn_w_down, v_ffn_w_down),
    }
    big_g, big_d, big_m, big_v = {}, {}, {}, {}

    def update(nm, grad):
        w, m, v = big_p[nm]
        as3 = lambda t: t.reshape((-1,) + w.shape[-2:])
        (dl, nm_, nv_), _ = _adamw(as3(w), as3(grad), as3(m), as3(v))
        big_g[nm] = grad.reshape(w.shape)
        big_d[nm], big_m[nm], big_v[nm] = dl.reshape(w.shape), nm_.reshape(w.shape), nv_.reshape(w.shape)

    full = lambda group, k=0: group["full"][k]
    update("ffn_w_up", jnp.stack([full(r_up0), full(r_up1)]))
    update("ada_w", g_ada_w)
    update("ffn_w_down", jnp.stack([full(r_dn0), full(r_dn1)]))
    update("cv_w_pw1", full(r_pw1))
    update("cv_w_pw2", full(r_last, 0))
    update("pool_w", full(r_last, 1))

    order = ["ada_w", "ada_b", "pre_g", "post_g", "pool_w", "pool_scale", "cv_w_pw1", "cv_b_pw1", "cv_w_dw", "cv_b_dw",
             "cv_ln_g", "cv_ln_b", "cv_w_pw2", "cv_b_pw2", "ffn_w_up", "ffn_w_dw", "ffn_w_down"]
    pick = lambda bigs, smalls: [bigs[nm] if nm in bigs else smalls[nm] for nm in order]
    return (loss, dx0[None], *pick(big_g, small_g), *pick(big_d, small_d), *pick(big_m, small_m),
            *pick(big_v, small_v))
```

```python
import functools

import jax
import jax.numpy as jnp
from jax import lax
from jax.experimental import pallas as pl
from jax.experimental.pallas import tpu as pltpu

F32 = jnp.float32
BF16 = jnp.bfloat16
EPS = 1e-6
N_CHIPS = 4
N_DEV = 8
POOL_WINDOWS = (2, 4, 8, 16)
POOL_HALO = 16
FFN_HALO = 16
MXU_LANES = 256
CONV_ROWS, CONV_LANES = 128, 128
ADAM_LR = 0.001
ADAM_B1 = 0.9
ADAM_B2 = 0.999
ADAM_EPS = 1e-08
ADAM_WD = 0.01
ADAM_STEP = 10
V7X_VMEM_LIMIT = 58 * 1024 * 1024
MESH = pl.DeviceIdType.MESH


def _cparams(sem=None, vmem=V7X_VMEM_LIMIT):
    return pltpu.CompilerParams(dimension_semantics=sem, vmem_limit_bytes=vmem)


def _row_tile(n, want):
    if n <= want:
        return n
    t = want - want % 8
    while n % t:
        t -= 8
    return t


def _lane_chunks(width):
    out, c = [], 0
    while c < width:
        w = min(512, width - c)
        out.append((c, w))
        c += w
    return out


def _dot(a, b):
    return jnp.dot(a, b, preferred_element_type=F32)


def _dot_nt(a, b):
    return lax.dot_general(a, b, (((1,), (1,)), ((), ())), preferred_element_type=F32)


def _store_dot_nt(dst, a_ref, b_ref):
    dst[...] = _dot_nt(a_ref[...], b_ref[...])


def _store_dot_nt2(dst, a1_ref, a2_ref, b1_ref, b2_ref):
    dst[...] = _dot_nt(a1_ref[...], b1_ref[...]) + _dot_nt(a2_ref[...], b2_ref[...])


def _dot_tn(a, b):
    return lax.dot_general(a, b, (((0,), (0,)), ((), ())), preferred_element_type=F32)


def _rms(x):
    r = lax.rsqrt(jnp.mean(x * x, axis=-1, keepdims=True) + EPS)
    return x * r, r


def _rms_bwd(dyn, yn, r):
    return r * (dyn - yn * jnp.mean(dyn * yn, axis=-1, keepdims=True))


def _sigmoid(x):
    return 0.5 * jnp.tanh(0.5 * x) + 0.5


def _colsum(x):
    return jnp.sum(x, axis=0, keepdims=True)


def _shift_down(x, k):
    return x if k == 0 else pltpu.roll(x, k, 0)


def _shift_up(x, k):
    return x if k == 0 else pltpu.roll(x, x.shape[0] - k, 0)


def _vec_rows(vec):
    return vec[0:1] * vec[1:2], vec[2:3], vec[3:4], vec[4:5]


def _norm_sums(do, yn, dh, xn, vec):
    p, q = _colsum(do * yn), _colsum(dh * xn)
    return [p * vec[3:4], p * vec[4:5], q * vec[1:2], q * vec[0:1]]


def _add_rows(sum_ref, rows):
    for k, r in enumerate(rows):
        sum_ref[k:k + 1, :] += r


def _ada_forward(c, ada_w, comm=None):
    n_layers, d, ncol = ada_w.shape
    comm = comm or _Comm([])
    nc = len(comm.arrays)

    def body(*refs):
        c_ref, w_ref = refs[:2]
        cin = refs[2:2 + nc]
        call_ref, mod_ref = refs[2 + nc:4 + nc]
        cout = refs[4 + nc:4 + 2 * nc]
        part_ref, sendbuf, send_sems, recv_sems, send2, recv2 = refs[4 + 2 * nc:10 + 2 * nc]
        carried_sems = refs[10 + 2 * nc:]
        if nc:
            comm.run(0, cin, cout, *carried_sems)
        x, y, cc = lax.axis_index("x"), lax.axis_index("y"), lax.axis_index("c")
        me = 4 * x + 2 * y + cc
        rel = [(x, y, 1 - cc), (1 - x, y, cc), (x, 1 - y, cc), (1 - x, 1 - y, cc),
               (1 - x, y, 1 - cc), (x, 1 - y, 1 - cc), (1 - x, 1 - y, 1 - cc)]
        cv = c_ref[...]
        call_ref[me] = jnp.broadcast_to(cv * _sigmoid(cv), (8, d))

        def gather(k, block, to):
            blk = call_ref.at[block]
            return pltpu.make_async_remote_copy(src_ref=blk, dst_ref=blk, send_sem=send_sems.at[k],
                                                recv_sem=recv_sems.at[k], device_id=to, device_id_type=MESH)

        for k, to in enumerate(rel):
            gather(k, me, to).start()
        for k, (px, py, pc) in enumerate(rel):
            gather(k, 4 * px + 2 * py + pc, rel[k]).wait_recv()
        for k, to in enumerate(rel):
            gather(k, me, to).wait_send()

        ca = call_ref[...].reshape(8 * N_DEV, d)
        for l in range(n_layers):
            part_ref[l] = jnp.dot(ca, w_ref[l], preferred_element_type=F32, precision=lax.Precision.HIGHEST)

        j = 2 * x + y
        chips = [(1 - x, y), (x, 1 - y), (1 - x, 1 - y)]

        def rows_of(b):
            return part_ref[:, pl.ds(pl.multiple_of(8 * b, 8), 8), :]

        def scatter(k, src_j, to):
            return pltpu.make_async_remote_copy(
                src_ref=sendbuf.at[k], dst_ref=mod_ref.at[src_j], send_sem=send2.at[k], recv_sem=recv2.at[k],
                device_id=to, device_id_type=MESH)

        mod_ref[j] = rows_of(me)
        for k, (px, py) in enumerate(chips):
            sendbuf[k] = rows_of(4 * px + 2 * py + cc)
            scatter(k, j, (px, py, cc)).start()
        for k, (px, py) in enumerate(chips):
            scatter(k, 2 * px + py, (px, py, cc)).wait_recv()
        for k, (px, py) in enumerate(chips):
            scatter(k, j, (px, py, cc)).wait_send()
        if nc:
            comm.run(1, cin, cout, *carried_sems)
            comm.run(2, cin, cout, *carried_sems)

    vm = pl.BlockSpec(memory_space=pltpu.VMEM)
    res = pl.pallas_call(
        body, name="ada_forward",
        out_shape=(jax.ShapeDtypeStruct((N_DEV, 8, d), F32), jax.ShapeDtypeStruct((N_CHIPS, n_layers, 8, ncol), F32),
                   *comm.outs),
        in_specs=[vm, vm] + comm.specs(), out_specs=(vm, vm, *comm.specs()),
        input_output_aliases=comm.aliases(2, 2),
        scratch_shapes=[pltpu.VMEM((n_layers, 8 * N_DEV, ncol), F32), pltpu.VMEM((3, n_layers, 8, ncol), F32),
                        pltpu.SemaphoreType.DMA((7,)), pltpu.SemaphoreType.DMA((7,)),
                        pltpu.SemaphoreType.DMA((3,)), pltpu.SemaphoreType.DMA((3,))] + (comm.scratch() if nc else []),
        compiler_params=_cparams(),
    )(c, ada_w, *comm.arrays)
    return res[:2], comm.split(res[2:])


def _cast_into_slot(w2d, slot, n_slots=N_CHIPS, dtype=None):
    r, c = w2d.shape
    tr = _row_tile(r, 256)
    dtype = dtype or BF16

    def body(slot_ref, w_ref, o_ref):
        o_ref[0] = w_ref[...].astype(dtype)

    return pl.pallas_call(
        body, name="cast_into_slot",
        grid_spec=pltpu.PrefetchScalarGridSpec(
            num_scalar_prefetch=1, grid=(r // tr,),
            in_specs=[pl.BlockSpec((tr, c), lambda i, slot_ref: (i, 0))],
            out_specs=pl.BlockSpec((1, tr, c), lambda i, slot_ref: (slot_ref[0], i, 0))),
        out_shape=jax.ShapeDtypeStruct((n_slots, r, c), dtype), compiler_params=_cparams(("parallel",)),
    )(slot, w2d)


def _place():
    x, y, c = lax.axis_index("x"), lax.axis_index("y"), lax.axis_index("c")
    return x, y, c, [(1 - x, y), (x, 1 - y), (1 - x, 1 - y)]


def _remote(src, dst, send_sem, recv_sem, to):
    return pltpu.make_async_remote_copy(src_ref=src, dst_ref=dst, send_sem=send_sem, recv_sem=recv_sem,
                                        device_id=to, device_id_type=MESH)


class _AllGather:
    def __init__(self, bufs):
        self.arrays = list(bufs)
        self.outs = [jax.ShapeDtypeStruct(b.shape, b.dtype) for b in bufs]
        self.aliased = True
        self.n_sems = 6 * len(bufs)

    def run(self, phase, ins, outs, send_sems, recv_sems, base):
        x, y, c, chips = _place()
        j = 2 * x + y
        for k, buf in enumerate(outs):
            half = buf.shape[1] // 2

            def part(src_j, h):
                return buf.at[src_j, pl.ds(h * half, half), :]

            def ici(r, src_j, to):
                s = base + 6 * k + r
                return _remote(part(src_j, c), part(src_j, c), send_sems.at[s], recv_sems.at[s], to)

            def d2d(r, src_j, h):
                s = base + 6 * k + 3 + r
                return _remote(part(src_j, h), part(src_j, h), send_sems.at[s], recv_sems.at[s], (x, y, 1 - c))

            for r, (px, py) in enumerate(chips):
                if phase == 0:
                    ici(r, j, (px, py, c)).start()
                elif phase == 1:
                    ici(r, 2 * px + py, (px, py, c)).wait_recv()
                    d2d(r, 2 * px + py, c).start()
                else:
                    d2d(r, 2 * px + py, 1 - c).wait_recv()
                    ici(r, j, (px, py, c)).wait_send()
                    d2d(r, 2 * px + py, c).wait_send()


class _Swap:
    def __init__(self, grads):
        self.arrays = list(grads)
        self.outs = [jax.ShapeDtypeStruct((g.shape[0],) + g.shape[2:], g.dtype) for g in grads]
        self.aliased = False
        self.n_sems = len(grads)

    def run(self, phase, ins, outs, send_sems, recv_sems, base):
        x, y, c, _ = _place()
        for k in range(len(ins)):
            cp = _remote(ins[k].at[:, 1 - c], outs[k], send_sems.at[base + k], recv_sems.at[base + k], (x, y, 1 - c))
            if phase == 0:
                cp.start()
            elif phase == 2:
                cp.wait()


class _Exchange:
    def __init__(self, parts):
        self.arrays = list(parts)
        self.outs = [jax.ShapeDtypeStruct((3,) + p.shape[1:], p.dtype) for p in parts]
        self.aliased = False
        self.n_sems = 3 * len(parts)

    def run(self, phase, ins, outs, send_sems, recv_sems, base):
        x, y, c, chips = _place()
        for k in range(len(ins)):
            for r, (px, py) in enumerate(chips):
                s = base + 3 * k + r
                cp = _remote(ins[k].at[2 * px + py], outs[k].at[r], send_sems.at[s], recv_sems.at[s], (px, py, c))
                if phase == 0:
                    cp.start()
                elif phase == 2:
                    cp.wait()


class _Join:
    def __init__(self, bufs):
        self.arrays = list(bufs)
        self.outs = [jax.ShapeDtypeStruct(b.shape, b.dtype) for b in bufs]
        self.aliased = True
        self.n_sems = len(bufs)

    def run(self, phase, ins, outs, send_sems, recv_sems, base):
        x, y, c, _ = _place()
        for k, buf in enumerate(outs):
            mine = _remote(buf.at[c], buf.at[c], send_sems.at[base + k], recv_sems.at[base + k], (x, y, 1 - c))
            if phase == 0:
                mine.start()
            elif phase == 2:
                mine.wait_send()
                _remote(buf.at[1 - c], buf.at[1 - c], send_sems.at[base + k], recv_sems.at[base + k],
                        (x, y, 1 - c)).wait_recv()


class _Comm:
    def __init__(self, ops):
        self.ops = list(ops)
        self.arrays = [a for op in self.ops for a in op.arrays]
        self.outs = [o for op in self.ops for o in op.outs]
        self.n_sems = sum(op.n_sems for op in self.ops)

    def specs(self):
        return [pl.BlockSpec(memory_space=pl.ANY)] * len(self.arrays)

    def aliases(self, first_in, first_out):
        out, k = {}, 0
        for op in self.ops:
            for i in range(len(op.arrays)):
                if op.aliased:
                    out[first_in + k + i] = first_out + k + i
            k += len(op.arrays)
        return out

    def scratch(self):
        return [pltpu.SemaphoreType.DMA((self.n_sems,)), pltpu.SemaphoreType.DMA((self.n_sems,))]

    def run(self, phase, ins, outs, send_sems, recv_sems):
        k = base = 0
        for op in self.ops:
            n = len(op.arrays)
            op.run(phase, ins[k:k + n], outs[k:k + n], send_sems, recv_sems, base)
            k += n
            base += op.n_sems

    def split(self, results):
        out, k = [], 0
        for op in self.ops:
            out.append(list(results[k:k + len(op.arrays)]))
            k += len(op.arrays)
        return out


def _communicate(ops):
    comm = _Comm(ops)
    n = len(comm.arrays)

    def body(*refs):
        ins, outs, (send_sems, recv_sems) = refs[:n], refs[n:2 * n], refs[2 * n:]
        for phase in range(3):
            comm.run(phase, ins, outs, send_sems, recv_sems)

    res = pl.pallas_call(
        body, name="communicate", out_shape=tuple(comm.outs), in_specs=comm.specs(), out_specs=tuple(comm.specs()),
        input_output_aliases=comm.aliases(0, 0), scratch_shapes=comm.scratch(),
    )(*comm.arrays)
    return comm.split(res)


def _pool_core(he, w_ref, scale, first_row, halo, n_rows):
    d = he.shape[1]
    gd = d // len(POOL_WINDOWS)
    t = first_row + lax.broadcasted_iota(jnp.int32, (n_rows, 1), 0)
    pooled, ypre, cnts = [], [], []
    for g, w in enumerate(POOL_WINDOWS):
        hg = he[:, g * gd:(g + 1) * gd]
        s, k = hg, 1
        while k < w:
            s = s + _shift_down(s, k)
            k *= 2
        cnt = jnp.minimum(t + 1, w).astype(F32)
        p = s[halo:] / cnt - hg[halo:]
        pooled.append(p.astype(BF16))
        cnts.append(cnt)
        ypre.append(_dot(pooled[-1], w_ref[g]))
    return pooled, jnp.concatenate(ypre, axis=1), cnts


def _pool_forward(x, vec, pool_w, comm=None):
    s, d = x.shape
    ts = _row_tile(s, 512)
    nb = s // ts
    n_g, gd, _ = pool_w.shape
    comm = comm or _Comm([])
    nc = len(comm.arrays)

    def body(*refs):
        x_ref, vec_ref, w_ref = refs[:3]
        cin = refs[3:3 + nc]
        o_ref = refs[3 + nc]
        cout = refs[4 + nc:4 + 2 * nc]
        carry = refs[4 + 2 * nc]
        sems = refs[5 + 2 * nc:]
        i = pl.program_id(0)

        @pl.when(i == 0)
        def _():
            carry[...] = jnp.zeros_like(carry)
            if nc:
                comm.run(0, cin, cout, *sems)

        if nc:
            @pl.when(i == nb - 1)
            def _():
                comm.run(1, cin, cout, *sems)

        vec = vec_ref[...]
        a, sh, gt, gpost = _vec_rows(vec)
        xb = x_ref[...]
        xn, _ = _rms(xb)
        h = xn * a + sh
        he = jnp.concatenate([carry[...], h], axis=0)
        carry[...] = h[ts - POOL_HALO:]
        _, ypre, _ = _pool_core(he, w_ref, vec[5:6], i * ts, POOL_HALO, ts)
        yn, _ = _rms(ypre * vec[5:6])
        o_ref[...] = xb + gt * (yn * gpost)
        if nc:
            @pl.when(i == nb - 1)
            def _():
                comm.run(2, cin, cout, *sems)

    res = pl.pallas_call(
        body, name="pool_forward", grid=(nb,),
        in_specs=[pl.BlockSpec((ts, d), lambda i: (i, 0)), pl.BlockSpec((8, d), lambda i: (0, 0)),
                  pl.BlockSpec((n_g, gd, gd), lambda i: (0, 0, 0))] + comm.specs(),
        out_specs=(pl.BlockSpec((ts, d), lambda i: (i, 0)), *comm.specs()),
        out_shape=(jax.ShapeDtypeStruct((s, d), F32), *comm.outs),
        input_output_aliases=comm.aliases(3, 1),
        scratch_shapes=[pltpu.VMEM((POOL_HALO, d), F32)] + (comm.scratch() if nc else []),
        compiler_params=_cparams(("arbitrary",)),
    )(x, vec, pool_w, *comm.arrays)
    return res[0], comm.split(res[1:])


def _pool_backward(dout, x, vec, pool_w):
    s, d = x.shape
    ts = _row_tile(s, 512)
    nb = s // ts
    hb = ts // POOL_HALO
    n_g, gd, _ = pool_w.shape

    def body(do_ref, x_ref, xh_ref, vec_ref, w_ref, dx_ref, sum_ref, dw_ref, carry):
        step = pl.program_id(0)
        i = nb - 1 - step

        @pl.when(step == 0)
        def _():
            carry[...] = jnp.zeros_like(carry)
            sum_ref[...] = jnp.zeros_like(sum_ref)
            dw_ref[...] = jnp.zeros_like(dw_ref)

        vec = vec_ref[...]
        a, sh, gt, gpost = _vec_rows(vec)
        scale = vec[5:6]
        do = do_ref[...]
        xe = jnp.concatenate([xh_ref[...], x_ref[...]], axis=0)
        xne, re = _rms(xe)
        he = xne * a + sh
        rowid = lax.broadcasted_iota(jnp.int32, (POOL_HALO + ts, 1), 0)
        he = jnp.where((rowid >= POOL_HALO) | (i > 0), he, 0.0)
        xn, r = xne[POOL_HALO:], re[POOL_HALO:]
        pooled, ypre, cnts = _pool_core(he, w_ref, scale, i * ts, POOL_HALO, ts)
        yn, ry = _rms(ypre * scale)
        dyn = do * (gt * gpost)
        dy = _rms_bwd(dyn, yn, ry)
        dypre = (dy * scale).astype(BF16)
        dh_parts, q_parts = [], []
        for g, w in enumerate(POOL_WINDOWS):
            dyg = dypre[:, g * gd:(g + 1) * gd]
            dpool = _dot_nt(dyg, w_ref[g])
            dw_ref[g] += _dot_tn(pooled[g], dyg)
            q = dpool / cnts[g]
            qe = jnp.concatenate([q, carry[:, g * gd:(g + 1) * gd]], axis=0)
            acc, k = qe, 1
            while k < w:
                acc = acc + _shift_up(acc, k)
                k *= 2
            dh_parts.append(acc[:ts] - dpool)
            q_parts.append(q[:POOL_HALO])
        carry[...] = jnp.concatenate(q_parts, axis=1)
        dh = jnp.concatenate(dh_parts, axis=1)
        dxn = dh * a
        dx_ref[...] = do + _rms_bwd(dxn, xn, r)
        _add_rows(sum_ref, _norm_sums(do, yn, dh, xn, vec) + [_colsum(dh), _colsum(dy * ypre)])

    blk = lambda st: (nb - 1 - st, 0)
    return pl.pallas_call(
        body, name="pool_backward", grid=(nb,),
        in_specs=[pl.BlockSpec((ts, d), blk), pl.BlockSpec((ts, d), blk),
                  pl.BlockSpec((POOL_HALO, d), lambda st: (jnp.maximum((nb - 1 - st) * hb - 1, 0), 0)),
                  pl.BlockSpec((8, d), lambda st: (0, 0)), pl.BlockSpec((n_g, gd, gd), lambda st: (0, 0, 0))],
        out_specs=(pl.BlockSpec((ts, d), blk), pl.BlockSpec((8, d), lambda st: (0, 0)),
                   pl.BlockSpec((n_g, gd, gd), lambda st: (0, 0, 0))),
        out_shape=(jax.ShapeDtypeStruct((s, d), F32), jax.ShapeDtypeStruct((8, d), F32),
                   jax.ShapeDtypeStruct((n_g, gd, gd), F32)),
        scratch_shapes=[pltpu.VMEM((POOL_HALO, d), F32)],
        compiler_params=_cparams(("arbitrary",)),
    )(dout, x, x, vec, pool_w)


def _ffn_forward(x, vec, w_up, w_dw, w_down, comm=None, target=None):
    s, d = x.shape
    _, _, cs = w_up.shape
    ts = _row_tile(s, 256)
    nb = s // ts
    chunks = _lane_chunks(cs)
    comm = comm or _Comm([])
    nc = len(comm.arrays)
    nl = 0 if target is None else 1
    n_in, n_out = 5 + nl, 6 + nl

    def body(*refs):
        x_ref, vec_ref, wup_ref, wdw_ref, wdn_ref = refs[:5]
        cin = refs[n_in:n_in + nc]
        o_ref, h_ref, a0_ref, cc_ref, u_ref, y_ref = refs[n_in + nc:n_in + nc + 6]
        loss_ref = refs[n_in + nc + 6] if nl else None
        cout = refs[n_in + nc + n_out:n_in + 2 * nc + n_out]
        carry = refs[n_in + 2 * nc + n_out]
        sems = refs[n_in + 2 * nc + n_out + 1:]
        i = pl.program_id(0)

        @pl.when(i == 0)
        def _():
            carry[...] = jnp.zeros_like(carry)
            if nl:
                loss_ref[...] = jnp.zeros_like(loss_ref)
            if nc:
                comm.run(0, cin, cout, *sems)

        if nc:
            @pl.when(i == (3 * nb) // 4)
            def _():
                comm.run(1, cin, cout, *sems)

        vec = vec_ref[...]
        a, sh, gt, gpost = _vec_rows(vec)
        xb = x_ref[...]
        xn, _ = _rms(xb)
        hb = (xn * a + sh).astype(BF16)
        h_ref[...] = hb
        for q in range(2):
            for c0, cw in chunks:
                conv = []
                for j in (q, q + 2):
                    a0 = _dot(hb, wup_ref[j, :, c0:c0 + cw])
                    a0_ref[j, :, c0:c0 + cw] = a0.astype(BF16)
                    ae = jnp.concatenate([carry[j, :, c0:c0 + cw], a0], axis=0)
                    carry[j, :, c0:c0 + cw] = a0[ts - FFN_HALO:]
                    w = wdw_ref[:, j * cs + c0:j * cs + c0 + cw]
                    conv.append((w[2:3] * ae + w[1:2] * _shift_down(ae, 1) + w[0:1] * _shift_down(ae, 2))[FFN_HALO:])
                    cc_ref[j, :, c0:c0 + cw] = conv[-1].astype(BF16)
                u_ref[q, :, c0:c0 + cw] = (conv[0] * _sigmoid(conv[0]) * conv[1]).astype(BF16)
        y = _dot(u_ref[0], wdn_ref[0]) + _dot(u_ref[1], wdn_ref[1])
        y_ref[...] = y
        yn, _ = _rms(y)
        x_out = xb + gt * (yn * gpost)
        if nl:
            err = x_out - refs[5][...]
            o_ref[...] = err * (1.0 / d)
            loss_ref[0:1, :] += _colsum(err * err) * (0.5 / d)
        else:
            o_ref[...] = x_out
        if nc:
            @pl.when(i == nb - 1)
            def _():
                comm.run(2, cin, cout, *sems)

    const3 = lambda i: (0, 0, 0)
    res = pl.pallas_call(
        body, name="ffn_forward", grid=(nb,),
        in_specs=[pl.BlockSpec((ts, d), lambda i: (i, 0)), pl.BlockSpec((8, d), lambda i: (0, 0)),
                  pl.BlockSpec(w_up.shape, const3, pipeline_mode=pl.Buffered(1)),
                  pl.BlockSpec(w_dw.shape, lambda i: (0, 0)),
                  pl.BlockSpec(w_down.shape, const3, pipeline_mode=pl.Buffered(1))]
        + [pl.BlockSpec((ts, d), lambda i: (i, 0))] * nl + comm.specs(),
        out_specs=(pl.BlockSpec((ts, d), lambda i: (i, 0)), pl.BlockSpec((ts, d), lambda i: (i, 0)),
                   pl.BlockSpec((4, ts, cs), lambda i: (0, i, 0)), pl.BlockSpec((4, ts, cs), lambda i: (0, i, 0)),
                   pl.BlockSpec((2, ts, cs), lambda i: (0, i, 0)), pl.BlockSpec((ts, d), lambda i: (i, 0)),
                   *[pl.BlockSpec((8, d), lambda i: (0, 0))] * nl, *comm.specs()),
        out_shape=(jax.ShapeDtypeStruct((s, d), F32), jax.ShapeDtypeStruct((s, d), BF16),
                   jax.ShapeDtypeStruct((4, s, cs), BF16), jax.ShapeDtypeStruct((4, s, cs), BF16),
                   jax.ShapeDtypeStruct((2, s, cs), BF16), jax.ShapeDtypeStruct((s, d), F32),
                   *[jax.ShapeDtypeStruct((8, d), F32)] * nl, *comm.outs),
        input_output_aliases=comm.aliases(n_in, n_out),
        scratch_shapes=[pltpu.VMEM((4, FFN_HALO, cs), F32)] + (comm.scratch() if nc else []),
        compiler_params=_cparams(("arbitrary",)),
    )(x, vec, w_up, w_dw, w_down, *([target] * nl), *comm.arrays)
    return res[:n_out], comm.split(res[n_out:])


def _ffn_backward(dout, x, y, a0, cc, vec, w_up, w_dw, w_down):
    s, d = x.shape
    _, _, cs = w_up.shape
    ts = _row_tile(s, 256)
    nb = s // ts
    chunks = _lane_chunks(cs)

    def body(do_ref, x_ref, y_ref, a0_ref, cc_ref, vec_ref, wup_ref, wdw_ref, wdn_ref,
             dx_ref, da0_ref, dy_ref, sum_ref, dwdw_ref, carry, du_s, dh_s):
        step = pl.program_id(0)

        @pl.when(step == 0)
        def _():
            carry[...] = jnp.zeros_like(carry)
            sum_ref[...] = jnp.zeros_like(sum_ref)
            dwdw_ref[...] = jnp.zeros_like(dwdw_ref)

        vec = vec_ref[...]
        a, sh, gt, gpost = _vec_rows(vec)
        do = do_ref[...]
        yn, ry = _rms(y_ref[...])
        dy = _rms_bwd(do * (gt * gpost), yn, ry)
        dyb = dy.astype(BF16)
        dy_ref[...] = dyb
        order = [(q, c0, cw) for q in range(2) for c0, cw in chunks]

        def du_pieces(idx):
            q, c0, cw = order[idx]
            return [functools.partial(_store_dot_nt, du_s.at[idx % 2, :, n0:min(n0 + MXU_LANES, cw)], dy_ref,
                                      wdn_ref.at[q, c0 + n0:c0 + min(n0 + MXU_LANES, cw), :])
                    for n0 in range(0, cw, MXU_LANES)]

        def dh_pieces():
            return [functools.partial(_store_dot_nt2, dh_s.at[:, n0:n0 + MXU_LANES], da0_ref.at[0], da0_ref.at[2],
                                      wup_ref.at[0, n0:n0 + MXU_LANES, :], wup_ref.at[2, n0:n0 + MXU_LANES, :])
                    for n0 in range(0, d, MXU_LANES)]

        for piece in du_pieces(0):
            piece()
        later = dh_pieces()
        for idx, (q, c0, cw) in enumerate(order):
            work = du_pieces(idx + 1) if idx + 1 < len(order) else []
            if q == 1:
                share = -(-len(later) // (len(order) - idx))
                work, later = work + later[:share], later[share:]

            def pump(part, of=3):
                for piece in work[part::of]:
                    piece()

            cg = cc_ref[q, :, c0:c0 + cw].astype(F32)
            cv = cc_ref[q + 2, :, c0:c0 + cw].astype(F32)
            sg = _sigmoid(cg)
            sl = cg * sg
            du = du_s[idx % 2, :, :cw]
            dconv = {q: du * cv * (sg * (1.0 + cg * (1.0 - sg))), q + 2: du * sl}
            pump(0)
            for part, j in enumerate((q, q + 2)):
                dae = jnp.concatenate([dconv[j], carry[j, :, c0:c0 + cw]], axis=0)
                carry[j, :, c0:c0 + cw] = dconv[j][:FFN_HALO]
                up1 = _shift_down(dae, FFN_HALO - 1)[FFN_HALO:]
                up2 = _shift_down(dae, FFN_HALO - 2)[FFN_HALO:]
                lanes = slice(j * cs + c0, j * cs + c0 + cw)
                w = wdw_ref[:, lanes]
                da0_ref[j, :, c0:c0 + cw] = (w[2:3] * dconv[j] + w[1:2] * up1 + w[0:1] * up2).astype(BF16)
                a0 = a0_ref[j, :, c0:c0 + cw].astype(F32)
                dwdw_ref[0:1, lanes] += _colsum(up2 * a0)
                dwdw_ref[1:2, lanes] += _colsum(up1 * a0)
                dwdw_ref[2:3, lanes] += _colsum(dconv[j] * a0)
                pump(part + 1)
        dh = dh_s[...] + _dot_nt(da0_ref[1], wup_ref[1]) + _dot_nt(da0_ref[3], wup_ref[3])
        xn, r = _rms(x_ref[...])
        dx_ref[...] = do + _rms_bwd(dh * a, xn, r)
        _add_rows(sum_ref, _norm_sums(do, yn, dh, xn, vec) + [_colsum(dh)])

    blk = lambda st: (nb - 1 - st, 0)
    blk3 = lambda st: (0, nb - 1 - st, 0)
    const3 = lambda st: (0, 0, 0)
    return pl.pallas_call(
        body, name="ffn_backward", grid=(nb,),
        in_specs=[pl.BlockSpec((ts, d), blk), pl.BlockSpec((ts, d), blk), pl.BlockSpec((ts, d), blk),
                  pl.BlockSpec((4, ts, cs), blk3), pl.BlockSpec((4, ts, cs), blk3),
                  pl.BlockSpec((8, d), lambda st: (0, 0)),
                  pl.BlockSpec(w_up.shape, const3, pipeline_mode=pl.Buffered(1)),
                  pl.BlockSpec(w_dw.shape, lambda st: (0, 0)),
                  pl.BlockSpec(w_down.shape, const3, pipeline_mode=pl.Buffered(1))],
        out_specs=(pl.BlockSpec((ts, d), blk), pl.BlockSpec((4, ts, cs), blk3),
                   pl.BlockSpec((ts, d), blk), pl.BlockSpec((8, d), lambda st: (0, 0)),
                   pl.BlockSpec((8, 4 * cs), lambda st: (0, 0))),
        out_shape=(jax.ShapeDtypeStruct((s, d), F32), jax.ShapeDtypeStruct((4, s, cs), BF16),
                   jax.ShapeDtypeStruct((s, d), BF16),
                   jax.ShapeDtypeStruct((8, d), F32), jax.ShapeDtypeStruct((8, 4 * cs), F32)),
        scratch_shapes=[pltpu.VMEM((4, FFN_HALO, cs), F32), pltpu.VMEM((2, ts, max(cw for _, cw in chunks)), F32),
                        pltpu.VMEM((ts, d), F32)],
        compiler_params=_cparams(("arbitrary",)),
    )(dout, x, y, a0, cc, vec, w_up, w_dw, w_down)


def _conv_halo(width):
    return -(-(width - 1) // 8) * 8


def _conv_forward(x, vec, cvec, w_pw1, b_pw1, w_dw, w_pw2):
    s, d = x.shape
    kw = w_dw.shape[0]
    halo = _conv_halo(kw)
    ts = _row_tile(s, 256)
    hd = d // 2

    def body(x_ref, vec_ref, cvec_ref, w1_ref, b1_ref, wdw_ref, w2_ref,
             o_ref, h_ref, a_ref, uc_ref, z_ref, y_ref, carry):
        i = pl.program_id(0)

        @pl.when(i == 0)
        def _():
            carry[...] = jnp.zeros_like(carry)

        vec, cvec = vec_ref[...], cvec_ref[...]
        a, sh, gt, gpost = _vec_rows(vec)
        xb = x_ref[...]
        xn, _ = _rms(xb)
        hb = (xn * a + sh).astype(BF16)
        h_ref[...] = hb
        for j in range(4):
            a_ref[:, j * hd:(j + 1) * hd] = _dot(hb, w1_ref[j]) + b1_ref[:, j * hd:(j + 1) * hd]
        u = a_ref[:, :d] * _sigmoid(a_ref[:, d:])
        carry[halo:, :] = u
        for r0 in range(0, ts, CONV_ROWS):
            for l0 in range(0, d, CONV_LANES):
                lanes = slice(l0, l0 + CONV_LANES)
                src = carry[r0:r0 + CONV_ROWS + halo, lanes]
                acc = jnp.zeros((CONV_ROWS, CONV_LANES), F32) + cvec[0:1, lanes]
                for k in range(kw):
                    acc = acc + wdw_ref[k:k + 1, lanes] * _shift_down(src, kw - 1 - k)[halo:]
                uc_ref[r0:r0 + CONV_ROWS, lanes] = acc
        carry[:halo, :] = u[ts - halo:]
        uc = uc_ref[...]
        mu = jnp.mean(uc, axis=-1, keepdims=True)
        cen = uc - mu
        rstd = lax.rsqrt(jnp.mean(cen * cen, axis=-1, keepdims=True) + EPS)
        l = cen * rstd * cvec[1:2] + cvec[2:3]
        zb = (l * _sigmoid(l)).astype(BF16)
        z_ref[...] = zb
        y = _dot(zb, w2_ref[...]) + cvec[3:4]
        y_ref[...] = y
        yn, _ = _rms(y)
        o_ref[...] = xb + gt * (yn * gpost)

    row = lambda i: (i, 0)
    const2 = lambda i: (0, 0)
    return pl.pallas_call(
        body, name="conv_forward", grid=(s // ts,),
        in_specs=[pl.BlockSpec((ts, d), row), pl.BlockSpec((8, d), const2), pl.BlockSpec((8, d), const2),
                  pl.BlockSpec(w_pw1.shape, lambda i: (0, 0, 0)), pl.BlockSpec(b_pw1.shape, const2),
                  pl.BlockSpec(w_dw.shape, const2), pl.BlockSpec(w_pw2.shape, const2)],
        out_specs=(pl.BlockSpec((ts, d), row), pl.BlockSpec((ts, d), row), pl.BlockSpec((ts, 2 * d), row),
                   pl.BlockSpec((ts, d), row), pl.BlockSpec((ts, d), row), pl.BlockSpec((ts, d), row)),
        out_shape=(jax.ShapeDtypeStruct((s, d), F32), jax.ShapeDtypeStruct((s, d), BF16),
                   jax.ShapeDtypeStruct((s, 2 * d), F32), jax.ShapeDtypeStruct((s, d), F32),
                   jax.ShapeDtypeStruct((s, d), BF16), jax.ShapeDtypeStruct((s, d), F32)),
        scratch_shapes=[pltpu.VMEM((halo + ts, d), F32)],
        compiler_params=_cparams(("arbitrary",)),
    )(x, vec, cvec, w_pw1, b_pw1, w_dw, w_pw2)


def _conv_backward(dout, x, y, a_pre, uc, vec, cvec, w_pw1, w_dw, w_pw2):
    s, d = x.shape
    kw = w_dw.shape[0]
    kpad = -(-kw // 8) * 8
    halo = _conv_halo(kw)
    ts = _row_tile(s, 256)
    nb = s // ts
    hb = ts // halo
    hd = d // 2

    def body(do_ref, x_ref, y_ref, a_ref, ah_ref, uc_ref, vec_ref, cvec_ref, w1_ref, wdw_ref, w2_ref,
             dx_ref, da_ref, dy_ref, sum_ref, dwdw_ref, carry):
        step = pl.program_id(0)
        i = nb - 1 - step

        @pl.when(step == 0)
        def _():
            carry[...] = jnp.zeros_like(carry)
            sum_ref[...] = jnp.zeros_like(sum_ref)
            dwdw_ref[...] = jnp.zeros_like(dwdw_ref)

        vec, cvec = vec_ref[...], cvec_ref[...]
        a, sh, gt, gpost = _vec_rows(vec)
        do = do_ref[...]
        yn, ry = _rms(y_ref[...])
        dy = _rms_bwd(do * (gt * gpost), yn, ry)
        dyb = dy.astype(BF16)
        dy_ref[...] = dyb
        dz = _dot_nt(dyb, w2_ref[...])
        uc = uc_ref[...]
        mu = jnp.mean(uc, axis=-1, keepdims=True)
        cen = uc - mu
        rstd = lax.rsqrt(jnp.mean(cen * cen, axis=-1, keepdims=True) + EPS)
        lhat = cen * rstd
        l = lhat * cvec[1:2] + cvec[2:3]
        sgl = _sigmoid(l)
        dl = dz * (sgl * (1.0 + l * (1.0 - sgl)))
        dlhat = dl * cvec[1:2]
        duc = rstd * (dlhat - jnp.mean(dlhat, axis=-1, keepdims=True)
                      - lhat * jnp.mean(dlhat * lhat, axis=-1, keepdims=True))
        ae = jnp.concatenate([ah_ref[...] * (i > 0).astype(F32), a_ref[...]], axis=0)
        sgate = _sigmoid(ae[:, d:])
        val = ae[:, :d]
        ue = val * sgate
        rowid = lax.broadcasted_iota(jnp.int32, (halo + ts, 1), 0)
        ue = jnp.where((rowid >= halo) | (i > 0), ue, 0.0)
        duce = jnp.concatenate([duc, carry[...]], axis=0)
        carry[...] = duc[:halo]
        du = jnp.zeros((ts, d), F32)
        for k in range(kw):
            du = du + wdw_ref[k:k + 1, :] * _shift_down(duce, halo - (kw - 1 - k))[halo:]
            dwdw_ref[k:k + 1, :] += _colsum(duc * _shift_down(ue, kw - 1 - k)[halo:])
        sg, vl = sgate[halo:], val[halo:]
        dval = du * sg
        dgate = du * vl * (sg * (1.0 - sg))
        dvb, dgb = dval.astype(BF16), dgate.astype(BF16)
        dh = jnp.zeros((ts, d), F32)
        for j in range(2):
            da_ref[j] = dvb[:, j * hd:(j + 1) * hd]
            da_ref[j + 2] = dgb[:, j * hd:(j + 1) * hd]
            dh = dh + _dot_nt(dvb[:, j * hd:(j + 1) * hd], w1_ref[j]) + _dot_nt(dgb[:, j * hd:(j + 1) * hd], w1_ref[j + 2])
        xn, r = _rms(x_ref[...])
        dx_ref[...] = do + _rms_bwd(dh * a, xn, r)
        _add_rows(sum_ref, _norm_sums(do, yn, dh, xn, vec) + [_colsum(dh), _colsum(dy), _colsum(dl * lhat), _colsum(dl),
                            _colsum(duc), _colsum(dval), _colsum(dgate)])

    blk = lambda st: (nb - 1 - st, 0)
    const2 = lambda st: (0, 0)
    return pl.pallas_call(
        body, name="conv_backward", grid=(nb,),
        in_specs=[pl.BlockSpec((ts, d), blk), pl.BlockSpec((ts, d), blk), pl.BlockSpec((ts, d), blk),
                  pl.BlockSpec((ts, 2 * d), blk),
                  pl.BlockSpec((halo, 2 * d), lambda st: (jnp.maximum((nb - 1 - st) * hb - 1, 0), 0)),
                  pl.BlockSpec((ts, d), blk), pl.BlockSpec((8, d), const2), pl.BlockSpec((8, d), const2),
                  pl.BlockSpec(w_pw1.shape, lambda st: (0, 0, 0)), pl.BlockSpec(w_dw.shape, const2),
                  pl.BlockSpec(w_pw2.shape, const2)],
        out_specs=(pl.BlockSpec((ts, d), blk), pl.BlockSpec((4, ts, hd), lambda st: (0, nb - 1 - st, 0)),
                   pl.BlockSpec((ts, d), blk), pl.BlockSpec((16, d), const2), pl.BlockSpec((kpad, d), const2)),
        out_shape=(jax.ShapeDtypeStruct((s, d), F32), jax.ShapeDtypeStruct((4, s, hd), BF16),
                   jax.ShapeDtypeStruct((s, d), BF16), jax.ShapeDtypeStruct((16, d), F32),
                   jax.ShapeDtypeStruct((kpad, d), F32)),
        scratch_shapes=[pltpu.VMEM((halo, d), F32)],
        compiler_params=_cparams(("arbitrary",)),
    )(dout, x, y, a_pre, a_pre, uc, vec, cvec, w_pw1, w_dw, w_pw2)


def _weight_grad(a, b, comm=None):
    na, s, k = a.shape
    nb_, _, n = b.shape
    nj = max(na, nb_)
    ts = _row_tile(s, 2048)
    nt = s // ts
    comm = comm or _Comm([])
    nc = len(comm.arrays)

    def body(*refs):
        a_ref, b_ref = refs[:2]
        cin = refs[2:2 + nc]
        o_ref = refs[2 + nc]
        cout = refs[3 + nc:3 + 2 * nc]
        sems = refs[3 + 2 * nc:]
        j, t = pl.program_id(0), pl.program_id(1)

        if nc:
            @pl.when((j == 0) & (t == 0))
            def _():
                comm.run(0, cin, cout, *sems)

            @pl.when((j == nj // 2) & (t == nt // 2))
            def _():
                comm.run(1, cin, cout, *sems)

        @pl.when(t == 0)
        def _():
            o_ref[...] = jnp.zeros_like(o_ref)

        o_ref[0] += _dot_tn(a_ref[0], b_ref[0])

        if nc:
            @pl.when((j == nj - 1) & (t == nt - 1))
            def _():
                comm.run(2, cin, cout, *sems)

    res = pl.pallas_call(
        body, name="weight_grad", grid=(nj, nt),
        in_specs=[pl.BlockSpec((1, ts, k), (lambda j, t: (j, t, 0)) if na > 1 else (lambda j, t: (0, t, 0))),
                  pl.BlockSpec((1, ts, n), (lambda j, t: (j, t, 0)) if nb_ > 1 else (lambda j, t: (0, t, 0)))]
        + comm.specs(),
        out_specs=(pl.BlockSpec((1, k, n), lambda j, t: (j, 0, 0)), *comm.specs()),
        out_shape=(jax.ShapeDtypeStruct((nj, k, n), F32), *comm.outs),
        input_output_aliases=comm.aliases(2, 1),
        scratch_shapes=comm.scratch() if nc else [],
        compiler_params=_cparams(("arbitrary", "arbitrary") if nc else ("parallel", "arbitrary")),
    )(a, b, *comm.arrays)
    return res[0], comm.split(res[1:])


def _adamw_math(w, g, m, v):
    nm = ADAM_B1 * m + (1.0 - ADAM_B1) * g
    nv = ADAM_B2 * v + (1.0 - ADAM_B2) * (g * g)
    m_hat = nm * (1.0 / (1.0 - ADAM_B1 ** ADAM_STEP))
    v_hat = nv * (1.0 / (1.0 - ADAM_B2 ** ADAM_STEP))
    return -ADAM_LR * (m_hat / (jnp.sqrt(v_hat) + ADAM_EPS) + ADAM_WD * w), nm, nv


def _adamw_many(params):
    n = len(params)

    def body(*refs):
        for k in range(n):
            w_ref, g_ref, m_ref, v_ref = refs[4 * k:4 * k + 4]
            outs = refs[4 * n + 3 * k:4 * n + 3 * k + 3]
            for o_ref, val in zip(outs, _adamw_math(w_ref[...], g_ref[...], m_ref[...], v_ref[...])):
                o_ref[...] = val

    vm = pl.BlockSpec(memory_space=pltpu.VMEM)
    res = pl.pallas_call(
        body, name="adamw_many", in_specs=[vm] * (4 * n), out_specs=tuple([vm] * (3 * n)),
        out_shape=tuple(jax.ShapeDtypeStruct(p[0].shape, F32) for p in params for _ in range(3)),
        compiler_params=_cparams(),
    )(*[a for p in params for a in p])
    return [res[3 * k:3 * k + 3] for k in range(n)]


def _adamw(w, g, m, v, comm=None):
    nl, r, c = w.shape
    tr = _row_tile(r, 256)
    nr = r // tr
    comm = comm or _Comm([])
    nc = len(comm.arrays)

    def body(*refs):
        w_ref, g_ref, m_ref, v_ref = refs[:4]
        cin = refs[4:4 + nc]
        d_ref, nm_ref, nv_ref = refs[4 + nc:7 + nc]
        cout = refs[7 + nc:7 + 2 * nc]
        sems = refs[7 + 2 * nc:]
        l, i = pl.program_id(0), pl.program_id(1)
        if nc:
            @pl.when((l == 0) & (i == 0))
            def _():
                comm.run(0, cin, cout, *sems)

            @pl.when((l == nl // 2) & (i == nr // 2))
            def _():
                comm.run(1, cin, cout, *sems)

        d_ref[...], nm_ref[...], nv_ref[...] = _adamw_math(w_ref[...], g_ref[...], m_ref[...], v_ref[...])
        if nc:
            @pl.when((l == nl - 1) & (i == nr - 1))
            def _():
                comm.run(2, cin, cout, *sems)

    spec = pl.BlockSpec((1, tr, c), lambda l, i: (l, i, 0))
    shp = jax.ShapeDtypeStruct((nl, r, c), F32)
    res = pl.pallas_call(
        body, name="adamw", grid=(nl, nr), in_specs=[spec] * 4 + comm.specs(),
        out_specs=(spec,) * 3 + tuple(comm.specs()), out_shape=(shp,) * 3 + tuple(comm.outs),
        input_output_aliases=comm.aliases(4, 3), scratch_shapes=comm.scratch() if nc else [],
        compiler_params=_cparams(("arbitrary", "arbitrary") if nc else ("parallel", "parallel")),
    )(w, g, m, v, *comm.arrays)
    return res[:3], comm.split(res[3:])


def _add_my_half(g, other, idx):
    _, _, h, c = g.shape
    th = _row_tile(h, 256)

    def body(idx_ref, g_ref, o_ref, out_ref):
        out_ref[...] = (g_ref[:, 0] + o_ref[...]).astype(BF16)

    return pl.pallas_call(
        body, name="add_my_half",
        grid_spec=pltpu.PrefetchScalarGridSpec(
            num_scalar_prefetch=1, grid=(4, h // th),
            in_specs=[pl.BlockSpec((1, 1, th, c), lambda j, i, idx_ref: (j, idx_ref[1], i, 0)),
                      pl.BlockSpec((1, th, c), lambda j, i, idx_ref: (j, i, 0))],
            out_specs=pl.BlockSpec((1, th, c), lambda j, i, idx_ref: (j, i, 0))),
        out_shape=jax.ShapeDtypeStruct(other.shape, BF16),
        compiler_params=_cparams(("parallel", "parallel")),
    )(idx, g, other)


def _sum_for_my_chip(g, other, got, idx):
    _, _, h, c = g.shape
    th = _row_tile(h, 256)

    def body(idx_ref, g_ref, o_ref, q_ref, out_ref):
        out_ref[0] = (((g_ref[0, 0] + o_ref[0]) + q_ref[0].astype(F32)) + q_ref[1].astype(F32)) + q_ref[2].astype(F32)

    return pl.pallas_call(
        body, name="sum_for_my_chip",
        grid_spec=pltpu.PrefetchScalarGridSpec(
            num_scalar_prefetch=1, grid=(h // th,),
            in_specs=[pl.BlockSpec((1, 1, th, c), lambda i, idx_ref: (idx_ref[0], idx_ref[1], i, 0)),
                      pl.BlockSpec((1, th, c), lambda i, idx_ref: (idx_ref[0], i, 0)),
                      pl.BlockSpec((3, th, c), lambda i, idx_ref: (0, i, 0))],
            out_specs=pl.BlockSpec((1, th, c), lambda i, idx_ref: (idx_ref[1], i, 0))),
        out_shape=jax.ShapeDtypeStruct((2, h, c), F32),
        compiler_params=_cparams(("parallel",)),
    )(idx, g, other, got)


class _Reducer:
    def __init__(self, idx):
        self.idx = idx
        self.groups = []

    def add(self, grads):
        group = {"state": 0, "g": [g.reshape(4, 2, g.shape[1] // 2, g.shape[2]) for g in grads]}
        self.groups.append(group)
        return group

    def steps(self):
        ops, owners = [], []
        for gr in self.groups:
            if gr["state"] == 0:
                ops.append(_Swap(gr["g"]))
            elif gr["state"] == 1:
                ops.append(_Exchange(gr["parts"]))
            elif gr["state"] == 2:
                ops.append(_Join(gr["bufs"]))
            else:
                continue
            owners.append(gr)
        return ops, owners

    def absorb(self, owners, results):
        for gr, res in zip(owners, results):
            if gr["state"] == 0:
                gr["other"] = res
                gr["parts"] = [_add_my_half(g, o, self.idx) for g, o in zip(gr["g"], res)]
            elif gr["state"] == 1:
                gr["bufs"] = [_sum_for_my_chip(g, o, q, self.idx) for g, o, q in zip(gr["g"], gr["other"], res)]
            else:
                gr["full"] = [b.reshape(2 * b.shape[1], b.shape[2]) for b in res]
            gr["state"] += 1

    def drain(self):
        while any(gr["state"] < 3 for gr in self.groups):
            ops, owners = self.steps()
            self.absorb(owners, _communicate(ops))


class _GatherRows:
    def __init__(self, bufs):
        self.arrays = list(bufs)
        self.outs = [jax.ShapeDtypeStruct(b.shape, b.dtype) for b in bufs]
        self.aliased = True
        self.n_sems = 7 * len(bufs)

    def run(self, phase, ins, outs, send_sems, recv_sems, base):
        x, y, c, chips = _place()
        me, sibling = (x, y, c), (x, y, 1 - c)
        for k, buf in enumerate(outs):
            def copy(i, block_of, to):
                blk = buf.at[4 * block_of[0] + 2 * block_of[1] + block_of[2]]
                return _remote(blk, blk, send_sems.at[base + 7 * k + i], recv_sems.at[base + 7 * k + i], to)

            if phase == 0:
                copy(0, me, sibling).start()
            for r, (px, py) in enumerate(chips):
                if phase == 0:
                    copy(1 + r, me, (px, py, c)).start()
                elif phase == 1:
                    copy(1 + r, (px, py, c), me).wait_recv()
                    copy(4 + r, (px, py, c), sibling).start()
                else:
                    copy(4 + r, (px, py, 1 - c), me).wait_recv()
                    copy(1 + r, me, (px, py, c)).wait_send()
                    copy(4 + r, (px, py, c), sibling).wait_send()
            if phase == 2:
                copy(0, sibling, me).wait_recv()
                copy(0, me, sibling).wait_send()


def _sum_devices(gathered):
    nd, m, n = gathered.shape

    def body(g_ref, o_ref):
        acc = g_ref[0]
        for b in range(1, nd):
            acc = acc + g_ref[b]
        o_ref[...] = acc

    return pl.pallas_call(
        body, name="sum_devices", out_shape=jax.ShapeDtypeStruct((m, n), F32),
        in_specs=[pl.BlockSpec(memory_space=pltpu.VMEM)], out_specs=pl.BlockSpec(memory_space=pltpu.VMEM),
        compiler_params=_cparams(),
    )(gathered)


def _ada_weight_grad(c_all, dmod_cols):
    nl, nd, ncol = dmod_cols.shape
    d = c_all.shape[1]

    def body(c_ref, dm_ref, o_ref):
        o_ref[0] = lax.dot_general(c_ref[...], dm_ref[0], (((0,), (0,)), ((), ())),
                                   preferred_element_type=F32, precision=lax.Precision.HIGHEST)

    return pl.pallas_call(
        body, name="ada_weight_grad", grid=(nl,),
        in_specs=[pl.BlockSpec((nd, d), lambda l: (0, 0)), pl.BlockSpec((1, nd, ncol), lambda l: (l, 0, 0))],
        out_specs=pl.BlockSpec((1, d, ncol), lambda l: (l, 0, 0)),
        out_shape=jax.ShapeDtypeStruct((nl, d, ncol), F32), compiler_params=_cparams(("parallel",)),
    )(c_all, dmod_cols)


def _pad_rows(a, rows):
    return jnp.pad(a, ((0, rows - a.shape[0]), (0, 0)))


def _shard_cols(full, chip, width):
    return lax.dynamic_slice_in_dim(full, chip * width, width, axis=full.ndim - 1)


def kernel(x, c, ada_w, ada_b, pre_g, post_g, pool_w, pool_scale, cv_w_pw1, cv_b_pw1, cv_w_dw, cv_b_dw, cv_ln_g, cv_ln_b, cv_w_pw2, cv_b_pw2, ffn_w_up, ffn_w_dw, ffn_w_down, loss_target, m_ada_w, m_ada_b, m_pre_g, m_post_g, m_pool_w, m_pool_scale, m_cv_w_pw1, m_cv_b_pw1, m_cv_w_dw, m_cv_b_dw, m_cv_ln_g, m_cv_ln_b, m_cv_w_pw2, m_cv_b_pw2, m_ffn_w_up, m_ffn_w_dw, m_ffn_w_down, v_ada_w, v_ada_b, v_pre_g, v_post_g, v_pool_w, v_pool_scale, v_cv_w_pw1, v_cv_b_pw1, v_cv_w_dw, v_cv_b_dw, v_cv_ln_g, v_cv_ln_b, v_cv_w_pw2, v_cv_b_pw2, v_ffn_w_up, v_ffn_w_dw, v_ffn_w_down):
    s, d = x.shape[1], x.shape[2]
    dq = d // N_CHIPS
    n_g = pool_w.shape[1]
    gq = pool_w.shape[2]
    gd = pool_w.shape[3]
    kw = cv_w_dw.shape[1]
    cs = ffn_w_up.shape[2]
    fq = ffn_w_down.shape[1]
    chip = 2 * lax.axis_index("x") + lax.axis_index("y")
    core = lax.axis_index("c")
    chip1 = jnp.reshape(chip, (1,)).astype(jnp.int32)
    core1 = jnp.reshape(core, (1,)).astype(jnp.int32)
    xs, tgt = x[0], loss_target[0]

    small_rows = [pre_g.reshape(4, dq), post_g.reshape(4, dq), cv_w_dw[0], cv_b_dw, cv_ln_g, cv_ln_b, cv_b_pw2,
                  cv_b_pw1.reshape(2, dq)]
    small = jnp.concatenate(small_rows, axis=0)
    n_small = small.shape[0]
    small = _pad_rows(small, -(-n_small // 16) * 16)
    dwf = _pad_rows(ffn_w_dw.reshape(6, cs), 16)
    first = _AllGather([_cast_into_slot(pool_w.reshape(n_g * gq, gd), chip1), _cast_into_slot(small, chip1, dtype=F32),
                        _cast_into_slot(dwf, chip1, dtype=F32), _cast_into_slot(ffn_w_up[0], chip1)])
    (c_rep, mod_rep), ((g_pool, g_small, g_dwf, g_up0),) = _ada_forward(c, ada_w, _Comm([first]))
    c_all = c_rep[:, 0, :]
    mod = mod_rep[:, :, 0, :].transpose(1, 0, 2).reshape(ada_b.shape) + ada_b
    second = _AllGather([_cast_into_slot(ffn_w_down[0], chip1)])
    later = _AllGather([_cast_into_slot(cv_w_pw1[0], chip1), _cast_into_slot(cv_w_pw2[0], chip1),
                        _cast_into_slot(ffn_w_up[1], chip1), _cast_into_slot(ffn_w_down[1], chip1)])
    poolw_full = g_pool.reshape(N_CHIPS, n_g, gq, gd).transpose(1, 0, 2, 3).reshape(n_g, gd, gd)
    smallf = g_small.transpose(1, 0, 2).reshape(g_small.shape[1], d)
    pre_full, post_full = smallf[0:4].reshape(2, 2, d), smallf[4:8].reshape(2, 2, d)
    wdw31 = smallf[8:8 + kw]
    o = 8 + kw
    b_dw, ln_g, ln_b, b_pw2 = smallf[o:o + 1], smallf[o + 1:o + 2], smallf[o + 2:o + 3], smallf[o + 3:o + 4]
    b_pw1 = g_small[:, o + 4:o + 6, :].reshape(1, 2 * d)
    ffn_dw = g_dwf[:, :6, :].transpose(1, 0, 2).reshape(2, 3, N_CHIPS * cs)

    def sub_vec(layer, sub, extra=None):
        m6 = mod[layer].reshape(6, d)
        rows = [pre_full[layer, sub][None], 1.0 + m6[3 * sub + 1][None], m6[3 * sub][None], m6[3 * sub + 2][None],
                post_full[layer, sub][None]]
        if extra is not None:
            rows.append(extra)
        return _pad_rows(jnp.concatenate(rows, axis=0), 8)

    vec_pool = sub_vec(0, 0, pool_scale)
    vec_f0, vec_conv, vec_f1 = sub_vec(0, 1), sub_vec(1, 0), sub_vec(1, 1)
    cvec = _pad_rows(jnp.concatenate([b_dw, ln_g, ln_b, b_pw2], axis=0), 8)

    x1, ((g_dn0,),) = _pool_forward(xs, vec_pool, poolw_full, _Comm([second]))
    w_up0, w_dn0 = g_up0, g_dn0.reshape(2, 2 * fq, d)
    (x2, h_f0, a0_f0, cc_f0, u_f0, y_f0), ((g_pw1, g_pw2, g_up1, g_dn1),) = _ffn_forward(
        x1, vec_f0, w_up0, ffn_dw[0], w_dn0, _Comm([later]))
    pw2_full = g_pw2.reshape(d, d)
    w_up1, w_dn1 = g_up1, g_dn1.reshape(2, 2 * fq, d)
    x3, h_cv, a_cv, uc_cv, z_cv, y_cv = _conv_forward(x2, vec_conv, cvec, g_pw1, b_pw1, wdw31, pw2_full)
    (dx4, h_f1, a0_f1, cc_f1, u_f1, y_f1, loss_rows), _ = _ffn_forward(x3, vec_f1, w_up1, ffn_dw[1], w_dn1, target=tgt)

    dx3, da0_f1, dy_f1, sum_f1, dwdw_f1 = _ffn_backward(dx4, x3, y_f1, a0_f1, cc_f1, vec_f1, w_up1, ffn_dw[1], w_dn1)
    dx2, da_cv, dy_cv, sum_cv, dwdw_cv = _conv_backward(dx3, x2, y_cv, a_cv, uc_cv, vec_conv, cvec, g_pw1, wdw31, pw2_full)
    dx1, da0_f0, dy_f0, sum_f0, dwdw_f0 = _ffn_backward(dx2, x1, y_f0, a0_f0, cc_f0, vec_f0, w_up0, ffn_dw[0], w_dn0)
    dx0, sum_pool, gw_pool = _pool_backward(dx1, xs, vec_pool, poolw_full)
    gw_pool4 = gw_pool.reshape(n_g, N_CHIPS, gq, gd).transpose(1, 0, 2, 3).reshape(N_CHIPS, n_g * gq, gd)

    slab = jnp.concatenate([sum_f1, sum_cv, dwdw_cv, sum_f0, sum_pool, loss_rows], axis=0)
    wide = jnp.concatenate([dwdw_f1, dwdw_f0], axis=0)
    n_slab = slab.shape[0]
    mine = jnp.concatenate([slab, wide.reshape(-1, d)], axis=0)
    rows_of_all = _cast_into_slot(mine, 2 * chip1 + core1, N_DEV, F32)
    red = _Reducer(jnp.concatenate([chip1, core1]))

    def carried(call, *args, extra=()):
        ops, owners = red.steps()
        out, results = call(*args, _Comm(ops + list(extra)))
        red.absorb(owners, results[:len(ops)])
        return out, results[len(ops):]

    gw_up1, ((both_all,),) = carried(_weight_grad, h_f1[None], da0_f1, extra=[_GatherRows([rows_of_all])])
    r_up1 = red.add([gw_up1])
    r_up0 = red.add([carried(_weight_grad, h_f0[None], da0_f0)[0]])
    r_dn1 = red.add([carried(_weight_grad, u_f1, dy_f1[None])[0].reshape(N_CHIPS, fq, d)])
    r_dn0 = red.add([carried(_weight_grad, u_f0, dy_f0[None])[0].reshape(N_CHIPS, fq, d)])
    r_pw1 = red.add([carried(_weight_grad, h_cv[None], da_cv)[0]])
    r_last = red.add([carried(_weight_grad, z_cv[None], dy_cv[None])[0].reshape(N_CHIPS, dq, d), gw_pool4])

    tot_both = _sum_devices(both_all)
    slab_all, tot = both_all[:, :n_slab], tot_both[:n_slab]
    tot_wide = tot_both[n_slab:].reshape(wide.shape)
    kpad = dwdw_cv.shape[0]
    o_cv, o_dw, o_f0 = 8, 24, 24 + kpad
    o_pool, o_loss = o_f0 + 8, o_f0 + 16
    loss = jnp.sum(tot[o_loss])
    dmod_l0 = jnp.concatenate([slab_all[:, o_pool + 4], slab_all[:, o_pool + 3], slab_all[:, o_pool + 1],
                               slab_all[:, o_f0 + 4], slab_all[:, o_f0 + 3], slab_all[:, o_f0 + 1]], axis=-1)
    dmod_l1 = jnp.concatenate([slab_all[:, o_cv + 4], slab_all[:, o_cv + 3], slab_all[:, o_cv + 1],
                               slab_all[:, 4], slab_all[:, 3], slab_all[:, 1]], axis=-1)
    dmod = jnp.stack([dmod_l0, dmod_l1], axis=0)
    g_ada_b = _sum_devices(dmod.transpose(1, 0, 2))
    ncol = ada_w.shape[2]
    g_ada_w = _ada_weight_grad(c_all, _shard_cols(dmod, chip, ncol))

    g_pre = jnp.stack([jnp.stack([tot[o_pool + 2], tot[o_f0 + 2]]), jnp.stack([tot[o_cv + 2], tot[2]])])
    g_post = jnp.stack([jnp.stack([tot[o_pool + 0], tot[o_f0 + 0]]), jnp.stack([tot[o_cv + 0], tot[0]])])
    g_pool_scale = tot[o_pool + 5][None]
    g_b_pw2, g_ln_g, g_ln_b, g_b_dw = tot[o_cv + 5], tot[o_cv + 6], tot[o_cv + 7], tot[o_cv + 8]
    g_b_pw1 = jnp.concatenate([tot[o_cv + 9], tot[o_cv + 10]])
    g_w_dw31 = tot[o_dw:o_dw + kw]
    g_ffn_dw = jnp.stack([tot_wide[8:11], tot_wide[0:3]])

    grads_small = {
        "pre_g": _shard_cols(g_pre, chip, dq), "post_g": _shard_cols(g_post, chip, dq),
        "pool_scale": g_pool_scale, "cv_b_pw1": _shard_cols(g_b_pw1[None], chip, 2 * dq),
        "cv_w_dw": _shard_cols(g_w_dw31[None], chip, dq), "cv_b_dw": _shard_cols(g_b_dw[None], chip, dq),
        "cv_ln_g": _shard_cols(g_ln_g[None], chip, dq), "cv_ln_b": _shard_cols(g_ln_b[None], chip, dq),
        "cv_b_pw2": _shard_cols(g_b_pw2[None], chip, dq), "ffn_w_dw": _shard_cols(g_ffn_dw, chip, cs),
        "ada_b": g_ada_b,
    }
    params_small = {
        "pre_g": (pre_g, m_pre_g, v_pre_g), "post_g": (post_g, m_post_g, v_post_g),
        "pool_scale": (pool_scale, m_pool_scale, v_pool_scale), "cv_b_pw1": (cv_b_pw1, m_cv_b_pw1, v_cv_b_pw1),
        "cv_w_dw": (cv_w_dw, m_cv_w_dw, v_cv_w_dw), "cv_b_dw": (cv_b_dw, m_cv_b_dw, v_cv_b_dw),
        "cv_ln_g": (cv_ln_g, m_cv_ln_g, v_cv_ln_g), "cv_ln_b": (cv_ln_b, m_cv_ln_b, v_cv_ln_b),
        "cv_b_pw2": (cv_b_pw2, m_cv_b_pw2, v_cv_b_pw2), "ffn_w_dw": (ffn_w_dw, m_ffn_w_dw, v_ffn_w_dw),
        "ada_b": (ada_b, m_ada_b, v_ada_b),
    }
    names = list(params_small)
    as2 = lambda t: t.reshape(-1, t.shape[-1])
    small_g = {nm: grads_small[nm].reshape(params_small[nm][0].shape) for nm in names}
    updated = _adamw_many([(as2(params_small[nm][0]), as2(small_g[nm]), as2(params_small[nm][1]),
                            as2(params_small[nm][2])) for nm in names])
    small_d = {nm: u[0].reshape(params_small[nm][0].shape) for nm, u in zip(names, updated)}
    small_m = {nm: u[1].reshape(params_small[nm][0].shape) for nm, u in zip(names, updated)}
    small_v = {nm: u[2].reshape(params_small[nm][0].shape) for nm, u in zip(names, updated)}

    red.drain()
    big_p = {
        "ada_w": (ada_w, m_ada_w, v_ada_w), "pool_w": (pool_w, m_pool_w, v_pool_w),
        "cv_w_pw1": (cv_w_pw1, m_cv_w_pw1, v_cv_w_pw1), "cv_w_pw2": (cv_w_pw2, m_cv_w_pw2, v_cv_w_pw2),
        "ffn_w_up": (ffn_w_up, m_ffn_w_up, v_ffn_w_up), "ffn_w_down": (ffn_w_down, m_ffn_w_down, v_ffn_w_down),
    }
    big_g, big_d, big_m, big_v = {}, {}, {}, {}

    def update(nm, grad):
        w, m, v = big_p[nm]
        as3 = lambda t: t.reshape((-1,) + w.shape[-2:])
        (dl, nm_, nv_), _ = _adamw(as3(w), as3(grad), as3(m), as3(v))
        big_g[nm] = grad.reshape(w.shape)
        big_d[nm], big_m[nm], big_v[nm] = dl.reshape(w.shape), nm_.reshape(w.shape), nv_.reshape(w.shape)

    full = lambda group, k=0: group["full"][k]
    update("ffn_w_up", jnp.stack([full(r_up0), full(r_up1)]))
    update("ada_w", g_ada_w)
    update("ffn_w_down", jnp.stack([full(r_dn0), full(r_dn1)]))
    update("cv_w_pw1", full(r_pw1))
    update("cv_w_pw2", full(r_last, 0))
    update("pool_w", full(r_last, 1))

    order = ["ada_w", "ada_b", "pre_g", "post_g", "pool_w", "pool_scale", "cv_w_pw1", "cv_b_pw1", "cv_w_dw", "cv_b_dw",
             "cv_ln_g", "cv_ln_b", "cv_w_pw2", "cv_b_pw2", "ffn_w_up", "ffn_w_dw", "ffn_w_down"]
    pick = lambda bigs, smalls: [bigs[nm] if nm in bigs else smalls[nm] for nm in order]
    return (loss, dx0[None], *pick(big_g, small_g), *pick(big_d, small_d), *pick(big_m, small_m),
            *pick(big_v, small_v))
```

```python
import functools

import jax
import jax.numpy as jnp
from jax import lax
from jax.experimental import pallas as pl
from jax.experimental.pallas import tpu as pltpu

F32 = jnp.float32
BF16 = jnp.bfloat16
EPS = 1e-6
N_CHIPS = 4
N_DEV = 8
POOL_WINDOWS = (2, 4, 8, 16)
POOL_HALO = 16
FFN_HALO = 16
MXU_LANES = 256
CONV_ROWS, CONV_LANES = 128, 128
ADAM_LR = 0.001
ADAM_B1 = 0.9
ADAM_B2 = 0.999
ADAM_EPS = 1e-08
ADAM_WD = 0.01
ADAM_STEP = 10
V7X_VMEM_LIMIT = 58 * 1024 * 1024
MESH = pl.DeviceIdType.MESH


def _cparams(sem=None, vmem=V7X_VMEM_LIMIT):
    return pltpu.CompilerParams(dimension_semantics=sem, vmem_limit_bytes=vmem)


def _row_tile(n, want):
    if n <= want:
        return n
    t = want - want % 8
    while n % t:
        t -= 8
    return t


def _lane_chunks(width):
    out, c = [], 0
    while c < width:
        w = min(512, width - c)
        out.append((c, w))
        c += w
    return out


def _dot(a, b):
    return jnp.dot(a, b, preferred_element_type=F32)


def _dot_nt(a, b):
    return lax.dot_general(a, b, (((1,), (1,)), ((), ())), preferred_element_type=F32)


def _store_dot_nt(dst, a_ref, b_ref):
    dst[...] = _dot_nt(a_ref[...], b_ref[...])


def _store_dot_nt2(dst, a1_ref, a2_ref, b1_ref, b2_ref):
    dst[...] = _dot_nt(a1_ref[...], b1_ref[...]) + _dot_nt(a2_ref[...], b2_ref[...])


def _dot_tn(a, b):
    return lax.dot_general(a, b, (((0,), (0,)), ((), ())), preferred_element_type=F32)


def _rms(x):
    r = lax.rsqrt(jnp.mean(x * x, axis=-1, keepdims=True) + EPS)
    return x * r, r


def _rms_bwd(dyn, yn, r):
    return r * (dyn - yn * jnp.mean(dyn * yn, axis=-1, keepdims=True))


def _sigmoid(x):
    return 0.5 * jnp.tanh(0.5 * x) + 0.5


def _colsum(x):
    return jnp.sum(x, axis=0, keepdims=True)


def _shift_down(x, k):
    return x if k == 0 else pltpu.roll(x, k, 0)


def _shift_up(x, k):
    return x if k == 0 else pltpu.roll(x, x.shape[0] - k, 0)


def _vec_rows(vec):
    return vec[0:1] * vec[1:2], vec[2:3], vec[3:4], vec[4:5]


def _norm_sums(do, yn, dh, xn, vec):
    p, q = _colsum(do * yn), _colsum(dh * xn)
    return [p * vec[3:4], p * vec[4:5], q * vec[1:2], q * vec[0:1]]


def _add_rows(sum_ref, rows):
    for k, r in enumerate(rows):
        sum_ref[k:k + 1, :] += r


def _ada_forward(c, ada_w, comm=None):
    n_layers, d, ncol = ada_w.shape
    comm = comm or _Comm([])
    nc = len(comm.arrays)

    def body(*refs):
        c_ref, w_ref = refs[:2]
        cin = refs[2:2 + nc]
        call_ref, mod_ref = refs[2 + nc:4 + nc]
        cout = refs[4 + nc:4 + 2 * nc]
        part_ref, sendbuf, send_sems, recv_sems, send2, recv2 = refs[4 + 2 * nc:10 + 2 * nc]
        carried_sems = refs[10 + 2 * nc:]
        if nc:
            comm.run(0, cin, cout, *carried_sems)
        x, y, cc = lax.axis_index("x"), lax.axis_index("y"), lax.axis_index("c")
        me = 4 * x + 2 * y + cc
        rel = [(x, y, 1 - cc), (1 - x, y, cc), (x, 1 - y, cc), (1 - x, 1 - y, cc),
               (1 - x, y, 1 - cc), (x, 1 - y, 1 - cc), (1 - x, 1 - y, 1 - cc)]
        cv = c_ref[...]
        call_ref[me] = jnp.broadcast_to(cv * _sigmoid(cv), (8, d))

        def gather(k, block, to):
            blk = call_ref.at[block]
            return pltpu.make_async_remote_copy(src_ref=blk, dst_ref=blk, send_sem=send_sems.at[k],
                                                recv_sem=recv_sems.at[k], device_id=to, device_id_type=MESH)

        for k, to in enumerate(rel):
            gather(k, me, to).start()
        for k, (px, py, pc) in enumerate(rel):
            gather(k, 4 * px + 2 * py + pc, rel[k]).wait_recv()
        for k, to in enumerate(rel):
            gather(k, me, to).wait_send()

        ca = call_ref[...].reshape(8 * N_DEV, d)
        for l in range(n_layers):
            part_ref[l] = jnp.dot(ca, w_ref[l], preferred_element_type=F32, precision=lax.Precision.HIGHEST)

        j = 2 * x + y
        chips = [(1 - x, y), (x, 1 - y), (1 - x, 1 - y)]

        def rows_of(b):
            return part_ref[:, pl.ds(pl.multiple_of(8 * b, 8), 8), :]

        def scatter(k, src_j, to):
            return pltpu.make_async_remote_copy(
                src_ref=sendbuf.at[k], dst_ref=mod_ref.at[src_j], send_sem=send2.at[k], recv_sem=recv2.at[k],
                device_id=to, device_id_type=MESH)

        mod_ref[j] = rows_of(me)
        for k, (px, py) in enumerate(chips):
            sendbuf[k] = rows_of(4 * px + 2 * py + cc)
            scatter(k, j, (px, py, cc)).start()
        for k, (px, py) in enumerate(chips):
            scatter(k, 2 * px + py, (px, py, cc)).wait_recv()
        for k, (px, py) in enumerate(chips):
            scatter(k, j, (px, py, cc)).wait_send()
        if nc:
            comm.run(1, cin, cout, *carried_sems)
            comm.run(2, cin, cout, *carried_sems)

    vm = pl.BlockSpec(memory_space=pltpu.VMEM)
    res = pl.pallas_call(
        body, name="ada_forward",
        out_shape=(jax.ShapeDtypeStruct((N_DEV, 8, d), F32), jax.ShapeDtypeStruct((N_CHIPS, n_layers, 8, ncol), F32),
                   *comm.outs),
        in_specs=[vm, vm] + comm.specs(), out_specs=(vm, vm, *comm.specs()),
        input_output_aliases=comm.aliases(2, 2),
        scratch_shapes=[pltpu.VMEM((n_layers, 8 * N_DEV, ncol), F32), pltpu.VMEM((3, n_layers, 8, ncol), F32),
                        pltpu.SemaphoreType.DMA((7,)), pltpu.SemaphoreType.DMA((7,)),
                        pltpu.SemaphoreType.DMA((3,)), pltpu.SemaphoreType.DMA((3,))] + (comm.scratch() if nc else []),
        compiler_params=_cparams(),
    )(c, ada_w, *comm.arrays)
    return res[:2], comm.split(res[2:])


def _cast_into_slot(w2d, slot, n_slots=N_CHIPS, dtype=None):
    r, c = w2d.shape
    tr = _row_tile(r, 256)
    dtype = dtype or BF16

    def body(slot_ref, w_ref, o_ref):
        o_ref[0] = w_ref[...].astype(dtype)

    return pl.pallas_call(
        body, name="cast_into_slot",
        grid_spec=pltpu.PrefetchScalarGridSpec(
            num_scalar_prefetch=1, grid=(r // tr,),
            in_specs=[pl.BlockSpec((tr, c), lambda i, slot_ref: (i, 0))],
            out_specs=pl.BlockSpec((1, tr, c), lambda i, slot_ref: (slot_ref[0], i, 0))),
        out_shape=jax.ShapeDtypeStruct((n_slots, r, c), dtype), compiler_params=_cparams(("parallel",)),
    )(slot, w2d)


def _place():
    x, y, c = lax.axis_index("x"), lax.axis_index("y"), lax.axis_index("c")
    return x, y, c, [(1 - x, y), (x, 1 - y), (1 - x, 1 - y)]


def _remote(src, dst, send_sem, recv_sem, to):
    return pltpu.make_async_remote_copy(src_ref=src, dst_ref=dst, send_sem=send_sem, recv_sem=recv_sem,
                                        device_id=to, device_id_type=MESH)


class _AllGather:
    def __init__(self, bufs):
        self.arrays = list(bufs)
        self.outs = [jax.ShapeDtypeStruct(b.shape, b.dtype) for b in bufs]
        self.aliased = True
        self.n_sems = 6 * len(bufs)

    def run(self, phase, ins, outs, send_sems, recv_sems, base):
        x, y, c, chips = _place()
        j = 2 * x + y
        for k, buf in enumerate(outs):
            half = buf.shape[1] // 2

            def part(src_j, h):
                return buf.at[src_j, pl.ds(h * half, half), :]

            def ici(r, src_j, to):
                s = base + 6 * k + r
                return _remote(part(src_j, c), part(src_j, c), send_sems.at[s], recv_sems.at[s], to)

            def d2d(r, src_j, h):
                s = base + 6 * k + 3 + r
                return _remote(part(src_j, h), part(src_j, h), send_sems.at[s], recv_sems.at[s], (x, y, 1 - c))

            for r, (px, py) in enumerate(chips):
                if phase == 0:
                    ici(r, j, (px, py, c)).start()
                elif phase == 1:
                    ici(r, 2 * px + py, (px, py, c)).wait_recv()
                    d2d(r, 2 * px + py, c).start()
                else:
                    d2d(r, 2 * px + py, 1 - c).wait_recv()
                    ici(r, j, (px, py, c)).wait_send()
                    d2d(r, 2 * px + py, c).wait_send()


class _Swap:
    def __init__(self, grads):
        self.arrays = list(grads)
        self.outs = [jax.ShapeDtypeStruct((g.shape[0],) + g.shape[2:], g.dtype) for g in grads]
        self.aliased = False
        self.n_sems = len(grads)

    def run(self, phase, ins, outs, send_sems, recv_sems, base):
        x, y, c, _ = _place()
        for k in range(len(ins)):
            cp = _remote(ins[k].at[:, 1 - c], outs[k], send_sems.at[base + k], recv_sems.at[base + k], (x, y, 1 - c))
            if phase == 0:
                cp.start()
            elif phase == 2:
                cp.wait()


class _Exchange:
    def __init__(self, parts):
        self.arrays = list(parts)
        self.outs = [jax.ShapeDtypeStruct((3,) + p.shape[1:], p.dtype) for p in parts]
        self.aliased = False
        self.n_sems = 3 * len(parts)

    def run(self, phase, ins, outs, send_sems, recv_sems, base):
        x, y, c, chips = _place()
        for k in range(len(ins)):
            for r, (px, py) in enumerate(chips):
                s = base + 3 * k + r
                cp = _remote(ins[k].at[2 * px + py], outs[k].at[r], send_sems.at[s], recv_sems.at[s], (px, py, c))
                if phase == 0:
                    cp.start()
                elif phase == 2:
                    cp.wait()


class _Join:
    def __init__(self, bufs):
        self.arrays = list(bufs)
        self.outs = [jax.ShapeDtypeStruct(b.shape, b.dtype) for b in bufs]
        self.aliased = True
        self.n_sems = len(bufs)

    def run(self, phase, ins, outs, send_sems, recv_sems, base):
        x, y, c, _ = _place()
        for k, buf in enumerate(outs):
            mine = _remote(buf.at[c], buf.at[c], send_sems.at[base + k], recv_sems.at[base + k], (x, y, 1 - c))
            if phase == 0:
                mine.start()
            elif phase == 2:
                mine.wait_send()
                _remote(buf.at[1 - c], buf.at[1 - c], send_sems.at[base + k], recv_sems.at[base + k],
                        (x, y, 1 - c)).wait_recv()


class _Comm:
    def __init__(self, ops):
        self.ops = list(ops)
        self.arrays = [a for op in self.ops for a in op.arrays]
        self.outs = [o for op in self.ops for o in op.outs]
        self.n_sems = sum(op.n_sems for op in self.ops)

    def specs(self):
        return [pl.BlockSpec(memory_space=pl.ANY)] * len(self.arrays)

    def aliases(self, first_in, first_out):
        out, k = {}, 0
        for op in self.ops:
            for i in range(len(op.arrays)):
                if op.aliased:
                    out[first_in + k + i] = first_out + k + i
            k += len(op.arrays)
        return out

    def scratch(self):
        return [pltpu.SemaphoreType.DMA((self.n_sems,)), pltpu.SemaphoreType.DMA((self.n_sems,))]

    def run(self, phase, ins, outs, send_sems, recv_sems):
        k = base = 0
        for op in self.ops:
            n = len(op.arrays)
            op.run(phase, ins[k:k + n], outs[k:k + n], send_sems, recv_sems, base)
            k += n
            base += op.n_sems

    def split(self, results):
        out, k = [], 0
        for op in self.ops:
            out.append(list(results[k:k + len(op.arrays)]))
            k += len(op.arrays)
        return out


def _communicate(ops):
    comm = _Comm(ops)
    n = len(comm.arrays)

    def body(*refs):
        ins, outs, (send_sems, recv_sems) = refs[:n], refs[n:2 * n], refs[2 * n:]
        for phase in range(3):
            comm.run(phase, ins, outs, send_sems, recv_sems)

    res = pl.pallas_call(
        body, name="communicate", out_shape=tuple(comm.outs), in_specs=comm.specs(), out_specs=tuple(comm.specs()),
        input_output_aliases=comm.aliases(0, 0), scratch_shapes=comm.scratch(),
    )(*comm.arrays)
    return comm.split(res)


def _pool_core(he, w_ref, scale, first_row, halo, n_rows):
    d = he.shape[1]
    gd = d // len(POOL_WINDOWS)
    t = first_row + lax.broadcasted_iota(jnp.int32, (n_rows, 1), 0)
    pooled, ypre, cnts = [], [], []
    for g, w in enumerate(POOL_WINDOWS):
        hg = he[:, g * gd:(g + 1) * gd]
        s, k = hg, 1
        while k < w:
            s = s + _shift_down(s, k)
            k *= 2
        cnt = jnp.minimum(t + 1, w).astype(F32)
        p = s[halo:] / cnt - hg[halo:]
        pooled.append(p.astype(BF16))
        cnts.append(cnt)
        ypre.append(_dot(pooled[-1], w_ref[g]))
    return pooled, jnp.concatenate(ypre, axis=1), cnts


def _pool_forward(x, vec, pool_w, comm=None):
    s, d = x.shape
    ts = _row_tile(s, 512)
    nb = s // ts
    n_g, gd, _ = pool_w.shape
    comm = comm or _Comm([])
    nc = len(comm.arrays)

    def body(*refs):
        x_ref, vec_ref, w_ref = refs[:3]
        cin = refs[3:3 + nc]
        o_ref = refs[3 + nc]
        cout = refs[4 + nc:4 + 2 * nc]
        carry = refs[4 + 2 * nc]
        sems = refs[5 + 2 * nc:]
        i = pl.program_id(0)

        @pl.when(i == 0)
        def _():
            carry[...] = jnp.zeros_like(carry)
            if nc:
                comm.run(0, cin, cout, *sems)

        if nc:
            @pl.when(i == nb - 1)
            def _():
                comm.run(1, cin, cout, *sems)

        vec = vec_ref[...]
        a, sh, gt, gpost = _vec_rows(vec)
        xb = x_ref[...]
        xn, _ = _rms(xb)
        h = xn * a + sh
        he = jnp.concatenate([carry[...], h], axis=0)
        carry[...] = h[ts - POOL_HALO:]
        _, ypre, _ = _pool_core(he, w_ref, vec[5:6], i * ts, POOL_HALO, ts)
        yn, _ = _rms(ypre * vec[5:6])
        o_ref[...] = xb + gt * (yn * gpost)
        if nc:
            @pl.when(i == nb - 1)
            def _():
                comm.run(2, cin, cout, *sems)

    res = pl.pallas_call(
        body, name="pool_forward", grid=(nb,),
        in_specs=[pl.BlockSpec((ts, d), lambda i: (i, 0)), pl.BlockSpec((8, d), lambda i: (0, 0)),
                  pl.BlockSpec((n_g, gd, gd), lambda i: (0, 0, 0))] + comm.specs(),
        out_specs=(pl.BlockSpec((ts, d), lambda i: (i, 0)), *comm.specs()),
        out_shape=(jax.ShapeDtypeStruct((s, d), F32), *comm.outs),
        input_output_aliases=comm.aliases(3, 1),
        scratch_shapes=[pltpu.VMEM((POOL_HALO, d), F32)] + (comm.scratch() if nc else []),
        compiler_params=_cparams(("arbitrary",)),
    )(x, vec, pool_w, *comm.arrays)
    return res[0], comm.split(res[1:])


def _pool_backward(dout, x, vec, pool_w):
    s, d = x.shape
    ts = _row_tile(s, 512)
    nb = s // ts
    hb = ts // POOL_HALO
    n_g, gd, _ = pool_w.shape

    def body(do_ref, x_ref, xh_ref, vec_ref, w_ref, dx_ref, sum_ref, dw_ref, carry):
        step = pl.program_id(0)
        i = nb - 1 - step

        @pl.when(step == 0)
        def _():
            carry[...] = jnp.zeros_like(carry)
            sum_ref[...] = jnp.zeros_like(sum_ref)
            dw_ref[...] = jnp.zeros_like(dw_ref)

        vec = vec_ref[...]
        a, sh, gt, gpost = _vec_rows(vec)
        scale = vec[5:6]
        do = do_ref[...]
        xe = jnp.concatenate([xh_ref[...], x_ref[...]], axis=0)
        xne, re = _rms(xe)
        he = xne * a + sh
        rowid = lax.broadcasted_iota(jnp.int32, (POOL_HALO + ts, 1), 0)
        he = jnp.where((rowid >= POOL_HALO) | (i > 0), he, 0.0)
        xn, r = xne[POOL_HALO:], re[POOL_HALO:]
        pooled, ypre, cnts = _pool_core(he, w_ref, scale, i * ts, POOL_HALO, ts)
        yn, ry = _rms(ypre * scale)
        dyn = do * (gt * gpost)
        dy = _rms_bwd(dyn, yn, ry)
        dypre = (dy * scale).astype(BF16)
        dh_parts, q_parts = [], []
        for g, w in enumerate(POOL_WINDOWS):
            dyg = dypre[:, g * gd:(g + 1) * gd]
            dpool = _dot_nt(dyg, w_ref[g])
            dw_ref[g] += _dot_tn(pooled[g], dyg)
            q = dpool / cnts[g]
            qe = jnp.concatenate([q, carry[:, g * gd:(g + 1) * gd]], axis=0)
            acc, k = qe, 1
            while k < w:
                acc = acc + _shift_up(acc, k)
                k *= 2
            dh_parts.append(acc[:ts] - dpool)
            q_parts.append(q[:POOL_HALO])
        carry[...] = jnp.concatenate(q_parts, axis=1)
        dh = jnp.concatenate(dh_parts, axis=1)
        dxn = dh * a
        dx_ref[...] = do + _rms_bwd(dxn, xn, r)
        _add_rows(sum_ref, _norm_sums(do, yn, dh, xn, vec) + [_colsum(dh), _colsum(dy * ypre)])

    blk = lambda st: (nb - 1 - st, 0)
    return pl.pallas_call(
        body, name="pool_backward", grid=(nb,),
        in_specs=[pl.BlockSpec((ts, d), blk), pl.BlockSpec((ts, d), blk),
                  pl.BlockSpec((POOL_HALO, d), lambda st: (jnp.maximum((nb - 1 - st) * hb - 1, 0), 0)),
                  pl.BlockSpec((8, d), lambda st: (0, 0)), pl.BlockSpec((n_g, gd, gd), lambda st: (0, 0, 0))],
        out_specs=(pl.BlockSpec((ts, d), blk), pl.BlockSpec((8, d), lambda st: (0, 0)),
                   pl.BlockSpec((n_g, gd, gd), lambda st: (0, 0, 0))),
        out_shape=(jax.ShapeDtypeStruct((s, d), F32), jax.ShapeDtypeStruct((8, d), F32),
                   jax.ShapeDtypeStruct((n_g, gd, gd), F32)),
        scratch_shapes=[pltpu.VMEM((POOL_HALO, d), F32)],
        compiler_params=_cparams(("arbitrary",)),
    )(dout, x, x, vec, pool_w)


def _ffn_forward(x, vec, w_up, w_dw, w_down, comm=None, target=None):
    s, d = x.shape
    _, _, cs = w_up.shape
    ts = _row_tile(s, 256)
    nb = s // ts
    chunks = _lane_chunks(cs)
    comm = comm or _Comm([])
    nc = len(comm.arrays)
    nl = 0 if target is None else 1
    n_in, n_out = 5 + nl, 6 + nl

    def body(*refs):
        x_ref, vec_ref, wup_ref, wdw_ref, wdn_ref = refs[:5]
        cin = refs[n_in:n_in + nc]
        o_ref, h_ref, a0_ref, cc_ref, u_ref, y_ref = refs[n_in + nc:n_in + nc + 6]
        loss_ref = refs[n_in + nc + 6] if nl else None
        cout = refs[n_in + nc + n_out:n_in + 2 * nc + n_out]
        carry = refs[n_in + 2 * nc + n_out]
        sems = refs[n_in + 2 * nc + n_out + 1:]
        i = pl.program_id(0)

        @pl.when(i == 0)
        def _():
            carry[...] = jnp.zeros_like(carry)
            if nl:
                loss_ref[...] = jnp.zeros_like(loss_ref)
            if nc:
                comm.run(0, cin, cout, *sems)

        if nc:
            @pl.when(i == (3 * nb) // 4)
            def _():
                comm.run(1, cin, cout, *sems)

        vec = vec_ref[...]
        a, sh, gt, gpost = _vec_rows(vec)
        xb = x_ref[...]
        xn, _ = _rms(xb)
        hb = (xn * a + sh).astype(BF16)
        h_ref[...] = hb
        for q in range(2):
            for c0, cw in chunks:
                conv = []
                for j in (q, q + 2):
                    a0 = _dot(hb, wup_ref[j, :, c0:c0 + cw])
                    a0_ref[j, :, c0:c0 + cw] = a0.astype(BF16)
                    ae = jnp.concatenate([carry[j, :, c0:c0 + cw], a0], axis=0)
                    carry[j, :, c0:c0 + cw] = a0[ts - FFN_HALO:]
                    w = wdw_ref[:, j * cs + c0:j * cs + c0 + cw]
                    conv.append((w[2:3] * ae + w[1:2] * _shift_down(ae, 1) + w[0:1] * _shift_down(ae, 2))[FFN_HALO:])
                    cc_ref[j, :, c0:c0 + cw] = conv[-1].astype(BF16)
                u_ref[q, :, c0:c0 + cw] = (conv[0] * _sigmoid(conv[0]) * conv[1]).astype(BF16)
        y = _dot(u_ref[0], wdn_ref[0]) + _dot(u_ref[1], wdn_ref[1])
        y_ref[...] = y
        yn, _ = _rms(y)
        x_out = xb + gt * (yn * gpost)
        if nl:
            err = x_out - refs[5][...]
            o_ref[...] = err * (1.0 / d)
            loss_ref[0:1, :] += _colsum(err * err) * (0.5 / d)
        else:
            o_ref[...] = x_out
        if nc:
            @pl.when(i == nb - 1)
            def _():
                comm.run(2, cin, cout, *sems)

    const3 = lambda i: (0, 0, 0)
    res = pl.pallas_call(
        body, name="ffn_forward", grid=(nb,),
        in_specs=[pl.BlockSpec((ts, d), lambda i: (i, 0)), pl.BlockSpec((8, d), lambda i: (0, 0)),
                  pl.BlockSpec(w_up.shape, const3, pipeline_mode=pl.Buffered(1)),
                  pl.BlockSpec(w_dw.shape, lambda i: (0, 0)),
                  pl.BlockSpec(w_down.shape, const3, pipeline_mode=pl.Buffered(1))]
        + [pl.BlockSpec((ts, d), lambda i: (i, 0))] * nl + comm.specs(),
        out_specs=(pl.BlockSpec((ts, d), lambda i: (i, 0)), pl.BlockSpec((ts, d), lambda i: (i, 0)),
                   pl.BlockSpec((4, ts, cs), lambda i: (0, i, 0)), pl.BlockSpec((4, ts, cs), lambda i: (0, i, 0)),
                   pl.BlockSpec((2, ts, cs), lambda i: (0, i, 0)), pl.BlockSpec((ts, d), lambda i: (i, 0)),
                   *[pl.BlockSpec((8, d), lambda i: (0, 0))] * nl, *comm.specs()),
        out_shape=(jax.ShapeDtypeStruct((s, d), F32), jax.ShapeDtypeStruct((s, d), BF16),
                   jax.ShapeDtypeStruct((4, s, cs), BF16), jax.ShapeDtypeStruct((4, s, cs), BF16),
                   jax.ShapeDtypeStruct((2, s, cs), BF16), jax.ShapeDtypeStruct((s, d), F32),
                   *[jax.ShapeDtypeStruct((8, d), F32)] * nl, *comm.outs),
        input_output_aliases=comm.aliases(n_in, n_out),
        scratch_shapes=[pltpu.VMEM((4, FFN_HALO, cs), F32)] + (comm.scratch() if nc else []),
        compiler_params=_cparams(("arbitrary",)),
    )(x, vec, w_up, w_dw, w_down, *([target] * nl), *comm.arrays)
    return res[:n_out], comm.split(res[n_out:])


def _ffn_backward(dout, x, y, a0, cc, vec, w_up, w_dw, w_down):
    s, d = x.shape
    _, _, cs = w_up.shape
    ts = _row_tile(s, 256)
    nb = s // ts
    chunks = _lane_chunks(cs)

    def body(do_ref, x_ref, y_ref, a0_ref, cc_ref, vec_ref, wup_ref, wdw_ref, wdn_ref,
             dx_ref, da0_ref, dy_ref, sum_ref, dwdw_ref, carry, du_s, dh_s):
        step = pl.program_id(0)

        @pl.when(step == 0)
        def _():
            carry[...] = jnp.zeros_like(carry)
            sum_ref[...] = jnp.zeros_like(sum_ref)
            dwdw_ref[...] = jnp.zeros_like(dwdw_ref)

        vec = vec_ref[...]
        a, sh, gt, gpost = _vec_rows(vec)
        do = do_ref[...]
        yn, ry = _rms(y_ref[...])
        dy = _rms_bwd(do * (gt * gpost), yn, ry)
        dyb = dy.astype(BF16)
        dy_ref[...] = dyb
        order = [(q, c0, cw) for q in range(2) for c0, cw in chunks]

        def du_pieces(idx):
            q, c0, cw = order[idx]
            return [functools.partial(_store_dot_nt, du_s.at[idx % 2, :, n0:min(n0 + MXU_LANES, cw)], dy_ref,
                                      wdn_ref.at[q, c0 + n0:c0 + min(n0 + MXU_LANES, cw), :])
                    for n0 in range(0, cw, MXU_LANES)]

        def dh_pieces():
            return [functools.partial(_store_dot_nt2, dh_s.at[:, n0:n0 + MXU_LANES], da0_ref.at[0], da0_ref.at[2],
                                      wup_ref.at[0, n0:n0 + MXU_LANES, :], wup_ref.at[2, n0:n0 + MXU_LANES, :])
                    for n0 in range(0, d, MXU_LANES)]

        for piece in du_pieces(0):
            piece()
        later = dh_pieces()
        for idx, (q, c0, cw) in enumerate(order):
            work = du_pieces(idx + 1) if idx + 1 < len(order) else []
            if q == 1:
                share = -(-len(later) // (len(order) - idx))
                work, later = work + later[:share], later[share:]

            def pump(part, of=3):
                for piece in work[part::of]:
                    piece()

            cg = cc_ref[q, :, c0:c0 + cw].astype(F32)
            cv = cc_ref[q + 2, :, c0:c0 + cw].astype(F32)
            sg = _sigmoid(cg)
            sl = cg * sg
            du = du_s[idx % 2, :, :cw]
            dconv = {q: du * cv * (sg * (1.0 + cg * (1.0 - sg))), q + 2: du * sl}
            pump(0)
            for part, j in enumerate((q, q + 2)):
                dae = jnp.concatenate([dconv[j], carry[j, :, c0:c0 + cw]], axis=0)
                carry[j, :, c0:c0 + cw] = dconv[j][:FFN_HALO]
                up1 = _shift_down(dae, FFN_HALO - 1)[FFN_HALO:]
                up2 = _shift_down(dae, FFN_HALO - 2)[FFN_HALO:]
                lanes = slice(j * cs + c0, j * cs + c0 + cw)
                w = wdw_ref[:, lanes]
                da0_ref[j, :, c0:c0 + cw] = (w[2:3] * dconv[j] + w[1:2] * up1 + w[0:1] * up2).astype(BF16)
                a0 = a0_ref[j, :, c0:c0 + cw].astype(F32)
                dwdw_ref[0:1, lanes] += _colsum(up2 * a0)
                dwdw_ref[1:2, lanes] += _colsum(up1 * a0)
                dwdw_ref[2:3, lanes] += _colsum(dconv[j] * a0)
                pump(part + 1)
        dh = dh_s[...] + _dot_nt(da0_ref[1], wup_ref[1]) + _dot_nt(da0_ref[3], wup_ref[3])
        xn, r = _rms(x_ref[...])
        dx_ref[...] = do + _rms_bwd(dh * a, xn, r)
        _add_rows(sum_ref, _norm_sums(do, yn, dh, xn, vec) + [_colsum(dh)])

    blk = lambda st: (nb - 1 - st, 0)
    blk3 = lambda st: (0, nb - 1 - st, 0)
    const3 = lambda st: (0, 0, 0)
    return pl.pallas_call(
        body, name="ffn_backward", grid=(nb,),
        in_specs=[pl.BlockSpec((ts, d), blk), pl.BlockSpec((ts, d), blk), pl.BlockSpec((ts, d), blk),
                  pl.BlockSpec((4, ts, cs), blk3), pl.BlockSpec((4, ts, cs), blk3),
                  pl.BlockSpec((8, d), lambda st: (0, 0)),
                  pl.BlockSpec(w_up.shape, const3, pipeline_mode=pl.Buffered(1)),
                  pl.BlockSpec(w_dw.shape, lambda st: (0, 0)),
                  pl.BlockSpec(w_down.shape, const3, pipeline_mode=pl.Buffered(1))],
        out_specs=(pl.BlockSpec((ts, d), blk), pl.BlockSpec((4, ts, cs), blk3),
                   pl.BlockSpec((ts, d), blk), pl.BlockSpec((8, d), lambda st: (0, 0)),
                   pl.BlockSpec((8, 4 * cs), lambda st: (0, 0))),
        out_shape=(jax.ShapeDtypeStruct((s, d), F32), jax.ShapeDtypeStruct((4, s, cs), BF16),
                   jax.ShapeDtypeStruct((s, d), BF16),
                   jax.ShapeDtypeStruct((8, d), F32), jax.ShapeDtypeStruct((8, 4 * cs), F32)),
        scratch_shapes=[pltpu.VMEM((4, FFN_HALO, cs), F32), pltpu.VMEM((2, ts, max(cw for _, cw in chunks)), F32),
                        pltpu.VMEM((ts, d), F32)],
        compiler_params=_cparams(("arbitrary",)),
    )(dout, x, y, a0, cc, vec, w_up, w_dw, w_down)


def _conv_halo(width):
    return -(-(width - 1) // 8) * 8


def _conv_forward(x, vec, cvec, w_pw1, b_pw1, w_dw, w_pw2):
    s, d = x.shape
    kw = w_dw.shape[0]
    halo = _conv_halo(kw)
    ts = _row_tile(s, 256)
    hd = d // 2

    def body(x_ref, vec_ref, cvec_ref, w1_ref, b1_ref, wdw_ref, w2_ref,
             o_ref, h_ref, a_ref, uc_ref, z_ref, y_ref, carry):
        i = pl.program_id(0)

        @pl.when(i == 0)
        def _():
            carry[...] = jnp.zeros_like(carry)

        vec, cvec = vec_ref[...], cvec_ref[...]
        a, sh, gt, gpost = _vec_rows(vec)
        xb = x_ref[...]
        xn, _ = _rms(xb)
        hb = (xn * a + sh).astype(BF16)
        h_ref[...] = hb
        for j in range(4):
            a_ref[:, j * hd:(j + 1) * hd] = _dot(hb, w1_ref[j]) + b1_ref[:, j * hd:(j + 1) * hd]
        u = a_ref[:, :d] * _sigmoid(a_ref[:, d:])
        carry[halo:, :] = u
        for r0 in range(0, ts, CONV_ROWS):
            for l0 in range(0, d, CONV_LANES):
                lanes = slice(l0, l0 + CONV_LANES)
                src = carry[r0:r0 + CONV_ROWS + halo, lanes]
                acc = jnp.zeros((CONV_ROWS, CONV_LANES), F32) + cvec[0:1, lanes]
                for k in range(kw):
                    acc = acc + wdw_ref[k:k + 1, lanes] * _shift_down(src, kw - 1 - k)[halo:]
                uc_ref[r0:r0 + CONV_ROWS, lanes] = acc
        carry[:halo, :] = u[ts - halo:]
        uc = uc_ref[...]
        mu = jnp.mean(uc, axis=-1, keepdims=True)
        cen = uc - mu
        rstd = lax.rsqrt(jnp.mean(cen * cen, axis=-1, keepdims=True) + EPS)
        l = cen * rstd * cvec[1:2] + cvec[2:3]
        zb = (l * _sigmoid(l)).astype(BF16)
        z_ref[...] = zb
        y = _dot(zb, w2_ref[...]) + cvec[3:4]
        y_ref[...] = y
        yn, _ = _rms(y)
        o_ref[...] = xb + gt * (yn * gpost)

    row = lambda i: (i, 0)
    const2 = lambda i: (0, 0)
    return pl.pallas_call(
        body, name="conv_forward", grid=(s // ts,),
        in_specs=[pl.BlockSpec((ts, d), row), pl.BlockSpec((8, d), const2), pl.BlockSpec((8, d), const2),
                  pl.BlockSpec(w_pw1.shape, lambda i: (0, 0, 0)), pl.BlockSpec(b_pw1.shape, const2),
                  pl.BlockSpec(w_dw.shape, const2), pl.BlockSpec(w_pw2.shape, const2)],
        out_specs=(pl.BlockSpec((ts, d), row), pl.BlockSpec((ts, d), row), pl.BlockSpec((ts, 2 * d), row),
                   pl.BlockSpec((ts, d), row), pl.BlockSpec((ts, d), row), pl.BlockSpec((ts, d), row)),
        out_shape=(jax.ShapeDtypeStruct((s, d), F32), jax.ShapeDtypeStruct((s, d), BF16),
                   jax.ShapeDtypeStruct((s, 2 * d), F32), jax.ShapeDtypeStruct((s, d), F32),
                   jax.ShapeDtypeStruct((s, d), BF16), jax.ShapeDtypeStruct((s, d), F32)),
        scratch_shapes=[pltpu.VMEM((halo + ts, d), F32)],
        compiler_params=_cparams(("arbitrary",)),
    )(x, vec, cvec, w_pw1, b_pw1, w_dw, w_pw2)


def _conv_backward(dout, x, y, a_pre, uc, vec, cvec, w_pw1, w_dw, w_pw2):
    s, d = x.shape
    kw = w_dw.shape[0]
    kpad = -(-kw // 8) * 8
    halo = _conv_halo(kw)
    ts = _row_tile(s, 256)
    nb = s // ts
    hb = ts // halo
    hd = d // 2

    def body(do_ref, x_ref, y_ref, a_ref, ah_ref, uc_ref, vec_ref, cvec_ref, w1_ref, wdw_ref, w2_ref,
             dx_ref, da_ref, dy_ref, sum_ref, dwdw_ref, carry):
        step = pl.program_id(0)
        i = nb - 1 - step

        @pl.when(step == 0)
        def _():
            carry[...] = jnp.zeros_like(carry)
            sum_ref[...] = jnp.zeros_like(sum_ref)
            dwdw_ref[...] = jnp.zeros_like(dwdw_ref)

        vec, cvec = vec_ref[...], cvec_ref[...]
        a, sh, gt, gpost = _vec_rows(vec)
        do = do_ref[...]
        yn, ry = _rms(y_ref[...])
        dy = _rms_bwd(do * (gt * gpost), yn, ry)
        dyb = dy.astype(BF16)
        dy_ref[...] = dyb
        dz = _dot_nt(dyb, w2_ref[...])
        uc = uc_ref[...]
        mu = jnp.mean(uc, axis=-1, keepdims=True)
        cen = uc - mu
        rstd = lax.rsqrt(jnp.mean(cen * cen, axis=-1, keepdims=True) + EPS)
        lhat = cen * rstd
        l = lhat * cvec[1:2] + cvec[2:3]
        sgl = _sigmoid(l)
        dl = dz * (sgl * (1.0 + l * (1.0 - sgl)))
        dlhat = dl * cvec[1:2]
        duc = rstd * (dlhat - jnp.mean(dlhat, axis=-1, keepdims=True)
                      - lhat * jnp.mean(dlhat * lhat, axis=-1, keepdims=True))
        ae = jnp.concatenate([ah_ref[...] * (i > 0).astype(F32), a_ref[...]], axis=0)
        sgate = _sigmoid(ae[:, d:])
        val = ae[:, :d]
        ue = val * sgate
        rowid = lax.broadcasted_iota(jnp.int32, (halo + ts, 1), 0)
        ue = jnp.where((rowid >= halo) | (i > 0), ue, 0.0)
        duce = jnp.concatenate([duc, carry[...]], axis=0)
        carry[...] = duc[:halo]
        du = jnp.zeros((ts, d), F32)
        for k in range(kw):
            du = du + wdw_ref[k:k + 1, :] * _shift_down(duce, halo - (kw - 1 - k))[halo:]
            dwdw_ref[k:k + 1, :] += _colsum(duc * _shift_down(ue, kw - 1 - k)[halo:])
        sg, vl = sgate[halo:], val[halo:]
        dval = du * sg
        dgate = du * vl * (sg * (1.0 - sg))
        dvb, dgb = dval.astype(BF16), dgate.astype(BF16)
        dh = jnp.zeros((ts, d), F32)
        for j in range(2):
            da_ref[j] = dvb[:, j * hd:(j + 1) * hd]
            da_ref[j + 2] = dgb[:, j * hd:(j + 1) * hd]
            dh = dh + _dot_nt(dvb[:, j * hd:(j + 1) * hd], w1_ref[j]) + _dot_nt(dgb[:, j * hd:(j + 1) * hd], w1_ref[j + 2])
        xn, r = _rms(x_ref[...])
        dx_ref[...] = do + _rms_bwd(dh * a, xn, r)
        _add_rows(sum_ref, _norm_sums(do, yn, dh, xn, vec) + [_colsum(dh), _colsum(dy), _colsum(dl * lhat), _colsum(dl),
                            _colsum(duc), _colsum(dval), _colsum(dgate)])

    blk = lambda st: (nb - 1 - st, 0)
    const2 = lambda st: (0, 0)
    return pl.pallas_call(
        body, name="conv_backward", grid=(nb,),
        in_specs=[pl.BlockSpec((ts, d), blk), pl.BlockSpec((ts, d), blk), pl.BlockSpec((ts, d), blk),
                  pl.BlockSpec((ts, 2 * d), blk),
                  pl.BlockSpec((halo, 2 * d), lambda st: (jnp.maximum((nb - 1 - st) * hb - 1, 0), 0)),
                  pl.BlockSpec((ts, d), blk), pl.BlockSpec((8, d), const2), pl.BlockSpec((8, d), const2),
                  pl.BlockSpec(w_pw1.shape, lambda st: (0, 0, 0)), pl.BlockSpec(w_dw.shape, const2),
                  pl.BlockSpec(w_pw2.shape, const2)],
        out_specs=(pl.BlockSpec((ts, d), blk), pl.BlockSpec((4, ts, hd), lambda st: (0, nb - 1 - st, 0)),
                   pl.BlockSpec((ts, d), blk), pl.BlockSpec((16, d), const2), pl.BlockSpec((kpad, d), const2)),
        out_shape=(jax.ShapeDtypeStruct((s, d), F32), jax.ShapeDtypeStruct((4, s, hd), BF16),
                   jax.ShapeDtypeStruct((s, d), BF16), jax.ShapeDtypeStruct((16, d), F32),
                   jax.ShapeDtypeStruct((kpad, d), F32)),
        scratch_shapes=[pltpu.VMEM((halo, d), F32)],
        compiler_params=_cparams(("arbitrary",)),
    )(dout, x, y, a_pre, a_pre, uc, vec, cvec, w_pw1, w_dw, w_pw2)


def _weight_grad(a, b, comm=None):
    na, s, k = a.shape
    nb_, _, n = b.shape
    nj = max(na, nb_)
    ts = _row_tile(s, 2048)
    nt = s // ts
    comm = comm or _Comm([])
    nc = len(comm.arrays)

    def body(*refs):
        a_ref, b_ref = refs[:2]
        cin = refs[2:2 + nc]
        o_ref = refs[2 + nc]
        cout = refs[3 + nc:3 + 2 * nc]
        sems = refs[3 + 2 * nc:]
        j, t = pl.program_id(0), pl.program_id(1)

        if nc:
            @pl.when((j == 0) & (t == 0))
            def _():
                comm.run(0, cin, cout, *sems)

            @pl.when((j == nj // 2) & (t == nt // 2))
            def _():
                comm.run(1, cin, cout, *sems)

        @pl.when(t == 0)
        def _():
            o_ref[...] = jnp.zeros_like(o_ref)

        o_ref[0] += _dot_tn(a_ref[0], b_ref[0])

        if nc:
            @pl.when((j == nj - 1) & (t == nt - 1))
            def _():
                comm.run(2, cin, cout, *sems)

    res = pl.pallas_call(
        body, name="weight_grad", grid=(nj, nt),
        in_specs=[pl.BlockSpec((1, ts, k), (lambda j, t: (j, t, 0)) if na > 1 else (lambda j, t: (0, t, 0))),
                  pl.BlockSpec((1, ts, n), (lambda j, t: (j, t, 0)) if nb_ > 1 else (lambda j, t: (0, t, 0)))]
        + comm.specs(),
        out_specs=(pl.BlockSpec((1, k, n), lambda j, t: (j, 0, 0)), *comm.specs()),
        out_shape=(jax.ShapeDtypeStruct((nj, k, n), F32), *comm.outs),
        input_output_aliases=comm.aliases(2, 1),
        scratch_shapes=comm.scratch() if nc else [],
        compiler_params=_cparams(("arbitrary", "arbitrary") if nc else ("parallel", "arbitrary")),
    )(a, b, *comm.arrays)
    return res[0], comm.split(res[1:])


def _adamw_math(w, g, m, v):
    nm = ADAM_B1 * m + (1.0 - ADAM_B1) * g
    nv = ADAM_B2 * v + (1.0 - ADAM_B2) * (g * g)
    m_hat = nm * (1.0 / (1.0 - ADAM_B1 ** ADAM_STEP))
    v_hat = nv * (1.0 / (1.0 - ADAM_B2 ** ADAM_STEP))
    return -ADAM_LR * (m_hat / (jnp.sqrt(v_hat) + ADAM_EPS) + ADAM_WD * w), nm, nv


def _adamw_many(params):
    n = len(params)

    def body(*refs):
        for k in range(n):
            w_ref, g_ref, m_ref, v_ref = refs[4 * k:4 * k + 4]
            outs = refs[4 * n + 3 * k:4 * n + 3 * k + 3]
            for o_ref, val in zip(outs, _adamw_math(w_ref[...], g_ref[...], m_ref[...], v_ref[...])):
                o_ref[...] = val

    vm = pl.BlockSpec(memory_space=pltpu.VMEM)
    res = pl.pallas_call(
        body, name="adamw_many", in_specs=[vm] * (4 * n), out_specs=tuple([vm] * (3 * n)),
        out_shape=tuple(jax.ShapeDtypeStruct(p[0].shape, F32) for p in params for _ in range(3)),
        compiler_params=_cparams(),
    )(*[a for p in params for a in p])
    return [res[3 * k:3 * k + 3] for k in range(n)]


def _adamw(w, g, m, v, comm=None):
    nl, r, c = w.shape
    tr = _row_tile(r, 256)
    nr = r // tr
    comm = comm or _Comm([])
    nc = len(comm.arrays)

    def body(*refs):
        w_ref, g_ref, m_ref, v_ref = refs[:4]
        cin = refs[4:4 + nc]
        d_ref, nm_ref, nv_ref = refs[4 + nc:7 + nc]
        cout = refs[7 + nc:7 + 2 * nc]
        sems = refs[7 + 2 * nc:]
        l, i = pl.program_id(0), pl.program_id(1)
        if nc:
            @pl.when((l == 0) & (i == 0))
            def _():
                comm.run(0, cin, cout, *sems)

            @pl.when((l == nl // 2) & (i == nr // 2))
            def _():
                comm.run(1, cin, cout, *sems)

        d_ref[...], nm_ref[...], nv_ref[...] = _adamw_math(w_ref[...], g_ref[...], m_ref[...], v_ref[...])
        if nc:
            @pl.when((l == nl - 1) & (i == nr - 1))
            def _():
                comm.run(2, cin, cout, *sems)

    spec = pl.BlockSpec((1, tr, c), lambda l, i: (l, i, 0))
    shp = jax.ShapeDtypeStruct((nl, r, c), F32)
    res = pl.pallas_call(
        body, name="adamw", grid=(nl, nr), in_specs=[spec] * 4 + comm.specs(),
        out_specs=(spec,) * 3 + tuple(comm.specs()), out_shape=(shp,) * 3 + tuple(comm.outs),
        input_output_aliases=comm.aliases(4, 3), scratch_shapes=comm.scratch() if nc else [],
        compiler_params=_cparams(("arbitrary", "arbitrary") if nc else ("parallel", "parallel")),
    )(w, g, m, v, *comm.arrays)
    return res[:3], comm.split(res[3:])


def _add_my_half(g, other, idx):
    _, _, h, c = g.shape
    th = _row_tile(h, 256)

    def body(idx_ref, g_ref, o_ref, out_ref):
        out_ref[...] = (g_ref[:, 0] + o_ref[...]).astype(BF16)

    return pl.pallas_call(
        body, name="add_my_half",
        grid_spec=pltpu.PrefetchScalarGridSpec(
            num_scalar_prefetch=1, grid=(4, h // th),
            in_specs=[pl.BlockSpec((1, 1, th, c), lambda j, i, idx_ref: (j, idx_ref[1], i, 0)),
                      pl.BlockSpec((1, th, c), lambda j, i, idx_ref: (j, i, 0))],
            out_specs=pl.BlockSpec((1, th, c), lambda j, i, idx_ref: (j, i, 0))),
        out_shape=jax.ShapeDtypeStruct(other.shape, BF16),
        compiler_params=_cparams(("parallel", "parallel")),
    )(idx, g, other)


def _sum_for_my_chip(g, other, got, idx):
    _, _, h, c = g.shape
    th = _row_tile(h, 256)

    def body(idx_ref, g_ref, o_ref, q_ref, out_ref):
        out_ref[0] = (((g_ref[0, 0] + o_ref[0]) + q_ref[0].astype(F32)) + q_ref[1].astype(F32)) + q_ref[2].astype(F32)

    return pl.pallas_call(
        body, name="sum_for_my_chip",
        grid_spec=pltpu.PrefetchScalarGridSpec(
            num_scalar_prefetch=1, grid=(h // th,),
            in_specs=[pl.BlockSpec((1, 1, th, c), lambda i, idx_ref: (idx_ref[0], idx_ref[1], i, 0)),
                      pl.BlockSpec((1, th, c), lambda i, idx_ref: (idx_ref[0], i, 0)),
                      pl.BlockSpec((3, th, c), lambda i, idx_ref: (0, i, 0))],
            out_specs=pl.BlockSpec((1, th, c), lambda i, idx_ref: (idx_ref[1], i, 0))),
        out_shape=jax.ShapeDtypeStruct((2, h, c), F32),
        compiler_params=_cparams(("parallel",)),
    )(idx, g, other, got)


class _Reducer:
    def __init__(self, idx):
        self.idx = idx
        self.groups = []

    def add(self, grads):
        group = {"state": 0, "g": [g.reshape(4, 2, g.shape[1] // 2, g.shape[2]) for g in grads]}
        self.groups.append(group)
        return group

    def steps(self):
        ops, owners = [], []
        for gr in self.groups:
            if gr["state"] == 0:
                ops.append(_Swap(gr["g"]))
            elif gr["state"] == 1:
                ops.append(_Exchange(gr["parts"]))
            elif gr["state"] == 2:
                ops.append(_Join(gr["bufs"]))
            else:
                continue
            owners.append(gr)
        return ops, owners

    def absorb(self, owners, results):
        for gr, res in zip(owners, results):
            if gr["state"] == 0:
                gr["other"] = res
                gr["parts"] = [_add_my_half(g, o, self.idx) for g, o in zip(gr["g"], res)]
            elif gr["state"] == 1:
                gr["bufs"] = [_sum_for_my_chip(g, o, q, self.idx) for g, o, q in zip(gr["g"], gr["other"], res)]
            else:
                gr["full"] = [b.reshape(2 * b.shape[1], b.shape[2]) for b in res]
            gr["state"] += 1

    def drain(self):
        while any(gr["state"] < 3 for gr in self.groups):
            ops, owners = self.steps()
            self.absorb(owners, _communicate(ops))


class _GatherRows:
    def __init__(self, bufs):
        self.arrays = list(bufs)
        self.outs = [jax.ShapeDtypeStruct(b.shape, b.dtype) for b in bufs]
        self.aliased = True
        self.n_sems = 7 * len(bufs)

    def run(self, phase, ins, outs, send_sems, recv_sems, base):
        x, y, c, chips = _place()
        me, sibling = (x, y, c), (x, y, 1 - c)
        for k, buf in enumerate(outs):
            def copy(i, block_of, to):
                blk = buf.at[4 * block_of[0] + 2 * block_of[1] + block_of[2]]
                return _remote(blk, blk, send_sems.at[base + 7 * k + i], recv_sems.at[base + 7 * k + i], to)

            if phase == 0:
                copy(0, me, sibling).start()
            for r, (px, py) in enumerate(chips):
                if phase == 0:
                    copy(1 + r, me, (px, py, c)).start()
                elif phase == 1:
                    copy(1 + r, (px, py, c), me).wait_recv()
                    copy(4 + r, (px, py, c), sibling).start()
                else:
                    copy(4 + r, (px, py, 1 - c), me).wait_recv()
                    copy(1 + r, me, (px, py, c)).wait_send()
                    copy(4 + r, (px, py, c), sibling).wait_send()
            if phase == 2:
                copy(0, sibling, me).wait_recv()
                copy(0, me, sibling).wait_send()


def _sum_devices(gathered):
    nd, m, n = gathered.shape

    def body(g_ref, o_ref):
        acc = g_ref[0]
        for b in range(1, nd):
            acc = acc + g_ref[b]
        o_ref[...] = acc

    return pl.pallas_call(
        body, name="sum_devices", out_shape=jax.ShapeDtypeStruct((m, n), F32),
        in_specs=[pl.BlockSpec(memory_space=pltpu.VMEM)], out_specs=pl.BlockSpec(memory_space=pltpu.VMEM),
        compiler_params=_cparams(),
    )(gathered)


def _ada_weight_grad(c_all, dmod_cols):
    nl, nd, ncol = dmod_cols.shape
    d = c_all.shape[1]

    def body(c_ref, dm_ref, o_ref):
        o_ref[0] = lax.dot_general(c_ref[...], dm_ref[0], (((0,), (0,)), ((), ())),
                                   preferred_element_type=F32, precision=lax.Precision.HIGHEST)

    return pl.pallas_call(
        body, name="ada_weight_grad", grid=(nl,),
        in_specs=[pl.BlockSpec((nd, d), lambda l: (0, 0)), pl.BlockSpec((1, nd, ncol), lambda l: (l, 0, 0))],
        out_specs=pl.BlockSpec((1, d, ncol), lambda l: (l, 0, 0)),
        out_shape=jax.ShapeDtypeStruct((nl, d, ncol), F32), compiler_params=_cparams(("parallel",)),
    )(c_all, dmod_cols)


def _pad_rows(a, rows):
    return jnp.pad(a, ((0, rows - a.shape[0]), (0, 0)))


def _shard_cols(full, chip, width):
    return lax.dynamic_slice_in_dim(full, chip * width, width, axis=full.ndim - 1)


def kernel(x, c, ada_w, ada_b, pre_g, post_g, pool_w, pool_scale, cv_w_pw1, cv_b_pw1, cv_w_dw, cv_b_dw, cv_ln_g, cv_ln_b, cv_w_pw2, cv_b_pw2, ffn_w_up, ffn_w_dw, ffn_w_down, loss_target, m_ada_w, m_ada_b, m_pre_g, m_post_g, m_pool_w, m_pool_scale, m_cv_w_pw1, m_cv_b_pw1, m_cv_w_dw, m_cv_b_dw, m_cv_ln_g, m_cv_ln_b, m_cv_w_pw2, m_cv_b_pw2, m_ffn_w_up, m_ffn_w_dw, m_ffn_w_down, v_ada_w, v_ada_b, v_pre_g, v_post_g, v_pool_w, v_pool_scale, v_cv_w_pw1, v_cv_b_pw1, v_cv_w_dw, v_cv_b_dw, v_cv_ln_g, v_cv_ln_b, v_cv_w_pw2, v_cv_b_pw2, v_ffn_w_up, v_ffn_w_dw, v_ffn_w_down):
    s, d = x.shape[1], x.shape[2]
    dq = d // N_CHIPS
    n_g = pool_w.shape[1]
    gq = pool_w.shape[2]
    gd = pool_w.shape[3]
    kw = cv_w_dw.shape[1]
    cs = ffn_w_up.shape[2]
    fq = ffn_w_down.shape[1]
    chip = 2 * lax.axis_index("x") + lax.axis_index("y")
    core = lax.axis_index("c")
    chip1 = jnp.reshape(chip, (1,)).astype(jnp.int32)
    core1 = jnp.reshape(core, (1,)).astype(jnp.int32)
    xs, tgt = x[0], loss_target[0]

    small_rows = [pre_g.reshape(4, dq), post_g.reshape(4, dq), cv_w_dw[0], cv_b_dw, cv_ln_g, cv_ln_b, cv_b_pw2,
                  cv_b_pw1.reshape(2, dq)]
    small = jnp.concatenate(small_rows, axis=0)
    n_small = small.shape[0]
    small = _pad_rows(small, -(-n_small // 16) * 16)
    dwf = _pad_rows(ffn_w_dw.reshape(6, cs), 16)
    first = _AllGather([_cast_into_slot(pool_w.reshape(n_g * gq, gd), chip1), _cast_into_slot(small, chip1, dtype=F32),
                        _cast_into_slot(dwf, chip1, dtype=F32), _cast_into_slot(ffn_w_up[0], chip1)])
    (c_rep, mod_rep), ((g_pool, g_small, g_dwf, g_up0),) = _ada_forward(c, ada_w, _Comm([first]))
    c_all = c_rep[:, 0, :]
    mod = mod_rep[:, :, 0, :].transpose(1, 0, 2).reshape(ada_b.shape) + ada_b
    second = _AllGather([_cast_into_slot(ffn_w_down[0], chip1)])
    later = _AllGather([_cast_into_slot(cv_w_pw1[0], chip1), _cast_into_slot(cv_w_pw2[0], chip1),
                        _cast_into_slot(ffn_w_up[1], chip1), _cast_into_slot(ffn_w_down[1], chip1)])
    poolw_full = g_pool.reshape(N_CHIPS, n_g, gq, gd).transpose(1, 0, 2, 3).reshape(n_g, gd, gd)
    smallf = g_small.transpose(1, 0, 2).reshape(g_small.shape[1], d)
    pre_full, post_full = smallf[0:4].reshape(2, 2, d), smallf[4:8].reshape(2, 2, d)
    wdw31 = smallf[8:8 + kw]
    o = 8 + kw
    b_dw, ln_g, ln_b, b_pw2 = smallf[o:o + 1], smallf[o + 1:o + 2], smallf[o + 2:o + 3], smallf[o + 3:o + 4]
    b_pw1 = g_small[:, o + 4:o + 6, :].reshape(1, 2 * d)
    ffn_dw = g_dwf[:, :6, :].transpose(1, 0, 2).reshape(2, 3, N_CHIPS * cs)

    def sub_vec(layer, sub, extra=None):
        m6 = mod[layer].reshape(6, d)
        rows = [pre_full[layer, sub][None], 1.0 + m6[3 * sub + 1][None], m6[3 * sub][None], m6[3 * sub + 2][None],
                post_full[layer, sub][None]]
        if extra is not None:
            rows.append(extra)
        return _pad_rows(jnp.concatenate(rows, axis=0), 8)

    vec_pool = sub_vec(0, 0, pool_scale)
    vec_f0, vec_conv, vec_f1 = sub_vec(0, 1), sub_vec(1, 0), sub_vec(1, 1)
    cvec = _pad_rows(jnp.concatenate([b_dw, ln_g, ln_b, b_pw2], axis=0), 8)

    x1, ((g_dn0,),) = _pool_forward(xs, vec_pool, poolw_full, _Comm([second]))
    w_up0, w_dn0 = g_up0, g_dn0.reshape(2, 2 * fq, d)
    (x2, h_f0, a0_f0, cc_f0, u_f0, y_f0), ((g_pw1, g_pw2, g_up1, g_dn1),) = _ffn_forward(
        x1, vec_f0, w_up0, ffn_dw[0], w_dn0, _Comm([later]))
    pw2_full = g_pw2.reshape(d, d)
    w_up1, w_dn1 = g_up1, g_dn1.reshape(2, 2 * fq, d)
    x3, h_cv, a_cv, uc_cv, z_cv, y_cv = _conv_forward(x2, vec_conv, cvec, g_pw1, b_pw1, wdw31, pw2_full)
    (dx4, h_f1, a0_f1, cc_f1, u_f1, y_f1, loss_rows), _ = _ffn_forward(x3, vec_f1, w_up1, ffn_dw[1], w_dn1, target=tgt)

    dx3, da0_f1, dy_f1, sum_f1, dwdw_f1 = _ffn_backward(dx4, x3, y_f1, a0_f1, cc_f1, vec_f1, w_up1, ffn_dw[1], w_dn1)
    dx2, da_cv, dy_cv, sum_cv, dwdw_cv = _conv_backward(dx3, x2, y_cv, a_cv, uc_cv, vec_conv, cvec, g_pw1, wdw31, pw2_full)
    dx1, da0_f0, dy_f0, sum_f0, dwdw_f0 = _ffn_backward(dx2, x1, y_f0, a0_f0, cc_f0, vec_f0, w_up0, ffn_dw[0], w_dn0)
    dx0, sum_pool, gw_pool = _pool_backward(dx1, xs, vec_pool, poolw_full)
    gw_pool4 = gw_pool.reshape(n_g, N_CHIPS, gq, gd).transpose(1, 0, 2, 3).reshape(N_CHIPS, n_g * gq, gd)

    slab = jnp.concatenate([sum_f1, sum_cv, dwdw_cv, sum_f0, sum_pool, loss_rows], axis=0)
    wide = jnp.concatenate([dwdw_f1, dwdw_f0], axis=0)
    n_slab = slab.shape[0]
    mine = jnp.concatenate([slab, wide.reshape(-1, d)], axis=0)
    rows_of_all = _cast_into_slot(mine, 2 * chip1 + core1, N_DEV, F32)
    red = _Reducer(jnp.concatenate([chip1, core1]))

    def carried(call, *args, extra=()):
        ops, owners = red.steps()
        out, results = call(*args, _Comm(ops + list(extra)))
        red.absorb(owners, results[:len(ops)])
        return out, results[len(ops):]

    gw_up1, ((both_all,),) = carried(_weight_grad, h_f1[None], da0_f1, extra=[_GatherRows([rows_of_all])])
    r_up1 = red.add([gw_up1])
    r_up0 = red.add([carried(_weight_grad, h_f0[None], da0_f0)[0]])
    r_dn1 = red.add([carried(_weight_grad, u_f1, dy_f1[None])[0].reshape(N_CHIPS, fq, d)])
    r_dn0 = red.add([carried(_weight_grad, u_f0, dy_f0[None])[0].reshape(N_CHIPS, fq, d)])
    r_pw1 = red.add([carried(_weight_grad, h_cv[None], da_cv)[0]])
    r_last = red.add([carried(_weight_grad, z_cv[None], dy_cv[None])[0].reshape(N_CHIPS, dq, d), gw_pool4])

    tot_both = _sum_devices(both_all)
    slab_all, tot = both_all[:, :n_slab], tot_both[:n_slab]
    tot_wide = tot_both[n_slab:].reshape(wide.shape)
    kpad = dwdw_cv.shape[0]
    o_cv, o_dw, o_f0 = 8, 24, 24 + kpad
    o_pool, o_loss = o_f0 + 8, o_f0 + 16
    loss = jnp.sum(tot[o_loss])
    dmod_l0 = jnp.concatenate([slab_all[:, o_pool + 4], slab_all[:, o_pool + 3], slab_all[:, o_pool + 1],
                               slab_all[:, o_f0 + 4], slab_all[:, o_f0 + 3], slab_all[:, o_f0 + 1]], axis=-1)
    dmod_l1 = jnp.concatenate([slab_all[:, o_cv + 4], slab_all[:, o_cv + 3], slab_all[:, o_cv + 1],
                               slab_all[:, 4], slab_all[:, 3], slab_all[:, 1]], axis=-1)
    dmod = jnp.stack([dmod_l0, dmod_l1], axis=0)
    g_ada_b = _sum_devices(dmod.transpose(1, 0, 2))
    ncol = ada_w.shape[2]
    g_ada_w = _ada_weight_grad(c_all, _shard_cols(dmod, chip, ncol))

    g_pre = jnp.stack([jnp.stack([tot[o_pool + 2], tot[o_f0 + 2]]), jnp.stack([tot[o_cv + 2], tot[2]])])
    g_post = jnp.stack([jnp.stack([tot[o_pool + 0], tot[o_f0 + 0]]), jnp.stack([tot[o_cv + 0], tot[0]])])
    g_pool_scale = tot[o_pool + 5][None]
    g_b_pw2, g_ln_g, g_ln_b, g_b_dw = tot[o_cv + 5], tot[o_cv + 6], tot[o_cv + 7], tot[o_cv + 8]
    g_b_pw1 = jnp.concatenate([tot[o_cv + 9], tot[o_cv + 10]])
    g_w_dw31 = tot[o_dw:o_dw + kw]
    g_ffn_dw = jnp.stack([tot_wide[8:11], tot_wide[0:3]])

    grads_small = {
        "pre_g": _shard_cols(g_pre, chip, dq), "post_g": _shard_cols(g_post, chip, dq),
        "pool_scale": g_pool_scale, "cv_b_pw1": _shard_cols(g_b_pw1[None], chip, 2 * dq),
        "cv_w_dw": _shard_cols(g_w_dw31[None], chip, dq), "cv_b_dw": _shard_cols(g_b_dw[None], chip, dq),
        "cv_ln_g": _shard_cols(g_ln_g[None], chip, dq), "cv_ln_b": _shard_cols(g_ln_b[None], chip, dq),
        "cv_b_pw2": _shard_cols(g_b_pw2[None], chip, dq), "ffn_w_dw": _shard_cols(g_ffn_dw, chip, cs),
        "ada_b": g_ada_b,
    }
    params_small = {
        "pre_g": (pre_g, m_pre_g, v_pre_g), "post_g": (post_g, m_post_g, v_post_g),
        "pool_scale": (pool_scale, m_pool_scale, v_pool_scale), "cv_b_pw1": (cv_b_pw1, m_cv_b_pw1, v_cv_b_pw1),
        "cv_w_dw": (cv_w_dw, m_cv_w_dw, v_cv_w_dw), "cv_b_dw": (cv_b_dw, m_cv_b_dw, v_cv_b_dw),
        "cv_ln_g": (cv_ln_g, m_cv_ln_g, v_cv_ln_g), "cv_ln_b": (cv_ln_b, m_cv_ln_b, v_cv_ln_b),
        "cv_b_pw2": (cv_b_pw2, m_cv_b_pw2, v_cv_b_pw2), "ffn_w_dw": (ffn_w_dw, m_ffn_w_dw, v_ffn_w_dw),
        "ada_b": (ada_b, m_ada_b, v_ada_b),
    }
    names = list(params_small)
    small_g = {nm: grads_small[nm].reshape(params_small[nm][0].shape) for nm in names}
    updated = _adamw_many([(params_small[nm][0], small_g[nm], params_small[nm][1], params_small[nm][2])
                           for nm in names])
    small_d = {nm: u[0] for nm, u in zip(names, updated)}
    small_m = {nm: u[1] for nm, u in zip(names, updated)}
    small_v = {nm: u[2] for nm, u in zip(names, updated)}

    red.drain()
    big_p = {
        "ada_w": (ada_w, m_ada_w, v_ada_w), "pool_w": (pool_w, m_pool_w, v_pool_w),
        "cv_w_pw1": (cv_w_pw1, m_cv_w_pw1, v_cv_w_pw1), "cv_w_pw2": (cv_w_pw2, m_cv_w_pw2, v_cv_w_pw2),
        "ffn_w_up": (ffn_w_up, m_ffn_w_up, v_ffn_w_up), "ffn_w_down": (ffn_w_down, m_ffn_w_down, v_ffn_w_down),
    }
    big_g, big_d, big_m, big_v = {}, {}, {}, {}

    def update(nm, grad):
        w, m, v = big_p[nm]
        as3 = lambda t: t.reshape((-1,) + w.shape[-2:])
        (dl, nm_, nv_), _ = _adamw(as3(w), as3(grad), as3(m), as3(v))
        big_g[nm] = grad.reshape(w.shape)
        big_d[nm], big_m[nm], big_v[nm] = dl.reshape(w.shape), nm_.reshape(w.shape), nv_.reshape(w.shape)

    full = lambda group, k=0: group["full"][k]
    update("ffn_w_up", jnp.stack([full(r_up0), full(r_up1)]))
    update("ada_w", g_ada_w)
    update("ffn_w_down", jnp.stack([full(r_dn0), full(r_dn1)]))
    update("cv_w_pw1", full(r_pw1))
    update("cv_w_pw2", full(r_last, 0))
    update("pool_w", full(r_last, 1))

    order = ["ada_w", "ada_b", "pre_g", "post_g", "pool_w", "pool_scale", "cv_w_pw1", "cv_b_pw1", "cv_w_dw", "cv_b_dw",
             "cv_ln_g", "cv_ln_b", "cv_w_pw2", "cv_b_pw2", "ffn_w_up", "ffn_w_dw", "ffn_w_down"]
    pick = lambda bigs, smalls: [bigs[nm] if nm in bigs else smalls[nm] for nm in order]
    return (loss, dx0[None], *pick(big_g, small_g), *pick(big_d, small_d), *pick(big_m, small_m),
            *pick(big_v, small_v))
```

```python
import functools

import jax
import jax.numpy as jnp
from jax import lax
from jax.experimental import pallas as pl
from jax.experimental.pallas import tpu as pltpu

F32 = jnp.float32
BF16 = jnp.bfloat16
EPS = 1e-6
N_CHIPS = 4
N_DEV = 8
POOL_WINDOWS = (2, 4, 8, 16)
POOL_HALO = 16
FFN_HALO = 16
MXU_LANES = 256
CONV_ROWS, CONV_LANES = 128, 128
ADAM_LR = 0.001
ADAM_B1 = 0.9
ADAM_B2 = 0.999
ADAM_EPS = 1e-08
ADAM_WD = 0.01
ADAM_STEP = 10
V7X_VMEM_LIMIT = 58 * 1024 * 1024
MESH = pl.DeviceIdType.MESH


def _cparams(sem=None, vmem=V7X_VMEM_LIMIT):
    return pltpu.CompilerParams(dimension_semantics=sem, vmem_limit_bytes=vmem)


def _row_tile(n, want):
    if n <= want:
        return n
    t = want - want % 8
    while n % t:
        t -= 8
    return t


def _lane_chunks(width):
    out, c = [], 0
    while c < width:
        w = min(512, width - c)
        out.append((c, w))
        c += w
    return out


def _dot(a, b):
    return jnp.dot(a, b, preferred_element_type=F32)


def _dot_nt(a, b):
    return lax.dot_general(a, b, (((1,), (1,)), ((), ())), preferred_element_type=F32)


def _store_dot_nt(dst, a_ref, b_ref):
    dst[...] = _dot_nt(a_ref[...], b_ref[...])


def _store_dot_nt2(dst, a1_ref, a2_ref, b1_ref, b2_ref):
    dst[...] = _dot_nt(a1_ref[...], b1_ref[...]) + _dot_nt(a2_ref[...], b2_ref[...])


def _dot_tn(a, b):
    return lax.dot_general(a, b, (((0,), (0,)), ((), ())), preferred_element_type=F32)


def _rms(x):
    r = lax.rsqrt(jnp.mean(x * x, axis=-1, keepdims=True) + EPS)
    return x * r, r


def _rms_bwd(dyn, yn, r):
    return r * (dyn - yn * jnp.mean(dyn * yn, axis=-1, keepdims=True))


def _sigmoid(x):
    return 0.5 * jnp.tanh(0.5 * x) + 0.5


def _colsum(x):
    return jnp.sum(x, axis=0, keepdims=True)


def _shift_down(x, k):
    return x if k == 0 else pltpu.roll(x, k, 0)


def _shift_up(x, k):
    return x if k == 0 else pltpu.roll(x, x.shape[0] - k, 0)


def _vec_rows(vec):
    return vec[0:1] * vec[1:2], vec[2:3], vec[3:4], vec[4:5]


def _norm_sums(do, yn, dh, xn, vec):
    p, q = _colsum(do * yn), _colsum(dh * xn)
    return [p * vec[3:4], p * vec[4:5], q * vec[1:2], q * vec[0:1]]


def _add_rows(sum_ref, rows):
    for k, r in enumerate(rows):
        sum_ref[k:k + 1, :] += r


def _ada_forward(c, ada_w, comm=None):
    n_layers, d, ncol = ada_w.shape
    comm = comm or _Comm([])
    nc = len(comm.arrays)

    def body(*refs):
        c_ref, w_ref = refs[:2]
        cin = refs[2:2 + nc]
        call_ref, mod_ref = refs[2 + nc:4 + nc]
        cout = refs[4 + nc:4 + 2 * nc]
        part_ref, sendbuf, send_sems, recv_sems, send2, recv2 = refs[4 + 2 * nc:10 + 2 * nc]
        carried_sems = refs[10 + 2 * nc:]
        if nc:
            comm.run(0, cin, cout, *carried_sems)
        x, y, cc = lax.axis_index("x"), lax.axis_index("y"), lax.axis_index("c")
        me = 4 * x + 2 * y + cc
        rel = [(x, y, 1 - cc), (1 - x, y, cc), (x, 1 - y, cc), (1 - x, 1 - y, cc),
               (1 - x, y, 1 - cc), (x, 1 - y, 1 - cc), (1 - x, 1 - y, 1 - cc)]
        cv = c_ref[...]
        call_ref[me] = jnp.broadcast_to(cv * _sigmoid(cv), (8, d))

        def gather(k, block, to):
            blk = call_ref.at[block]
            return pltpu.make_async_remote_copy(src_ref=blk, dst_ref=blk, send_sem=send_sems.at[k],
                                                recv_sem=recv_sems.at[k], device_id=to, device_id_type=MESH)

        for k, to in enumerate(rel):
            gather(k, me, to).start()
        for k, (px, py, pc) in enumerate(rel):
            gather(k, 4 * px + 2 * py + pc, rel[k]).wait_recv()
        for k, to in enumerate(rel):
            gather(k, me, to).wait_send()

        ca = call_ref[...].reshape(8 * N_DEV, d)
        for l in range(n_layers):
            part_ref[l] = jnp.dot(ca, w_ref[l], preferred_element_type=F32, precision=lax.Precision.HIGHEST)

        j = 2 * x + y
        chips = [(1 - x, y), (x, 1 - y), (1 - x, 1 - y)]

        def rows_of(b):
            return part_ref[:, pl.ds(pl.multiple_of(8 * b, 8), 8), :]

        def scatter(k, src_j, to):
            return pltpu.make_async_remote_copy(
                src_ref=sendbuf.at[k], dst_ref=mod_ref.at[src_j], send_sem=send2.at[k], recv_sem=recv2.at[k],
                device_id=to, device_id_type=MESH)

        mod_ref[j] = rows_of(me)
        for k, (px, py) in enumerate(chips):
            sendbuf[k] = rows_of(4 * px + 2 * py + cc)
            scatter(k, j, (px, py, cc)).start()
        for k, (px, py) in enumerate(chips):
            scatter(k, 2 * px + py, (px, py, cc)).wait_recv()
        for k, (px, py) in enumerate(chips):
            scatter(k, j, (px, py, cc)).wait_send()
        if nc:
            comm.run(1, cin, cout, *carried_sems)
            comm.run(2, cin, cout, *carried_sems)

    vm = pl.BlockSpec(memory_space=pltpu.VMEM)
    res = pl.pallas_call(
        body, name="ada_forward",
        out_shape=(jax.ShapeDtypeStruct((N_DEV, 8, d), F32), jax.ShapeDtypeStruct((N_CHIPS, n_layers, 8, ncol), F32),
                   *comm.outs),
        in_specs=[vm, vm] + comm.specs(), out_specs=(vm, vm, *comm.specs()),
        input_output_aliases=comm.aliases(2, 2),
        scratch_shapes=[pltpu.VMEM((n_layers, 8 * N_DEV, ncol), F32), pltpu.VMEM((3, n_layers, 8, ncol), F32),
                        pltpu.SemaphoreType.DMA((7,)), pltpu.SemaphoreType.DMA((7,)),
                        pltpu.SemaphoreType.DMA((3,)), pltpu.SemaphoreType.DMA((3,))] + (comm.scratch() if nc else []),
        compiler_params=_cparams(),
    )(c, ada_w, *comm.arrays)
    return res[:2], comm.split(res[2:])


def _cast_into_slot(w2d, slot, n_slots=N_CHIPS, dtype=None):
    r, c = w2d.shape
    tr = _row_tile(r, 256)
    dtype = dtype or BF16

    def body(slot_ref, w_ref, o_ref):
        o_ref[0] = w_ref[...].astype(dtype)

    return pl.pallas_call(
        body, name="cast_into_slot",
        grid_spec=pltpu.PrefetchScalarGridSpec(
            num_scalar_prefetch=1, grid=(r // tr,),
            in_specs=[pl.BlockSpec((tr, c), lambda i, slot_ref: (i, 0))],
            out_specs=pl.BlockSpec((1, tr, c), lambda i, slot_ref: (slot_ref[0], i, 0))),
        out_shape=jax.ShapeDtypeStruct((n_slots, r, c), dtype), compiler_params=_cparams(("parallel",)),
    )(slot, w2d)


def _place():
    x, y, c = lax.axis_index("x"), lax.axis_index("y"), lax.axis_index("c")
    return x, y, c, [(1 - x, y), (x, 1 - y), (1 - x, 1 - y)]


def _remote(src, dst, send_sem, recv_sem, to):
    return pltpu.make_async_remote_copy(src_ref=src, dst_ref=dst, send_sem=send_sem, recv_sem=recv_sem,
                                        device_id=to, device_id_type=MESH)


class _AllGather:
    def __init__(self, bufs):
        self.arrays = list(bufs)
        self.outs = [jax.ShapeDtypeStruct(b.shape, b.dtype) for b in bufs]
        self.aliased = True
        self.n_sems = 6 * len(bufs)

    def run(self, phase, ins, outs, send_sems, recv_sems, base):
        x, y, c, chips = _place()
        j = 2 * x + y
        for k, buf in enumerate(outs):
            half = buf.shape[1] // 2

            def part(src_j, h):
                return buf.at[src_j, pl.ds(h * half, half), :]

            def ici(r, src_j, to):
                s = base + 6 * k + r
                return _remote(part(src_j, c), part(src_j, c), send_sems.at[s], recv_sems.at[s], to)

            def d2d(r, src_j, h):
                s = base + 6 * k + 3 + r
                return _remote(part(src_j, h), part(src_j, h), send_sems.at[s], recv_sems.at[s], (x, y, 1 - c))

            for r, (px, py) in enumerate(chips):
                if phase == 0:
                    ici(r, j, (px, py, c)).start()
                elif phase == 1:
                    ici(r, 2 * px + py, (px, py, c)).wait_recv()
                    d2d(r, 2 * px + py, c).start()
                else:
                    d2d(r, 2 * px + py, 1 - c).wait_recv()
                    ici(r, j, (px, py, c)).wait_send()
                    d2d(r, 2 * px + py, c).wait_send()


class _Swap:
    def __init__(self, grads):
        self.arrays = list(grads)
        self.outs = [jax.ShapeDtypeStruct((g.shape[0],) + g.shape[2:], g.dtype) for g in grads]
        self.aliased = False
        self.n_sems = len(grads)

    def run(self, phase, ins, outs, send_sems, recv_sems, base):
        x, y, c, _ = _place()
        for k in range(len(ins)):
            cp = _remote(ins[k].at[:, 1 - c], outs[k], send_sems.at[base + k], recv_sems.at[base + k], (x, y, 1 - c))
            if phase == 0:
                cp.start()
            elif phase == 2:
                cp.wait()


class _Exchange:
    def __init__(self, parts):
        self.arrays = list(parts)
        self.outs = [jax.ShapeDtypeStruct((3,) + p.shape[1:], p.dtype) for p in parts]
        self.aliased = False
        self.n_sems = 3 * len(parts)

    def run(self, phase, ins, outs, send_sems, recv_sems, base):
        x, y, c, chips = _place()
        for k in range(len(ins)):
            for r, (px, py) in enumerate(chips):
                s = base + 3 * k + r
                cp = _remote(ins[k].at[2 * px + py], outs[k].at[r], send_sems.at[s], recv_sems.at[s], (px, py, c))
                if phase == 0:
                    cp.start()
                elif phase == 2:
                    cp.wait()


class _Join:
    def __init__(self, bufs):
        self.arrays = list(bufs)
        self.outs = [jax.ShapeDtypeStruct(b.shape, b.dtype) for b in bufs]
        self.aliased = True
        self.n_sems = len(bufs)

    def run(self, phase, ins, outs, send_sems, recv_sems, base):
        x, y, c, _ = _place()
        for k, buf in enumerate(outs):
            mine = _remote(buf.at[c], buf.at[c], send_sems.at[base + k], recv_sems.at[base + k], (x, y, 1 - c))
            if phase == 0:
                mine.start()
            elif phase == 2:
                mine.wait_send()
                _remote(buf.at[1 - c], buf.at[1 - c], send_sems.at[base + k], recv_sems.at[base + k],
                        (x, y, 1 - c)).wait_recv()


class _Comm:
    def __init__(self, ops):
        self.ops = list(ops)
        self.arrays = [a for op in self.ops for a in op.arrays]
        self.outs = [o for op in self.ops for o in op.outs]
        self.n_sems = sum(op.n_sems for op in self.ops)

    def specs(self):
        return [pl.BlockSpec(memory_space=pl.ANY)] * len(self.arrays)

    def aliases(self, first_in, first_out):
        out, k = {}, 0
        for op in self.ops:
            for i in range(len(op.arrays)):
                if op.aliased:
                    out[first_in + k + i] = first_out + k + i
            k += len(op.arrays)
        return out

    def scratch(self):
        return [pltpu.SemaphoreType.DMA((self.n_sems,)), pltpu.SemaphoreType.DMA((self.n_sems,))]

    def run(self, phase, ins, outs, send_sems, recv_sems):
        k = base = 0
        for op in self.ops:
            n = len(op.arrays)
            op.run(phase, ins[k:k + n], outs[k:k + n], send_sems, recv_sems, base)
            k += n
            base += op.n_sems

    def split(self, results):
        out, k = [], 0
        for op in self.ops:
            out.append(list(results[k:k + len(op.arrays)]))
            k += len(op.arrays)
        return out


def _communicate(ops):
    comm = _Comm(ops)
    n = len(comm.arrays)

    def body(*refs):
        ins, outs, (send_sems, recv_sems) = refs[:n], refs[n:2 * n], refs[2 * n:]
        for phase in range(3):
            comm.run(phase, ins, outs, send_sems, recv_sems)

    res = pl.pallas_call(
        body, name="communicate", out_shape=tuple(comm.outs), in_specs=comm.specs(), out_specs=tuple(comm.specs()),
        input_output_aliases=comm.aliases(0, 0), scratch_shapes=comm.scratch(),
    )(*comm.arrays)
    return comm.split(res)


def _pool_core(he, w_ref, scale, first_row, halo, n_rows):
    d = he.shape[1]
    gd = d // len(POOL_WINDOWS)
    t = first_row + lax.broadcasted_iota(jnp.int32, (n_rows, 1), 0)
    pooled, ypre, cnts = [], [], []
    for g, w in enumerate(POOL_WINDOWS):
        hg = he[:, g * gd:(g + 1) * gd]
        s, k = hg, 1
        while k < w:
            s = s + _shift_down(s, k)
            k *= 2
        cnt = jnp.minimum(t + 1, w).astype(F32)
        p = s[halo:] / cnt - hg[halo:]
        pooled.append(p.astype(BF16))
        cnts.append(cnt)
        ypre.append(_dot(pooled[-1], w_ref[g]))
    return pooled, jnp.concatenate(ypre, axis=1), cnts


def _pool_forward(x, vec, pool_w, comm=None):
    s, d = x.shape
    ts = _row_tile(s, 512)
    nb = s // ts
    n_g, gd, _ = pool_w.shape
    comm = comm or _Comm([])
    nc = len(comm.arrays)

    def body(*refs):
        x_ref, vec_ref, w_ref = refs[:3]
        cin = refs[3:3 + nc]
        o_ref = refs[3 + nc]
        cout = refs[4 + nc:4 + 2 * nc]
        carry = refs[4 + 2 * nc]
        sems = refs[5 + 2 * nc:]
        i = pl.program_id(0)

        @pl.when(i == 0)
        def _():
            carry[...] = jnp.zeros_like(carry)
            if nc:
                comm.run(0, cin, cout, *sems)

        if nc:
            @pl.when(i == nb - 1)
            def _():
                comm.run(1, cin, cout, *sems)

        vec = vec_ref[...]
        a, sh, gt, gpost = _vec_rows(vec)
        xb = x_ref[...]
        xn, _ = _rms(xb)
        h = xn * a + sh
        he = jnp.concatenate([carry[...], h], axis=0)
        carry[...] = h[ts - POOL_HALO:]
        _, ypre, _ = _pool_core(he, w_ref, vec[5:6], i * ts, POOL_HALO, ts)
        yn, _ = _rms(ypre * vec[5:6])
        o_ref[...] = xb + gt * (yn * gpost)
        if nc:
            @pl.when(i == nb - 1)
            def _():
                comm.run(2, cin, cout, *sems)

    res = pl.pallas_call(
        body, name="pool_forward", grid=(nb,),
        in_specs=[pl.BlockSpec((ts, d), lambda i: (i, 0)), pl.BlockSpec((8, d), lambda i: (0, 0)),
                  pl.BlockSpec((n_g, gd, gd), lambda i: (0, 0, 0))] + comm.specs(),
        out_specs=(pl.BlockSpec((ts, d), lambda i: (i, 0)), *comm.specs()),
        out_shape=(jax.ShapeDtypeStruct((s, d), F32), *comm.outs),
        input_output_aliases=comm.aliases(3, 1),
        scratch_shapes=[pltpu.VMEM((POOL_HALO, d), F32)] + (comm.scratch() if nc else []),
        compiler_params=_cparams(("arbitrary",)),
    )(x, vec, pool_w, *comm.arrays)
    return res[0], comm.split(res[1:])


def _pool_backward(dout, x, vec, pool_w):
    s, d = x.shape
    ts = _row_tile(s, 512)
    nb = s // ts
    hb = ts // POOL_HALO
    n_g, gd, _ = pool_w.shape

    def body(do_ref, x_ref, xh_ref, vec_ref, w_ref, dx_ref, sum_ref, dw_ref, carry):
        step = pl.program_id(0)
        i = nb - 1 - step

        @pl.when(step == 0)
        def _():
            carry[...] = jnp.zeros_like(carry)
            sum_ref[...] = jnp.zeros_like(sum_ref)
            dw_ref[...] = jnp.zeros_like(dw_ref)

        vec = vec_ref[...]
        a, sh, gt, gpost = _vec_rows(vec)
        scale = vec[5:6]
        do = do_ref[...]
        xe = jnp.concatenate([xh_ref[...], x_ref[...]], axis=0)
        xne, re = _rms(xe)
        he = xne * a + sh
        rowid = lax.broadcasted_iota(jnp.int32, (POOL_HALO + ts, 1), 0)
        he = jnp.where((rowid >= POOL_HALO) | (i > 0), he, 0.0)
        xn, r = xne[POOL_HALO:], re[POOL_HALO:]
        pooled, ypre, cnts = _pool_core(he, w_ref, scale, i * ts, POOL_HALO, ts)
        yn, ry = _rms(ypre * scale)
        dyn = do * (gt * gpost)
        dy = _rms_bwd(dyn, yn, ry)
        dypre = (dy * scale).astype(BF16)
        dh_parts, q_parts = [], []
        for g, w in enumerate(POOL_WINDOWS):
            dyg = dypre[:, g * gd:(g + 1) * gd]
            dpool = _dot_nt(dyg, w_ref[g])
            dw_ref[g] += _dot_tn(pooled[g], dyg)
            q = dpool / cnts[g]
            qe = jnp.concatenate([q, carry[:, g * gd:(g + 1) * gd]], axis=0)
            acc, k = qe, 1
            while k < w:
                acc = acc + _shift_up(acc, k)
                k *= 2
            dh_parts.append(acc[:ts] - dpool)
            q_parts.append(q[:POOL_HALO])
        carry[...] = jnp.concatenate(q_parts, axis=1)
        dh = jnp.concatenate(dh_parts, axis=1)
        dxn = dh * a
        dx_ref[...] = do + _rms_bwd(dxn, xn, r)
        _add_rows(sum_ref, _norm_sums(do, yn, dh, xn, vec) + [_colsum(dh), _colsum(dy * ypre)])

    blk = lambda st: (nb - 1 - st, 0)
    return pl.pallas_call(
        body, name="pool_backward", grid=(nb,),
        in_specs=[pl.BlockSpec((ts, d), blk), pl.BlockSpec((ts, d), blk),
                  pl.BlockSpec((POOL_HALO, d), lambda st: (jnp.maximum((nb - 1 - st) * hb - 1, 0), 0)),
                  pl.BlockSpec((8, d), lambda st: (0, 0)), pl.BlockSpec((n_g, gd, gd), lambda st: (0, 0, 0))],
        out_specs=(pl.BlockSpec((ts, d), blk), pl.BlockSpec((8, d), lambda st: (0, 0)),
                   pl.BlockSpec((n_g, gd, gd), lambda st: (0, 0, 0))),
        out_shape=(jax.ShapeDtypeStruct((s, d), F32), jax.ShapeDtypeStruct((8, d), F32),
                   jax.ShapeDtypeStruct((n_g, gd, gd), F32)),
        scratch_shapes=[pltpu.VMEM((POOL_HALO, d), F32)],
        compiler_params=_cparams(("arbitrary",)),
    )(dout, x, x, vec, pool_w)


def _ffn_forward(x, vec, w_up, w_dw, w_down, comm=None, target=None):
    s, d = x.shape
    _, _, cs = w_up.shape
    ts = _row_tile(s, 256)
    nb = s // ts
    chunks = _lane_chunks(cs)
    comm = comm or _Comm([])
    nc = len(comm.arrays)
    nl = 0 if target is None else 1
    n_in, n_out = 5 + nl, 6 + nl

    def body(*refs):
        x_ref, vec_ref, wup_ref, wdw_ref, wdn_ref = refs[:5]
        cin = refs[n_in:n_in + nc]
        o_ref, h_ref, a0_ref, cc_ref, u_ref, y_ref = refs[n_in + nc:n_in + nc + 6]
        loss_ref = refs[n_in + nc + 6] if nl else None
        cout = refs[n_in + nc + n_out:n_in + 2 * nc + n_out]
        carry = refs[n_in + 2 * nc + n_out]
        sems = refs[n_in + 2 * nc + n_out + 1:]
        i = pl.program_id(0)

        @pl.when(i == 0)
        def _():
            carry[...] = jnp.zeros_like(carry)
            if nl:
                loss_ref[...] = jnp.zeros_like(loss_ref)
            if nc:
                comm.run(0, cin, cout, *sems)

        if nc:
            @pl.when(i == (3 * nb) // 4)
            def _():
                comm.run(1, cin, cout, *sems)

        vec = vec_ref[...]
        a, sh, gt, gpost = _vec_rows(vec)
        xb = x_ref[...]
        xn, _ = _rms(xb)
        hb = (xn * a + sh).astype(BF16)
        h_ref[...] = hb
        for q in range(2):
            for c0, cw in chunks:
                conv = []
                for j in (q, q + 2):
                    a0 = _dot(hb, wup_ref[j, :, c0:c0 + cw])
                    a0_ref[j, :, c0:c0 + cw] = a0.astype(BF16)
                    ae = jnp.concatenate([carry[j, :, c0:c0 + cw], a0], axis=0)
                    carry[j, :, c0:c0 + cw] = a0[ts - FFN_HALO:]
                    w = wdw_ref[:, j * cs + c0:j * cs + c0 + cw]
                    conv.append((w[2:3] * ae + w[1:2] * _shift_down(ae, 1) + w[0:1] * _shift_down(ae, 2))[FFN_HALO:])
                    cc_ref[j, :, c0:c0 + cw] = conv[-1].astype(BF16)
                u_ref[q, :, c0:c0 + cw] = (conv[0] * _sigmoid(conv[0]) * conv[1]).astype(BF16)
        y = _dot(u_ref[0], wdn_ref[0]) + _dot(u_ref[1], wdn_ref[1])
        y_ref[...] = y
        yn, _ = _rms(y)
        x_out = xb + gt * (yn * gpost)
        if nl:
            err = x_out - refs[5][...]
            o_ref[...] = err * (1.0 / d)
            loss_ref[0:1, :] += _colsum(err * err) * (0.5 / d)
        else:
            o_ref[...] = x_out
        if nc:
            @pl.when(i == nb - 1)
            def _():
                comm.run(2, cin, cout, *sems)

    const3 = lambda i: (0, 0, 0)
    res = pl.pallas_call(
        body, name="ffn_forward", grid=(nb,),
        in_specs=[pl.BlockSpec((ts, d), lambda i: (i, 0)), pl.BlockSpec((8, d), lambda i: (0, 0)),
                  pl.BlockSpec(w_up.shape, const3, pipeline_mode=pl.Buffered(1)),
                  pl.BlockSpec(w_dw.shape, lambda i: (0, 0)),
                  pl.BlockSpec(w_down.shape, const3, pipeline_mode=pl.Buffered(1))]
        + [pl.BlockSpec((ts, d), lambda i: (i, 0))] * nl + comm.specs(),
        out_specs=(pl.BlockSpec((ts, d), lambda i: (i, 0)), pl.BlockSpec((ts, d), lambda i: (i, 0)),
                   pl.BlockSpec((4, ts, cs), lambda i: (0, i, 0)), pl.BlockSpec((4, ts, cs), lambda i: (0, i, 0)),
                   pl.BlockSpec((2, ts, cs), lambda i: (0, i, 0)), pl.BlockSpec((ts, d), lambda i: (i, 0)),
                   *[pl.BlockSpec((8, d), lambda i: (0, 0))] * nl, *comm.specs()),
        out_shape=(jax.ShapeDtypeStruct((s, d), F32), jax.ShapeDtypeStruct((s, d), BF16),
                   jax.ShapeDtypeStruct((4, s, cs), BF16), jax.ShapeDtypeStruct((4, s, cs), BF16),
                   jax.ShapeDtypeStruct((2, s, cs), BF16), jax.ShapeDtypeStruct((s, d), F32),
                   *[jax.ShapeDtypeStruct((8, d), F32)] * nl, *comm.outs),
        input_output_aliases=comm.aliases(n_in, n_out),
        scratch_shapes=[pltpu.VMEM((4, FFN_HALO, cs), F32)] + (comm.scratch() if nc else []),
        compiler_params=_cparams(("arbitrary",)),
    )(x, vec, w_up, w_dw, w_down, *([target] * nl), *comm.arrays)
    return res[:n_out], comm.split(res[n_out:])


def _ffn_backward(dout, x, y, a0, cc, vec, w_up, w_dw, w_down):
    s, d = x.shape
    _, _, cs = w_up.shape
    ts = _row_tile(s, 256)
    nb = s // ts
    chunks = _lane_chunks(cs)

    def body(do_ref, x_ref, y_ref, a0_ref, cc_ref, vec_ref, wup_ref, wdw_ref, wdn_ref,
             dx_ref, da0_ref, dy_ref, sum_ref, dwdw_ref, carry, du_s, dh_s):
        step = pl.program_id(0)

        @pl.when(step == 0)
        def _():
            carry[...] = jnp.zeros_like(carry)
            sum_ref[...] = jnp.zeros_like(sum_ref)
            dwdw_ref[...] = jnp.zeros_like(dwdw_ref)

        vec = vec_ref[...]
        a, sh, gt, gpost = _vec_rows(vec)
        do = do_ref[...]
        yn, ry = _rms(y_ref[...])
        dy = _rms_bwd(do * (gt * gpost), yn, ry)
        dyb = dy.astype(BF16)
        dy_ref[...] = dyb
        order = [(q, c0, cw) for q in range(2) for c0, cw in chunks]

        def du_pieces(idx):
            q, c0, cw = order[idx]
            return [functools.partial(_store_dot_nt, du_s.at[idx % 2, :, n0:min(n0 + MXU_LANES, cw)], dy_ref,
                                      wdn_ref.at[q, c0 + n0:c0 + min(n0 + MXU_LANES, cw), :])
                    for n0 in range(0, cw, MXU_LANES)]

        def dh_pieces():
            return [functools.partial(_store_dot_nt2, dh_s.at[:, n0:n0 + MXU_LANES], da0_ref.at[0], da0_ref.at[2],
                                      wup_ref.at[0, n0:n0 + MXU_LANES, :], wup_ref.at[2, n0:n0 + MXU_LANES, :])
                    for n0 in range(0, d, MXU_LANES)]

        for piece in du_pieces(0):
            piece()
        later = dh_pieces()
        for idx, (q, c0, cw) in enumerate(order):
            work = du_pieces(idx + 1) if idx + 1 < len(order) else []
            if q == 1:
                share = -(-len(later) // (len(order) - idx))
                work, later = work + later[:share], later[share:]

            def pump(part, of=3):
                for piece in work[part::of]:
                    piece()

            cg = cc_ref[q, :, c0:c0 + cw].astype(F32)
            cv = cc_ref[q + 2, :, c0:c0 + cw].astype(F32)
            sg = _sigmoid(cg)
            sl = cg * sg
            du = du_s[idx % 2, :, :cw]
            dconv = {q: du * cv * (sg * (1.0 + cg * (1.0 - sg))), q + 2: du * sl}
            pump(0)
            for part, j in enumerate((q, q + 2)):
                dae = jnp.concatenate([dconv[j], carry[j, :, c0:c0 + cw]], axis=0)
                carry[j, :, c0:c0 + cw] = dconv[j][:FFN_HALO]
                up1 = _shift_down(dae, FFN_HALO - 1)[FFN_HALO:]
                up2 = _shift_down(dae, FFN_HALO - 2)[FFN_HALO:]
                lanes = slice(j * cs + c0, j * cs + c0 + cw)
                w = wdw_ref[:, lanes]
                da0_ref[j, :, c0:c0 + cw] = (w[2:3] * dconv[j] + w[1:2] * up1 + w[0:1] * up2).astype(BF16)
                a0 = a0_ref[j, :, c0:c0 + cw].astype(F32)
                dwdw_ref[0:1, lanes] += _colsum(up2 * a0)
                dwdw_ref[1:2, lanes] += _colsum(up1 * a0)
                dwdw_ref[2:3, lanes] += _colsum(dconv[j] * a0)
                pump(part + 1)
        dh = dh_s[...] + _dot_nt(da0_ref[1], wup_ref[1]) + _dot_nt(da0_ref[3], wup_ref[3])
        xn, r = _rms(x_ref[...])
        dx_ref[...] = do + _rms_bwd(dh * a, xn, r)
        _add_rows(sum_ref, _norm_sums(do, yn, dh, xn, vec) + [_colsum(dh)])

    blk = lambda st: (nb - 1 - st, 0)
    blk3 = lambda st: (0, nb - 1 - st, 0)
    const3 = lambda st: (0, 0, 0)
    return pl.pallas_call(
        body, name="ffn_backward", grid=(nb,),
        in_specs=[pl.BlockSpec((ts, d), blk), pl.BlockSpec((ts, d), blk), pl.BlockSpec((ts, d), blk),
                  pl.BlockSpec((4, ts, cs), blk3), pl.BlockSpec((4, ts, cs), blk3),
                  pl.BlockSpec((8, d), lambda st: (0, 0)),
                  pl.BlockSpec(w_up.shape, const3, pipeline_mode=pl.Buffered(1)),
                  pl.BlockSpec(w_dw.shape, lambda st: (0, 0)),
                  pl.BlockSpec(w_down.shape, const3, pipeline_mode=pl.Buffered(1))],
        out_specs=(pl.BlockSpec((ts, d), blk), pl.BlockSpec((4, ts, cs), blk3),
                   pl.BlockSpec((ts, d), blk), pl.BlockSpec((8, d), lambda st: (0, 0)),
                   pl.BlockSpec((8, 4 * cs), lambda st: (0, 0))),
        out_shape=(jax.ShapeDtypeStruct((s, d), F32), jax.ShapeDtypeStruct((4, s, cs), BF16),
                   jax.ShapeDtypeStruct((s, d), BF16),
                   jax.ShapeDtypeStruct((8, d), F32), jax.ShapeDtypeStruct((8, 4 * cs), F32)),
        scratch_shapes=[pltpu.VMEM((4, FFN_HALO, cs), F32), pltpu.VMEM((2, ts, max(cw for _, cw in chunks)), F32),
                        pltpu.VMEM((ts, d), F32)],
        compiler_params=_cparams(("arbitrary",)),
    )(dout, x, y, a0, cc, vec, w_up, w_dw, w_down)


def _conv_halo(width):
    return -(-(width - 1) // 8) * 8


def _conv_forward(x, vec, cvec, w_pw1, b_pw1, w_dw, w_pw2):
    s, d = x.shape
    kw = w_dw.shape[0]
    halo = _conv_halo(kw)
    ts = _row_tile(s, 512)
    hd = d // 2

    def body(x_ref, vec_ref, cvec_ref, w1_ref, b1_ref, wdw_ref, w2_ref,
             o_ref, h_ref, a_ref, uc_ref, z_ref, y_ref, carry):
        i = pl.program_id(0)

        @pl.when(i == 0)
        def _():
            carry[...] = jnp.zeros_like(carry)

        vec, cvec = vec_ref[...], cvec_ref[...]
        a, sh, gt, gpost = _vec_rows(vec)
        xb = x_ref[...]
        xn, _ = _rms(xb)
        hb = (xn * a + sh).astype(BF16)
        h_ref[...] = hb
        for j in range(4):
            a_ref[:, j * hd:(j + 1) * hd] = _dot(hb, w1_ref[j]) + b1_ref[:, j * hd:(j + 1) * hd]
        u = a_ref[:, :d] * _sigmoid(a_ref[:, d:])
        carry[halo:, :] = u
        for r0 in range(0, ts, CONV_ROWS):
            for l0 in range(0, d, CONV_LANES):
                lanes = slice(l0, l0 + CONV_LANES)
                src = carry[r0:r0 + CONV_ROWS + halo, lanes]
                acc = jnp.zeros((CONV_ROWS, CONV_LANES), F32) + cvec[0:1, lanes]
                for k in range(kw):
                    acc = acc + wdw_ref[k:k + 1, lanes] * _shift_down(src, kw - 1 - k)[halo:]
                uc_ref[r0:r0 + CONV_ROWS, lanes] = acc
        carry[:halo, :] = u[ts - halo:]
        uc = uc_ref[...]
        mu = jnp.mean(uc, axis=-1, keepdims=True)
        cen = uc - mu
        rstd = lax.rsqrt(jnp.mean(cen * cen, axis=-1, keepdims=True) + EPS)
        l = cen * rstd * cvec[1:2] + cvec[2:3]
        zb = (l * _sigmoid(l)).astype(BF16)
        z_ref[...] = zb
        y = _dot(zb, w2_ref[...]) + cvec[3:4]
        y_ref[...] = y
        yn, _ = _rms(y)
        o_ref[...] = xb + gt * (yn * gpost)

    row = lambda i: (i, 0)
    const2 = lambda i: (0, 0)
    return pl.pallas_call(
        body, name="conv_forward", grid=(s // ts,),
        in_specs=[pl.BlockSpec((ts, d), row), pl.BlockSpec((8, d), const2), pl.BlockSpec((8, d), const2),
                  pl.BlockSpec(w_pw1.shape, lambda i: (0, 0, 0)), pl.BlockSpec(b_pw1.shape, const2),
                  pl.BlockSpec(w_dw.shape, const2), pl.BlockSpec(w_pw2.shape, const2)],
        out_specs=(pl.BlockSpec((ts, d), row), pl.BlockSpec((ts, d), row), pl.BlockSpec((ts, 2 * d), row),
                   pl.BlockSpec((ts, d), row), pl.BlockSpec((ts, d), row), pl.BlockSpec((ts, d), row)),
        out_shape=(jax.ShapeDtypeStruct((s, d), F32), jax.ShapeDtypeStruct((s, d), BF16),
                   jax.ShapeDtypeStruct((s, 2 * d), F32), jax.ShapeDtypeStruct((s, d), F32),
                   jax.ShapeDtypeStruct((s, d), BF16), jax.ShapeDtypeStruct((s, d), F32)),
        scratch_shapes=[pltpu.VMEM((halo + ts, d), F32)],
        compiler_params=_cparams(("arbitrary",)),
    )(x, vec, cvec, w_pw1, b_pw1, w_dw, w_pw2)


def _conv_backward(dout, x, y, a_pre, uc, vec, cvec, w_pw1, w_dw, w_pw2):
    s, d = x.shape
    kw = w_dw.shape[0]
    kpad = -(-kw // 8) * 8
    halo = _conv_halo(kw)
    ts = _row_tile(s, 512)
    nb = s // ts
    hb = ts // halo
    hd = d // 2

    def body(do_ref, x_ref, y_ref, a_ref, ah_ref, uc_ref, vec_ref, cvec_ref, w1_ref, wdw_ref, w2_ref,
             dx_ref, da_ref, dy_ref, sum_ref, dwdw_ref, carry):
        step = pl.program_id(0)
        i = nb - 1 - step

        @pl.when(step == 0)
        def _():
            carry[...] = jnp.zeros_like(carry)
            sum_ref[...] = jnp.zeros_like(sum_ref)
            dwdw_ref[...] = jnp.zeros_like(dwdw_ref)

        vec, cvec = vec_ref[...], cvec_ref[...]
        a, sh, gt, gpost = _vec_rows(vec)
        do = do_ref[...]
        yn, ry = _rms(y_ref[...])
        dy = _rms_bwd(do * (gt * gpost), yn, ry)
        dyb = dy.astype(BF16)
        dy_ref[...] = dyb
        dz = _dot_nt(dyb, w2_ref[...])
        uc = uc_ref[...]
        mu = jnp.mean(uc, axis=-1, keepdims=True)
        cen = uc - mu
        rstd = lax.rsqrt(jnp.mean(cen * cen, axis=-1, keepdims=True) + EPS)
        lhat = cen * rstd
        l = lhat * cvec[1:2] + cvec[2:3]
        sgl = _sigmoid(l)
        dl = dz * (sgl * (1.0 + l * (1.0 - sgl)))
        dlhat = dl * cvec[1:2]
        duc = rstd * (dlhat - jnp.mean(dlhat, axis=-1, keepdims=True)
                      - lhat * jnp.mean(dlhat * lhat, axis=-1, keepdims=True))
        ae = jnp.concatenate([ah_ref[...] * (i > 0).astype(F32), a_ref[...]], axis=0)
        sgate = _sigmoid(ae[:, d:])
        val = ae[:, :d]
        ue = val * sgate
        rowid = lax.broadcasted_iota(jnp.int32, (halo + ts, 1), 0)
        ue = jnp.where((rowid >= halo) | (i > 0), ue, 0.0)
        duce = jnp.concatenate([duc, carry[...]], axis=0)
        carry[...] = duc[:halo]
        du = jnp.zeros((ts, d), F32)
        for k in range(kw):
            du = du + wdw_ref[k:k + 1, :] * _shift_down(duce, halo - (kw - 1 - k))[halo:]
            dwdw_ref[k:k + 1, :] += _colsum(duc * _shift_down(ue, kw - 1 - k)[halo:])
        sg, vl = sgate[halo:], val[halo:]
        dval = du * sg
        dgate = du * vl * (sg * (1.0 - sg))
        dvb, dgb = dval.astype(BF16), dgate.astype(BF16)
        dh = jnp.zeros((ts, d), F32)
        for j in range(2):
            da_ref[j] = dvb[:, j * hd:(j + 1) * hd]
            da_ref[j + 2] = dgb[:, j * hd:(j + 1) * hd]
            dh = dh + _dot_nt(dvb[:, j * hd:(j + 1) * hd], w1_ref[j]) + _dot_nt(dgb[:, j * hd:(j + 1) * hd], w1_ref[j + 2])
        xn, r = _rms(x_ref[...])
        dx_ref[...] = do + _rms_bwd(dh * a, xn, r)
        _add_rows(sum_ref, _norm_sums(do, yn, dh, xn, vec) + [_colsum(dh), _colsum(dy), _colsum(dl * lhat), _colsum(dl),
                            _colsum(duc), _colsum(dval), _colsum(dgate)])

    blk = lambda st: (nb - 1 - st, 0)
    const2 = lambda st: (0, 0)
    return pl.pallas_call(
        body, name="conv_backward", grid=(nb,),
        in_specs=[pl.BlockSpec((ts, d), blk), pl.BlockSpec((ts, d), blk), pl.BlockSpec((ts, d), blk),
                  pl.BlockSpec((ts, 2 * d), blk),
                  pl.BlockSpec((halo, 2 * d), lambda st: (jnp.maximum((nb - 1 - st) * hb - 1, 0), 0)),
                  pl.BlockSpec((ts, d), blk), pl.BlockSpec((8, d), const2), pl.BlockSpec((8, d), const2),
                  pl.BlockSpec(w_pw1.shape, lambda st: (0, 0, 0)), pl.BlockSpec(w_dw.shape, const2),
                  pl.BlockSpec(w_pw2.shape, const2)],
        out_specs=(pl.BlockSpec((ts, d), blk), pl.BlockSpec((4, ts, hd), lambda st: (0, nb - 1 - st, 0)),
                   pl.BlockSpec((ts, d), blk), pl.BlockSpec((16, d), const2), pl.BlockSpec((kpad, d), const2)),
        out_shape=(jax.ShapeDtypeStruct((s, d), F32), jax.ShapeDtypeStruct((4, s, hd), BF16),
                   jax.ShapeDtypeStruct((s, d), BF16), jax.ShapeDtypeStruct((16, d), F32),
                   jax.ShapeDtypeStruct((kpad, d), F32)),
        scratch_shapes=[pltpu.VMEM((halo, d), F32)],
        compiler_params=_cparams(("arbitrary",)),
    )(dout, x, y, a_pre, a_pre, uc, vec, cvec, w_pw1, w_dw, w_pw2)


def _weight_grad(a, b, comm=None):
    na, s, k = a.shape
    nb_, _, n = b.shape
    nj = max(na, nb_)
    ts = _row_tile(s, 2048)
    nt = s // ts
    comm = comm or _Comm([])
    nc = len(comm.arrays)

    def body(*refs):
        a_ref, b_ref = refs[:2]
        cin = refs[2:2 + nc]
        o_ref = refs[2 + nc]
        cout = refs[3 + nc:3 + 2 * nc]
        sems = refs[3 + 2 * nc:]
        j, t = pl.program_id(0), pl.program_id(1)

        if nc:
            @pl.when((j == 0) & (t == 0))
            def _():
                comm.run(0, cin, cout, *sems)

            @pl.when((j == nj // 2) & (t == nt // 2))
            def _():
                comm.run(1, cin, cout, *sems)

        @pl.when(t == 0)
        def _():
            o_ref[...] = jnp.zeros_like(o_ref)

        o_ref[0] += _dot_tn(a_ref[0], b_ref[0])

        if nc:
            @pl.when((j == nj - 1) & (t == nt - 1))
            def _():
                comm.run(2, cin, cout, *sems)

    res = pl.pallas_call(
        body, name="weight_grad", grid=(nj, nt),
        in_specs=[pl.BlockSpec((1, ts, k), (lambda j, t: (j, t, 0)) if na > 1 else (lambda j, t: (0, t, 0))),
                  pl.BlockSpec((1, ts, n), (lambda j, t: (j, t, 0)) if nb_ > 1 else (lambda j, t: (0, t, 0)))]
        + comm.specs(),
        out_specs=(pl.BlockSpec((1, k, n), lambda j, t: (j, 0, 0)), *comm.specs()),
        out_shape=(jax.ShapeDtypeStruct((nj, k, n), F32), *comm.outs),
        input_output_aliases=comm.aliases(2, 1),
        scratch_shapes=comm.scratch() if nc else [],
        compiler_params=_cparams(("arbitrary", "arbitrary") if nc else ("parallel", "arbitrary")),
    )(a, b, *comm.arrays)
    return res[0], comm.split(res[1:])


def _adamw_math(w, g, m, v):
    nm = ADAM_B1 * m + (1.0 - ADAM_B1) * g
    nv = ADAM_B2 * v + (1.0 - ADAM_B2) * (g * g)
    m_hat = nm * (1.0 / (1.0 - ADAM_B1 ** ADAM_STEP))
    v_hat = nv * (1.0 / (1.0 - ADAM_B2 ** ADAM_STEP))
    return -ADAM_LR * (m_hat / (jnp.sqrt(v_hat) + ADAM_EPS) + ADAM_WD * w), nm, nv


def _adamw_many(params):
    n = len(params)

    def body(*refs):
        for k in range(n):
            w_ref, g_ref, m_ref, v_ref = refs[4 * k:4 * k + 4]
            outs = refs[4 * n + 3 * k:4 * n + 3 * k + 3]
            for o_ref, val in zip(outs, _adamw_math(w_ref[...], g_ref[...], m_ref[...], v_ref[...])):
                o_ref[...] = val

    vm = pl.BlockSpec(memory_space=pltpu.VMEM)
    res = pl.pallas_call(
        body, name="adamw_many", in_specs=[vm] * (4 * n), out_specs=tuple([vm] * (3 * n)),
        out_shape=tuple(jax.ShapeDtypeStruct(p[0].shape, F32) for p in params for _ in range(3)),
        compiler_params=_cparams(),
    )(*[a for p in params for a in p])
    return [res[3 * k:3 * k + 3] for k in range(n)]


def _adamw(w, g, m, v, comm=None):
    nl, r, c = w.shape
    tr = _row_tile(r, 256)
    nr = r // tr
    comm = comm or _Comm([])
    nc = len(comm.arrays)

    def body(*refs):
        w_ref, g_ref, m_ref, v_ref = refs[:4]
        cin = refs[4:4 + nc]
        d_ref, nm_ref, nv_ref = refs[4 + nc:7 + nc]
        cout = refs[7 + nc:7 + 2 * nc]
        sems = refs[7 + 2 * nc:]
        l, i = pl.program_id(0), pl.program_id(1)
        if nc:
            @pl.when((l == 0) & (i == 0))
            def _():
                comm.run(0, cin, cout, *sems)

            @pl.when((l == nl // 2) & (i == nr // 2))
            def _():
                comm.run(1, cin, cout, *sems)

        d_ref[...], nm_ref[...], nv_ref[...] = _adamw_math(w_ref[...], g_ref[...], m_ref[...], v_ref[...])
        if nc:
            @pl.when((l == nl - 1) & (i == nr - 1))
            def _():
                comm.run(2, cin, cout, *sems)

    spec = pl.BlockSpec((1, tr, c), lambda l, i: (l, i, 0))
    shp = jax.ShapeDtypeStruct((nl, r, c), F32)
    res = pl.pallas_call(
        body, name="adamw", grid=(nl, nr), in_specs=[spec] * 4 + comm.specs(),
        out_specs=(spec,) * 3 + tuple(comm.specs()), out_shape=(shp,) * 3 + tuple(comm.outs),
        input_output_aliases=comm.aliases(4, 3), scratch_shapes=comm.scratch() if nc else [],
        compiler_params=_cparams(("arbitrary", "arbitrary") if nc else ("parallel", "parallel")),
    )(w, g, m, v, *comm.arrays)
    return res[:3], comm.split(res[3:])


def _add_my_half(g, other, idx):
    _, _, h, c = g.shape
    th = _row_tile(h, 256)

    def body(idx_ref, g_ref, o_ref, out_ref):
        out_ref[...] = (g_ref[:, 0] + o_ref[...]).astype(BF16)

    return pl.pallas_call(
        body, name="add_my_half",
        grid_spec=pltpu.PrefetchScalarGridSpec(
            num_scalar_prefetch=1, grid=(4, h // th),
            in_specs=[pl.BlockSpec((1, 1, th, c), lambda j, i, idx_ref: (j, idx_ref[1], i, 0)),
                      pl.BlockSpec((1, th, c), lambda j, i, idx_ref: (j, i, 0))],
            out_specs=pl.BlockSpec((1, th, c), lambda j, i, idx_ref: (j, i, 0))),
        out_shape=jax.ShapeDtypeStruct(other.shape, BF16),
        compiler_params=_cparams(("parallel", "parallel")),
    )(idx, g, other)


def _sum_for_my_chip(g, other, got, idx):
    _, _, h, c = g.shape
    th = _row_tile(h, 256)

    def body(idx_ref, g_ref, o_ref, q_ref, out_ref):
        out_ref[0] = (((g_ref[0, 0] + o_ref[0]) + q_ref[0].astype(F32)) + q_ref[1].astype(F32)) + q_ref[2].astype(F32)

    return pl.pallas_call(
        body, name="sum_for_my_chip",
        grid_spec=pltpu.PrefetchScalarGridSpec(
            num_scalar_prefetch=1, grid=(h // th,),
            in_specs=[pl.BlockSpec((1, 1, th, c), lambda i, idx_ref: (idx_ref[0], idx_ref[1], i, 0)),
                      pl.BlockSpec((1, th, c), lambda i, idx_ref: (idx_ref[0], i, 0)),
                      pl.BlockSpec((3, th, c), lambda i, idx_ref: (0, i, 0))],
            out_specs=pl.BlockSpec((1, th, c), lambda i, idx_ref: (idx_ref[1], i, 0))),
        out_shape=jax.ShapeDtypeStruct((2, h, c), F32),
        compiler_params=_cparams(("parallel",)),
    )(idx, g, other, got)


class _Reducer:
    def __init__(self, idx):
        self.idx = idx
        self.groups = []

    def add(self, grads):
        group = {"state": 0, "g": [g.reshape(4, 2, g.shape[1] // 2, g.shape[2]) for g in grads]}
        self.groups.append(group)
        return group

    def steps(self):
        ops, owners = [], []
        for gr in self.groups:
            if gr["state"] == 0:
                ops.append(_Swap(gr["g"]))
            elif gr["state"] == 1:
                ops.append(_Exchange(gr["parts"]))
            elif gr["state"] == 2:
                ops.append(_Join(gr["bufs"]))
            else:
                continue
            owners.append(gr)
        return ops, owners

    def absorb(self, owners, results):
        for gr, res in zip(owners, results):
            if gr["state"] == 0:
                gr["other"] = res
                gr["parts"] = [_add_my_half(g, o, self.idx) for g, o in zip(gr["g"], res)]
            elif gr["state"] == 1:
                gr["bufs"] = [_sum_for_my_chip(g, o, q, self.idx) for g, o, q in zip(gr["g"], gr["other"], res)]
            else:
                gr["full"] = [b.reshape(2 * b.shape[1], b.shape[2]) for b in res]
            gr["state"] += 1

    def drain(self):
        while any(gr["state"] < 3 for gr in self.groups):
            ops, owners = self.steps()
            self.absorb(owners, _communicate(ops))


class _GatherRows:
    def __init__(self, bufs):
        self.arrays = list(bufs)
        self.outs = [jax.ShapeDtypeStruct(b.shape, b.dtype) for b in bufs]
        self.aliased = True
        self.n_sems = 7 * len(bufs)

    def run(self, phase, ins, outs, send_sems, recv_sems, base):
        x, y, c, chips = _place()
        me, sibling = (x, y, c), (x, y, 1 - c)
        for k, buf in enumerate(outs):
            def copy(i, block_of, to):
                blk = buf.at[4 * block_of[0] + 2 * block_of[1] + block_of[2]]
                return _remote(blk, blk, send_sems.at[base + 7 * k + i], recv_sems.at[base + 7 * k + i], to)

            if phase == 0:
                copy(0, me, sibling).start()
            for r, (px, py) in enumerate(chips):
                if phase == 0:
                    copy(1 + r, me, (px, py, c)).start()
                elif phase == 1:
                    copy(1 + r, (px, py, c), me).wait_recv()
                    copy(4 + r, (px, py, c), sibling).start()
                else:
                    copy(4 + r, (px, py, 1 - c), me).wait_recv()
                    copy(1 + r, me, (px, py, c)).wait_send()
                    copy(4 + r, (px, py, c), sibling).wait_send()
            if phase == 2:
                copy(0, sibling, me).wait_recv()
                copy(0, me, sibling).wait_send()


def _sum_devices(gathered):
    nd, m, n = gathered.shape

    def body(g_ref, o_ref):
        acc = g_ref[0]
        for b in range(1, nd):
            acc = acc + g_ref[b]
        o_ref[...] = acc

    return pl.pallas_call(
        body, name="sum_devices", out_shape=jax.ShapeDtypeStruct((m, n), F32),
        in_specs=[pl.BlockSpec(memory_space=pltpu.VMEM)], out_specs=pl.BlockSpec(memory_space=pltpu.VMEM),
        compiler_params=_cparams(),
    )(gathered)


def _ada_weight_grad(c_all, dmod_cols):
    nl, nd, ncol = dmod_cols.shape
    d = c_all.shape[1]

    def body(c_ref, dm_ref, o_ref):
        o_ref[0] = lax.dot_general(c_ref[...], dm_ref[0], (((0,), (0,)), ((), ())),
                                   preferred_element_type=F32, precision=lax.Precision.HIGHEST)

    return pl.pallas_call(
        body, name="ada_weight_grad", grid=(nl,),
        in_specs=[pl.BlockSpec((nd, d), lambda l: (0, 0)), pl.BlockSpec((1, nd, ncol), lambda l: (l, 0, 0))],
        out_specs=pl.BlockSpec((1, d, ncol), lambda l: (l, 0, 0)),
        out_shape=jax.ShapeDtypeStruct((nl, d, ncol), F32), compiler_params=_cparams(("parallel",)),
    )(c_all, dmod_cols)


def _pad_rows(a, rows):
    return jnp.pad(a, ((0, rows - a.shape[0]), (0, 0)))


def _shard_cols(full, chip, width):
    return lax.dynamic_slice_in_dim(full, chip * width, width, axis=full.ndim - 1)


def kernel(x, c, ada_w, ada_b, pre_g, post_g, pool_w, pool_scale, cv_w_pw1, cv_b_pw1, cv_w_dw, cv_b_dw, cv_ln_g, cv_ln_b, cv_w_pw2, cv_b_pw2, ffn_w_up, ffn_w_dw, ffn_w_down, loss_target, m_ada_w, m_ada_b, m_pre_g, m_post_g, m_pool_w, m_pool_scale, m_cv_w_pw1, m_cv_b_pw1, m_cv_w_dw, m_cv_b_dw, m_cv_ln_g, m_cv_ln_b, m_cv_w_pw2, m_cv_b_pw2, m_ffn_w_up, m_ffn_w_dw, m_ffn_w_down, v_ada_w, v_ada_b, v_pre_g, v_post_g, v_pool_w, v_pool_scale, v_cv_w_pw1, v_cv_b_pw1, v_cv_w_dw, v_cv_b_dw, v_cv_ln_g, v_cv_ln_b, v_cv_w_pw2, v_cv_b_pw2, v_ffn_w_up, v_ffn_w_dw, v_ffn_w_down):
    s, d = x.shape[1], x.shape[2]
    dq = d // N_CHIPS
    n_g = pool_w.shape[1]
    gq = pool_w.shape[2]
    gd = pool_w.shape[3]
    kw = cv_w_dw.shape[1]
    cs = ffn_w_up.shape[2]
    fq = ffn_w_down.shape[1]
    chip = 2 * lax.axis_index("x") + lax.axis_index("y")
    core = lax.axis_index("c")
    chip1 = jnp.reshape(chip, (1,)).astype(jnp.int32)
    core1 = jnp.reshape(core, (1,)).astype(jnp.int32)
    xs, tgt = x[0], loss_target[0]

    small_rows = [pre_g.reshape(4, dq), post_g.reshape(4, dq), cv_w_dw[0], cv_b_dw, cv_ln_g, cv_ln_b, cv_b_pw2,
                  cv_b_pw1.reshape(2, dq)]
    small = jnp.concatenate(small_rows, axis=0)
    n_small = small.shape[0]
    small = _pad_rows(small, -(-n_small // 16) * 16)
    dwf = _pad_rows(ffn_w_dw.reshape(6, cs), 16)
    first = _AllGather([_cast_into_slot(pool_w.reshape(n_g * gq, gd), chip1), _cast_into_slot(small, chip1, dtype=F32),
                        _cast_into_slot(dwf, chip1, dtype=F32), _cast_into_slot(ffn_w_up[0], chip1)])
    (c_rep, mod_rep), ((g_pool, g_small, g_dwf, g_up0),) = _ada_forward(c, ada_w, _Comm([first]))
    c_all = c_rep[:, 0, :]
    mod = mod_rep[:, :, 0, :].transpose(1, 0, 2).reshape(ada_b.shape) + ada_b
    second = _AllGather([_cast_into_slot(ffn_w_down[0], chip1)])
    later = _AllGather([_cast_into_slot(cv_w_pw1[0], chip1), _cast_into_slot(cv_w_pw2[0], chip1),
                        _cast_into_slot(ffn_w_up[1], chip1), _cast_into_slot(ffn_w_down[1], chip1)])
    poolw_full = g_pool.reshape(N_CHIPS, n_g, gq, gd).transpose(1, 0, 2, 3).reshape(n_g, gd, gd)
    smallf = g_small.transpose(1, 0, 2).reshape(g_small.shape[1], d)
    pre_full, post_full = smallf[0:4].reshape(2, 2, d), smallf[4:8].reshape(2, 2, d)
    wdw31 = smallf[8:8 + kw]
    o = 8 + kw
    b_dw, ln_g, ln_b, b_pw2 = smallf[o:o + 1], smallf[o + 1:o + 2], smallf[o + 2:o + 3], smallf[o + 3:o + 4]
    b_pw1 = g_small[:, o + 4:o + 6, :].reshape(1, 2 * d)
    ffn_dw = g_dwf[:, :6, :].transpose(1, 0, 2).reshape(2, 3, N_CHIPS * cs)

    def sub_vec(layer, sub, extra=None):
        m6 = mod[layer].reshape(6, d)
        rows = [pre_full[layer, sub][None], 1.0 + m6[3 * sub + 1][None], m6[3 * sub][None], m6[3 * sub + 2][None],
                post_full[layer, sub][None]]
        if extra is not None:
            rows.append(extra)
        return _pad_rows(jnp.concatenate(rows, axis=0), 8)

    vec_pool = sub_vec(0, 0, pool_scale)
    vec_f0, vec_conv, vec_f1 = sub_vec(0, 1), sub_vec(1, 0), sub_vec(1, 1)
    cvec = _pad_rows(jnp.concatenate([b_dw, ln_g, ln_b, b_pw2], axis=0), 8)

    x1, ((g_dn0,),) = _pool_forward(xs, vec_pool, poolw_full, _Comm([second]))
    w_up0, w_dn0 = g_up0, g_dn0.reshape(2, 2 * fq, d)
    (x2, h_f0, a0_f0, cc_f0, u_f0, y_f0), ((g_pw1, g_pw2, g_up1, g_dn1),) = _ffn_forward(
        x1, vec_f0, w_up0, ffn_dw[0], w_dn0, _Comm([later]))
    pw2_full = g_pw2.reshape(d, d)
    w_up1, w_dn1 = g_up1, g_dn1.reshape(2, 2 * fq, d)
    x3, h_cv, a_cv, uc_cv, z_cv, y_cv = _conv_forward(x2, vec_conv, cvec, g_pw1, b_pw1, wdw31, pw2_full)
    (dx4, h_f1, a0_f1, cc_f1, u_f1, y_f1, loss_rows), _ = _ffn_forward(x3, vec_f1, w_up1, ffn_dw[1], w_dn1, target=tgt)

    dx3, da0_f1, dy_f1, sum_f1, dwdw_f1 = _ffn_backward(dx4, x3, y_f1, a0_f1, cc_f1, vec_f1, w_up1, ffn_dw[1], w_dn1)
    dx2, da_cv, dy_cv, sum_cv, dwdw_cv = _conv_backward(dx3, x2, y_cv, a_cv, uc_cv, vec_conv, cvec, g_pw1, wdw31, pw2_full)
    dx1, da0_f0, dy_f0, sum_f0, dwdw_f0 = _ffn_backward(dx2, x1, y_f0, a0_f0, cc_f0, vec_f0, w_up0, ffn_dw[0], w_dn0)
    dx0, sum_pool, gw_pool = _pool_backward(dx1, xs, vec_pool, poolw_full)
    gw_pool4 = gw_pool.reshape(n_g, N_CHIPS, gq, gd).transpose(1, 0, 2, 3).reshape(N_CHIPS, n_g * gq, gd)

    slab = jnp.concatenate([sum_f1, sum_cv, dwdw_cv, sum_f0, sum_pool, loss_rows], axis=0)
    wide = jnp.concatenate([dwdw_f1, dwdw_f0], axis=0)
    n_slab = slab.shape[0]
    mine = jnp.concatenate([slab, wide.reshape(-1, d)], axis=0)
    rows_of_all = _cast_into_slot(mine, 2 * chip1 + core1, N_DEV, F32)
    red = _Reducer(jnp.concatenate([chip1, core1]))

    def carried(call, *args, extra=()):
        ops, owners = red.steps()
        out, results = call(*args, _Comm(ops + list(extra)))
        red.absorb(owners, results[:len(ops)])
        return out, results[len(ops):]

    gw_up1, ((both_all,),) = carried(_weight_grad, h_f1[None], da0_f1, extra=[_GatherRows([rows_of_all])])
    r_up1 = red.add([gw_up1])
    r_up0 = red.add([carried(_weight_grad, h_f0[None], da0_f0)[0]])
    r_dn1 = red.add([carried(_weight_grad, u_f1, dy_f1[None])[0].reshape(N_CHIPS, fq, d)])
    r_dn0 = red.add([carried(_weight_grad, u_f0, dy_f0[None])[0].reshape(N_CHIPS, fq, d)])
    r_pw1 = red.add([carried(_weight_grad, h_cv[None], da_cv)[0]])
    r_last = red.add([carried(_weight_grad, z_cv[None], dy_cv[None])[0].reshape(N_CHIPS, dq, d), gw_pool4])

    tot_both = _sum_devices(both_all)
    slab_all, tot = both_all[:, :n_slab], tot_both[:n_slab]
    tot_wide = tot_both[n_slab:].reshape(wide.shape)
    kpad = dwdw_cv.shape[0]
    o_cv, o_dw, o_f0 = 8, 24, 24 + kpad
    o_pool, o_loss = o_f0 + 8, o_f0 + 16
    loss = jnp.sum(tot[o_loss])
    dmod_l0 = jnp.concatenate([slab_all[:, o_pool + 4], slab_all[:, o_pool + 3], slab_all[:, o_pool + 1],
                               slab_all[:, o_f0 + 4], slab_all[:, o_f0 + 3], slab_all[:, o_f0 + 1]], axis=-1)
    dmod_l1 = jnp.concatenate([slab_all[:, o_cv + 4], slab_all[:, o_cv + 3], slab_all[:, o_cv + 1],
                               slab_all[:, 4], slab_all[:, 3], slab_all[:, 1]], axis=-1)
    dmod = jnp.stack([dmod_l0, dmod_l1], axis=0)
    g_ada_b = _sum_devices(dmod.transpose(1, 0, 2))
    ncol = ada_w.shape[2]
    g_ada_w = _ada_weight_grad(c_all, _shard_cols(dmod, chip, ncol))

    g_pre = jnp.stack([jnp.stack([tot[o_pool + 2], tot[o_f0 + 2]]), jnp.stack([tot[o_cv + 2], tot[2]])])
    g_post = jnp.stack([jnp.stack([tot[o_pool + 0], tot[o_f0 + 0]]), jnp.stack([tot[o_cv + 0], tot[0]])])
    g_pool_scale = tot[o_pool + 5][None]
    g_b_pw2, g_ln_g, g_ln_b, g_b_dw = tot[o_cv + 5], tot[o_cv + 6], tot[o_cv + 7], tot[o_cv + 8]
    g_b_pw1 = jnp.concatenate([tot[o_cv + 9], tot[o_cv + 10]])
    g_w_dw31 = tot[o_dw:o_dw + kw]
    g_ffn_dw = jnp.stack([tot_wide[8:11], tot_wide[0:3]])

    grads_small = {
        "pre_g": _shard_cols(g_pre, chip, dq), "post_g": _shard_cols(g_post, chip, dq),
        "pool_scale": g_pool_scale, "cv_b_pw1": _shard_cols(g_b_pw1[None], chip, 2 * dq),
        "cv_w_dw": _shard_cols(g_w_dw31[None], chip, dq), "cv_b_dw": _shard_cols(g_b_dw[None], chip, dq),
        "cv_ln_g": _shard_cols(g_ln_g[None], chip, dq), "cv_ln_b": _shard_cols(g_ln_b[None], chip, dq),
        "cv_b_pw2": _shard_cols(g_b_pw2[None], chip, dq), "ffn_w_dw": _shard_cols(g_ffn_dw, chip, cs),
        "ada_b": g_ada_b,
    }
    params_small = {
        "pre_g": (pre_g, m_pre_g, v_pre_g), "post_g": (post_g, m_post_g, v_post_g),
        "pool_scale": (pool_scale, m_pool_scale, v_pool_scale), "cv_b_pw1": (cv_b_pw1, m_cv_b_pw1, v_cv_b_pw1),
        "cv_w_dw": (cv_w_dw, m_cv_w_dw, v_cv_w_dw), "cv_b_dw": (cv_b_dw, m_cv_b_dw, v_cv_b_dw),
        "cv_ln_g": (cv_ln_g, m_cv_ln_g, v_cv_ln_g), "cv_ln_b": (cv_ln_b, m_cv_ln_b, v_cv_ln_b),
        "cv_b_pw2": (cv_b_pw2, m_cv_b_pw2, v_cv_b_pw2), "ffn_w_dw": (ffn_w_dw, m_ffn_w_dw, v_ffn_w_dw),
        "ada_b": (ada_b, m_ada_b, v_ada_b),
    }
    names = list(params_small)
    small_g = {nm: grads_small[nm].reshape(params_small[nm][0].shape) for nm in names}
    updated = _adamw_many([(params_small[nm][0], small_g[nm], params_small[nm][1], params_small[nm][2])
                           for nm in names])
    small_d = {nm: u[0] for nm, u in zip(names, updated)}
    small_m = {nm: u[1] for nm, u in zip(names, updated)}
    small_v = {nm: u[2] for nm, u in zip(names, updated)}

    red.drain()
    big_p = {
        "ada_w": (ada_w, m_ada_w, v_ada_w), "pool_w": (pool_w, m_pool_w, v_pool_w),
        "cv_w_pw1": (cv_w_pw1, m_cv_w_pw1, v_cv_w_pw1), "cv_w_pw2": (cv_w_pw2, m_cv_w_pw2, v_cv_w_pw2),
        "ffn_w_up": (ffn_w_up, m_ffn_w_up, v_ffn_w_up), "ffn_w_down": (ffn_w_down, m_ffn_w_down, v_ffn_w_down),
    }
    big_g, big_d, big_m, big_v = {}, {}, {}, {}

    def update(nm, grad):
        w, m, v = big_p[nm]
        as3 = lambda t: t.reshape((-1,) + w.shape[-2:])
        (dl, nm_, nv_), _ = _adamw(as3(w), as3(grad), as3(m), as3(v))
        big_g[nm] = grad.reshape(w.shape)
        big_d[nm], big_m[nm], big_v[nm] = dl.reshape(w.shape), nm_.reshape(w.shape), nv_.reshape(w.shape)

    full = lambda group, k=0: group["full"][k]
    update("ffn_w_up", jnp.stack([full(r_up0), full(r_up1)]))
    update("ada_w", g_ada_w)
    update("ffn_w_down", jnp.stack([full(r_dn0), full(r_dn1)]))
    update("cv_w_pw1", full(r_pw1))
    update("cv_w_pw2", full(r_last, 0))
    update("pool_w", full(r_last, 1))

    order = ["ada_w", "ada_b", "pre_g", "post_g", "pool_w", "pool_scale", "cv_w_pw1", "cv_b_pw1", "cv_w_dw", "cv_b_dw",
             "cv_ln_g", "cv_ln_b", "cv_w_pw2", "cv_b_pw2", "ffn_w_up", "ffn_w_dw", "ffn_w_down"]
    pick = lambda bigs, smalls: [bigs[nm] if nm in bigs else smalls[nm] for nm in order]
    return (loss, dx0[None], *pick(big_g, small_g), *pick(big_d, small_d), *pick(big_m, small_m),
            *pick(big_v, small_v))
```

```python
import functools

import jax
import jax.numpy as jnp
from jax import lax
from jax.experimental import pallas as pl
from jax.experimental.pallas import tpu as pltpu

F32 = jnp.float32
BF16 = jnp.bfloat16
EPS = 1e-6
N_CHIPS = 4
N_DEV = 8
POOL_WINDOWS = (2, 4, 8, 16)
POOL_HALO = 16
FFN_HALO = 16
MXU_LANES = 256
CONV_ROWS, CONV_LANES = 128, 128
ADAM_LR = 0.001
ADAM_B1 = 0.9
ADAM_B2 = 0.999
ADAM_EPS = 1e-08
ADAM_WD = 0.01
ADAM_STEP = 10
V7X_VMEM_LIMIT = 58 * 1024 * 1024
MESH = pl.DeviceIdType.MESH


def _cparams(sem=None, vmem=V7X_VMEM_LIMIT):
    return pltpu.CompilerParams(dimension_semantics=sem, vmem_limit_bytes=vmem)


def _row_tile(n, want):
    if n <= want:
        return n
    t = want - want % 8
    while n % t:
        t -= 8
    return t


def _lane_chunks(width):
    out, c = [], 0
    while c < width:
        w = min(512, width - c)
        out.append((c, w))
        c += w
    return out


def _dot(a, b):
    return jnp.dot(a, b, preferred_element_type=F32)


def _dot_nt(a, b):
    return lax.dot_general(a, b, (((1,), (1,)), ((), ())), preferred_element_type=F32)


def _store_dot_nt(dst, a_ref, b_ref):
    dst[...] = _dot_nt(a_ref[...], b_ref[...])


def _store_dot_nt2(dst, a1_ref, a2_ref, b1_ref, b2_ref):
    dst[...] = _dot_nt(a1_ref[...], b1_ref[...]) + _dot_nt(a2_ref[...], b2_ref[...])


def _dot_tn(a, b):
    return lax.dot_general(a, b, (((0,), (0,)), ((), ())), preferred_element_type=F32)


def _rms(x):
    r = lax.rsqrt(jnp.mean(x * x, axis=-1, keepdims=True) + EPS)
    return x * r, r


def _rms_bwd(dyn, yn, r):
    return r * (dyn - yn * jnp.mean(dyn * yn, axis=-1, keepdims=True))


def _sigmoid(x):
    return 0.5 * jnp.tanh(0.5 * x) + 0.5


def _colsum(x):
    return jnp.sum(x, axis=0, keepdims=True)


def _shift_down(x, k):
    return x if k == 0 else pltpu.roll(x, k, 0)


def _shift_up(x, k):
    return x if k == 0 else pltpu.roll(x, x.shape[0] - k, 0)


def _vec_rows(vec):
    return vec[0:1] * vec[1:2], vec[2:3], vec[3:4], vec[4:5]


def _norm_sums(do, yn, dh, xn, vec):
    p, q = _colsum(do * yn), _colsum(dh * xn)
    return [p * vec[3:4], p * vec[4:5], q * vec[1:2], q * vec[0:1]]


def _add_rows(sum_ref, rows):
    for k, r in enumerate(rows):
        sum_ref[k:k + 1, :] += r


def _ada_forward(c, ada_w, comm=None):
    n_layers, d, ncol = ada_w.shape
    comm = comm or _Comm([])
    nc = len(comm.arrays)

    def body(*refs):
        c_ref, w_ref = refs[:2]
        cin = refs[2:2 + nc]
        call_ref, mod_ref = refs[2 + nc:4 + nc]
        cout = refs[4 + nc:4 + 2 * nc]
        part_ref, sendbuf, send_sems, recv_sems, send2, recv2 = refs[4 + 2 * nc:10 + 2 * nc]
        carried_sems = refs[10 + 2 * nc:]
        if nc:
            comm.run(0, cin, cout, *carried_sems)
        x, y, cc = lax.axis_index("x"), lax.axis_index("y"), lax.axis_index("c")
        me = 4 * x + 2 * y + cc
        rel = [(x, y, 1 - cc), (1 - x, y, cc), (x, 1 - y, cc), (1 - x, 1 - y, cc),
               (1 - x, y, 1 - cc), (x, 1 - y, 1 - cc), (1 - x, 1 - y, 1 - cc)]
        cv = c_ref[...]
        call_ref[me] = jnp.broadcast_to(cv * _sigmoid(cv), (8, d))

        def gather(k, block, to):
            blk = call_ref.at[block]
            return pltpu.make_async_remote_copy(src_ref=blk, dst_ref=blk, send_sem=send_sems.at[k],
                                                recv_sem=recv_sems.at[k], device_id=to, device_id_type=MESH)

        for k, to in enumerate(rel):
            gather(k, me, to).start()
        for k, (px, py, pc) in enumerate(rel):
            gather(k, 4 * px + 2 * py + pc, rel[k]).wait_recv()
        for k, to in enumerate(rel):
            gather(k, me, to).wait_send()

        ca = call_ref[...].reshape(8 * N_DEV, d)
        for l in range(n_layers):
            part_ref[l] = jnp.dot(ca, w_ref[l], preferred_element_type=F32, precision=lax.Precision.HIGHEST)

        j = 2 * x + y
        chips = [(1 - x, y), (x, 1 - y), (1 - x, 1 - y)]

        def rows_of(b):
            return part_ref[:, pl.ds(pl.multiple_of(8 * b, 8), 8), :]

        def scatter(k, src_j, to):
            return pltpu.make_async_remote_copy(
                src_ref=sendbuf.at[k], dst_ref=mod_ref.at[src_j], send_sem=send2.at[k], recv_sem=recv2.at[k],
                device_id=to, device_id_type=MESH)

        mod_ref[j] = rows_of(me)
        for k, (px, py) in enumerate(chips):
            sendbuf[k] = rows_of(4 * px + 2 * py + cc)
            scatter(k, j, (px, py, cc)).start()
        for k, (px, py) in enumerate(chips):
            scatter(k, 2 * px + py, (px, py, cc)).wait_recv()
        for k, (px, py) in enumerate(chips):
            scatter(k, j, (px, py, cc)).wait_send()
        if nc:
            comm.run(1, cin, cout, *carried_sems)
            comm.run(2, cin, cout, *carried_sems)

    vm = pl.BlockSpec(memory_space=pltpu.VMEM)
    res = pl.pallas_call(
        body, name="ada_forward",
        out_shape=(jax.ShapeDtypeStruct((N_DEV, 8, d), F32), jax.ShapeDtypeStruct((N_CHIPS, n_layers, 8, ncol), F32),
                   *comm.outs),
        in_specs=[vm, vm] + comm.specs(), out_specs=(vm, vm, *comm.specs()),
        input_output_aliases=comm.aliases(2, 2),
        scratch_shapes=[pltpu.VMEM((n_layers, 8 * N_DEV, ncol), F32), pltpu.VMEM((3, n_layers, 8, ncol), F32),
                        pltpu.SemaphoreType.DMA((7,)), pltpu.SemaphoreType.DMA((7,)),
                        pltpu.SemaphoreType.DMA((3,)), pltpu.SemaphoreType.DMA((3,))] + (comm.scratch() if nc else []),
        compiler_params=_cparams(),
    )(c, ada_w, *comm.arrays)
    return res[:2], comm.split(res[2:])


def _cast_into_slot(w2d, slot, n_slots=N_CHIPS, dtype=None):
    r, c = w2d.shape
    tr = _row_tile(r, 256)
    dtype = dtype or BF16

    def body(slot_ref, w_ref, o_ref):
        o_ref[0] = w_ref[...].astype(dtype)

    return pl.pallas_call(
        body, name="cast_into_slot",
        grid_spec=pltpu.PrefetchScalarGridSpec(
            num_scalar_prefetch=1, grid=(r // tr,),
            in_specs=[pl.BlockSpec((tr, c), lambda i, slot_ref: (i, 0))],
            out_specs=pl.BlockSpec((1, tr, c), lambda i, slot_ref: (slot_ref[0], i, 0))),
        out_shape=jax.ShapeDtypeStruct((n_slots, r, c), dtype), compiler_params=_cparams(("parallel",)),
    )(slot, w2d)


def _place():
    x, y, c = lax.axis_index("x"), lax.axis_index("y"), lax.axis_index("c")
    return x, y, c, [(1 - x, y), (x, 1 - y), (1 - x, 1 - y)]


def _remote(src, dst, send_sem, recv_sem, to):
    return pltpu.make_async_remote_copy(src_ref=src, dst_ref=dst, send_sem=send_sem, recv_sem=recv_sem,
                                        device_id=to, device_id_type=MESH)


class _AllGather:
    def __init__(self, bufs):
        self.arrays = list(bufs)
        self.outs = [jax.ShapeDtypeStruct(b.shape, b.dtype) for b in bufs]
        self.aliased = True
        self.n_sems = 6 * len(bufs)

    def run(self, phase, ins, outs, send_sems, recv_sems, base):
        x, y, c, chips = _place()
        j = 2 * x + y
        for k, buf in enumerate(outs):
            half = buf.shape[1] // 2

            def part(src_j, h):
                return buf.at[src_j, pl.ds(h * half, half), :]

            def ici(r, src_j, to):
                s = base + 6 * k + r
                return _remote(part(src_j, c), part(src_j, c), send_sems.at[s], recv_sems.at[s], to)

            def d2d(r, src_j, h):
                s = base + 6 * k + 3 + r
                return _remote(part(src_j, h), part(src_j, h), send_sems.at[s], recv_sems.at[s], (x, y, 1 - c))

            for r, (px, py) in enumerate(chips):
                if phase == 0:
                    ici(r, j, (px, py, c)).start()
                elif phase == 1:
                    ici(r, 2 * px + py, (px, py, c)).wait_recv()
                    d2d(r, 2 * px + py, c).start()
                else:
                    d2d(r, 2 * px + py, 1 - c).wait_recv()
                    ici(r, j, (px, py, c)).wait_send()
                    d2d(r, 2 * px + py, c).wait_send()


class _Swap:
    def __init__(self, grads):
        self.arrays = list(grads)
        self.outs = [jax.ShapeDtypeStruct((g.shape[0],) + g.shape[2:], g.dtype) for g in grads]
        self.aliased = False
        self.n_sems = len(grads)

    def run(self, phase, ins, outs, send_sems, recv_sems, base):
        x, y, c, _ = _place()
        for k in range(len(ins)):
            cp = _remote(ins[k].at[:, 1 - c], outs[k], send_sems.at[base + k], recv_sems.at[base + k], (x, y, 1 - c))
            if phase == 0:
                cp.start()
            elif phase == 2:
                cp.wait()


class _Exchange:
    def __init__(self, parts):
        self.arrays = list(parts)
        self.outs = [jax.ShapeDtypeStruct((3,) + p.shape[1:], p.dtype) for p in parts]
        self.aliased = False
        self.n_sems = 3 * len(parts)

    def run(self, phase, ins, outs, send_sems, recv_sems, base):
        x, y, c, chips = _place()
        for k in range(len(ins)):
            for r, (px, py) in enumerate(chips):
                s = base + 3 * k + r
                cp = _remote(ins[k].at[2 * px + py], outs[k].at[r], send_sems.at[s], recv_sems.at[s], (px, py, c))
                if phase == 0:
                    cp.start()
                elif phase == 2:
                    cp.wait()


class _Join:
    def __init__(self, bufs):
        self.arrays = list(bufs)
        self.outs = [jax.ShapeDtypeStruct(b.shape, b.dtype) for b in bufs]
        self.aliased = True
        self.n_sems = len(bufs)

    def run(self, phase, ins, outs, send_sems, recv_sems, base):
        x, y, c, _ = _place()
        for k, buf in enumerate(outs):
            mine = _remote(buf.at[c], buf.at[c], send_sems.at[base + k], recv_sems.at[base + k], (x, y, 1 - c))
            if phase == 0:
                mine.start()
            elif phase == 2:
                mine.wait_send()
                _remote(buf.at[1 - c], buf.at[1 - c], send_sems.at[base + k], recv_sems.at[base + k],
                        (x, y, 1 - c)).wait_recv()


class _Comm:
    def __init__(self, ops):
        self.ops = list(ops)
        self.arrays = [a for op in self.ops for a in op.arrays]
        self.outs = [o for op in self.ops for o in op.outs]
        self.n_sems = sum(op.n_sems for op in self.ops)

    def specs(self):
        return [pl.BlockSpec(memory_space=pl.ANY)] * len(self.arrays)

    def aliases(self, first_in, first_out):
        out, k = {}, 0
        for op in self.ops:
            for i in range(len(op.arrays)):
                if op.aliased:
                    out[first_in + k + i] = first_out + k + i
            k += len(op.arrays)
        return out

    def scratch(self):
        return [pltpu.SemaphoreType.DMA((self.n_sems,)), pltpu.SemaphoreType.DMA((self.n_sems,))]

    def run(self, phase, ins, outs, send_sems, recv_sems):
        k = base = 0
        for op in self.ops:
            n = len(op.arrays)
            op.run(phase, ins[k:k + n], outs[k:k + n], send_sems, recv_sems, base)
            k += n
            base += op.n_sems

    def split(self, results):
        out, k = [], 0
        for op in self.ops:
            out.append(list(results[k:k + len(op.arrays)]))
            k += len(op.arrays)
        return out


def _communicate(ops):
    comm = _Comm(ops)
    n = len(comm.arrays)

    def body(*refs):
        ins, outs, (send_sems, recv_sems) = refs[:n], refs[n:2 * n], refs[2 * n:]
        for phase in range(3):
            comm.run(phase, ins, outs, send_sems, recv_sems)

    res = pl.pallas_call(
        body, name="communicate", out_shape=tuple(comm.outs), in_specs=comm.specs(), out_specs=tuple(comm.specs()),
        input_output_aliases=comm.aliases(0, 0), scratch_shapes=comm.scratch(),
    )(*comm.arrays)
    return comm.split(res)


def _pool_core(he, w_ref, scale, first_row, halo, n_rows):
    d = he.shape[1]
    gd = d // len(POOL_WINDOWS)
    t = first_row + lax.broadcasted_iota(jnp.int32, (n_rows, 1), 0)
    pooled, ypre, cnts = [], [], []
    for g, w in enumerate(POOL_WINDOWS):
        hg = he[:, g * gd:(g + 1) * gd]
        s, k = hg, 1
        while k < w:
            s = s + _shift_down(s, k)
            k *= 2
        cnt = jnp.minimum(t + 1, w).astype(F32)
        p = s[halo:] / cnt - hg[halo:]
        pooled.append(p.astype(BF16))
        cnts.append(cnt)
        ypre.append(_dot(pooled[-1], w_ref[g]))
    return pooled, jnp.concatenate(ypre, axis=1), cnts


def _pool_forward(x, vec, pool_w, comm=None):
    s, d = x.shape
    ts = _row_tile(s, 1024)
    nb = s // ts
    n_g, gd, _ = pool_w.shape
    comm = comm or _Comm([])
    nc = len(comm.arrays)

    def body(*refs):
        x_ref, vec_ref, w_ref = refs[:3]
        cin = refs[3:3 + nc]
        o_ref = refs[3 + nc]
        cout = refs[4 + nc:4 + 2 * nc]
        carry = refs[4 + 2 * nc]
        sems = refs[5 + 2 * nc:]
        i = pl.program_id(0)

        @pl.when(i == 0)
        def _():
            carry[...] = jnp.zeros_like(carry)
            if nc:
                comm.run(0, cin, cout, *sems)

        if nc:
            @pl.when(i == nb - 1)
            def _():
                comm.run(1, cin, cout, *sems)

        vec = vec_ref[...]
        a, sh, gt, gpost = _vec_rows(vec)
        xb = x_ref[...]
        xn, _ = _rms(xb)
        h = xn * a + sh
        he = jnp.concatenate([carry[...], h], axis=0)
        carry[...] = h[ts - POOL_HALO:]
        _, ypre, _ = _pool_core(he, w_ref, vec[5:6], i * ts, POOL_HALO, ts)
        yn, _ = _rms(ypre * vec[5:6])
        o_ref[...] = xb + gt * (yn * gpost)
        if nc:
            @pl.when(i == nb - 1)
            def _():
                comm.run(2, cin, cout, *sems)

    res = pl.pallas_call(
        body, name="pool_forward", grid=(nb,),
        in_specs=[pl.BlockSpec((ts, d), lambda i: (i, 0)), pl.BlockSpec((8, d), lambda i: (0, 0)),
                  pl.BlockSpec((n_g, gd, gd), lambda i: (0, 0, 0))] + comm.specs(),
        out_specs=(pl.BlockSpec((ts, d), lambda i: (i, 0)), *comm.specs()),
        out_shape=(jax.ShapeDtypeStruct((s, d), F32), *comm.outs),
        input_output_aliases=comm.aliases(3, 1),
        scratch_shapes=[pltpu.VMEM((POOL_HALO, d), F32)] + (comm.scratch() if nc else []),
        compiler_params=_cparams(("arbitrary",)),
    )(x, vec, pool_w, *comm.arrays)
    return res[0], comm.split(res[1:])


def _pool_backward(dout, x, vec, pool_w):
    s, d = x.shape
    ts = _row_tile(s, 1024)
    nb = s // ts
    hb = ts // POOL_HALO
    n_g, gd, _ = pool_w.shape

    def body(do_ref, x_ref, xh_ref, vec_ref, w_ref, dx_ref, sum_ref, dw_ref, carry):
        step = pl.program_id(0)
        i = nb - 1 - step

        @pl.when(step == 0)
        def _():
            carry[...] = jnp.zeros_like(carry)
            sum_ref[...] = jnp.zeros_like(sum_ref)
            dw_ref[...] = jnp.zeros_like(dw_ref)

        vec = vec_ref[...]
        a, sh, gt, gpost = _vec_rows(vec)
        scale = vec[5:6]
        do = do_ref[...]
        xe = jnp.concatenate([xh_ref[...], x_ref[...]], axis=0)
        xne, re = _rms(xe)
        he = xne * a + sh
        rowid = lax.broadcasted_iota(jnp.int32, (POOL_HALO + ts, 1), 0)
        he = jnp.where((rowid >= POOL_HALO) | (i > 0), he, 0.0)
        xn, r = xne[POOL_HALO:], re[POOL_HALO:]
        pooled, ypre, cnts = _pool_core(he, w_ref, scale, i * ts, POOL_HALO, ts)
        yn, ry = _rms(ypre * scale)
        dyn = do * (gt * gpost)
        dy = _rms_bwd(dyn, yn, ry)
        dypre = (dy * scale).astype(BF16)
        dh_parts, q_parts = [], []
        for g, w in enumerate(POOL_WINDOWS):
            dyg = dypre[:, g * gd:(g + 1) * gd]
            dpool = _dot_nt(dyg, w_ref[g])
            dw_ref[g] += _dot_tn(pooled[g], dyg)
            q = dpool / cnts[g]
            qe = jnp.concatenate([q, carry[:, g * gd:(g + 1) * gd]], axis=0)
            acc, k = qe, 1
            while k < w:
                acc = acc + _shift_up(acc, k)
                k *= 2
            dh_parts.append(acc[:ts] - dpool)
            q_parts.append(q[:POOL_HALO])
        carry[...] = jnp.concatenate(q_parts, axis=1)
        dh = jnp.concatenate(dh_parts, axis=1)
        dxn = dh * a
        dx_ref[...] = do + _rms_bwd(dxn, xn, r)
        _add_rows(sum_ref, _norm_sums(do, yn, dh, xn, vec) + [_colsum(dh), _colsum(dy * ypre)])

    blk = lambda st: (nb - 1 - st, 0)
    return pl.pallas_call(
        body, name="pool_backward", grid=(nb,),
        in_specs=[pl.BlockSpec((ts, d), blk), pl.BlockSpec((ts, d), blk),
                  pl.BlockSpec((POOL_HALO, d), lambda st: (jnp.maximum((nb - 1 - st) * hb - 1, 0), 0)),
                  pl.BlockSpec((8, d), lambda st: (0, 0)), pl.BlockSpec((n_g, gd, gd), lambda st: (0, 0, 0))],
        out_specs=(pl.BlockSpec((ts, d), blk), pl.BlockSpec((8, d), lambda st: (0, 0)),
                   pl.BlockSpec((n_g, gd, gd), lambda st: (0, 0, 0))),
        out_shape=(jax.ShapeDtypeStruct((s, d), F32), jax.ShapeDtypeStruct((8, d), F32),
                   jax.ShapeDtypeStruct((n_g, gd, gd), F32)),
        scratch_shapes=[pltpu.VMEM((POOL_HALO, d), F32)],
        compiler_params=_cparams(("arbitrary",)),
    )(dout, x, x, vec, pool_w)


def _ffn_forward(x, vec, w_up, w_dw, w_down, comm=None, target=None):
    s, d = x.shape
    _, _, cs = w_up.shape
    ts = _row_tile(s, 256)
    nb = s // ts
    chunks = _lane_chunks(cs)
    comm = comm or _Comm([])
    nc = len(comm.arrays)
    nl = 0 if target is None else 1
    n_in, n_out = 5 + nl, 6 + nl

    def body(*refs):
        x_ref, vec_ref, wup_ref, wdw_ref, wdn_ref = refs[:5]
        cin = refs[n_in:n_in + nc]
        o_ref, h_ref, a0_ref, cc_ref, u_ref, y_ref = refs[n_in + nc:n_in + nc + 6]
        loss_ref = refs[n_in + nc + 6] if nl else None
        cout = refs[n_in + nc + n_out:n_in + 2 * nc + n_out]
        carry = refs[n_in + 2 * nc + n_out]
        sems = refs[n_in + 2 * nc + n_out + 1:]
        i = pl.program_id(0)

        @pl.when(i == 0)
        def _():
            carry[...] = jnp.zeros_like(carry)
            if nl:
                loss_ref[...] = jnp.zeros_like(loss_ref)
            if nc:
                comm.run(0, cin, cout, *sems)

        if nc:
            @pl.when(i == (3 * nb) // 4)
            def _():
                comm.run(1, cin, cout, *sems)

        vec = vec_ref[...]
        a, sh, gt, gpost = _vec_rows(vec)
        xb = x_ref[...]
        xn, _ = _rms(xb)
        hb = (xn * a + sh).astype(BF16)
        h_ref[...] = hb
        for q in range(2):
            for c0, cw in chunks:
                conv = []
                for j in (q, q + 2):
                    a0 = _dot(hb, wup_ref[j, :, c0:c0 + cw])
                    a0_ref[j, :, c0:c0 + cw] = a0.astype(BF16)
                    ae = jnp.concatenate([carry[j, :, c0:c0 + cw], a0], axis=0)
                    carry[j, :, c0:c0 + cw] = a0[ts - FFN_HALO:]
                    w = wdw_ref[:, j * cs + c0:j * cs + c0 + cw]
                    conv.append((w[2:3] * ae + w[1:2] * _shift_down(ae, 1) + w[0:1] * _shift_down(ae, 2))[FFN_HALO:])
                    cc_ref[j, :, c0:c0 + cw] = conv[-1].astype(BF16)
                u_ref[q, :, c0:c0 + cw] = (conv[0] * _sigmoid(conv[0]) * conv[1]).astype(BF16)
        y = _dot(u_ref[0], wdn_ref[0]) + _dot(u_ref[1], wdn_ref[1])
        y_ref[...] = y
        yn, _ = _rms(y)
        x_out = xb + gt * (yn * gpost)
        if nl:
            err = x_out - refs[5][...]
            o_ref[...] = err * (1.0 / d)
            loss_ref[0:1, :] += _colsum(err * err) * (0.5 / d)
        else:
            o_ref[...] = x_out
        if nc:
            @pl.when(i == nb - 1)
            def _():
                comm.run(2, cin, cout, *sems)

    const3 = lambda i: (0, 0, 0)
    res = pl.pallas_call(
        body, name="ffn_forward", grid=(nb,),
        in_specs=[pl.BlockSpec((ts, d), lambda i: (i, 0)), pl.BlockSpec((8, d), lambda i: (0, 0)),
                  pl.BlockSpec(w_up.shape, const3, pipeline_mode=pl.Buffered(1)),
                  pl.BlockSpec(w_dw.shape, lambda i: (0, 0)),
                  pl.BlockSpec(w_down.shape, const3, pipeline_mode=pl.Buffered(1))]
        + [pl.BlockSpec((ts, d), lambda i: (i, 0))] * nl + comm.specs(),
        out_specs=(pl.BlockSpec((ts, d), lambda i: (i, 0)), pl.BlockSpec((ts, d), lambda i: (i, 0)),
                   pl.BlockSpec((4, ts, cs), lambda i: (0, i, 0)), pl.BlockSpec((4, ts, cs), lambda i: (0, i, 0)),
                   pl.BlockSpec((2, ts, cs), lambda i: (0, i, 0)), pl.BlockSpec((ts, d), lambda i: (i, 0)),
                   *[pl.BlockSpec((8, d), lambda i: (0, 0))] * nl, *comm.specs()),
        out_shape=(jax.ShapeDtypeStruct((s, d), F32), jax.ShapeDtypeStruct((s, d), BF16),
                   jax.ShapeDtypeStruct((4, s, cs), BF16), jax.ShapeDtypeStruct((4, s, cs), BF16),
                   jax.ShapeDtypeStruct((2, s, cs), BF16), jax.ShapeDtypeStruct((s, d), F32),
                   *[jax.ShapeDtypeStruct((8, d), F32)] * nl, *comm.outs),
        input_output_aliases=comm.aliases(n_in, n_out),
        scratch_shapes=[pltpu.VMEM((4, FFN_HALO, cs), F32)] + (comm.scratch() if nc else []),
        compiler_params=_cparams(("arbitrary",)),
    )(x, vec, w_up, w_dw, w_down, *([target] * nl), *comm.arrays)
    return res[:n_out], comm.split(res[n_out:])


def _ffn_backward(dout, x, y, a0, cc, vec, w_up, w_dw, w_down):
    s, d = x.shape
    _, _, cs = w_up.shape
    ts = _row_tile(s, 256)
    nb = s // ts
    chunks = _lane_chunks(cs)

    def body(do_ref, x_ref, y_ref, a0_ref, cc_ref, vec_ref, wup_ref, wdw_ref, wdn_ref,
             dx_ref, da0_ref, dy_ref, sum_ref, dwdw_ref, carry, du_s, dh_s):
        step = pl.program_id(0)

        @pl.when(step == 0)
        def _():
            carry[...] = jnp.zeros_like(carry)
            sum_ref[...] = jnp.zeros_like(sum_ref)
            dwdw_ref[...] = jnp.zeros_like(dwdw_ref)

        vec = vec_ref[...]
        a, sh, gt, gpost = _vec_rows(vec)
        do = do_ref[...]
        yn, ry = _rms(y_ref[...])
        dy = _rms_bwd(do * (gt * gpost), yn, ry)
        dyb = dy.astype(BF16)
        dy_ref[...] = dyb
        order = [(q, c0, cw) for q in range(2) for c0, cw in chunks]

        def du_pieces(idx):
            q, c0, cw = order[idx]
            return [functools.partial(_store_dot_nt, du_s.at[idx % 2, :, n0:min(n0 + MXU_LANES, cw)], dy_ref,
                                      wdn_ref.at[q, c0 + n0:c0 + min(n0 + MXU_LANES, cw), :])
                    for n0 in range(0, cw, MXU_LANES)]

        def dh_pieces():
            return [functools.partial(_store_dot_nt2, dh_s.at[:, n0:n0 + MXU_LANES], da0_ref.at[0], da0_ref.at[2],
                                      wup_ref.at[0, n0:n0 + MXU_LANES, :], wup_ref.at[2, n0:n0 + MXU_LANES, :])
                    for n0 in range(0, d, MXU_LANES)]

        for piece in du_pieces(0):
            piece()
        later = dh_pieces()
        for idx, (q, c0, cw) in enumerate(order):
            work = du_pieces(idx + 1) if idx + 1 < len(order) else []
            if q == 1:
                share = -(-len(later) // (len(order) - idx))
                work, later = work + later[:share], later[share:]

            def pump(part, of=3):
                for piece in work[part::of]:
                    piece()

            cg = cc_ref[q, :, c0:c0 + cw].astype(F32)
            cv = cc_ref[q + 2, :, c0:c0 + cw].astype(F32)
            sg = _sigmoid(cg)
            sl = cg * sg
            du = du_s[idx % 2, :, :cw]
            dconv = {q: du * cv * (sg * (1.0 + cg * (1.0 - sg))), q + 2: du * sl}
            pump(0)
            for part, j in enumerate((q, q + 2)):
                dae = jnp.concatenate([dconv[j], carry[j, :, c0:c0 + cw]], axis=0)
                carry[j, :, c0:c0 + cw] = dconv[j][:FFN_HALO]
                up1 = _shift_down(dae, FFN_HALO - 1)[FFN_HALO:]
                up2 = _shift_down(dae, FFN_HALO - 2)[FFN_HALO:]
                lanes = slice(j * cs + c0, j * cs + c0 + cw)
                w = wdw_ref[:, lanes]
                da0_ref[j, :, c0:c0 + cw] = (w[2:3] * dconv[j] + w[1:2] * up1 + w[0:1] * up2).astype(BF16)
                a0 = a0_ref[j, :, c0:c0 + cw].astype(F32)
                dwdw_ref[0:1, lanes] += _colsum(up2 * a0)
                dwdw_ref[1:2, lanes] += _colsum(up1 * a0)
                dwdw_ref[2:3, lanes] += _colsum(dconv[j] * a0)
                pump(part + 1)
        dh = dh_s[...] + _dot_nt(da0_ref[1], wup_ref[1]) + _dot_nt(da0_ref[3], wup_ref[3])
        xn, r = _rms(x_ref[...])
        dx_ref[...] = do + _rms_bwd(dh * a, xn, r)
        _add_rows(sum_ref, _norm_sums(do, yn, dh, xn, vec) + [_colsum(dh)])

    blk = lambda st: (nb - 1 - st, 0)
    blk3 = lambda st: (0, nb - 1 - st, 0)
    const3 = lambda st: (0, 0, 0)
    return pl.pallas_call(
        body, name="ffn_backward", grid=(nb,),
        in_specs=[pl.BlockSpec((ts, d), blk), pl.BlockSpec((ts, d), blk), pl.BlockSpec((ts, d), blk),
                  pl.BlockSpec((4, ts, cs), blk3), pl.BlockSpec((4, ts, cs), blk3),
                  pl.BlockSpec((8, d), lambda st: (0, 0)),
                  pl.BlockSpec(w_up.shape, const3, pipeline_mode=pl.Buffered(1)),
                  pl.BlockSpec(w_dw.shape, lambda st: (0, 0)),
                  pl.BlockSpec(w_down.shape, const3, pipeline_mode=pl.Buffered(1))],
        out_specs=(pl.BlockSpec((ts, d), blk), pl.BlockSpec((4, ts, cs), blk3),
                   pl.BlockSpec((ts, d), blk), pl.BlockSpec((8, d), lambda st: (0, 0)),
                   pl.BlockSpec((8, 4 * cs), lambda st: (0, 0))),
        out_shape=(jax.ShapeDtypeStruct((s, d), F32), jax.ShapeDtypeStruct((4, s, cs), BF16),
                   jax.ShapeDtypeStruct((s, d), BF16),
                   jax.ShapeDtypeStruct((8, d), F32), jax.ShapeDtypeStruct((8, 4 * cs), F32)),
        scratch_shapes=[pltpu.VMEM((4, FFN_HALO, cs), F32), pltpu.VMEM((2, ts, max(cw for _, cw in chunks)), F32),
                        pltpu.VMEM((ts, d), F32)],
        compiler_params=_cparams(("arbitrary",)),
    )(dout, x, y, a0, cc, vec, w_up, w_dw, w_down)


def _conv_halo(width):
    return -(-(width - 1) // 8) * 8


def _conv_forward(x, vec, cvec, w_pw1, b_pw1, w_dw, w_pw2):
    s, d = x.shape
    kw = w_dw.shape[0]
    halo = _conv_halo(kw)
    ts = _row_tile(s, 512)
    hd = d // 2

    def body(x_ref, vec_ref, cvec_ref, w1_ref, b1_ref, wdw_ref, w2_ref,
             o_ref, h_ref, a_ref, uc_ref, z_ref, y_ref, carry):
        i = pl.program_id(0)

        @pl.when(i == 0)
        def _():
            carry[...] = jnp.zeros_like(carry)

        vec, cvec = vec_ref[...], cvec_ref[...]
        a, sh, gt, gpost = _vec_rows(vec)
        xb = x_ref[...]
        xn, _ = _rms(xb)
        hb = (xn * a + sh).astype(BF16)
        h_ref[...] = hb
        for j in range(4):
            a_ref[:, j * hd:(j + 1) * hd] = _dot(hb, w1_ref[j]) + b1_ref[:, j * hd:(j + 1) * hd]
        u = a_ref[:, :d] * _sigmoid(a_ref[:, d:])
        carry[halo:, :] = u
        for r0 in range(0, ts, CONV_ROWS):
            for l0 in range(0, d, CONV_LANES):
                lanes = slice(l0, l0 + CONV_LANES)
                src = carry[r0:r0 + CONV_ROWS + halo, lanes]
                acc = jnp.zeros((CONV_ROWS, CONV_LANES), F32) + cvec[0:1, lanes]
                for k in range(kw):
                    acc = acc + wdw_ref[k:k + 1, lanes] * _shift_down(src, kw - 1 - k)[halo:]
                uc_ref[r0:r0 + CONV_ROWS, lanes] = acc
        carry[:halo, :] = u[ts - halo:]
        uc = uc_ref[...]
        mu = jnp.mean(uc, axis=-1, keepdims=True)
        cen = uc - mu
        rstd = lax.rsqrt(jnp.mean(cen * cen, axis=-1, keepdims=True) + EPS)
        l = cen * rstd * cvec[1:2] + cvec[2:3]
        zb = (l * _sigmoid(l)).astype(BF16)
        z_ref[...] = zb
        y = _dot(zb, w2_ref[...]) + cvec[3:4]
        y_ref[...] = y
        yn, _ = _rms(y)
        o_ref[...] = xb + gt * (yn * gpost)

    row = lambda i: (i, 0)
    const2 = lambda i: (0, 0)
    return pl.pallas_call(
        body, name="conv_forward", grid=(s // ts,),
        in_specs=[pl.BlockSpec((ts, d), row), pl.BlockSpec((8, d), const2), pl.BlockSpec((8, d), const2),
                  pl.BlockSpec(w_pw1.shape, lambda i: (0, 0, 0)), pl.BlockSpec(b_pw1.shape, const2),
                  pl.BlockSpec(w_dw.shape, const2), pl.BlockSpec(w_pw2.shape, const2)],
        out_specs=(pl.BlockSpec((ts, d), row), pl.BlockSpec((ts, d), row), pl.BlockSpec((ts, 2 * d), row),
                   pl.BlockSpec((ts, d), row), pl.BlockSpec((ts, d), row), pl.BlockSpec((ts, d), row)),
        out_shape=(jax.ShapeDtypeStruct((s, d), F32), jax.ShapeDtypeStruct((s, d), BF16),
                   jax.ShapeDtypeStruct((s, 2 * d), F32), jax.ShapeDtypeStruct((s, d), F32),
                   jax.ShapeDtypeStruct((s, d), BF16), jax.ShapeDtypeStruct((s, d), F32)),
        scratch_shapes=[pltpu.VMEM((halo + ts, d), F32)],
        compiler_params=_cparams(("arbitrary",)),
    )(x, vec, cvec, w_pw1, b_pw1, w_dw, w_pw2)


def _conv_backward(dout, x, y, a_pre, uc, vec, cvec, w_pw1, w_dw, w_pw2):
    s, d = x.shape
    kw = w_dw.shape[0]
    kpad = -(-kw // 8) * 8
    halo = _conv_halo(kw)
    ts = _row_tile(s, 512)
    nb = s // ts
    hb = ts // halo
    hd = d // 2

    def body(do_ref, x_ref, y_ref, a_ref, ah_ref, uc_ref, vec_ref, cvec_ref, w1_ref, wdw_ref, w2_ref,
             dx_ref, da_ref, dy_ref, sum_ref, dwdw_ref, carry):
        step = pl.program_id(0)
        i = nb - 1 - step

        @pl.when(step == 0)
        def _():
            carry[...] = jnp.zeros_like(carry)
            sum_ref[...] = jnp.zeros_like(sum_ref)
            dwdw_ref[...] = jnp.zeros_like(dwdw_ref)

        vec, cvec = vec_ref[...], cvec_ref[...]
        a, sh, gt, gpost = _vec_rows(vec)
        do = do_ref[...]
        yn, ry = _rms(y_ref[...])
        dy = _rms_bwd(do * (gt * gpost), yn, ry)
        dyb = dy.astype(BF16)
        dy_ref[...] = dyb
        dz = _dot_nt(dyb, w2_ref[...])
        uc = uc_ref[...]
        mu = jnp.mean(uc, axis=-1, keepdims=True)
        cen = uc - mu
        rstd = lax.rsqrt(jnp.mean(cen * cen, axis=-1, keepdims=True) + EPS)
        lhat = cen * rstd
        l = lhat * cvec[1:2] + cvec[2:3]
        sgl = _sigmoid(l)
        dl = dz * (sgl * (1.0 + l * (1.0 - sgl)))
        dlhat = dl * cvec[1:2]
        duc = rstd * (dlhat - jnp.mean(dlhat, axis=-1, keepdims=True)
                      - lhat * jnp.mean(dlhat * lhat, axis=-1, keepdims=True))
        ae = jnp.concatenate([ah_ref[...] * (i > 0).astype(F32), a_ref[...]], axis=0)
        sgate = _sigmoid(ae[:, d:])
        val = ae[:, :d]
        ue = val * sgate
        rowid = lax.broadcasted_iota(jnp.int32, (halo + ts, 1), 0)
        ue = jnp.where((rowid >= halo) | (i > 0), ue, 0.0)
        duce = jnp.concatenate([duc, carry[...]], axis=0)
        carry[...] = duc[:halo]
        du = jnp.zeros((ts, d), F32)
        for k in range(kw):
            du = du + wdw_ref[k:k + 1, :] * _shift_down(duce, halo - (kw - 1 - k))[halo:]
            dwdw_ref[k:k + 1, :] += _colsum(duc * _shift_down(ue, kw - 1 - k)[halo:])
        sg, vl = sgate[halo:], val[halo:]
        dval = du * sg
        dgate = du * vl * (sg * (1.0 - sg))
        dvb, dgb = dval.astype(BF16), dgate.astype(BF16)
        dh = jnp.zeros((ts, d), F32)
        for j in range(2):
            da_ref[j] = dvb[:, j * hd:(j + 1) * hd]
            da_ref[j + 2] = dgb[:, j * hd:(j + 1) * hd]
            dh = dh + _dot_nt(dvb[:, j * hd:(j + 1) * hd], w1_ref[j]) + _dot_nt(dgb[:, j * hd:(j + 1) * hd], w1_ref[j + 2])
        xn, r = _rms(x_ref[...])
        dx_ref[...] = do + _rms_bwd(dh * a, xn, r)
        _add_rows(sum_ref, _norm_sums(do, yn, dh, xn, vec) + [_colsum(dh), _colsum(dy), _colsum(dl * lhat), _colsum(dl),
                            _colsum(duc), _colsum(dval), _colsum(dgate)])

    blk = lambda st: (nb - 1 - st, 0)
    const2 = lambda st: (0, 0)
    return pl.pallas_call(
        body, name="conv_backward", grid=(nb,),
        in_specs=[pl.BlockSpec((ts, d), blk), pl.BlockSpec((ts, d), blk), pl.BlockSpec((ts, d), blk),
                  pl.BlockSpec((ts, 2 * d), blk),
                  pl.BlockSpec((halo, 2 * d), lambda st: (jnp.maximum((nb - 1 - st) * hb - 1, 0), 0)),
                  pl.BlockSpec((ts, d), blk), pl.BlockSpec((8, d), const2), pl.BlockSpec((8, d), const2),
                  pl.BlockSpec(w_pw1.shape, lambda st: (0, 0, 0)), pl.BlockSpec(w_dw.shape, const2),
                  pl.BlockSpec(w_pw2.shape, const2)],
        out_specs=(pl.BlockSpec((ts, d), blk), pl.BlockSpec((4, ts, hd), lambda st: (0, nb - 1 - st, 0)),
                   pl.BlockSpec((ts, d), blk), pl.BlockSpec((16, d), const2), pl.BlockSpec((kpad, d), const2)),
        out_shape=(jax.ShapeDtypeStruct((s, d), F32), jax.ShapeDtypeStruct((4, s, hd), BF16),
                   jax.ShapeDtypeStruct((s, d), BF16), jax.ShapeDtypeStruct((16, d), F32),
                   jax.ShapeDtypeStruct((kpad, d), F32)),
        scratch_shapes=[pltpu.VMEM((halo, d), F32)],
        compiler_params=_cparams(("arbitrary",)),
    )(dout, x, y, a_pre, a_pre, uc, vec, cvec, w_pw1, w_dw, w_pw2)


def _weight_grad(a, b, comm=None):
    na, s, k = a.shape
    nb_, _, n = b.shape
    nj = max(na, nb_)
    ts = _row_tile(s, 2048)
    nt = s // ts
    comm = comm or _Comm([])
    nc = len(comm.arrays)

    def body(*refs):
        a_ref, b_ref = refs[:2]
        cin = refs[2:2 + nc]
        o_ref = refs[2 + nc]
        cout = refs[3 + nc:3 + 2 * nc]
        sems = refs[3 + 2 * nc:]
        j, t = pl.program_id(0), pl.program_id(1)

        if nc:
            @pl.when((j == 0) & (t == 0))
            def _():
                comm.run(0, cin, cout, *sems)

            @pl.when((j == nj // 2) & (t == nt // 2))
            def _():
                comm.run(1, cin, cout, *sems)

        @pl.when(t == 0)
        def _():
            o_ref[...] = jnp.zeros_like(o_ref)

        o_ref[0] += _dot_tn(a_ref[0], b_ref[0])

        if nc:
            @pl.when((j == nj - 1) & (t == nt - 1))
            def _():
                comm.run(2, cin, cout, *sems)

    res = pl.pallas_call(
        body, name="weight_grad", grid=(nj, nt),
        in_specs=[pl.BlockSpec((1, ts, k), (lambda j, t: (j, t, 0)) if na > 1 else (lambda j, t: (0, t, 0))),
                  pl.BlockSpec((1, ts, n), (lambda j, t: (j, t, 0)) if nb_ > 1 else (lambda j, t: (0, t, 0)))]
        + comm.specs(),
        out_specs=(pl.BlockSpec((1, k, n), lambda j, t: (j, 0, 0)), *comm.specs()),
        out_shape=(jax.ShapeDtypeStruct((nj, k, n), F32), *comm.outs),
        input_output_aliases=comm.aliases(2, 1),
        scratch_shapes=comm.scratch() if nc else [],
        compiler_params=_cparams(("arbitrary", "arbitrary") if nc else ("parallel", "arbitrary")),
    )(a, b, *comm.arrays)
    return res[0], comm.split(res[1:])


def _adamw_math(w, g, m, v):
    nm = ADAM_B1 * m + (1.0 - ADAM_B1) * g
    nv = ADAM_B2 * v + (1.0 - ADAM_B2) * (g * g)
    m_hat = nm * (1.0 / (1.0 - ADAM_B1 ** ADAM_STEP))
    v_hat = nv * (1.0 / (1.0 - ADAM_B2 ** ADAM_STEP))
    return -ADAM_LR * (m_hat / (jnp.sqrt(v_hat) + ADAM_EPS) + ADAM_WD * w), nm, nv


def _adamw_many(params):
    n = len(params)

    def body(*refs):
        for k in range(n):
            w_ref, g_ref, m_ref, v_ref = refs[4 * k:4 * k + 4]
            outs = refs[4 * n + 3 * k:4 * n + 3 * k + 3]
            for o_ref, val in zip(outs, _adamw_math(w_ref[...], g_ref[...], m_ref[...], v_ref[...])):
                o_ref[...] = val

    vm = pl.BlockSpec(memory_space=pltpu.VMEM)
    res = pl.pallas_call(
        body, name="adamw_many", in_specs=[vm] * (4 * n), out_specs=tuple([vm] * (3 * n)),
        out_shape=tuple(jax.ShapeDtypeStruct(p[0].shape, F32) for p in params for _ in range(3)),
        compiler_params=_cparams(),
    )(*[a for p in params for a in p])
    return [res[3 * k:3 * k + 3] for k in range(n)]


def _adamw(w, g, m, v, comm=None):
    nl, r, c = w.shape
    tr = _row_tile(r, 256)
    nr = r // tr
    comm = comm or _Comm([])
    nc = len(comm.arrays)

    def body(*refs):
        w_ref, g_ref, m_ref, v_ref = refs[:4]
        cin = refs[4:4 + nc]
        d_ref, nm_ref, nv_ref = refs[4 + nc:7 + nc]
        cout = refs[7 + nc:7 + 2 * nc]
        sems = refs[7 + 2 * nc:]
        l, i = pl.program_id(0), pl.program_id(1)
        if nc:
            @pl.when((l == 0) & (i == 0))
            def _():
                comm.run(0, cin, cout, *sems)

            @pl.when((l == nl // 2) & (i == nr // 2))
            def _():
                comm.run(1, cin, cout, *sems)

        d_ref[...], nm_ref[...], nv_ref[...] = _adamw_math(w_ref[...], g_ref[...], m_ref[...], v_ref[...])
        if nc:
            @pl.when((l == nl - 1) & (i == nr - 1))
            def _():
                comm.run(2, cin, cout, *sems)

    spec = pl.BlockSpec((1, tr, c), lambda l, i: (l, i, 0))
    shp = jax.ShapeDtypeStruct((nl, r, c), F32)
    res = pl.pallas_call(
        body, name="adamw", grid=(nl, nr), in_specs=[spec] * 4 + comm.specs(),
        out_specs=(spec,) * 3 + tuple(comm.specs()), out_shape=(shp,) * 3 + tuple(comm.outs),
        input_output_aliases=comm.aliases(4, 3), scratch_shapes=comm.scratch() if nc else [],
        compiler_params=_cparams(("arbitrary", "arbitrary") if nc else ("parallel", "parallel")),
    )(w, g, m, v, *comm.arrays)
    return res[:3], comm.split(res[3:])


def _add_my_half(g, other, idx):
    _, _, h, c = g.shape
    th = _row_tile(h, 256)

    def body(idx_ref, g_ref, o_ref, out_ref):
        out_ref[...] = (g_ref[:, 0] + o_ref[...]).astype(BF16)

    return pl.pallas_call(
        body, name="add_my_half",
        grid_spec=pltpu.PrefetchScalarGridSpec(
            num_scalar_prefetch=1, grid=(4, h // th),
            in_specs=[pl.BlockSpec((1, 1, th, c), lambda j, i, idx_ref: (j, idx_ref[1], i, 0)),
                      pl.BlockSpec((1, th, c), lambda j, i, idx_ref: (j, i, 0))],
            out_specs=pl.BlockSpec((1, th, c), lambda j, i, idx_ref: (j, i, 0))),
        out_shape=jax.ShapeDtypeStruct(other.shape, BF16),
        compiler_params=_cparams(("parallel", "parallel")),
    )(idx, g, other)


def _sum_for_my_chip(g, other, got, idx):
    _, _, h, c = g.shape
    th = _row_tile(h, 256)

    def body(idx_ref, g_ref, o_ref, q_ref, out_ref):
        out_ref[0] = (((g_ref[0, 0] + o_ref[0]) + q_ref[0].astype(F32)) + q_ref[1].astype(F32)) + q_ref[2].astype(F32)

    return pl.pallas_call(
        body, name="sum_for_my_chip",
        grid_spec=pltpu.PrefetchScalarGridSpec(
            num_scalar_prefetch=1, grid=(h // th,),
            in_specs=[pl.BlockSpec((1, 1, th, c), lambda i, idx_ref: (idx_ref[0], idx_ref[1], i, 0)),
                      pl.BlockSpec((1, th, c), lambda i, idx_ref: (idx_ref[0], i, 0)),
                      pl.BlockSpec((3, th, c), lambda i, idx_ref: (0, i, 0))],
            out_specs=pl.BlockSpec((1, th, c), lambda i, idx_ref: (idx_ref[1], i, 0))),
        out_shape=jax.ShapeDtypeStruct((2, h, c), F32),
        compiler_params=_cparams(("parallel",)),
    )(idx, g, other, got)


class _Reducer:
    def __init__(self, idx):
        self.idx = idx
        self.groups = []

    def add(self, grads):
        group = {"state": 0, "g": [g.reshape(4, 2, g.shape[1] // 2, g.shape[2]) for g in grads]}
        self.groups.append(group)
        return group

    def steps(self):
        ops, owners = [], []
        for gr in self.groups:
            if gr["state"] == 0:
                ops.append(_Swap(gr["g"]))
            elif gr["state"] == 1:
                ops.append(_Exchange(gr["parts"]))
            elif gr["state"] == 2:
                ops.append(_Join(gr["bufs"]))
            else:
                continue
            owners.append(gr)
        return ops, owners

    def absorb(self, owners, results):
        for gr, res in zip(owners, results):
            if gr["state"] == 0:
                gr["other"] = res
                gr["parts"] = [_add_my_half(g, o, self.idx) for g, o in zip(gr["g"], res)]
            elif gr["state"] == 1:
                gr["bufs"] = [_sum_for_my_chip(g, o, q, self.idx) for g, o, q in zip(gr["g"], gr["other"], res)]
            else:
                gr["full"] = [b.reshape(2 * b.shape[1], b.shape[2]) for b in res]
            gr["state"] += 1

    def drain(self):
        while any(gr["state"] < 3 for gr in self.groups):
            ops, owners = self.steps()
            self.absorb(owners, _communicate(ops))


class _GatherRows:
    def __init__(self, bufs):
        self.arrays = list(bufs)
        self.outs = [jax.ShapeDtypeStruct(b.shape, b.dtype) for b in bufs]
        self.aliased = True
        self.n_sems = 7 * len(bufs)

    def run(self, phase, ins, outs, send_sems, recv_sems, base):
        x, y, c, chips = _place()
        me, sibling = (x, y, c), (x, y, 1 - c)
        for k, buf in enumerate(outs):
            def copy(i, block_of, to):
                blk = buf.at[4 * block_of[0] + 2 * block_of[1] + block_of[2]]
                return _remote(blk, blk, send_sems.at[base + 7 * k + i], recv_sems.at[base + 7 * k + i], to)

            if phase == 0:
                copy(0, me, sibling).start()
            for r, (px, py) in enumerate(chips):
                if phase == 0:
                    copy(1 + r, me, (px, py, c)).start()
                elif phase == 1:
                    copy(1 + r, (px, py, c), me).wait_recv()
                    copy(4 + r, (px, py, c), sibling).start()
                else:
                    copy(4 + r, (px, py, 1 - c), me).wait_recv()
                    copy(1 + r, me, (px, py, c)).wait_send()
                    copy(4 + r, (px, py, c), sibling).wait_send()
            if phase == 2:
                copy(0, sibling, me).wait_recv()
                copy(0, me, sibling).wait_send()


def _sum_devices(gathered):
    nd, m, n = gathered.shape

    def body(g_ref, o_ref):
        acc = g_ref[0]
        for b in range(1, nd):
            acc = acc + g_ref[b]
        o_ref[...] = acc

    return pl.pallas_call(
        body, name="sum_devices", out_shape=jax.ShapeDtypeStruct((m, n), F32),
        in_specs=[pl.BlockSpec(memory_space=pltpu.VMEM)], out_specs=pl.BlockSpec(memory_space=pltpu.VMEM),
        compiler_params=_cparams(),
    )(gathered)


def _ada_weight_grad(c_all, dmod_cols):
    nl, nd, ncol = dmod_cols.shape
    d = c_all.shape[1]

    def body(c_ref, dm_ref, o_ref):
        o_ref[0] = lax.dot_general(c_ref[...], dm_ref[0], (((0,), (0,)), ((), ())),
                                   preferred_element_type=F32, precision=lax.Precision.HIGHEST)

    return pl.pallas_call(
        body, name="ada_weight_grad", grid=(nl,),
        in_specs=[pl.BlockSpec((nd, d), lambda l: (0, 0)), pl.BlockSpec((1, nd, ncol), lambda l: (l, 0, 0))],
        out_specs=pl.BlockSpec((1, d, ncol), lambda l: (l, 0, 0)),
        out_shape=jax.ShapeDtypeStruct((nl, d, ncol), F32), compiler_params=_cparams(("parallel",)),
    )(c_all, dmod_cols)


def _pad_rows(a, rows):
    return jnp.pad(a, ((0, rows - a.shape[0]), (0, 0)))


def _shard_cols(full, chip, width):
    return lax.dynamic_slice_in_dim(full, chip * width, width, axis=full.ndim - 1)


def kernel(x, c, ada_w, ada_b, pre_g, post_g, pool_w, pool_scale, cv_w_pw1, cv_b_pw1, cv_w_dw, cv_b_dw, cv_ln_g, cv_ln_b, cv_w_pw2, cv_b_pw2, ffn_w_up, ffn_w_dw, ffn_w_down, loss_target, m_ada_w, m_ada_b, m_pre_g, m_post_g, m_pool_w, m_pool_scale, m_cv_w_pw1, m_cv_b_pw1, m_cv_w_dw, m_cv_b_dw, m_cv_ln_g, m_cv_ln_b, m_cv_w_pw2, m_cv_b_pw2, m_ffn_w_up, m_ffn_w_dw, m_ffn_w_down, v_ada_w, v_ada_b, v_pre_g, v_post_g, v_pool_w, v_pool_scale, v_cv_w_pw1, v_cv_b_pw1, v_cv_w_dw, v_cv_b_dw, v_cv_ln_g, v_cv_ln_b, v_cv_w_pw2, v_cv_b_pw2, v_ffn_w_up, v_ffn_w_dw, v_ffn_w_down):
    s, d = x.shape[1], x.shape[2]
    dq = d // N_CHIPS
    n_g = pool_w.shape[1]
    gq = pool_w.shape[2]
    gd = pool_w.shape[3]
    kw = cv_w_dw.shape[1]
    cs = ffn_w_up.shape[2]
    fq = ffn_w_down.shape[1]
    chip = 2 * lax.axis_index("x") + lax.axis_index("y")
    core = lax.axis_index("c")
    chip1 = jnp.reshape(chip, (1,)).astype(jnp.int32)
    core1 = jnp.reshape(core, (1,)).astype(jnp.int32)
    xs, tgt = x[0], loss_target[0]

    small_rows = [pre_g.reshape(4, dq), post_g.reshape(4, dq), cv_w_dw[0], cv_b_dw, cv_ln_g, cv_ln_b, cv_b_pw2,
                  cv_b_pw1.reshape(2, dq)]
    small = jnp.concatenate(small_rows, axis=0)
    n_small = small.shape[0]
    small = _pad_rows(small, -(-n_small // 16) * 16)
    dwf = _pad_rows(ffn_w_dw.reshape(6, cs), 16)
    first = _AllGather([_cast_into_slot(pool_w.reshape(n_g * gq, gd), chip1), _cast_into_slot(small, chip1, dtype=F32),
                        _cast_into_slot(dwf, chip1, dtype=F32), _cast_into_slot(ffn_w_up[0], chip1)])
    (c_rep, mod_rep), ((g_pool, g_small, g_dwf, g_up0),) = _ada_forward(c, ada_w, _Comm([first]))
    c_all = c_rep[:, 0, :]
    mod = mod_rep[:, :, 0, :].transpose(1, 0, 2).reshape(ada_b.shape) + ada_b
    second = _AllGather([_cast_into_slot(ffn_w_down[0], chip1)])
    later = _AllGather([_cast_into_slot(cv_w_pw1[0], chip1), _cast_into_slot(cv_w_pw2[0], chip1),
                        _cast_into_slot(ffn_w_up[1], chip1), _cast_into_slot(ffn_w_down[1], chip1)])
    poolw_full = g_pool.reshape(N_CHIPS, n_g, gq, gd).transpose(1, 0, 2, 3).reshape(n_g, gd, gd)
    smallf = g_small.transpose(1, 0, 2).reshape(g_small.shape[1], d)
    pre_full, post_full = smallf[0:4].reshape(2, 2, d), smallf[4:8].reshape(2, 2, d)
    wdw31 = smallf[8:8 + kw]
    o = 8 + kw
    b_dw, ln_g, ln_b, b_pw2 = smallf[o:o + 1], smallf[o + 1:o + 2], smallf[o + 2:o + 3], smallf[o + 3:o + 4]
    b_pw1 = g_small[:, o + 4:o + 6, :].reshape(1, 2 * d)
    ffn_dw = g_dwf[:, :6, :].transpose(1, 0, 2).reshape(2, 3, N_CHIPS * cs)

    def sub_vec(layer, sub, extra=None):
        m6 = mod[layer].reshape(6, d)
        rows = [pre_full[layer, sub][None], 1.0 + m6[3 * sub + 1][None], m6[3 * sub][None], m6[3 * sub + 2][None],
                post_full[layer, sub][None]]
        if extra is not None:
            rows.append(extra)
        return _pad_rows(jnp.concatenate(rows, axis=0), 8)

    vec_pool = sub_vec(0, 0, pool_scale)
    vec_f0, vec_conv, vec_f1 = sub_vec(0, 1), sub_vec(1, 0), sub_vec(1, 1)
    cvec = _pad_rows(jnp.concatenate([b_dw, ln_g, ln_b, b_pw2], axis=0), 8)

    x1, ((g_dn0,),) = _pool_forward(xs, vec_pool, poolw_full, _Comm([second]))
    w_up0, w_dn0 = g_up0, g_dn0.reshape(2, 2 * fq, d)
    (x2, h_f0, a0_f0, cc_f0, u_f0, y_f0), ((g_pw1, g_pw2, g_up1, g_dn1),) = _ffn_forward(
        x1, vec_f0, w_up0, ffn_dw[0], w_dn0, _Comm([later]))
    pw2_full = g_pw2.reshape(d, d)
    w_up1, w_dn1 = g_up1, g_dn1.reshape(2, 2 * fq, d)
    x3, h_cv, a_cv, uc_cv, z_cv, y_cv = _conv_forward(x2, vec_conv, cvec, g_pw1, b_pw1, wdw31, pw2_full)
    (dx4, h_f1, a0_f1, cc_f1, u_f1, y_f1, loss_rows), _ = _ffn_forward(x3, vec_f1, w_up1, ffn_dw[1], w_dn1, target=tgt)

    dx3, da0_f1, dy_f1, sum_f1, dwdw_f1 = _ffn_backward(dx4, x3, y_f1, a0_f1, cc_f1, vec_f1, w_up1, ffn_dw[1], w_dn1)
    dx2, da_cv, dy_cv, sum_cv, dwdw_cv = _conv_backward(dx3, x2, y_cv, a_cv, uc_cv, vec_conv, cvec, g_pw1, wdw31, pw2_full)
    dx1, da0_f0, dy_f0, sum_f0, dwdw_f0 = _ffn_backward(dx2, x1, y_f0, a0_f0, cc_f0, vec_f0, w_up0, ffn_dw[0], w_dn0)
    dx0, sum_pool, gw_pool = _pool_backward(dx1, xs, vec_pool, poolw_full)
    gw_pool4 = gw_pool.reshape(n_g, N_CHIPS, gq, gd).transpose(1, 0, 2, 3).reshape(N_CHIPS, n_g * gq, gd)

    slab = jnp.concatenate([sum_f1, sum_cv, dwdw_cv, sum_f0, sum_pool, loss_rows], axis=0)
    wide = jnp.concatenate([dwdw_f1, dwdw_f0], axis=0)
    n_slab = slab.shape[0]
    mine = jnp.concatenate([slab, wide.reshape(-1, d)], axis=0)
    rows_of_all = _cast_into_slot(mine, 2 * chip1 + core1, N_DEV, F32)
    red = _Reducer(jnp.concatenate([chip1, core1]))

    def carried(call, *args, extra=()):
        ops, owners = red.steps()
        out, results = call(*args, _Comm(ops + list(extra)))
        red.absorb(owners, results[:len(ops)])
        return out, results[len(ops):]

    gw_up1, ((both_all,),) = carried(_weight_grad, h_f1[None], da0_f1, extra=[_GatherRows([rows_of_all])])
    r_up1 = red.add([gw_up1])
    r_up0 = red.add([carried(_weight_grad, h_f0[None], da0_f0)[0]])
    r_dn1 = red.add([carried(_weight_grad, u_f1, dy_f1[None])[0].reshape(N_CHIPS, fq, d)])
    r_dn0 = red.add([carried(_weight_grad, u_f0, dy_f0[None])[0].reshape(N_CHIPS, fq, d)])
    r_pw1 = red.add([carried(_weight_grad, h_cv[None], da_cv)[0]])
    r_last = red.add([carried(_weight_grad, z_cv[None], dy_cv[None])[0].reshape(N_CHIPS, dq, d), gw_pool4])

    tot_both = _sum_devices(both_all)
    slab_all, tot = both_all[:, :n_slab], tot_both[:n_slab]
    tot_wide = tot_both[n_slab:].reshape(wide.shape)
    kpad = dwdw_cv.shape[0]
    o_cv, o_dw, o_f0 = 8, 24, 24 + kpad
    o_pool, o_loss = o_f0 + 8, o_f0 + 16
    loss = jnp.sum(tot[o_loss])
    dmod_l0 = jnp.concatenate([slab_all[:, o_pool + 4], slab_all[:, o_pool + 3], slab_all[:, o_pool + 1],
                               slab_all[:, o_f0 + 4], slab_all[:, o_f0 + 3], slab_all[:, o_f0 + 1]], axis=-1)
    dmod_l1 = jnp.concatenate([slab_all[:, o_cv + 4], slab_all[:, o_cv + 3], slab_all[:, o_cv + 1],
                               slab_all[:, 4], slab_all[:, 3], slab_all[:, 1]], axis=-1)
    dmod = jnp.stack([dmod_l0, dmod_l1], axis=0)
    g_ada_b = _sum_devices(dmod.transpose(1, 0, 2))
    ncol = ada_w.shape[2]
    g_ada_w = _ada_weight_grad(c_all, _shard_cols(dmod, chip, ncol))

    g_pre = jnp.stack([jnp.stack([tot[o_pool + 2], tot[o_f0 + 2]]), jnp.stack([tot[o_cv + 2], tot[2]])])
    g_post = jnp.stack([jnp.stack([tot[o_pool + 0], tot[o_f0 + 0]]), jnp.stack([tot[o_cv + 0], tot[0]])])
    g_pool_scale = tot[o_pool + 5][None]
    g_b_pw2, g_ln_g, g_ln_b, g_b_dw = tot[o_cv + 5], tot[o_cv + 6], tot[o_cv + 7], tot[o_cv + 8]
    g_b_pw1 = jnp.concatenate([tot[o_cv + 9], tot[o_cv + 10]])
    g_w_dw31 = tot[o_dw:o_dw + kw]
    g_ffn_dw = jnp.stack([tot_wide[8:11], tot_wide[0:3]])

    grads_small = {
        "pre_g": _shard_cols(g_pre, chip, dq), "post_g": _shard_cols(g_post, chip, dq),
        "pool_scale": g_pool_scale, "cv_b_pw1": _shard_cols(g_b_pw1[None], chip, 2 * dq),
        "cv_w_dw": _shard_cols(g_w_dw31[None], chip, dq), "cv_b_dw": _shard_cols(g_b_dw[None], chip, dq),
        "cv_ln_g": _shard_cols(g_ln_g[None], chip, dq), "cv_ln_b": _shard_cols(g_ln_b[None], chip, dq),
        "cv_b_pw2": _shard_cols(g_b_pw2[None], chip, dq), "ffn_w_dw": _shard_cols(g_ffn_dw, chip, cs),
        "ada_b": g_ada_b,
    }
    params_small = {
        "pre_g": (pre_g, m_pre_g, v_pre_g), "post_g": (post_g, m_post_g, v_post_g),
        "pool_scale": (pool_scale, m_pool_scale, v_pool_scale), "cv_b_pw1": (cv_b_pw1, m_cv_b_pw1, v_cv_b_pw1),
        "cv_w_dw": (cv_w_dw, m_cv_w_dw, v_cv_w_dw), "cv_b_dw": (cv_b_dw, m_cv_b_dw, v_cv_b_dw),
        "cv_ln_g": (cv_ln_g, m_cv_ln_g, v_cv_ln_g), "cv_ln_b": (cv_ln_b, m_cv_ln_b, v_cv_ln_b),
        "cv_b_pw2": (cv_b_pw2, m_cv_b_pw2, v_cv_b_pw2), "ffn_w_dw": (ffn_w_dw, m_ffn_w_dw, v_ffn_w_dw),
        "ada_b": (ada_b, m_ada_b, v_ada_b),
    }
    names = list(params_small)
    small_g = {nm: grads_small[nm].reshape(params_small[nm][0].shape) for nm in names}
    updated = _adamw_many([(params_small[nm][0], small_g[nm], params_small[nm][1], params_small[nm][2])
                           for nm in names])
    small_d = {nm: u[0] for nm, u in zip(names, updated)}
    small_m = {nm: u[1] for nm, u in zip(names, updated)}
    small_v = {nm: u[2] for nm, u in zip(names, updated)}

    red.drain()
    big_p = {
        "ada_w": (ada_w, m_ada_w, v_ada_w), "pool_w": (pool_w, m_pool_w, v_pool_w),
        "cv_w_pw1": (cv_w_pw1, m_cv_w_pw1, v_cv_w_pw1), "cv_w_pw2": (cv_w_pw2, m_cv_w_pw2, v_cv_w_pw2),
        "ffn_w_up": (ffn_w_up, m_ffn_w_up, v_ffn_w_up), "ffn_w_down": (ffn_w_down, m_ffn_w_down, v_ffn_w_down),
    }
    big_g, big_d, big_m, big_v = {}, {}, {}, {}

    def update(nm, grad):
        w, m, v = big_p[nm]
        as3 = lambda t: t.reshape((-1,) + w.shape[-2:])
        (dl, nm_, nv_), _ = _adamw(as3(w), as3(grad), as3(m), as3(v))
        big_g[nm] = grad.reshape(w.shape)
        big_d[nm], big_m[nm], big_v[nm] = dl.reshape(w.shape), nm_.reshape(w.shape), nv_.reshape(w.shape)

    full = lambda group, k=0: group["full"][k]
    update("ffn_w_up", jnp.stack([full(r_up0), full(r_up1)]))
    update("ada_w", g_ada_w)
    update("ffn_w_down", jnp.stack([full(r_dn0), full(r_dn1)]))
    update("cv_w_pw1", full(r_pw1))
    update("cv_w_pw2", full(r_last, 0))
    update("pool_w", full(r_last, 1))

    order = ["ada_w", "ada_b", "pre_g", "post_g", "pool_w", "pool_scale", "cv_w_pw1", "cv_b_pw1", "cv_w_dw", "cv_b_dw",
             "cv_ln_g", "cv_ln_b", "cv_w_pw2", "cv_b_pw2", "ffn_w_up", "ffn_w_dw", "ffn_w_down"]
    pick = lambda bigs, smalls: [bigs[nm] if nm in bigs else smalls[nm] for nm in order]
    return (loss, dx0[None], *pick(big_g, small_g), *pick(big_d, small_d), *pick(big_m, small_m),
            *pick(big_v, small_v))
```

```python
import functools

import jax
import jax.numpy as jnp
from jax import lax
from jax.experimental import pallas as pl
from jax.experimental.pallas import tpu as pltpu

F32 = jnp.float32
BF16 = jnp.bfloat16
EPS = 1e-6
N_CHIPS = 4
N_DEV = 8
POOL_WINDOWS = (2, 4, 8, 16)
POOL_HALO = 16
FFN_HALO = 16
MXU_LANES = 256
CONV_ROWS, CONV_LANES = 128, 128
ADAM_LR = 0.001
ADAM_B1 = 0.9
ADAM_B2 = 0.999
ADAM_EPS = 1e-08
ADAM_WD = 0.01
ADAM_STEP = 10
V7X_VMEM_LIMIT = 58 * 1024 * 1024
MESH = pl.DeviceIdType.MESH


def _cparams(sem=None, vmem=V7X_VMEM_LIMIT):
    return pltpu.CompilerParams(dimension_semantics=sem, vmem_limit_bytes=vmem)


def _row_tile(n, want):
    if n <= want:
        return n
    t = want - want % 8
    while n % t:
        t -= 8
    return t


def _lane_chunks(width):
    out, c = [], 0
    while c < width:
        w = min(512, width - c)
        out.append((c, w))
        c += w
    return out


def _dot(a, b):
    return jnp.dot(a, b, preferred_element_type=F32)


def _dot_nt(a, b):
    return lax.dot_general(a, b, (((1,), (1,)), ((), ())), preferred_element_type=F32)


def _store_dot_nt(dst, a_ref, b_ref):
    dst[...] = _dot_nt(a_ref[...], b_ref[...])


def _store_dot_nt2(dst, a1_ref, a2_ref, b1_ref, b2_ref):
    dst[...] = _dot_nt(a1_ref[...], b1_ref[...]) + _dot_nt(a2_ref[...], b2_ref[...])


def _dot_tn(a, b):
    return lax.dot_general(a, b, (((0,), (0,)), ((), ())), preferred_element_type=F32)


def _rms(x):
    r = lax.rsqrt(jnp.mean(x * x, axis=-1, keepdims=True) + EPS)
    return x * r, r


def _rms_bwd(dyn, yn, r):
    return r * (dyn - yn * jnp.mean(dyn * yn, axis=-1, keepdims=True))


def _sigmoid(x):
    return 0.5 * jnp.tanh(0.5 * x) + 0.5


def _colsum(x):
    return jnp.sum(x, axis=0, keepdims=True)


def _shift_down(x, k):
    return x if k == 0 else pltpu.roll(x, k, 0)


def _shift_up(x, k):
    return x if k == 0 else pltpu.roll(x, x.shape[0] - k, 0)


def _vec_rows(vec):
    return vec[0:1] * vec[1:2], vec[2:3], vec[3:4], vec[4:5]


def _norm_sums(do, yn, dh, xn, vec):
    p, q = _colsum(do * yn), _colsum(dh * xn)
    return [p * vec[3:4], p * vec[4:5], q * vec[1:2], q * vec[0:1]]


def _add_rows(sum_ref, rows):
    for k, r in enumerate(rows):
        sum_ref[k:k + 1, :] += r


def _ada_forward(c, ada_w, comm=None):
    n_layers, d, ncol = ada_w.shape
    comm = comm or _Comm([])
    nc = len(comm.arrays)

    def body(*refs):
        c_ref, w_ref = refs[:2]
        cin = refs[2:2 + nc]
        call_ref, mod_ref = refs[2 + nc:4 + nc]
        cout = refs[4 + nc:4 + 2 * nc]
        part_ref, sendbuf, send_sems, recv_sems, send2, recv2 = refs[4 + 2 * nc:10 + 2 * nc]
        carried_sems = refs[10 + 2 * nc:]
        if nc:
            comm.run(0, cin, cout, *carried_sems)
        x, y, cc = lax.axis_index("x"), lax.axis_index("y"), lax.axis_index("c")
        me = 4 * x + 2 * y + cc
        rel = [(x, y, 1 - cc), (1 - x, y, cc), (x, 1 - y, cc), (1 - x, 1 - y, cc),
               (1 - x, y, 1 - cc), (x, 1 - y, 1 - cc), (1 - x, 1 - y, 1 - cc)]
        cv = c_ref[...]
        call_ref[me] = jnp.broadcast_to(cv * _sigmoid(cv), (8, d))

        def gather(k, block, to):
            blk = call_ref.at[block]
            return pltpu.make_async_remote_copy(src_ref=blk, dst_ref=blk, send_sem=send_sems.at[k],
                                                recv_sem=recv_sems.at[k], device_id=to, device_id_type=MESH)

        for k, to in enumerate(rel):
            gather(k, me, to).start()
        for k, (px, py, pc) in enumerate(rel):
            gather(k, 4 * px + 2 * py + pc, rel[k]).wait_recv()
        for k, to in enumerate(rel):
            gather(k, me, to).wait_send()

        ca = call_ref[...].reshape(8 * N_DEV, d)
        for l in range(n_layers):
            part_ref[l] = jnp.dot(ca, w_ref[l], preferred_element_type=F32, precision=lax.Precision.HIGHEST)

        j = 2 * x + y
        chips = [(1 - x, y), (x, 1 - y), (1 - x, 1 - y)]

        def rows_of(b):
            return part_ref[:, pl.ds(pl.multiple_of(8 * b, 8), 8), :]

        def scatter(k, src_j, to):
            return pltpu.make_async_remote_copy(
                src_ref=sendbuf.at[k], dst_ref=mod_ref.at[src_j], send_sem=send2.at[k], recv_sem=recv2.at[k],
                device_id=to, device_id_type=MESH)

        mod_ref[j] = rows_of(me)
        for k, (px, py) in enumerate(chips):
            sendbuf[k] = rows_of(4 * px + 2 * py + cc)
            scatter(k, j, (px, py, cc)).start()
        for k, (px, py) in enumerate(chips):
            scatter(k, 2 * px + py, (px, py, cc)).wait_recv()
        for k, (px, py) in enumerate(chips):
            scatter(k, j, (px, py, cc)).wait_send()
        if nc:
            comm.run(1, cin, cout, *carried_sems)
            comm.run(2, cin, cout, *carried_sems)

    vm = pl.BlockSpec(memory_space=pltpu.VMEM)
    res = pl.pallas_call(
        body, name="ada_forward",
        out_shape=(jax.ShapeDtypeStruct((N_DEV, 8, d), F32), jax.ShapeDtypeStruct((N_CHIPS, n_layers, 8, ncol), F32),
                   *comm.outs),
        in_specs=[vm, vm] + comm.specs(), out_specs=(vm, vm, *comm.specs()),
        input_output_aliases=comm.aliases(2, 2),
        scratch_shapes=[pltpu.VMEM((n_layers, 8 * N_DEV, ncol), F32), pltpu.VMEM((3, n_layers, 8, ncol), F32),
                        pltpu.SemaphoreType.DMA((7,)), pltpu.SemaphoreType.DMA((7,)),
                        pltpu.SemaphoreType.DMA((3,)), pltpu.SemaphoreType.DMA((3,))] + (comm.scratch() if nc else []),
        compiler_params=_cparams(),
    )(c, ada_w, *comm.arrays)
    return res[:2], comm.split(res[2:])


def _cast_into_slot(w2d, slot, n_slots=N_CHIPS, dtype=None):
    r, c = w2d.shape
    tr = _row_tile(r, 256)
    dtype = dtype or BF16

    def body(slot_ref, w_ref, o_ref):
        o_ref[0] = w_ref[...].astype(dtype)

    return pl.pallas_call(
        body, name="cast_into_slot",
        grid_spec=pltpu.PrefetchScalarGridSpec(
            num_scalar_prefetch=1, grid=(r // tr,),
            in_specs=[pl.BlockSpec((tr, c), lambda i, slot_ref: (i, 0))],
            out_specs=pl.BlockSpec((1, tr, c), lambda i, slot_ref: (slot_ref[0], i, 0))),
        out_shape=jax.ShapeDtypeStruct((n_slots, r, c), dtype), compiler_params=_cparams(("parallel",)),
    )(slot, w2d)


def _place():
    x, y, c = lax.axis_index("x"), lax.axis_index("y"), lax.axis_index("c")
    return x, y, c, [(1 - x, y), (x, 1 - y), (1 - x, 1 - y)]


def _remote(src, dst, send_sem, recv_sem, to):
    return pltpu.make_async_remote_copy(src_ref=src, dst_ref=dst, send_sem=send_sem, recv_sem=recv_sem,
                                        device_id=to, device_id_type=MESH)


class _AllGather:
    def __init__(self, bufs):
        self.arrays = list(bufs)
        self.outs = [jax.ShapeDtypeStruct(b.shape, b.dtype) for b in bufs]
        self.aliased = True
        self.n_sems = 6 * len(bufs)

    def run(self, phase, ins, outs, send_sems, recv_sems, base):
        x, y, c, chips = _place()
        j = 2 * x + y
        for k, buf in enumerate(outs):
            half = buf.shape[1] // 2

            def part(src_j, h):
                return buf.at[src_j, pl.ds(h * half, half), :]

            def ici(r, src_j, to):
                s = base + 6 * k + r
                return _remote(part(src_j, c), part(src_j, c), send_sems.at[s], recv_sems.at[s], to)

            def d2d(r, src_j, h):
                s = base + 6 * k + 3 + r
                return _remote(part(src_j, h), part(src_j, h), send_sems.at[s], recv_sems.at[s], (x, y, 1 - c))

            for r, (px, py) in enumerate(chips):
                if phase == 0:
                    ici(r, j, (px, py, c)).start()
                elif phase == 1:
                    ici(r, 2 * px + py, (px, py, c)).wait_recv()
                    d2d(r, 2 * px + py, c).start()
                else:
                    d2d(r, 2 * px + py, 1 - c).wait_recv()
                    ici(r, j, (px, py, c)).wait_send()
                    d2d(r, 2 * px + py, c).wait_send()


class _Swap:
    def __init__(self, grads):
        self.arrays = list(grads)
        self.outs = [jax.ShapeDtypeStruct((g.shape[0],) + g.shape[2:], g.dtype) for g in grads]
        self.aliased = False
        self.n_sems = len(grads)

    def run(self, phase, ins, outs, send_sems, recv_sems, base):
        x, y, c, _ = _place()
        for k in range(len(ins)):
            cp = _remote(ins[k].at[:, 1 - c], outs[k], send_sems.at[base + k], recv_sems.at[base + k], (x, y, 1 - c))
            if phase == 0:
                cp.start()
            elif phase == 2:
                cp.wait()


class _Exchange:
    def __init__(self, parts):
        self.arrays = list(parts)
        self.outs = [jax.ShapeDtypeStruct((3,) + p.shape[1:], p.dtype) for p in parts]
        self.aliased = False
        self.n_sems = 3 * len(parts)

    def run(self, phase, ins, outs, send_sems, recv_sems, base):
        x, y, c, chips = _place()
        for k in range(len(ins)):
            for r, (px, py) in enumerate(chips):
                s = base + 3 * k + r
                cp = _remote(ins[k].at[2 * px + py], outs[k].at[r], send_sems.at[s], recv_sems.at[s], (px, py, c))
                if phase == 0:
                    cp.start()
                elif phase == 2:
                    cp.wait()


class _Join:
    def __init__(self, bufs):
        self.arrays = list(bufs)
        self.outs = [jax.ShapeDtypeStruct(b.shape, b.dtype) for b in bufs]
        self.aliased = True
        self.n_sems = len(bufs)

    def run(self, phase, ins, outs, send_sems, recv_sems, base):
        x, y, c, _ = _place()
        for k, buf in enumerate(outs):
            mine = _remote(buf.at[c], buf.at[c], send_sems.at[base + k], recv_sems.at[base + k], (x, y, 1 - c))
            if phase == 0:
                mine.start()
            elif phase == 2:
                mine.wait_send()
                _remote(buf.at[1 - c], buf.at[1 - c], send_sems.at[base + k], recv_sems.at[base + k],
                        (x, y, 1 - c)).wait_recv()


class _Comm:
    def __init__(self, ops):
        self.ops = list(ops)
        self.arrays = [a for op in self.ops for a in op.arrays]
        self.outs = [o for op in self.ops for o in op.outs]
        self.n_sems = sum(op.n_sems for op in self.ops)

    def specs(self):
        return [pl.BlockSpec(memory_space=pl.ANY)] * len(self.arrays)

    def aliases(self, first_in, first_out):
        out, k = {}, 0
        for op in self.ops:
            for i in range(len(op.arrays)):
                if op.aliased:
                    out[first_in + k + i] = first_out + k + i
            k += len(op.arrays)
        return out

    def scratch(self):
        return [pltpu.SemaphoreType.DMA((self.n_sems,)), pltpu.SemaphoreType.DMA((self.n_sems,))]

    def run(self, phase, ins, outs, send_sems, recv_sems):
        k = base = 0
        for op in self.ops:
            n = len(op.arrays)
            op.run(phase, ins[k:k + n], outs[k:k + n], send_sems, recv_sems, base)
            k += n
            base += op.n_sems

    def split(self, results):
        out, k = [], 0
        for op in self.ops:
            out.append(list(results[k:k + len(op.arrays)]))
            k += len(op.arrays)
        return out


def _communicate(ops):
    comm = _Comm(ops)
    n = len(comm.arrays)

    def body(*refs):
        ins, outs, (send_sems, recv_sems) = refs[:n], refs[n:2 * n], refs[2 * n:]
        for phase in range(3):
            comm.run(phase, ins, outs, send_sems, recv_sems)

    res = pl.pallas_call(
        body, name="communicate", out_shape=tuple(comm.outs), in_specs=comm.specs(), out_specs=tuple(comm.specs()),
        input_output_aliases=comm.aliases(0, 0), scratch_shapes=comm.scratch(),
    )(*comm.arrays)
    return comm.split(res)


def _pool_core(he, w_ref, scale, first_row, halo, n_rows):
    d = he.shape[1]
    gd = d // len(POOL_WINDOWS)
    t = first_row + lax.broadcasted_iota(jnp.int32, (n_rows, 1), 0)
    pooled, ypre, cnts = [], [], []
    for g, w in enumerate(POOL_WINDOWS):
        hg = he[:, g * gd:(g + 1) * gd]
        s, k = hg, 1
        while k < w:
            s = s + _shift_down(s, k)
            k *= 2
        cnt = jnp.minimum(t + 1, w).astype(F32)
        p = s[halo:] / cnt - hg[halo:]
        pooled.append(p.astype(BF16))
        cnts.append(cnt)
        ypre.append(_dot(pooled[-1], w_ref[g]))
    return pooled, jnp.concatenate(ypre, axis=1), cnts


def _pool_forward(x, vec, pool_w, comm=None):
    s, d = x.shape
    ts = _row_tile(s, 512)
    nb = s // ts
    n_g, gd, _ = pool_w.shape
    comm = comm or _Comm([])
    nc = len(comm.arrays)

    def body(*refs):
        x_ref, vec_ref, w_ref = refs[:3]
        cin = refs[3:3 + nc]
        o_ref = refs[3 + nc]
        cout = refs[4 + nc:4 + 2 * nc]
        carry = refs[4 + 2 * nc]
        sems = refs[5 + 2 * nc:]
        i = pl.program_id(0)

        @pl.when(i == 0)
        def _():
            carry[...] = jnp.zeros_like(carry)
            if nc:
                comm.run(0, cin, cout, *sems)

        if nc:
            @pl.when(i == nb - 1)
            def _():
                comm.run(1, cin, cout, *sems)

        vec = vec_ref[...]
        a, sh, gt, gpost = _vec_rows(vec)
        xb = x_ref[...]
        xn, _ = _rms(xb)
        h = xn * a + sh
        he = jnp.concatenate([carry[...], h], axis=0)
        carry[...] = h[ts - POOL_HALO:]
        _, ypre, _ = _pool_core(he, w_ref, vec[5:6], i * ts, POOL_HALO, ts)
        yn, _ = _rms(ypre * vec[5:6])
        o_ref[...] = xb + gt * (yn * gpost)
        if nc:
            @pl.when(i == nb - 1)
            def _():
                comm.run(2, cin, cout, *sems)

    res = pl.pallas_call(
        body, name="pool_forward", grid=(nb,),
        in_specs=[pl.BlockSpec((ts, d), lambda i: (i, 0)), pl.BlockSpec((8, d), lambda i: (0, 0)),
                  pl.BlockSpec((n_g, gd, gd), lambda i: (0, 0, 0))] + comm.specs(),
        out_specs=(pl.BlockSpec((ts, d), lambda i: (i, 0)), *comm.specs()),
        out_shape=(jax.ShapeDtypeStruct((s, d), F32), *comm.outs),
        input_output_aliases=comm.aliases(3, 1),
        scratch_shapes=[pltpu.VMEM((POOL_HALO, d), F32)] + (comm.scratch() if nc else []),
        compiler_params=_cparams(("arbitrary",)),
    )(x, vec, pool_w, *comm.arrays)
    return res[0], comm.split(res[1:])


def _pool_backward(dout, x, vec, pool_w):
    s, d = x.shape
    ts = _row_tile(s, 512)
    nb = s // ts
    hb = ts // POOL_HALO
    n_g, gd, _ = pool_w.shape

    def body(do_ref, x_ref, xh_ref, vec_ref, w_ref, dx_ref, sum_ref, dw_ref, carry):
        step = pl.program_id(0)
        i = nb - 1 - step

        @pl.when(step == 0)
        def _():
            carry[...] = jnp.zeros_like(carry)
            sum_ref[...] = jnp.zeros_like(sum_ref)
            dw_ref[...] = jnp.zeros_like(dw_ref)

        vec = vec_ref[...]
        a, sh, gt, gpost = _vec_rows(vec)
        scale = vec[5:6]
        do = do_ref[...]
        xe = jnp.concatenate([xh_ref[...], x_ref[...]], axis=0)
        xne, re = _rms(xe)
        he = xne * a + sh
        rowid = lax.broadcasted_iota(jnp.int32, (POOL_HALO + ts, 1), 0)
        he = jnp.where((rowid >= POOL_HALO) | (i > 0), he, 0.0)
        xn, r = xne[POOL_HALO:], re[POOL_HALO:]
        pooled, ypre, cnts = _pool_core(he, w_ref, scale, i * ts, POOL_HALO, ts)
        yn, ry = _rms(ypre * scale)
        dyn = do * (gt * gpost)
        dy = _rms_bwd(dyn, yn, ry)
        dypre = (dy * scale).astype(BF16)
        dh_parts, q_parts = [], []
        for g, w in enumerate(POOL_WINDOWS):
            dyg = dypre[:, g * gd:(g + 1) * gd]
            dpool = _dot_nt(dyg, w_ref[g])
            dw_ref[g] += _dot_tn(pooled[g], dyg)
            q = dpool / cnts[g]
            qe = jnp.concatenate([q, carry[:, g * gd:(g + 1) * gd]], axis=0)
            acc, k = qe, 1
            while k < w:
                acc = acc + _shift_up(acc, k)
                k *= 2
            dh_parts.append(acc[:ts] - dpool)
            q_parts.append(q[:POOL_HALO])
        carry[...] = jnp.concatenate(q_parts, axis=1)
        dh = jnp.concatenate(dh_parts, axis=1)
        dxn = dh * a
        dx_ref[...] = do + _rms_bwd(dxn, xn, r)
        _add_rows(sum_ref, _norm_sums(do, yn, dh, xn, vec) + [_colsum(dh), _colsum(dy * ypre)])

    blk = lambda st: (nb - 1 - st, 0)
    return pl.pallas_call(
        body, name="pool_backward", grid=(nb,),
        in_specs=[pl.BlockSpec((ts, d), blk), pl.BlockSpec((ts, d), blk),
                  pl.BlockSpec((POOL_HALO, d), lambda st: (jnp.maximum((nb - 1 - st) * hb - 1, 0), 0)),
                  pl.BlockSpec((8, d), lambda st: (0, 0)), pl.BlockSpec((n_g, gd, gd), lambda st: (0, 0, 0))],
        out_specs=(pl.BlockSpec((ts, d), blk), pl.BlockSpec((8, d), lambda st: (0, 0)),
                   pl.BlockSpec((n_g, gd, gd), lambda st: (0, 0, 0))),
        out_shape=(jax.ShapeDtypeStruct((s, d), F32), jax.ShapeDtypeStruct((8, d), F32),
                   jax.ShapeDtypeStruct((n_g, gd, gd), F32)),
        scratch_shapes=[pltpu.VMEM((POOL_HALO, d), F32)],
        compiler_params=_cparams(("arbitrary",)),
    )(dout, x, x, vec, pool_w)


def _ffn_forward(x, vec, w_up, w_dw, w_down, comm=None, target=None):
    s, d = x.shape
    _, _, cs = w_up.shape
    ts = _row_tile(s, 256)
    nb = s // ts
    chunks = _lane_chunks(cs)
    comm = comm or _Comm([])
    nc = len(comm.arrays)
    nl = 0 if target is None else 1
    n_in, n_out = 5 + nl, 6 + nl

    def body(*refs):
        x_ref, vec_ref, wup_ref, wdw_ref, wdn_ref = refs[:5]
        cin = refs[n_in:n_in + nc]
        o_ref, h_ref, a0_ref, cc_ref, u_ref, y_ref = refs[n_in + nc:n_in + nc + 6]
        loss_ref = refs[n_in + nc + 6] if nl else None
        cout = refs[n_in + nc + n_out:n_in + 2 * nc + n_out]
        carry = refs[n_in + 2 * nc + n_out]
        sems = refs[n_in + 2 * nc + n_out + 1:]
        i = pl.program_id(0)

        @pl.when(i == 0)
        def _():
            carry[...] = jnp.zeros_like(carry)
            if nl:
                loss_ref[...] = jnp.zeros_like(loss_ref)
            if nc:
                comm.run(0, cin, cout, *sems)

        if nc:
            @pl.when(i == (3 * nb) // 4)
            def _():
                comm.run(1, cin, cout, *sems)

        vec = vec_ref[...]
        a, sh, gt, gpost = _vec_rows(vec)
        xb = x_ref[...]
        xn, _ = _rms(xb)
        hb = (xn * a + sh).astype(BF16)
        h_ref[...] = hb
        for q in range(2):
            for c0, cw in chunks:
                conv = []
                for j in (q, q + 2):
                    a0 = _dot(hb, wup_ref[j, :, c0:c0 + cw])
                    a0_ref[j, :, c0:c0 + cw] = a0.astype(BF16)
                    ae = jnp.concatenate([carry[j, :, c0:c0 + cw], a0], axis=0)
                    carry[j, :, c0:c0 + cw] = a0[ts - FFN_HALO:]
                    w = wdw_ref[:, j * cs + c0:j * cs + c0 + cw]
                    conv.append((w[2:3] * ae + w[1:2] * _shift_down(ae, 1) + w[0:1] * _shift_down(ae, 2))[FFN_HALO:])
                    cc_ref[j, :, c0:c0 + cw] = conv[-1].astype(BF16)
                u_ref[q, :, c0:c0 + cw] = (conv[0] * _sigmoid(conv[0]) * conv[1]).astype(BF16)
        y = _dot(u_ref[0], wdn_ref[0]) + _dot(u_ref[1], wdn_ref[1])
        y_ref[...] = y
        yn, _ = _rms(y)
        x_out = xb + gt * (yn * gpost)
        if nl:
            err = x_out - refs[5][...]
            o_ref[...] = err * (1.0 / d)
            loss_ref[0:1, :] += _colsum(err * err) * (0.5 / d)
        else:
            o_ref[...] = x_out
        if nc:
            @pl.when(i == nb - 1)
            def _():
                comm.run(2, cin, cout, *sems)

    const3 = lambda i: (0, 0, 0)
    res = pl.pallas_call(
        body, name="ffn_forward", grid=(nb,),
        in_specs=[pl.BlockSpec((ts, d), lambda i: (i, 0)), pl.BlockSpec((8, d), lambda i: (0, 0)),
                  pl.BlockSpec(w_up.shape, const3, pipeline_mode=pl.Buffered(1)),
                  pl.BlockSpec(w_dw.shape, lambda i: (0, 0)),
                  pl.BlockSpec(w_down.shape, const3, pipeline_mode=pl.Buffered(1))]
        + [pl.BlockSpec((ts, d), lambda i: (i, 0))] * nl + comm.specs(),
        out_specs=(pl.BlockSpec((ts, d), lambda i: (i, 0)), pl.BlockSpec((ts, d), lambda i: (i, 0)),
                   pl.BlockSpec((4, ts, cs), lambda i: (0, i, 0)), pl.BlockSpec((4, ts, cs), lambda i: (0, i, 0)),
                   pl.BlockSpec((2, ts, cs), lambda i: (0, i, 0)), pl.BlockSpec((ts, d), lambda i: (i, 0)),
                   *[pl.BlockSpec((8, d), lambda i: (0, 0))] * nl, *comm.specs()),
        out_shape=(jax.ShapeDtypeStruct((s, d), F32), jax.ShapeDtypeStruct((s, d), BF16),
                   jax.ShapeDtypeStruct((4, s, cs), BF16), jax.ShapeDtypeStruct((4, s, cs), BF16),
                   jax.ShapeDtypeStruct((2, s, cs), BF16), jax.ShapeDtypeStruct((s, d), F32),
                   *[jax.ShapeDtypeStruct((8, d), F32)] * nl, *comm.outs),
        input_output_aliases=comm.aliases(n_in, n_out),
        scratch_shapes=[pltpu.VMEM((4, FFN_HALO, cs), F32)] + (comm.scratch() if nc else []),
        compiler_params=_cparams(("arbitrary",)),
    )(x, vec, w_up, w_dw, w_down, *([target] * nl), *comm.arrays)
    return res[:n_out], comm.split(res[n_out:])


def _ffn_backward(dout, x, y, a0, cc, vec, w_up, w_dw, w_down):
    s, d = x.shape
    _, _, cs = w_up.shape
    ts = _row_tile(s, 256)
    nb = s // ts
    chunks = _lane_chunks(cs)

    def body(do_ref, x_ref, y_ref, a0_ref, cc_ref, vec_ref, wup_ref, wdw_ref, wdn_ref,
             dx_ref, da0_ref, dy_ref, sum_ref, dwdw_ref, carry, du_s, dh_s):
        step = pl.program_id(0)

        @pl.when(step == 0)
        def _():
            carry[...] = jnp.zeros_like(carry)
            sum_ref[...] = jnp.zeros_like(sum_ref)
            dwdw_ref[...] = jnp.zeros_like(dwdw_ref)

        vec = vec_ref[...]
        a, sh, gt, gpost = _vec_rows(vec)
        do = do_ref[...]
        yn, ry = _rms(y_ref[...])
        dy = _rms_bwd(do * (gt * gpost), yn, ry)
        dyb = dy.astype(BF16)
        dy_ref[...] = dyb
        order = [(q, c0, cw) for q in range(2) for c0, cw in chunks]

        def du_pieces(idx):
            q, c0, cw = order[idx]
            return [functools.partial(_store_dot_nt, du_s.at[idx % 2, :, n0:min(n0 + MXU_LANES, cw)], dy_ref,
                                      wdn_ref.at[q, c0 + n0:c0 + min(n0 + MXU_LANES, cw), :])
                    for n0 in range(0, cw, MXU_LANES)]

        def dh_pieces():
            return [functools.partial(_store_dot_nt2, dh_s.at[:, n0:n0 + MXU_LANES], da0_ref.at[0], da0_ref.at[2],
                                      wup_ref.at[0, n0:n0 + MXU_LANES, :], wup_ref.at[2, n0:n0 + MXU_LANES, :])
                    for n0 in range(0, d, MXU_LANES)]

        for piece in du_pieces(0):
            piece()
        later = dh_pieces()
        for idx, (q, c0, cw) in enumerate(order):
            work = du_pieces(idx + 1) if idx + 1 < len(order) else []
            if q == 1:
                share = -(-len(later) // (len(order) - idx))
                work, later = work + later[:share], later[share:]

            def pump(part, of=3):
                for piece in work[part::of]:
                    piece()

            cg = cc_ref[q, :, c0:c0 + cw].astype(F32)
            cv = cc_ref[q + 2, :, c0:c0 + cw].astype(F32)
            sg = _sigmoid(cg)
            sl = cg * sg
            du = du_s[idx % 2, :, :cw]
            dconv = {q: du * cv * (sg * (1.0 + cg * (1.0 - sg))), q + 2: du * sl}
            pump(0)
            for part, j in enumerate((q, q + 2)):
                dae = jnp.concatenate([dconv[j], carry[j, :, c0:c0 + cw]], axis=0)
                carry[j, :, c0:c0 + cw] = dconv[j][:FFN_HALO]
                up1 = _shift_down(dae, FFN_HALO - 1)[FFN_HALO:]
                up2 = _shift_down(dae, FFN_HALO - 2)[FFN_HALO:]
                lanes = slice(j * cs + c0, j * cs + c0 + cw)
                w = wdw_ref[:, lanes]
                da0_ref[j, :, c0:c0 + cw] = (w[2:3] * dconv[j] + w[1:2] * up1 + w[0:1] * up2).astype(BF16)
                a0 = a0_ref[j, :, c0:c0 + cw].astype(F32)
                dwdw_ref[0:1, lanes] += _colsum(up2 * a0)
                dwdw_ref[1:2, lanes] += _colsum(up1 * a0)
                dwdw_ref[2:3, lanes] += _colsum(dconv[j] * a0)
                pump(part + 1)
        dh = dh_s[...] + _dot_nt(da0_ref[1], wup_ref[1]) + _dot_nt(da0_ref[3], wup_ref[3])
        xn, r = _rms(x_ref[...])
        dx_ref[...] = do + _rms_bwd(dh * a, xn, r)
        _add_rows(sum_ref, _norm_sums(do, yn, dh, xn, vec) + [_colsum(dh)])

    blk = lambda st: (nb - 1 - st, 0)
    blk3 = lambda st: (0, nb - 1 - st, 0)
    const3 = lambda st: (0, 0, 0)
    return pl.pallas_call(
        body, name="ffn_backward", grid=(nb,),
        in_specs=[pl.BlockSpec((ts, d), blk), pl.BlockSpec((ts, d), blk), pl.BlockSpec((ts, d), blk),
                  pl.BlockSpec((4, ts, cs), blk3), pl.BlockSpec((4, ts, cs), blk3),
                  pl.BlockSpec((8, d), lambda st: (0, 0)),
                  pl.BlockSpec(w_up.shape, const3, pipeline_mode=pl.Buffered(1)),
                  pl.BlockSpec(w_dw.shape, lambda st: (0, 0)),
                  pl.BlockSpec(w_down.shape, const3, pipeline_mode=pl.Buffered(1))],
        out_specs=(pl.BlockSpec((ts, d), blk), pl.BlockSpec((4, ts, cs), blk3),
                   pl.BlockSpec((ts, d), blk), pl.BlockSpec((8, d), lambda st: (0, 0)),
                   pl.BlockSpec((8, 4 * cs), lambda st: (0, 0))),
        out_shape=(jax.ShapeDtypeStruct((s, d), F32), jax.ShapeDtypeStruct((4, s, cs), BF16),
                   jax.ShapeDtypeStruct((s, d), BF16),
                   jax.ShapeDtypeStruct((8, d), F32), jax.ShapeDtypeStruct((8, 4 * cs), F32)),
        scratch_shapes=[pltpu.VMEM((4, FFN_HALO, cs), F32), pltpu.VMEM((2, ts, max(cw for _, cw in chunks)), F32),
                        pltpu.VMEM((ts, d), F32)],
        compiler_params=_cparams(("arbitrary",)),
    )(dout, x, y, a0, cc, vec, w_up, w_dw, w_down)


def _conv_halo(width):
    return -(-(width - 1) // 8) * 8


def _conv_forward(x, vec, cvec, w_pw1, b_pw1, w_dw, w_pw2):
    s, d = x.shape
    kw = w_dw.shape[0]
    halo = _conv_halo(kw)
    ts = _row_tile(s, 512)
    hd = d // 2

    def body(x_ref, vec_ref, cvec_ref, w1_ref, b1_ref, wdw_ref, w2_ref,
             o_ref, h_ref, a_ref, uc_ref, z_ref, y_ref, carry):
        i = pl.program_id(0)

        @pl.when(i == 0)
        def _():
            carry[...] = jnp.zeros_like(carry)

        vec, cvec = vec_ref[...], cvec_ref[...]
        a, sh, gt, gpost = _vec_rows(vec)
        xb = x_ref[...]
        xn, _ = _rms(xb)
        hb = (xn * a + sh).astype(BF16)
        h_ref[...] = hb
        for j in range(4):
            a_ref[:, j * hd:(j + 1) * hd] = _dot(hb, w1_ref[j]) + b1_ref[:, j * hd:(j + 1) * hd]
        u = a_ref[:, :d] * _sigmoid(a_ref[:, d:])
        carry[halo:, :] = u
        for r0 in range(0, ts, CONV_ROWS):
            for l0 in range(0, d, CONV_LANES):
                lanes = slice(l0, l0 + CONV_LANES)
                src = carry[r0:r0 + CONV_ROWS + halo, lanes]
                acc = jnp.zeros((CONV_ROWS, CONV_LANES), F32) + cvec[0:1, lanes]
                for k in range(kw):
                    acc = acc + wdw_ref[k:k + 1, lanes] * _shift_down(src, kw - 1 - k)[halo:]
                uc_ref[r0:r0 + CONV_ROWS, lanes] = acc
        carry[:halo, :] = u[ts - halo:]
        uc = uc_ref[...]
        mu = jnp.mean(uc, axis=-1, keepdims=True)
        cen = uc - mu
        rstd = lax.rsqrt(jnp.mean(cen * cen, axis=-1, keepdims=True) + EPS)
        l = cen * rstd * cvec[1:2] + cvec[2:3]
        zb = (l * _sigmoid(l)).astype(BF16)
        z_ref[...] = zb
        y = _dot(zb, w2_ref[...]) + cvec[3:4]
        y_ref[...] = y
        yn, _ = _rms(y)
        o_ref[...] = xb + gt * (yn * gpost)

    row = lambda i: (i, 0)
    const2 = lambda i: (0, 0)
    return pl.pallas_call(
        body, name="conv_forward", grid=(s // ts,),
        in_specs=[pl.BlockSpec((ts, d), row), pl.BlockSpec((8, d), const2), pl.BlockSpec((8, d), const2),
                  pl.BlockSpec(w_pw1.shape, lambda i: (0, 0, 0)), pl.BlockSpec(b_pw1.shape, const2),
                  pl.BlockSpec(w_dw.shape, const2), pl.BlockSpec(w_pw2.shape, const2)],
        out_specs=(pl.BlockSpec((ts, d), row), pl.BlockSpec((ts, d), row), pl.BlockSpec((ts, 2 * d), row),
                   pl.BlockSpec((ts, d), row), pl.BlockSpec((ts, d), row), pl.BlockSpec((ts, d), row)),
        out_shape=(jax.ShapeDtypeStruct((s, d), F32), jax.ShapeDtypeStruct((s, d), BF16),
                   jax.ShapeDtypeStruct((s, 2 * d), F32), jax.ShapeDtypeStruct((s, d), F32),
                   jax.ShapeDtypeStruct((s, d), BF16), jax.ShapeDtypeStruct((s, d), F32)),
        scratch_shapes=[pltpu.VMEM((halo + ts, d), F32)],
        compiler_params=_cparams(("arbitrary",)),
    )(x, vec, cvec, w_pw1, b_pw1, w_dw, w_pw2)


def _conv_backward(dout, x, y, a_pre, uc, vec, cvec, w_pw1, w_dw, w_pw2):
    s, d = x.shape
    kw = w_dw.shape[0]
    kpad = -(-kw // 8) * 8
    halo = _conv_halo(kw)
    ts = _row_tile(s, 512)
    nb = s // ts
    hb = ts // halo
    hd = d // 2

    def body(do_ref, x_ref, y_ref, a_ref, ah_ref, uc_ref, vec_ref, cvec_ref, w1_ref, wdw_ref, w2_ref,
             dx_ref, da_ref, dy_ref, sum_ref, dwdw_ref, carry):
        step = pl.program_id(0)
        i = nb - 1 - step

        @pl.when(step == 0)
        def _():
            carry[...] = jnp.zeros_like(carry)
            sum_ref[...] = jnp.zeros_like(sum_ref)
            dwdw_ref[...] = jnp.zeros_like(dwdw_ref)

        vec, cvec = vec_ref[...], cvec_ref[...]
        a, sh, gt, gpost = _vec_rows(vec)
        do = do_ref[...]
        yn, ry = _rms(y_ref[...])
        dy = _rms_bwd(do * (gt * gpost), yn, ry)
        dyb = dy.astype(BF16)
        dy_ref[...] = dyb
        dz = _dot_nt(dyb, w2_ref[...])
        uc = uc_ref[...]
        mu = jnp.mean(uc, axis=-1, keepdims=True)
        cen = uc - mu
        rstd = lax.rsqrt(jnp.mean(cen * cen, axis=-1, keepdims=True) + EPS)
        lhat = cen * rstd
        l = lhat * cvec[1:2] + cvec[2:3]
        sgl = _sigmoid(l)
        dl = dz * (sgl * (1.0 + l * (1.0 - sgl)))
        dlhat = dl * cvec[1:2]
        duc = rstd * (dlhat - jnp.mean(dlhat, axis=-1, keepdims=True)
                      - lhat * jnp.mean(dlhat * lhat, axis=-1, keepdims=True))
        ae = jnp.concatenate([ah_ref[...] * (i > 0).astype(F32), a_ref[...]], axis=0)
        sgate = _sigmoid(ae[:, d:])
        val = ae[:, :d]
        ue = val * sgate
        rowid = lax.broadcasted_iota(jnp.int32, (halo + ts, 1), 0)
        ue = jnp.where((rowid >= halo) | (i > 0), ue, 0.0)
        duce = jnp.concatenate([duc, carry[...]], axis=0)
        carry[...] = duc[:halo]
        du = jnp.zeros((ts, d), F32)
        for k in range(kw):
            du = du + wdw_ref[k:k + 1, :] * _shift_down(duce, halo - (kw - 1 - k))[halo:]
            dwdw_ref[k:k + 1, :] += _colsum(duc * _shift_down(ue, kw - 1 - k)[halo:])
        sg, vl = sgate[halo:], val[halo:]
        dval = du * sg
        dgate = du * vl * (sg * (1.0 - sg))
        dvb, dgb = dval.astype(BF16), dgate.astype(BF16)
        dh = jnp.zeros((ts, d), F32)
        for j in range(2):
            da_ref[j] = dvb[:, j * hd:(j + 1) * hd]
            da_ref[j + 2] = dgb[:, j * hd:(j + 1) * hd]
            dh = dh + _dot_nt(dvb[:, j * hd:(j + 1) * hd], w1_ref[j]) + _dot_nt(dgb[:, j * hd:(j + 1) * hd], w1_ref[j + 2])
        xn, r = _rms(x_ref[...])
        dx_ref[...] = do + _rms_bwd(dh * a, xn, r)
        _add_rows(sum_ref, _norm_sums(do, yn, dh, xn, vec) + [_colsum(dh), _colsum(dy), _colsum(dl * lhat), _colsum(dl),
                            _colsum(duc), _colsum(dval), _colsum(dgate)])

    blk = lambda st: (nb - 1 - st, 0)
    const2 = lambda st: (0, 0)
    return pl.pallas_call(
        body, name="conv_backward", grid=(nb,),
        in_specs=[pl.BlockSpec((ts, d), blk), pl.BlockSpec((ts, d), blk), pl.BlockSpec((ts, d), blk),
                  pl.BlockSpec((ts, 2 * d), blk),
                  pl.BlockSpec((halo, 2 * d), lambda st: (jnp.maximum((nb - 1 - st) * hb - 1, 0), 0)),
                  pl.BlockSpec((ts, d), blk), pl.BlockSpec((8, d), const2), pl.BlockSpec((8, d), const2),
                  pl.BlockSpec(w_pw1.shape, lambda st: (0, 0, 0)), pl.BlockSpec(w_dw.shape, const2),
                  pl.BlockSpec(w_pw2.shape, const2)],
        out_specs=(pl.BlockSpec((ts, d), blk), pl.BlockSpec((4, ts, hd), lambda st: (0, nb - 1 - st, 0)),
                   pl.BlockSpec((ts, d), blk), pl.BlockSpec((16, d), const2), pl.BlockSpec((kpad, d), const2)),
        out_shape=(jax.ShapeDtypeStruct((s, d), F32), jax.ShapeDtypeStruct((4, s, hd), BF16),
                   jax.ShapeDtypeStruct((s, d), BF16), jax.ShapeDtypeStruct((16, d), F32),
                   jax.ShapeDtypeStruct((kpad, d), F32)),
        scratch_shapes=[pltpu.VMEM((halo, d), F32)],
        compiler_params=_cparams(("arbitrary",)),
    )(dout, x, y, a_pre, a_pre, uc, vec, cvec, w_pw1, w_dw, w_pw2)


def _weight_grad(a, b, comm=None):
    na, s, k = a.shape
    nb_, _, n = b.shape
    nj = max(na, nb_)
    ts = _row_tile(s, 2048)
    nt = s // ts
    comm = comm or _Comm([])
    nc = len(comm.arrays)

    def body(*refs):
        a_ref, b_ref = refs[:2]
        cin = refs[2:2 + nc]
        o_ref = refs[2 + nc]
        cout = refs[3 + nc:3 + 2 * nc]
        sems = refs[3 + 2 * nc:]
        j, t = pl.program_id(0), pl.program_id(1)

        if nc:
            @pl.when((j == 0) & (t == 0))
            def _():
                comm.run(0, cin, cout, *sems)

            @pl.when((j == nj // 2) & (t == nt // 2))
            def _():
                comm.run(1, cin, cout, *sems)

        @pl.when(t == 0)
        def _():
            o_ref[...] = jnp.zeros_like(o_ref)

        o_ref[0] += _dot_tn(a_ref[0], b_ref[0])

        if nc:
            @pl.when((j == nj - 1) & (t == nt - 1))
            def _():
                comm.run(2, cin, cout, *sems)

    res = pl.pallas_call(
        body, name="weight_grad", grid=(nj, nt),
        in_specs=[pl.BlockSpec((1, ts, k), (lambda j, t: (j, t, 0)) if na > 1 else (lambda j, t: (0, t, 0))),
                  pl.BlockSpec((1, ts, n), (lambda j, t: (j, t, 0)) if nb_ > 1 else (lambda j, t: (0, t, 0)))]
        + comm.specs(),
        out_specs=(pl.BlockSpec((1, k, n), lambda j, t: (j, 0, 0)), *comm.specs()),
        out_shape=(jax.ShapeDtypeStruct((nj, k, n), F32), *comm.outs),
        input_output_aliases=comm.aliases(2, 1),
        scratch_shapes=comm.scratch() if nc else [],
        compiler_params=_cparams(("arbitrary", "arbitrary") if nc else ("parallel", "arbitrary")),
    )(a, b, *comm.arrays)
    return res[0], comm.split(res[1:])


def _adamw_math(w, g, m, v):
    nm = ADAM_B1 * m + (1.0 - ADAM_B1) * g
    nv = ADAM_B2 * v + (1.0 - ADAM_B2) * (g * g)
    m_hat = nm * (1.0 / (1.0 - ADAM_B1 ** ADAM_STEP))
    v_hat = nv * (1.0 / (1.0 - ADAM_B2 ** ADAM_STEP))
    return -ADAM_LR * (m_hat / (jnp.sqrt(v_hat) + ADAM_EPS) + ADAM_WD * w), nm, nv


def _adamw_many(params):
    n = len(params)

    def body(*refs):
        for k in range(n):
            w_ref, g_ref, m_ref, v_ref = refs[4 * k:4 * k + 4]
            outs = refs[4 * n + 3 * k:4 * n + 3 * k + 3]
            for o_ref, val in zip(outs, _adamw_math(w_ref[...], g_ref[...], m_ref[...], v_ref[...])):
                o_ref[...] = val

    vm = pl.BlockSpec(memory_space=pltpu.VMEM)
    res = pl.pallas_call(
        body, name="adamw_many", in_specs=[vm] * (4 * n), out_specs=tuple([vm] * (3 * n)),
        out_shape=tuple(jax.ShapeDtypeStruct(p[0].shape, F32) for p in params for _ in range(3)),
        compiler_params=_cparams(),
    )(*[a for p in params for a in p])
    return [res[3 * k:3 * k + 3] for k in range(n)]


def _adamw(w, g, m, v, comm=None):
    nl, r, c = w.shape
    tr = _row_tile(r, 256)
    nr = r // tr
    comm = comm or _Comm([])
    nc = len(comm.arrays)

    def body(*refs):
        w_ref, g_ref, m_ref, v_ref = refs[:4]
        cin = refs[4:4 + nc]
        d_ref, nm_ref, nv_ref = refs[4 + nc:7 + nc]
        cout = refs[7 + nc:7 + 2 * nc]
        sems = refs[7 + 2 * nc:]
        l, i = pl.program_id(0), pl.program_id(1)
        if nc:
            @pl.when((l == 0) & (i == 0))
            def _():
                comm.run(0, cin, cout, *sems)

            @pl.when((l == nl // 2) & (i == nr // 2))
            def _():
                comm.run(1, cin, cout, *sems)

        d_ref[...], nm_ref[...], nv_ref[...] = _adamw_math(w_ref[...], g_ref[...], m_ref[...], v_ref[...])
        if nc:
            @pl.when((l == nl - 1) & (i == nr - 1))
            def _():
                comm.run(2, cin, cout, *sems)

    spec = pl.BlockSpec((1, tr, c), lambda l, i: (l, i, 0))
    shp = jax.ShapeDtypeStruct((nl, r, c), F32)
    res = pl.pallas_call(
        body, name="adamw", grid=(nl, nr), in_specs=[spec] * 4 + comm.specs(),
        out_specs=(spec,) * 3 + tuple(comm.specs()), out_shape=(shp,) * 3 + tuple(comm.outs),
        input_output_aliases=comm.aliases(4, 3), scratch_shapes=comm.scratch() if nc else [],
        compiler_params=_cparams(("arbitrary", "arbitrary") if nc else ("parallel", "parallel")),
    )(w, g, m, v, *comm.arrays)
    return res[:3], comm.split(res[3:])


def _add_my_half(g, other, idx):
    _, _, h, c = g.shape
    th = _row_tile(h, 256)

    def body(idx_ref, g_ref, o_ref, out_ref):
        out_ref[...] = (g_ref[:, 0] + o_ref[...]).astype(BF16)

    return pl.pallas_call(
        body, name="add_my_half",
        grid_spec=pltpu.PrefetchScalarGridSpec(
            num_scalar_prefetch=1, grid=(4, h // th),
            in_specs=[pl.BlockSpec((1, 1, th, c), lambda j, i, idx_ref: (j, idx_ref[1], i, 0)),
                      pl.BlockSpec((1, th, c), lambda j, i, idx_ref: (j, i, 0))],
            out_specs=pl.BlockSpec((1, th, c), lambda j, i, idx_ref: (j, i, 0))),
        out_shape=jax.ShapeDtypeStruct(other.shape, BF16),
        compiler_params=_cparams(("parallel", "parallel")),
    )(idx, g, other)


def _sum_for_my_chip(g, other, got, idx):
    _, _, h, c = g.shape
    th = _row_tile(h, 256)

    def body(idx_ref, g_ref, o_ref, q_ref, out_ref):
        out_ref[0] = (((g_ref[0, 0] + o_ref[0]) + q_ref[0].astype(F32)) + q_ref[1].astype(F32)) + q_ref[2].astype(F32)

    return pl.pallas_call(
        body, name="sum_for_my_chip",
        grid_spec=pltpu.PrefetchScalarGridSpec(
            num_scalar_prefetch=1, grid=(h // th,),
            in_specs=[pl.BlockSpec((1, 1, th, c), lambda i, idx_ref: (idx_ref[0], idx_ref[1], i, 0)),
                      pl.BlockSpec((1, th, c), lambda i, idx_ref: (idx_ref[0], i, 0)),
                      pl.BlockSpec((3, th, c), lambda i, idx_ref: (0, i, 0))],
            out_specs=pl.BlockSpec((1, th, c), lambda i, idx_ref: (idx_ref[1], i, 0))),
        out_shape=jax.ShapeDtypeStruct((2, h, c), F32),
        compiler_params=_cparams(("parallel",)),
    )(idx, g, other, got)


class _Reducer:
    def __init__(self, idx):
        self.idx = idx
        self.groups = []

    def add(self, grads):
        group = {"state": 0, "g": [g.reshape(4, 2, g.shape[1] // 2, g.shape[2]) for g in grads]}
        self.groups.append(group)
        return group

    def steps(self):
        ops, owners = [], []
        for gr in self.groups:
            if gr["state"] == 0:
                ops.append(_Swap(gr["g"]))
            elif gr["state"] == 1:
                ops.append(_Exchange(gr["parts"]))
            elif gr["state"] == 2:
                ops.append(_Join(gr["bufs"]))
            else:
                continue
            owners.append(gr)
        return ops, owners

    def absorb(self, owners, results):
        for gr, res in zip(owners, results):
            if gr["state"] == 0:
                gr["other"] = res
                gr["parts"] = [_add_my_half(g, o, self.idx) for g, o in zip(gr["g"], res)]
            elif gr["state"] == 1:
                gr["bufs"] = [_sum_for_my_chip(g, o, q, self.idx) for g, o, q in zip(gr["g"], gr["other"], res)]
            else:
                gr["full"] = [b.reshape(2 * b.shape[1], b.shape[2]) for b in res]
            gr["state"] += 1

    def drain(self):
        while any(gr["state"] < 3 for gr in self.groups):
            ops, owners = self.steps()
            self.absorb(owners, _communicate(ops))


class _GatherRows:
    def __init__(self, bufs):
        self.arrays = list(bufs)
        self.outs = [jax.ShapeDtypeStruct(b.shape, b.dtype) for b in bufs]
        self.aliased = True
        self.n_sems = 7 * len(bufs)

    def run(self, phase, ins, outs, send_sems, recv_sems, base):
        x, y, c, chips = _place()
        me, sibling = (x, y, c), (x, y, 1 - c)
        for k, buf in enumerate(outs):
            def copy(i, block_of, to):
                blk = buf.at[4 * block_of[0] + 2 * block_of[1] + block_of[2]]
                return _remote(blk, blk, send_sems.at[base + 7 * k + i], recv_sems.at[base + 7 * k + i], to)

            if phase == 0:
                copy(0, me, sibling).start()
            for r, (px, py) in enumerate(chips):
                if phase == 0:
                    copy(1 + r, me, (px, py, c)).start()
                elif phase == 1:
                    copy(1 + r, (px, py, c), me).wait_recv()
                    copy(4 + r, (px, py, c), sibling).start()
                else:
                    copy(4 + r, (px, py, 1 - c), me).wait_recv()
                    copy(1 + r, me, (px, py, c)).wait_send()
                    copy(4 + r, (px, py, c), sibling).wait_send()
            if phase == 2:
                copy(0, sibling, me).wait_recv()
                copy(0, me, sibling).wait_send()


def _sum_devices(gathered):
    nd, m, n = gathered.shape

    def body(g_ref, o_ref):
        acc = g_ref[0]
        for b in range(1, nd):
            acc = acc + g_ref[b]
        o_ref[...] = acc

    return pl.pallas_call(
        body, name="sum_devices", out_shape=jax.ShapeDtypeStruct((m, n), F32),
        in_specs=[pl.BlockSpec(memory_space=pltpu.VMEM)], out_specs=pl.BlockSpec(memory_space=pltpu.VMEM),
        compiler_params=_cparams(),
    )(gathered)


def _ada_update(c_all, dmod_cols, w, m, v):
    nl, nd, ncol = dmod_cols.shape
    d = c_all.shape[1]
    tr = _row_tile(d, 256)

    def body(c_ref, dm_ref, w_ref, m_ref, v_ref, g_ref, d_ref, nm_ref, nv_ref):
        g = lax.dot_general(c_ref[...], dm_ref[0], (((0,), (0,)), ((), ())),
                            preferred_element_type=F32, precision=lax.Precision.HIGHEST)
        g_ref[0] = g
        d_ref[0], nm_ref[0], nv_ref[0] = _adamw_math(w_ref[0], g, m_ref[0], v_ref[0])

    spec = pl.BlockSpec((1, tr, ncol), lambda l, i: (l, i, 0))
    shp = jax.ShapeDtypeStruct((nl, d, ncol), F32)
    return pl.pallas_call(
        body, name="ada_update", grid=(nl, d // tr),
        in_specs=[pl.BlockSpec((nd, tr), lambda l, i: (0, i)), pl.BlockSpec((1, nd, ncol), lambda l, i: (l, 0, 0)),
                  spec, spec, spec],
        out_specs=(spec,) * 4, out_shape=(shp,) * 4, compiler_params=_cparams(("parallel", "parallel")),
    )(c_all, dmod_cols, w, m, v)


def _pad_rows(a, rows):
    return jnp.pad(a, ((0, rows - a.shape[0]), (0, 0)))


def _shard_cols(full, chip, width):
    return lax.dynamic_slice_in_dim(full, chip * width, width, axis=full.ndim - 1)


def kernel(x, c, ada_w, ada_b, pre_g, post_g, pool_w, pool_scale, cv_w_pw1, cv_b_pw1, cv_w_dw, cv_b_dw, cv_ln_g, cv_ln_b, cv_w_pw2, cv_b_pw2, ffn_w_up, ffn_w_dw, ffn_w_down, loss_target, m_ada_w, m_ada_b, m_pre_g, m_post_g, m_pool_w, m_pool_scale, m_cv_w_pw1, m_cv_b_pw1, m_cv_w_dw, m_cv_b_dw, m_cv_ln_g, m_cv_ln_b, m_cv_w_pw2, m_cv_b_pw2, m_ffn_w_up, m_ffn_w_dw, m_ffn_w_down, v_ada_w, v_ada_b, v_pre_g, v_post_g, v_pool_w, v_pool_scale, v_cv_w_pw1, v_cv_b_pw1, v_cv_w_dw, v_cv_b_dw, v_cv_ln_g, v_cv_ln_b, v_cv_w_pw2, v_cv_b_pw2, v_ffn_w_up, v_ffn_w_dw, v_ffn_w_down):
    s, d = x.shape[1], x.shape[2]
    dq = d // N_CHIPS
    n_g = pool_w.shape[1]
    gq = pool_w.shape[2]
    gd = pool_w.shape[3]
    kw = cv_w_dw.shape[1]
    cs = ffn_w_up.shape[2]
    fq = ffn_w_down.shape[1]
    chip = 2 * lax.axis_index("x") + lax.axis_index("y")
    core = lax.axis_index("c")
    chip1 = jnp.reshape(chip, (1,)).astype(jnp.int32)
    core1 = jnp.reshape(core, (1,)).astype(jnp.int32)
    xs, tgt = x[0], loss_target[0]

    small_rows = [pre_g.reshape(4, dq), post_g.reshape(4, dq), cv_w_dw[0], cv_b_dw, cv_ln_g, cv_ln_b, cv_b_pw2,
                  cv_b_pw1.reshape(2, dq)]
    small = jnp.concatenate(small_rows, axis=0)
    n_small = small.shape[0]
    small = _pad_rows(small, -(-n_small // 16) * 16)
    dwf = _pad_rows(ffn_w_dw.reshape(6, cs), 16)
    first = _AllGather([_cast_into_slot(pool_w.reshape(n_g * gq, gd), chip1), _cast_into_slot(small, chip1, dtype=F32),
                        _cast_into_slot(dwf, chip1, dtype=F32), _cast_into_slot(ffn_w_up[0], chip1)])
    (c_rep, mod_rep), ((g_pool, g_small, g_dwf, g_up0),) = _ada_forward(c, ada_w, _Comm([first]))
    c_all = c_rep[:, 0, :]
    mod = mod_rep[:, :, 0, :].transpose(1, 0, 2).reshape(ada_b.shape) + ada_b
    second = _AllGather([_cast_into_slot(ffn_w_down[0], chip1)])
    later = _AllGather([_cast_into_slot(cv_w_pw1[0], chip1), _cast_into_slot(cv_w_pw2[0], chip1),
                        _cast_into_slot(ffn_w_up[1], chip1), _cast_into_slot(ffn_w_down[1], chip1)])
    poolw_full = g_pool.reshape(N_CHIPS, n_g, gq, gd).transpose(1, 0, 2, 3).reshape(n_g, gd, gd)
    smallf = g_small.transpose(1, 0, 2).reshape(g_small.shape[1], d)
    pre_full, post_full = smallf[0:4].reshape(2, 2, d), smallf[4:8].reshape(2, 2, d)
    wdw31 = smallf[8:8 + kw]
    o = 8 + kw
    b_dw, ln_g, ln_b, b_pw2 = smallf[o:o + 1], smallf[o + 1:o + 2], smallf[o + 2:o + 3], smallf[o + 3:o + 4]
    b_pw1 = g_small[:, o + 4:o + 6, :].reshape(1, 2 * d)
    ffn_dw = g_dwf[:, :6, :].transpose(1, 0, 2).reshape(2, 3, N_CHIPS * cs)

    def sub_vec(layer, sub, extra=None):
        m6 = mod[layer].reshape(6, d)
        rows = [pre_full[layer, sub][None], 1.0 + m6[3 * sub + 1][None], m6[3 * sub][None], m6[3 * sub + 2][None],
                post_full[layer, sub][None]]
        if extra is not None:
            rows.append(extra)
        return _pad_rows(jnp.concatenate(rows, axis=0), 8)

    vec_pool = sub_vec(0, 0, pool_scale)
    vec_f0, vec_conv, vec_f1 = sub_vec(0, 1), sub_vec(1, 0), sub_vec(1, 1)
    cvec = _pad_rows(jnp.concatenate([b_dw, ln_g, ln_b, b_pw2], axis=0), 8)

    x1, ((g_dn0,),) = _pool_forward(xs, vec_pool, poolw_full, _Comm([second]))
    w_up0, w_dn0 = g_up0, g_dn0.reshape(2, 2 * fq, d)
    (x2, h_f0, a0_f0, cc_f0, u_f0, y_f0), ((g_pw1, g_pw2, g_up1, g_dn1),) = _ffn_forward(
        x1, vec_f0, w_up0, ffn_dw[0], w_dn0, _Comm([later]))
    pw2_full = g_pw2.reshape(d, d)
    w_up1, w_dn1 = g_up1, g_dn1.reshape(2, 2 * fq, d)
    x3, h_cv, a_cv, uc_cv, z_cv, y_cv = _conv_forward(x2, vec_conv, cvec, g_pw1, b_pw1, wdw31, pw2_full)
    (dx4, h_f1, a0_f1, cc_f1, u_f1, y_f1, loss_rows), _ = _ffn_forward(x3, vec_f1, w_up1, ffn_dw[1], w_dn1, target=tgt)

    dx3, da0_f1, dy_f1, sum_f1, dwdw_f1 = _ffn_backward(dx4, x3, y_f1, a0_f1, cc_f1, vec_f1, w_up1, ffn_dw[1], w_dn1)
    dx2, da_cv, dy_cv, sum_cv, dwdw_cv = _conv_backward(dx3, x2, y_cv, a_cv, uc_cv, vec_conv, cvec, g_pw1, wdw31, pw2_full)
    dx1, da0_f0, dy_f0, sum_f0, dwdw_f0 = _ffn_backward(dx2, x1, y_f0, a0_f0, cc_f0, vec_f0, w_up0, ffn_dw[0], w_dn0)
    dx0, sum_pool, gw_pool = _pool_backward(dx1, xs, vec_pool, poolw_full)
    gw_pool4 = gw_pool.reshape(n_g, N_CHIPS, gq, gd).transpose(1, 0, 2, 3).reshape(N_CHIPS, n_g * gq, gd)

    slab = jnp.concatenate([sum_f1, sum_cv, dwdw_cv, sum_f0, sum_pool, loss_rows], axis=0)
    wide = jnp.concatenate([dwdw_f1, dwdw_f0], axis=0)
    n_slab = slab.shape[0]
    mine = jnp.concatenate([slab, wide.reshape(-1, d)], axis=0)
    rows_of_all = _cast_into_slot(mine, 2 * chip1 + core1, N_DEV, F32)
    red = _Reducer(jnp.concatenate([chip1, core1]))

    def carried(call, *args, extra=()):
        ops, owners = red.steps()
        out, results = call(*args, _Comm(ops + list(extra)))
        red.absorb(owners, results[:len(ops)])
        return out, results[len(ops):]

    gw_up1, ((both_all,),) = carried(_weight_grad, h_f1[None], da0_f1, extra=[_GatherRows([rows_of_all])])
    r_up1 = red.add([gw_up1])
    r_up0 = red.add([carried(_weight_grad, h_f0[None], da0_f0)[0]])
    r_dn1 = red.add([carried(_weight_grad, u_f1, dy_f1[None])[0].reshape(N_CHIPS, fq, d)])
    r_dn0 = red.add([carried(_weight_grad, u_f0, dy_f0[None])[0].reshape(N_CHIPS, fq, d)])
    r_pw1 = red.add([carried(_weight_grad, h_cv[None], da_cv)[0]])
    r_last = red.add([carried(_weight_grad, z_cv[None], dy_cv[None])[0].reshape(N_CHIPS, dq, d), gw_pool4])

    tot_both = _sum_devices(both_all)
    slab_all, tot = both_all[:, :n_slab], tot_both[:n_slab]
    tot_wide = tot_both[n_slab:].reshape(wide.shape)
    kpad = dwdw_cv.shape[0]
    o_cv, o_dw, o_f0 = 8, 24, 24 + kpad
    o_pool, o_loss = o_f0 + 8, o_f0 + 16
    loss = jnp.sum(tot[o_loss])
    dmod_l0 = jnp.concatenate([slab_all[:, o_pool + 4], slab_all[:, o_pool + 3], slab_all[:, o_pool + 1],
                               slab_all[:, o_f0 + 4], slab_all[:, o_f0 + 3], slab_all[:, o_f0 + 1]], axis=-1)
    dmod_l1 = jnp.concatenate([slab_all[:, o_cv + 4], slab_all[:, o_cv + 3], slab_all[:, o_cv + 1],
                               slab_all[:, 4], slab_all[:, 3], slab_all[:, 1]], axis=-1)
    dmod = jnp.stack([dmod_l0, dmod_l1], axis=0)
    g_ada_b = _sum_devices(dmod.transpose(1, 0, 2))
    ada_w_step = _ada_update(c_all, _shard_cols(dmod, chip, ada_w.shape[2]), ada_w, m_ada_w, v_ada_w)

    g_pre = jnp.stack([jnp.stack([tot[o_pool + 2], tot[o_f0 + 2]]), jnp.stack([tot[o_cv + 2], tot[2]])])
    g_post = jnp.stack([jnp.stack([tot[o_pool + 0], tot[o_f0 + 0]]), jnp.stack([tot[o_cv + 0], tot[0]])])
    g_pool_scale = tot[o_pool + 5][None]
    g_b_pw2, g_ln_g, g_ln_b, g_b_dw = tot[o_cv + 5], tot[o_cv + 6], tot[o_cv + 7], tot[o_cv + 8]
    g_b_pw1 = jnp.concatenate([tot[o_cv + 9], tot[o_cv + 10]])
    g_w_dw31 = tot[o_dw:o_dw + kw]
    g_ffn_dw = jnp.stack([tot_wide[8:11], tot_wide[0:3]])

    grads_small = {
        "pre_g": _shard_cols(g_pre, chip, dq), "post_g": _shard_cols(g_post, chip, dq),
        "pool_scale": g_pool_scale, "cv_b_pw1": _shard_cols(g_b_pw1[None], chip, 2 * dq),
        "cv_w_dw": _shard_cols(g_w_dw31[None], chip, dq), "cv_b_dw": _shard_cols(g_b_dw[None], chip, dq),
        "cv_ln_g": _shard_cols(g_ln_g[None], chip, dq), "cv_ln_b": _shard_cols(g_ln_b[None], chip, dq),
        "cv_b_pw2": _shard_cols(g_b_pw2[None], chip, dq), "ffn_w_dw": _shard_cols(g_ffn_dw, chip, cs),
        "ada_b": g_ada_b,
    }
    params_small = {
        "pre_g": (pre_g, m_pre_g, v_pre_g), "post_g": (post_g, m_post_g, v_post_g),
        "pool_scale": (pool_scale, m_pool_scale, v_pool_scale), "cv_b_pw1": (cv_b_pw1, m_cv_b_pw1, v_cv_b_pw1),
        "cv_w_dw": (cv_w_dw, m_cv_w_dw, v_cv_w_dw), "cv_b_dw": (cv_b_dw, m_cv_b_dw, v_cv_b_dw),
        "cv_ln_g": (cv_ln_g, m_cv_ln_g, v_cv_ln_g), "cv_ln_b": (cv_ln_b, m_cv_ln_b, v_cv_ln_b),
        "cv_b_pw2": (cv_b_pw2, m_cv_b_pw2, v_cv_b_pw2), "ffn_w_dw": (ffn_w_dw, m_ffn_w_dw, v_ffn_w_dw),
        "ada_b": (ada_b, m_ada_b, v_ada_b),
    }
    names = list(params_small)
    small_g = {nm: grads_small[nm].reshape(params_small[nm][0].shape) for nm in names}
    updated = _adamw_many([(params_small[nm][0], small_g[nm], params_small[nm][1], params_small[nm][2])
                           for nm in names])
    small_d = {nm: u[0] for nm, u in zip(names, updated)}
    small_m = {nm: u[1] for nm, u in zip(names, updated)}
    small_v = {nm: u[2] for nm, u in zip(names, updated)}

    red.drain()
    big_p = {
        "ada_w": (ada_w, m_ada_w, v_ada_w), "pool_w": (pool_w, m_pool_w, v_pool_w),
        "cv_w_pw1": (cv_w_pw1, m_cv_w_pw1, v_cv_w_pw1), "cv_w_pw2": (cv_w_pw2, m_cv_w_pw2, v_cv_w_pw2),
        "ffn_w_up": (ffn_w_up, m_ffn_w_up, v_ffn_w_up), "ffn_w_down": (ffn_w_down, m_ffn_w_down, v_ffn_w_down),
    }
    big_g, big_d, big_m, big_v = {}, {}, {}, {}

    def update(nm, grad):
        w, m, v = big_p[nm]
        as3 = lambda t: t.reshape((-1,) + w.shape[-2:])
        (dl, nm_, nv_), _ = _adamw(as3(w), as3(grad), as3(m), as3(v))
        big_g[nm] = grad.reshape(w.shape)
        big_d[nm], big_m[nm], big_v[nm] = dl.reshape(w.shape), nm_.reshape(w.shape), nv_.reshape(w.shape)

    full = lambda group, k=0: group["full"][k]
    update("ffn_w_up", jnp.stack([full(r_up0), full(r_up1)]))
    big_g["ada_w"], big_d["ada_w"], big_m["ada_w"], big_v["ada_w"] = ada_w_step
    update("ffn_w_down", jnp.stack([full(r_dn0), full(r_dn1)]))
    update("cv_w_pw1", full(r_pw1))
    update("cv_w_pw2", full(r_last, 0))
    update("pool_w", full(r_last, 1))

    order = ["ada_w", "ada_b", "pre_g", "post_g", "pool_w", "pool_scale", "cv_w_pw1", "cv_b_pw1", "cv_w_dw", "cv_b_dw",
             "cv_ln_g", "cv_ln_b", "cv_w_pw2", "cv_b_pw2", "ffn_w_up", "ffn_w_dw", "ffn_w_down"]
    pick = lambda bigs, smalls: [bigs[nm] if nm in bigs else smalls[nm] for nm in order]
    return (loss, dx0[None], *pick(big_g, small_g), *pick(big_d, small_d), *pick(big_m, small_m),
            *pick(big_v, small_v))
```

```python
import functools

import jax
import jax.numpy as jnp
from jax import lax
from jax.experimental import pallas as pl
from jax.experimental.pallas import tpu as pltpu

F32 = jnp.float32
BF16 = jnp.bfloat16
EPS = 1e-6
N_CHIPS = 4
N_DEV = 8
POOL_WINDOWS = (2, 4, 8, 16)
POOL_HALO = 16
FFN_HALO = 16
MXU_LANES = 256
CONV_ROWS, CONV_LANES = 128, 128
ADAM_LR = 0.001
ADAM_B1 = 0.9
ADAM_B2 = 0.999
ADAM_EPS = 1e-08
ADAM_WD = 0.01
ADAM_STEP = 10
V7X_VMEM_LIMIT = 58 * 1024 * 1024
MESH = pl.DeviceIdType.MESH


def _cparams(sem=None, vmem=V7X_VMEM_LIMIT):
    return pltpu.CompilerParams(dimension_semantics=sem, vmem_limit_bytes=vmem)


def _row_tile(n, want):
    if n <= want:
        return n
    t = want - want % 8
    while n % t:
        t -= 8
    return t


def _lane_chunks(width):
    out, c = [], 0
    while c < width:
        w = min(512, width - c)
        out.append((c, w))
        c += w
    return out


def _dot(a, b):
    return jnp.dot(a, b, preferred_element_type=F32)


def _dot_nt(a, b):
    return lax.dot_general(a, b, (((1,), (1,)), ((), ())), preferred_element_type=F32)


def _store_dot_nt(dst, a_ref, b_ref):
    dst[...] = _dot_nt(a_ref[...], b_ref[...])


def _store_dot_nt2(dst, a1_ref, a2_ref, b1_ref, b2_ref):
    dst[...] = _dot_nt(a1_ref[...], b1_ref[...]) + _dot_nt(a2_ref[...], b2_ref[...])


def _dot_tn(a, b):
    return lax.dot_general(a, b, (((0,), (0,)), ((), ())), preferred_element_type=F32)


def _rms(x):
    r = lax.rsqrt(jnp.mean(x * x, axis=-1, keepdims=True) + EPS)
    return x * r, r


def _rms_bwd(dyn, yn, r):
    return r * (dyn - yn * jnp.mean(dyn * yn, axis=-1, keepdims=True))


def _sigmoid(x):
    return 0.5 * jnp.tanh(0.5 * x) + 0.5


def _colsum(x):
    return jnp.sum(x, axis=0, keepdims=True)


def _shift_down(x, k):
    return x if k == 0 else pltpu.roll(x, k, 0)


def _shift_up(x, k):
    return x if k == 0 else pltpu.roll(x, x.shape[0] - k, 0)


def _vec_rows(vec):
    return vec[0:1] * vec[1:2], vec[2:3], vec[3:4], vec[4:5]


def _norm_sums(do, yn, dh, xn, vec):
    p, q = _colsum(do * yn), _colsum(dh * xn)
    return [p * vec[3:4], p * vec[4:5], q * vec[1:2], q * vec[0:1]]


def _add_rows(sum_ref, rows):
    for k, r in enumerate(rows):
        sum_ref[k:k + 1, :] += r


def _ada_forward(c, ada_w, comm=None):
    n_layers, d, ncol = ada_w.shape
    comm = comm or _Comm([])
    nc = len(comm.arrays)

    def body(*refs):
        c_ref, w_ref = refs[:2]
        cin = refs[2:2 + nc]
        call_ref, mod_ref = refs[2 + nc:4 + nc]
        cout = refs[4 + nc:4 + 2 * nc]
        part_ref, sendbuf, send_sems, recv_sems, send2, recv2 = refs[4 + 2 * nc:10 + 2 * nc]
        carried_sems = refs[10 + 2 * nc:]
        if nc:
            comm.run(0, cin, cout, *carried_sems)
        x, y, cc = lax.axis_index("x"), lax.axis_index("y"), lax.axis_index("c")
        me = 4 * x + 2 * y + cc
        rel = [(x, y, 1 - cc), (1 - x, y, cc), (x, 1 - y, cc), (1 - x, 1 - y, cc),
               (1 - x, y, 1 - cc), (x, 1 - y, 1 - cc), (1 - x, 1 - y, 1 - cc)]
        cv = c_ref[...]
        call_ref[me] = jnp.broadcast_to(cv * _sigmoid(cv), (8, d))

        def gather(k, block, to):
            blk = call_ref.at[block]
            return pltpu.make_async_remote_copy(src_ref=blk, dst_ref=blk, send_sem=send_sems.at[k],
                                                recv_sem=recv_sems.at[k], device_id=to, device_id_type=MESH)

        for k, to in enumerate(rel):
            gather(k, me, to).start()
        for k, (px, py, pc) in enumerate(rel):
            gather(k, 4 * px + 2 * py + pc, rel[k]).wait_recv()
        for k, to in enumerate(rel):
            gather(k, me, to).wait_send()

        ca = call_ref[...].reshape(8 * N_DEV, d)
        for l in range(n_layers):
            part_ref[l] = jnp.dot(ca, w_ref[l], preferred_element_type=F32, precision=lax.Precision.HIGHEST)

        j = 2 * x + y
        chips = [(1 - x, y), (x, 1 - y), (1 - x, 1 - y)]

        def rows_of(b):
            return part_ref[:, pl.ds(pl.multiple_of(8 * b, 8), 8), :]

        def scatter(k, src_j, to):
            return pltpu.make_async_remote_copy(
                src_ref=sendbuf.at[k], dst_ref=mod_ref.at[src_j], send_sem=send2.at[k], recv_sem=recv2.at[k],
                device_id=to, device_id_type=MESH)

        mod_ref[j] = rows_of(me)
        for k, (px, py) in enumerate(chips):
            sendbuf[k] = rows_of(4 * px + 2 * py + cc)
            scatter(k, j, (px, py, cc)).start()
        for k, (px, py) in enumerate(chips):
            scatter(k, 2 * px + py, (px, py, cc)).wait_recv()
        for k, (px, py) in enumerate(chips):
            scatter(k, j, (px, py, cc)).wait_send()
        if nc:
            comm.run(1, cin, cout, *carried_sems)
            comm.run(2, cin, cout, *carried_sems)

    vm = pl.BlockSpec(memory_space=pltpu.VMEM)
    res = pl.pallas_call(
        body, name="ada_forward",
        out_shape=(jax.ShapeDtypeStruct((N_DEV, 8, d), F32), jax.ShapeDtypeStruct((N_CHIPS, n_layers, 8, ncol), F32),
                   *comm.outs),
        in_specs=[vm, vm] + comm.specs(), out_specs=(vm, vm, *comm.specs()),
        input_output_aliases=comm.aliases(2, 2),
        scratch_shapes=[pltpu.VMEM((n_layers, 8 * N_DEV, ncol), F32), pltpu.VMEM((3, n_layers, 8, ncol), F32),
                        pltpu.SemaphoreType.DMA((7,)), pltpu.SemaphoreType.DMA((7,)),
                        pltpu.SemaphoreType.DMA((3,)), pltpu.SemaphoreType.DMA((3,))] + (comm.scratch() if nc else []),
        compiler_params=_cparams(),
    )(c, ada_w, *comm.arrays)
    return res[:2], comm.split(res[2:])


def _cast_into_slot(w2d, slot, n_slots=N_CHIPS, dtype=None):
    r, c = w2d.shape
    tr = _row_tile(r, 256)
    dtype = dtype or BF16

    def body(slot_ref, w_ref, o_ref):
        o_ref[0] = w_ref[...].astype(dtype)

    return pl.pallas_call(
        body, name="cast_into_slot",
        grid_spec=pltpu.PrefetchScalarGridSpec(
            num_scalar_prefetch=1, grid=(r // tr,),
            in_specs=[pl.BlockSpec((tr, c), lambda i, slot_ref: (i, 0))],
            out_specs=pl.BlockSpec((1, tr, c), lambda i, slot_ref: (slot_ref[0], i, 0))),
        out_shape=jax.ShapeDtypeStruct((n_slots, r, c), dtype), compiler_params=_cparams(("parallel",)),
    )(slot, w2d)


def _place():
    x, y, c = lax.axis_index("x"), lax.axis_index("y"), lax.axis_index("c")
    return x, y, c, [(1 - x, y), (x, 1 - y), (1 - x, 1 - y)]


def _remote(src, dst, send_sem, recv_sem, to):
    return pltpu.make_async_remote_copy(src_ref=src, dst_ref=dst, send_sem=send_sem, recv_sem=recv_sem,
                                        device_id=to, device_id_type=MESH)


class _AllGather:
    def __init__(self, bufs):
        self.arrays = list(bufs)
        self.outs = [jax.ShapeDtypeStruct(b.shape, b.dtype) for b in bufs]
        self.aliased = True
        self.n_sems = 6 * len(bufs)

    def run(self, phase, ins, outs, send_sems, recv_sems, base):
        x, y, c, chips = _place()
        j = 2 * x + y
        for k, buf in enumerate(outs):
            half = buf.shape[1] // 2

            def part(src_j, h):
                return buf.at[src_j, pl.ds(h * half, half), :]

            def ici(r, src_j, to):
                s = base + 6 * k + r
                return _remote(part(src_j, c), part(src_j, c), send_sems.at[s], recv_sems.at[s], to)

            def d2d(r, src_j, h):
                s = base + 6 * k + 3 + r
                return _remote(part(src_j, h), part(src_j, h), send_sems.at[s], recv_sems.at[s], (x, y, 1 - c))

            for r, (px, py) in enumerate(chips):
                if phase == 0:
                    ici(r, j, (px, py, c)).start()
                elif phase == 1:
                    ici(r, 2 * px + py, (px, py, c)).wait_recv()
                    d2d(r, 2 * px + py, c).start()
                else:
                    d2d(r, 2 * px + py, 1 - c).wait_recv()
                    ici(r, j, (px, py, c)).wait_send()
                    d2d(r, 2 * px + py, c).wait_send()


class _Swap:
    def __init__(self, grads):
        self.arrays = list(grads)
        self.outs = [jax.ShapeDtypeStruct((g.shape[0],) + g.shape[2:], g.dtype) for g in grads]
        self.aliased = False
        self.n_sems = len(grads)

    def run(self, phase, ins, outs, send_sems, recv_sems, base):
        x, y, c, _ = _place()
        for k in range(len(ins)):
            cp = _remote(ins[k].at[:, 1 - c], outs[k], send_sems.at[base + k], recv_sems.at[base + k], (x, y, 1 - c))
            if phase == 0:
                cp.start()
            elif phase == 2:
                cp.wait()


class _Exchange:
    def __init__(self, parts):
        self.arrays = list(parts)
        self.outs = [jax.ShapeDtypeStruct((3,) + p.shape[1:], p.dtype) for p in parts]
        self.aliased = False
        self.n_sems = 3 * len(parts)

    def run(self, phase, ins, outs, send_sems, recv_sems, base):
        x, y, c, chips = _place()
        for k in range(len(ins)):
            for r, (px, py) in enumerate(chips):
                s = base + 3 * k + r
                cp = _remote(ins[k].at[2 * px + py], outs[k].at[r], send_sems.at[s], recv_sems.at[s], (px, py, c))
                if phase == 0:
                    cp.start()
                elif phase == 2:
                    cp.wait()


class _Join:
    def __init__(self, bufs):
        self.arrays = list(bufs)
        self.outs = [jax.ShapeDtypeStruct(b.shape, b.dtype) for b in bufs]
        self.aliased = True
        self.n_sems = len(bufs)

    def run(self, phase, ins, outs, send_sems, recv_sems, base):
        x, y, c, _ = _place()
        for k, buf in enumerate(outs):
            mine = _remote(buf.at[c], buf.at[c], send_sems.at[base + k], recv_sems.at[base + k], (x, y, 1 - c))
            if phase == 0:
                mine.start()
            elif phase == 2:
                mine.wait_send()
                _remote(buf.at[1 - c], buf.at[1 - c], send_sems.at[base + k], recv_sems.at[base + k],
                        (x, y, 1 - c)).wait_recv()


class _Comm:
    def __init__(self, ops):
        self.ops = list(ops)
        self.arrays = [a for op in self.ops for a in op.arrays]
        self.outs = [o for op in self.ops for o in op.outs]
        self.n_sems = sum(op.n_sems for op in self.ops)

    def specs(self):
        return [pl.BlockSpec(memory_space=pl.ANY)] * len(self.arrays)

    def aliases(self, first_in, first_out):
        out, k = {}, 0
        for op in self.ops:
            for i in range(len(op.arrays)):
                if op.aliased:
                    out[first_in + k + i] = first_out + k + i
            k += len(op.arrays)
        return out

    def scratch(self):
        return [pltpu.SemaphoreType.DMA((self.n_sems,)), pltpu.SemaphoreType.DMA((self.n_sems,))]

    def run(self, phase, ins, outs, send_sems, recv_sems):
        k = base = 0
        for op in self.ops:
            n = len(op.arrays)
            op.run(phase, ins[k:k + n], outs[k:k + n], send_sems, recv_sems, base)
            k += n
            base += op.n_sems

    def split(self, results):
        out, k = [], 0
        for op in self.ops:
            out.append(list(results[k:k + len(op.arrays)]))
            k += len(op.arrays)
        return out


def _communicate(ops):
    comm = _Comm(ops)
    n = len(comm.arrays)

    def body(*refs):
        ins, outs, (send_sems, recv_sems) = refs[:n], refs[n:2 * n], refs[2 * n:]
        for phase in range(3):
            comm.run(phase, ins, outs, send_sems, recv_sems)

    res = pl.pallas_call(
        body, name="communicate", out_shape=tuple(comm.outs), in_specs=comm.specs(), out_specs=tuple(comm.specs()),
        input_output_aliases=comm.aliases(0, 0), scratch_shapes=comm.scratch(),
    )(*comm.arrays)
    return comm.split(res)


def _pool_core(he, w_ref, scale, first_row, halo, n_rows):
    d = he.shape[1]
    gd = d // len(POOL_WINDOWS)
    t = first_row + lax.broadcasted_iota(jnp.int32, (n_rows, 1), 0)
    pooled, ypre, cnts = [], [], []
    for g, w in enumerate(POOL_WINDOWS):
        hg = he[:, g * gd:(g + 1) * gd]
        s, k = hg, 1
        while k < w:
            s = s + _shift_down(s, k)
            k *= 2
        cnt = jnp.minimum(t + 1, w).astype(F32)
        p = s[halo:] / cnt - hg[halo:]
        pooled.append(p.astype(BF16))
        cnts.append(cnt)
        ypre.append(_dot(pooled[-1], w_ref[g]))
    return pooled, jnp.concatenate(ypre, axis=1), cnts


def _pool_forward(x, vec, pool_w, comm=None):
    s, d = x.shape
    ts = _row_tile(s, 512)
    nb = s // ts
    n_g, gd, _ = pool_w.shape
    comm = comm or _Comm([])
    nc = len(comm.arrays)

    def body(*refs):
        x_ref, vec_ref, w_ref = refs[:3]
        cin = refs[3:3 + nc]
        o_ref = refs[3 + nc]
        cout = refs[4 + nc:4 + 2 * nc]
        carry = refs[4 + 2 * nc]
        sems = refs[5 + 2 * nc:]
        i = pl.program_id(0)

        @pl.when(i == 0)
        def _():
            carry[...] = jnp.zeros_like(carry)
            if nc:
                comm.run(0, cin, cout, *sems)

        if nc:
            @pl.when(i == nb - 1)
            def _():
                comm.run(1, cin, cout, *sems)

        vec = vec_ref[...]
        a, sh, gt, gpost = _vec_rows(vec)
        xb = x_ref[...]
        xn, _ = _rms(xb)
        h = xn * a + sh
        he = jnp.concatenate([carry[...], h], axis=0)
        carry[...] = h[ts - POOL_HALO:]
        _, ypre, _ = _pool_core(he, w_ref, vec[5:6], i * ts, POOL_HALO, ts)
        yn, _ = _rms(ypre * vec[5:6])
        o_ref[...] = xb + gt * (yn * gpost)
        if nc:
            @pl.when(i == nb - 1)
            def _():
                comm.run(2, cin, cout, *sems)

    res = pl.pallas_call(
        body, name="pool_forward", grid=(nb,),
        in_specs=[pl.BlockSpec((ts, d), lambda i: (i, 0)), pl.BlockSpec((8, d), lambda i: (0, 0)),
                  pl.BlockSpec((n_g, gd, gd), lambda i: (0, 0, 0))] + comm.specs(),
        out_specs=(pl.BlockSpec((ts, d), lambda i: (i, 0)), *comm.specs()),
        out_shape=(jax.ShapeDtypeStruct((s, d), F32), *comm.outs),
        input_output_aliases=comm.aliases(3, 1),
        scratch_shapes=[pltpu.VMEM((POOL_HALO, d), F32)] + (comm.scratch() if nc else []),
        compiler_params=_cparams(("arbitrary",)),
    )(x, vec, pool_w, *comm.arrays)
    return res[0], comm.split(res[1:])


def _pool_backward(dout, x, vec, pool_w):
    s, d = x.shape
    ts = _row_tile(s, 512)
    nb = s // ts
    hb = ts // POOL_HALO
    n_g, gd, _ = pool_w.shape

    def body(do_ref, x_ref, xh_ref, vec_ref, w_ref, dx_ref, sum_ref, dw_ref, carry):
        step = pl.program_id(0)
        i = nb - 1 - step

        @pl.when(step == 0)
        def _():
            carry[...] = jnp.zeros_like(carry)
            sum_ref[...] = jnp.zeros_like(sum_ref)
            dw_ref[...] = jnp.zeros_like(dw_ref)

        vec = vec_ref[...]
        a, sh, gt, gpost = _vec_rows(vec)
        scale = vec[5:6]
        do = do_ref[...]
        xe = jnp.concatenate([xh_ref[...], x_ref[...]], axis=0)
        xne, re = _rms(xe)
        he = xne * a + sh
        rowid = lax.broadcasted_iota(jnp.int32, (POOL_HALO + ts, 1), 0)
        he = jnp.where((rowid >= POOL_HALO) | (i > 0), he, 0.0)
        xn, r = xne[POOL_HALO:], re[POOL_HALO:]
        pooled, ypre, cnts = _pool_core(he, w_ref, scale, i * ts, POOL_HALO, ts)
        yn, ry = _rms(ypre * scale)
        dyn = do * (gt * gpost)
        dy = _rms_bwd(dyn, yn, ry)
        dypre = (dy * scale).astype(BF16)
        dh_parts, q_parts = [], []
        for g, w in enumerate(POOL_WINDOWS):
            dyg = dypre[:, g * gd:(g + 1) * gd]
            dpool = _dot_nt(dyg, w_ref[g])
            dw_ref[g] += _dot_tn(pooled[g], dyg)
            q = dpool / cnts[g]
            qe = jnp.concatenate([q, carry[:, g * gd:(g + 1) * gd]], axis=0)
            acc, k = qe, 1
            while k < w:
                acc = acc + _shift_up(acc, k)
                k *= 2
            dh_parts.append(acc[:ts] - dpool)
            q_parts.append(q[:POOL_HALO])
        carry[...] = jnp.concatenate(q_parts, axis=1)
        dh = jnp.concatenate(dh_parts, axis=1)
        dxn = dh * a
        dx_ref[...] = do + _rms_bwd(dxn, xn, r)
        _add_rows(sum_ref, _norm_sums(do, yn, dh, xn, vec) + [_colsum(dh), _colsum(dy * ypre)])

    blk = lambda st: (nb - 1 - st, 0)
    return pl.pallas_call(
        body, name="pool_backward", grid=(nb,),
        in_specs=[pl.BlockSpec((ts, d), blk), pl.BlockSpec((ts, d), blk),
                  pl.BlockSpec((POOL_HALO, d), lambda st: (jnp.maximum((nb - 1 - st) * hb - 1, 0), 0)),
                  pl.BlockSpec((8, d), lambda st: (0, 0)), pl.BlockSpec((n_g, gd, gd), lambda st: (0, 0, 0))],
        out_specs=(pl.BlockSpec((ts, d), blk), pl.BlockSpec((8, d), lambda st: (0, 0)),
                   pl.BlockSpec((n_g, gd, gd), lambda st: (0, 0, 0))),
        out_shape=(jax.ShapeDtypeStruct((s, d), F32), jax.ShapeDtypeStruct((8, d), F32),
                   jax.ShapeDtypeStruct((n_g, gd, gd), F32)),
        scratch_shapes=[pltpu.VMEM((POOL_HALO, d), F32)],
        compiler_params=_cparams(("arbitrary",)),
    )(dout, x, x, vec, pool_w)


def _ffn_forward(x, vec, w_up, w_dw, w_down, comm=None, target=None):
    s, d = x.shape
    _, _, cs = w_up.shape
    ts = _row_tile(s, 256)
    nb = s // ts
    chunks = _lane_chunks(cs)
    comm = comm or _Comm([])
    nc = len(comm.arrays)
    nl = 0 if target is None else 1
    n_in, n_out = 5 + nl, 6 + nl

    def body(*refs):
        x_ref, vec_ref, wup_ref, wdw_ref, wdn_ref = refs[:5]
        cin = refs[n_in:n_in + nc]
        o_ref, h_ref, a0_ref, cc_ref, u_ref, y_ref = refs[n_in + nc:n_in + nc + 6]
        loss_ref = refs[n_in + nc + 6] if nl else None
        cout = refs[n_in + nc + n_out:n_in + 2 * nc + n_out]
        carry = refs[n_in + 2 * nc + n_out]
        sems = refs[n_in + 2 * nc + n_out + 1:]
        i = pl.program_id(0)

        @pl.when(i == 0)
        def _():
            carry[...] = jnp.zeros_like(carry)
            if nl:
                loss_ref[...] = jnp.zeros_like(loss_ref)
            if nc:
                comm.run(0, cin, cout, *sems)

        if nc:
            @pl.when(i == (3 * nb) // 4)
            def _():
                comm.run(1, cin, cout, *sems)

        vec = vec_ref[...]
        a, sh, gt, gpost = _vec_rows(vec)
        xb = x_ref[...]
        xn, _ = _rms(xb)
        hb = (xn * a + sh).astype(BF16)
        h_ref[...] = hb
        for q in range(2):
            for c0, cw in chunks:
                conv = []
                for j in (q, q + 2):
                    a0 = _dot(hb, wup_ref[j, :, c0:c0 + cw])
                    a0_ref[j, :, c0:c0 + cw] = a0.astype(BF16)
                    ae = jnp.concatenate([carry[j, :, c0:c0 + cw], a0], axis=0)
                    carry[j, :, c0:c0 + cw] = a0[ts - FFN_HALO:]
                    w = wdw_ref[:, j * cs + c0:j * cs + c0 + cw]
                    conv.append((w[2:3] * ae + w[1:2] * _shift_down(ae, 1) + w[0:1] * _shift_down(ae, 2))[FFN_HALO:])
                    cc_ref[j, :, c0:c0 + cw] = conv[-1].astype(BF16)
                u_ref[q, :, c0:c0 + cw] = (conv[0] * _sigmoid(conv[0]) * conv[1]).astype(BF16)
        y = _dot(u_ref[0], wdn_ref[0]) + _dot(u_ref[1], wdn_ref[1])
        y_ref[...] = y
        yn, _ = _rms(y)
        x_out = xb + gt * (yn * gpost)
        if nl:
            err = x_out - refs[5][...]
            o_ref[...] = err * (1.0 / d)
            loss_ref[0:1, :] += _colsum(err * err) * (0.5 / d)
        else:
            o_ref[...] = x_out
        if nc:
            @pl.when(i == nb - 1)
            def _():
                comm.run(2, cin, cout, *sems)

    const3 = lambda i: (0, 0, 0)
    res = pl.pallas_call(
        body, name="ffn_forward", grid=(nb,),
        in_specs=[pl.BlockSpec((ts, d), lambda i: (i, 0)), pl.BlockSpec((8, d), lambda i: (0, 0)),
                  pl.BlockSpec(w_up.shape, const3, pipeline_mode=pl.Buffered(1)),
                  pl.BlockSpec(w_dw.shape, lambda i: (0, 0)),
                  pl.BlockSpec(w_down.shape, const3, pipeline_mode=pl.Buffered(1))]
        + [pl.BlockSpec((ts, d), lambda i: (i, 0))] * nl + comm.specs(),
        out_specs=(pl.BlockSpec((ts, d), lambda i: (i, 0)), pl.BlockSpec((ts, d), lambda i: (i, 0)),
                   pl.BlockSpec((4, ts, cs), lambda i: (0, i, 0)), pl.BlockSpec((4, ts, cs), lambda i: (0, i, 0)),
                   pl.BlockSpec((2, ts, cs), lambda i: (0, i, 0)), pl.BlockSpec((ts, d), lambda i: (i, 0)),
                   *[pl.BlockSpec((8, d), lambda i: (0, 0))] * nl, *comm.specs()),
        out_shape=(jax.ShapeDtypeStruct((s, d), F32), jax.ShapeDtypeStruct((s, d), BF16),
                   jax.ShapeDtypeStruct((4, s, cs), BF16), jax.ShapeDtypeStruct((4, s, cs), BF16),
                   jax.ShapeDtypeStruct((2, s, cs), BF16), jax.ShapeDtypeStruct((s, d), F32),
                   *[jax.ShapeDtypeStruct((8, d), F32)] * nl, *comm.outs),
        input_output_aliases=comm.aliases(n_in, n_out),
        scratch_shapes=[pltpu.VMEM((4, FFN_HALO, cs), F32)] + (comm.scratch() if nc else []),
        compiler_params=_cparams(("arbitrary",)),
    )(x, vec, w_up, w_dw, w_down, *([target] * nl), *comm.arrays)
    return res[:n_out], comm.split(res[n_out:])


def _ffn_backward(dout, x, y, a0, cc, vec, w_up, w_dw, w_down):
    s, d = x.shape
    _, _, cs = w_up.shape
    ts = _row_tile(s, 256)
    nb = s // ts
    chunks = _lane_chunks(cs)

    def body(do_ref, x_ref, y_ref, a0_ref, cc_ref, vec_ref, wup_ref, wdw_ref, wdn_ref,
             dx_ref, da0_ref, dy_ref, sum_ref, dwdw_ref, carry, du_s, dh_s):
        step = pl.program_id(0)

        @pl.when(step == 0)
        def _():
            carry[...] = jnp.zeros_like(carry)
            sum_ref[...] = jnp.zeros_like(sum_ref)
            dwdw_ref[...] = jnp.zeros_like(dwdw_ref)

        vec = vec_ref[...]
        a, sh, gt, gpost = _vec_rows(vec)
        do = do_ref[...]
        yn, ry = _rms(y_ref[...])
        dy = _rms_bwd(do * (gt * gpost), yn, ry)
        dyb = dy.astype(BF16)
        dy_ref[...] = dyb
        order = [(q, c0, cw) for q in range(2) for c0, cw in chunks]

        def du_pieces(idx):
            q, c0, cw = order[idx]
            return [functools.partial(_store_dot_nt, du_s.at[idx % 2, :, n0:min(n0 + MXU_LANES, cw)], dy_ref,
                                      wdn_ref.at[q, c0 + n0:c0 + min(n0 + MXU_LANES, cw), :])
                    for n0 in range(0, cw, MXU_LANES)]

        def dh_pieces():
            return [functools.partial(_store_dot_nt2, dh_s.at[:, n0:n0 + MXU_LANES], da0_ref.at[0], da0_ref.at[2],
                                      wup_ref.at[0, n0:n0 + MXU_LANES, :], wup_ref.at[2, n0:n0 + MXU_LANES, :])
                    for n0 in range(0, d, MXU_LANES)]

        for piece in du_pieces(0):
            piece()
        later = dh_pieces()
        for idx, (q, c0, cw) in enumerate(order):
            work = du_pieces(idx + 1) if idx + 1 < len(order) else []
            if q == 1:
                share = -(-len(later) // (len(order) - idx))
                work, later = work + later[:share], later[share:]

            def pump(part, of=3):
                for piece in work[part::of]:
                    piece()

            cg = cc_ref[q, :, c0:c0 + cw].astype(F32)
            cv = cc_ref[q + 2, :, c0:c0 + cw].astype(F32)
            sg = _sigmoid(cg)
            sl = cg * sg
            du = du_s[idx % 2, :, :cw]
            dconv = {q: du * cv * (sg * (1.0 + cg * (1.0 - sg))), q + 2: du * sl}
            pump(0)
            for part, j in enumerate((q, q + 2)):
                dae = jnp.concatenate([dconv[j], carry[j, :, c0:c0 + cw]], axis=0)
                carry[j, :, c0:c0 + cw] = dconv[j][:FFN_HALO]
                up1 = _shift_down(dae, FFN_HALO - 1)[FFN_HALO:]
                up2 = _shift_down(dae, FFN_HALO - 2)[FFN_HALO:]
                lanes = slice(j * cs + c0, j * cs + c0 + cw)
                w = wdw_ref[:, lanes]
                da0_ref[j, :, c0:c0 + cw] = (w[2:3] * dconv[j] + w[1:2] * up1 + w[0:1] * up2).astype(BF16)
                a0 = a0_ref[j, :, c0:c0 + cw].astype(F32)
                dwdw_ref[0:1, lanes] += _colsum(up2 * a0)
                dwdw_ref[1:2, lanes] += _colsum(up1 * a0)
                dwdw_ref[2:3, lanes] += _colsum(dconv[j] * a0)
                pump(part + 1)
        dh = dh_s[...] + _dot_nt(da0_ref[1], wup_ref[1]) + _dot_nt(da0_ref[3], wup_ref[3])
        xn, r = _rms(x_ref[...])
        dx_ref[...] = do + _rms_bwd(dh * a, xn, r)
        _add_rows(sum_ref, _norm_sums(do, yn, dh, xn, vec) + [_colsum(dh)])

    blk = lambda st: (nb - 1 - st, 0)
    blk3 = lambda st: (0, nb - 1 - st, 0)
    const3 = lambda st: (0, 0, 0)
    return pl.pallas_call(
        body, name="ffn_backward", grid=(nb,),
        in_specs=[pl.BlockSpec((ts, d), blk), pl.BlockSpec((ts, d), blk), pl.BlockSpec((ts, d), blk),
                  pl.BlockSpec((4, ts, cs), blk3), pl.BlockSpec((4, ts, cs), blk3),
                  pl.BlockSpec((8, d), lambda st: (0, 0)),
                  pl.BlockSpec(w_up.shape, const3, pipeline_mode=pl.Buffered(1)),
                  pl.BlockSpec(w_dw.shape, lambda st: (0, 0)),
                  pl.BlockSpec(w_down.shape, const3, pipeline_mode=pl.Buffered(1))],
        out_specs=(pl.BlockSpec((ts, d), blk), pl.BlockSpec((4, ts, cs), blk3),
                   pl.BlockSpec((ts, d), blk), pl.BlockSpec((8, d), lambda st: (0, 0)),
                   pl.BlockSpec((8, 4 * cs), lambda st: (0, 0))),
        out_shape=(jax.ShapeDtypeStruct((s, d), F32), jax.ShapeDtypeStruct((4, s, cs), BF16),
                   jax.ShapeDtypeStruct((s, d), BF16),
                   jax.ShapeDtypeStruct((8, d), F32), jax.ShapeDtypeStruct((8, 4 * cs), F32)),
        scratch_shapes=[pltpu.VMEM((4, FFN_HALO, cs), F32), pltpu.VMEM((2, ts, max(cw for _, cw in chunks)), F32),
                        pltpu.VMEM((ts, d), F32)],
        compiler_params=_cparams(("arbitrary",)),
    )(dout, x, y, a0, cc, vec, w_up, w_dw, w_down)


def _conv_halo(width):
    return -(-(width - 1) // 8) * 8


def _conv_forward(x, vec, cvec, w_pw1, b_pw1, w_dw, w_pw2):
    s, d = x.shape
    kw = w_dw.shape[0]
    halo = _conv_halo(kw)
    ts = _row_tile(s, 512)
    hd = d // 2

    def body(x_ref, vec_ref, cvec_ref, w1_ref, b1_ref, wdw_ref, w2_ref,
             o_ref, h_ref, a_ref, uc_ref, z_ref, y_ref, carry):
        i = pl.program_id(0)

        @pl.when(i == 0)
        def _():
            carry[...] = jnp.zeros_like(carry)

        vec, cvec = vec_ref[...], cvec_ref[...]
        a, sh, gt, gpost = _vec_rows(vec)
        xb = x_ref[...]
        xn, _ = _rms(xb)
        hb = (xn * a + sh).astype(BF16)
        h_ref[...] = hb
        for j in range(4):
            a_ref[:, j * hd:(j + 1) * hd] = _dot(hb, w1_ref[j]) + b1_ref[:, j * hd:(j + 1) * hd]
        u = a_ref[:, :d] * _sigmoid(a_ref[:, d:])
        carry[halo:, :] = u
        for r0 in range(0, ts, CONV_ROWS):
            for l0 in range(0, d, CONV_LANES):
                lanes = slice(l0, l0 + CONV_LANES)
                src = carry[r0:r0 + CONV_ROWS + halo, lanes]
                acc = jnp.zeros((CONV_ROWS, CONV_LANES), F32) + cvec[0:1, lanes]
                for k in range(kw):
                    acc = acc + wdw_ref[k:k + 1, lanes] * _shift_down(src, kw - 1 - k)[halo:]
                uc_ref[r0:r0 + CONV_ROWS, lanes] = acc
        carry[:halo, :] = u[ts - halo:]
        uc = uc_ref[...]
        mu = jnp.mean(uc, axis=-1, keepdims=True)
        cen = uc - mu
        rstd = lax.rsqrt(jnp.mean(cen * cen, axis=-1, keepdims=True) + EPS)
        l = cen * rstd * cvec[1:2] + cvec[2:3]
        zb = (l * _sigmoid(l)).astype(BF16)
        z_ref[...] = zb
        y = _dot(zb, w2_ref[...]) + cvec[3:4]
        y_ref[...] = y
        yn, _ = _rms(y)
        o_ref[...] = xb + gt * (yn * gpost)

    row = lambda i: (i, 0)
    const2 = lambda i: (0, 0)
    return pl.pallas_call(
        body, name="conv_forward", grid=(s // ts,),
        in_specs=[pl.BlockSpec((ts, d), row), pl.BlockSpec((8, d), const2), pl.BlockSpec((8, d), const2),
                  pl.BlockSpec(w_pw1.shape, lambda i: (0, 0, 0)), pl.BlockSpec(b_pw1.shape, const2),
                  pl.BlockSpec(w_dw.shape, const2), pl.BlockSpec(w_pw2.shape, const2)],
        out_specs=(pl.BlockSpec((ts, d), row), pl.BlockSpec((ts, d), row), pl.BlockSpec((ts, 2 * d), row),
                   pl.BlockSpec((ts, d), row), pl.BlockSpec((ts, d), row), pl.BlockSpec((ts, d), row)),
        out_shape=(jax.ShapeDtypeStruct((s, d), F32), jax.ShapeDtypeStruct((s, d), BF16),
                   jax.ShapeDtypeStruct((s, 2 * d), F32), jax.ShapeDtypeStruct((s, d), F32),
                   jax.ShapeDtypeStruct((s, d), BF16), jax.ShapeDtypeStruct((s, d), F32)),
        scratch_shapes=[pltpu.VMEM((halo + ts, d), F32)],
        compiler_params=_cparams(("arbitrary",)),
    )(x, vec, cvec, w_pw1, b_pw1, w_dw, w_pw2)


def _conv_backward(dout, x, y, a_pre, uc, vec, cvec, w_pw1, w_dw, w_pw2):
    s, d = x.shape
    kw = w_dw.shape[0]
    kpad = -(-kw // 8) * 8
    halo = _conv_halo(kw)
    ts = _row_tile(s, 512)
    nb = s // ts
    hb = ts // halo
    hd = d // 2

    def body(do_ref, x_ref, y_ref, a_ref, ah_ref, uc_ref, vec_ref, cvec_ref, w1_ref, wdw_ref, w2_ref,
             dx_ref, da_ref, dy_ref, sum_ref, dwdw_ref, carry):
        step = pl.program_id(0)
        i = nb - 1 - step

        @pl.when(step == 0)
        def _():
            carry[...] = jnp.zeros_like(carry)
            sum_ref[...] = jnp.zeros_like(sum_ref)
            dwdw_ref[...] = jnp.zeros_like(dwdw_ref)

        vec, cvec = vec_ref[...], cvec_ref[...]
        a, sh, gt, gpost = _vec_rows(vec)
        do = do_ref[...]
        yn, ry = _rms(y_ref[...])
        dy = _rms_bwd(do * (gt * gpost), yn, ry)
        dyb = dy.astype(BF16)
        dy_ref[...] = dyb
        dz = _dot_nt(dyb, w2_ref[...])
        uc = uc_ref[...]
        mu = jnp.mean(uc, axis=-1, keepdims=True)
        cen = uc - mu
        rstd = lax.rsqrt(jnp.mean(cen * cen, axis=-1, keepdims=True) + EPS)
        lhat = cen * rstd
        l = lhat * cvec[1:2] + cvec[2:3]
        sgl = _sigmoid(l)
        dl = dz * (sgl * (1.0 + l * (1.0 - sgl)))
        dlhat = dl * cvec[1:2]
        duc = rstd * (dlhat - jnp.mean(dlhat, axis=-1, keepdims=True)
                      - lhat * jnp.mean(dlhat * lhat, axis=-1, keepdims=True))
        ae = jnp.concatenate([ah_ref[...] * (i > 0).astype(F32), a_ref[...]], axis=0)
        sgate = _sigmoid(ae[:, d:])
        val = ae[:, :d]
        ue = val * sgate
        rowid = lax.broadcasted_iota(jnp.int32, (halo + ts, 1), 0)
        ue = jnp.where((rowid >= halo) | (i > 0), ue, 0.0)
        duce = jnp.concatenate([duc, carry[...]], axis=0)
        carry[...] = duc[:halo]
        du = jnp.zeros((ts, d), F32)
        for k in range(kw):
            du = du + wdw_ref[k:k + 1, :] * _shift_down(duce, halo - (kw - 1 - k))[halo:]
            dwdw_ref[k:k + 1, :] += _colsum(duc * _shift_down(ue, kw - 1 - k)[halo:])
        sg, vl = sgate[halo:], val[halo:]
        dval = du * sg
        dgate = du * vl * (sg * (1.0 - sg))
        dvb, dgb = dval.astype(BF16), dgate.astype(BF16)
        dh = jnp.zeros((ts, d), F32)
        for j in range(2):
            da_ref[j] = dvb[:, j * hd:(j + 1) * hd]
            da_ref[j + 2] = dgb[:, j * hd:(j + 1) * hd]
            dh = dh + _dot_nt(dvb[:, j * hd:(j + 1) * hd], w1_ref[j]) + _dot_nt(dgb[:, j * hd:(j + 1) * hd], w1_ref[j + 2])
        xn, r = _rms(x_ref[...])
        dx_ref[...] = do + _rms_bwd(dh * a, xn, r)
        _add_rows(sum_ref, _norm_sums(do, yn, dh, xn, vec) + [_colsum(dh), _colsum(dy), _colsum(dl * lhat), _colsum(dl),
                            _colsum(duc), _colsum(dval), _colsum(dgate)])

    blk = lambda st: (nb - 1 - st, 0)
    const2 = lambda st: (0, 0)
    return pl.pallas_call(
        body, name="conv_backward", grid=(nb,),
        in_specs=[pl.BlockSpec((ts, d), blk), pl.BlockSpec((ts, d), blk), pl.BlockSpec((ts, d), blk),
                  pl.BlockSpec((ts, 2 * d), blk),
                  pl.BlockSpec((halo, 2 * d), lambda st: (jnp.maximum((nb - 1 - st) * hb - 1, 0), 0)),
                  pl.BlockSpec((ts, d), blk), pl.BlockSpec((8, d), const2), pl.BlockSpec((8, d), const2),
                  pl.BlockSpec(w_pw1.shape, lambda st: (0, 0, 0)), pl.BlockSpec(w_dw.shape, const2),
                  pl.BlockSpec(w_pw2.shape, const2)],
        out_specs=(pl.BlockSpec((ts, d), blk), pl.BlockSpec((4, ts, hd), lambda st: (0, nb - 1 - st, 0)),
                   pl.BlockSpec((ts, d), blk), pl.BlockSpec((16, d), const2), pl.BlockSpec((kpad, d), const2)),
        out_shape=(jax.ShapeDtypeStruct((s, d), F32), jax.ShapeDtypeStruct((4, s, hd), BF16),
                   jax.ShapeDtypeStruct((s, d), BF16), jax.ShapeDtypeStruct((16, d), F32),
                   jax.ShapeDtypeStruct((kpad, d), F32)),
        scratch_shapes=[pltpu.VMEM((halo, d), F32)],
        compiler_params=_cparams(("arbitrary",)),
    )(dout, x, y, a_pre, a_pre, uc, vec, cvec, w_pw1, w_dw, w_pw2)


def _weight_grad(a, b, comm=None):
    na, s, k = a.shape
    nb_, _, n = b.shape
    nj = max(na, nb_)
    ts = _row_tile(s, 2048)
    nt = s // ts
    comm = comm or _Comm([])
    nc = len(comm.arrays)

    def body(*refs):
        a_ref, b_ref = refs[:2]
        cin = refs[2:2 + nc]
        o_ref = refs[2 + nc]
        cout = refs[3 + nc:3 + 2 * nc]
        sems = refs[3 + 2 * nc:]
        j, t = pl.program_id(0), pl.program_id(1)

        if nc:
            @pl.when((j == 0) & (t == 0))
            def _():
                comm.run(0, cin, cout, *sems)

            @pl.when((j == nj // 2) & (t == nt // 2))
            def _():
                comm.run(1, cin, cout, *sems)

        @pl.when(t == 0)
        def _():
            o_ref[...] = jnp.zeros_like(o_ref)

        o_ref[0] += _dot_tn(a_ref[0], b_ref[0])

        if nc:
            @pl.when((j == nj - 1) & (t == nt - 1))
            def _():
                comm.run(2, cin, cout, *sems)

    res = pl.pallas_call(
        body, name="weight_grad", grid=(nj, nt),
        in_specs=[pl.BlockSpec((1, ts, k), (lambda j, t: (j, t, 0)) if na > 1 else (lambda j, t: (0, t, 0))),
                  pl.BlockSpec((1, ts, n), (lambda j, t: (j, t, 0)) if nb_ > 1 else (lambda j, t: (0, t, 0)))]
        + comm.specs(),
        out_specs=(pl.BlockSpec((1, k, n), lambda j, t: (j, 0, 0)), *comm.specs()),
        out_shape=(jax.ShapeDtypeStruct((nj, k, n), F32), *comm.outs),
        input_output_aliases=comm.aliases(2, 1),
        scratch_shapes=comm.scratch() if nc else [],
        compiler_params=_cparams(("arbitrary", "arbitrary") if nc else ("parallel", "arbitrary")),
    )(a, b, *comm.arrays)
    return res[0], comm.split(res[1:])


def _adamw_math(w, g, m, v):
    nm = ADAM_B1 * m + (1.0 - ADAM_B1) * g
    nv = ADAM_B2 * v + (1.0 - ADAM_B2) * (g * g)
    m_hat = nm * (1.0 / (1.0 - ADAM_B1 ** ADAM_STEP))
    v_hat = nv * (1.0 / (1.0 - ADAM_B2 ** ADAM_STEP))
    return -ADAM_LR * (m_hat / (jnp.sqrt(v_hat) + ADAM_EPS) + ADAM_WD * w), nm, nv


def _adamw_many(params):
    n = len(params)

    def body(*refs):
        for k in range(n):
            w_ref, g_ref, m_ref, v_ref = refs[4 * k:4 * k + 4]
            outs = refs[4 * n + 3 * k:4 * n + 3 * k + 3]
            for o_ref, val in zip(outs, _adamw_math(w_ref[...], g_ref[...], m_ref[...], v_ref[...])):
                o_ref[...] = val

    vm = pl.BlockSpec(memory_space=pltpu.VMEM)
    res = pl.pallas_call(
        body, name="adamw_many", in_specs=[vm] * (4 * n), out_specs=tuple([vm] * (3 * n)),
        out_shape=tuple(jax.ShapeDtypeStruct(p[0].shape, F32) for p in params for _ in range(3)),
        compiler_params=_cparams(),
    )(*[a for p in params for a in p])
    return [res[3 * k:3 * k + 3] for k in range(n)]


def _adamw(w, g, m, v, comm=None):
    nl, r, c = w.shape
    tr = _row_tile(r, 256)
    nr = r // tr
    comm = comm or _Comm([])
    nc = len(comm.arrays)

    def body(*refs):
        w_ref, g_ref, m_ref, v_ref = refs[:4]
        cin = refs[4:4 + nc]
        d_ref, nm_ref, nv_ref = refs[4 + nc:7 + nc]
        cout = refs[7 + nc:7 + 2 * nc]
        sems = refs[7 + 2 * nc:]
        l, i = pl.program_id(0), pl.program_id(1)
        if nc:
            @pl.when((l == 0) & (i == 0))
            def _():
                comm.run(0, cin, cout, *sems)

            @pl.when((l == nl // 2) & (i == nr // 2))
            def _():
                comm.run(1, cin, cout, *sems)

        d_ref[...], nm_ref[...], nv_ref[...] = _adamw_math(w_ref[...], g_ref[...], m_ref[...], v_ref[...])
        if nc:
            @pl.when((l == nl - 1) & (i == nr - 1))
            def _():
                comm.run(2, cin, cout, *sems)

    spec = pl.BlockSpec((1, tr, c), lambda l, i: (l, i, 0))
    shp = jax.ShapeDtypeStruct((nl, r, c), F32)
    res = pl.pallas_call(
        body, name="adamw", grid=(nl, nr), in_specs=[spec] * 4 + comm.specs(),
        out_specs=(spec,) * 3 + tuple(comm.specs()), out_shape=(shp,) * 3 + tuple(comm.outs),
        input_output_aliases=comm.aliases(4, 3), scratch_shapes=comm.scratch() if nc else [],
        compiler_params=_cparams(("arbitrary", "arbitrary") if nc else ("parallel", "parallel")),
    )(w, g, m, v, *comm.arrays)
    return res[:3], comm.split(res[3:])


def _add_my_half(g, other, idx):
    _, _, h, c = g.shape
    th = _row_tile(h, 256)

    def body(idx_ref, g_ref, o_ref, out_ref):
        out_ref[...] = (g_ref[:, 0] + o_ref[...]).astype(BF16)

    return pl.pallas_call(
        body, name="add_my_half",
        grid_spec=pltpu.PrefetchScalarGridSpec(
            num_scalar_prefetch=1, grid=(3, h // th),
            in_specs=[pl.BlockSpec((1, 1, th, c), lambda j, i, idx_ref: (idx_ref[2 + j], idx_ref[1], i, 0)),
                      pl.BlockSpec((1, th, c), lambda j, i, idx_ref: (idx_ref[2 + j], i, 0))],
            out_specs=pl.BlockSpec((1, th, c), lambda j, i, idx_ref: (idx_ref[2 + j], i, 0))),
        out_shape=jax.ShapeDtypeStruct(other.shape, BF16),
        compiler_params=_cparams(("parallel", "parallel")),
    )(idx, g, other)


def _sum_for_my_chip(g, other, got, idx):
    _, _, h, c = g.shape
    th = _row_tile(h, 256)

    def body(idx_ref, g_ref, o_ref, q_ref, out_ref):
        out_ref[0] = (((g_ref[0, 0] + o_ref[0]) + q_ref[0].astype(F32)) + q_ref[1].astype(F32)) + q_ref[2].astype(F32)

    return pl.pallas_call(
        body, name="sum_for_my_chip",
        grid_spec=pltpu.PrefetchScalarGridSpec(
            num_scalar_prefetch=1, grid=(h // th,),
            in_specs=[pl.BlockSpec((1, 1, th, c), lambda i, idx_ref: (idx_ref[0], idx_ref[1], i, 0)),
                      pl.BlockSpec((1, th, c), lambda i, idx_ref: (idx_ref[0], i, 0)),
                      pl.BlockSpec((3, th, c), lambda i, idx_ref: (0, i, 0))],
            out_specs=pl.BlockSpec((1, th, c), lambda i, idx_ref: (idx_ref[1], i, 0))),
        out_shape=jax.ShapeDtypeStruct((2, h, c), F32),
        compiler_params=_cparams(("parallel",)),
    )(idx, g, other, got)


class _Reducer:
    def __init__(self, idx):
        self.idx = idx
        self.groups = []

    def add(self, grads):
        group = {"state": 0, "g": [g.reshape(4, 2, g.shape[1] // 2, g.shape[2]) for g in grads]}
        self.groups.append(group)
        return group

    def steps(self):
        ops, owners = [], []
        for gr in self.groups:
            if gr["state"] == 0:
                ops.append(_Swap(gr["g"]))
            elif gr["state"] == 1:
                ops.append(_Exchange(gr["parts"]))
            elif gr["state"] == 2:
                ops.append(_Join(gr["bufs"]))
            else:
                continue
            owners.append(gr)
        return ops, owners

    def absorb(self, owners, results):
        for gr, res in zip(owners, results):
            if gr["state"] == 0:
                gr["other"] = res
                gr["parts"] = [_add_my_half(g, o, self.idx) for g, o in zip(gr["g"], res)]
            elif gr["state"] == 1:
                gr["bufs"] = [_sum_for_my_chip(g, o, q, self.idx) for g, o, q in zip(gr["g"], gr["other"], res)]
            else:
                gr["full"] = [b.reshape(2 * b.shape[1], b.shape[2]) for b in res]
            gr["state"] += 1

    def drain(self):
        while any(gr["state"] < 3 for gr in self.groups):
            ops, owners = self.steps()
            self.absorb(owners, _communicate(ops))


class _GatherRows:
    def __init__(self, bufs):
        self.arrays = list(bufs)
        self.outs = [jax.ShapeDtypeStruct(b.shape, b.dtype) for b in bufs]
        self.aliased = True
        self.n_sems = 7 * len(bufs)

    def run(self, phase, ins, outs, send_sems, recv_sems, base):
        x, y, c, chips = _place()
        me, sibling = (x, y, c), (x, y, 1 - c)
        for k, buf in enumerate(outs):
            def copy(i, block_of, to):
                blk = buf.at[4 * block_of[0] + 2 * block_of[1] + block_of[2]]
                return _remote(blk, blk, send_sems.at[base + 7 * k + i], recv_sems.at[base + 7 * k + i], to)

            if phase == 0:
                copy(0, me, sibling).start()
            for r, (px, py) in enumerate(chips):
                if phase == 0:
                    copy(1 + r, me, (px, py, c)).start()
                elif phase == 1:
                    copy(1 + r, (px, py, c), me).wait_recv()
                    copy(4 + r, (px, py, c), sibling).start()
                else:
                    copy(4 + r, (px, py, 1 - c), me).wait_recv()
                    copy(1 + r, me, (px, py, c)).wait_send()
                    copy(4 + r, (px, py, c), sibling).wait_send()
            if phase == 2:
                copy(0, sibling, me).wait_recv()
                copy(0, me, sibling).wait_send()


def _sum_devices(gathered):
    nd, m, n = gathered.shape

    def body(g_ref, o_ref):
        acc = g_ref[0]
        for b in range(1, nd):
            acc = acc + g_ref[b]
        o_ref[...] = acc

    return pl.pallas_call(
        body, name="sum_devices", out_shape=jax.ShapeDtypeStruct((m, n), F32),
        in_specs=[pl.BlockSpec(memory_space=pltpu.VMEM)], out_specs=pl.BlockSpec(memory_space=pltpu.VMEM),
        compiler_params=_cparams(),
    )(gathered)


def _ada_update(c_all, dmod_cols, w, m, v):
    nl, nd, ncol = dmod_cols.shape
    d = c_all.shape[1]
    tr = _row_tile(d, 256)

    def body(c_ref, dm_ref, w_ref, m_ref, v_ref, g_ref, d_ref, nm_ref, nv_ref):
        g = lax.dot_general(c_ref[...], dm_ref[0], (((0,), (0,)), ((), ())),
                            preferred_element_type=F32, precision=lax.Precision.HIGHEST)
        g_ref[0] = g
        d_ref[0], nm_ref[0], nv_ref[0] = _adamw_math(w_ref[0], g, m_ref[0], v_ref[0])

    spec = pl.BlockSpec((1, tr, ncol), lambda l, i: (l, i, 0))
    shp = jax.ShapeDtypeStruct((nl, d, ncol), F32)
    return pl.pallas_call(
        body, name="ada_update", grid=(nl, d // tr),
        in_specs=[pl.BlockSpec((nd, tr), lambda l, i: (0, i)), pl.BlockSpec((1, nd, ncol), lambda l, i: (l, 0, 0)),
                  spec, spec, spec],
        out_specs=(spec,) * 4, out_shape=(shp,) * 4, compiler_params=_cparams(("parallel", "parallel")),
    )(c_all, dmod_cols, w, m, v)


def _pad_rows(a, rows):
    return jnp.pad(a, ((0, rows - a.shape[0]), (0, 0)))


def _shard_cols(full, chip, width):
    return lax.dynamic_slice_in_dim(full, chip * width, width, axis=full.ndim - 1)


def kernel(x, c, ada_w, ada_b, pre_g, post_g, pool_w, pool_scale, cv_w_pw1, cv_b_pw1, cv_w_dw, cv_b_dw, cv_ln_g, cv_ln_b, cv_w_pw2, cv_b_pw2, ffn_w_up, ffn_w_dw, ffn_w_down, loss_target, m_ada_w, m_ada_b, m_pre_g, m_post_g, m_pool_w, m_pool_scale, m_cv_w_pw1, m_cv_b_pw1, m_cv_w_dw, m_cv_b_dw, m_cv_ln_g, m_cv_ln_b, m_cv_w_pw2, m_cv_b_pw2, m_ffn_w_up, m_ffn_w_dw, m_ffn_w_down, v_ada_w, v_ada_b, v_pre_g, v_post_g, v_pool_w, v_pool_scale, v_cv_w_pw1, v_cv_b_pw1, v_cv_w_dw, v_cv_b_dw, v_cv_ln_g, v_cv_ln_b, v_cv_w_pw2, v_cv_b_pw2, v_ffn_w_up, v_ffn_w_dw, v_ffn_w_down):
    s, d = x.shape[1], x.shape[2]
    dq = d // N_CHIPS
    n_g = pool_w.shape[1]
    gq = pool_w.shape[2]
    gd = pool_w.shape[3]
    kw = cv_w_dw.shape[1]
    cs = ffn_w_up.shape[2]
    fq = ffn_w_down.shape[1]
    chip = 2 * lax.axis_index("x") + lax.axis_index("y")
    core = lax.axis_index("c")
    chip1 = jnp.reshape(chip, (1,)).astype(jnp.int32)
    core1 = jnp.reshape(core, (1,)).astype(jnp.int32)
    xs, tgt = x[0], loss_target[0]

    small_rows = [pre_g.reshape(4, dq), post_g.reshape(4, dq), cv_w_dw[0], cv_b_dw, cv_ln_g, cv_ln_b, cv_b_pw2,
                  cv_b_pw1.reshape(2, dq)]
    small = jnp.concatenate(small_rows, axis=0)
    n_small = small.shape[0]
    small = _pad_rows(small, -(-n_small // 16) * 16)
    dwf = _pad_rows(ffn_w_dw.reshape(6, cs), 16)
    first = _AllGather([_cast_into_slot(pool_w.reshape(n_g * gq, gd), chip1), _cast_into_slot(small, chip1, dtype=F32),
                        _cast_into_slot(dwf, chip1, dtype=F32), _cast_into_slot(ffn_w_up[0], chip1)])
    (c_rep, mod_rep), ((g_pool, g_small, g_dwf, g_up0),) = _ada_forward(c, ada_w, _Comm([first]))
    c_all = c_rep[:, 0, :]
    mod = mod_rep[:, :, 0, :].transpose(1, 0, 2).reshape(ada_b.shape) + ada_b
    second = _AllGather([_cast_into_slot(ffn_w_down[0], chip1)])
    later = _AllGather([_cast_into_slot(cv_w_pw1[0], chip1), _cast_into_slot(cv_w_pw2[0], chip1),
                        _cast_into_slot(ffn_w_up[1], chip1), _cast_into_slot(ffn_w_down[1], chip1)])
    poolw_full = g_pool.reshape(N_CHIPS, n_g, gq, gd).transpose(1, 0, 2, 3).reshape(n_g, gd, gd)
    smallf = g_small.transpose(1, 0, 2).reshape(g_small.shape[1], d)
    pre_full, post_full = smallf[0:4].reshape(2, 2, d), smallf[4:8].reshape(2, 2, d)
    wdw31 = smallf[8:8 + kw]
    o = 8 + kw
    b_dw, ln_g, ln_b, b_pw2 = smallf[o:o + 1], smallf[o + 1:o + 2], smallf[o + 2:o + 3], smallf[o + 3:o + 4]
    b_pw1 = g_small[:, o + 4:o + 6, :].reshape(1, 2 * d)
    ffn_dw = g_dwf[:, :6, :].transpose(1, 0, 2).reshape(2, 3, N_CHIPS * cs)

    def sub_vec(layer, sub, extra=None):
        m6 = mod[layer].reshape(6, d)
        rows = [pre_full[layer, sub][None], 1.0 + m6[3 * sub + 1][None], m6[3 * sub][None], m6[3 * sub + 2][None],
                post_full[layer, sub][None]]
        if extra is not None:
            rows.append(extra)
        return _pad_rows(jnp.concatenate(rows, axis=0), 8)

    vec_pool = sub_vec(0, 0, pool_scale)
    vec_f0, vec_conv, vec_f1 = sub_vec(0, 1), sub_vec(1, 0), sub_vec(1, 1)
    cvec = _pad_rows(jnp.concatenate([b_dw, ln_g, ln_b, b_pw2], axis=0), 8)

    x1, ((g_dn0,),) = _pool_forward(xs, vec_pool, poolw_full, _Comm([second]))
    w_up0, w_dn0 = g_up0, g_dn0.reshape(2, 2 * fq, d)
    (x2, h_f0, a0_f0, cc_f0, u_f0, y_f0), ((g_pw1, g_pw2, g_up1, g_dn1),) = _ffn_forward(
        x1, vec_f0, w_up0, ffn_dw[0], w_dn0, _Comm([later]))
    pw2_full = g_pw2.reshape(d, d)
    w_up1, w_dn1 = g_up1, g_dn1.reshape(2, 2 * fq, d)
    x3, h_cv, a_cv, uc_cv, z_cv, y_cv = _conv_forward(x2, vec_conv, cvec, g_pw1, b_pw1, wdw31, pw2_full)
    (dx4, h_f1, a0_f1, cc_f1, u_f1, y_f1, loss_rows), _ = _ffn_forward(x3, vec_f1, w_up1, ffn_dw[1], w_dn1, target=tgt)

    dx3, da0_f1, dy_f1, sum_f1, dwdw_f1 = _ffn_backward(dx4, x3, y_f1, a0_f1, cc_f1, vec_f1, w_up1, ffn_dw[1], w_dn1)
    dx2, da_cv, dy_cv, sum_cv, dwdw_cv = _conv_backward(dx3, x2, y_cv, a_cv, uc_cv, vec_conv, cvec, g_pw1, wdw31, pw2_full)
    dx1, da0_f0, dy_f0, sum_f0, dwdw_f0 = _ffn_backward(dx2, x1, y_f0, a0_f0, cc_f0, vec_f0, w_up0, ffn_dw[0], w_dn0)
    dx0, sum_pool, gw_pool = _pool_backward(dx1, xs, vec_pool, poolw_full)
    gw_pool4 = gw_pool.reshape(n_g, N_CHIPS, gq, gd).transpose(1, 0, 2, 3).reshape(N_CHIPS, n_g * gq, gd)

    slab = jnp.concatenate([sum_f1, sum_cv, dwdw_cv, sum_f0, sum_pool, loss_rows], axis=0)
    wide = jnp.concatenate([dwdw_f1, dwdw_f0], axis=0)
    n_slab = slab.shape[0]
    mine = jnp.concatenate([slab, wide.reshape(-1, d)], axis=0)
    rows_of_all = _cast_into_slot(mine, 2 * chip1 + core1, N_DEV, F32)
    red = _Reducer(jnp.concatenate([chip1, core1, chip1 ^ 1, chip1 ^ 2, chip1 ^ 3]))

    def carried(call, *args, extra=()):
        ops, owners = red.steps()
        out, results = call(*args, _Comm(ops + list(extra)))
        red.absorb(owners, results[:len(ops)])
        return out, results[len(ops):]

    gw_up1, ((both_all,),) = carried(_weight_grad, h_f1[None], da0_f1, extra=[_GatherRows([rows_of_all])])
    r_up1 = red.add([gw_up1])
    r_up0 = red.add([carried(_weight_grad, h_f0[None], da0_f0)[0]])
    r_dn1 = red.add([carried(_weight_grad, u_f1, dy_f1[None])[0].reshape(N_CHIPS, fq, d)])
    r_dn0 = red.add([carried(_weight_grad, u_f0, dy_f0[None])[0].reshape(N_CHIPS, fq, d)])
    r_pw1 = red.add([carried(_weight_grad, h_cv[None], da_cv)[0]])
    r_last = red.add([carried(_weight_grad, z_cv[None], dy_cv[None])[0].reshape(N_CHIPS, dq, d), gw_pool4])

    tot_both = _sum_devices(both_all)
    slab_all, tot = both_all[:, :n_slab], tot_both[:n_slab]
    tot_wide = tot_both[n_slab:].reshape(wide.shape)
    kpad = dwdw_cv.shape[0]
    o_cv, o_dw, o_f0 = 8, 24, 24 + kpad
    o_pool, o_loss = o_f0 + 8, o_f0 + 16
    loss = jnp.sum(tot[o_loss])
    dmod_l0 = jnp.concatenate([slab_all[:, o_pool + 4], slab_all[:, o_pool + 3], slab_all[:, o_pool + 1],
                               slab_all[:, o_f0 + 4], slab_all[:, o_f0 + 3], slab_all[:, o_f0 + 1]], axis=-1)
    dmod_l1 = jnp.concatenate([slab_all[:, o_cv + 4], slab_all[:, o_cv + 3], slab_all[:, o_cv + 1],
                               slab_all[:, 4], slab_all[:, 3], slab_all[:, 1]], axis=-1)
    dmod = jnp.stack([dmod_l0, dmod_l1], axis=0)
    g_ada_b = _sum_devices(dmod.transpose(1, 0, 2))
    ada_w_step = _ada_update(c_all, _shard_cols(dmod, chip, ada_w.shape[2]), ada_w, m_ada_w, v_ada_w)

    g_pre = jnp.stack([jnp.stack([tot[o_pool + 2], tot[o_f0 + 2]]), jnp.stack([tot[o_cv + 2], tot[2]])])
    g_post = jnp.stack([jnp.stack([tot[o_pool + 0], tot[o_f0 + 0]]), jnp.stack([tot[o_cv + 0], tot[0]])])
    g_pool_scale = tot[o_pool + 5][None]
    g_b_pw2, g_ln_g, g_ln_b, g_b_dw = tot[o_cv + 5], tot[o_cv + 6], tot[o_cv + 7], tot[o_cv + 8]
    g_b_pw1 = jnp.concatenate([tot[o_cv + 9], tot[o_cv + 10]])
    g_w_dw31 = tot[o_dw:o_dw + kw]
    g_ffn_dw = jnp.stack([tot_wide[8:11], tot_wide[0:3]])

    grads_small = {
        "pre_g": _shard_cols(g_pre, chip, dq), "post_g": _shard_cols(g_post, chip, dq),
        "pool_scale": g_pool_scale, "cv_b_pw1": _shard_cols(g_b_pw1[None], chip, 2 * dq),
        "cv_w_dw": _shard_cols(g_w_dw31[None], chip, dq), "cv_b_dw": _shard_cols(g_b_dw[None], chip, dq),
        "cv_ln_g": _shard_cols(g_ln_g[None], chip, dq), "cv_ln_b": _shard_cols(g_ln_b[None], chip, dq),
        "cv_b_pw2": _shard_cols(g_b_pw2[None], chip, dq), "ffn_w_dw": _shard_cols(g_ffn_dw, chip, cs),
        "ada_b": g_ada_b,
    }
    params_small = {
        "pre_g": (pre_g, m_pre_g, v_pre_g), "post_g": (post_g, m_post_g, v_post_g),
        "pool_scale": (pool_scale, m_pool_scale, v_pool_scale), "cv_b_pw1": (cv_b_pw1, m_cv_b_pw1, v_cv_b_pw1),
        "cv_w_dw": (cv_w_dw, m_cv_w_dw, v_cv_w_dw), "cv_b_dw": (cv_b_dw, m_cv_b_dw, v_cv_b_dw),
        "cv_ln_g": (cv_ln_g, m_cv_ln_g, v_cv_ln_g), "cv_ln_b": (cv_ln_b, m_cv_ln_b, v_cv_ln_b),
        "cv_b_pw2": (cv_b_pw2, m_cv_b_pw2, v_cv_b_pw2), "ffn_w_dw": (ffn_w_dw, m_ffn_w_dw, v_ffn_w_dw),
        "ada_b": (ada_b, m_ada_b, v_ada_b),
    }
    names = list(params_small)
    small_g = {nm: grads_small[nm].reshape(params_small[nm][0].shape) for nm in names}
    updated = _adamw_many([(params_small[nm][0], small_g[nm], params_small[nm][1], params_small[nm][2])
                           for nm in names])
    small_d = {nm: u[0] for nm, u in zip(names, updated)}
    small_m = {nm: u[1] for nm, u in zip(names, updated)}
    small_v = {nm: u[2] for nm, u in zip(names, updated)}

    red.drain()
    big_p = {
        "ada_w": (ada_w, m_ada_w, v_ada_w), "pool_w": (pool_w, m_pool_w, v_pool_w),
        "cv_w_pw1": (cv_w_pw1, m_cv_w_pw1, v_cv_w_pw1), "cv_w_pw2": (cv_w_pw2, m_cv_w_pw2, v_cv_w_pw2),
        "ffn_w_up": (ffn_w_up, m_ffn_w_up, v_ffn_w_up), "ffn_w_down": (ffn_w_down, m_ffn_w_down, v_ffn_w_down),
    }
    big_g, big_d, big_m, big_v = {}, {}, {}, {}

    def update(nm, grad):
        w, m, v = big_p[nm]
        as3 = lambda t: t.reshape((-1,) + w.shape[-2:])
        (dl, nm_, nv_), _ = _adamw(as3(w), as3(grad), as3(m), as3(v))
        big_g[nm] = grad.reshape(w.shape)
        big_d[nm], big_m[nm], big_v[nm] = dl.reshape(w.shape), nm_.reshape(w.shape), nv_.reshape(w.shape)

    full = lambda group, k=0: group["full"][k]
    update("ffn_w_up", jnp.stack([full(r_up0), full(r_up1)]))
    big_g["ada_w"], big_d["ada_w"], big_m["ada_w"], big_v["ada_w"] = ada_w_step
    update("ffn_w_down", jnp.stack([full(r_dn0), full(r_dn1)]))
    update("cv_w_pw1", full(r_pw1))
    update("cv_w_pw2", full(r_last, 0))
    update("pool_w", full(r_last, 1))

    order = ["ada_w", "ada_b", "pre_g", "post_g", "pool_w", "pool_scale", "cv_w_pw1", "cv_b_pw1", "cv_w_dw", "cv_b_dw",
             "cv_ln_g", "cv_ln_b", "cv_w_pw2", "cv_b_pw2", "ffn_w_up", "ffn_w_dw", "ffn_w_down"]
    pick = lambda bigs, smalls: [bigs[nm] if nm in bigs else smalls[nm] for nm in order]
    return (loss, dx0[None], *pick(big_g, small_g), *pick(big_d, small_d), *pick(big_m, small_m),
            *pick(big_v, small_v))
```

```python
import functools

import jax
import jax.numpy as jnp
from jax import lax
from jax.experimental import pallas as pl
from jax.experimental.pallas import tpu as pltpu

F32 = jnp.float32
BF16 = jnp.bfloat16
EPS = 1e-6
N_CHIPS = 4
N_DEV = 8
POOL_WINDOWS = (2, 4, 8, 16)
POOL_HALO = 16
FFN_HALO = 16
MXU_LANES = 256
CONV_ROWS, CONV_LANES = 128, 128
ADAM_LR = 0.001
ADAM_B1 = 0.9
ADAM_B2 = 0.999
ADAM_EPS = 1e-08
ADAM_WD = 0.01
ADAM_STEP = 10
V7X_VMEM_LIMIT = 58 * 1024 * 1024
MESH = pl.DeviceIdType.MESH


def _cparams(sem=None, vmem=V7X_VMEM_LIMIT):
    return pltpu.CompilerParams(dimension_semantics=sem, vmem_limit_bytes=vmem)


def _row_tile(n, want):
    if n <= want:
        return n
    t = want - want % 8
    while n % t:
        t -= 8
    return t


def _lane_chunks(width):
    out, c = [], 0
    while c < width:
        w = min(512, width - c)
        out.append((c, w))
        c += w
    return out


def _dot(a, b):
    return jnp.dot(a, b, preferred_element_type=F32)


def _dot_nt(a, b):
    return lax.dot_general(a, b, (((1,), (1,)), ((), ())), preferred_element_type=F32)


def _store_dot_nt(dst, a_ref, b_ref):
    dst[...] = _dot_nt(a_ref[...], b_ref[...])


def _store_dot_nt2(dst, a1_ref, a2_ref, b1_ref, b2_ref):
    dst[...] = _dot_nt(a1_ref[...], b1_ref[...]) + _dot_nt(a2_ref[...], b2_ref[...])


def _dot_tn(a, b):
    return lax.dot_general(a, b, (((0,), (0,)), ((), ())), preferred_element_type=F32)


def _rms(x):
    r = lax.rsqrt(jnp.mean(x * x, axis=-1, keepdims=True) + EPS)
    return x * r, r


def _rms_bwd(dyn, yn, r):
    return r * (dyn - yn * jnp.mean(dyn * yn, axis=-1, keepdims=True))


def _sigmoid(x):
    return 0.5 * jnp.tanh(0.5 * x) + 0.5


def _colsum(x):
    return jnp.sum(x, axis=0, keepdims=True)


def _shift_down(x, k):
    return x if k == 0 else pltpu.roll(x, k, 0)


def _shift_up(x, k):
    return x if k == 0 else pltpu.roll(x, x.shape[0] - k, 0)


def _vec_rows(vec):
    return vec[0:1] * vec[1:2], vec[2:3], vec[3:4], vec[4:5]


def _norm_sums(do, yn, dh, xn, vec):
    p, q = _colsum(do * yn), _colsum(dh * xn)
    return [p * vec[3:4], p * vec[4:5], q * vec[1:2], q * vec[0:1]]


def _add_rows(sum_ref, rows):
    for k, r in enumerate(rows):
        sum_ref[k:k + 1, :] += r


def _ada_forward(c, ada_w, comm=None):
    n_layers, d, ncol = ada_w.shape
    comm = comm or _Comm([])
    nc = len(comm.arrays)

    def body(*refs):
        c_ref, w_ref = refs[:2]
        cin = refs[2:2 + nc]
        call_ref, mod_ref = refs[2 + nc:4 + nc]
        cout = refs[4 + nc:4 + 2 * nc]
        part_ref, sendbuf, send_sems, recv_sems, send2, recv2 = refs[4 + 2 * nc:10 + 2 * nc]
        carried_sems = refs[10 + 2 * nc:]
        if nc:
            comm.run(0, cin, cout, *carried_sems)
        x, y, cc = lax.axis_index("x"), lax.axis_index("y"), lax.axis_index("c")
        me = 4 * x + 2 * y + cc
        rel = [(x, y, 1 - cc), (1 - x, y, cc), (x, 1 - y, cc), (1 - x, 1 - y, cc),
               (1 - x, y, 1 - cc), (x, 1 - y, 1 - cc), (1 - x, 1 - y, 1 - cc)]
        cv = c_ref[...]
        call_ref[me] = jnp.broadcast_to(cv * _sigmoid(cv), (8, d))

        def gather(k, block, to):
            blk = call_ref.at[block]
            return pltpu.make_async_remote_copy(src_ref=blk, dst_ref=blk, send_sem=send_sems.at[k],
                                                recv_sem=recv_sems.at[k], device_id=to, device_id_type=MESH)

        for k, to in enumerate(rel):
            gather(k, me, to).start()
        for k, (px, py, pc) in enumerate(rel):
            gather(k, 4 * px + 2 * py + pc, rel[k]).wait_recv()
        for k, to in enumerate(rel):
            gather(k, me, to).wait_send()

        ca = call_ref[...].reshape(8 * N_DEV, d)
        for l in range(n_layers):
            part_ref[l] = jnp.dot(ca, w_ref[l], preferred_element_type=F32, precision=lax.Precision.HIGHEST)

        j = 2 * x + y
        chips = [(1 - x, y), (x, 1 - y), (1 - x, 1 - y)]

        def rows_of(b):
            return part_ref[:, pl.ds(pl.multiple_of(8 * b, 8), 8), :]

        def scatter(k, src_j, to):
            return pltpu.make_async_remote_copy(
                src_ref=sendbuf.at[k], dst_ref=mod_ref.at[src_j], send_sem=send2.at[k], recv_sem=recv2.at[k],
                device_id=to, device_id_type=MESH)

        mod_ref[j] = rows_of(me)
        for k, (px, py) in enumerate(chips):
            sendbuf[k] = rows_of(4 * px + 2 * py + cc)
            scatter(k, j, (px, py, cc)).start()
        for k, (px, py) in enumerate(chips):
            scatter(k, 2 * px + py, (px, py, cc)).wait_recv()
        for k, (px, py) in enumerate(chips):
            scatter(k, j, (px, py, cc)).wait_send()
        if nc:
            comm.run(1, cin, cout, *carried_sems)
            comm.run(2, cin, cout, *carried_sems)

    vm = pl.BlockSpec(memory_space=pltpu.VMEM)
    res = pl.pallas_call(
        body, name="ada_forward",
        out_shape=(jax.ShapeDtypeStruct((N_DEV, 8, d), F32), jax.ShapeDtypeStruct((N_CHIPS, n_layers, 8, ncol), F32),
                   *comm.outs),
        in_specs=[vm, vm] + comm.specs(), out_specs=(vm, vm, *comm.specs()),
        input_output_aliases=comm.aliases(2, 2),
        scratch_shapes=[pltpu.VMEM((n_layers, 8 * N_DEV, ncol), F32), pltpu.VMEM((3, n_layers, 8, ncol), F32),
                        pltpu.SemaphoreType.DMA((7,)), pltpu.SemaphoreType.DMA((7,)),
                        pltpu.SemaphoreType.DMA((3,)), pltpu.SemaphoreType.DMA((3,))] + (comm.scratch() if nc else []),
        compiler_params=_cparams(),
    )(c, ada_w, *comm.arrays)
    return res[:2], comm.split(res[2:])


def _cast_into_slot(w2d, slot, n_slots=N_CHIPS, dtype=None):
    r, c = w2d.shape
    tr = _row_tile(r, 256)
    dtype = dtype or BF16

    def body(slot_ref, w_ref, o_ref):
        o_ref[0] = w_ref[...].astype(dtype)

    return pl.pallas_call(
        body, name="cast_into_slot",
        grid_spec=pltpu.PrefetchScalarGridSpec(
            num_scalar_prefetch=1, grid=(r // tr,),
            in_specs=[pl.BlockSpec((tr, c), lambda i, slot_ref: (i, 0))],
            out_specs=pl.BlockSpec((1, tr, c), lambda i, slot_ref: (slot_ref[0], i, 0))),
        out_shape=jax.ShapeDtypeStruct((n_slots, r, c), dtype), compiler_params=_cparams(("parallel",)),
    )(slot, w2d)


def _place():
    x, y, c = lax.axis_index("x"), lax.axis_index("y"), lax.axis_index("c")
    return x, y, c, [(1 - x, y), (x, 1 - y), (1 - x, 1 - y)]


def _remote(src, dst, send_sem, recv_sem, to):
    return pltpu.make_async_remote_copy(src_ref=src, dst_ref=dst, send_sem=send_sem, recv_sem=recv_sem,
                                        device_id=to, device_id_type=MESH)


class _AllGather:
    def __init__(self, bufs, part=(0, 1)):
        self.arrays = list(bufs)
        self.outs = [jax.ShapeDtypeStruct(b.shape, b.dtype) for b in bufs]
        self.aliased = True
        self.n_sems = 6 * len(bufs)
        self.part = part

    def run(self, phase, ins, outs, send_sems, recv_sems, base):
        x, y, c, chips = _place()
        j = 2 * x + y
        for k, buf in enumerate(outs):
            rows = buf.shape[1] // self.part[1]
            half = rows // 2

            def part(src_j, h):
                return buf.at[src_j, pl.ds(self.part[0] * rows + h * half, half), :]

            def ici(r, src_j, to):
                s = base + 6 * k + r
                return _remote(part(src_j, c), part(src_j, c), send_sems.at[s], recv_sems.at[s], to)

            def d2d(r, src_j, h):
                s = base + 6 * k + 3 + r
                return _remote(part(src_j, h), part(src_j, h), send_sems.at[s], recv_sems.at[s], (x, y, 1 - c))

            for r, (px, py) in enumerate(chips):
                if phase == 0:
                    ici(r, j, (px, py, c)).start()
                elif phase == 1:
                    ici(r, 2 * px + py, (px, py, c)).wait_recv()
                    d2d(r, 2 * px + py, c).start()
                else:
                    d2d(r, 2 * px + py, 1 - c).wait_recv()
                    ici(r, j, (px, py, c)).wait_send()
                    d2d(r, 2 * px + py, c).wait_send()


class _Swap:
    def __init__(self, grads):
        self.arrays = list(grads)
        self.outs = [jax.ShapeDtypeStruct((g.shape[0],) + g.shape[2:], g.dtype) for g in grads]
        self.aliased = False
        self.n_sems = len(grads)

    def run(self, phase, ins, outs, send_sems, recv_sems, base):
        x, y, c, _ = _place()
        for k in range(len(ins)):
            cp = _remote(ins[k].at[:, 1 - c], outs[k], send_sems.at[base + k], recv_sems.at[base + k], (x, y, 1 - c))
            if phase == 0:
                cp.start()
            elif phase == 2:
                cp.wait()


class _Exchange:
    def __init__(self, parts):
        self.arrays = list(parts)
        self.outs = [jax.ShapeDtypeStruct((3,) + p.shape[1:], p.dtype) for p in parts]
        self.aliased = False
        self.n_sems = 3 * len(parts)

    def run(self, phase, ins, outs, send_sems, recv_sems, base):
        x, y, c, chips = _place()
        for k in range(len(ins)):
            for r, (px, py) in enumerate(chips):
                s = base + 3 * k + r
                cp = _remote(ins[k].at[2 * px + py], outs[k].at[r], send_sems.at[s], recv_sems.at[s], (px, py, c))
                if phase == 0:
                    cp.start()
                elif phase == 2:
                    cp.wait()


class _Join:
    def __init__(self, bufs):
        self.arrays = list(bufs)
        self.outs = [jax.ShapeDtypeStruct(b.shape, b.dtype) for b in bufs]
        self.aliased = True
        self.n_sems = len(bufs)

    def run(self, phase, ins, outs, send_sems, recv_sems, base):
        x, y, c, _ = _place()
        for k, buf in enumerate(outs):
            mine = _remote(buf.at[c], buf.at[c], send_sems.at[base + k], recv_sems.at[base + k], (x, y, 1 - c))
            if phase == 0:
                mine.start()
            elif phase == 2:
                mine.wait_send()
                _remote(buf.at[1 - c], buf.at[1 - c], send_sems.at[base + k], recv_sems.at[base + k],
                        (x, y, 1 - c)).wait_recv()


class _Comm:
    def __init__(self, ops):
        self.ops = list(ops)
        self.arrays = [a for op in self.ops for a in op.arrays]
        self.outs = [o for op in self.ops for o in op.outs]
        self.n_sems = sum(op.n_sems for op in self.ops)

    def specs(self):
        return [pl.BlockSpec(memory_space=pl.ANY)] * len(self.arrays)

    def aliases(self, first_in, first_out):
        out, k = {}, 0
        for op in self.ops:
            for i in range(len(op.arrays)):
                if op.aliased:
                    out[first_in + k + i] = first_out + k + i
            k += len(op.arrays)
        return out

    def scratch(self):
        return [pltpu.SemaphoreType.DMA((self.n_sems,)), pltpu.SemaphoreType.DMA((self.n_sems,))]

    def run(self, phase, ins, outs, send_sems, recv_sems):
        k = base = 0
        for op in self.ops:
            n = len(op.arrays)
            op.run(phase, ins[k:k + n], outs[k:k + n], send_sems, recv_sems, base)
            k += n
            base += op.n_sems

    def split(self, results):
        out, k = [], 0
        for op in self.ops:
            out.append(list(results[k:k + len(op.arrays)]))
            k += len(op.arrays)
        return out


def _communicate(ops):
    comm = _Comm(ops)
    n = len(comm.arrays)

    def body(*refs):
        ins, outs, (send_sems, recv_sems) = refs[:n], refs[n:2 * n], refs[2 * n:]
        for phase in range(3):
            comm.run(phase, ins, outs, send_sems, recv_sems)

    res = pl.pallas_call(
        body, name="communicate", out_shape=tuple(comm.outs), in_specs=comm.specs(), out_specs=tuple(comm.specs()),
        input_output_aliases=comm.aliases(0, 0), scratch_shapes=comm.scratch(),
    )(*comm.arrays)
    return comm.split(res)


def _pool_core(he, w_ref, scale, first_row, halo, n_rows):
    d = he.shape[1]
    gd = d // len(POOL_WINDOWS)
    t = first_row + lax.broadcasted_iota(jnp.int32, (n_rows, 1), 0)
    pooled, ypre, cnts = [], [], []
    for g, w in enumerate(POOL_WINDOWS):
        hg = he[:, g * gd:(g + 1) * gd]
        s, k = hg, 1
        while k < w:
            s = s + _shift_down(s, k)
            k *= 2
        cnt = jnp.minimum(t + 1, w).astype(F32)
        p = s[halo:] / cnt - hg[halo:]
        pooled.append(p.astype(BF16))
        cnts.append(cnt)
        ypre.append(_dot(pooled[-1], w_ref[g]))
    return pooled, jnp.concatenate(ypre, axis=1), cnts


def _pool_forward(x, vec, pool_w, comm=None):
    s, d = x.shape
    ts = _row_tile(s, 512)
    nb = s // ts
    n_g, gd, _ = pool_w.shape
    comm = comm or _Comm([])
    nc = len(comm.arrays)

    def body(*refs):
        x_ref, vec_ref, w_ref = refs[:3]
        cin = refs[3:3 + nc]
        o_ref = refs[3 + nc]
        cout = refs[4 + nc:4 + 2 * nc]
        carry = refs[4 + 2 * nc]
        sems = refs[5 + 2 * nc:]
        i = pl.program_id(0)

        @pl.when(i == 0)
        def _():
            carry[...] = jnp.zeros_like(carry)
            if nc:
                comm.run(0, cin, cout, *sems)

        if nc:
            @pl.when(i == nb - 1)
            def _():
                comm.run(1, cin, cout, *sems)

        vec = vec_ref[...]
        a, sh, gt, gpost = _vec_rows(vec)
        xb = x_ref[...]
        xn, _ = _rms(xb)
        h = xn * a + sh
        he = jnp.concatenate([carry[...], h], axis=0)
        carry[...] = h[ts - POOL_HALO:]
        _, ypre, _ = _pool_core(he, w_ref, vec[5:6], i * ts, POOL_HALO, ts)
        yn, _ = _rms(ypre * vec[5:6])
        o_ref[...] = xb + gt * (yn * gpost)
        if nc:
            @pl.when(i == nb - 1)
            def _():
                comm.run(2, cin, cout, *sems)

    res = pl.pallas_call(
        body, name="pool_forward", grid=(nb,),
        in_specs=[pl.BlockSpec((ts, d), lambda i: (i, 0)), pl.BlockSpec((8, d), lambda i: (0, 0)),
                  pl.BlockSpec((n_g, gd, gd), lambda i: (0, 0, 0))] + comm.specs(),
        out_specs=(pl.BlockSpec((ts, d), lambda i: (i, 0)), *comm.specs()),
        out_shape=(jax.ShapeDtypeStruct((s, d), F32), *comm.outs),
        input_output_aliases=comm.aliases(3, 1),
        scratch_shapes=[pltpu.VMEM((POOL_HALO, d), F32)] + (comm.scratch() if nc else []),
        compiler_params=_cparams(("arbitrary",)),
    )(x, vec, pool_w, *comm.arrays)
    return res[0], comm.split(res[1:])


def _pool_backward(dout, x, vec, pool_w):
    s, d = x.shape
    ts = _row_tile(s, 512)
    nb = s // ts
    hb = ts // POOL_HALO
    n_g, gd, _ = pool_w.shape

    def body(do_ref, x_ref, xh_ref, vec_ref, w_ref, dx_ref, sum_ref, dw_ref, carry):
        step = pl.program_id(0)
        i = nb - 1 - step

        @pl.when(step == 0)
        def _():
            carry[...] = jnp.zeros_like(carry)
            sum_ref[...] = jnp.zeros_like(sum_ref)
            dw_ref[...] = jnp.zeros_like(dw_ref)

        vec = vec_ref[...]
        a, sh, gt, gpost = _vec_rows(vec)
        scale = vec[5:6]
        do = do_ref[...]
        xe = jnp.concatenate([xh_ref[...], x_ref[...]], axis=0)
        xne, re = _rms(xe)
        he = xne * a + sh
        rowid = lax.broadcasted_iota(jnp.int32, (POOL_HALO + ts, 1), 0)
        he = jnp.where((rowid >= POOL_HALO) | (i > 0), he, 0.0)
        xn, r = xne[POOL_HALO:], re[POOL_HALO:]
        pooled, ypre, cnts = _pool_core(he, w_ref, scale, i * ts, POOL_HALO, ts)
        yn, ry = _rms(ypre * scale)
        dyn = do * (gt * gpost)
        dy = _rms_bwd(dyn, yn, ry)
        dypre = (dy * scale).astype(BF16)
        dh_parts, q_parts = [], []
        for g, w in enumerate(POOL_WINDOWS):
            dyg = dypre[:, g * gd:(g + 1) * gd]
            dpool = _dot_nt(dyg, w_ref[g])
            dw_ref[g] += _dot_tn(pooled[g], dyg)
            q = dpool / cnts[g]
            qe = jnp.concatenate([q, carry[:, g * gd:(g + 1) * gd]], axis=0)
            acc, k = qe, 1
            while k < w:
                acc = acc + _shift_up(acc, k)
                k *= 2
            dh_parts.append(acc[:ts] - dpool)
            q_parts.append(q[:POOL_HALO])
        carry[...] = jnp.concatenate(q_parts, axis=1)
        dh = jnp.concatenate(dh_parts, axis=1)
        dxn = dh * a
        dx_ref[...] = do + _rms_bwd(dxn, xn, r)
        _add_rows(sum_ref, _norm_sums(do, yn, dh, xn, vec) + [_colsum(dh), _colsum(dy * ypre)])

    blk = lambda st: (nb - 1 - st, 0)
    return pl.pallas_call(
        body, name="pool_backward", grid=(nb,),
        in_specs=[pl.BlockSpec((ts, d), blk), pl.BlockSpec((ts, d), blk),
                  pl.BlockSpec((POOL_HALO, d), lambda st: (jnp.maximum((nb - 1 - st) * hb - 1, 0), 0)),
                  pl.BlockSpec((8, d), lambda st: (0, 0)), pl.BlockSpec((n_g, gd, gd), lambda st: (0, 0, 0))],
        out_specs=(pl.BlockSpec((ts, d), blk), pl.BlockSpec((8, d), lambda st: (0, 0)),
                   pl.BlockSpec((n_g, gd, gd), lambda st: (0, 0, 0))),
        out_shape=(jax.ShapeDtypeStruct((s, d), F32), jax.ShapeDtypeStruct((8, d), F32),
                   jax.ShapeDtypeStruct((n_g, gd, gd), F32)),
        scratch_shapes=[pltpu.VMEM((POOL_HALO, d), F32)],
        compiler_params=_cparams(("arbitrary",)),
    )(dout, x, x, vec, pool_w)


def _ffn_forward(x, vec, w_up, w_dw, w_down, comm=None, target=None):
    s, d = x.shape
    _, _, cs = w_up.shape
    ts = _row_tile(s, 256)
    nb = s // ts
    chunks = _lane_chunks(cs)
    comm = comm or _Comm([])
    nc = len(comm.arrays)
    nl = 0 if target is None else 1
    n_in, n_out = 5 + nl, 6 + nl

    def body(*refs):
        x_ref, vec_ref, wup_ref, wdw_ref, wdn_ref = refs[:5]
        cin = refs[n_in:n_in + nc]
        o_ref, h_ref, a0_ref, cc_ref, u_ref, y_ref = refs[n_in + nc:n_in + nc + 6]
        loss_ref = refs[n_in + nc + 6] if nl else None
        cout = refs[n_in + nc + n_out:n_in + 2 * nc + n_out]
        carry = refs[n_in + 2 * nc + n_out]
        sems = refs[n_in + 2 * nc + n_out + 1:]
        i = pl.program_id(0)

        @pl.when(i == 0)
        def _():
            carry[...] = jnp.zeros_like(carry)
            if nl:
                loss_ref[...] = jnp.zeros_like(loss_ref)
            if nc:
                comm.run(0, cin, cout, *sems)

        if nc:
            @pl.when(i == (3 * nb) // 4)
            def _():
                comm.run(1, cin, cout, *sems)

        vec = vec_ref[...]
        a, sh, gt, gpost = _vec_rows(vec)
        xb = x_ref[...]
        xn, _ = _rms(xb)
        hb = (xn * a + sh).astype(BF16)
        h_ref[...] = hb
        for q in range(2):
            for c0, cw in chunks:
                conv = []
                for j in (q, q + 2):
                    a0 = _dot(hb, wup_ref[j, :, c0:c0 + cw])
                    a0_ref[j, :, c0:c0 + cw] = a0.astype(BF16)
                    ae = jnp.concatenate([carry[j, :, c0:c0 + cw], a0], axis=0)
                    carry[j, :, c0:c0 + cw] = a0[ts - FFN_HALO:]
                    w = wdw_ref[:, j * cs + c0:j * cs + c0 + cw]
                    conv.append((w[2:3] * ae + w[1:2] * _shift_down(ae, 1) + w[0:1] * _shift_down(ae, 2))[FFN_HALO:])
                    cc_ref[j, :, c0:c0 + cw] = conv[-1].astype(BF16)
                u_ref[q, :, c0:c0 + cw] = (conv[0] * _sigmoid(conv[0]) * conv[1]).astype(BF16)
        y = _dot(u_ref[0], wdn_ref[0]) + _dot(u_ref[1], wdn_ref[1])
        y_ref[...] = y
        yn, _ = _rms(y)
        x_out = xb + gt * (yn * gpost)
        if nl:
            err = x_out - refs[5][...]
            o_ref[...] = err * (1.0 / d)
            loss_ref[0:1, :] += _colsum(err * err) * (0.5 / d)
        else:
            o_ref[...] = x_out
        if nc:
            @pl.when(i == nb - 1)
            def _():
                comm.run(2, cin, cout, *sems)

    const3 = lambda i: (0, 0, 0)
    res = pl.pallas_call(
        body, name="ffn_forward", grid=(nb,),
        in_specs=[pl.BlockSpec((ts, d), lambda i: (i, 0)), pl.BlockSpec((8, d), lambda i: (0, 0)),
                  pl.BlockSpec(w_up.shape, const3, pipeline_mode=pl.Buffered(1)),
                  pl.BlockSpec(w_dw.shape, lambda i: (0, 0)),
                  pl.BlockSpec(w_down.shape, const3, pipeline_mode=pl.Buffered(1))]
        + [pl.BlockSpec((ts, d), lambda i: (i, 0))] * nl + comm.specs(),
        out_specs=(pl.BlockSpec((ts, d), lambda i: (i, 0)), pl.BlockSpec((ts, d), lambda i: (i, 0)),
                   pl.BlockSpec((4, ts, cs), lambda i: (0, i, 0)), pl.BlockSpec((4, ts, cs), lambda i: (0, i, 0)),
                   pl.BlockSpec((2, ts, cs), lambda i: (0, i, 0)), pl.BlockSpec((ts, d), lambda i: (i, 0)),
                   *[pl.BlockSpec((8, d), lambda i: (0, 0))] * nl, *comm.specs()),
        out_shape=(jax.ShapeDtypeStruct((s, d), F32), jax.ShapeDtypeStruct((s, d), BF16),
                   jax.ShapeDtypeStruct((4, s, cs), BF16), jax.ShapeDtypeStruct((4, s, cs), BF16),
                   jax.ShapeDtypeStruct((2, s, cs), BF16), jax.ShapeDtypeStruct((s, d), F32),
                   *[jax.ShapeDtypeStruct((8, d), F32)] * nl, *comm.outs),
        input_output_aliases=comm.aliases(n_in, n_out),
        scratch_shapes=[pltpu.VMEM((4, FFN_HALO, cs), F32)] + (comm.scratch() if nc else []),
        compiler_params=_cparams(("arbitrary",)),
    )(x, vec, w_up, w_dw, w_down, *([target] * nl), *comm.arrays)
    return res[:n_out], comm.split(res[n_out:])


def _ffn_backward(dout, x, y, a0, cc, vec, w_up, w_dw, w_down):
    s, d = x.shape
    _, _, cs = w_up.shape
    ts = _row_tile(s, 256)
    nb = s // ts
    chunks = _lane_chunks(cs)

    def body(do_ref, x_ref, y_ref, a0_ref, cc_ref, vec_ref, wup_ref, wdw_ref, wdn_ref,
             dx_ref, da0_ref, dy_ref, sum_ref, dwdw_ref, carry, du_s, dh_s):
        step = pl.program_id(0)

        @pl.when(step == 0)
        def _():
            carry[...] = jnp.zeros_like(carry)
            sum_ref[...] = jnp.zeros_like(sum_ref)
            dwdw_ref[...] = jnp.zeros_like(dwdw_ref)

        vec = vec_ref[...]
        a, sh, gt, gpost = _vec_rows(vec)
        do = do_ref[...]
        yn, ry = _rms(y_ref[...])
        dy = _rms_bwd(do * (gt * gpost), yn, ry)
        dyb = dy.astype(BF16)
        dy_ref[...] = dyb
        order = [(q, c0, cw) for q in range(2) for c0, cw in chunks]

        def du_pieces(idx):
            q, c0, cw = order[idx]
            return [functools.partial(_store_dot_nt, du_s.at[idx % 2, :, n0:min(n0 + MXU_LANES, cw)], dy_ref,
                                      wdn_ref.at[q, c0 + n0:c0 + min(n0 + MXU_LANES, cw), :])
                    for n0 in range(0, cw, MXU_LANES)]

        def dh_pieces():
            return [functools.partial(_store_dot_nt2, dh_s.at[:, n0:n0 + MXU_LANES], da0_ref.at[0], da0_ref.at[2],
                                      wup_ref.at[0, n0:n0 + MXU_LANES, :], wup_ref.at[2, n0:n0 + MXU_LANES, :])
                    for n0 in range(0, d, MXU_LANES)]

        for piece in du_pieces(0):
            piece()
        later = dh_pieces()
        for idx, (q, c0, cw) in enumerate(order):
            work = du_pieces(idx + 1) if idx + 1 < len(order) else []
            if q == 1:
                share = -(-len(later) // (len(order) - idx))
                work, later = work + later[:share], later[share:]

            def pump(part, of=3):
                for piece in work[part::of]:
                    piece()

            cg = cc_ref[q, :, c0:c0 + cw].astype(F32)
            cv = cc_ref[q + 2, :, c0:c0 + cw].astype(F32)
            sg = _sigmoid(cg)
            sl = cg * sg
            du = du_s[idx % 2, :, :cw]
            dconv = {q: du * cv * (sg * (1.0 + cg * (1.0 - sg))), q + 2: du * sl}
            pump(0)
            for part, j in enumerate((q, q + 2)):
                dae = jnp.concatenate([dconv[j], carry[j, :, c0:c0 + cw]], axis=0)
                carry[j, :, c0:c0 + cw] = dconv[j][:FFN_HALO]
                up1 = _shift_down(dae, FFN_HALO - 1)[FFN_HALO:]
                up2 = _shift_down(dae, FFN_HALO - 2)[FFN_HALO:]
                lanes = slice(j * cs + c0, j * cs + c0 + cw)
                w = wdw_ref[:, lanes]
                da0_ref[j, :, c0:c0 + cw] = (w[2:3] * dconv[j] + w[1:2] * up1 + w[0:1] * up2).astype(BF16)
                a0 = a0_ref[j, :, c0:c0 + cw].astype(F32)
                dwdw_ref[0:1, lanes] += _colsum(up2 * a0)
                dwdw_ref[1:2, lanes] += _colsum(up1 * a0)
                dwdw_ref[2:3, lanes] += _colsum(dconv[j] * a0)
                pump(part + 1)
        dh = dh_s[...] + _dot_nt(da0_ref[1], wup_ref[1]) + _dot_nt(da0_ref[3], wup_ref[3])
        xn, r = _rms(x_ref[...])
        dx_ref[...] = do + _rms_bwd(dh * a, xn, r)
        _add_rows(sum_ref, _norm_sums(do, yn, dh, xn, vec) + [_colsum(dh)])

    blk = lambda st: (nb - 1 - st, 0)
    blk3 = lambda st: (0, nb - 1 - st, 0)
    const3 = lambda st: (0, 0, 0)
    return pl.pallas_call(
        body, name="ffn_backward", grid=(nb,),
        in_specs=[pl.BlockSpec((ts, d), blk), pl.BlockSpec((ts, d), blk), pl.BlockSpec((ts, d), blk),
                  pl.BlockSpec((4, ts, cs), blk3), pl.BlockSpec((4, ts, cs), blk3),
                  pl.BlockSpec((8, d), lambda st: (0, 0)),
                  pl.BlockSpec(w_up.shape, const3, pipeline_mode=pl.Buffered(1)),
                  pl.BlockSpec(w_dw.shape, lambda st: (0, 0)),
                  pl.BlockSpec(w_down.shape, const3, pipeline_mode=pl.Buffered(1))],
        out_specs=(pl.BlockSpec((ts, d), blk), pl.BlockSpec((4, ts, cs), blk3),
                   pl.BlockSpec((ts, d), blk), pl.BlockSpec((8, d), lambda st: (0, 0)),
                   pl.BlockSpec((8, 4 * cs), lambda st: (0, 0))),
        out_shape=(jax.ShapeDtypeStruct((s, d), F32), jax.ShapeDtypeStruct((4, s, cs), BF16),
                   jax.ShapeDtypeStruct((s, d), BF16),
                   jax.ShapeDtypeStruct((8, d), F32), jax.ShapeDtypeStruct((8, 4 * cs), F32)),
        scratch_shapes=[pltpu.VMEM((4, FFN_HALO, cs), F32), pltpu.VMEM((2, ts, max(cw for _, cw in chunks)), F32),
                        pltpu.VMEM((ts, d), F32)],
        compiler_params=_cparams(("arbitrary",)),
    )(dout, x, y, a0, cc, vec, w_up, w_dw, w_down)


def _conv_halo(width):
    return -(-(width - 1) // 8) * 8


def _conv_forward(x, vec, cvec, w_pw1, b_pw1, w_dw, w_pw2):
    s, d = x.shape
    kw = w_dw.shape[0]
    halo = _conv_halo(kw)
    ts = _row_tile(s, 512)
    hd = d // 2

    def body(x_ref, vec_ref, cvec_ref, w1_ref, b1_ref, wdw_ref, w2_ref,
             o_ref, h_ref, a_ref, uc_ref, z_ref, y_ref, carry):
        i = pl.program_id(0)

        @pl.when(i == 0)
        def _():
            carry[...] = jnp.zeros_like(carry)

        vec, cvec = vec_ref[...], cvec_ref[...]
        a, sh, gt, gpost = _vec_rows(vec)
        xb = x_ref[...]
        xn, _ = _rms(xb)
        hb = (xn * a + sh).astype(BF16)
        h_ref[...] = hb
        for j in range(4):
            a_ref[:, j * hd:(j + 1) * hd] = _dot(hb, w1_ref[j]) + b1_ref[:, j * hd:(j + 1) * hd]
        u = a_ref[:, :d] * _sigmoid(a_ref[:, d:])
        carry[halo:, :] = u
        for r0 in range(0, ts, CONV_ROWS):
            for l0 in range(0, d, CONV_LANES):
                lanes = slice(l0, l0 + CONV_LANES)
                src = carry[r0:r0 + CONV_ROWS + halo, lanes]
                acc = jnp.zeros((CONV_ROWS, CONV_LANES), F32) + cvec[0:1, lanes]
                for k in range(kw):
                    acc = acc + wdw_ref[k:k + 1, lanes] * _shift_down(src, kw - 1 - k)[halo:]
                uc_ref[r0:r0 + CONV_ROWS, lanes] = acc
        carry[:halo, :] = u[ts - halo:]
        uc = uc_ref[...]
        mu = jnp.mean(uc, axis=-1, keepdims=True)
        cen = uc - mu
        rstd = lax.rsqrt(jnp.mean(cen * cen, axis=-1, keepdims=True) + EPS)
        l = cen * rstd * cvec[1:2] + cvec[2:3]
        zb = (l * _sigmoid(l)).astype(BF16)
        z_ref[...] = zb
        y = _dot(zb, w2_ref[...]) + cvec[3:4]
        y_ref[...] = y
        yn, _ = _rms(y)
        o_ref[...] = xb + gt * (yn * gpost)

    row = lambda i: (i, 0)
    const2 = lambda i: (0, 0)
    return pl.pallas_call(
        body, name="conv_forward", grid=(s // ts,),
        in_specs=[pl.BlockSpec((ts, d), row), pl.BlockSpec((8, d), const2), pl.BlockSpec((8, d), const2),
                  pl.BlockSpec(w_pw1.shape, lambda i: (0, 0, 0)), pl.BlockSpec(b_pw1.shape, const2),
                  pl.BlockSpec(w_dw.shape, const2), pl.BlockSpec(w_pw2.shape, const2)],
        out_specs=(pl.BlockSpec((ts, d), row), pl.BlockSpec((ts, d), row), pl.BlockSpec((ts, 2 * d), row),
                   pl.BlockSpec((ts, d), row), pl.BlockSpec((ts, d), row), pl.BlockSpec((ts, d), row)),
        out_shape=(jax.ShapeDtypeStruct((s, d), F32), jax.ShapeDtypeStruct((s, d), BF16),
                   jax.ShapeDtypeStruct((s, 2 * d), F32), jax.ShapeDtypeStruct((s, d), F32),
                   jax.ShapeDtypeStruct((s, d), BF16), jax.ShapeDtypeStruct((s, d), F32)),
        scratch_shapes=[pltpu.VMEM((halo + ts, d), F32)],
        compiler_params=_cparams(("arbitrary",)),
    )(x, vec, cvec, w_pw1, b_pw1, w_dw, w_pw2)


def _conv_backward(dout, x, y, a_pre, uc, vec, cvec, w_pw1, w_dw, w_pw2):
    s, d = x.shape
    kw = w_dw.shape[0]
    kpad = -(-kw // 8) * 8
    halo = _conv_halo(kw)
    ts = _row_tile(s, 512)
    nb = s // ts
    hb = ts // halo
    hd = d // 2

    def body(do_ref, x_ref, y_ref, a_ref, ah_ref, uc_ref, vec_ref, cvec_ref, w1_ref, wdw_ref, w2_ref,
             dx_ref, da_ref, dy_ref, sum_ref, dwdw_ref, carry):
        step = pl.program_id(0)
        i = nb - 1 - step

        @pl.when(step == 0)
        def _():
            carry[...] = jnp.zeros_like(carry)
            sum_ref[...] = jnp.zeros_like(sum_ref)
            dwdw_ref[...] = jnp.zeros_like(dwdw_ref)

        vec, cvec = vec_ref[...], cvec_ref[...]
        a, sh, gt, gpost = _vec_rows(vec)
        do = do_ref[...]
        yn, ry = _rms(y_ref[...])
        dy = _rms_bwd(do * (gt * gpost), yn, ry)
        dyb = dy.astype(BF16)
        dy_ref[...] = dyb
        dz = _dot_nt(dyb, w2_ref[...])
        uc = uc_ref[...]
        mu = jnp.mean(uc, axis=-1, keepdims=True)
        cen = uc - mu
        rstd = lax.rsqrt(jnp.mean(cen * cen, axis=-1, keepdims=True) + EPS)
        lhat = cen * rstd
        l = lhat * cvec[1:2] + cvec[2:3]
        sgl = _sigmoid(l)
        dl = dz * (sgl * (1.0 + l * (1.0 - sgl)))
        dlhat = dl * cvec[1:2]
        duc = rstd * (dlhat - jnp.mean(dlhat, axis=-1, keepdims=True)
                      - lhat * jnp.mean(dlhat * lhat, axis=-1, keepdims=True))
        ae = jnp.concatenate([ah_ref[...] * (i > 0).astype(F32), a_ref[...]], axis=0)
        sgate = _sigmoid(ae[:, d:])
        val = ae[:, :d]
        ue = val * sgate
        rowid = lax.broadcasted_iota(jnp.int32, (halo + ts, 1), 0)
        ue = jnp.where((rowid >= halo) | (i > 0), ue, 0.0)
        duce = jnp.concatenate([duc, carry[...]], axis=0)
        carry[...] = duc[:halo]
        du = jnp.zeros((ts, d), F32)
        for k in range(kw):
            du = du + wdw_ref[k:k + 1, :] * _shift_down(duce, halo - (kw - 1 - k))[halo:]
            dwdw_ref[k:k + 1, :] += _colsum(duc * _shift_down(ue, kw - 1 - k)[halo:])
        sg, vl = sgate[halo:], val[halo:]
        dval = du * sg
        dgate = du * vl * (sg * (1.0 - sg))
        dvb, dgb = dval.astype(BF16), dgate.astype(BF16)
        dh = jnp.zeros((ts, d), F32)
        for j in range(2):
            da_ref[j] = dvb[:, j * hd:(j + 1) * hd]
            da_ref[j + 2] = dgb[:, j * hd:(j + 1) * hd]
            dh = dh + _dot_nt(dvb[:, j * hd:(j + 1) * hd], w1_ref[j]) + _dot_nt(dgb[:, j * hd:(j + 1) * hd], w1_ref[j + 2])
        xn, r = _rms(x_ref[...])
        dx_ref[...] = do + _rms_bwd(dh * a, xn, r)
        _add_rows(sum_ref, _norm_sums(do, yn, dh, xn, vec) + [_colsum(dh), _colsum(dy), _colsum(dl * lhat), _colsum(dl),
                            _colsum(duc), _colsum(dval), _colsum(dgate)])

    blk = lambda st: (nb - 1 - st, 0)
    const2 = lambda st: (0, 0)
    return pl.pallas_call(
        body, name="conv_backward", grid=(nb,),
        in_specs=[pl.BlockSpec((ts, d), blk), pl.BlockSpec((ts, d), blk), pl.BlockSpec((ts, d), blk),
                  pl.BlockSpec((ts, 2 * d), blk),
                  pl.BlockSpec((halo, 2 * d), lambda st: (jnp.maximum((nb - 1 - st) * hb - 1, 0), 0)),
                  pl.BlockSpec((ts, d), blk), pl.BlockSpec((8, d), const2), pl.BlockSpec((8, d), const2),
                  pl.BlockSpec(w_pw1.shape, lambda st: (0, 0, 0)), pl.BlockSpec(w_dw.shape, const2),
                  pl.BlockSpec(w_pw2.shape, const2)],
        out_specs=(pl.BlockSpec((ts, d), blk), pl.BlockSpec((4, ts, hd), lambda st: (0, nb - 1 - st, 0)),
                   pl.BlockSpec((ts, d), blk), pl.BlockSpec((16, d), const2), pl.BlockSpec((kpad, d), const2)),
        out_shape=(jax.ShapeDtypeStruct((s, d), F32), jax.ShapeDtypeStruct((4, s, hd), BF16),
                   jax.ShapeDtypeStruct((s, d), BF16), jax.ShapeDtypeStruct((16, d), F32),
                   jax.ShapeDtypeStruct((kpad, d), F32)),
        scratch_shapes=[pltpu.VMEM((halo, d), F32)],
        compiler_params=_cparams(("arbitrary",)),
    )(dout, x, y, a_pre, a_pre, uc, vec, cvec, w_pw1, w_dw, w_pw2)


def _weight_grad(a, b, comm=None):
    na, s, k = a.shape
    nb_, _, n = b.shape
    nj = max(na, nb_)
    ts = _row_tile(s, 2048)
    nt = s // ts
    comm = comm or _Comm([])
    nc = len(comm.arrays)

    def body(*refs):
        a_ref, b_ref = refs[:2]
        cin = refs[2:2 + nc]
        o_ref = refs[2 + nc]
        cout = refs[3 + nc:3 + 2 * nc]
        sems = refs[3 + 2 * nc:]
        j, t = pl.program_id(0), pl.program_id(1)

        if nc:
            @pl.when((j == 0) & (t == 0))
            def _():
                comm.run(0, cin, cout, *sems)

            @pl.when((j == nj // 2) & (t == nt // 2))
            def _():
                comm.run(1, cin, cout, *sems)

        @pl.when(t == 0)
        def _():
            o_ref[...] = jnp.zeros_like(o_ref)

        o_ref[0] += _dot_tn(a_ref[0], b_ref[0])

        if nc:
            @pl.when((j == nj - 1) & (t == nt - 1))
            def _():
                comm.run(2, cin, cout, *sems)

    res = pl.pallas_call(
        body, name="weight_grad", grid=(nj, nt),
        in_specs=[pl.BlockSpec((1, ts, k), (lambda j, t: (j, t, 0)) if na > 1 else (lambda j, t: (0, t, 0))),
                  pl.BlockSpec((1, ts, n), (lambda j, t: (j, t, 0)) if nb_ > 1 else (lambda j, t: (0, t, 0)))]
        + comm.specs(),
        out_specs=(pl.BlockSpec((1, k, n), lambda j, t: (j, 0, 0)), *comm.specs()),
        out_shape=(jax.ShapeDtypeStruct((nj, k, n), F32), *comm.outs),
        input_output_aliases=comm.aliases(2, 1),
        scratch_shapes=comm.scratch() if nc else [],
        compiler_params=_cparams(("arbitrary", "arbitrary") if nc else ("parallel", "arbitrary")),
    )(a, b, *comm.arrays)
    return res[0], comm.split(res[1:])


def _adamw_math(w, g, m, v):
    nm = ADAM_B1 * m + (1.0 - ADAM_B1) * g
    nv = ADAM_B2 * v + (1.0 - ADAM_B2) * (g * g)
    m_hat = nm * (1.0 / (1.0 - ADAM_B1 ** ADAM_STEP))
    v_hat = nv * (1.0 / (1.0 - ADAM_B2 ** ADAM_STEP))
    return -ADAM_LR * (m_hat / (jnp.sqrt(v_hat) + ADAM_EPS) + ADAM_WD * w), nm, nv


def _adamw_many(params):
    n = len(params)

    def body(*refs):
        for k in range(n):
            w_ref, g_ref, m_ref, v_ref = refs[4 * k:4 * k + 4]
            outs = refs[4 * n + 3 * k:4 * n + 3 * k + 3]
            for o_ref, val in zip(outs, _adamw_math(w_ref[...], g_ref[...], m_ref[...], v_ref[...])):
                o_ref[...] = val

    vm = pl.BlockSpec(memory_space=pltpu.VMEM)
    res = pl.pallas_call(
        body, name="adamw_many", in_specs=[vm] * (4 * n), out_specs=tuple([vm] * (3 * n)),
        out_shape=tuple(jax.ShapeDtypeStruct(p[0].shape, F32) for p in params for _ in range(3)),
        compiler_params=_cparams(),
    )(*[a for p in params for a in p])
    return [res[3 * k:3 * k + 3] for k in range(n)]


def _adamw(w, g, m, v, comm=None):
    nl, r, c = w.shape
    tr = _row_tile(r, 256)
    nr = r // tr
    comm = comm or _Comm([])
    nc = len(comm.arrays)

    def body(*refs):
        w_ref, g_ref, m_ref, v_ref = refs[:4]
        cin = refs[4:4 + nc]
        d_ref, nm_ref, nv_ref = refs[4 + nc:7 + nc]
        cout = refs[7 + nc:7 + 2 * nc]
        sems = refs[7 + 2 * nc:]
        l, i = pl.program_id(0), pl.program_id(1)
        if nc:
            @pl.when((l == 0) & (i == 0))
            def _():
                comm.run(0, cin, cout, *sems)

            @pl.when((l == nl // 2) & (i == nr // 2))
            def _():
                comm.run(1, cin, cout, *sems)

        d_ref[...], nm_ref[...], nv_ref[...] = _adamw_math(w_ref[...], g_ref[...], m_ref[...], v_ref[...])
        if nc:
            @pl.when((l == nl - 1) & (i == nr - 1))
            def _():
                comm.run(2, cin, cout, *sems)

    spec = pl.BlockSpec((1, tr, c), lambda l, i: (l, i, 0))
    shp = jax.ShapeDtypeStruct((nl, r, c), F32)
    res = pl.pallas_call(
        body, name="adamw", grid=(nl, nr), in_specs=[spec] * 4 + comm.specs(),
        out_specs=(spec,) * 3 + tuple(comm.specs()), out_shape=(shp,) * 3 + tuple(comm.outs),
        input_output_aliases=comm.aliases(4, 3), scratch_shapes=comm.scratch() if nc else [],
        compiler_params=_cparams(("arbitrary", "arbitrary") if nc else ("parallel", "parallel")),
    )(w, g, m, v, *comm.arrays)
    return res[:3], comm.split(res[3:])


def _add_my_half(g, other, idx):
    _, _, h, c = g.shape
    th = _row_tile(h, 256)

    def body(idx_ref, g_ref, o_ref, out_ref):
        out_ref[...] = (g_ref[:, 0] + o_ref[...]).astype(BF16)

    return pl.pallas_call(
        body, name="add_my_half",
        grid_spec=pltpu.PrefetchScalarGridSpec(
            num_scalar_prefetch=1, grid=(4, h // th),
            in_specs=[pl.BlockSpec((1, 1, th, c), lambda j, i, idx_ref: (j, idx_ref[1], i, 0)),
                      pl.BlockSpec((1, th, c), lambda j, i, idx_ref: (j, i, 0))],
            out_specs=pl.BlockSpec((1, th, c), lambda j, i, idx_ref: (j, i, 0))),
        out_shape=jax.ShapeDtypeStruct(other.shape, BF16),
        compiler_params=_cparams(("parallel", "parallel")),
    )(idx, g, other)


def _sum_for_my_chip(g, other, got, idx):
    _, _, h, c = g.shape
    th = _row_tile(h, 256)

    def body(idx_ref, g_ref, o_ref, q_ref, out_ref):
        out_ref[0] = (((g_ref[0, 0] + o_ref[0]) + q_ref[0].astype(F32)) + q_ref[1].astype(F32)) + q_ref[2].astype(F32)

    return pl.pallas_call(
        body, name="sum_for_my_chip",
        grid_spec=pltpu.PrefetchScalarGridSpec(
            num_scalar_prefetch=1, grid=(h // th,),
            in_specs=[pl.BlockSpec((1, 1, th, c), lambda i, idx_ref: (idx_ref[0], idx_ref[1], i, 0)),
                      pl.BlockSpec((1, th, c), lambda i, idx_ref: (idx_ref[0], i, 0)),
                      pl.BlockSpec((3, th, c), lambda i, idx_ref: (0, i, 0))],
            out_specs=pl.BlockSpec((1, th, c), lambda i, idx_ref: (idx_ref[1], i, 0))),
        out_shape=jax.ShapeDtypeStruct((2, h, c), F32),
        compiler_params=_cparams(("parallel",)),
    )(idx, g, other, got)


class _Reducer:
    def __init__(self, idx):
        self.idx = idx
        self.groups = []

    def add(self, grads):
        group = {"state": 0, "g": [g.reshape(4, 2, g.shape[1] // 2, g.shape[2]) for g in grads]}
        self.groups.append(group)
        return group

    def steps(self):
        ops, owners = [], []
        for gr in self.groups:
            if gr["state"] == 0:
                ops.append(_Swap(gr["g"]))
            elif gr["state"] == 1:
                ops.append(_Exchange(gr["parts"]))
            elif gr["state"] == 2:
                ops.append(_Join(gr["bufs"]))
            else:
                continue
            owners.append(gr)
        return ops, owners

    def absorb(self, owners, results):
        for gr, res in zip(owners, results):
            if gr["state"] == 0:
                gr["other"] = res
                gr["parts"] = [_add_my_half(g, o, self.idx) for g, o in zip(gr["g"], res)]
            elif gr["state"] == 1:
                gr["bufs"] = [_sum_for_my_chip(g, o, q, self.idx) for g, o, q in zip(gr["g"], gr["other"], res)]
            else:
                gr["full"] = [b.reshape(2 * b.shape[1], b.shape[2]) for b in res]
            gr["state"] += 1

    def drain(self):
        while any(gr["state"] < 3 for gr in self.groups):
            ops, owners = self.steps()
            self.absorb(owners, _communicate(ops))


class _GatherRows:
    def __init__(self, bufs):
        self.arrays = list(bufs)
        self.outs = [jax.ShapeDtypeStruct(b.shape, b.dtype) for b in bufs]
        self.aliased = True
        self.n_sems = 7 * len(bufs)

    def run(self, phase, ins, outs, send_sems, recv_sems, base):
        x, y, c, chips = _place()
        me, sibling = (x, y, c), (x, y, 1 - c)
        for k, buf in enumerate(outs):
            def copy(i, block_of, to):
                blk = buf.at[4 * block_of[0] + 2 * block_of[1] + block_of[2]]
                return _remote(blk, blk, send_sems.at[base + 7 * k + i], recv_sems.at[base + 7 * k + i], to)

            if phase == 0:
                copy(0, me, sibling).start()
            for r, (px, py) in enumerate(chips):
                if phase == 0:
                    copy(1 + r, me, (px, py, c)).start()
                elif phase == 1:
                    copy(1 + r, (px, py, c), me).wait_recv()
                    copy(4 + r, (px, py, c), sibling).start()
                else:
                    copy(4 + r, (px, py, 1 - c), me).wait_recv()
                    copy(1 + r, me, (px, py, c)).wait_send()
                    copy(4 + r, (px, py, c), sibling).wait_send()
            if phase == 2:
                copy(0, sibling, me).wait_recv()
                copy(0, me, sibling).wait_send()


def _sum_devices(gathered):
    nd, m, n = gathered.shape

    def body(g_ref, o_ref):
        acc = g_ref[0]
        for b in range(1, nd):
            acc = acc + g_ref[b]
        o_ref[...] = acc

    return pl.pallas_call(
        body, name="sum_devices", out_shape=jax.ShapeDtypeStruct((m, n), F32),
        in_specs=[pl.BlockSpec(memory_space=pltpu.VMEM)], out_specs=pl.BlockSpec(memory_space=pltpu.VMEM),
        compiler_params=_cparams(),
    )(gathered)


def _ada_update(c_all, dmod_cols, w, m, v):
    nl, nd, ncol = dmod_cols.shape
    d = c_all.shape[1]
    tr = _row_tile(d, 256)

    def body(c_ref, dm_ref, w_ref, m_ref, v_ref, g_ref, d_ref, nm_ref, nv_ref):
        g = lax.dot_general(c_ref[...], dm_ref[0], (((0,), (0,)), ((), ())),
                            preferred_element_type=F32, precision=lax.Precision.HIGHEST)
        g_ref[0] = g
        d_ref[0], nm_ref[0], nv_ref[0] = _adamw_math(w_ref[0], g, m_ref[0], v_ref[0])

    spec = pl.BlockSpec((1, tr, ncol), lambda l, i: (l, i, 0))
    shp = jax.ShapeDtypeStruct((nl, d, ncol), F32)
    return pl.pallas_call(
        body, name="ada_update", grid=(nl, d // tr),
        in_specs=[pl.BlockSpec((nd, tr), lambda l, i: (0, i)), pl.BlockSpec((1, nd, ncol), lambda l, i: (l, 0, 0)),
                  spec, spec, spec],
        out_specs=(spec,) * 4, out_shape=(shp,) * 4, compiler_params=_cparams(("parallel", "parallel")),
    )(c_all, dmod_cols, w, m, v)


def _pad_rows(a, rows):
    return jnp.pad(a, ((0, rows - a.shape[0]), (0, 0)))


def _shard_cols(full, chip, width):
    return lax.dynamic_slice_in_dim(full, chip * width, width, axis=full.ndim - 1)


def kernel(x, c, ada_w, ada_b, pre_g, post_g, pool_w, pool_scale, cv_w_pw1, cv_b_pw1, cv_w_dw, cv_b_dw, cv_ln_g, cv_ln_b, cv_w_pw2, cv_b_pw2, ffn_w_up, ffn_w_dw, ffn_w_down, loss_target, m_ada_w, m_ada_b, m_pre_g, m_post_g, m_pool_w, m_pool_scale, m_cv_w_pw1, m_cv_b_pw1, m_cv_w_dw, m_cv_b_dw, m_cv_ln_g, m_cv_ln_b, m_cv_w_pw2, m_cv_b_pw2, m_ffn_w_up, m_ffn_w_dw, m_ffn_w_down, v_ada_w, v_ada_b, v_pre_g, v_post_g, v_pool_w, v_pool_scale, v_cv_w_pw1, v_cv_b_pw1, v_cv_w_dw, v_cv_b_dw, v_cv_ln_g, v_cv_ln_b, v_cv_w_pw2, v_cv_b_pw2, v_ffn_w_up, v_ffn_w_dw, v_ffn_w_down):
    s, d = x.shape[1], x.shape[2]
    dq = d // N_CHIPS
    n_g = pool_w.shape[1]
    gq = pool_w.shape[2]
    gd = pool_w.shape[3]
    kw = cv_w_dw.shape[1]
    cs = ffn_w_up.shape[2]
    fq = ffn_w_down.shape[1]
    chip = 2 * lax.axis_index("x") + lax.axis_index("y")
    core = lax.axis_index("c")
    chip1 = jnp.reshape(chip, (1,)).astype(jnp.int32)
    core1 = jnp.reshape(core, (1,)).astype(jnp.int32)
    xs, tgt = x[0], loss_target[0]

    small_rows = [pre_g.reshape(4, dq), post_g.reshape(4, dq), cv_w_dw[0], cv_b_dw, cv_ln_g, cv_ln_b, cv_b_pw2,
                  cv_b_pw1.reshape(2, dq)]
    small = jnp.concatenate(small_rows, axis=0)
    n_small = small.shape[0]
    small = _pad_rows(small, -(-n_small // 16) * 16)
    dwf = _pad_rows(ffn_w_dw.reshape(6, cs), 16)
    first = _AllGather([_cast_into_slot(pool_w.reshape(n_g * gq, gd), chip1), _cast_into_slot(small, chip1, dtype=F32),
                        _cast_into_slot(dwf, chip1, dtype=F32)])
    upper = _AllGather([_cast_into_slot(ffn_w_up[0], chip1)], part=(0, 2))
    (c_rep, mod_rep), ((g_pool, g_small, g_dwf), (g_up0,)) = _ada_forward(c, ada_w, _Comm([first, upper]))
    c_all = c_rep[:, 0, :]
    mod = mod_rep[:, :, 0, :].transpose(1, 0, 2).reshape(ada_b.shape) + ada_b
    lower = _AllGather([g_up0], part=(1, 2))
    second = _AllGather([_cast_into_slot(ffn_w_down[0], chip1)])
    later = _AllGather([_cast_into_slot(cv_w_pw1[0], chip1), _cast_into_slot(cv_w_pw2[0], chip1),
                        _cast_into_slot(ffn_w_up[1], chip1), _cast_into_slot(ffn_w_down[1], chip1)])
    poolw_full = g_pool.reshape(N_CHIPS, n_g, gq, gd).transpose(1, 0, 2, 3).reshape(n_g, gd, gd)
    smallf = g_small.transpose(1, 0, 2).reshape(g_small.shape[1], d)
    pre_full, post_full = smallf[0:4].reshape(2, 2, d), smallf[4:8].reshape(2, 2, d)
    wdw31 = smallf[8:8 + kw]
    o = 8 + kw
    b_dw, ln_g, ln_b, b_pw2 = smallf[o:o + 1], smallf[o + 1:o + 2], smallf[o + 2:o + 3], smallf[o + 3:o + 4]
    b_pw1 = g_small[:, o + 4:o + 6, :].reshape(1, 2 * d)
    ffn_dw = g_dwf[:, :6, :].transpose(1, 0, 2).reshape(2, 3, N_CHIPS * cs)

    def sub_vec(layer, sub, extra=None):
        m6 = mod[layer].reshape(6, d)
        rows = [pre_full[layer, sub][None], 1.0 + m6[3 * sub + 1][None], m6[3 * sub][None], m6[3 * sub + 2][None],
                post_full[layer, sub][None]]
        if extra is not None:
            rows.append(extra)
        return _pad_rows(jnp.concatenate(rows, axis=0), 8)

    vec_pool = sub_vec(0, 0, pool_scale)
    vec_f0, vec_conv, vec_f1 = sub_vec(0, 1), sub_vec(1, 0), sub_vec(1, 1)
    cvec = _pad_rows(jnp.concatenate([b_dw, ln_g, ln_b, b_pw2], axis=0), 8)

    x1, ((g_up0,), (g_dn0,)) = _pool_forward(xs, vec_pool, poolw_full, _Comm([lower, second]))
    w_up0, w_dn0 = g_up0, g_dn0.reshape(2, 2 * fq, d)
    (x2, h_f0, a0_f0, cc_f0, u_f0, y_f0), ((g_pw1, g_pw2, g_up1, g_dn1),) = _ffn_forward(
        x1, vec_f0, w_up0, ffn_dw[0], w_dn0, _Comm([later]))
    pw2_full = g_pw2.reshape(d, d)
    w_up1, w_dn1 = g_up1, g_dn1.reshape(2, 2 * fq, d)
    x3, h_cv, a_cv, uc_cv, z_cv, y_cv = _conv_forward(x2, vec_conv, cvec, g_pw1, b_pw1, wdw31, pw2_full)
    (dx4, h_f1, a0_f1, cc_f1, u_f1, y_f1, loss_rows), _ = _ffn_forward(x3, vec_f1, w_up1, ffn_dw[1], w_dn1, target=tgt)

    dx3, da0_f1, dy_f1, sum_f1, dwdw_f1 = _ffn_backward(dx4, x3, y_f1, a0_f1, cc_f1, vec_f1, w_up1, ffn_dw[1], w_dn1)
    dx2, da_cv, dy_cv, sum_cv, dwdw_cv = _conv_backward(dx3, x2, y_cv, a_cv, uc_cv, vec_conv, cvec, g_pw1, wdw31, pw2_full)
    dx1, da0_f0, dy_f0, sum_f0, dwdw_f0 = _ffn_backward(dx2, x1, y_f0, a0_f0, cc_f0, vec_f0, w_up0, ffn_dw[0], w_dn0)
    dx0, sum_pool, gw_pool = _pool_backward(dx1, xs, vec_pool, poolw_full)
    gw_pool4 = gw_pool.reshape(n_g, N_CHIPS, gq, gd).transpose(1, 0, 2, 3).reshape(N_CHIPS, n_g * gq, gd)

    slab = jnp.concatenate([sum_f1, sum_cv, dwdw_cv, sum_f0, sum_pool, loss_rows], axis=0)
    wide = jnp.concatenate([dwdw_f1, dwdw_f0], axis=0)
    n_slab = slab.shape[0]
    mine = jnp.concatenate([slab, wide.reshape(-1, d)], axis=0)
    rows_of_all = _cast_into_slot(mine, 2 * chip1 + core1, N_DEV, F32)
    red = _Reducer(jnp.concatenate([chip1, core1]))

    def carried(call, *args, extra=()):
        ops, owners = red.steps()
        out, results = call(*args, _Comm(ops + list(extra)))
        red.absorb(owners, results[:len(ops)])
        return out, results[len(ops):]

    gw_up1, ((both_all,),) = carried(_weight_grad, h_f1[None], da0_f1, extra=[_GatherRows([rows_of_all])])
    r_up1 = red.add([gw_up1])
    r_up0 = red.add([carried(_weight_grad, h_f0[None], da0_f0)[0]])
    r_dn1 = red.add([carried(_weight_grad, u_f1, dy_f1[None])[0].reshape(N_CHIPS, fq, d)])
    r_dn0 = red.add([carried(_weight_grad, u_f0, dy_f0[None])[0].reshape(N_CHIPS, fq, d)])
    r_pw1 = red.add([carried(_weight_grad, h_cv[None], da_cv)[0]])
    r_last = red.add([carried(_weight_grad, z_cv[None], dy_cv[None])[0].reshape(N_CHIPS, dq, d), gw_pool4])

    tot_both = _sum_devices(both_all)
    slab_all, tot = both_all[:, :n_slab], tot_both[:n_slab]
    tot_wide = tot_both[n_slab:].reshape(wide.shape)
    kpad = dwdw_cv.shape[0]
    o_cv, o_dw, o_f0 = 8, 24, 24 + kpad
    o_pool, o_loss = o_f0 + 8, o_f0 + 16
    loss = jnp.sum(tot[o_loss])
    dmod_l0 = jnp.concatenate([slab_all[:, o_pool + 4], slab_all[:, o_pool + 3], slab_all[:, o_pool + 1],
                               slab_all[:, o_f0 + 4], slab_all[:, o_f0 + 3], slab_all[:, o_f0 + 1]], axis=-1)
    dmod_l1 = jnp.concatenate([slab_all[:, o_cv + 4], slab_all[:, o_cv + 3], slab_all[:, o_cv + 1],
                               slab_all[:, 4], slab_all[:, 3], slab_all[:, 1]], axis=-1)
    dmod = jnp.stack([dmod_l0, dmod_l1], axis=0)
    g_ada_b = _sum_devices(dmod.transpose(1, 0, 2))
    ada_w_step = _ada_update(c_all, _shard_cols(dmod, chip, ada_w.shape[2]), ada_w, m_ada_w, v_ada_w)

    g_pre = jnp.stack([jnp.stack([tot[o_pool + 2], tot[o_f0 + 2]]), jnp.stack([tot[o_cv + 2], tot[2]])])
    g_post = jnp.stack([jnp.stack([tot[o_pool + 0], tot[o_f0 + 0]]), jnp.stack([tot[o_cv + 0], tot[0]])])
    g_pool_scale = tot[o_pool + 5][None]
    g_b_pw2, g_ln_g, g_ln_b, g_b_dw = tot[o_cv + 5], tot[o_cv + 6], tot[o_cv + 7], tot[o_cv + 8]
    g_b_pw1 = jnp.concatenate([tot[o_cv + 9], tot[o_cv + 10]])
    g_w_dw31 = tot[o_dw:o_dw + kw]
    g_ffn_dw = jnp.stack([tot_wide[8:11], tot_wide[0:3]])

    grads_small = {
        "pre_g": _shard_cols(g_pre, chip, dq), "post_g": _shard_cols(g_post, chip, dq),
        "pool_scale": g_pool_scale, "cv_b_pw1": _shard_cols(g_b_pw1[None], chip, 2 * dq),
        "cv_w_dw": _shard_cols(g_w_dw31[None], chip, dq), "cv_b_dw": _shard_cols(g_b_dw[None], chip, dq),
        "cv_ln_g": _shard_cols(g_ln_g[None], chip, dq), "cv_ln_b": _shard_cols(g_ln_b[None], chip, dq),
        "cv_b_pw2": _shard_cols(g_b_pw2[None], chip, dq), "ffn_w_dw": _shard_cols(g_ffn_dw, chip, cs),
        "ada_b": g_ada_b,
    }
    params_small = {
        "pre_g": (pre_g, m_pre_g, v_pre_g), "post_g": (post_g, m_post_g, v_post_g),
        "pool_scale": (pool_scale, m_pool_scale, v_pool_scale), "cv_b_pw1": (cv_b_pw1, m_cv_b_pw1, v_cv_b_pw1),
        "cv_w_dw": (cv_w_dw, m_cv_w_dw, v_cv_w_dw), "cv_b_dw": (cv_b_dw, m_cv_b_dw, v_cv_b_dw),
        "cv_ln_g": (cv_ln_g, m_cv_ln_g, v_cv_ln_g), "cv_ln_b": (cv_ln_b, m_cv_ln_b, v_cv_ln_b),
        "cv_b_pw2": (cv_b_pw2, m_cv_b_pw2, v_cv_b_pw2), "ffn_w_dw": (ffn_w_dw, m_ffn_w_dw, v_ffn_w_dw),
        "ada_b": (ada_b, m_ada_b, v_ada_b),
    }
    names = list(params_small)
    small_g = {nm: grads_small[nm].reshape(params_small[nm][0].shape) for nm in names}
    updated = _adamw_many([(params_small[nm][0], small_g[nm], params_small[nm][1], params_small[nm][2])
                           for nm in names])
    small_d = {nm: u[0] for nm, u in zip(names, updated)}
    small_m = {nm: u[1] for nm, u in zip(names, updated)}
    small_v = {nm: u[2] for nm, u in zip(names, updated)}

    red.drain()
    big_p = {
        "ada_w": (ada_w, m_ada_w, v_ada_w), "pool_w": (pool_w, m_pool_w, v_pool_w),
        "cv_w_pw1": (cv_w_pw1, m_cv_w_pw1, v_cv_w_pw1), "cv_w_pw2": (cv_w_pw2, m_cv_w_pw2, v_cv_w_pw2),
        "ffn_w_up": (ffn_w_up, m_ffn_w_up, v_ffn_w_up), "ffn_w_down": (ffn_w_down, m_ffn_w_down, v_ffn_w_down),
    }
    big_g, big_d, big_m, big_v = {}, {}, {}, {}

    def update(nm, grad):
        w, m, v = big_p[nm]
        as3 = lambda t: t.reshape((-1,) + w.shape[-2:])
        (dl, nm_, nv_), _ = _adamw(as3(w), as3(grad), as3(m), as3(v))
        big_g[nm] = grad.reshape(w.shape)
        big_d[nm], big_m[nm], big_v[nm] = dl.reshape(w.shape), nm_.reshape(w.shape), nv_.reshape(w.shape)

    full = lambda group, k=0: group["full"][k]
    update("ffn_w_up", jnp.stack([full(r_up0), full(r_up1)]))
    big_g["ada_w"], big_d["ada_w"], big_m["ada_w"], big_v["ada_w"] = ada_w_step
    update("ffn_w_down", jnp.stack([full(r_dn0), full(r_dn1)]))
    update("cv_w_pw1", full(r_pw1))
    update("cv_w_pw2", full(r_last, 0))
    update("pool_w", full(r_last, 1))

    order = ["ada_w", "ada_b", "pre_g", "post_g", "pool_w", "pool_scale", "cv_w_pw1", "cv_b_pw1", "cv_w_dw", "cv_b_dw",
             "cv_ln_g", "cv_ln_b", "cv_w_pw2", "cv_b_pw2", "ffn_w_up", "ffn_w_dw", "ffn_w_down"]
    pick = lambda bigs, smalls: [bigs[nm] if nm in bigs else smalls[nm] for nm in order]
    return (loss, dx0[None], *pick(big_g, small_g), *pick(big_d, small_d), *pick(big_m, small_m),
            *pick(big_v, small_v))
```

```python
import functools

import jax
import jax.numpy as jnp
from jax import lax
from jax.experimental import pallas as pl
from jax.experimental.pallas import tpu as pltpu

F32 = jnp.float32
BF16 = jnp.bfloat16
EPS = 1e-6
N_CHIPS = 4
N_DEV = 8
POOL_WINDOWS = (2, 4, 8, 16)
POOL_HALO = 16
FFN_HALO = 16
MXU_LANES = 256
CONV_ROWS, CONV_LANES = 128, 128
ADAM_LR = 0.001
ADAM_B1 = 0.9
ADAM_B2 = 0.999
ADAM_EPS = 1e-08
ADAM_WD = 0.01
ADAM_STEP = 10
V7X_VMEM_LIMIT = 58 * 1024 * 1024
MESH = pl.DeviceIdType.MESH


def _cparams(sem=None, vmem=V7X_VMEM_LIMIT):
    return pltpu.CompilerParams(dimension_semantics=sem, vmem_limit_bytes=vmem)


def _row_tile(n, want):
    if n <= want:
        return n
    t = want - want % 8
    while n % t:
        t -= 8
    return t


def _lane_chunks(width):
    out, c = [], 0
    while c < width:
        w = min(512, width - c)
        out.append((c, w))
        c += w
    return out


def _dot(a, b):
    return jnp.dot(a, b, preferred_element_type=F32)


def _dot_nt(a, b):
    return lax.dot_general(a, b, (((1,), (1,)), ((), ())), preferred_element_type=F32)


def _store_dot_nt(dst, a_ref, b_ref):
    dst[...] = _dot_nt(a_ref[...], b_ref[...])


def _store_dot_nt2(dst, a1_ref, a2_ref, b1_ref, b2_ref):
    dst[...] = _dot_nt(a1_ref[...], b1_ref[...]) + _dot_nt(a2_ref[...], b2_ref[...])


def _dot_tn(a, b):
    return lax.dot_general(a, b, (((0,), (0,)), ((), ())), preferred_element_type=F32)


def _rms(x):
    r = lax.rsqrt(jnp.mean(x * x, axis=-1, keepdims=True) + EPS)
    return x * r, r


def _rms_bwd(dyn, yn, r):
    return r * (dyn - yn * jnp.mean(dyn * yn, axis=-1, keepdims=True))


def _sigmoid(x):
    return 0.5 * jnp.tanh(0.5 * x) + 0.5


def _colsum(x):
    return jnp.sum(x, axis=0, keepdims=True)


def _shift_down(x, k):
    return x if k == 0 else pltpu.roll(x, k, 0)


def _shift_up(x, k):
    return x if k == 0 else pltpu.roll(x, x.shape[0] - k, 0)


def _vec_rows(vec):
    return vec[0:1] * vec[1:2], vec[2:3], vec[3:4], vec[4:5]


def _norm_sums(do, yn, dh, xn, vec):
    p, q = _colsum(do * yn), _colsum(dh * xn)
    return [p * vec[3:4], p * vec[4:5], q * vec[1:2], q * vec[0:1]]


def _add_rows(sum_ref, rows):
    for k, r in enumerate(rows):
        sum_ref[k:k + 1, :] += r


def _ada_forward(c, ada_w, comm=None):
    n_layers, d, ncol = ada_w.shape
    comm = comm or _Comm([])
    nc = len(comm.arrays)

    def body(*refs):
        c_ref, w_ref = refs[:2]
        cin = refs[2:2 + nc]
        call_ref, mod_ref = refs[2 + nc:4 + nc]
        cout = refs[4 + nc:4 + 2 * nc]
        part_ref, sendbuf, send_sems, recv_sems, send2, recv2 = refs[4 + 2 * nc:10 + 2 * nc]
        carried_sems = refs[10 + 2 * nc:]
        if nc:
            comm.run(0, cin, cout, *carried_sems)
        x, y, cc = lax.axis_index("x"), lax.axis_index("y"), lax.axis_index("c")
        me = 4 * x + 2 * y + cc
        rel = [(x, y, 1 - cc), (1 - x, y, cc), (x, 1 - y, cc), (1 - x, 1 - y, cc),
               (1 - x, y, 1 - cc), (x, 1 - y, 1 - cc), (1 - x, 1 - y, 1 - cc)]
        cv = c_ref[...]
        call_ref[me] = jnp.broadcast_to(cv * _sigmoid(cv), (8, d))

        def gather(k, block, to):
            blk = call_ref.at[block]
            return pltpu.make_async_remote_copy(src_ref=blk, dst_ref=blk, send_sem=send_sems.at[k],
                                                recv_sem=recv_sems.at[k], device_id=to, device_id_type=MESH)

        for k, to in enumerate(rel):
            gather(k, me, to).start()
        for k, (px, py, pc) in enumerate(rel):
            gather(k, 4 * px + 2 * py + pc, rel[k]).wait_recv()
        for k, to in enumerate(rel):
            gather(k, me, to).wait_send()

        ca = call_ref[...].reshape(8 * N_DEV, d)
        for l in range(n_layers):
            part_ref[l] = jnp.dot(ca, w_ref[l], preferred_element_type=F32, precision=lax.Precision.HIGHEST)

        j = 2 * x + y
        chips = [(1 - x, y), (x, 1 - y), (1 - x, 1 - y)]

        def rows_of(b):
            return part_ref[:, pl.ds(pl.multiple_of(8 * b, 8), 8), :]

        def scatter(k, src_j, to):
            return pltpu.make_async_remote_copy(
                src_ref=sendbuf.at[k], dst_ref=mod_ref.at[src_j], send_sem=send2.at[k], recv_sem=recv2.at[k],
                device_id=to, device_id_type=MESH)

        mod_ref[j] = rows_of(me)
        for k, (px, py) in enumerate(chips):
            sendbuf[k] = rows_of(4 * px + 2 * py + cc)
            scatter(k, j, (px, py, cc)).start()
        for k, (px, py) in enumerate(chips):
            scatter(k, 2 * px + py, (px, py, cc)).wait_recv()
        for k, (px, py) in enumerate(chips):
            scatter(k, j, (px, py, cc)).wait_send()
        if nc:
            comm.run(1, cin, cout, *carried_sems)
            comm.run(2, cin, cout, *carried_sems)

    vm = pl.BlockSpec(memory_space=pltpu.VMEM)
    res = pl.pallas_call(
        body, name="ada_forward",
        out_shape=(jax.ShapeDtypeStruct((N_DEV, 8, d), F32), jax.ShapeDtypeStruct((N_CHIPS, n_layers, 8, ncol), F32),
                   *comm.outs),
        in_specs=[vm, vm] + comm.specs(), out_specs=(vm, vm, *comm.specs()),
        input_output_aliases=comm.aliases(2, 2),
        scratch_shapes=[pltpu.VMEM((n_layers, 8 * N_DEV, ncol), F32), pltpu.VMEM((3, n_layers, 8, ncol), F32),
                        pltpu.SemaphoreType.DMA((7,)), pltpu.SemaphoreType.DMA((7,)),
                        pltpu.SemaphoreType.DMA((3,)), pltpu.SemaphoreType.DMA((3,))] + (comm.scratch() if nc else []),
        compiler_params=_cparams(),
    )(c, ada_w, *comm.arrays)
    return res[:2], comm.split(res[2:])


def _cast_into_slot(w2d, slot, n_slots=N_CHIPS, dtype=None):
    r, c = w2d.shape
    tr = _row_tile(r, 256)
    dtype = dtype or BF16

    def body(slot_ref, w_ref, o_ref):
        o_ref[0] = w_ref[...].astype(dtype)

    return pl.pallas_call(
        body, name="cast_into_slot",
        grid_spec=pltpu.PrefetchScalarGridSpec(
            num_scalar_prefetch=1, grid=(r // tr,),
            in_specs=[pl.BlockSpec((tr, c), lambda i, slot_ref: (i, 0))],
            out_specs=pl.BlockSpec((1, tr, c), lambda i, slot_ref: (slot_ref[0], i, 0))),
        out_shape=jax.ShapeDtypeStruct((n_slots, r, c), dtype), compiler_params=_cparams(("parallel",)),
    )(slot, w2d)


def _place():
    x, y, c = lax.axis_index("x"), lax.axis_index("y"), lax.axis_index("c")
    return x, y, c, [(1 - x, y), (x, 1 - y), (1 - x, 1 - y)]


def _remote(src, dst, send_sem, recv_sem, to):
    return pltpu.make_async_remote_copy(src_ref=src, dst_ref=dst, send_sem=send_sem, recv_sem=recv_sem,
                                        device_id=to, device_id_type=MESH)


class _AllGather:
    def __init__(self, bufs, part=(0, 1)):
        self.arrays = list(bufs)
        self.outs = [jax.ShapeDtypeStruct(b.shape, b.dtype) for b in bufs]
        self.aliased = True
        self.n_sems = 6 * len(bufs)
        self.part = part

    def run(self, phase, ins, outs, send_sems, recv_sems, base):
        x, y, c, chips = _place()
        j = 2 * x + y
        for k, buf in enumerate(outs):
            rows = buf.shape[1] // self.part[1]
            half = rows // 2

            def part(src_j, h):
                return buf.at[src_j, pl.ds(self.part[0] * rows + h * half, half), :]

            def ici(r, src_j, to):
                s = base + 6 * k + r
                return _remote(part(src_j, c), part(src_j, c), send_sems.at[s], recv_sems.at[s], to)

            def d2d(r, src_j, h):
                s = base + 6 * k + 3 + r
                return _remote(part(src_j, h), part(src_j, h), send_sems.at[s], recv_sems.at[s], (x, y, 1 - c))

            for r, (px, py) in enumerate(chips):
                if phase == 0:
                    ici(r, j, (px, py, c)).start()
                elif phase == 1:
                    ici(r, 2 * px + py, (px, py, c)).wait_recv()
                    d2d(r, 2 * px + py, c).start()
                else:
                    d2d(r, 2 * px + py, 1 - c).wait_recv()
                    ici(r, j, (px, py, c)).wait_send()
                    d2d(r, 2 * px + py, c).wait_send()


class _Swap:
    def __init__(self, grads):
        self.arrays = list(grads)
        self.outs = [jax.ShapeDtypeStruct((g.shape[0],) + g.shape[2:], g.dtype) for g in grads]
        self.aliased = False
        self.n_sems = len(grads)

    def run(self, phase, ins, outs, send_sems, recv_sems, base):
        x, y, c, _ = _place()
        for k in range(len(ins)):
            cp = _remote(ins[k].at[:, 1 - c], outs[k], send_sems.at[base + k], recv_sems.at[base + k], (x, y, 1 - c))
            if phase == 0:
                cp.start()
            elif phase == 2:
                cp.wait()


class _Exchange:
    def __init__(self, parts):
        self.arrays = list(parts)
        self.outs = [jax.ShapeDtypeStruct((3,) + p.shape[1:], p.dtype) for p in parts]
        self.aliased = False
        self.n_sems = 3 * len(parts)

    def run(self, phase, ins, outs, send_sems, recv_sems, base):
        x, y, c, chips = _place()
        for k in range(len(ins)):
            for r, (px, py) in enumerate(chips):
                s = base + 3 * k + r
                cp = _remote(ins[k].at[2 * px + py], outs[k].at[r], send_sems.at[s], recv_sems.at[s], (px, py, c))
                if phase == 0:
                    cp.start()
                elif phase == 2:
                    cp.wait()


class _Join:
    def __init__(self, bufs):
        self.arrays = list(bufs)
        self.outs = [jax.ShapeDtypeStruct(b.shape, b.dtype) for b in bufs]
        self.aliased = True
        self.n_sems = len(bufs)

    def run(self, phase, ins, outs, send_sems, recv_sems, base):
        x, y, c, _ = _place()
        for k, buf in enumerate(outs):
            mine = _remote(buf.at[c], buf.at[c], send_sems.at[base + k], recv_sems.at[base + k], (x, y, 1 - c))
            if phase == 0:
                mine.start()
            elif phase == 2:
                mine.wait_send()
                _remote(buf.at[1 - c], buf.at[1 - c], send_sems.at[base + k], recv_sems.at[base + k],
                        (x, y, 1 - c)).wait_recv()


class _Comm:
    def __init__(self, ops):
        self.ops = list(ops)
        self.arrays = [a for op in self.ops for a in op.arrays]
        self.outs = [o for op in self.ops for o in op.outs]
        self.n_sems = sum(op.n_sems for op in self.ops)

    def specs(self):
        return [pl.BlockSpec(memory_space=pl.ANY)] * len(self.arrays)

    def aliases(self, first_in, first_out):
        out, k = {}, 0
        for op in self.ops:
            for i in range(len(op.arrays)):
                if op.aliased:
                    out[first_in + k + i] = first_out + k + i
            k += len(op.arrays)
        return out

    def scratch(self):
        return [pltpu.SemaphoreType.DMA((self.n_sems,)), pltpu.SemaphoreType.DMA((self.n_sems,))]

    def run(self, phase, ins, outs, send_sems, recv_sems):
        k = base = 0
        for op in self.ops:
            n = len(op.arrays)
            op.run(phase, ins[k:k + n], outs[k:k + n], send_sems, recv_sems, base)
            k += n
            base += op.n_sems

    def split(self, results):
        out, k = [], 0
        for op in self.ops:
            out.append(list(results[k:k + len(op.arrays)]))
            k += len(op.arrays)
        return out


def _communicate(ops):
    comm = _Comm(ops)
    n = len(comm.arrays)

    def body(*refs):
        ins, outs, (send_sems, recv_sems) = refs[:n], refs[n:2 * n], refs[2 * n:]
        for phase in range(3):
            comm.run(phase, ins, outs, send_sems, recv_sems)

    res = pl.pallas_call(
        body, name="communicate", out_shape=tuple(comm.outs), in_specs=comm.specs(), out_specs=tuple(comm.specs()),
        input_output_aliases=comm.aliases(0, 0), scratch_shapes=comm.scratch(),
    )(*comm.arrays)
    return comm.split(res)


def _pool_core(he, w_ref, scale, first_row, halo, n_rows):
    d = he.shape[1]
    gd = d // len(POOL_WINDOWS)
    t = first_row + lax.broadcasted_iota(jnp.int32, (n_rows, 1), 0)
    pooled, ypre, cnts = [], [], []
    for g, w in enumerate(POOL_WINDOWS):
        hg = he[:, g * gd:(g + 1) * gd]
        s, k = hg, 1
        while k < w:
            s = s + _shift_down(s, k)
            k *= 2
        cnt = jnp.minimum(t + 1, w).astype(F32)
        p = s[halo:] / cnt - hg[halo:]
        pooled.append(p.astype(BF16))
        cnts.append(cnt)
        ypre.append(_dot(pooled[-1], w_ref[g]))
    return pooled, jnp.concatenate(ypre, axis=1), cnts


def _pool_forward(x, vec, pool_w, comm=None):
    s, d = x.shape
    ts = _row_tile(s, 512)
    nb = s // ts
    n_g, gd, _ = pool_w.shape
    comm = comm or _Comm([])
    nc = len(comm.arrays)

    def body(*refs):
        x_ref, vec_ref, w_ref = refs[:3]
        cin = refs[3:3 + nc]
        o_ref = refs[3 + nc]
        cout = refs[4 + nc:4 + 2 * nc]
        carry = refs[4 + 2 * nc]
        sems = refs[5 + 2 * nc:]
        i = pl.program_id(0)

        @pl.when(i == 0)
        def _():
            carry[...] = jnp.zeros_like(carry)
            if nc:
                comm.run(0, cin, cout, *sems)

        if nc:
            @pl.when(i == nb - 1)
            def _():
                comm.run(1, cin, cout, *sems)

        vec = vec_ref[...]
        a, sh, gt, gpost = _vec_rows(vec)
        xb = x_ref[...]
        xn, _ = _rms(xb)
        h = xn * a + sh
        he = jnp.concatenate([carry[...], h], axis=0)
        carry[...] = h[ts - POOL_HALO:]
        _, ypre, _ = _pool_core(he, w_ref, vec[5:6], i * ts, POOL_HALO, ts)
        yn, _ = _rms(ypre * vec[5:6])
        o_ref[...] = xb + gt * (yn * gpost)
        if nc:
            @pl.when(i == nb - 1)
            def _():
                comm.run(2, cin, cout, *sems)

    res = pl.pallas_call(
        body, name="pool_forward", grid=(nb,),
        in_specs=[pl.BlockSpec((ts, d), lambda i: (i, 0)), pl.BlockSpec((8, d), lambda i: (0, 0)),
                  pl.BlockSpec((n_g, gd, gd), lambda i: (0, 0, 0))] + comm.specs(),
        out_specs=(pl.BlockSpec((ts, d), lambda i: (i, 0)), *comm.specs()),
        out_shape=(jax.ShapeDtypeStruct((s, d), F32), *comm.outs),
        input_output_aliases=comm.aliases(3, 1),
        scratch_shapes=[pltpu.VMEM((POOL_HALO, d), F32)] + (comm.scratch() if nc else []),
        compiler_params=_cparams(("arbitrary",)),
    )(x, vec, pool_w, *comm.arrays)
    return res[0], comm.split(res[1:])


def _pool_backward(dout, x, vec, pool_w):
    s, d = x.shape
    ts = _row_tile(s, 512)
    nb = s // ts
    hb = ts // POOL_HALO
    n_g, gd, _ = pool_w.shape

    def body(do_ref, x_ref, xh_ref, vec_ref, w_ref, dx_ref, sum_ref, dw_ref, carry):
        step = pl.program_id(0)
        i = nb - 1 - step

        @pl.when(step == 0)
        def _():
            carry[...] = jnp.zeros_like(carry)
            sum_ref[...] = jnp.zeros_like(sum_ref)
            dw_ref[...] = jnp.zeros_like(dw_ref)

        vec = vec_ref[...]
        a, sh, gt, gpost = _vec_rows(vec)
        scale = vec[5:6]
        do = do_ref[...]
        xe = jnp.concatenate([xh_ref[...], x_ref[...]], axis=0)
        xne, re = _rms(xe)
        he = xne * a + sh
        rowid = lax.broadcasted_iota(jnp.int32, (POOL_HALO + ts, 1), 0)
        he = jnp.where((rowid >= POOL_HALO) | (i > 0), he, 0.0)
        xn, r = xne[POOL_HALO:], re[POOL_HALO:]
        pooled, ypre, cnts = _pool_core(he, w_ref, scale, i * ts, POOL_HALO, ts)
        yn, ry = _rms(ypre * scale)
        dyn = do * (gt * gpost)
        dy = _rms_bwd(dyn, yn, ry)
        dypre = (dy * scale).astype(BF16)
        dh_parts, q_parts = [], []
        for g, w in enumerate(POOL_WINDOWS):
            dyg = dypre[:, g * gd:(g + 1) * gd]
            dpool = _dot_nt(dyg, w_ref[g])
            dw_ref[g] += _dot_tn(pooled[g], dyg)
            q = dpool / cnts[g]
            qe = jnp.concatenate([q, carry[:, g * gd:(g + 1) * gd]], axis=0)
            acc, k = qe, 1
            while k < w:
                acc = acc + _shift_up(acc, k)
                k *= 2
            dh_parts.append(acc[:ts] - dpool)
            q_parts.append(q[:POOL_HALO])
        carry[...] = jnp.concatenate(q_parts, axis=1)
        dh = jnp.concatenate(dh_parts, axis=1)
        dxn = dh * a
        dx_ref[...] = do + _rms_bwd(dxn, xn, r)
        _add_rows(sum_ref, _norm_sums(do, yn, dh, xn, vec) + [_colsum(dh), _colsum(dy * ypre)])

    blk = lambda st: (nb - 1 - st, 0)
    return pl.pallas_call(
        body, name="pool_backward", grid=(nb,),
        in_specs=[pl.BlockSpec((ts, d), blk), pl.BlockSpec((ts, d), blk),
                  pl.BlockSpec((POOL_HALO, d), lambda st: (jnp.maximum((nb - 1 - st) * hb - 1, 0), 0)),
                  pl.BlockSpec((8, d), lambda st: (0, 0)), pl.BlockSpec((n_g, gd, gd), lambda st: (0, 0, 0))],
        out_specs=(pl.BlockSpec((ts, d), blk), pl.BlockSpec((8, d), lambda st: (0, 0)),
                   pl.BlockSpec((n_g, gd, gd), lambda st: (0, 0, 0))),
        out_shape=(jax.ShapeDtypeStruct((s, d), F32), jax.ShapeDtypeStruct((8, d), F32),
                   jax.ShapeDtypeStruct((n_g, gd, gd), F32)),
        scratch_shapes=[pltpu.VMEM((POOL_HALO, d), F32)],
        compiler_params=_cparams(("arbitrary",)),
    )(dout, x, x, vec, pool_w)


def _ffn_forward(x, vec, w_up, w_dw, w_down, comm=None, target=None):
    s, d = x.shape
    _, _, cs = w_up.shape
    ts = _row_tile(s, 256)
    nb = s // ts
    chunks = _lane_chunks(cs)
    comm = comm or _Comm([])
    nc = len(comm.arrays)
    nl = 0 if target is None else 1
    n_in, n_out = 5 + nl, 6 + nl

    def body(*refs):
        x_ref, vec_ref, wup_ref, wdw_ref, wdn_ref = refs[:5]
        cin = refs[n_in:n_in + nc]
        o_ref, h_ref, a0_ref, cc_ref, u_ref, y_ref = refs[n_in + nc:n_in + nc + 6]
        loss_ref = refs[n_in + nc + 6] if nl else None
        cout = refs[n_in + nc + n_out:n_in + 2 * nc + n_out]
        carry = refs[n_in + 2 * nc + n_out]
        sems = refs[n_in + 2 * nc + n_out + 1:]
        i = pl.program_id(0)

        @pl.when(i == 0)
        def _():
            carry[...] = jnp.zeros_like(carry)
            if nl:
                loss_ref[...] = jnp.zeros_like(loss_ref)
            if nc:
                comm.run(0, cin, cout, *sems)

        if nc:
            @pl.when(i == (3 * nb) // 4)
            def _():
                comm.run(1, cin, cout, *sems)

        vec = vec_ref[...]
        a, sh, gt, gpost = _vec_rows(vec)
        xb = x_ref[...]
        xn, _ = _rms(xb)
        hb = (xn * a + sh).astype(BF16)
        h_ref[...] = hb
        for q in range(2):
            for c0, cw in chunks:
                conv = []
                for j in (q, q + 2):
                    a0 = _dot(hb, wup_ref[j, :, c0:c0 + cw])
                    a0_ref[j, :, c0:c0 + cw] = a0.astype(BF16)
                    ae = jnp.concatenate([carry[j, :, c0:c0 + cw], a0], axis=0)
                    carry[j, :, c0:c0 + cw] = a0[ts - FFN_HALO:]
                    w = wdw_ref[:, j * cs + c0:j * cs + c0 + cw]
                    conv.append((w[2:3] * ae + w[1:2] * _shift_down(ae, 1) + w[0:1] * _shift_down(ae, 2))[FFN_HALO:])
                    cc_ref[j, :, c0:c0 + cw] = conv[-1].astype(BF16)
                u_ref[q, :, c0:c0 + cw] = (conv[0] * _sigmoid(conv[0]) * conv[1]).astype(BF16)
        y = _dot(u_ref[0], wdn_ref[0]) + _dot(u_ref[1], wdn_ref[1])
        y_ref[...] = y
        yn, _ = _rms(y)
        x_out = xb + gt * (yn * gpost)
        if nl:
            err = x_out - refs[5][...]
            o_ref[...] = err * (1.0 / d)
            loss_ref[0:1, :] += _colsum(err * err) * (0.5 / d)
        else:
            o_ref[...] = x_out
        if nc:
            @pl.when(i == nb - 1)
            def _():
                comm.run(2, cin, cout, *sems)

    const3 = lambda i: (0, 0, 0)
    res = pl.pallas_call(
        body, name="ffn_forward", grid=(nb,),
        in_specs=[pl.BlockSpec((ts, d), lambda i: (i, 0)), pl.BlockSpec((8, d), lambda i: (0, 0)),
                  pl.BlockSpec(w_up.shape, const3, pipeline_mode=pl.Buffered(1)),
                  pl.BlockSpec(w_dw.shape, lambda i: (0, 0)),
                  pl.BlockSpec(w_down.shape, const3, pipeline_mode=pl.Buffered(1))]
        + [pl.BlockSpec((ts, d), lambda i: (i, 0))] * nl + comm.specs(),
        out_specs=(pl.BlockSpec((ts, d), lambda i: (i, 0)), pl.BlockSpec((ts, d), lambda i: (i, 0)),
                   pl.BlockSpec((4, ts, cs), lambda i: (0, i, 0)), pl.BlockSpec((4, ts, cs), lambda i: (0, i, 0)),
                   pl.BlockSpec((2, ts, cs), lambda i: (0, i, 0)), pl.BlockSpec((ts, d), lambda i: (i, 0)),
                   *[pl.BlockSpec((8, d), lambda i: (0, 0))] * nl, *comm.specs()),
        out_shape=(jax.ShapeDtypeStruct((s, d), F32), jax.ShapeDtypeStruct((s, d), BF16),
                   jax.ShapeDtypeStruct((4, s, cs), BF16), jax.ShapeDtypeStruct((4, s, cs), BF16),
                   jax.ShapeDtypeStruct((2, s, cs), BF16), jax.ShapeDtypeStruct((s, d), F32),
                   *[jax.ShapeDtypeStruct((8, d), F32)] * nl, *comm.outs),
        input_output_aliases=comm.aliases(n_in, n_out),
        scratch_shapes=[pltpu.VMEM((4, FFN_HALO, cs), F32)] + (comm.scratch() if nc else []),
        compiler_params=_cparams(("arbitrary",)),
    )(x, vec, w_up, w_dw, w_down, *([target] * nl), *comm.arrays)
    return res[:n_out], comm.split(res[n_out:])


def _ffn_backward(dout, x, y, a0, cc, vec, w_up, w_dw, w_down):
    s, d = x.shape
    _, _, cs = w_up.shape
    ts = _row_tile(s, 256)
    nb = s // ts
    chunks = _lane_chunks(cs)

    def body(do_ref, x_ref, y_ref, a0_ref, cc_ref, vec_ref, wup_ref, wdw_ref, wdn_ref,
             dx_ref, da0_ref, dy_ref, sum_ref, dwdw_ref, carry, du_s, dh_s):
        step = pl.program_id(0)

        @pl.when(step == 0)
        def _():
            carry[...] = jnp.zeros_like(carry)
            sum_ref[...] = jnp.zeros_like(sum_ref)
            dwdw_ref[...] = jnp.zeros_like(dwdw_ref)

        vec = vec_ref[...]
        a, sh, gt, gpost = _vec_rows(vec)
        do = do_ref[...]
        yn, ry = _rms(y_ref[...])
        dy = _rms_bwd(do * (gt * gpost), yn, ry)
        dyb = dy.astype(BF16)
        dy_ref[...] = dyb
        order = [(q, c0, cw) for q in range(2) for c0, cw in chunks]

        def du_pieces(idx):
            q, c0, cw = order[idx]
            return [functools.partial(_store_dot_nt, du_s.at[idx % 2, :, n0:min(n0 + MXU_LANES, cw)], dy_ref,
                                      wdn_ref.at[q, c0 + n0:c0 + min(n0 + MXU_LANES, cw), :])
                    for n0 in range(0, cw, MXU_LANES)]

        def dh_pieces():
            return [functools.partial(_store_dot_nt2, dh_s.at[:, n0:n0 + MXU_LANES], da0_ref.at[0], da0_ref.at[2],
                                      wup_ref.at[0, n0:n0 + MXU_LANES, :], wup_ref.at[2, n0:n0 + MXU_LANES, :])
                    for n0 in range(0, d, MXU_LANES)]

        for piece in du_pieces(0):
            piece()
        later = dh_pieces()
        for idx, (q, c0, cw) in enumerate(order):
            work = du_pieces(idx + 1) if idx + 1 < len(order) else []
            if q == 1:
                share = -(-len(later) // (len(order) - idx))
                work, later = work + later[:share], later[share:]

            def pump(part, of=3):
                for piece in work[part::of]:
                    piece()

            cg = cc_ref[q, :, c0:c0 + cw].astype(F32)
            cv = cc_ref[q + 2, :, c0:c0 + cw].astype(F32)
            sg = _sigmoid(cg)
            sl = cg * sg
            du = du_s[idx % 2, :, :cw]
            dconv = {q: du * cv * (sg * (1.0 + cg * (1.0 - sg))), q + 2: du * sl}
            pump(0)
            for part, j in enumerate((q, q + 2)):
                dae = jnp.concatenate([dconv[j], carry[j, :, c0:c0 + cw]], axis=0)
                carry[j, :, c0:c0 + cw] = dconv[j][:FFN_HALO]
                up1 = _shift_down(dae, FFN_HALO - 1)[FFN_HALO:]
                up2 = _shift_down(dae, FFN_HALO - 2)[FFN_HALO:]
                lanes = slice(j * cs + c0, j * cs + c0 + cw)
                w = wdw_ref[:, lanes]
                da0_ref[j, :, c0:c0 + cw] = (w[2:3] * dconv[j] + w[1:2] * up1 + w[0:1] * up2).astype(BF16)
                a0 = a0_ref[j, :, c0:c0 + cw].astype(F32)
                dwdw_ref[0:1, lanes] += _colsum(up2 * a0)
                dwdw_ref[1:2, lanes] += _colsum(up1 * a0)
                dwdw_ref[2:3, lanes] += _colsum(dconv[j] * a0)
                pump(part + 1)
        dh = dh_s[...] + _dot_nt(da0_ref[1], wup_ref[1]) + _dot_nt(da0_ref[3], wup_ref[3])
        xn, r = _rms(x_ref[...])
        dx_ref[...] = do + _rms_bwd(dh * a, xn, r)
        _add_rows(sum_ref, _norm_sums(do, yn, dh, xn, vec) + [_colsum(dh)])

    blk = lambda st: (nb - 1 - st, 0)
    blk3 = lambda st: (0, nb - 1 - st, 0)
    const3 = lambda st: (0, 0, 0)
    return pl.pallas_call(
        body, name="ffn_backward", grid=(nb,),
        in_specs=[pl.BlockSpec((ts, d), blk), pl.BlockSpec((ts, d), blk), pl.BlockSpec((ts, d), blk),
                  pl.BlockSpec((4, ts, cs), blk3), pl.BlockSpec((4, ts, cs), blk3),
                  pl.BlockSpec((8, d), lambda st: (0, 0)),
                  pl.BlockSpec(w_up.shape, const3, pipeline_mode=pl.Buffered(1)),
                  pl.BlockSpec(w_dw.shape, lambda st: (0, 0)),
                  pl.BlockSpec(w_down.shape, const3, pipeline_mode=pl.Buffered(1))],
        out_specs=(pl.BlockSpec((ts, d), blk), pl.BlockSpec((4, ts, cs), blk3),
                   pl.BlockSpec((ts, d), blk), pl.BlockSpec((8, d), lambda st: (0, 0)),
                   pl.BlockSpec((8, 4 * cs), lambda st: (0, 0))),
        out_shape=(jax.ShapeDtypeStruct((s, d), F32), jax.ShapeDtypeStruct((4, s, cs), BF16),
                   jax.ShapeDtypeStruct((s, d), BF16),
                   jax.ShapeDtypeStruct((8, d), F32), jax.ShapeDtypeStruct((8, 4 * cs), F32)),
        scratch_shapes=[pltpu.VMEM((4, FFN_HALO, cs), F32), pltpu.VMEM((2, ts, max(cw for _, cw in chunks)), F32),
                        pltpu.VMEM((ts, d), F32)],
        compiler_params=_cparams(("arbitrary",)),
    )(dout, x, y, a0, cc, vec, w_up, w_dw, w_down)


def _conv_halo(width):
    return -(-(width - 1) // 8) * 8


def _conv_forward(x, vec, cvec, w_pw1, b_pw1, w_dw, w_pw2):
    s, d = x.shape
    kw = w_dw.shape[0]
    halo = _conv_halo(kw)
    ts = _row_tile(s, 512)
    hd = d // 2

    def body(x_ref, vec_ref, cvec_ref, w1_ref, b1_ref, wdw_ref, w2_ref,
             o_ref, h_ref, a_ref, uc_ref, z_ref, y_ref, carry):
        i = pl.program_id(0)

        @pl.when(i == 0)
        def _():
            carry[...] = jnp.zeros_like(carry)

        vec, cvec = vec_ref[...], cvec_ref[...]
        a, sh, gt, gpost = _vec_rows(vec)
        xb = x_ref[...]
        xn, _ = _rms(xb)
        hb = (xn * a + sh).astype(BF16)
        h_ref[...] = hb
        for j in range(4):
            a_ref[:, j * hd:(j + 1) * hd] = _dot(hb, w1_ref[j]) + b1_ref[:, j * hd:(j + 1) * hd]
        u = a_ref[:, :d] * _sigmoid(a_ref[:, d:])
        carry[halo:, :] = u
        for r0 in range(0, ts, CONV_ROWS):
            for l0 in range(0, d, CONV_LANES):
                lanes = slice(l0, l0 + CONV_LANES)
                src = carry[r0:r0 + CONV_ROWS + halo, lanes]
                acc = jnp.zeros((CONV_ROWS, CONV_LANES), F32) + cvec[0:1, lanes]
                for k in range(kw):
                    acc = acc + wdw_ref[k:k + 1, lanes] * _shift_down(src, kw - 1 - k)[halo:]
                uc_ref[r0:r0 + CONV_ROWS, lanes] = acc
        carry[:halo, :] = u[ts - halo:]
        uc = uc_ref[...]
        mu = jnp.mean(uc, axis=-1, keepdims=True)
        cen = uc - mu
        rstd = lax.rsqrt(jnp.mean(cen * cen, axis=-1, keepdims=True) + EPS)
        l = cen * rstd * cvec[1:2] + cvec[2:3]
        zb = (l * _sigmoid(l)).astype(BF16)
        z_ref[...] = zb
        y = _dot(zb, w2_ref[...]) + cvec[3:4]
        y_ref[...] = y
        yn, _ = _rms(y)
        o_ref[...] = xb + gt * (yn * gpost)

    row = lambda i: (i, 0)
    const2 = lambda i: (0, 0)
    return pl.pallas_call(
        body, name="conv_forward", grid=(s // ts,),
        in_specs=[pl.BlockSpec((ts, d), row), pl.BlockSpec((8, d), const2), pl.BlockSpec((8, d), const2),
                  pl.BlockSpec(w_pw1.shape, lambda i: (0, 0, 0)), pl.BlockSpec(b_pw1.shape, const2),
                  pl.BlockSpec(w_dw.shape, const2), pl.BlockSpec(w_pw2.shape, const2)],
        out_specs=(pl.BlockSpec((ts, d), row), pl.BlockSpec((ts, d), row), pl.BlockSpec((ts, 2 * d), row),
                   pl.BlockSpec((ts, d), row), pl.BlockSpec((ts, d), row), pl.BlockSpec((ts, d), row)),
        out_shape=(jax.ShapeDtypeStruct((s, d), F32), jax.ShapeDtypeStruct((s, d), BF16),
                   jax.ShapeDtypeStruct((s, 2 * d), F32), jax.ShapeDtypeStruct((s, d), F32),
                   jax.ShapeDtypeStruct((s, d), BF16), jax.ShapeDtypeStruct((s, d), F32)),
        scratch_shapes=[pltpu.VMEM((halo + ts, d), F32)],
        compiler_params=_cparams(("arbitrary",)),
    )(x, vec, cvec, w_pw1, b_pw1, w_dw, w_pw2)


def _conv_backward(dout, x, y, a_pre, uc, vec, cvec, w_pw1, w_dw, w_pw2):
    s, d = x.shape
    kw = w_dw.shape[0]
    kpad = -(-kw // 8) * 8
    halo = _conv_halo(kw)
    ts = _row_tile(s, 512)
    nb = s // ts
    hb = ts // halo
    hd = d // 2

    def body(do_ref, x_ref, y_ref, a_ref, ah_ref, uc_ref, vec_ref, cvec_ref, w1_ref, wdw_ref, w2_ref,
             dx_ref, da_ref, dy_ref, sum_ref, dwdw_ref, carry):
        step = pl.program_id(0)
        i = nb - 1 - step

        @pl.when(step == 0)
        def _():
            carry[...] = jnp.zeros_like(carry)
            sum_ref[...] = jnp.zeros_like(sum_ref)
            dwdw_ref[...] = jnp.zeros_like(dwdw_ref)

        vec, cvec = vec_ref[...], cvec_ref[...]
        a, sh, gt, gpost = _vec_rows(vec)
        do = do_ref[...]
        yn, ry = _rms(y_ref[...])
        dy = _rms_bwd(do * (gt * gpost), yn, ry)
        dyb = dy.astype(BF16)
        dy_ref[...] = dyb
        dz = _dot_nt(dyb, w2_ref[...])
        uc = uc_ref[...]
        mu = jnp.mean(uc, axis=-1, keepdims=True)
        cen = uc - mu
        rstd = lax.rsqrt(jnp.mean(cen * cen, axis=-1, keepdims=True) + EPS)
        lhat = cen * rstd
        l = lhat * cvec[1:2] + cvec[2:3]
        sgl = _sigmoid(l)
        dl = dz * (sgl * (1.0 + l * (1.0 - sgl)))
        dlhat = dl * cvec[1:2]
        duc = rstd * (dlhat - jnp.mean(dlhat, axis=-1, keepdims=True)
                      - lhat * jnp.mean(dlhat * lhat, axis=-1, keepdims=True))
        ae = jnp.concatenate([ah_ref[...] * (i > 0).astype(F32), a_ref[...]], axis=0)
        sgate = _sigmoid(ae[:, d:])
        val = ae[:, :d]
        ue = val * sgate
        rowid = lax.broadcasted_iota(jnp.int32, (halo + ts, 1), 0)
        ue = jnp.where((rowid >= halo) | (i > 0), ue, 0.0)
        duce = jnp.concatenate([duc, carry[...]], axis=0)
        carry[...] = duc[:halo]
        du = jnp.zeros((ts, d), F32)
        for k in range(kw):
            du = du + wdw_ref[k:k + 1, :] * _shift_down(duce, halo - (kw - 1 - k))[halo:]
            dwdw_ref[k:k + 1, :] += _colsum(duc * _shift_down(ue, kw - 1 - k)[halo:])
        sg, vl = sgate[halo:], val[halo:]
        dval = du * sg
        dgate = du * vl * (sg * (1.0 - sg))
        dvb, dgb = dval.astype(BF16), dgate.astype(BF16)
        dh = jnp.zeros((ts, d), F32)
        for j in range(2):
            da_ref[j] = dvb[:, j * hd:(j + 1) * hd]
            da_ref[j + 2] = dgb[:, j * hd:(j + 1) * hd]
            dh = dh + _dot_nt(dvb[:, j * hd:(j + 1) * hd], w1_ref[j]) + _dot_nt(dgb[:, j * hd:(j + 1) * hd], w1_ref[j + 2])
        xn, r = _rms(x_ref[...])
        dx_ref[...] = do + _rms_bwd(dh * a, xn, r)
        _add_rows(sum_ref, _norm_sums(do, yn, dh, xn, vec) + [_colsum(dh), _colsum(dy), _colsum(dl * lhat), _colsum(dl),
                            _colsum(duc), _colsum(dval), _colsum(dgate)])

    blk = lambda st: (nb - 1 - st, 0)
    const2 = lambda st: (0, 0)
    return pl.pallas_call(
        body, name="conv_backward", grid=(nb,),
        in_specs=[pl.BlockSpec((ts, d), blk), pl.BlockSpec((ts, d), blk), pl.BlockSpec((ts, d), blk),
                  pl.BlockSpec((ts, 2 * d), blk),
                  pl.BlockSpec((halo, 2 * d), lambda st: (jnp.maximum((nb - 1 - st) * hb - 1, 0), 0)),
                  pl.BlockSpec((ts, d), blk), pl.BlockSpec((8, d), const2), pl.BlockSpec((8, d), const2),
                  pl.BlockSpec(w_pw1.shape, lambda st: (0, 0, 0)), pl.BlockSpec(w_dw.shape, const2),
                  pl.BlockSpec(w_pw2.shape, const2)],
        out_specs=(pl.BlockSpec((ts, d), blk), pl.BlockSpec((4, ts, hd), lambda st: (0, nb - 1 - st, 0)),
                   pl.BlockSpec((ts, d), blk), pl.BlockSpec((16, d), const2), pl.BlockSpec((kpad, d), const2)),
        out_shape=(jax.ShapeDtypeStruct((s, d), F32), jax.ShapeDtypeStruct((4, s, hd), BF16),
                   jax.ShapeDtypeStruct((s, d), BF16), jax.ShapeDtypeStruct((16, d), F32),
                   jax.ShapeDtypeStruct((kpad, d), F32)),
        scratch_shapes=[pltpu.VMEM((halo, d), F32)],
        compiler_params=_cparams(("arbitrary",)),
    )(dout, x, y, a_pre, a_pre, uc, vec, cvec, w_pw1, w_dw, w_pw2)


def _weight_grad(a, b, comm=None):
    na, s, k = a.shape
    nb_, _, n = b.shape
    nj = max(na, nb_)
    ts = _row_tile(s, 2048)
    nt = s // ts
    comm = comm or _Comm([])
    nc = len(comm.arrays)

    def body(*refs):
        a_ref, b_ref = refs[:2]
        cin = refs[2:2 + nc]
        o_ref = refs[2 + nc]
        cout = refs[3 + nc:3 + 2 * nc]
        sems = refs[3 + 2 * nc:]
        j, t = pl.program_id(0), pl.program_id(1)

        if nc:
            @pl.when((j == 0) & (t == 0))
            def _():
                comm.run(0, cin, cout, *sems)

            @pl.when((j == nj // 2) & (t == nt // 2))
            def _():
                comm.run(1, cin, cout, *sems)

        @pl.when(t == 0)
        def _():
            o_ref[...] = jnp.zeros_like(o_ref)

        o_ref[0] += _dot_tn(a_ref[0], b_ref[0])

        if nc:
            @pl.when((j == nj - 1) & (t == nt - 1))
            def _():
                comm.run(2, cin, cout, *sems)

    res = pl.pallas_call(
        body, name="weight_grad", grid=(nj, nt),
        in_specs=[pl.BlockSpec((1, ts, k), (lambda j, t: (j, t, 0)) if na > 1 else (lambda j, t: (0, t, 0))),
                  pl.BlockSpec((1, ts, n), (lambda j, t: (j, t, 0)) if nb_ > 1 else (lambda j, t: (0, t, 0)))]
        + comm.specs(),
        out_specs=(pl.BlockSpec((1, k, n), lambda j, t: (j, 0, 0)), *comm.specs()),
        out_shape=(jax.ShapeDtypeStruct((nj, k, n), F32), *comm.outs),
        input_output_aliases=comm.aliases(2, 1),
        scratch_shapes=comm.scratch() if nc else [],
        compiler_params=_cparams(("arbitrary", "arbitrary") if nc else ("parallel", "arbitrary")),
    )(a, b, *comm.arrays)
    return res[0], comm.split(res[1:])


def _adamw_math(w, g, m, v):
    nm = ADAM_B1 * m + (1.0 - ADAM_B1) * g
    nv = ADAM_B2 * v + (1.0 - ADAM_B2) * (g * g)
    m_hat = nm * (1.0 / (1.0 - ADAM_B1 ** ADAM_STEP))
    v_hat = nv * (1.0 / (1.0 - ADAM_B2 ** ADAM_STEP))
    return -ADAM_LR * (m_hat / (jnp.sqrt(v_hat) + ADAM_EPS) + ADAM_WD * w), nm, nv


def _adamw_many(params):
    n = len(params)

    def body(*refs):
        for k in range(n):
            w_ref, g_ref, m_ref, v_ref = refs[4 * k:4 * k + 4]
            outs = refs[4 * n + 3 * k:4 * n + 3 * k + 3]
            for o_ref, val in zip(outs, _adamw_math(w_ref[...], g_ref[...], m_ref[...], v_ref[...])):
                o_ref[...] = val

    vm = pl.BlockSpec(memory_space=pltpu.VMEM)
    res = pl.pallas_call(
        body, name="adamw_many", in_specs=[vm] * (4 * n), out_specs=tuple([vm] * (3 * n)),
        out_shape=tuple(jax.ShapeDtypeStruct(p[0].shape, F32) for p in params for _ in range(3)),
        compiler_params=_cparams(),
    )(*[a for p in params for a in p])
    return [res[3 * k:3 * k + 3] for k in range(n)]


def _adamw(w, g, m, v, comm=None):
    nl, r, c = w.shape
    tr = _row_tile(r, 256)
    nr = r // tr
    comm = comm or _Comm([])
    nc = len(comm.arrays)

    def body(*refs):
        w_ref, g_ref, m_ref, v_ref = refs[:4]
        cin = refs[4:4 + nc]
        d_ref, nm_ref, nv_ref = refs[4 + nc:7 + nc]
        cout = refs[7 + nc:7 + 2 * nc]
        sems = refs[7 + 2 * nc:]
        l, i = pl.program_id(0), pl.program_id(1)
        if nc:
            @pl.when((l == 0) & (i == 0))
            def _():
                comm.run(0, cin, cout, *sems)

            @pl.when((l == nl // 2) & (i == nr // 2))
            def _():
                comm.run(1, cin, cout, *sems)

        d_ref[...], nm_ref[...], nv_ref[...] = _adamw_math(w_ref[...], g_ref[...], m_ref[...], v_ref[...])
        if nc:
            @pl.when((l == nl - 1) & (i == nr - 1))
            def _():
                comm.run(2, cin, cout, *sems)

    spec = pl.BlockSpec((1, tr, c), lambda l, i: (l, i, 0))
    shp = jax.ShapeDtypeStruct((nl, r, c), F32)
    res = pl.pallas_call(
        body, name="adamw", grid=(nl, nr), in_specs=[spec] * 4 + comm.specs(),
        out_specs=(spec,) * 3 + tuple(comm.specs()), out_shape=(shp,) * 3 + tuple(comm.outs),
        input_output_aliases=comm.aliases(4, 3), scratch_shapes=comm.scratch() if nc else [],
        compiler_params=_cparams(("arbitrary", "arbitrary") if nc else ("parallel", "parallel")),
    )(w, g, m, v, *comm.arrays)
    return res[:3], comm.split(res[3:])


def _add_my_half(g, other, idx):
    _, _, h, c = g.shape
    th = _row_tile(h, 256)

    def body(idx_ref, g_ref, o_ref, out_ref):
        out_ref[...] = (g_ref[:, 0] + o_ref[...]).astype(BF16)

    return pl.pallas_call(
        body, name="add_my_half",
        grid_spec=pltpu.PrefetchScalarGridSpec(
            num_scalar_prefetch=1, grid=(3, h // th),
            in_specs=[pl.BlockSpec((1, 1, th, c), lambda j, i, idx_ref: (idx_ref[2 + j], idx_ref[1], i, 0)),
                      pl.BlockSpec((1, th, c), lambda j, i, idx_ref: (idx_ref[2 + j], i, 0))],
            out_specs=pl.BlockSpec((1, th, c), lambda j, i, idx_ref: (idx_ref[2 + j], i, 0))),
        out_shape=jax.ShapeDtypeStruct(other.shape, BF16),
        compiler_params=_cparams(("parallel", "parallel")),
    )(idx, g, other)


def _sum_for_my_chip(g, other, got, idx):
    _, _, h, c = g.shape
    th = _row_tile(h, 256)

    def body(idx_ref, g_ref, o_ref, q_ref, out_ref):
        out_ref[0] = (((g_ref[0, 0] + o_ref[0]) + q_ref[0].astype(F32)) + q_ref[1].astype(F32)) + q_ref[2].astype(F32)

    return pl.pallas_call(
        body, name="sum_for_my_chip",
        grid_spec=pltpu.PrefetchScalarGridSpec(
            num_scalar_prefetch=1, grid=(h // th,),
            in_specs=[pl.BlockSpec((1, 1, th, c), lambda i, idx_ref: (idx_ref[0], idx_ref[1], i, 0)),
                      pl.BlockSpec((1, th, c), lambda i, idx_ref: (idx_ref[0], i, 0)),
                      pl.BlockSpec((3, th, c), lambda i, idx_ref: (0, i, 0))],
            out_specs=pl.BlockSpec((1, th, c), lambda i, idx_ref: (idx_ref[1], i, 0))),
        out_shape=jax.ShapeDtypeStruct((2, h, c), F32),
        compiler_params=_cparams(("parallel",)),
    )(idx, g, other, got)


class _Reducer:
    def __init__(self, idx):
        self.idx = idx
        self.groups = []

    def add(self, grads):
        group = {"state": 0, "g": [g.reshape(4, 2, g.shape[1] // 2, g.shape[2]) for g in grads]}
        self.groups.append(group)
        return group

    def steps(self):
        ops, owners = [], []
        for gr in self.groups:
            if gr["state"] == 0:
                ops.append(_Swap(gr["g"]))
            elif gr["state"] == 1:
                ops.append(_Exchange(gr["parts"]))
            elif gr["state"] == 2:
                ops.append(_Join(gr["bufs"]))
            else:
                continue
            owners.append(gr)
        return ops, owners

    def absorb(self, owners, results):
        for gr, res in zip(owners, results):
            if gr["state"] == 0:
                gr["other"] = res
                gr["parts"] = [_add_my_half(g, o, self.idx) for g, o in zip(gr["g"], res)]
            elif gr["state"] == 1:
                gr["bufs"] = [_sum_for_my_chip(g, o, q, self.idx) for g, o, q in zip(gr["g"], gr["other"], res)]
            else:
                gr["full"] = [b.reshape(2 * b.shape[1], b.shape[2]) for b in res]
            gr["state"] += 1

    def drain(self):
        while any(gr["state"] < 3 for gr in self.groups):
            ops, owners = self.steps()
            self.absorb(owners, _communicate(ops))


class _GatherRows:
    def __init__(self, bufs):
        self.arrays = list(bufs)
        self.outs = [jax.ShapeDtypeStruct(b.shape, b.dtype) for b in bufs]
        self.aliased = True
        self.n_sems = 7 * len(bufs)

    def run(self, phase, ins, outs, send_sems, recv_sems, base):
        x, y, c, chips = _place()
        me, sibling = (x, y, c), (x, y, 1 - c)
        for k, buf in enumerate(outs):
            def copy(i, block_of, to):
                blk = buf.at[4 * block_of[0] + 2 * block_of[1] + block_of[2]]
                return _remote(blk, blk, send_sems.at[base + 7 * k + i], recv_sems.at[base + 7 * k + i], to)

            if phase == 0:
                copy(0, me, sibling).start()
            for r, (px, py) in enumerate(chips):
                if phase == 0:
                    copy(1 + r, me, (px, py, c)).start()
                elif phase == 1:
                    copy(1 + r, (px, py, c), me).wait_recv()
                    copy(4 + r, (px, py, c), sibling).start()
                else:
                    copy(4 + r, (px, py, 1 - c), me).wait_recv()
                    copy(1 + r, me, (px, py, c)).wait_send()
                    copy(4 + r, (px, py, c), sibling).wait_send()
            if phase == 2:
                copy(0, sibling, me).wait_recv()
                copy(0, me, sibling).wait_send()


def _sum_devices(gathered):
    nd, m, n = gathered.shape

    def body(g_ref, o_ref):
        acc = g_ref[0]
        for b in range(1, nd):
            acc = acc + g_ref[b]
        o_ref[...] = acc

    return pl.pallas_call(
        body, name="sum_devices", out_shape=jax.ShapeDtypeStruct((m, n), F32),
        in_specs=[pl.BlockSpec(memory_space=pltpu.VMEM)], out_specs=pl.BlockSpec(memory_space=pltpu.VMEM),
        compiler_params=_cparams(),
    )(gathered)


def _ada_update(c_all, dmod_cols, w, m, v):
    nl, nd, ncol = dmod_cols.shape
    d = c_all.shape[1]
    tr = _row_tile(d, 256)

    def body(c_ref, dm_ref, w_ref, m_ref, v_ref, g_ref, d_ref, nm_ref, nv_ref):
        g = lax.dot_general(c_ref[...], dm_ref[0], (((0,), (0,)), ((), ())),
                            preferred_element_type=F32, precision=lax.Precision.HIGHEST)
        g_ref[0] = g
        d_ref[0], nm_ref[0], nv_ref[0] = _adamw_math(w_ref[0], g, m_ref[0], v_ref[0])

    spec = pl.BlockSpec((1, tr, ncol), lambda l, i: (l, i, 0))
    shp = jax.ShapeDtypeStruct((nl, d, ncol), F32)
    return pl.pallas_call(
        body, name="ada_update", grid=(nl, d // tr),
        in_specs=[pl.BlockSpec((nd, tr), lambda l, i: (0, i)), pl.BlockSpec((1, nd, ncol), lambda l, i: (l, 0, 0)),
                  spec, spec, spec],
        out_specs=(spec,) * 4, out_shape=(shp,) * 4, compiler_params=_cparams(("parallel", "parallel")),
    )(c_all, dmod_cols, w, m, v)


def _pad_rows(a, rows):
    return jnp.pad(a, ((0, rows - a.shape[0]), (0, 0)))


def _shard_cols(full, chip, width):
    return lax.dynamic_slice_in_dim(full, chip * width, width, axis=full.ndim - 1)


def kernel(x, c, ada_w, ada_b, pre_g, post_g, pool_w, pool_scale, cv_w_pw1, cv_b_pw1, cv_w_dw, cv_b_dw, cv_ln_g, cv_ln_b, cv_w_pw2, cv_b_pw2, ffn_w_up, ffn_w_dw, ffn_w_down, loss_target, m_ada_w, m_ada_b, m_pre_g, m_post_g, m_pool_w, m_pool_scale, m_cv_w_pw1, m_cv_b_pw1, m_cv_w_dw, m_cv_b_dw, m_cv_ln_g, m_cv_ln_b, m_cv_w_pw2, m_cv_b_pw2, m_ffn_w_up, m_ffn_w_dw, m_ffn_w_down, v_ada_w, v_ada_b, v_pre_g, v_post_g, v_pool_w, v_pool_scale, v_cv_w_pw1, v_cv_b_pw1, v_cv_w_dw, v_cv_b_dw, v_cv_ln_g, v_cv_ln_b, v_cv_w_pw2, v_cv_b_pw2, v_ffn_w_up, v_ffn_w_dw, v_ffn_w_down):
    s, d = x.shape[1], x.shape[2]
    dq = d // N_CHIPS
    n_g = pool_w.shape[1]
    gq = pool_w.shape[2]
    gd = pool_w.shape[3]
    kw = cv_w_dw.shape[1]
    cs = ffn_w_up.shape[2]
    fq = ffn_w_down.shape[1]
    chip = 2 * lax.axis_index("x") + lax.axis_index("y")
    core = lax.axis_index("c")
    chip1 = jnp.reshape(chip, (1,)).astype(jnp.int32)
    core1 = jnp.reshape(core, (1,)).astype(jnp.int32)
    xs, tgt = x[0], loss_target[0]

    small_rows = [pre_g.reshape(4, dq), post_g.reshape(4, dq), cv_w_dw[0], cv_b_dw, cv_ln_g, cv_ln_b, cv_b_pw2,
                  cv_b_pw1.reshape(2, dq)]
    small = jnp.concatenate(small_rows, axis=0)
    n_small = small.shape[0]
    small = _pad_rows(small, -(-n_small // 16) * 16)
    dwf = _pad_rows(ffn_w_dw.reshape(6, cs), 16)
    first = _AllGather([_cast_into_slot(pool_w.reshape(n_g * gq, gd), chip1), _cast_into_slot(small, chip1, dtype=F32),
                        _cast_into_slot(dwf, chip1, dtype=F32)])
    upper = _AllGather([_cast_into_slot(ffn_w_up[0], chip1)], part=(0, 2))
    (c_rep, mod_rep), ((g_pool, g_small, g_dwf), (g_up0,)) = _ada_forward(c, ada_w, _Comm([first, upper]))
    c_all = c_rep[:, 0, :]
    mod = mod_rep[:, :, 0, :].transpose(1, 0, 2).reshape(ada_b.shape) + ada_b
    lower = _AllGather([g_up0], part=(1, 2))
    second = _AllGather([_cast_into_slot(ffn_w_down[0], chip1)])
    later = _AllGather([_cast_into_slot(cv_w_pw1[0], chip1), _cast_into_slot(cv_w_pw2[0], chip1),
                        _cast_into_slot(ffn_w_up[1], chip1), _cast_into_slot(ffn_w_down[1], chip1)])
    poolw_full = g_pool.reshape(N_CHIPS, n_g, gq, gd).transpose(1, 0, 2, 3).reshape(n_g, gd, gd)
    smallf = g_small.transpose(1, 0, 2).reshape(g_small.shape[1], d)
    pre_full, post_full = smallf[0:4].reshape(2, 2, d), smallf[4:8].reshape(2, 2, d)
    wdw31 = smallf[8:8 + kw]
    o = 8 + kw
    b_dw, ln_g, ln_b, b_pw2 = smallf[o:o + 1], smallf[o + 1:o + 2], smallf[o + 2:o + 3], smallf[o + 3:o + 4]
    b_pw1 = g_small[:, o + 4:o + 6, :].reshape(1, 2 * d)
    ffn_dw = g_dwf[:, :6, :].transpose(1, 0, 2).reshape(2, 3, N_CHIPS * cs)

    def sub_vec(layer, sub, extra=None):
        m6 = mod[layer].reshape(6, d)
        rows = [pre_full[layer, sub][None], 1.0 + m6[3 * sub + 1][None], m6[3 * sub][None], m6[3 * sub + 2][None],
                post_full[layer, sub][None]]
        if extra is not None:
            rows.append(extra)
        return _pad_rows(jnp.concatenate(rows, axis=0), 8)

    vec_pool = sub_vec(0, 0, pool_scale)
    vec_f0, vec_conv, vec_f1 = sub_vec(0, 1), sub_vec(1, 0), sub_vec(1, 1)
    cvec = _pad_rows(jnp.concatenate([b_dw, ln_g, ln_b, b_pw2], axis=0), 8)

    x1, ((g_up0,), (g_dn0,)) = _pool_forward(xs, vec_pool, poolw_full, _Comm([lower, second]))
    w_up0, w_dn0 = g_up0, g_dn0.reshape(2, 2 * fq, d)
    (x2, h_f0, a0_f0, cc_f0, u_f0, y_f0), ((g_pw1, g_pw2, g_up1, g_dn1),) = _ffn_forward(
        x1, vec_f0, w_up0, ffn_dw[0], w_dn0, _Comm([later]))
    pw2_full = g_pw2.reshape(d, d)
    w_up1, w_dn1 = g_up1, g_dn1.reshape(2, 2 * fq, d)
    x3, h_cv, a_cv, uc_cv, z_cv, y_cv = _conv_forward(x2, vec_conv, cvec, g_pw1, b_pw1, wdw31, pw2_full)
    (dx4, h_f1, a0_f1, cc_f1, u_f1, y_f1, loss_rows), _ = _ffn_forward(x3, vec_f1, w_up1, ffn_dw[1], w_dn1, target=tgt)

    dx3, da0_f1, dy_f1, sum_f1, dwdw_f1 = _ffn_backward(dx4, x3, y_f1, a0_f1, cc_f1, vec_f1, w_up1, ffn_dw[1], w_dn1)
    dx2, da_cv, dy_cv, sum_cv, dwdw_cv = _conv_backward(dx3, x2, y_cv, a_cv, uc_cv, vec_conv, cvec, g_pw1, wdw31, pw2_full)
    dx1, da0_f0, dy_f0, sum_f0, dwdw_f0 = _ffn_backward(dx2, x1, y_f0, a0_f0, cc_f0, vec_f0, w_up0, ffn_dw[0], w_dn0)
    dx0, sum_pool, gw_pool = _pool_backward(dx1, xs, vec_pool, poolw_full)
    gw_pool4 = gw_pool.reshape(n_g, N_CHIPS, gq, gd).transpose(1, 0, 2, 3).reshape(N_CHIPS, n_g * gq, gd)

    slab = jnp.concatenate([sum_f1, sum_cv, dwdw_cv, sum_f0, sum_pool, loss_rows], axis=0)
    wide = jnp.concatenate([dwdw_f1, dwdw_f0], axis=0)
    n_slab = slab.shape[0]
    mine = jnp.concatenate([slab, wide.reshape(-1, d)], axis=0)
    rows_of_all = _cast_into_slot(mine, 2 * chip1 + core1, N_DEV, F32)
    red = _Reducer(jnp.concatenate([chip1, core1, chip1 ^ 1, chip1 ^ 2, chip1 ^ 3]))

    def carried(call, *args, extra=()):
        ops, owners = red.steps()
        out, results = call(*args, _Comm(ops + list(extra)))
        red.absorb(owners, results[:len(ops)])
        return out, results[len(ops):]

    gw_up1, ((both_all,),) = carried(_weight_grad, h_f1[None], da0_f1, extra=[_GatherRows([rows_of_all])])
    r_up1 = red.add([gw_up1])
    r_up0 = red.add([carried(_weight_grad, h_f0[None], da0_f0)[0]])
    r_dn1 = red.add([carried(_weight_grad, u_f1, dy_f1[None])[0].reshape(N_CHIPS, fq, d)])
    r_dn0 = red.add([carried(_weight_grad, u_f0, dy_f0[None])[0].reshape(N_CHIPS, fq, d)])
    r_pw1 = red.add([carried(_weight_grad, h_cv[None], da_cv)[0]])
    r_last = red.add([carried(_weight_grad, z_cv[None], dy_cv[None])[0].reshape(N_CHIPS, dq, d), gw_pool4])

    tot_both = _sum_devices(both_all)
    slab_all, tot = both_all[:, :n_slab], tot_both[:n_slab]
    tot_wide = tot_both[n_slab:].reshape(wide.shape)
    kpad = dwdw_cv.shape[0]
    o_cv, o_dw, o_f0 = 8, 24, 24 + kpad
    o_pool, o_loss = o_f0 + 8, o_f0 + 16
    loss = jnp.sum(tot[o_loss])
    dmod_l0 = jnp.concatenate([slab_all[:, o_pool + 4], slab_all[:, o_pool + 3], slab_all[:, o_pool + 1],
                               slab_all[:, o_f0 + 4], slab_all[:, o_f0 + 3], slab_all[:, o_f0 + 1]], axis=-1)
    dmod_l1 = jnp.concatenate([slab_all[:, o_cv + 4], slab_all[:, o_cv + 3], slab_all[:, o_cv + 1],
                               slab_all[:, 4], slab_all[:, 3], slab_all[:, 1]], axis=-1)
    dmod = jnp.stack([dmod_l0, dmod_l1], axis=0)
    g_ada_b = _sum_devices(dmod.transpose(1, 0, 2))
    ada_w_step = _ada_update(c_all, _shard_cols(dmod, chip, ada_w.shape[2]), ada_w, m_ada_w, v_ada_w)

    g_pre = jnp.stack([jnp.stack([tot[o_pool + 2], tot[o_f0 + 2]]), jnp.stack([tot[o_cv + 2], tot[2]])])
    g_post = jnp.stack([jnp.stack([tot[o_pool + 0], tot[o_f0 + 0]]), jnp.stack([tot[o_cv + 0], tot[0]])])
    g_pool_scale = tot[o_pool + 5][None]
    g_b_pw2, g_ln_g, g_ln_b, g_b_dw = tot[o_cv + 5], tot[o_cv + 6], tot[o_cv + 7], tot[o_cv + 8]
    g_b_pw1 = jnp.concatenate([tot[o_cv + 9], tot[o_cv + 10]])
    g_w_dw31 = tot[o_dw:o_dw + kw]
    g_ffn_dw = jnp.stack([tot_wide[8:11], tot_wide[0:3]])

    grads_small = {
        "pre_g": _shard_cols(g_pre, chip, dq), "post_g": _shard_cols(g_post, chip, dq),
        "pool_scale": g_pool_scale, "cv_b_pw1": _shard_cols(g_b_pw1[None], chip, 2 * dq),
        "cv_w_dw": _shard_cols(g_w_dw31[None], chip, dq), "cv_b_dw": _shard_cols(g_b_dw[None], chip, dq),
        "cv_ln_g": _shard_cols(g_ln_g[None], chip, dq), "cv_ln_b": _shard_cols(g_ln_b[None], chip, dq),
        "cv_b_pw2": _shard_cols(g_b_pw2[None], chip, dq), "ffn_w_dw": _shard_cols(g_ffn_dw, chip, cs),
        "ada_b": g_ada_b,
    }
    params_small = {
        "pre_g": (pre_g, m_pre_g, v_pre_g), "post_g": (post_g, m_post_g, v_post_g),
        "pool_scale": (pool_scale, m_pool_scale, v_pool_scale), "cv_b_pw1": (cv_b_pw1, m_cv_b_pw1, v_cv_b_pw1),
        "cv_w_dw": (cv_w_dw, m_cv_w_dw, v_cv_w_dw), "cv_b_dw": (cv_b_dw, m_cv_b_dw, v_cv_b_dw),
        "cv_ln_g": (cv_ln_g, m_cv_ln_g, v_cv_ln_g), "cv_ln_b": (cv_ln_b, m_cv_ln_b, v_cv_ln_b),
        "cv_b_pw2": (cv_b_pw2, m_cv_b_pw2, v_cv_b_pw2), "ffn_w_dw": (ffn_w_dw, m_ffn_w_dw, v_ffn_w_dw),
        "ada_b": (ada_b, m_ada_b, v_ada_b),
    }
    names = list(params_small)
    small_g = {nm: grads_small[nm].reshape(params_small[nm][0].shape) for nm in names}
    updated = _adamw_many([(params_small[nm][0], small_g[nm], params_small[nm][1], params_small[nm][2])
                           for nm in names])
    small_d = {nm: u[0] for nm, u in zip(names, updated)}
    small_m = {nm: u[1] for nm, u in zip(names, updated)}
    small_v = {nm: u[2] for nm, u in zip(names, updated)}

    red.drain()
    big_p = {
        "ada_w": (ada_w, m_ada_w, v_ada_w), "pool_w": (pool_w, m_pool_w, v_pool_w),
        "cv_w_pw1": (cv_w_pw1, m_cv_w_pw1, v_cv_w_pw1), "cv_w_pw2": (cv_w_pw2, m_cv_w_pw2, v_cv_w_pw2),
        "ffn_w_up": (ffn_w_up, m_ffn_w_up, v_ffn_w_up), "ffn_w_down": (ffn_w_down, m_ffn_w_down, v_ffn_w_down),
    }
    big_g, big_d, big_m, big_v = {}, {}, {}, {}

    def update(nm, grad):
        w, m, v = big_p[nm]
        as3 = lambda t: t.reshape((-1,) + w.shape[-2:])
        (dl, nm_, nv_), _ = _adamw(as3(w), as3(grad), as3(m), as3(v))
        big_g[nm] = grad.reshape(w.shape)
        big_d[nm], big_m[nm], big_v[nm] = dl.reshape(w.shape), nm_.reshape(w.shape), nv_.reshape(w.shape)

    full = lambda group, k=0: group["full"][k]
    update("ffn_w_up", jnp.stack([full(r_up0), full(r_up1)]))
    big_g["ada_w"], big_d["ada_w"], big_m["ada_w"], big_v["ada_w"] = ada_w_step
    update("ffn_w_down", jnp.stack([full(r_dn0), full(r_dn1)]))
    update("cv_w_pw1", full(r_pw1))
    update("cv_w_pw2", full(r_last, 0))
    update("pool_w", full(r_last, 1))

    order = ["ada_w", "ada_b", "pre_g", "post_g", "pool_w", "pool_scale", "cv_w_pw1", "cv_b_pw1", "cv_w_dw", "cv_b_dw",
             "cv_ln_g", "cv_ln_b", "cv_w_pw2", "cv_b_pw2", "ffn_w_up", "ffn_w_dw", "ffn_w_down"]
    pick = lambda bigs, smalls: [bigs[nm] if nm in bigs else smalls[nm] for nm in order]
    return (loss, dx0[None], *pick(big_g, small_g), *pick(big_d, small_d), *pick(big_m, small_m),
            *pick(big_v, small_v))
```

```python
import functools

import jax
import jax.numpy as jnp
from jax import lax
from jax.experimental import pallas as pl
from jax.experimental.pallas import tpu as pltpu

F32 = jnp.float32
BF16 = jnp.bfloat16
EPS = 1e-6
N_CHIPS = 4
N_DEV = 8
POOL_WINDOWS = (2, 4, 8, 16)
POOL_HALO = 16
FFN_HALO = 16
MXU_LANES = 256
CONV_ROWS, CONV_LANES = 128, 128
ADAM_LR = 0.001
ADAM_B1 = 0.9
ADAM_B2 = 0.999
ADAM_EPS = 1e-08
ADAM_WD = 0.01
ADAM_STEP = 10
V7X_VMEM_LIMIT = 58 * 1024 * 1024
MESH = pl.DeviceIdType.MESH


def _cparams(sem=None, vmem=V7X_VMEM_LIMIT):
    return pltpu.CompilerParams(dimension_semantics=sem, vmem_limit_bytes=vmem)


def _row_tile(n, want):
    if n <= want:
        return n
    t = want - want % 8
    while n % t:
        t -= 8
    return t


def _lane_chunks(width):
    out, c = [], 0
    while c < width:
        w = min(512, width - c)
        out.append((c, w))
        c += w
    return out


def _dot(a, b):
    return jnp.dot(a, b, preferred_element_type=F32)


def _dot_nt(a, b):
    return lax.dot_general(a, b, (((1,), (1,)), ((), ())), preferred_element_type=F32)


def _store_dot_nt(dst, a_ref, b_ref):
    dst[...] = _dot_nt(a_ref[...], b_ref[...])


def _store_dot_nt2(dst, a1_ref, a2_ref, b1_ref, b2_ref):
    dst[...] = _dot_nt(a1_ref[...], b1_ref[...]) + _dot_nt(a2_ref[...], b2_ref[...])


def _dot_tn(a, b):
    return lax.dot_general(a, b, (((0,), (0,)), ((), ())), preferred_element_type=F32)


def _rms(x):
    r = lax.rsqrt(jnp.mean(x * x, axis=-1, keepdims=True) + EPS)
    return x * r, r


def _rms_bwd(dyn, yn, r):
    return r * (dyn - yn * jnp.mean(dyn * yn, axis=-1, keepdims=True))


def _sigmoid(x):
    return 0.5 * jnp.tanh(0.5 * x) + 0.5


def _colsum(x):
    return jnp.sum(x, axis=0, keepdims=True)


def _shift_down(x, k):
    return x if k == 0 else pltpu.roll(x, k, 0)


def _shift_up(x, k):
    return x if k == 0 else pltpu.roll(x, x.shape[0] - k, 0)


def _vec_rows(vec):
    return vec[0:1] * vec[1:2], vec[2:3], vec[3:4], vec[4:5]


def _norm_sums(do, yn, dh, xn, vec):
    p, q = _colsum(do * yn), _colsum(dh * xn)
    return [p * vec[3:4], p * vec[4:5], q * vec[1:2], q * vec[0:1]]


def _add_rows(sum_ref, rows):
    for k, r in enumerate(rows):
        sum_ref[k:k + 1, :] += r


def _ada_forward(c, ada_w, comm=None):
    n_layers, d, ncol = ada_w.shape
    comm = comm or _Comm([])
    nc = len(comm.arrays)

    def body(*refs):
        c_ref, w_ref = refs[:2]
        cin = refs[2:2 + nc]
        call_ref, mod_ref = refs[2 + nc:4 + nc]
        cout = refs[4 + nc:4 + 2 * nc]
        part_ref, sendbuf, send_sems, recv_sems, send2, recv2 = refs[4 + 2 * nc:10 + 2 * nc]
        carried_sems = refs[10 + 2 * nc:]
        if nc:
            comm.run(0, cin, cout, *carried_sems)
        x, y, cc = lax.axis_index("x"), lax.axis_index("y"), lax.axis_index("c")
        me = 4 * x + 2 * y + cc
        rel = [(x, y, 1 - cc), (1 - x, y, cc), (x, 1 - y, cc), (1 - x, 1 - y, cc),
               (1 - x, y, 1 - cc), (x, 1 - y, 1 - cc), (1 - x, 1 - y, 1 - cc)]
        cv = c_ref[...]
        call_ref[me] = jnp.broadcast_to(cv * _sigmoid(cv), (8, d))

        def gather(k, block, to):
            blk = call_ref.at[block]
            return pltpu.make_async_remote_copy(src_ref=blk, dst_ref=blk, send_sem=send_sems.at[k],
                                                recv_sem=recv_sems.at[k], device_id=to, device_id_type=MESH)

        for k, to in enumerate(rel):
            gather(k, me, to).start()
        for k, (px, py, pc) in enumerate(rel):
            gather(k, 4 * px + 2 * py + pc, rel[k]).wait_recv()
        for k, to in enumerate(rel):
            gather(k, me, to).wait_send()

        ca = call_ref[...].reshape(8 * N_DEV, d)
        for l in range(n_layers):
            part_ref[l] = jnp.dot(ca, w_ref[l], preferred_element_type=F32, precision=lax.Precision.HIGHEST)

        j = 2 * x + y
        chips = [(1 - x, y), (x, 1 - y), (1 - x, 1 - y)]

        def rows_of(b):
            return part_ref[:, pl.ds(pl.multiple_of(8 * b, 8), 8), :]

        def scatter(k, src_j, to):
            return pltpu.make_async_remote_copy(
                src_ref=sendbuf.at[k], dst_ref=mod_ref.at[src_j], send_sem=send2.at[k], recv_sem=recv2.at[k],
                device_id=to, device_id_type=MESH)

        if nc:
            comm.run(1, cin, cout, *carried_sems)
        mod_ref[j] = rows_of(me)
        for k, (px, py) in enumerate(chips):
            sendbuf[k] = rows_of(4 * px + 2 * py + cc)
            scatter(k, j, (px, py, cc)).start()
        for k, (px, py) in enumerate(chips):
            scatter(k, 2 * px + py, (px, py, cc)).wait_recv()
        for k, (px, py) in enumerate(chips):
            scatter(k, j, (px, py, cc)).wait_send()
        if nc:
            comm.run(2, cin, cout, *carried_sems)

    vm = pl.BlockSpec(memory_space=pltpu.VMEM)
    res = pl.pallas_call(
        body, name="ada_forward",
        out_shape=(jax.ShapeDtypeStruct((N_DEV, 8, d), F32), jax.ShapeDtypeStruct((N_CHIPS, n_layers, 8, ncol), F32),
                   *comm.outs),
        in_specs=[vm, vm] + comm.specs(), out_specs=(vm, vm, *comm.specs()),
        input_output_aliases=comm.aliases(2, 2),
        scratch_shapes=[pltpu.VMEM((n_layers, 8 * N_DEV, ncol), F32), pltpu.VMEM((3, n_layers, 8, ncol), F32),
                        pltpu.SemaphoreType.DMA((7,)), pltpu.SemaphoreType.DMA((7,)),
                        pltpu.SemaphoreType.DMA((3,)), pltpu.SemaphoreType.DMA((3,))] + (comm.scratch() if nc else []),
        compiler_params=_cparams(),
    )(c, ada_w, *comm.arrays)
    return res[:2], comm.split(res[2:])


def _cast_into_slot(w2d, slot, n_slots=N_CHIPS, dtype=None):
    r, c = w2d.shape
    tr = _row_tile(r, 256)
    dtype = dtype or BF16

    def body(slot_ref, w_ref, o_ref):
        o_ref[0] = w_ref[...].astype(dtype)

    return pl.pallas_call(
        body, name="cast_into_slot",
        grid_spec=pltpu.PrefetchScalarGridSpec(
            num_scalar_prefetch=1, grid=(r // tr,),
            in_specs=[pl.BlockSpec((tr, c), lambda i, slot_ref: (i, 0))],
            out_specs=pl.BlockSpec((1, tr, c), lambda i, slot_ref: (slot_ref[0], i, 0))),
        out_shape=jax.ShapeDtypeStruct((n_slots, r, c), dtype), compiler_params=_cparams(("parallel",)),
    )(slot, w2d)


def _place():
    x, y, c = lax.axis_index("x"), lax.axis_index("y"), lax.axis_index("c")
    return x, y, c, [(1 - x, y), (x, 1 - y), (1 - x, 1 - y)]


def _remote(src, dst, send_sem, recv_sem, to):
    return pltpu.make_async_remote_copy(src_ref=src, dst_ref=dst, send_sem=send_sem, recv_sem=recv_sem,
                                        device_id=to, device_id_type=MESH)


class _AllGather:
    def __init__(self, bufs, part=(0, 1)):
        self.arrays = list(bufs)
        self.outs = [jax.ShapeDtypeStruct(b.shape, b.dtype) for b in bufs]
        self.aliased = True
        self.n_sems = 6 * len(bufs)
        self.part = part

    def run(self, phase, ins, outs, send_sems, recv_sems, base):
        x, y, c, chips = _place()
        j = 2 * x + y
        for k, buf in enumerate(outs):
            rows = buf.shape[1] // self.part[1]
            half = rows // 2

            def part(src_j, h):
                return buf.at[src_j, pl.ds(self.part[0] * rows + h * half, half), :]

            def ici(r, src_j, to):
                s = base + 6 * k + r
                return _remote(part(src_j, c), part(src_j, c), send_sems.at[s], recv_sems.at[s], to)

            def d2d(r, src_j, h):
                s = base + 6 * k + 3 + r
                return _remote(part(src_j, h), part(src_j, h), send_sems.at[s], recv_sems.at[s], (x, y, 1 - c))

            for r, (px, py) in enumerate(chips):
                if phase == 0:
                    ici(r, j, (px, py, c)).start()
                elif phase == 1:
                    ici(r, 2 * px + py, (px, py, c)).wait_recv()
                    d2d(r, 2 * px + py, c).start()
                else:
                    d2d(r, 2 * px + py, 1 - c).wait_recv()
                    ici(r, j, (px, py, c)).wait_send()
                    d2d(r, 2 * px + py, c).wait_send()


class _Swap:
    def __init__(self, grads):
        self.arrays = list(grads)
        self.outs = [jax.ShapeDtypeStruct((g.shape[0],) + g.shape[2:], g.dtype) for g in grads]
        self.aliased = False
        self.n_sems = len(grads)

    def run(self, phase, ins, outs, send_sems, recv_sems, base):
        x, y, c, _ = _place()
        for k in range(len(ins)):
            cp = _remote(ins[k].at[:, 1 - c], outs[k], send_sems.at[base + k], recv_sems.at[base + k], (x, y, 1 - c))
            if phase == 0:
                cp.start()
            elif phase == 2:
                cp.wait()


class _Exchange:
    def __init__(self, parts):
        self.arrays = list(parts)
        self.outs = [jax.ShapeDtypeStruct((3,) + p.shape[1:], p.dtype) for p in parts]
        self.aliased = False
        self.n_sems = 3 * len(parts)

    def run(self, phase, ins, outs, send_sems, recv_sems, base):
        x, y, c, chips = _place()
        for k in range(len(ins)):
            for r, (px, py) in enumerate(chips):
                s = base + 3 * k + r
                cp = _remote(ins[k].at[2 * px + py], outs[k].at[r], send_sems.at[s], recv_sems.at[s], (px, py, c))
                if phase == 0:
                    cp.start()
                elif phase == 2:
                    cp.wait()


class _Join:
    def __init__(self, bufs):
        self.arrays = list(bufs)
        self.outs = [jax.ShapeDtypeStruct(b.shape, b.dtype) for b in bufs]
        self.aliased = True
        self.n_sems = len(bufs)

    def run(self, phase, ins, outs, send_sems, recv_sems, base):
        x, y, c, _ = _place()
        for k, buf in enumerate(outs):
            mine = _remote(buf.at[c], buf.at[c], send_sems.at[base + k], recv_sems.at[base + k], (x, y, 1 - c))
            if phase == 0:
                mine.start()
            elif phase == 2:
                mine.wait_send()
                _remote(buf.at[1 - c], buf.at[1 - c], send_sems.at[base + k], recv_sems.at[base + k],
                        (x, y, 1 - c)).wait_recv()


class _Comm:
    def __init__(self, ops):
        self.ops = list(ops)
        self.arrays = [a for op in self.ops for a in op.arrays]
        self.outs = [o for op in self.ops for o in op.outs]
        self.n_sems = sum(op.n_sems for op in self.ops)

    def specs(self):
        return [pl.BlockSpec(memory_space=pl.ANY)] * len(self.arrays)

    def aliases(self, first_in, first_out):
        out, k = {}, 0
        for op in self.ops:
            for i in range(len(op.arrays)):
                if op.aliased:
                    out[first_in + k + i] = first_out + k + i
            k += len(op.arrays)
        return out

    def scratch(self):
        return [pltpu.SemaphoreType.DMA((self.n_sems,)), pltpu.SemaphoreType.DMA((self.n_sems,))]

    def run(self, phase, ins, outs, send_sems, recv_sems):
        k = base = 0
        for op in self.ops:
            n = len(op.arrays)
            op.run(phase, ins[k:k + n], outs[k:k + n], send_sems, recv_sems, base)
            k += n
            base += op.n_sems

    def split(self, results):
        out, k = [], 0
        for op in self.ops:
            out.append(list(results[k:k + len(op.arrays)]))
            k += len(op.arrays)
        return out


def _communicate(ops):
    comm = _Comm(ops)
    n = len(comm.arrays)

    def body(*refs):
        ins, outs, (send_sems, recv_sems) = refs[:n], refs[n:2 * n], refs[2 * n:]
        for phase in range(3):
            comm.run(phase, ins, outs, send_sems, recv_sems)

    res = pl.pallas_call(
        body, name="communicate", out_shape=tuple(comm.outs), in_specs=comm.specs(), out_specs=tuple(comm.specs()),
        input_output_aliases=comm.aliases(0, 0), scratch_shapes=comm.scratch(),
    )(*comm.arrays)
    return comm.split(res)


def _pool_core(he, w_ref, scale, first_row, halo, n_rows):
    d = he.shape[1]
    gd = d // len(POOL_WINDOWS)
    t = first_row + lax.broadcasted_iota(jnp.int32, (n_rows, 1), 0)
    pooled, ypre, cnts = [], [], []
    for g, w in enumerate(POOL_WINDOWS):
        hg = he[:, g * gd:(g + 1) * gd]
        s, k = hg, 1
        while k < w:
            s = s + _shift_down(s, k)
            k *= 2
        cnt = jnp.minimum(t + 1, w).astype(F32)
        p = s[halo:] / cnt - hg[halo:]
        pooled.append(p.astype(BF16))
        cnts.append(cnt)
        ypre.append(_dot(pooled[-1], w_ref[g]))
    return pooled, jnp.concatenate(ypre, axis=1), cnts


def _pool_forward(x, vec, pool_w, comm=None):
    s, d = x.shape
    ts = _row_tile(s, 512)
    nb = s // ts
    n_g, gd, _ = pool_w.shape
    comm = comm or _Comm([])
    nc = len(comm.arrays)

    def body(*refs):
        x_ref, vec_ref, w_ref = refs[:3]
        cin = refs[3:3 + nc]
        o_ref = refs[3 + nc]
        cout = refs[4 + nc:4 + 2 * nc]
        carry = refs[4 + 2 * nc]
        sems = refs[5 + 2 * nc:]
        i = pl.program_id(0)

        @pl.when(i == 0)
        def _():
            carry[...] = jnp.zeros_like(carry)
            if nc:
                comm.run(0, cin, cout, *sems)

        if nc:
            @pl.when(i == nb - 1)
            def _():
                comm.run(1, cin, cout, *sems)

        vec = vec_ref[...]
        a, sh, gt, gpost = _vec_rows(vec)
        xb = x_ref[...]
        xn, _ = _rms(xb)
        h = xn * a + sh
        he = jnp.concatenate([carry[...], h], axis=0)
        carry[...] = h[ts - POOL_HALO:]
        _, ypre, _ = _pool_core(he, w_ref, vec[5:6], i * ts, POOL_HALO, ts)
        yn, _ = _rms(ypre * vec[5:6])
        o_ref[...] = xb + gt * (yn * gpost)
        if nc:
            @pl.when(i == nb - 1)
            def _():
                comm.run(2, cin, cout, *sems)

    res = pl.pallas_call(
        body, name="pool_forward", grid=(nb,),
        in_specs=[pl.BlockSpec((ts, d), lambda i: (i, 0)), pl.BlockSpec((8, d), lambda i: (0, 0)),
                  pl.BlockSpec((n_g, gd, gd), lambda i: (0, 0, 0))] + comm.specs(),
        out_specs=(pl.BlockSpec((ts, d), lambda i: (i, 0)), *comm.specs()),
        out_shape=(jax.ShapeDtypeStruct((s, d), F32), *comm.outs),
        input_output_aliases=comm.aliases(3, 1),
        scratch_shapes=[pltpu.VMEM((POOL_HALO, d), F32)] + (comm.scratch() if nc else []),
        compiler_params=_cparams(("arbitrary",)),
    )(x, vec, pool_w, *comm.arrays)
    return res[0], comm.split(res[1:])


def _pool_backward(dout, x, vec, pool_w):
    s, d = x.shape
    ts = _row_tile(s, 512)
    nb = s // ts
    hb = ts // POOL_HALO
    n_g, gd, _ = pool_w.shape

    def body(do_ref, x_ref, xh_ref, vec_ref, w_ref, dx_ref, sum_ref, dw_ref, carry):
        step = pl.program_id(0)
        i = nb - 1 - step

        @pl.when(step == 0)
        def _():
            carry[...] = jnp.zeros_like(carry)
            sum_ref[...] = jnp.zeros_like(sum_ref)
            dw_ref[...] = jnp.zeros_like(dw_ref)

        vec = vec_ref[...]
        a, sh, gt, gpost = _vec_rows(vec)
        scale = vec[5:6]
        do = do_ref[...]
        xe = jnp.concatenate([xh_ref[...], x_ref[...]], axis=0)
        xne, re = _rms(xe)
        he = xne * a + sh
        rowid = lax.broadcasted_iota(jnp.int32, (POOL_HALO + ts, 1), 0)
        he = jnp.where((rowid >= POOL_HALO) | (i > 0), he, 0.0)
        xn, r = xne[POOL_HALO:], re[POOL_HALO:]
        pooled, ypre, cnts = _pool_core(he, w_ref, scale, i * ts, POOL_HALO, ts)
        yn, ry = _rms(ypre * scale)
        dyn = do * (gt * gpost)
        dy = _rms_bwd(dyn, yn, ry)
        dypre = (dy * scale).astype(BF16)
        dh_parts, q_parts = [], []
        for g, w in enumerate(POOL_WINDOWS):
            dyg = dypre[:, g * gd:(g + 1) * gd]
            dpool = _dot_nt(dyg, w_ref[g])
            dw_ref[g] += _dot_tn(pooled[g], dyg)
            q = dpool / cnts[g]
            qe = jnp.concatenate([q, carry[:, g * gd:(g + 1) * gd]], axis=0)
            acc, k = qe, 1
            while k < w:
                acc = acc + _shift_up(acc, k)
                k *= 2
            dh_parts.append(acc[:ts] - dpool)
            q_parts.append(q[:POOL_HALO])
        carry[...] = jnp.concatenate(q_parts, axis=1)
        dh = jnp.concatenate(dh_parts, axis=1)
        dxn = dh * a
        dx_ref[...] = do + _rms_bwd(dxn, xn, r)
        _add_rows(sum_ref, _norm_sums(do, yn, dh, xn, vec) + [_colsum(dh), _colsum(dy * ypre)])

    blk = lambda st: (nb - 1 - st, 0)
    return pl.pallas_call(
        body, name="pool_backward", grid=(nb,),
        in_specs=[pl.BlockSpec((ts, d), blk), pl.BlockSpec((ts, d), blk),
                  pl.BlockSpec((POOL_HALO, d), lambda st: (jnp.maximum((nb - 1 - st) * hb - 1, 0), 0)),
                  pl.BlockSpec((8, d), lambda st: (0, 0)), pl.BlockSpec((n_g, gd, gd), lambda st: (0, 0, 0))],
        out_specs=(pl.BlockSpec((ts, d), blk), pl.BlockSpec((8, d), lambda st: (0, 0)),
                   pl.BlockSpec((n_g, gd, gd), lambda st: (0, 0, 0))),
        out_shape=(jax.ShapeDtypeStruct((s, d), F32), jax.ShapeDtypeStruct((8, d), F32),
                   jax.ShapeDtypeStruct((n_g, gd, gd), F32)),
        scratch_shapes=[pltpu.VMEM((POOL_HALO, d), F32)],
        compiler_params=_cparams(("arbitrary",)),
    )(dout, x, x, vec, pool_w)


def _ffn_forward(x, vec, w_up, w_dw, w_down, comm=None, target=None):
    s, d = x.shape
    _, _, cs = w_up.shape
    ts = _row_tile(s, 256)
    nb = s // ts
    chunks = _lane_chunks(cs)
    comm = comm or _Comm([])
    nc = len(comm.arrays)
    nl = 0 if target is None else 1
    n_in, n_out = 5 + nl, 6 + nl

    def body(*refs):
        x_ref, vec_ref, wup_ref, wdw_ref, wdn_ref = refs[:5]
        cin = refs[n_in:n_in + nc]
        o_ref, h_ref, a0_ref, cc_ref, u_ref, y_ref = refs[n_in + nc:n_in + nc + 6]
        loss_ref = refs[n_in + nc + 6] if nl else None
        cout = refs[n_in + nc + n_out:n_in + 2 * nc + n_out]
        carry = refs[n_in + 2 * nc + n_out]
        sems = refs[n_in + 2 * nc + n_out + 1:]
        i = pl.program_id(0)

        @pl.when(i == 0)
        def _():
            carry[...] = jnp.zeros_like(carry)
            if nl:
                loss_ref[...] = jnp.zeros_like(loss_ref)
            if nc:
                comm.run(0, cin, cout, *sems)

        if nc:
            @pl.when(i == (3 * nb) // 4)
            def _():
                comm.run(1, cin, cout, *sems)

        vec = vec_ref[...]
        a, sh, gt, gpost = _vec_rows(vec)
        xb = x_ref[...]
        xn, _ = _rms(xb)
        hb = (xn * a + sh).astype(BF16)
        h_ref[...] = hb
        for q in range(2):
            for c0, cw in chunks:
                conv = []
                for j in (q, q + 2):
                    a0 = _dot(hb, wup_ref[j, :, c0:c0 + cw])
                    a0_ref[j, :, c0:c0 + cw] = a0.astype(BF16)
                    ae = jnp.concatenate([carry[j, :, c0:c0 + cw], a0], axis=0)
                    carry[j, :, c0:c0 + cw] = a0[ts - FFN_HALO:]
                    w = wdw_ref[:, j * cs + c0:j * cs + c0 + cw]
                    conv.append((w[2:3] * ae + w[1:2] * _shift_down(ae, 1) + w[0:1] * _shift_down(ae, 2))[FFN_HALO:])
                    cc_ref[j, :, c0:c0 + cw] = conv[-1].astype(BF16)
                u_ref[q, :, c0:c0 + cw] = (conv[0] * _sigmoid(conv[0]) * conv[1]).astype(BF16)
        y = _dot(u_ref[0], wdn_ref[0]) + _dot(u_ref[1], wdn_ref[1])
        y_ref[...] = y
        yn, _ = _rms(y)
        x_out = xb + gt * (yn * gpost)
        if nl:
            err = x_out - refs[5][...]
            o_ref[...] = err * (1.0 / d)
            loss_ref[0:1, :] += _colsum(err * err) * (0.5 / d)
        else:
            o_ref[...] = x_out
        if nc:
            @pl.when(i == nb - 1)
            def _():
                comm.run(2, cin, cout, *sems)

    const3 = lambda i: (0, 0, 0)
    res = pl.pallas_call(
        body, name="ffn_forward", grid=(nb,),
        in_specs=[pl.BlockSpec((ts, d), lambda i: (i, 0)), pl.BlockSpec((8, d), lambda i: (0, 0)),
                  pl.BlockSpec(w_up.shape, const3, pipeline_mode=pl.Buffered(1)),
                  pl.BlockSpec(w_dw.shape, lambda i: (0, 0)),
                  pl.BlockSpec(w_down.shape, const3, pipeline_mode=pl.Buffered(1))]
        + [pl.BlockSpec((ts, d), lambda i: (i, 0))] * nl + comm.specs(),
        out_specs=(pl.BlockSpec((ts, d), lambda i: (i, 0)), pl.BlockSpec((ts, d), lambda i: (i, 0)),
                   pl.BlockSpec((4, ts, cs), lambda i: (0, i, 0)), pl.BlockSpec((4, ts, cs), lambda i: (0, i, 0)),
                   pl.BlockSpec((2, ts, cs), lambda i: (0, i, 0)), pl.BlockSpec((ts, d), lambda i: (i, 0)),
                   *[pl.BlockSpec((8, d), lambda i: (0, 0))] * nl, *comm.specs()),
        out_shape=(jax.ShapeDtypeStruct((s, d), F32), jax.ShapeDtypeStruct((s, d), BF16),
                   jax.ShapeDtypeStruct((4, s, cs), BF16), jax.ShapeDtypeStruct((4, s, cs), BF16),
                   jax.ShapeDtypeStruct((2, s, cs), BF16), jax.ShapeDtypeStruct((s, d), F32),
                   *[jax.ShapeDtypeStruct((8, d), F32)] * nl, *comm.outs),
        input_output_aliases=comm.aliases(n_in, n_out),
        scratch_shapes=[pltpu.VMEM((4, FFN_HALO, cs), F32)] + (comm.scratch() if nc else []),
        compiler_params=_cparams(("arbitrary",)),
    )(x, vec, w_up, w_dw, w_down, *([target] * nl), *comm.arrays)
    return res[:n_out], comm.split(res[n_out:])


def _ffn_backward(dout, x, y, a0, cc, vec, w_up, w_dw, w_down):
    s, d = x.shape
    _, _, cs = w_up.shape
    ts = _row_tile(s, 256)
    nb = s // ts
    chunks = _lane_chunks(cs)

    def body(do_ref, x_ref, y_ref, a0_ref, cc_ref, vec_ref, wup_ref, wdw_ref, wdn_ref,
             dx_ref, da0_ref, dy_ref, sum_ref, dwdw_ref, carry, du_s, dh_s):
        step = pl.program_id(0)

        @pl.when(step == 0)
        def _():
            carry[...] = jnp.zeros_like(carry)
            sum_ref[...] = jnp.zeros_like(sum_ref)
            dwdw_ref[...] = jnp.zeros_like(dwdw_ref)

        vec = vec_ref[...]
        a, sh, gt, gpost = _vec_rows(vec)
        do = do_ref[...]
        yn, ry = _rms(y_ref[...])
        dy = _rms_bwd(do * (gt * gpost), yn, ry)
        dyb = dy.astype(BF16)
        dy_ref[...] = dyb
        order = [(q, c0, cw) for q in range(2) for c0, cw in chunks]

        def du_pieces(idx):
            q, c0, cw = order[idx]
            return [functools.partial(_store_dot_nt, du_s.at[idx % 2, :, n0:min(n0 + MXU_LANES, cw)], dy_ref,
                                      wdn_ref.at[q, c0 + n0:c0 + min(n0 + MXU_LANES, cw), :])
                    for n0 in range(0, cw, MXU_LANES)]

        def dh_pieces():
            return [functools.partial(_store_dot_nt2, dh_s.at[:, n0:n0 + MXU_LANES], da0_ref.at[0], da0_ref.at[2],
                                      wup_ref.at[0, n0:n0 + MXU_LANES, :], wup_ref.at[2, n0:n0 + MXU_LANES, :])
                    for n0 in range(0, d, MXU_LANES)]

        for piece in du_pieces(0):
            piece()
        later = dh_pieces()
        for idx, (q, c0, cw) in enumerate(order):
            work = du_pieces(idx + 1) if idx + 1 < len(order) else []
            if q == 1:
                share = -(-len(later) // (len(order) - idx))
                work, later = work + later[:share], later[share:]

            def pump(part, of=3):
                for piece in work[part::of]:
                    piece()

            cg = cc_ref[q, :, c0:c0 + cw].astype(F32)
            cv = cc_ref[q + 2, :, c0:c0 + cw].astype(F32)
            sg = _sigmoid(cg)
            sl = cg * sg
            du = du_s[idx % 2, :, :cw]
            dconv = {q: du * cv * (sg * (1.0 + cg * (1.0 - sg))), q + 2: du * sl}
            pump(0)
            for part, j in enumerate((q, q + 2)):
                dae = jnp.concatenate([dconv[j], carry[j, :, c0:c0 + cw]], axis=0)
                carry[j, :, c0:c0 + cw] = dconv[j][:FFN_HALO]
                up1 = _shift_down(dae, FFN_HALO - 1)[FFN_HALO:]
                up2 = _shift_down(dae, FFN_HALO - 2)[FFN_HALO:]
                lanes = slice(j * cs + c0, j * cs + c0 + cw)
                w = wdw_ref[:, lanes]
                da0_ref[j, :, c0:c0 + cw] = (w[2:3] * dconv[j] + w[1:2] * up1 + w[0:1] * up2).astype(BF16)
                a0 = a0_ref[j, :, c0:c0 + cw].astype(F32)
                dwdw_ref[0:1, lanes] += _colsum(up2 * a0)
                dwdw_ref[1:2, lanes] += _colsum(up1 * a0)
                dwdw_ref[2:3, lanes] += _colsum(dconv[j] * a0)
                pump(part + 1)
        dh = dh_s[...] + _dot_nt(da0_ref[1], wup_ref[1]) + _dot_nt(da0_ref[3], wup_ref[3])
        xn, r = _rms(x_ref[...])
        dx_ref[...] = do + _rms_bwd(dh * a, xn, r)
        _add_rows(sum_ref, _norm_sums(do, yn, dh, xn, vec) + [_colsum(dh)])

    blk = lambda st: (nb - 1 - st, 0)
    blk3 = lambda st: (0, nb - 1 - st, 0)
    const3 = lambda st: (0, 0, 0)
    return pl.pallas_call(
        body, name="ffn_backward", grid=(nb,),
        in_specs=[pl.BlockSpec((ts, d), blk), pl.BlockSpec((ts, d), blk), pl.BlockSpec((ts, d), blk),
                  pl.BlockSpec((4, ts, cs), blk3), pl.BlockSpec((4, ts, cs), blk3),
                  pl.BlockSpec((8, d), lambda st: (0, 0)),
                  pl.BlockSpec(w_up.shape, const3, pipeline_mode=pl.Buffered(1)),
                  pl.BlockSpec(w_dw.shape, lambda st: (0, 0)),
                  pl.BlockSpec(w_down.shape, const3, pipeline_mode=pl.Buffered(1))],
        out_specs=(pl.BlockSpec((ts, d), blk), pl.BlockSpec((4, ts, cs), blk3),
                   pl.BlockSpec((ts, d), blk), pl.BlockSpec((8, d), lambda st: (0, 0)),
                   pl.BlockSpec((8, 4 * cs), lambda st: (0, 0))),
        out_shape=(jax.ShapeDtypeStruct((s, d), F32), jax.ShapeDtypeStruct((4, s, cs), BF16),
                   jax.ShapeDtypeStruct((s, d), BF16),
                   jax.ShapeDtypeStruct((8, d), F32), jax.ShapeDtypeStruct((8, 4 * cs), F32)),
        scratch_shapes=[pltpu.VMEM((4, FFN_HALO, cs), F32), pltpu.VMEM((2, ts, max(cw for _, cw in chunks)), F32),
                        pltpu.VMEM((ts, d), F32)],
        compiler_params=_cparams(("arbitrary",)),
    )(dout, x, y, a0, cc, vec, w_up, w_dw, w_down)


def _conv_halo(width):
    return -(-(width - 1) // 8) * 8


def _conv_forward(x, vec, cvec, w_pw1, b_pw1, w_dw, w_pw2):
    s, d = x.shape
    kw = w_dw.shape[0]
    halo = _conv_halo(kw)
    ts = _row_tile(s, 512)
    hd = d // 2

    def body(x_ref, vec_ref, cvec_ref, w1_ref, b1_ref, wdw_ref, w2_ref,
             o_ref, h_ref, a_ref, uc_ref, z_ref, y_ref, carry):
        i = pl.program_id(0)

        @pl.when(i == 0)
        def _():
            carry[...] = jnp.zeros_like(carry)

        vec, cvec = vec_ref[...], cvec_ref[...]
        a, sh, gt, gpost = _vec_rows(vec)
        xb = x_ref[...]
        xn, _ = _rms(xb)
        hb = (xn * a + sh).astype(BF16)
        h_ref[...] = hb
        for j in range(4):
            a_ref[:, j * hd:(j + 1) * hd] = _dot(hb, w1_ref[j]) + b1_ref[:, j * hd:(j + 1) * hd]
        u = a_ref[:, :d] * _sigmoid(a_ref[:, d:])
        carry[halo:, :] = u
        for r0 in range(0, ts, CONV_ROWS):
            for l0 in range(0, d, CONV_LANES):
                lanes = slice(l0, l0 + CONV_LANES)
                src = carry[r0:r0 + CONV_ROWS + halo, lanes]
                acc = jnp.zeros((CONV_ROWS, CONV_LANES), F32) + cvec[0:1, lanes]
                for k in range(kw):
                    acc = acc + wdw_ref[k:k + 1, lanes] * _shift_down(src, kw - 1 - k)[halo:]
                uc_ref[r0:r0 + CONV_ROWS, lanes] = acc
        carry[:halo, :] = u[ts - halo:]
        uc = uc_ref[...]
        mu = jnp.mean(uc, axis=-1, keepdims=True)
        cen = uc - mu
        rstd = lax.rsqrt(jnp.mean(cen * cen, axis=-1, keepdims=True) + EPS)
        l = cen * rstd * cvec[1:2] + cvec[2:3]
        zb = (l * _sigmoid(l)).astype(BF16)
        z_ref[...] = zb
        y = _dot(zb, w2_ref[...]) + cvec[3:4]
        y_ref[...] = y
        yn, _ = _rms(y)
        o_ref[...] = xb + gt * (yn * gpost)

    row = lambda i: (i, 0)
    const2 = lambda i: (0, 0)
    return pl.pallas_call(
        body, name="conv_forward", grid=(s // ts,),
        in_specs=[pl.BlockSpec((ts, d), row), pl.BlockSpec((8, d), const2), pl.BlockSpec((8, d), const2),
                  pl.BlockSpec(w_pw1.shape, lambda i: (0, 0, 0)), pl.BlockSpec(b_pw1.shape, const2),
                  pl.BlockSpec(w_dw.shape, const2), pl.BlockSpec(w_pw2.shape, const2)],
        out_specs=(pl.BlockSpec((ts, d), row), pl.BlockSpec((ts, d), row), pl.BlockSpec((ts, 2 * d), row),
                   pl.BlockSpec((ts, d), row), pl.BlockSpec((ts, d), row), pl.BlockSpec((ts, d), row)),
        out_shape=(jax.ShapeDtypeStruct((s, d), F32), jax.ShapeDtypeStruct((s, d), BF16),
                   jax.ShapeDtypeStruct((s, 2 * d), F32), jax.ShapeDtypeStruct((s, d), F32),
                   jax.ShapeDtypeStruct((s, d), BF16), jax.ShapeDtypeStruct((s, d), F32)),
        scratch_shapes=[pltpu.VMEM((halo + ts, d), F32)],
        compiler_params=_cparams(("arbitrary",)),
    )(x, vec, cvec, w_pw1, b_pw1, w_dw, w_pw2)


def _conv_backward(dout, x, y, a_pre, uc, vec, cvec, w_pw1, w_dw, w_pw2):
    s, d = x.shape
    kw = w_dw.shape[0]
    kpad = -(-kw // 8) * 8
    halo = _conv_halo(kw)
    ts = _row_tile(s, 512)
    nb = s // ts
    hb = ts // halo
    hd = d // 2

    def body(do_ref, x_ref, y_ref, a_ref, ah_ref, uc_ref, vec_ref, cvec_ref, w1_ref, wdw_ref, w2_ref,
             dx_ref, da_ref, dy_ref, sum_ref, dwdw_ref, carry):
        step = pl.program_id(0)
        i = nb - 1 - step

        @pl.when(step == 0)
        def _():
            carry[...] = jnp.zeros_like(carry)
            sum_ref[...] = jnp.zeros_like(sum_ref)
            dwdw_ref[...] = jnp.zeros_like(dwdw_ref)

        vec, cvec = vec_ref[...], cvec_ref[...]
        a, sh, gt, gpost = _vec_rows(vec)
        do = do_ref[...]
        yn, ry = _rms(y_ref[...])
        dy = _rms_bwd(do * (gt * gpost), yn, ry)
        dyb = dy.astype(BF16)
        dy_ref[...] = dyb
        dz = _dot_nt(dyb, w2_ref[...])
        uc = uc_ref[...]
        mu = jnp.mean(uc, axis=-1, keepdims=True)
        cen = uc - mu
        rstd = lax.rsqrt(jnp.mean(cen * cen, axis=-1, keepdims=True) + EPS)
        lhat = cen * rstd
        l = lhat * cvec[1:2] + cvec[2:3]
        sgl = _sigmoid(l)
        dl = dz * (sgl * (1.0 + l * (1.0 - sgl)))
        dlhat = dl * cvec[1:2]
        duc = rstd * (dlhat - jnp.mean(dlhat, axis=-1, keepdims=True)
                      - lhat * jnp.mean(dlhat * lhat, axis=-1, keepdims=True))
        ae = jnp.concatenate([ah_ref[...] * (i > 0).astype(F32), a_ref[...]], axis=0)
        sgate = _sigmoid(ae[:, d:])
        val = ae[:, :d]
        ue = val * sgate
        rowid = lax.broadcasted_iota(jnp.int32, (halo + ts, 1), 0)
        ue = jnp.where((rowid >= halo) | (i > 0), ue, 0.0)
        duce = jnp.concatenate([duc, carry[...]], axis=0)
        carry[...] = duc[:halo]
        du = jnp.zeros((ts, d), F32)
        for k in range(kw):
            du = du + wdw_ref[k:k + 1, :] * _shift_down(duce, halo - (kw - 1 - k))[halo:]
            dwdw_ref[k:k + 1, :] += _colsum(duc * _shift_down(ue, kw - 1 - k)[halo:])
        sg, vl = sgate[halo:], val[halo:]
        dval = du * sg
        dgate = du * vl * (sg * (1.0 - sg))
        dvb, dgb = dval.astype(BF16), dgate.astype(BF16)
        dh = jnp.zeros((ts, d), F32)
        for j in range(2):
            da_ref[j] = dvb[:, j * hd:(j + 1) * hd]
            da_ref[j + 2] = dgb[:, j * hd:(j + 1) * hd]
            dh = dh + _dot_nt(dvb[:, j * hd:(j + 1) * hd], w1_ref[j]) + _dot_nt(dgb[:, j * hd:(j + 1) * hd], w1_ref[j + 2])
        xn, r = _rms(x_ref[...])
        dx_ref[...] = do + _rms_bwd(dh * a, xn, r)
        _add_rows(sum_ref, _norm_sums(do, yn, dh, xn, vec) + [_colsum(dh), _colsum(dy), _colsum(dl * lhat), _colsum(dl),
                            _colsum(duc), _colsum(dval), _colsum(dgate)])

    blk = lambda st: (nb - 1 - st, 0)
    const2 = lambda st: (0, 0)
    return pl.pallas_call(
        body, name="conv_backward", grid=(nb,),
        in_specs=[pl.BlockSpec((ts, d), blk), pl.BlockSpec((ts, d), blk), pl.BlockSpec((ts, d), blk),
                  pl.BlockSpec((ts, 2 * d), blk),
                  pl.BlockSpec((halo, 2 * d), lambda st: (jnp.maximum((nb - 1 - st) * hb - 1, 0), 0)),
                  pl.BlockSpec((ts, d), blk), pl.BlockSpec((8, d), const2), pl.BlockSpec((8, d), const2),
                  pl.BlockSpec(w_pw1.shape, lambda st: (0, 0, 0)), pl.BlockSpec(w_dw.shape, const2),
                  pl.BlockSpec(w_pw2.shape, const2)],
        out_specs=(pl.BlockSpec((ts, d), blk), pl.BlockSpec((4, ts, hd), lambda st: (0, nb - 1 - st, 0)),
                   pl.BlockSpec((ts, d), blk), pl.BlockSpec((16, d), const2), pl.BlockSpec((kpad, d), const2)),
        out_shape=(jax.ShapeDtypeStruct((s, d), F32), jax.ShapeDtypeStruct((4, s, hd), BF16),
                   jax.ShapeDtypeStruct((s, d), BF16), jax.ShapeDtypeStruct((16, d), F32),
                   jax.ShapeDtypeStruct((kpad, d), F32)),
        scratch_shapes=[pltpu.VMEM((halo, d), F32)],
        compiler_params=_cparams(("arbitrary",)),
    )(dout, x, y, a_pre, a_pre, uc, vec, cvec, w_pw1, w_dw, w_pw2)


def _weight_grad(a, b, comm=None):
    na, s, k = a.shape
    nb_, _, n = b.shape
    nj = max(na, nb_)
    ts = _row_tile(s, 2048)
    nt = s // ts
    comm = comm or _Comm([])
    nc = len(comm.arrays)

    def body(*refs):
        a_ref, b_ref = refs[:2]
        cin = refs[2:2 + nc]
        o_ref = refs[2 + nc]
        cout = refs[3 + nc:3 + 2 * nc]
        sems = refs[3 + 2 * nc:]
        j, t = pl.program_id(0), pl.program_id(1)

        if nc:
            @pl.when((j == 0) & (t == 0))
            def _():
                comm.run(0, cin, cout, *sems)

            @pl.when((j == nj // 2) & (t == nt // 2))
            def _():
                comm.run(1, cin, cout, *sems)

        @pl.when(t == 0)
        def _():
            o_ref[...] = jnp.zeros_like(o_ref)

        o_ref[0] += _dot_tn(a_ref[0], b_ref[0])

        if nc:
            @pl.when((j == nj - 1) & (t == nt - 1))
            def _():
                comm.run(2, cin, cout, *sems)

    res = pl.pallas_call(
        body, name="weight_grad", grid=(nj, nt),
        in_specs=[pl.BlockSpec((1, ts, k), (lambda j, t: (j, t, 0)) if na > 1 else (lambda j, t: (0, t, 0))),
                  pl.BlockSpec((1, ts, n), (lambda j, t: (j, t, 0)) if nb_ > 1 else (lambda j, t: (0, t, 0)))]
        + comm.specs(),
        out_specs=(pl.BlockSpec((1, k, n), lambda j, t: (j, 0, 0)), *comm.specs()),
        out_shape=(jax.ShapeDtypeStruct((nj, k, n), F32), *comm.outs),
        input_output_aliases=comm.aliases(2, 1),
        scratch_shapes=comm.scratch() if nc else [],
        compiler_params=_cparams(("arbitrary", "arbitrary") if nc else ("parallel", "arbitrary")),
    )(a, b, *comm.arrays)
    return res[0], comm.split(res[1:])


def _adamw_math(w, g, m, v):
    nm = ADAM_B1 * m + (1.0 - ADAM_B1) * g
    nv = ADAM_B2 * v + (1.0 - ADAM_B2) * (g * g)
    m_hat = nm * (1.0 / (1.0 - ADAM_B1 ** ADAM_STEP))
    v_hat = nv * (1.0 / (1.0 - ADAM_B2 ** ADAM_STEP))
    return -ADAM_LR * (m_hat / (jnp.sqrt(v_hat) + ADAM_EPS) + ADAM_WD * w), nm, nv


def _adamw_many(params):
    n = len(params)

    def body(*refs):
        for k in range(n):
            w_ref, g_ref, m_ref, v_ref = refs[4 * k:4 * k + 4]
            outs = refs[4 * n + 3 * k:4 * n + 3 * k + 3]
            for o_ref, val in zip(outs, _adamw_math(w_ref[...], g_ref[...], m_ref[...], v_ref[...])):
                o_ref[...] = val

    vm = pl.BlockSpec(memory_space=pltpu.VMEM)
    res = pl.pallas_call(
        body, name="adamw_many", in_specs=[vm] * (4 * n), out_specs=tuple([vm] * (3 * n)),
        out_shape=tuple(jax.ShapeDtypeStruct(p[0].shape, F32) for p in params for _ in range(3)),
        compiler_params=_cparams(),
    )(*[a for p in params for a in p])
    return [res[3 * k:3 * k + 3] for k in range(n)]


def _adamw(w, g, m, v, comm=None):
    nl, r, c = w.shape
    tr = _row_tile(r, 256)
    nr = r // tr
    comm = comm or _Comm([])
    nc = len(comm.arrays)

    def body(*refs):
        w_ref, g_ref, m_ref, v_ref = refs[:4]
        cin = refs[4:4 + nc]
        d_ref, nm_ref, nv_ref = refs[4 + nc:7 + nc]
        cout = refs[7 + nc:7 + 2 * nc]
        sems = refs[7 + 2 * nc:]
        l, i = pl.program_id(0), pl.program_id(1)
        if nc:
            @pl.when((l == 0) & (i == 0))
            def _():
                comm.run(0, cin, cout, *sems)

            @pl.when((l == nl // 2) & (i == nr // 2))
            def _():
                comm.run(1, cin, cout, *sems)

        d_ref[...], nm_ref[...], nv_ref[...] = _adamw_math(w_ref[...], g_ref[...], m_ref[...], v_ref[...])
        if nc:
            @pl.when((l == nl - 1) & (i == nr - 1))
            def _():
                comm.run(2, cin, cout, *sems)

    spec = pl.BlockSpec((1, tr, c), lambda l, i: (l, i, 0))
    shp = jax.ShapeDtypeStruct((nl, r, c), F32)
    res = pl.pallas_call(
        body, name="adamw", grid=(nl, nr), in_specs=[spec] * 4 + comm.specs(),
        out_specs=(spec,) * 3 + tuple(comm.specs()), out_shape=(shp,) * 3 + tuple(comm.outs),
        input_output_aliases=comm.aliases(4, 3), scratch_shapes=comm.scratch() if nc else [],
        compiler_params=_cparams(("arbitrary", "arbitrary") if nc else ("parallel", "parallel")),
    )(w, g, m, v, *comm.arrays)
    return res[:3], comm.split(res[3:])


def _add_my_half(g, other, idx):
    _, _, h, c = g.shape
    th = _row_tile(h, 256)

    def body(idx_ref, g_ref, o_ref, out_ref):
        out_ref[...] = (g_ref[:, 0] + o_ref[...]).astype(BF16)

    return pl.pallas_call(
        body, name="add_my_half",
        grid_spec=pltpu.PrefetchScalarGridSpec(
            num_scalar_prefetch=1, grid=(3, h // th),
            in_specs=[pl.BlockSpec((1, 1, th, c), lambda j, i, idx_ref: (idx_ref[2 + j], idx_ref[1], i, 0)),
                      pl.BlockSpec((1, th, c), lambda j, i, idx_ref: (idx_ref[2 + j], i, 0))],
            out_specs=pl.BlockSpec((1, th, c), lambda j, i, idx_ref: (idx_ref[2 + j], i, 0))),
        out_shape=jax.ShapeDtypeStruct(other.shape, BF16),
        compiler_params=_cparams(("parallel", "parallel")),
    )(idx, g, other)


def _sum_for_my_chip(g, other, got, idx):
    _, _, h, c = g.shape
    th = _row_tile(h, 256)

    def body(idx_ref, g_ref, o_ref, q_ref, out_ref):
        out_ref[0] = (((g_ref[0, 0] + o_ref[0]) + q_ref[0].astype(F32)) + q_ref[1].astype(F32)) + q_ref[2].astype(F32)

    return pl.pallas_call(
        body, name="sum_for_my_chip",
        grid_spec=pltpu.PrefetchScalarGridSpec(
            num_scalar_prefetch=1, grid=(h // th,),
            in_specs=[pl.BlockSpec((1, 1, th, c), lambda i, idx_ref: (idx_ref[0], idx_ref[1], i, 0)),
                      pl.BlockSpec((1, th, c), lambda i, idx_ref: (idx_ref[0], i, 0)),
                      pl.BlockSpec((3, th, c), lambda i, idx_ref: (0, i, 0))],
            out_specs=pl.BlockSpec((1, th, c), lambda i, idx_ref: (idx_ref[1], i, 0))),
        out_shape=jax.ShapeDtypeStruct((2, h, c), F32),
        compiler_params=_cparams(("parallel",)),
    )(idx, g, other, got)


class _Reducer:
    def __init__(self, idx):
        self.idx = idx
        self.groups = []

    def add(self, grads):
        group = {"state": 0, "g": [g.reshape(4, 2, g.shape[1] // 2, g.shape[2]) for g in grads]}
        self.groups.append(group)
        return group

    def steps(self):
        ops, owners = [], []
        for gr in self.groups:
            if gr["state"] == 0:
                ops.append(_Swap(gr["g"]))
            elif gr["state"] == 1:
                ops.append(_Exchange(gr["parts"]))
            elif gr["state"] == 2:
                ops.append(_Join(gr["bufs"]))
            else:
                continue
            owners.append(gr)
        return ops, owners

    def absorb(self, owners, results):
        for gr, res in zip(owners, results):
            if gr["state"] == 0:
                gr["other"] = res
                gr["parts"] = [_add_my_half(g, o, self.idx) for g, o in zip(gr["g"], res)]
            elif gr["state"] == 1:
                gr["bufs"] = [_sum_for_my_chip(g, o, q, self.idx) for g, o, q in zip(gr["g"], gr["other"], res)]
            else:
                gr["full"] = [b.reshape(2 * b.shape[1], b.shape[2]) for b in res]
            gr["state"] += 1

    def drain(self):
        while any(gr["state"] < 3 for gr in self.groups):
            ops, owners = self.steps()
            self.absorb(owners, _communicate(ops))


class _GatherRows:
    def __init__(self, bufs):
        self.arrays = list(bufs)
        self.outs = [jax.ShapeDtypeStruct(b.shape, b.dtype) for b in bufs]
        self.aliased = True
        self.n_sems = 7 * len(bufs)

    def run(self, phase, ins, outs, send_sems, recv_sems, base):
        x, y, c, chips = _place()
        me, sibling = (x, y, c), (x, y, 1 - c)
        for k, buf in enumerate(outs):
            def copy(i, block_of, to):
                blk = buf.at[4 * block_of[0] + 2 * block_of[1] + block_of[2]]
                return _remote(blk, blk, send_sems.at[base + 7 * k + i], recv_sems.at[base + 7 * k + i], to)

            if phase == 0:
                copy(0, me, sibling).start()
            for r, (px, py) in enumerate(chips):
                if phase == 0:
                    copy(1 + r, me, (px, py, c)).start()
                elif phase == 1:
                    copy(1 + r, (px, py, c), me).wait_recv()
                    copy(4 + r, (px, py, c), sibling).start()
                else:
                    copy(4 + r, (px, py, 1 - c), me).wait_recv()
                    copy(1 + r, me, (px, py, c)).wait_send()
                    copy(4 + r, (px, py, c), sibling).wait_send()
            if phase == 2:
                copy(0, sibling, me).wait_recv()
                copy(0, me, sibling).wait_send()


def _sum_devices(gathered):
    nd, m, n = gathered.shape

    def body(g_ref, o_ref):
        acc = g_ref[0]
        for b in range(1, nd):
            acc = acc + g_ref[b]
        o_ref[...] = acc

    return pl.pallas_call(
        body, name="sum_devices", out_shape=jax.ShapeDtypeStruct((m, n), F32),
        in_specs=[pl.BlockSpec(memory_space=pltpu.VMEM)], out_specs=pl.BlockSpec(memory_space=pltpu.VMEM),
        compiler_params=_cparams(),
    )(gathered)


def _ada_update(c_all, dmod_cols, w, m, v):
    nl, nd, ncol = dmod_cols.shape
    d = c_all.shape[1]
    tr = _row_tile(d, 256)

    def body(c_ref, dm_ref, w_ref, m_ref, v_ref, g_ref, d_ref, nm_ref, nv_ref):
        g = lax.dot_general(c_ref[...], dm_ref[0], (((0,), (0,)), ((), ())),
                            preferred_element_type=F32, precision=lax.Precision.HIGHEST)
        g_ref[0] = g
        d_ref[0], nm_ref[0], nv_ref[0] = _adamw_math(w_ref[0], g, m_ref[0], v_ref[0])

    spec = pl.BlockSpec((1, tr, ncol), lambda l, i: (l, i, 0))
    shp = jax.ShapeDtypeStruct((nl, d, ncol), F32)
    return pl.pallas_call(
        body, name="ada_update", grid=(nl, d // tr),
        in_specs=[pl.BlockSpec((nd, tr), lambda l, i: (0, i)), pl.BlockSpec((1, nd, ncol), lambda l, i: (l, 0, 0)),
                  spec, spec, spec],
        out_specs=(spec,) * 4, out_shape=(shp,) * 4, compiler_params=_cparams(("parallel", "parallel")),
    )(c_all, dmod_cols, w, m, v)


def _pad_rows(a, rows):
    return jnp.pad(a, ((0, rows - a.shape[0]), (0, 0)))


def _shard_cols(full, chip, width):
    return lax.dynamic_slice_in_dim(full, chip * width, width, axis=full.ndim - 1)


def kernel(x, c, ada_w, ada_b, pre_g, post_g, pool_w, pool_scale, cv_w_pw1, cv_b_pw1, cv_w_dw, cv_b_dw, cv_ln_g, cv_ln_b, cv_w_pw2, cv_b_pw2, ffn_w_up, ffn_w_dw, ffn_w_down, loss_target, m_ada_w, m_ada_b, m_pre_g, m_post_g, m_pool_w, m_pool_scale, m_cv_w_pw1, m_cv_b_pw1, m_cv_w_dw, m_cv_b_dw, m_cv_ln_g, m_cv_ln_b, m_cv_w_pw2, m_cv_b_pw2, m_ffn_w_up, m_ffn_w_dw, m_ffn_w_down, v_ada_w, v_ada_b, v_pre_g, v_post_g, v_pool_w, v_pool_scale, v_cv_w_pw1, v_cv_b_pw1, v_cv_w_dw, v_cv_b_dw, v_cv_ln_g, v_cv_ln_b, v_cv_w_pw2, v_cv_b_pw2, v_ffn_w_up, v_ffn_w_dw, v_ffn_w_down):
    s, d = x.shape[1], x.shape[2]
    dq = d // N_CHIPS
    n_g = pool_w.shape[1]
    gq = pool_w.shape[2]
    gd = pool_w.shape[3]
    kw = cv_w_dw.shape[1]
    cs = ffn_w_up.shape[2]
    fq = ffn_w_down.shape[1]
    chip = 2 * lax.axis_index("x") + lax.axis_index("y")
    core = lax.axis_index("c")
    chip1 = jnp.reshape(chip, (1,)).astype(jnp.int32)
    core1 = jnp.reshape(core, (1,)).astype(jnp.int32)
    xs, tgt = x[0], loss_target[0]

    small_rows = [pre_g.reshape(4, dq), post_g.reshape(4, dq), cv_w_dw[0], cv_b_dw, cv_ln_g, cv_ln_b, cv_b_pw2,
                  cv_b_pw1.reshape(2, dq)]
    small = jnp.concatenate(small_rows, axis=0)
    n_small = small.shape[0]
    small = _pad_rows(small, -(-n_small // 16) * 16)
    dwf = _pad_rows(ffn_w_dw.reshape(6, cs), 16)
    first = _AllGather([_cast_into_slot(pool_w.reshape(n_g * gq, gd), chip1), _cast_into_slot(small, chip1, dtype=F32),
                        _cast_into_slot(dwf, chip1, dtype=F32)])
    upper = _AllGather([_cast_into_slot(ffn_w_up[0], chip1)], part=(0, 2))
    (c_rep, mod_rep), ((g_pool, g_small, g_dwf), (g_up0,)) = _ada_forward(c, ada_w, _Comm([first, upper]))
    c_all = c_rep[:, 0, :]
    mod = mod_rep[:, :, 0, :].transpose(1, 0, 2).reshape(ada_b.shape) + ada_b
    lower = _AllGather([g_up0], part=(1, 2))
    second = _AllGather([_cast_into_slot(ffn_w_down[0], chip1)])
    later = _AllGather([_cast_into_slot(cv_w_pw1[0], chip1), _cast_into_slot(cv_w_pw2[0], chip1),
                        _cast_into_slot(ffn_w_up[1], chip1), _cast_into_slot(ffn_w_down[1], chip1)])
    poolw_full = g_pool.reshape(N_CHIPS, n_g, gq, gd).transpose(1, 0, 2, 3).reshape(n_g, gd, gd)
    smallf = g_small.transpose(1, 0, 2).reshape(g_small.shape[1], d)
    pre_full, post_full = smallf[0:4].reshape(2, 2, d), smallf[4:8].reshape(2, 2, d)
    wdw31 = smallf[8:8 + kw]
    o = 8 + kw
    b_dw, ln_g, ln_b, b_pw2 = smallf[o:o + 1], smallf[o + 1:o + 2], smallf[o + 2:o + 3], smallf[o + 3:o + 4]
    b_pw1 = g_small[:, o + 4:o + 6, :].reshape(1, 2 * d)
    ffn_dw = g_dwf[:, :6, :].transpose(1, 0, 2).reshape(2, 3, N_CHIPS * cs)

    def sub_vec(layer, sub, extra=None):
        m6 = mod[layer].reshape(6, d)
        rows = [pre_full[layer, sub][None], 1.0 + m6[3 * sub + 1][None], m6[3 * sub][None], m6[3 * sub + 2][None],
                post_full[layer, sub][None]]
        if extra is not None:
            rows.append(extra)
        return _pad_rows(jnp.concatenate(rows, axis=0), 8)

    vec_pool = sub_vec(0, 0, pool_scale)
    vec_f0, vec_conv, vec_f1 = sub_vec(0, 1), sub_vec(1, 0), sub_vec(1, 1)
    cvec = _pad_rows(jnp.concatenate([b_dw, ln_g, ln_b, b_pw2], axis=0), 8)

    x1, ((g_up0,), (g_dn0,)) = _pool_forward(xs, vec_pool, poolw_full, _Comm([lower, second]))
    w_up0, w_dn0 = g_up0, g_dn0.reshape(2, 2 * fq, d)
    (x2, h_f0, a0_f0, cc_f0, u_f0, y_f0), ((g_pw1, g_pw2, g_up1, g_dn1),) = _ffn_forward(
        x1, vec_f0, w_up0, ffn_dw[0], w_dn0, _Comm([later]))
    pw2_full = g_pw2.reshape(d, d)
    w_up1, w_dn1 = g_up1, g_dn1.reshape(2, 2 * fq, d)
    x3, h_cv, a_cv, uc_cv, z_cv, y_cv = _conv_forward(x2, vec_conv, cvec, g_pw1, b_pw1, wdw31, pw2_full)
    (dx4, h_f1, a0_f1, cc_f1, u_f1, y_f1, loss_rows), _ = _ffn_forward(x3, vec_f1, w_up1, ffn_dw[1], w_dn1, target=tgt)

    dx3, da0_f1, dy_f1, sum_f1, dwdw_f1 = _ffn_backward(dx4, x3, y_f1, a0_f1, cc_f1, vec_f1, w_up1, ffn_dw[1], w_dn1)
    dx2, da_cv, dy_cv, sum_cv, dwdw_cv = _conv_backward(dx3, x2, y_cv, a_cv, uc_cv, vec_conv, cvec, g_pw1, wdw31, pw2_full)
    dx1, da0_f0, dy_f0, sum_f0, dwdw_f0 = _ffn_backward(dx2, x1, y_f0, a0_f0, cc_f0, vec_f0, w_up0, ffn_dw[0], w_dn0)
    dx0, sum_pool, gw_pool = _pool_backward(dx1, xs, vec_pool, poolw_full)
    gw_pool4 = gw_pool.reshape(n_g, N_CHIPS, gq, gd).transpose(1, 0, 2, 3).reshape(N_CHIPS, n_g * gq, gd)

    slab = jnp.concatenate([sum_f1, sum_cv, dwdw_cv, sum_f0, sum_pool, loss_rows], axis=0)
    wide = jnp.concatenate([dwdw_f1, dwdw_f0], axis=0)
    n_slab = slab.shape[0]
    mine = jnp.concatenate([slab, wide.reshape(-1, d)], axis=0)
    rows_of_all = _cast_into_slot(mine, 2 * chip1 + core1, N_DEV, F32)
    red = _Reducer(jnp.concatenate([chip1, core1, chip1 ^ 1, chip1 ^ 2, chip1 ^ 3]))

    def carried(call, *args, extra=()):
        ops, owners = red.steps()
        out, results = call(*args, _Comm(ops + list(extra)))
        red.absorb(owners, results[:len(ops)])
        return out, results[len(ops):]

    gw_up1, ((both_all,),) = carried(_weight_grad, h_f1[None], da0_f1, extra=[_GatherRows([rows_of_all])])
    r_up1 = red.add([gw_up1])
    r_up0 = red.add([carried(_weight_grad, h_f0[None], da0_f0)[0]])
    r_dn1 = red.add([carried(_weight_grad, u_f1, dy_f1[None])[0].reshape(N_CHIPS, fq, d)])
    r_dn0 = red.add([carried(_weight_grad, u_f0, dy_f0[None])[0].reshape(N_CHIPS, fq, d)])
    r_pw1 = red.add([carried(_weight_grad, h_cv[None], da_cv)[0]])
    r_last = red.add([carried(_weight_grad, z_cv[None], dy_cv[None])[0].reshape(N_CHIPS, dq, d), gw_pool4])

    tot_both = _sum_devices(both_all)
    slab_all, tot = both_all[:, :n_slab], tot_both[:n_slab]
    tot_wide = tot_both[n_slab:].reshape(wide.shape)
    kpad = dwdw_cv.shape[0]
    o_cv, o_dw, o_f0 = 8, 24, 24 + kpad
    o_pool, o_loss = o_f0 + 8, o_f0 + 16
    loss = jnp.sum(tot[o_loss])
    dmod_l0 = jnp.concatenate([slab_all[:, o_pool + 4], slab_all[:, o_pool + 3], slab_all[:, o_pool + 1],
                               slab_all[:, o_f0 + 4], slab_all[:, o_f0 + 3], slab_all[:, o_f0 + 1]], axis=-1)
    dmod_l1 = jnp.concatenate([slab_all[:, o_cv + 4], slab_all[:, o_cv + 3], slab_all[:, o_cv + 1],
                               slab_all[:, 4], slab_all[:, 3], slab_all[:, 1]], axis=-1)
    dmod = jnp.stack([dmod_l0, dmod_l1], axis=0)
    g_ada_b = _sum_devices(dmod.transpose(1, 0, 2))
    ada_w_step = _ada_update(c_all, _shard_cols(dmod, chip, ada_w.shape[2]), ada_w, m_ada_w, v_ada_w)

    g_pre = jnp.stack([jnp.stack([tot[o_pool + 2], tot[o_f0 + 2]]), jnp.stack([tot[o_cv + 2], tot[2]])])
    g_post = jnp.stack([jnp.stack([tot[o_pool + 0], tot[o_f0 + 0]]), jnp.stack([tot[o_cv + 0], tot[0]])])
    g_pool_scale = tot[o_pool + 5][None]
    g_b_pw2, g_ln_g, g_ln_b, g_b_dw = tot[o_cv + 5], tot[o_cv + 6], tot[o_cv + 7], tot[o_cv + 8]
    g_b_pw1 = jnp.concatenate([tot[o_cv + 9], tot[o_cv + 10]])
    g_w_dw31 = tot[o_dw:o_dw + kw]
    g_ffn_dw = jnp.stack([tot_wide[8:11], tot_wide[0:3]])

    grads_small = {
        "pre_g": _shard_cols(g_pre, chip, dq), "post_g": _shard_cols(g_post, chip, dq),
        "pool_scale": g_pool_scale, "cv_b_pw1": _shard_cols(g_b_pw1[None], chip, 2 * dq),
        "cv_w_dw": _shard_cols(g_w_dw31[None], chip, dq), "cv_b_dw": _shard_cols(g_b_dw[None], chip, dq),
        "cv_ln_g": _shard_cols(g_ln_g[None], chip, dq), "cv_ln_b": _shard_cols(g_ln_b[None], chip, dq),
        "cv_b_pw2": _shard_cols(g_b_pw2[None], chip, dq), "ffn_w_dw": _shard_cols(g_ffn_dw, chip, cs),
        "ada_b": g_ada_b,
    }
    params_small = {
        "pre_g": (pre_g, m_pre_g, v_pre_g), "post_g": (post_g, m_post_g, v_post_g),
        "pool_scale": (pool_scale, m_pool_scale, v_pool_scale), "cv_b_pw1": (cv_b_pw1, m_cv_b_pw1, v_cv_b_pw1),
        "cv_w_dw": (cv_w_dw, m_cv_w_dw, v_cv_w_dw), "cv_b_dw": (cv_b_dw, m_cv_b_dw, v_cv_b_dw),
        "cv_ln_g": (cv_ln_g, m_cv_ln_g, v_cv_ln_g), "cv_ln_b": (cv_ln_b, m_cv_ln_b, v_cv_ln_b),
        "cv_b_pw2": (cv_b_pw2, m_cv_b_pw2, v_cv_b_pw2), "ffn_w_dw": (ffn_w_dw, m_ffn_w_dw, v_ffn_w_dw),
        "ada_b": (ada_b, m_ada_b, v_ada_b),
    }
    names = list(params_small)
    small_g = {nm: grads_small[nm].reshape(params_small[nm][0].shape) for nm in names}
    updated = _adamw_many([(params_small[nm][0], small_g[nm], params_small[nm][1], params_small[nm][2])
                           for nm in names])
    small_d = {nm: u[0] for nm, u in zip(names, updated)}
    small_m = {nm: u[1] for nm, u in zip(names, updated)}
    small_v = {nm: u[2] for nm, u in zip(names, updated)}

    red.drain()
    big_p = {
        "ada_w": (ada_w, m_ada_w, v_ada_w), "pool_w": (pool_w, m_pool_w, v_pool_w),
        "cv_w_pw1": (cv_w_pw1, m_cv_w_pw1, v_cv_w_pw1), "cv_w_pw2": (cv_w_pw2, m_cv_w_pw2, v_cv_w_pw2),
        "ffn_w_up": (ffn_w_up, m_ffn_w_up, v_ffn_w_up), "ffn_w_down": (ffn_w_down, m_ffn_w_down, v_ffn_w_down),
    }
    big_g, big_d, big_m, big_v = {}, {}, {}, {}

    def update(nm, grad):
        w, m, v = big_p[nm]
        as3 = lambda t: t.reshape((-1,) + w.shape[-2:])
        (dl, nm_, nv_), _ = _adamw(as3(w), as3(grad), as3(m), as3(v))
        big_g[nm] = grad.reshape(w.shape)
        big_d[nm], big_m[nm], big_v[nm] = dl.reshape(w.shape), nm_.reshape(w.shape), nv_.reshape(w.shape)

    full = lambda group, k=0: group["full"][k]
    update("ffn_w_up", jnp.stack([full(r_up0), full(r_up1)]))
    big_g["ada_w"], big_d["ada_w"], big_m["ada_w"], big_v["ada_w"] = ada_w_step
    update("ffn_w_down", jnp.stack([full(r_dn0), full(r_dn1)]))
    update("cv_w_pw1", full(r_pw1))
    update("cv_w_pw2", full(r_last, 0))
    update("pool_w", full(r_last, 1))

    order = ["ada_w", "ada_b", "pre_g", "post_g", "pool_w", "pool_scale", "cv_w_pw1", "cv_b_pw1", "cv_w_dw", "cv_b_dw",
             "cv_ln_g", "cv_ln_b", "cv_w_pw2", "cv_b_pw2", "ffn_w_up", "ffn_w_dw", "ffn_w_down"]
    pick = lambda bigs, smalls: [bigs[nm] if nm in bigs else smalls[nm] for nm in order]
    return (loss, dx0[None], *pick(big_g, small_g), *pick(big_d, small_d), *pick(big_m, small_m),
            *pick(big_v, small_v))
```

```python
import functools

import jax
import jax.numpy as jnp
from jax import lax
from jax.experimental import pallas as pl
from jax.experimental.pallas import tpu as pltpu

F32 = jnp.float32
BF16 = jnp.bfloat16
EPS = 1e-6
N_CHIPS = 4
N_DEV = 8
POOL_WINDOWS = (2, 4, 8, 16)
POOL_HALO = 16
FFN_HALO = 16
MXU_LANES = 256
CONV_ROWS, CONV_LANES = 128, 128
ADAM_LR = 0.001
ADAM_B1 = 0.9
ADAM_B2 = 0.999
ADAM_EPS = 1e-08
ADAM_WD = 0.01
ADAM_STEP = 10
V7X_VMEM_LIMIT = 58 * 1024 * 1024
MESH = pl.DeviceIdType.MESH


def _cparams(sem=None, vmem=V7X_VMEM_LIMIT):
    return pltpu.CompilerParams(dimension_semantics=sem, vmem_limit_bytes=vmem)


def _row_tile(n, want):
    if n <= want:
        return n
    t = want - want % 8
    while n % t:
        t -= 8
    return t


def _lane_chunks(width):
    out, c = [], 0
    while c < width:
        w = min(512, width - c)
        out.append((c, w))
        c += w
    return out


def _dot(a, b):
    return jnp.dot(a, b, preferred_element_type=F32)


def _dot_nt(a, b):
    return lax.dot_general(a, b, (((1,), (1,)), ((), ())), preferred_element_type=F32)


def _store_dot_nt(dst, a_ref, b_ref):
    dst[...] = _dot_nt(a_ref[...], b_ref[...])


def _store_dot_nt2(dst, a1_ref, a2_ref, b1_ref, b2_ref):
    dst[...] = _dot_nt(a1_ref[...], b1_ref[...]) + _dot_nt(a2_ref[...], b2_ref[...])


def _dot_tn(a, b):
    return lax.dot_general(a, b, (((0,), (0,)), ((), ())), preferred_element_type=F32)


def _rms(x):
    r = lax.rsqrt(jnp.mean(x * x, axis=-1, keepdims=True) + EPS)
    return x * r, r


def _rms_bwd(dyn, yn, r):
    return r * (dyn - yn * jnp.mean(dyn * yn, axis=-1, keepdims=True))


def _sigmoid(x):
    return 0.5 * jnp.tanh(0.5 * x) + 0.5


def _colsum(x):
    return jnp.sum(x, axis=0, keepdims=True)


def _shift_down(x, k):
    return x if k == 0 else pltpu.roll(x, k, 0)


def _shift_up(x, k):
    return x if k == 0 else pltpu.roll(x, x.shape[0] - k, 0)


def _vec_rows(vec):
    return vec[0:1] * vec[1:2], vec[2:3], vec[3:4], vec[4:5]


def _norm_sums(do, yn, dh, xn, vec):
    p, q = _colsum(do * yn), _colsum(dh * xn)
    return [p * vec[3:4], p * vec[4:5], q * vec[1:2], q * vec[0:1]]


def _add_rows(sum_ref, rows):
    for k, r in enumerate(rows):
        sum_ref[k:k + 1, :] += r


def _ada_forward(c, ada_w, comm=None):
    n_layers, d, ncol = ada_w.shape
    comm = comm or _Comm([])
    nc = len(comm.arrays)

    def body(*refs):
        c_ref, w_ref = refs[:2]
        cin = refs[2:2 + nc]
        call_ref, mod_ref = refs[2 + nc:4 + nc]
        cout = refs[4 + nc:4 + 2 * nc]
        part_ref, sendbuf, send_sems, recv_sems, send2, recv2 = refs[4 + 2 * nc:10 + 2 * nc]
        carried_sems = refs[10 + 2 * nc:]
        if nc:
            comm.run(0, cin, cout, *carried_sems)
        x, y, cc = lax.axis_index("x"), lax.axis_index("y"), lax.axis_index("c")
        me = 4 * x + 2 * y + cc
        rel = [(x, y, 1 - cc), (1 - x, y, cc), (x, 1 - y, cc), (1 - x, 1 - y, cc),
               (1 - x, y, 1 - cc), (x, 1 - y, 1 - cc), (1 - x, 1 - y, 1 - cc)]
        cv = c_ref[...]
        call_ref[me] = jnp.broadcast_to(cv * _sigmoid(cv), (8, d))

        def gather(k, block, to):
            blk = call_ref.at[block]
            return pltpu.make_async_remote_copy(src_ref=blk, dst_ref=blk, send_sem=send_sems.at[k],
                                                recv_sem=recv_sems.at[k], device_id=to, device_id_type=MESH)

        for k, to in enumerate(rel):
            gather(k, me, to).start()
        for k, (px, py, pc) in enumerate(rel):
            gather(k, 4 * px + 2 * py + pc, rel[k]).wait_recv()
        for k, to in enumerate(rel):
            gather(k, me, to).wait_send()

        ca = call_ref[...].reshape(8 * N_DEV, d)
        for l in range(n_layers):
            part_ref[l] = jnp.dot(ca, w_ref[l], preferred_element_type=F32, precision=lax.Precision.HIGHEST)

        j = 2 * x + y
        chips = [(1 - x, y), (x, 1 - y), (1 - x, 1 - y)]

        def rows_of(b):
            return part_ref[:, pl.ds(pl.multiple_of(8 * b, 8), 8), :]

        def scatter(k, src_j, to):
            return pltpu.make_async_remote_copy(
                src_ref=sendbuf.at[k], dst_ref=mod_ref.at[src_j], send_sem=send2.at[k], recv_sem=recv2.at[k],
                device_id=to, device_id_type=MESH)

        if nc:
            comm.run(1, cin, cout, *carried_sems)
        mod_ref[j] = rows_of(me)
        for k, (px, py) in enumerate(chips):
            sendbuf[k] = rows_of(4 * px + 2 * py + cc)
            scatter(k, j, (px, py, cc)).start()
        for k, (px, py) in enumerate(chips):
            scatter(k, 2 * px + py, (px, py, cc)).wait_recv()
        for k, (px, py) in enumerate(chips):
            scatter(k, j, (px, py, cc)).wait_send()
        if nc:
            comm.run(2, cin, cout, *carried_sems)

    vm = pl.BlockSpec(memory_space=pltpu.VMEM)
    res = pl.pallas_call(
        body, name="ada_forward",
        out_shape=(jax.ShapeDtypeStruct((N_DEV, 8, d), F32), jax.ShapeDtypeStruct((N_CHIPS, n_layers, 8, ncol), F32),
                   *comm.outs),
        in_specs=[vm, vm] + comm.specs(), out_specs=(vm, vm, *comm.specs()),
        input_output_aliases=comm.aliases(2, 2),
        scratch_shapes=[pltpu.VMEM((n_layers, 8 * N_DEV, ncol), F32), pltpu.VMEM((3, n_layers, 8, ncol), F32),
                        pltpu.SemaphoreType.DMA((7,)), pltpu.SemaphoreType.DMA((7,)),
                        pltpu.SemaphoreType.DMA((3,)), pltpu.SemaphoreType.DMA((3,))] + (comm.scratch() if nc else []),
        compiler_params=_cparams(),
    )(c, ada_w, *comm.arrays)
    return res[:2], comm.split(res[2:])


def _cast_into_slot(w2d, slot, n_slots=N_CHIPS, dtype=None):
    r, c = w2d.shape
    tr = _row_tile(r, 256)
    dtype = dtype or BF16

    def body(slot_ref, w_ref, o_ref):
        o_ref[0] = w_ref[...].astype(dtype)

    return pl.pallas_call(
        body, name="cast_into_slot",
        grid_spec=pltpu.PrefetchScalarGridSpec(
            num_scalar_prefetch=1, grid=(r // tr,),
            in_specs=[pl.BlockSpec((tr, c), lambda i, slot_ref: (i, 0))],
            out_specs=pl.BlockSpec((1, tr, c), lambda i, slot_ref: (slot_ref[0], i, 0))),
        out_shape=jax.ShapeDtypeStruct((n_slots, r, c), dtype), compiler_params=_cparams(("parallel",)),
    )(slot, w2d)


def _place():
    x, y, c = lax.axis_index("x"), lax.axis_index("y"), lax.axis_index("c")
    return x, y, c, [(1 - x, y), (x, 1 - y), (1 - x, 1 - y)]


def _remote(src, dst, send_sem, recv_sem, to):
    return pltpu.make_async_remote_copy(src_ref=src, dst_ref=dst, send_sem=send_sem, recv_sem=recv_sem,
                                        device_id=to, device_id_type=MESH)


class _AllGather:
    def __init__(self, bufs, part=(0, 1)):
        self.arrays = list(bufs)
        self.outs = [jax.ShapeDtypeStruct(b.shape, b.dtype) for b in bufs]
        self.aliased = True
        self.n_sems = 6 * len(bufs)
        self.part = part

    def run(self, phase, ins, outs, send_sems, recv_sems, base):
        x, y, c, chips = _place()
        j = 2 * x + y
        for k, buf in enumerate(outs):
            rows = buf.shape[1] // self.part[1]
            half = rows // 2

            def part(src_j, h):
                return buf.at[src_j, pl.ds(self.part[0] * rows + h * half, half), :]

            def ici(r, src_j, to):
                s = base + 6 * k + r
                return _remote(part(src_j, c), part(src_j, c), send_sems.at[s], recv_sems.at[s], to)

            def d2d(r, src_j, h):
                s = base + 6 * k + 3 + r
                return _remote(part(src_j, h), part(src_j, h), send_sems.at[s], recv_sems.at[s], (x, y, 1 - c))

            for r, (px, py) in enumerate(chips):
                if phase == 0:
                    ici(r, j, (px, py, c)).start()
                elif phase == 1:
                    ici(r, 2 * px + py, (px, py, c)).wait_recv()
                    d2d(r, 2 * px + py, c).start()
                else:
                    d2d(r, 2 * px + py, 1 - c).wait_recv()
                    ici(r, j, (px, py, c)).wait_send()
                    d2d(r, 2 * px + py, c).wait_send()


class _Swap:
    def __init__(self, grads):
        self.arrays = list(grads)
        self.outs = [jax.ShapeDtypeStruct((g.shape[0],) + g.shape[2:], g.dtype) for g in grads]
        self.aliased = False
        self.n_sems = len(grads)

    def run(self, phase, ins, outs, send_sems, recv_sems, base):
        x, y, c, _ = _place()
        for k in range(len(ins)):
            cp = _remote(ins[k].at[:, 1 - c], outs[k], send_sems.at[base + k], recv_sems.at[base + k], (x, y, 1 - c))
            if phase == 0:
                cp.start()
            elif phase == 2:
                cp.wait()


class _Exchange:
    def __init__(self, parts):
        self.arrays = list(parts)
        self.outs = [jax.ShapeDtypeStruct((3,) + p.shape[1:], p.dtype) for p in parts]
        self.aliased = False
        self.n_sems = 3 * len(parts)

    def run(self, phase, ins, outs, send_sems, recv_sems, base):
        x, y, c, chips = _place()
        for k in range(len(ins)):
            for r, (px, py) in enumerate(chips):
                s = base + 3 * k + r
                cp = _remote(ins[k].at[2 * px + py], outs[k].at[r], send_sems.at[s], recv_sems.at[s], (px, py, c))
                if phase == 0:
                    cp.start()
                elif phase == 2:
                    cp.wait()


class _Join:
    def __init__(self, bufs):
        self.arrays = list(bufs)
        self.outs = [jax.ShapeDtypeStruct(b.shape, b.dtype) for b in bufs]
        self.aliased = True
        self.n_sems = len(bufs)

    def run(self, phase, ins, outs, send_sems, recv_sems, base):
        x, y, c, _ = _place()
        for k, buf in enumerate(outs):
            mine = _remote(buf.at[c], buf.at[c], send_sems.at[base + k], recv_sems.at[base + k], (x, y, 1 - c))
            if phase == 0:
                mine.start()
            elif phase == 2:
                mine.wait_send()
                _remote(buf.at[1 - c], buf.at[1 - c], send_sems.at[base + k], recv_sems.at[base + k],
                        (x, y, 1 - c)).wait_recv()


class _Comm:
    def __init__(self, ops):
        self.ops = list(ops)
        self.arrays = [a for op in self.ops for a in op.arrays]
        self.outs = [o for op in self.ops for o in op.outs]
        self.n_sems = sum(op.n_sems for op in self.ops)

    def specs(self):
        return [pl.BlockSpec(memory_space=pl.ANY)] * len(self.arrays)

    def aliases(self, first_in, first_out):
        out, k = {}, 0
        for op in self.ops:
            for i in range(len(op.arrays)):
                if op.aliased:
                    out[first_in + k + i] = first_out + k + i
            k += len(op.arrays)
        return out

    def scratch(self):
        return [pltpu.SemaphoreType.DMA((self.n_sems,)), pltpu.SemaphoreType.DMA((self.n_sems,))]

    def run(self, phase, ins, outs, send_sems, recv_sems):
        k = base = 0
        for op in self.ops:
            n = len(op.arrays)
            op.run(phase, ins[k:k + n], outs[k:k + n], send_sems, recv_sems, base)
            k += n
            base += op.n_sems

    def split(self, results):
        out, k = [], 0
        for op in self.ops:
            out.append(list(results[k:k + len(op.arrays)]))
            k += len(op.arrays)
        return out


def _communicate(ops):
    comm = _Comm(ops)
    n = len(comm.arrays)

    def body(*refs):
        ins, outs, (send_sems, recv_sems) = refs[:n], refs[n:2 * n], refs[2 * n:]
        for phase in range(3):
            comm.run(phase, ins, outs, send_sems, recv_sems)

    res = pl.pallas_call(
        body, name="communicate", out_shape=tuple(comm.outs), in_specs=comm.specs(), out_specs=tuple(comm.specs()),
        input_output_aliases=comm.aliases(0, 0), scratch_shapes=comm.scratch(),
    )(*comm.arrays)
    return comm.split(res)


def _pool_core(he, w_ref, scale, first_row, halo, n_rows):
    d = he.shape[1]
    gd = d // len(POOL_WINDOWS)
    t = first_row + lax.broadcasted_iota(jnp.int32, (n_rows, 1), 0)
    pooled, ypre, cnts = [], [], []
    for g, w in enumerate(POOL_WINDOWS):
        hg = he[:, g * gd:(g + 1) * gd]
        s, k = hg, 1
        while k < w:
            s = s + _shift_down(s, k)
            k *= 2
        cnt = jnp.minimum(t + 1, w).astype(F32)
        p = s[halo:] / cnt - hg[halo:]
        pooled.append(p.astype(BF16))
        cnts.append(cnt)
        ypre.append(_dot(pooled[-1], w_ref[g]))
    return pooled, jnp.concatenate(ypre, axis=1), cnts


def _pool_forward(x, vec, pool_w, comm=None):
    s, d = x.shape
    ts = _row_tile(s, 512)
    nb = s // ts
    n_g, gd, _ = pool_w.shape
    comm = comm or _Comm([])
    nc = len(comm.arrays)

    def body(*refs):
        x_ref, vec_ref, w_ref = refs[:3]
        cin = refs[3:3 + nc]
        o_ref = refs[3 + nc]
        cout = refs[4 + nc:4 + 2 * nc]
        carry = refs[4 + 2 * nc]
        sems = refs[5 + 2 * nc:]
        i = pl.program_id(0)

        @pl.when(i == 0)
        def _():
            carry[...] = jnp.zeros_like(carry)
            if nc:
                comm.run(0, cin, cout, *sems)

        if nc:
            @pl.when(i == max(nb - 4, 0))
            def _():
                comm.run(1, cin, cout, *sems)

        vec = vec_ref[...]
        a, sh, gt, gpost = _vec_rows(vec)
        xb = x_ref[...]
        xn, _ = _rms(xb)
        h = xn * a + sh
        he = jnp.concatenate([carry[...], h], axis=0)
        carry[...] = h[ts - POOL_HALO:]
        _, ypre, _ = _pool_core(he, w_ref, vec[5:6], i * ts, POOL_HALO, ts)
        yn, _ = _rms(ypre * vec[5:6])
        o_ref[...] = xb + gt * (yn * gpost)
        if nc:
            @pl.when(i == nb - 1)
            def _():
                comm.run(2, cin, cout, *sems)

    res = pl.pallas_call(
        body, name="pool_forward", grid=(nb,),
        in_specs=[pl.BlockSpec((ts, d), lambda i: (i, 0)), pl.BlockSpec((8, d), lambda i: (0, 0)),
                  pl.BlockSpec((n_g, gd, gd), lambda i: (0, 0, 0))] + comm.specs(),
        out_specs=(pl.BlockSpec((ts, d), lambda i: (i, 0)), *comm.specs()),
        out_shape=(jax.ShapeDtypeStruct((s, d), F32), *comm.outs),
        input_output_aliases=comm.aliases(3, 1),
        scratch_shapes=[pltpu.VMEM((POOL_HALO, d), F32)] + (comm.scratch() if nc else []),
        compiler_params=_cparams(("arbitrary",)),
    )(x, vec, pool_w, *comm.arrays)
    return res[0], comm.split(res[1:])


def _pool_backward(dout, x, vec, pool_w):
    s, d = x.shape
    ts = _row_tile(s, 512)
    nb = s // ts
    hb = ts // POOL_HALO
    n_g, gd, _ = pool_w.shape

    def body(do_ref, x_ref, xh_ref, vec_ref, w_ref, dx_ref, sum_ref, dw_ref, carry):
        step = pl.program_id(0)
        i = nb - 1 - step

        @pl.when(step == 0)
        def _():
            carry[...] = jnp.zeros_like(carry)
            sum_ref[...] = jnp.zeros_like(sum_ref)
            dw_ref[...] = jnp.zeros_like(dw_ref)

        vec = vec_ref[...]
        a, sh, gt, gpost = _vec_rows(vec)
        scale = vec[5:6]
        do = do_ref[...]
        xe = jnp.concatenate([xh_ref[...], x_ref[...]], axis=0)
        xne, re = _rms(xe)
        he = xne * a + sh
        rowid = lax.broadcasted_iota(jnp.int32, (POOL_HALO + ts, 1), 0)
        he = jnp.where((rowid >= POOL_HALO) | (i > 0), he, 0.0)
        xn, r = xne[POOL_HALO:], re[POOL_HALO:]
        pooled, ypre, cnts = _pool_core(he, w_ref, scale, i * ts, POOL_HALO, ts)
        yn, ry = _rms(ypre * scale)
        dyn = do * (gt * gpost)
        dy = _rms_bwd(dyn, yn, ry)
        dypre = (dy * scale).astype(BF16)
        dh_parts, q_parts = [], []
        for g, w in enumerate(POOL_WINDOWS):
            dyg = dypre[:, g * gd:(g + 1) * gd]
            dpool = _dot_nt(dyg, w_ref[g])
            dw_ref[g] += _dot_tn(pooled[g], dyg)
            q = dpool / cnts[g]
            qe = jnp.concatenate([q, carry[:, g * gd:(g + 1) * gd]], axis=0)
            acc, k = qe, 1
            while k < w:
                acc = acc + _shift_up(acc, k)
                k *= 2
            dh_parts.append(acc[:ts] - dpool)
            q_parts.append(q[:POOL_HALO])
        carry[...] = jnp.concatenate(q_parts, axis=1)
        dh = jnp.concatenate(dh_parts, axis=1)
        dxn = dh * a
        dx_ref[...] = do + _rms_bwd(dxn, xn, r)
        _add_rows(sum_ref, _norm_sums(do, yn, dh, xn, vec) + [_colsum(dh), _colsum(dy * ypre)])

    blk = lambda st: (nb - 1 - st, 0)
    return pl.pallas_call(
        body, name="pool_backward", grid=(nb,),
        in_specs=[pl.BlockSpec((ts, d), blk), pl.BlockSpec((ts, d), blk),
                  pl.BlockSpec((POOL_HALO, d), lambda st: (jnp.maximum((nb - 1 - st) * hb - 1, 0), 0)),
                  pl.BlockSpec((8, d), lambda st: (0, 0)), pl.BlockSpec((n_g, gd, gd), lambda st: (0, 0, 0))],
        out_specs=(pl.BlockSpec((ts, d), blk), pl.BlockSpec((8, d), lambda st: (0, 0)),
                   pl.BlockSpec((n_g, gd, gd), lambda st: (0, 0, 0))),
        out_shape=(jax.ShapeDtypeStruct((s, d), F32), jax.ShapeDtypeStruct((8, d), F32),
                   jax.ShapeDtypeStruct((n_g, gd, gd), F32)),
        scratch_shapes=[pltpu.VMEM((POOL_HALO, d), F32)],
        compiler_params=_cparams(("arbitrary",)),
    )(dout, x, x, vec, pool_w)


def _ffn_forward(x, vec, w_up, w_dw, w_down, comm=None, target=None):
    s, d = x.shape
    _, _, cs = w_up.shape
    ts = _row_tile(s, 256)
    nb = s // ts
    chunks = _lane_chunks(cs)
    comm = comm or _Comm([])
    nc = len(comm.arrays)
    nl = 0 if target is None else 1
    n_in, n_out = 5 + nl, 6 + nl

    def body(*refs):
        x_ref, vec_ref, wup_ref, wdw_ref, wdn_ref = refs[:5]
        cin = refs[n_in:n_in + nc]
        o_ref, h_ref, a0_ref, cc_ref, u_ref, y_ref = refs[n_in + nc:n_in + nc + 6]
        loss_ref = refs[n_in + nc + 6] if nl else None
        cout = refs[n_in + nc + n_out:n_in + 2 * nc + n_out]
        carry = refs[n_in + 2 * nc + n_out]
        sems = refs[n_in + 2 * nc + n_out + 1:]
        i = pl.program_id(0)

        @pl.when(i == 0)
        def _():
            carry[...] = jnp.zeros_like(carry)
            if nl:
                loss_ref[...] = jnp.zeros_like(loss_ref)
            if nc:
                comm.run(0, cin, cout, *sems)

        if nc:
            @pl.when(i == (3 * nb) // 4)
            def _():
                comm.run(1, cin, cout, *sems)

        vec = vec_ref[...]
        a, sh, gt, gpost = _vec_rows(vec)
        xb = x_ref[...]
        xn, _ = _rms(xb)
        hb = (xn * a + sh).astype(BF16)
        h_ref[...] = hb
        for q in range(2):
            for c0, cw in chunks:
                conv = []
                for j in (q, q + 2):
                    a0 = _dot(hb, wup_ref[j, :, c0:c0 + cw])
                    a0_ref[j, :, c0:c0 + cw] = a0.astype(BF16)
                    ae = jnp.concatenate([carry[j, :, c0:c0 + cw], a0], axis=0)
                    carry[j, :, c0:c0 + cw] = a0[ts - FFN_HALO:]
                    w = wdw_ref[:, j * cs + c0:j * cs + c0 + cw]
                    conv.append((w[2:3] * ae + w[1:2] * _shift_down(ae, 1) + w[0:1] * _shift_down(ae, 2))[FFN_HALO:])
                    cc_ref[j, :, c0:c0 + cw] = conv[-1].astype(BF16)
                u_ref[q, :, c0:c0 + cw] = (conv[0] * _sigmoid(conv[0]) * conv[1]).astype(BF16)
        y = _dot(u_ref[0], wdn_ref[0]) + _dot(u_ref[1], wdn_ref[1])
        y_ref[...] = y
        yn, _ = _rms(y)
        x_out = xb + gt * (yn * gpost)
        if nl:
            err = x_out - refs[5][...]
            o_ref[...] = err * (1.0 / d)
            loss_ref[0:1, :] += _colsum(err * err) * (0.5 / d)
        else:
            o_ref[...] = x_out
        if nc:
            @pl.when(i == nb - 1)
            def _():
                comm.run(2, cin, cout, *sems)

    const3 = lambda i: (0, 0, 0)
    res = pl.pallas_call(
        body, name="ffn_forward", grid=(nb,),
        in_specs=[pl.BlockSpec((ts, d), lambda i: (i, 0)), pl.BlockSpec((8, d), lambda i: (0, 0)),
                  pl.BlockSpec(w_up.shape, const3, pipeline_mode=pl.Buffered(1)),
                  pl.BlockSpec(w_dw.shape, lambda i: (0, 0)),
                  pl.BlockSpec(w_down.shape, const3, pipeline_mode=pl.Buffered(1))]
        + [pl.BlockSpec((ts, d), lambda i: (i, 0))] * nl + comm.specs(),
        out_specs=(pl.BlockSpec((ts, d), lambda i: (i, 0)), pl.BlockSpec((ts, d), lambda i: (i, 0)),
                   pl.BlockSpec((4, ts, cs), lambda i: (0, i, 0)), pl.BlockSpec((4, ts, cs), lambda i: (0, i, 0)),
                   pl.BlockSpec((2, ts, cs), lambda i: (0, i, 0)), pl.BlockSpec((ts, d), lambda i: (i, 0)),
                   *[pl.BlockSpec((8, d), lambda i: (0, 0))] * nl, *comm.specs()),
        out_shape=(jax.ShapeDtypeStruct((s, d), F32), jax.ShapeDtypeStruct((s, d), BF16),
                   jax.ShapeDtypeStruct((4, s, cs), BF16), jax.ShapeDtypeStruct((4, s, cs), BF16),
                   jax.ShapeDtypeStruct((2, s, cs), BF16), jax.ShapeDtypeStruct((s, d), F32),
                   *[jax.ShapeDtypeStruct((8, d), F32)] * nl, *comm.outs),
        input_output_aliases=comm.aliases(n_in, n_out),
        scratch_shapes=[pltpu.VMEM((4, FFN_HALO, cs), F32)] + (comm.scratch() if nc else []),
        compiler_params=_cparams(("arbitrary",)),
    )(x, vec, w_up, w_dw, w_down, *([target] * nl), *comm.arrays)
    return res[:n_out], comm.split(res[n_out:])


def _ffn_backward(dout, x, y, a0, cc, vec, w_up, w_dw, w_down):
    s, d = x.shape
    _, _, cs = w_up.shape
    ts = _row_tile(s, 256)
    nb = s // ts
    chunks = _lane_chunks(cs)

    def body(do_ref, x_ref, y_ref, a0_ref, cc_ref, vec_ref, wup_ref, wdw_ref, wdn_ref,
             dx_ref, da0_ref, dy_ref, sum_ref, dwdw_ref, carry, du_s, dh_s):
        step = pl.program_id(0)

        @pl.when(step == 0)
        def _():
            carry[...] = jnp.zeros_like(carry)
            sum_ref[...] = jnp.zeros_like(sum_ref)
            dwdw_ref[...] = jnp.zeros_like(dwdw_ref)

        vec = vec_ref[...]
        a, sh, gt, gpost = _vec_rows(vec)
        do = do_ref[...]
        yn, ry = _rms(y_ref[...])
        dy = _rms_bwd(do * (gt * gpost), yn, ry)
        dyb = dy.astype(BF16)
        dy_ref[...] = dyb
        order = [(q, c0, cw) for q in range(2) for c0, cw in chunks]

        def du_pieces(idx):
            q, c0, cw = order[idx]
            return [functools.partial(_store_dot_nt, du_s.at[idx % 2, :, n0:min(n0 + MXU_LANES, cw)], dy_ref,
                                      wdn_ref.at[q, c0 + n0:c0 + min(n0 + MXU_LANES, cw), :])
                    for n0 in range(0, cw, MXU_LANES)]

        def dh_pieces():
            return [functools.partial(_store_dot_nt2, dh_s.at[:, n0:n0 + MXU_LANES], da0_ref.at[0], da0_ref.at[2],
                                      wup_ref.at[0, n0:n0 + MXU_LANES, :], wup_ref.at[2, n0:n0 + MXU_LANES, :])
                    for n0 in range(0, d, MXU_LANES)]

        for piece in du_pieces(0):
            piece()
        later = dh_pieces()
        for idx, (q, c0, cw) in enumerate(order):
            work = du_pieces(idx + 1) if idx + 1 < len(order) else []
            if q == 1:
                share = -(-len(later) // (len(order) - idx))
                work, later = work + later[:share], later[share:]

            def pump(part, of=3):
                for piece in work[part::of]:
                    piece()

            cg = cc_ref[q, :, c0:c0 + cw].astype(F32)
            cv = cc_ref[q + 2, :, c0:c0 + cw].astype(F32)
            sg = _sigmoid(cg)
            sl = cg * sg
            du = du_s[idx % 2, :, :cw]
            dconv = {q: du * cv * (sg * (1.0 + cg * (1.0 - sg))), q + 2: du * sl}
            pump(0)
            for part, j in enumerate((q, q + 2)):
                dae = jnp.concatenate([dconv[j], carry[j, :, c0:c0 + cw]], axis=0)
                carry[j, :, c0:c0 + cw] = dconv[j][:FFN_HALO]
                up1 = _shift_down(dae, FFN_HALO - 1)[FFN_HALO:]
                up2 = _shift_down(dae, FFN_HALO - 2)[FFN_HALO:]
                lanes = slice(j * cs + c0, j * cs + c0 + cw)
                w = wdw_ref[:, lanes]
                da0_ref[j, :, c0:c0 + cw] = (w[2:3] * dconv[j] + w[1:2] * up1 + w[0:1] * up2).astype(BF16)
                a0 = a0_ref[j, :, c0:c0 + cw].astype(F32)
                dwdw_ref[0:1, lanes] += _colsum(up2 * a0)
                dwdw_ref[1:2, lanes] += _colsum(up1 * a0)
                dwdw_ref[2:3, lanes] += _colsum(dconv[j] * a0)
                pump(part + 1)
        dh = dh_s[...] + _dot_nt(da0_ref[1], wup_ref[1]) + _dot_nt(da0_ref[3], wup_ref[3])
        xn, r = _rms(x_ref[...])
        dx_ref[...] = do + _rms_bwd(dh * a, xn, r)
        _add_rows(sum_ref, _norm_sums(do, yn, dh, xn, vec) + [_colsum(dh)])

    blk = lambda st: (nb - 1 - st, 0)
    blk3 = lambda st: (0, nb - 1 - st, 0)
    const3 = lambda st: (0, 0, 0)
    return pl.pallas_call(
        body, name="ffn_backward", grid=(nb,),
        in_specs=[pl.BlockSpec((ts, d), blk), pl.BlockSpec((ts, d), blk), pl.BlockSpec((ts, d), blk),
                  pl.BlockSpec((4, ts, cs), blk3), pl.BlockSpec((4, ts, cs), blk3),
                  pl.BlockSpec((8, d), lambda st: (0, 0)),
                  pl.BlockSpec(w_up.shape, const3, pipeline_mode=pl.Buffered(1)),
                  pl.BlockSpec(w_dw.shape, lambda st: (0, 0)),
                  pl.BlockSpec(w_down.shape, const3, pipeline_mode=pl.Buffered(1))],
        out_specs=(pl.BlockSpec((ts, d), blk), pl.BlockSpec((4, ts, cs), blk3),
                   pl.BlockSpec((ts, d), blk), pl.BlockSpec((8, d), lambda st: (0, 0)),
                   pl.BlockSpec((8, 4 * cs), lambda st: (0, 0))),
        out_shape=(jax.ShapeDtypeStruct((s, d), F32), jax.ShapeDtypeStruct((4, s, cs), BF16),
                   jax.ShapeDtypeStruct((s, d), BF16),
                   jax.ShapeDtypeStruct((8, d), F32), jax.ShapeDtypeStruct((8, 4 * cs), F32)),
        scratch_shapes=[pltpu.VMEM((4, FFN_HALO, cs), F32), pltpu.VMEM((2, ts, max(cw for _, cw in chunks)), F32),
                        pltpu.VMEM((ts, d), F32)],
        compiler_params=_cparams(("arbitrary",)),
    )(dout, x, y, a0, cc, vec, w_up, w_dw, w_down)


def _conv_halo(width):
    return -(-(width - 1) // 8) * 8


def _conv_forward(x, vec, cvec, w_pw1, b_pw1, w_dw, w_pw2):
    s, d = x.shape
    kw = w_dw.shape[0]
    halo = _conv_halo(kw)
    ts = _row_tile(s, 512)
    hd = d // 2

    def body(x_ref, vec_ref, cvec_ref, w1_ref, b1_ref, wdw_ref, w2_ref,
             o_ref, h_ref, a_ref, uc_ref, z_ref, y_ref, carry):
        i = pl.program_id(0)

        @pl.when(i == 0)
        def _():
            carry[...] = jnp.zeros_like(carry)

        vec, cvec = vec_ref[...], cvec_ref[...]
        a, sh, gt, gpost = _vec_rows(vec)
        xb = x_ref[...]
        xn, _ = _rms(xb)
        hb = (xn * a + sh).astype(BF16)
        h_ref[...] = hb
        for j in range(4):
            a_ref[:, j * hd:(j + 1) * hd] = _dot(hb, w1_ref[j]) + b1_ref[:, j * hd:(j + 1) * hd]
        u = a_ref[:, :d] * _sigmoid(a_ref[:, d:])
        carry[halo:, :] = u
        for r0 in range(0, ts, CONV_ROWS):
            for l0 in range(0, d, CONV_LANES):
                lanes = slice(l0, l0 + CONV_LANES)
                src = carry[r0:r0 + CONV_ROWS + halo, lanes]
                acc = jnp.zeros((CONV_ROWS, CONV_LANES), F32) + cvec[0:1, lanes]
                for k in range(kw):
                    acc = acc + wdw_ref[k:k + 1, lanes] * _shift_down(src, kw - 1 - k)[halo:]
                uc_ref[r0:r0 + CONV_ROWS, lanes] = acc
        carry[:halo, :] = u[ts - halo:]
        uc = uc_ref[...]
        mu = jnp.mean(uc, axis=-1, keepdims=True)
        cen = uc - mu
        rstd = lax.rsqrt(jnp.mean(cen * cen, axis=-1, keepdims=True) + EPS)
        l = cen * rstd * cvec[1:2] + cvec[2:3]
        zb = (l * _sigmoid(l)).astype(BF16)
        z_ref[...] = zb
        y = _dot(zb, w2_ref[...]) + cvec[3:4]
        y_ref[...] = y
        yn, _ = _rms(y)
        o_ref[...] = xb + gt * (yn * gpost)

    row = lambda i: (i, 0)
    const2 = lambda i: (0, 0)
    return pl.pallas_call(
        body, name="conv_forward", grid=(s // ts,),
        in_specs=[pl.BlockSpec((ts, d), row), pl.BlockSpec((8, d), const2), pl.BlockSpec((8, d), const2),
                  pl.BlockSpec(w_pw1.shape, lambda i: (0, 0, 0)), pl.BlockSpec(b_pw1.shape, const2),
                  pl.BlockSpec(w_dw.shape, const2), pl.BlockSpec(w_pw2.shape, const2)],
        out_specs=(pl.BlockSpec((ts, d), row), pl.BlockSpec((ts, d), row), pl.BlockSpec((ts, 2 * d), row),
                   pl.BlockSpec((ts, d), row), pl.BlockSpec((ts, d), row), pl.BlockSpec((ts, d), row)),
        out_shape=(jax.ShapeDtypeStruct((s, d), F32), jax.ShapeDtypeStruct((s, d), BF16),
                   jax.ShapeDtypeStruct((s, 2 * d), F32), jax.ShapeDtypeStruct((s, d), F32),
                   jax.ShapeDtypeStruct((s, d), BF16), jax.ShapeDtypeStruct((s, d), F32)),
        scratch_shapes=[pltpu.VMEM((halo + ts, d), F32)],
        compiler_params=_cparams(("arbitrary",)),
    )(x, vec, cvec, w_pw1, b_pw1, w_dw, w_pw2)


def _conv_backward(dout, x, y, a_pre, uc, vec, cvec, w_pw1, w_dw, w_pw2):
    s, d = x.shape
    kw = w_dw.shape[0]
    kpad = -(-kw // 8) * 8
    halo = _conv_halo(kw)
    ts = _row_tile(s, 512)
    nb = s // ts
    hb = ts // halo
    hd = d // 2

    def body(do_ref, x_ref, y_ref, a_ref, ah_ref, uc_ref, vec_ref, cvec_ref, w1_ref, wdw_ref, w2_ref,
             dx_ref, da_ref, dy_ref, sum_ref, dwdw_ref, carry):
        step = pl.program_id(0)
        i = nb - 1 - step

        @pl.when(step == 0)
        def _():
            carry[...] = jnp.zeros_like(carry)
            sum_ref[...] = jnp.zeros_like(sum_ref)
            dwdw_ref[...] = jnp.zeros_like(dwdw_ref)

        vec, cvec = vec_ref[...], cvec_ref[...]
        a, sh, gt, gpost = _vec_rows(vec)
        do = do_ref[...]
        yn, ry = _rms(y_ref[...])
        dy = _rms_bwd(do * (gt * gpost), yn, ry)
        dyb = dy.astype(BF16)
        dy_ref[...] = dyb
        dz = _dot_nt(dyb, w2_ref[...])
        uc = uc_ref[...]
        mu = jnp.mean(uc, axis=-1, keepdims=True)
        cen = uc - mu
        rstd = lax.rsqrt(jnp.mean(cen * cen, axis=-1, keepdims=True) + EPS)
        lhat = cen * rstd
        l = lhat * cvec[1:2] + cvec[2:3]
        sgl = _sigmoid(l)
        dl = dz * (sgl * (1.0 + l * (1.0 - sgl)))
        dlhat = dl * cvec[1:2]
        duc = rstd * (dlhat - jnp.mean(dlhat, axis=-1, keepdims=True)
                      - lhat * jnp.mean(dlhat * lhat, axis=-1, keepdims=True))
        ae = jnp.concatenate([ah_ref[...] * (i > 0).astype(F32), a_ref[...]], axis=0)
        sgate = _sigmoid(ae[:, d:])
        val = ae[:, :d]
        ue = val * sgate
        rowid = lax.broadcasted_iota(jnp.int32, (halo + ts, 1), 0)
        ue = jnp.where((rowid >= halo) | (i > 0), ue, 0.0)
        duce = jnp.concatenate([duc, carry[...]], axis=0)
        carry[...] = duc[:halo]
        du = jnp.zeros((ts, d), F32)
        for k in range(kw):
            du = du + wdw_ref[k:k + 1, :] * _shift_down(duce, halo - (kw - 1 - k))[halo:]
            dwdw_ref[k:k + 1, :] += _colsum(duc * _shift_down(ue, kw - 1 - k)[halo:])
        sg, vl = sgate[halo:], val[halo:]
        dval = du * sg
        dgate = du * vl * (sg * (1.0 - sg))
        dvb, dgb = dval.astype(BF16), dgate.astype(BF16)
        dh = jnp.zeros((ts, d), F32)
        for j in range(2):
            da_ref[j] = dvb[:, j * hd:(j + 1) * hd]
            da_ref[j + 2] = dgb[:, j * hd:(j + 1) * hd]
            dh = dh + _dot_nt(dvb[:, j * hd:(j + 1) * hd], w1_ref[j]) + _dot_nt(dgb[:, j * hd:(j + 1) * hd], w1_ref[j + 2])
        xn, r = _rms(x_ref[...])
        dx_ref[...] = do + _rms_bwd(dh * a, xn, r)
        _add_rows(sum_ref, _norm_sums(do, yn, dh, xn, vec) + [_colsum(dh), _colsum(dy), _colsum(dl * lhat), _colsum(dl),
                            _colsum(duc), _colsum(dval), _colsum(dgate)])

    blk = lambda st: (nb - 1 - st, 0)
    const2 = lambda st: (0, 0)
    return pl.pallas_call(
        body, name="conv_backward", grid=(nb,),
        in_specs=[pl.BlockSpec((ts, d), blk), pl.BlockSpec((ts, d), blk), pl.BlockSpec((ts, d), blk),
                  pl.BlockSpec((ts, 2 * d), blk),
                  pl.BlockSpec((halo, 2 * d), lambda st: (jnp.maximum((nb - 1 - st) * hb - 1, 0), 0)),
                  pl.BlockSpec((ts, d), blk), pl.BlockSpec((8, d), const2), pl.BlockSpec((8, d), const2),
                  pl.BlockSpec(w_pw1.shape, lambda st: (0, 0, 0)), pl.BlockSpec(w_dw.shape, const2),
                  pl.BlockSpec(w_pw2.shape, const2)],
        out_specs=(pl.BlockSpec((ts, d), blk), pl.BlockSpec((4, ts, hd), lambda st: (0, nb - 1 - st, 0)),
                   pl.BlockSpec((ts, d), blk), pl.BlockSpec((16, d), const2), pl.BlockSpec((kpad, d), const2)),
        out_shape=(jax.ShapeDtypeStruct((s, d), F32), jax.ShapeDtypeStruct((4, s, hd), BF16),
                   jax.ShapeDtypeStruct((s, d), BF16), jax.ShapeDtypeStruct((16, d), F32),
                   jax.ShapeDtypeStruct((kpad, d), F32)),
        scratch_shapes=[pltpu.VMEM((halo, d), F32)],
        compiler_params=_cparams(("arbitrary",)),
    )(dout, x, y, a_pre, a_pre, uc, vec, cvec, w_pw1, w_dw, w_pw2)


def _weight_grad(a, b, comm=None):
    na, s, k = a.shape
    nb_, _, n = b.shape
    nj = max(na, nb_)
    ts = _row_tile(s, 2048)
    nt = s // ts
    comm = comm or _Comm([])
    nc = len(comm.arrays)

    def body(*refs):
        a_ref, b_ref = refs[:2]
        cin = refs[2:2 + nc]
        o_ref = refs[2 + nc]
        cout = refs[3 + nc:3 + 2 * nc]
        sems = refs[3 + 2 * nc:]
        j, t = pl.program_id(0), pl.program_id(1)

        if nc:
            @pl.when((j == 0) & (t == 0))
            def _():
                comm.run(0, cin, cout, *sems)

            @pl.when((j == nj // 2) & (t == nt // 2))
            def _():
                comm.run(1, cin, cout, *sems)

        @pl.when(t == 0)
        def _():
            o_ref[...] = jnp.zeros_like(o_ref)

        o_ref[0] += _dot_tn(a_ref[0], b_ref[0])

        if nc:
            @pl.when((j == nj - 1) & (t == nt - 1))
            def _():
                comm.run(2, cin, cout, *sems)

    res = pl.pallas_call(
        body, name="weight_grad", grid=(nj, nt),
        in_specs=[pl.BlockSpec((1, ts, k), (lambda j, t: (j, t, 0)) if na > 1 else (lambda j, t: (0, t, 0))),
                  pl.BlockSpec((1, ts, n), (lambda j, t: (j, t, 0)) if nb_ > 1 else (lambda j, t: (0, t, 0)))]
        + comm.specs(),
        out_specs=(pl.BlockSpec((1, k, n), lambda j, t: (j, 0, 0)), *comm.specs()),
        out_shape=(jax.ShapeDtypeStruct((nj, k, n), F32), *comm.outs),
        input_output_aliases=comm.aliases(2, 1),
        scratch_shapes=comm.scratch() if nc else [],
        compiler_params=_cparams(("arbitrary", "arbitrary") if nc else ("parallel", "arbitrary")),
    )(a, b, *comm.arrays)
    return res[0], comm.split(res[1:])


def _adamw_math(w, g, m, v):
    nm = ADAM_B1 * m + (1.0 - ADAM_B1) * g
    nv = ADAM_B2 * v + (1.0 - ADAM_B2) * (g * g)
    m_hat = nm * (1.0 / (1.0 - ADAM_B1 ** ADAM_STEP))
    v_hat = nv * (1.0 / (1.0 - ADAM_B2 ** ADAM_STEP))
    return -ADAM_LR * (m_hat / (jnp.sqrt(v_hat) + ADAM_EPS) + ADAM_WD * w), nm, nv


def _adamw_many(params):
    n = len(params)

    def body(*refs):
        for k in range(n):
            w_ref, g_ref, m_ref, v_ref = refs[4 * k:4 * k + 4]
            outs = refs[4 * n + 3 * k:4 * n + 3 * k + 3]
            for o_ref, val in zip(outs, _adamw_math(w_ref[...], g_ref[...], m_ref[...], v_ref[...])):
                o_ref[...] = val

    vm = pl.BlockSpec(memory_space=pltpu.VMEM)
    res = pl.pallas_call(
        body, name="adamw_many", in_specs=[vm] * (4 * n), out_specs=tuple([vm] * (3 * n)),
        out_shape=tuple(jax.ShapeDtypeStruct(p[0].shape, F32) for p in params for _ in range(3)),
        compiler_params=_cparams(),
    )(*[a for p in params for a in p])
    return [res[3 * k:3 * k + 3] for k in range(n)]


def _adamw(w, g, m, v, comm=None):
    nl, r, c = w.shape
    tr = _row_tile(r, 256)
    nr = r // tr
    comm = comm or _Comm([])
    nc = len(comm.arrays)

    def body(*refs):
        w_ref, g_ref, m_ref, v_ref = refs[:4]
        cin = refs[4:4 + nc]
        d_ref, nm_ref, nv_ref = refs[4 + nc:7 + nc]
        cout = refs[7 + nc:7 + 2 * nc]
        sems = refs[7 + 2 * nc:]
        l, i = pl.program_id(0), pl.program_id(1)
        if nc:
            @pl.when((l == 0) & (i == 0))
            def _():
                comm.run(0, cin, cout, *sems)

            @pl.when((l == nl // 2) & (i == nr // 2))
            def _():
                comm.run(1, cin, cout, *sems)

        d_ref[...], nm_ref[...], nv_ref[...] = _adamw_math(w_ref[...], g_ref[...], m_ref[...], v_ref[...])
        if nc:
            @pl.when((l == nl - 1) & (i == nr - 1))
            def _():
                comm.run(2, cin, cout, *sems)

    spec = pl.BlockSpec((1, tr, c), lambda l, i: (l, i, 0))
    shp = jax.ShapeDtypeStruct((nl, r, c), F32)
    res = pl.pallas_call(
        body, name="adamw", grid=(nl, nr), in_specs=[spec] * 4 + comm.specs(),
        out_specs=(spec,) * 3 + tuple(comm.specs()), out_shape=(shp,) * 3 + tuple(comm.outs),
        input_output_aliases=comm.aliases(4, 3), scratch_shapes=comm.scratch() if nc else [],
        compiler_params=_cparams(("arbitrary", "arbitrary") if nc else ("parallel", "parallel")),
    )(w, g, m, v, *comm.arrays)
    return res[:3], comm.split(res[3:])


def _add_my_half(g, other, idx):
    _, _, h, c = g.shape
    th = _row_tile(h, 256)

    def body(idx_ref, g_ref, o_ref, out_ref):
        out_ref[...] = (g_ref[:, 0] + o_ref[...]).astype(BF16)

    return pl.pallas_call(
        body, name="add_my_half",
        grid_spec=pltpu.PrefetchScalarGridSpec(
            num_scalar_prefetch=1, grid=(3, h // th),
            in_specs=[pl.BlockSpec((1, 1, th, c), lambda j, i, idx_ref: (idx_ref[2 + j], idx_ref[1], i, 0)),
                      pl.BlockSpec((1, th, c), lambda j, i, idx_ref: (idx_ref[2 + j], i, 0))],
            out_specs=pl.BlockSpec((1, th, c), lambda j, i, idx_ref: (idx_ref[2 + j], i, 0))),
        out_shape=jax.ShapeDtypeStruct(other.shape, BF16),
        compiler_params=_cparams(("parallel", "parallel")),
    )(idx, g, other)


def _sum_for_my_chip(g, other, got, idx):
    _, _, h, c = g.shape
    th = _row_tile(h, 256)

    def body(idx_ref, g_ref, o_ref, q_ref, out_ref):
        out_ref[0] = (((g_ref[0, 0] + o_ref[0]) + q_ref[0].astype(F32)) + q_ref[1].astype(F32)) + q_ref[2].astype(F32)

    return pl.pallas_call(
        body, name="sum_for_my_chip",
        grid_spec=pltpu.PrefetchScalarGridSpec(
            num_scalar_prefetch=1, grid=(h // th,),
            in_specs=[pl.BlockSpec((1, 1, th, c), lambda i, idx_ref: (idx_ref[0], idx_ref[1], i, 0)),
                      pl.BlockSpec((1, th, c), lambda i, idx_ref: (idx_ref[0], i, 0)),
                      pl.BlockSpec((3, th, c), lambda i, idx_ref: (0, i, 0))],
            out_specs=pl.BlockSpec((1, th, c), lambda i, idx_ref: (idx_ref[1], i, 0))),
        out_shape=jax.ShapeDtypeStruct((2, h, c), F32),
        compiler_params=_cparams(("parallel",)),
    )(idx, g, other, got)


class _Reducer:
    def __init__(self, idx):
        self.idx = idx
        self.groups = []

    def add(self, grads):
        group = {"state": 0, "g": [g.reshape(4, 2, g.shape[1] // 2, g.shape[2]) for g in grads]}
        self.groups.append(group)
        return group

    def steps(self):
        ops, owners = [], []
        for gr in self.groups:
            if gr["state"] == 0:
                ops.append(_Swap(gr["g"]))
            elif gr["state"] == 1:
                ops.append(_Exchange(gr["parts"]))
            elif gr["state"] == 2:
                ops.append(_Join(gr["bufs"]))
            else:
                continue
            owners.append(gr)
        return ops, owners

    def absorb(self, owners, results):
        for gr, res in zip(owners, results):
            if gr["state"] == 0:
                gr["other"] = res
                gr["parts"] = [_add_my_half(g, o, self.idx) for g, o in zip(gr["g"], res)]
            elif gr["state"] == 1:
                gr["bufs"] = [_sum_for_my_chip(g, o, q, self.idx) for g, o, q in zip(gr["g"], gr["other"], res)]
            else:
                gr["full"] = [b.reshape(2 * b.shape[1], b.shape[2]) for b in res]
            gr["state"] += 1

    def drain(self):
        while any(gr["state"] < 3 for gr in self.groups):
            ops, owners = self.steps()
            self.absorb(owners, _communicate(ops))


class _GatherRows:
    def __init__(self, bufs):
        self.arrays = list(bufs)
        self.outs = [jax.ShapeDtypeStruct(b.shape, b.dtype) for b in bufs]
        self.aliased = True
        self.n_sems = 7 * len(bufs)

    def run(self, phase, ins, outs, send_sems, recv_sems, base):
        x, y, c, chips = _place()
        me, sibling = (x, y, c), (x, y, 1 - c)
        for k, buf in enumerate(outs):
            def copy(i, block_of, to):
                blk = buf.at[4 * block_of[0] + 2 * block_of[1] + block_of[2]]
                return _remote(blk, blk, send_sems.at[base + 7 * k + i], recv_sems.at[base + 7 * k + i], to)

            if phase == 0:
                copy(0, me, sibling).start()
            for r, (px, py) in enumerate(chips):
                if phase == 0:
                    copy(1 + r, me, (px, py, c)).start()
                elif phase == 1:
                    copy(1 + r, (px, py, c), me).wait_recv()
                    copy(4 + r, (px, py, c), sibling).start()
                else:
                    copy(4 + r, (px, py, 1 - c), me).wait_recv()
                    copy(1 + r, me, (px, py, c)).wait_send()
                    copy(4 + r, (px, py, c), sibling).wait_send()
            if phase == 2:
                copy(0, sibling, me).wait_recv()
                copy(0, me, sibling).wait_send()


def _sum_devices(gathered):
    nd, m, n = gathered.shape

    def body(g_ref, o_ref):
        acc = g_ref[0]
        for b in range(1, nd):
            acc = acc + g_ref[b]
        o_ref[...] = acc

    return pl.pallas_call(
        body, name="sum_devices", out_shape=jax.ShapeDtypeStruct((m, n), F32),
        in_specs=[pl.BlockSpec(memory_space=pltpu.VMEM)], out_specs=pl.BlockSpec(memory_space=pltpu.VMEM),
        compiler_params=_cparams(),
    )(gathered)


def _ada_update(c_all, dmod_cols, w, m, v):
    nl, nd, ncol = dmod_cols.shape
    d = c_all.shape[1]
    tr = _row_tile(d, 256)

    def body(c_ref, dm_ref, w_ref, m_ref, v_ref, g_ref, d_ref, nm_ref, nv_ref):
        g = lax.dot_general(c_ref[...], dm_ref[0], (((0,), (0,)), ((), ())),
                            preferred_element_type=F32, precision=lax.Precision.HIGHEST)
        g_ref[0] = g
        d_ref[0], nm_ref[0], nv_ref[0] = _adamw_math(w_ref[0], g, m_ref[0], v_ref[0])

    spec = pl.BlockSpec((1, tr, ncol), lambda l, i: (l, i, 0))
    shp = jax.ShapeDtypeStruct((nl, d, ncol), F32)
    return pl.pallas_call(
        body, name="ada_update", grid=(nl, d // tr),
        in_specs=[pl.BlockSpec((nd, tr), lambda l, i: (0, i)), pl.BlockSpec((1, nd, ncol), lambda l, i: (l, 0, 0)),
                  spec, spec, spec],
        out_specs=(spec,) * 4, out_shape=(shp,) * 4, compiler_params=_cparams(("parallel", "parallel")),
    )(c_all, dmod_cols, w, m, v)


def _pad_rows(a, rows):
    return jnp.pad(a, ((0, rows - a.shape[0]), (0, 0)))


def _shard_cols(full, chip, width):
    return lax.dynamic_slice_in_dim(full, chip * width, width, axis=full.ndim - 1)


def kernel(x, c, ada_w, ada_b, pre_g, post_g, pool_w, pool_scale, cv_w_pw1, cv_b_pw1, cv_w_dw, cv_b_dw, cv_ln_g, cv_ln_b, cv_w_pw2, cv_b_pw2, ffn_w_up, ffn_w_dw, ffn_w_down, loss_target, m_ada_w, m_ada_b, m_pre_g, m_post_g, m_pool_w, m_pool_scale, m_cv_w_pw1, m_cv_b_pw1, m_cv_w_dw, m_cv_b_dw, m_cv_ln_g, m_cv_ln_b, m_cv_w_pw2, m_cv_b_pw2, m_ffn_w_up, m_ffn_w_dw, m_ffn_w_down, v_ada_w, v_ada_b, v_pre_g, v_post_g, v_pool_w, v_pool_scale, v_cv_w_pw1, v_cv_b_pw1, v_cv_w_dw, v_cv_b_dw, v_cv_ln_g, v_cv_ln_b, v_cv_w_pw2, v_cv_b_pw2, v_ffn_w_up, v_ffn_w_dw, v_ffn_w_down):
    s, d = x.shape[1], x.shape[2]
    dq = d // N_CHIPS
    n_g = pool_w.shape[1]
    gq = pool_w.shape[2]
    gd = pool_w.shape[3]
    kw = cv_w_dw.shape[1]
    cs = ffn_w_up.shape[2]
    fq = ffn_w_down.shape[1]
    chip = 2 * lax.axis_index("x") + lax.axis_index("y")
    core = lax.axis_index("c")
    chip1 = jnp.reshape(chip, (1,)).astype(jnp.int32)
    core1 = jnp.reshape(core, (1,)).astype(jnp.int32)
    xs, tgt = x[0], loss_target[0]

    small_rows = [pre_g.reshape(4, dq), post_g.reshape(4, dq), cv_w_dw[0], cv_b_dw, cv_ln_g, cv_ln_b, cv_b_pw2,
                  cv_b_pw1.reshape(2, dq)]
    small = jnp.concatenate(small_rows, axis=0)
    n_small = small.shape[0]
    small = _pad_rows(small, -(-n_small // 16) * 16)
    dwf = _pad_rows(ffn_w_dw.reshape(6, cs), 16)
    first = _AllGather([_cast_into_slot(pool_w.reshape(n_g * gq, gd), chip1), _cast_into_slot(small, chip1, dtype=F32),
                        _cast_into_slot(dwf, chip1, dtype=F32)])
    upper = _AllGather([_cast_into_slot(ffn_w_up[0], chip1)], part=(0, 2))
    (c_rep, mod_rep), ((g_pool, g_small, g_dwf), (g_up0,)) = _ada_forward(c, ada_w, _Comm([first, upper]))
    c_all = c_rep[:, 0, :]
    mod = mod_rep[:, :, 0, :].transpose(1, 0, 2).reshape(ada_b.shape) + ada_b
    lower = _AllGather([g_up0], part=(1, 2))
    second = _AllGather([_cast_into_slot(ffn_w_down[0], chip1)])
    later = _AllGather([_cast_into_slot(cv_w_pw1[0], chip1), _cast_into_slot(cv_w_pw2[0], chip1),
                        _cast_into_slot(ffn_w_up[1], chip1), _cast_into_slot(ffn_w_down[1], chip1)])
    poolw_full = g_pool.reshape(N_CHIPS, n_g, gq, gd).transpose(1, 0, 2, 3).reshape(n_g, gd, gd)
    smallf = g_small.transpose(1, 0, 2).reshape(g_small.shape[1], d)
    pre_full, post_full = smallf[0:4].reshape(2, 2, d), smallf[4:8].reshape(2, 2, d)
    wdw31 = smallf[8:8 + kw]
    o = 8 + kw
    b_dw, ln_g, ln_b, b_pw2 = smallf[o:o + 1], smallf[o + 1:o + 2], smallf[o + 2:o + 3], smallf[o + 3:o + 4]
    b_pw1 = g_small[:, o + 4:o + 6, :].reshape(1, 2 * d)
    ffn_dw = g_dwf[:, :6, :].transpose(1, 0, 2).reshape(2, 3, N_CHIPS * cs)

    def sub_vec(layer, sub, extra=None):
        m6 = mod[layer].reshape(6, d)
        rows = [pre_full[layer, sub][None], 1.0 + m6[3 * sub + 1][None], m6[3 * sub][None], m6[3 * sub + 2][None],
                post_full[layer, sub][None]]
        if extra is not None:
            rows.append(extra)
        return _pad_rows(jnp.concatenate(rows, axis=0), 8)

    vec_pool = sub_vec(0, 0, pool_scale)
    vec_f0, vec_conv, vec_f1 = sub_vec(0, 1), sub_vec(1, 0), sub_vec(1, 1)
    cvec = _pad_rows(jnp.concatenate([b_dw, ln_g, ln_b, b_pw2], axis=0), 8)

    x1, ((g_up0,), (g_dn0,)) = _pool_forward(xs, vec_pool, poolw_full, _Comm([lower, second]))
    w_up0, w_dn0 = g_up0, g_dn0.reshape(2, 2 * fq, d)
    (x2, h_f0, a0_f0, cc_f0, u_f0, y_f0), ((g_pw1, g_pw2, g_up1, g_dn1),) = _ffn_forward(
        x1, vec_f0, w_up0, ffn_dw[0], w_dn0, _Comm([later]))
    pw2_full = g_pw2.reshape(d, d)
    w_up1, w_dn1 = g_up1, g_dn1.reshape(2, 2 * fq, d)
    x3, h_cv, a_cv, uc_cv, z_cv, y_cv = _conv_forward(x2, vec_conv, cvec, g_pw1, b_pw1, wdw31, pw2_full)
    (dx4, h_f1, a0_f1, cc_f1, u_f1, y_f1, loss_rows), _ = _ffn_forward(x3, vec_f1, w_up1, ffn_dw[1], w_dn1, target=tgt)

    dx3, da0_f1, dy_f1, sum_f1, dwdw_f1 = _ffn_backward(dx4, x3, y_f1, a0_f1, cc_f1, vec_f1, w_up1, ffn_dw[1], w_dn1)
    dx2, da_cv, dy_cv, sum_cv, dwdw_cv = _conv_backward(dx3, x2, y_cv, a_cv, uc_cv, vec_conv, cvec, g_pw1, wdw31, pw2_full)
    dx1, da0_f0, dy_f0, sum_f0, dwdw_f0 = _ffn_backward(dx2, x1, y_f0, a0_f0, cc_f0, vec_f0, w_up0, ffn_dw[0], w_dn0)
    dx0, sum_pool, gw_pool = _pool_backward(dx1, xs, vec_pool, poolw_full)
    gw_pool4 = gw_pool.reshape(n_g, N_CHIPS, gq, gd).transpose(1, 0, 2, 3).reshape(N_CHIPS, n_g * gq, gd)

    slab = jnp.concatenate([sum_f1, sum_cv, dwdw_cv, sum_f0, sum_pool, loss_rows], axis=0)
    wide = jnp.concatenate([dwdw_f1, dwdw_f0], axis=0)
    n_slab = slab.shape[0]
    mine = jnp.concatenate([slab, wide.reshape(-1, d)], axis=0)
    rows_of_all = _cast_into_slot(mine, 2 * chip1 + core1, N_DEV, F32)
    red = _Reducer(jnp.concatenate([chip1, core1, chip1 ^ 1, chip1 ^ 2, chip1 ^ 3]))

    def carried(call, *args, extra=()):
        ops, owners = red.steps()
        out, results = call(*args, _Comm(ops + list(extra)))
        red.absorb(owners, results[:len(ops)])
        return out, results[len(ops):]

    gw_up1, ((both_all,),) = carried(_weight_grad, h_f1[None], da0_f1, extra=[_GatherRows([rows_of_all])])
    r_up1 = red.add([gw_up1])
    r_up0 = red.add([carried(_weight_grad, h_f0[None], da0_f0)[0]])
    r_dn1 = red.add([carried(_weight_grad, u_f1, dy_f1[None])[0].reshape(N_CHIPS, fq, d)])
    r_dn0 = red.add([carried(_weight_grad, u_f0, dy_f0[None])[0].reshape(N_CHIPS, fq, d)])
    r_pw1 = red.add([carried(_weight_grad, h_cv[None], da_cv)[0]])
    r_last = red.add([carried(_weight_grad, z_cv[None], dy_cv[None])[0].reshape(N_CHIPS, dq, d), gw_pool4])

    tot_both = _sum_devices(both_all)
    slab_all, tot = both_all[:, :n_slab], tot_both[:n_slab]
    tot_wide = tot_both[n_slab:].reshape(wide.shape)
    kpad = dwdw_cv.shape[0]
    o_cv, o_dw, o_f0 = 8, 24, 24 + kpad
    o_pool, o_loss = o_f0 + 8, o_f0 + 16
    loss = jnp.sum(tot[o_loss])
    dmod_l0 = jnp.concatenate([slab_all[:, o_pool + 4], slab_all[:, o_pool + 3], slab_all[:, o_pool + 1],
                               slab_all[:, o_f0 + 4], slab_all[:, o_f0 + 3], slab_all[:, o_f0 + 1]], axis=-1)
    dmod_l1 = jnp.concatenate([slab_all[:, o_cv + 4], slab_all[:, o_cv + 3], slab_all[:, o_cv + 1],
                               slab_all[:, 4], slab_all[:, 3], slab_all[:, 1]], axis=-1)
    dmod = jnp.stack([dmod_l0, dmod_l1], axis=0)
    g_ada_b = _sum_devices(dmod.transpose(1, 0, 2))
    ada_w_step = _ada_update(c_all, _shard_cols(dmod, chip, ada_w.shape[2]), ada_w, m_ada_w, v_ada_w)

    g_pre = jnp.stack([jnp.stack([tot[o_pool + 2], tot[o_f0 + 2]]), jnp.stack([tot[o_cv + 2], tot[2]])])
    g_post = jnp.stack([jnp.stack([tot[o_pool + 0], tot[o_f0 + 0]]), jnp.stack([tot[o_cv + 0], tot[0]])])
    g_pool_scale = tot[o_pool + 5][None]
    g_b_pw2, g_ln_g, g_ln_b, g_b_dw = tot[o_cv + 5], tot[o_cv + 6], tot[o_cv + 7], tot[o_cv + 8]
    g_b_pw1 = jnp.concatenate([tot[o_cv + 9], tot[o_cv + 10]])
    g_w_dw31 = tot[o_dw:o_dw + kw]
    g_ffn_dw = jnp.stack([tot_wide[8:11], tot_wide[0:3]])

    grads_small = {
        "pre_g": _shard_cols(g_pre, chip, dq), "post_g": _shard_cols(g_post, chip, dq),
        "pool_scale": g_pool_scale, "cv_b_pw1": _shard_cols(g_b_pw1[None], chip, 2 * dq),
        "cv_w_dw": _shard_cols(g_w_dw31[None], chip, dq), "cv_b_dw": _shard_cols(g_b_dw[None], chip, dq),
        "cv_ln_g": _shard_cols(g_ln_g[None], chip, dq), "cv_ln_b": _shard_cols(g_ln_b[None], chip, dq),
        "cv_b_pw2": _shard_cols(g_b_pw2[None], chip, dq), "ffn_w_dw": _shard_cols(g_ffn_dw, chip, cs),
        "ada_b": g_ada_b,
    }
    params_small = {
        "pre_g": (pre_g, m_pre_g, v_pre_g), "post_g": (post_g, m_post_g, v_post_g),
        "pool_scale": (pool_scale, m_pool_scale, v_pool_scale), "cv_b_pw1": (cv_b_pw1, m_cv_b_pw1, v_cv_b_pw1),
        "cv_w_dw": (cv_w_dw, m_cv_w_dw, v_cv_w_dw), "cv_b_dw": (cv_b_dw, m_cv_b_dw, v_cv_b_dw),
        "cv_ln_g": (cv_ln_g, m_cv_ln_g, v_cv_ln_g), "cv_ln_b": (cv_ln_b, m_cv_ln_b, v_cv_ln_b),
        "cv_b_pw2": (cv_b_pw2, m_cv_b_pw2, v_cv_b_pw2), "ffn_w_dw": (ffn_w_dw, m_ffn_w_dw, v_ffn_w_dw),
        "ada_b": (ada_b, m_ada_b, v_ada_b),
    }
    names = list(params_small)
    small_g = {nm: grads_small[nm].reshape(params_small[nm][0].shape) for nm in names}
    updated = _adamw_many([(params_small[nm][0], small_g[nm], params_small[nm][1], params_small[nm][2])
                           for nm in names])
    small_d = {nm: u[0] for nm, u in zip(names, updated)}
    small_m = {nm: u[1] for nm, u in zip(names, updated)}
    small_v = {nm: u[2] for nm, u in zip(names, updated)}

    red.drain()
    big_p = {
        "ada_w": (ada_w, m_ada_w, v_ada_w), "pool_w": (pool_w, m_pool_w, v_pool_w),
        "cv_w_pw1": (cv_w_pw1, m_cv_w_pw1, v_cv_w_pw1), "cv_w_pw2": (cv_w_pw2, m_cv_w_pw2, v_cv_w_pw2),
        "ffn_w_up": (ffn_w_up, m_ffn_w_up, v_ffn_w_up), "ffn_w_down": (ffn_w_down, m_ffn_w_down, v_ffn_w_down),
    }
    big_g, big_d, big_m, big_v = {}, {}, {}, {}

    def update(nm, grad):
        w, m, v = big_p[nm]
        as3 = lambda t: t.reshape((-1,) + w.shape[-2:])
        (dl, nm_, nv_), _ = _adamw(as3(w), as3(grad), as3(m), as3(v))
        big_g[nm] = grad.reshape(w.shape)
        big_d[nm], big_m[nm], big_v[nm] = dl.reshape(w.shape), nm_.reshape(w.shape), nv_.reshape(w.shape)

    full = lambda group, k=0: group["full"][k]
    update("ffn_w_up", jnp.stack([full(r_up0), full(r_up1)]))
    big_g["ada_w"], big_d["ada_w"], big_m["ada_w"], big_v["ada_w"] = ada_w_step
    update("ffn_w_down", jnp.stack([full(r_dn0), full(r_dn1)]))
    update("cv_w_pw1", full(r_pw1))
    update("cv_w_pw2", full(r_last, 0))
    update("pool_w", full(r_last, 1))

    order = ["ada_w", "ada_b", "pre_g", "post_g", "pool_w", "pool_scale", "cv_w_pw1", "cv_b_pw1", "cv_w_dw", "cv_b_dw",
             "cv_ln_g", "cv_ln_b", "cv_w_pw2", "cv_b_pw2", "ffn_w_up", "ffn_w_dw", "ffn_w_down"]
    pick = lambda bigs, smalls: [bigs[nm] if nm in bigs else smalls[nm] for nm in order]
    return (loss, dx0[None], *pick(big_g, small_g), *pick(big_d, small_d), *pick(big_m, small_m),
            *pick(big_v, small_v))
```
